```python
import math
import jax, jax.numpy as jnp
from jax import lax
import numpy as np

D_MODEL = 1024
BATCH = 16
SEQ = 256
DEPTH = 2
DEC_BATCH = 2
DEC_SEQ = 2048
PAST_LEN = 512

GRID_W = 64
MIX_W = D_MODEL
HG_W = MIX_W // 2
HG_HEADS = 4
HG_DK = HG_W // HG_HEADS
HG_DV = HG_W // HG_HEADS
HG_CHUNK = 16
S5_W = MIX_W - HG_W
S5_CH = 16
S5_GROUPS = S5_W // S5_CH
S5_P = 64
N_DIR = 2
IN_W = 5 * HG_W + S5_W
D_FF = -(-8 * D_MODEL // (3 * 256)) * 256
EPS = 1e-6

kernel_name = "hymba_hgrn2_s5_prefix_dit_step"


def rmsnorm(x, gain):
    xf = x.astype(jnp.float32)
    y = xf * lax.rsqrt(jnp.mean(xf * xf, axis=-1, keepdims=True) + EPS)
    return (y * gain.astype(jnp.float32)).astype(x.dtype)


def grid_pos_embed(n_tokens, dim):
    rows = n_tokens // GRID_W
    t = jnp.arange(rows * GRID_W)
    r = (t // GRID_W).astype(jnp.float32)
    col = (t % GRID_W).astype(jnp.float32)
    nf = dim // 4
    omega = 1.0 / (10000.0 ** (jnp.arange(nf, dtype=jnp.float32) / nf))
    def enc(p):
        a = p[:, None] * omega[None, :]
        return jnp.concatenate([jnp.sin(a), jnp.cos(a)], axis=-1)
    return jnp.concatenate([enc(r), enc(col)], axis=-1)


def hgrn2_scan(q, k, logf, v, s0):
    B, L = q.shape[0], q.shape[1]
    N = L // HG_CHUNK
    def chunk(a):
        return a.reshape(B, N, HG_CHUNK, HG_HEADS, a.shape[-1]).transpose(0, 3, 1, 2, 4)
    q, k, logf, v = chunk(q), chunk(k), chunk(logf), chunk(v)
    b = jnp.cumsum(logf, axis=3)
    b_last = b[:, :, :, -1:, :]
    causal = jnp.tril(jnp.ones((HG_CHUNK, HG_CHUNK), dtype=bool))
    diff = b[:, :, :, :, None, :] - b[:, :, :, None, :, :]
    decay = jnp.exp(jnp.where(causal[:, :, None], diff, -jnp.inf))
    scores = jnp.einsum('bhnik,bhnjk,bhnijk->bhnij', q, k, decay)
    o_intra = jnp.einsum('bhnij,bhnjv->bhniv', scores, v)
    k_to_end = k * jnp.exp(b_last - b)
    u = jnp.einsum('bhnjk,bhnjv->bhnkv', k_to_end, v)
    g = jnp.exp(b_last[:, :, :, 0, :])
    def step(s, xs):
        g_n, u_n = xs
        return g_n[..., None] * s + u_n, s
    s_final, s_enter = lax.scan(step, s0, (jnp.moveaxis(g, 2, 0), jnp.moveaxis(u, 2, 0)))
    s_enter = jnp.moveaxis(s_enter, 0, 2)
    o_inter = jnp.einsum('bhnik,bhnkv->bhniv', q * jnp.exp(b), s_enter)
    o = (o_intra + o_inter).transpose(0, 2, 3, 1, 4).reshape(B, L, HG_HEADS, HG_DV)
    return o, s_final


def hgrn2_mixer(q, f_fwd, f_bwd, i, g, lb, norm_gain, s0):
    f32 = jnp.float32
    B, L = q.shape[0], q.shape[1]
    def split(a):
        return a.reshape(B, L, HG_HEADS, -1).astype(f32)
    qh = jax.nn.silu(split(q)) * (HG_DK ** -0.5)
    vh = split(i)
    o_sum = None
    finals = []
    for d, f in enumerate((f_fwd, f_bwd)):
        lb_d = lb[d].astype(f32).reshape(HG_HEADS, HG_DK)
        fl = split(f)
        forget = lb_d + (1.0 - lb_d) * jax.nn.sigmoid(fl)
        logf = jnp.log(forget)
        key_in = (1.0 - lb_d) * jax.nn.sigmoid(-fl)
        args = (qh, key_in, logf, vh)
        if d == 1:
            args = tuple(jnp.flip(a, axis=1) for a in args)
        o, sf = hgrn2_scan(*args, s0[:, d].astype(f32))
        if d == 1:
            o = jnp.flip(o, axis=1)
        o_sum = o if o_sum is None else o_sum + o
        finals.append(sf)
    o = o_sum * lax.rsqrt(jnp.mean(o_sum * o_sum, axis=-1, keepdims=True) + EPS) * norm_gain.astype(f32)
    o = o * jax.nn.silu(split(g))
    return o.reshape(B, L, HG_W), jnp.stack(finals, axis=1)


def s5_combine(e1, e2):
    a1r, a1i, b1r, b1i = e1
    a2r, a2i, b2r, b2i = e2
    return (a2r * a1r - a2i * a1i, a2r * a1i + a2i * a1r,
            a2r * b1r - a2i * b1i + b2r, a2r * b1i + a2i * b1r + b2i)


def s5_mixer(u, lam_re, lam_im, log_dt, b_re, b_im, c_re, c_im, d_skip, w_glu, h0):
    f32 = jnp.float32
    B, L = u.shape[0], u.shape[1]
    uf = u.astype(f32).reshape(B, L, S5_GROUPS, S5_CH)
    y = d_skip.astype(f32) * uf
    finals = []
    for d in range(N_DIR):
        lr = jnp.minimum(lam_re[d].astype(f32), -1e-4)
        li = lam_im[d].astype(f32)
        dt = jnp.exp(log_dt[d].astype(f32))[:, None]
        mag = jnp.exp(lr * dt)
        ab_re, ab_im = mag * jnp.cos(li * dt), mag * jnp.sin(li * dt)
        nr, ni = ab_re - 1.0, ab_im
        den = lr * lr + li * li
        z_re, z_im = (nr * lr + ni * li) / den, (ni * lr - nr * li) / den
        br, bi = b_re[d].astype(f32), b_im[d].astype(f32)
        bb_re = z_re[..., None] * br - z_im[..., None] * bi
        bb_im = z_re[..., None] * bi + z_im[..., None] * br
        x_in = jnp.flip(uf, axis=1) if d == 1 else uf
        bu_re = jnp.einsum('gpc,blgc->blgp', bb_re, x_in)
        bu_im = jnp.einsum('gpc,blgc->blgp', bb_im, x_in)
        h0r = h0[:, d, :, :, 0].astype(f32)
        h0i = h0[:, d, :, :, 1].astype(f32)
        bu_re = bu_re.at[:, 0].add(ab_re * h0r - ab_im * h0i)
        bu_im = bu_im.at[:, 0].add(ab_re * h0i + ab_im * h0r)
        a_re = jnp.broadcast_to(ab_re, bu_re.shape)
        a_im = jnp.broadcast_to(ab_im, bu_im.shape)
        _, _, hr, hi = lax.associative_scan(s5_combine, (a_re, a_im, bu_re, bu_im), axis=1)
        yd = jnp.einsum('gcp,blgp->blgc', c_re[d].astype(f32), hr) - jnp.einsum('gcp,blgp->blgc', c_im[d].astype(f32), hi)
        if d == 1:
            yd = jnp.flip(yd, axis=1)
        y = y + yd
        finals.append(jnp.stack([hr[:, -1], hi[:, -1]], axis=-1))
    y = jax.nn.gelu(y.reshape(B, L, S5_W))
    y = y * jax.nn.sigmoid(y @ w_glu.astype(f32))
    return y, jnp.stack(finals, axis=1)


def setup_inputs(seed: int = 0) -> dict:
    key = jax.random.key(seed)
    ks = iter(jax.random.split(key, 40))
    f32 = jnp.float32
    def nrm(shape, scale):
        return jax.random.normal(next(ks), shape, f32) * scale
    inp = {}
    inp["x_prompt"] = nrm((BATCH, SEQ, D_MODEL), 1.0)
    inp["x_sample"] = nrm((DEC_BATCH, DEC_SEQ, D_MODEL), 1.0)
    inp["state_hgrn"] = nrm((DEC_BATCH, DEPTH, N_DIR, HG_HEADS, HG_DK, HG_DV), 0.5)
    inp["state_s5"] = nrm((DEC_BATCH, DEPTH, N_DIR, S5_GROUPS, S5_P, 2), 0.1)
    inp["c"] = nrm((DEC_BATCH, D_MODEL), 1.0)
    inp["c_ctx"] = nrm((D_MODEL,), 1.0)
    inp["w_mod"] = nrm((DEPTH, D_MODEL, 6 * D_MODEL), 0.5 * D_MODEL ** -0.5)
    inp["b_mod"] = nrm((DEPTH, 6 * D_MODEL), 0.02)
    inp["norm_mix"] = 1.0 + nrm((DEPTH, D_MODEL), 0.02)
    inp["norm_ffn"] = 1.0 + nrm((DEPTH, D_MODEL), 0.02)
    inp["norm_final"] = 1.0 + nrm((D_MODEL,), 0.02)
    inp["w_in"] = nrm((DEPTH, D_MODEL, IN_W), D_MODEL ** -0.5)
    inp["w_out"] = nrm((DEPTH, MIX_W, D_MODEL), MIX_W ** -0.5)
    inp["hg_lb_logits"] = nrm((N_DIR, DEPTH, HG_W), 1.0)
    inp["hg_norm"] = 1.0 + nrm((DEPTH, HG_DV), 0.02)
    inp["s5_lam_re"] = -0.5 + nrm((DEPTH, N_DIR, S5_GROUPS, S5_P), 0.01)
    inp["s5_lam_im"] = math.pi * jnp.arange(S5_P, dtype=f32) + nrm((DEPTH, N_DIR, S5_GROUPS, S5_P), 0.01)
    inp["s5_log_dt"] = jax.random.uniform(next(ks), (DEPTH, N_DIR, S5_GROUPS), f32, math.log(1e-3), math.log(1e-1))
    inp["s5_b_re"] = nrm((DEPTH, N_DIR, S5_GROUPS, S5_P, S5_CH), S5_CH ** -0.5)
    inp["s5_b_im"] = nrm((DEPTH, N_DIR, S5_GROUPS, S5_P, S5_CH), S5_CH ** -0.5)
    inp["s5_c_re"] = nrm((DEPTH, N_DIR, S5_GROUPS, S5_CH, S5_P), S5_P ** -0.5)
    inp["s5_c_im"] = nrm((DEPTH, N_DIR, S5_GROUPS, S5_CH, S5_P), S5_P ** -0.5)
    inp["s5_d"] = nrm((DEPTH, S5_GROUPS, S5_CH), 1.0)
    inp["s5_w_glu"] = nrm((DEPTH, S5_W, S5_W), S5_W ** -0.5)
    inp["w_gate"] = nrm((DEPTH, D_MODEL, D_FF), D_MODEL ** -0.5)
    inp["w_up"] = nrm((DEPTH, D_MODEL, D_FF), D_MODEL ** -0.5)
    inp["w_down"] = nrm((DEPTH, D_FF, D_MODEL), D_FF ** -0.5)
    return inp


def reference(x_prompt, x_sample, state_hgrn, state_s5, c, c_ctx, w_mod, b_mod, norm_mix, norm_ffn,
              norm_final, w_in, w_out, hg_lb_logits, hg_norm, s5_lam_re, s5_lam_im, s5_log_dt,
              s5_b_re, s5_b_im, s5_c_re, s5_c_im, s5_d, s5_w_glu, w_gate, w_up, w_down):
    f32 = jnp.float32
    lb_soft = jax.nn.softmax(hg_lb_logits.astype(f32), axis=1)
    lower_bounds = jnp.cumsum(lb_soft, axis=1) - lb_soft[:, :1]

    def layer(x, cond, s_hg0, s_s50, l):
        mod = jax.nn.silu(cond) @ w_mod[l] + b_mod[l]
        sh1, sc1, g1, sh2, sc2, g2 = jnp.split(mod[:, None, :], 6, axis=-1)
        h = rmsnorm(x, norm_mix[l]) * (1.0 + sc1) + sh1
        proj = h @ w_in[l]
        q, ff, fb, iv, og, u = jnp.split(proj, [HG_W, 2 * HG_W, 3 * HG_W, 4 * HG_W, 5 * HG_W], axis=-1)
        o_hg, s_hg = hgrn2_mixer(q, ff, fb, iv, og, lower_bounds[:, l], hg_norm[l], s_hg0)
        o_s5, s_s5 = s5_mixer(u, s5_lam_re[l], s5_lam_im[l], s5_log_dt[l], s5_b_re[l], s5_b_im[l],
                              s5_c_re[l], s5_c_im[l], s5_d[l], s5_w_glu[l], s_s50)
        mix = jnp.concatenate([o_hg, o_s5], axis=-1).astype(x.dtype) @ w_out[l]
        x = x + g1 * mix
        h = rmsnorm(x, norm_ffn[l]) * (1.0 + sc2) + sh2
        x = x + g2 * ((jax.nn.silu(h @ w_gate[l]) * (h @ w_up[l])) @ w_down[l])
        return x, s_hg, s_s5

    bp = x_prompt.shape[0]
    zero_hg = jnp.zeros((bp, N_DIR, HG_HEADS, HG_DK, HG_DV), f32)
    zero_s5 = jnp.zeros((bp, N_DIR, S5_GROUPS, S5_P, 2), f32)
    xc = x_prompt
    hg_states, s5_states = [], []
    for l in range(DEPTH):
        xc, s_hg, s_s5 = layer(xc, c_ctx[None, :], zero_hg, zero_s5, l)
        hg_states.append(s_hg)
        s5_states.append(s_s5)
    y_prompt = rmsnorm(xc, norm_final)
    new_state_hgrn = jnp.stack(hg_states, axis=1)
    new_state_s5 = jnp.stack(s5_states, axis=1)

    xs = x_sample + grid_pos_embed(x_sample.shape[1], D_MODEL).astype(x_sample.dtype)
    for l in range(DEPTH):
        xs, _, _ = layer(xs, c, state_hgrn[:, l], state_s5[:, l], l)
    y_sample = rmsnorm(xs, norm_final)
    return (y_prompt, y_sample, new_state_hgrn, new_state_s5)
```

```python
import functools
import math

import jax
import jax.numpy as jnp
from jax import lax
from jax.experimental import pallas as pl
from jax.experimental.pallas import tpu as pltpu

F32 = jnp.float32
BF16 = jnp.bfloat16

LANES = 128
SUBLANES = 8

D_MODEL = 1024
DEPTH = 2
GRID_W = 64
HG_W = 512
HG_HEADS = 4
HG_D = HG_W // HG_HEADS
S5_W = 512
S5_CH = 16
S5_GROUPS = S5_W // S5_CH
S5_P = 64
S5_GB = LANES // S5_CH
S5_NGB = S5_GROUPS // S5_GB
S5_SW = S5_GB * S5_P
IN_W = 5 * HG_W + S5_W
D_FF = 2816
EPS = 1e-6

HG_CHUNK = 128
HG_LEVELS = (64, 32, 16, 8, 4, 2, 1)
S5_SEQ = 256
S5_NSEQ = 16
S5_TC = 32

ROW_TILE_IN = 512
ROW_TILE_OUT = 256
MOD_TILE_N = 1536
VMEM_LIMIT = 56 * 1024 * 1024


def _sigmoid(x):
    return 1.0 / (1.0 + jnp.exp(-x))


def _silu(x):
    return x * _sigmoid(x)


def _gelu_tanh(x):
    return 0.5 * x * (1.0 + jnp.tanh(math.sqrt(2.0 / math.pi) * (x + 0.044715 * (x * x * x))))


def _rms(x):
    return x * lax.rsqrt(jnp.mean(x * x, axis=-1, keepdims=True) + EPS)


def _dot(a, b):
    return jnp.dot(a, b, preferred_element_type=F32)


def _dot_nt(a, b):
    return lax.dot_general(a, b, (((1,), (1,)), ((), ())), preferred_element_type=F32)


def _dot_tn(a, b):
    return lax.dot_general(a, b, (((0,), (0,)), ((), ())), preferred_element_type=F32)


def _const_spec(shape):
    nd = len(shape)
    return pl.BlockSpec(shape, lambda *_: (0,) * nd, pipeline_mode=pl.Buffered(1))


def _mod_kernel(cond_ref, w_ref, b_ref, o_ref):
    a = _silu(cond_ref[...]).astype(BF16)
    o_ref[0] = _dot(a, w_ref[0].astype(BF16)) + b_ref[0]


def _mod_call(cond, w_mod, b_mod):
    n_cond = cond.shape[0]
    n_out = w_mod.shape[-1]
    return pl.pallas_call(
        _mod_kernel,
        grid=(DEPTH, n_out // MOD_TILE_N),
        in_specs=[
            pl.BlockSpec((n_cond, D_MODEL), lambda l, j: (0, 0)),
            pl.BlockSpec((1, D_MODEL, MOD_TILE_N), lambda l, j: (l, 0, j)),
            pl.BlockSpec((1, 1, MOD_TILE_N), lambda l, j: (l, 0, j)),
        ],
        out_specs=pl.BlockSpec((1, n_cond, MOD_TILE_N), lambda l, j: (l, 0, j)),
        out_shape=jax.ShapeDtypeStruct((DEPTH, n_cond, n_out), F32),
        compiler_params=pltpu.CompilerParams(dimension_semantics=("parallel", "parallel"),
                                             vmem_limit_bytes=VMEM_LIMIT),
        name="adaln_mod",
    )(cond, w_mod, b_mod.reshape(DEPTH, 1, n_out))


def _in_kernel(*refs, add_pos):
    if add_pos:
        x_ref, pos_ref, gain_ref, sh_ref, sc_ref, w_ref, proj_ref, xs_ref = refs
        x = x_ref[...] + pos_ref[...]
        xs_ref[...] = x
    else:
        x_ref, gain_ref, sh_ref, sc_ref, w_ref, proj_ref = refs
        x = x_ref[...]
    h = _rms(x) * gain_ref[...]
    h = h * (1.0 + sc_ref[...]) + sh_ref[...]
    proj_ref[...] = _dot(h.astype(BF16), w_ref[...])


def _mod_spec(layer, cond0, rows_per_cond, tile, col):
    return pl.BlockSpec((None, None, 1, D_MODEL),
                        lambda i: (layer, cond0 + (i * tile) // rows_per_cond, 0, col))


def _in_call(x, pos, gain, mod4, w_in_b, layer, cond0, rows_per_cond):
    rows = x.shape[0]
    tm = ROW_TILE_IN
    add_pos = pos is not None
    row_spec = pl.BlockSpec((tm, D_MODEL), lambda i: (i, 0))
    in_specs = [row_spec]
    args = [x]
    if add_pos:
        pos_tiles = pos.shape[0] // tm
        in_specs.append(pl.BlockSpec((tm, D_MODEL), lambda i: (i % pos_tiles, 0)))
        args.append(pos)
    in_specs += [
        pl.BlockSpec((1, D_MODEL), lambda i: (0, 0)),
        _mod_spec(layer, cond0, rows_per_cond, tm, 0),
        _mod_spec(layer, cond0, rows_per_cond, tm, 1),
        _const_spec((D_MODEL, IN_W)),
    ]
    args += [gain, mod4, mod4, w_in_b]
    out_specs = [pl.BlockSpec((tm, IN_W), lambda i: (i, 0))]
    out_shape = [jax.ShapeDtypeStruct((rows, IN_W), F32)]
    if add_pos:
        out_specs.append(row_spec)
        out_shape.append(jax.ShapeDtypeStruct((rows, D_MODEL), F32))
    res = pl.pallas_call(
        functools.partial(_in_kernel, add_pos=add_pos),
        grid=(rows // tm,),
        in_specs=in_specs,
        out_specs=out_specs,
        out_shape=out_shape,
        compiler_params=pltpu.CompilerParams(dimension_semantics=("parallel",),
                                             vmem_limit_bytes=VMEM_LIMIT),
        name="in_proj",
    )(*args)
    return (res[0], res[1]) if add_pos else (res[0], x)


def _pair_boundary(b, m, rev):
    c = b.shape[0]
    span = 2 * m
    at = m if rev else m - 1
    if span >= SUBLANES:
        b3 = b.reshape(c // span, span, LANES)
        return jnp.broadcast_to(b3[:, at:at + 1, :], b3.shape).reshape(c, LANES)
    b3 = b.reshape(c // SUBLANES, SUBLANES, LANES)
    sub = lax.broadcasted_iota(jnp.int32, b3.shape, 1)
    out = None
    for p in range(SUBLANES // span):
        piece = jnp.broadcast_to(b3[:, p * span + at:p * span + at + 1, :], b3.shape)
        out = piece if out is None else jnp.where(sub >= p * span, piece, out)
    return out.reshape(c, LANES)


def _hg_chunk(qh, fl, v, lb, st_ref, tri, sx, eye, rev):
    c = qh.shape[0]
    sig = _sigmoid(fl)
    forget = lb + (1.0 - lb) * sig
    logf = jnp.log(forget)
    key = (1.0 - lb) * (1.0 - sig)
    hi = logf.astype(BF16)
    r1 = logf - hi.astype(F32)
    mid = r1.astype(BF16)
    lo = (r1 - mid.astype(F32)).astype(BF16)
    parts = _dot(tri, jnp.concatenate([hi, mid, lo], axis=1))
    b = parts[:, :LANES] + parts[:, LANES:2 * LANES] + parts[:, 2 * LANES:]
    b_edge = b[0:1, :] if rev else b[c - 1:c, :]

    st = st_ref[...]
    vb = v.astype(BF16)
    o = _dot_nt((qh * jnp.exp(b)).astype(BF16), st.astype(BF16))
    k_end = key * jnp.exp(b_edge - b)
    st_ref[...] = jnp.exp(b_edge) * st + _dot_tn(vb, k_end.astype(BF16))

    scores = jnp.where(eye, jnp.sum(qh * key, axis=-1, keepdims=True), 0.0)
    for m in HG_LEVELS:
        e = jnp.exp(-jnp.abs(b - _pair_boundary(b, m, rev)))
        p = _dot_nt((qh * e).astype(BF16), (key * e).astype(BF16))
        scores = jnp.where((sx >> int(math.log2(m))) == 1, p, scores)
    return o + _dot(scores.astype(BF16), vb)


def _hgrn_kernel(*refs, layer, n_chunks, zero_init):
    if zero_init:
        q_ref, ff_ref, fb_ref, v_ref, g_ref, lbl_ref, gain_ref, o_ref, sfin_ref, st_ref = refs
        s0_ref = None
    else:
        q_ref, ff_ref, fb_ref, v_ref, g_ref, lbl_ref, gain_ref, s0_ref, o_ref, sfin_ref, st_ref = refs
    c = HG_CHUNK
    t = lax.broadcasted_iota(jnp.int32, (c, c), 0)
    s = lax.broadcasted_iota(jnp.int32, (c, c), 1)
    eye = t == s
    x = t ^ s

    def lower_bound(d):
        lg = lbl_ref[d]
        ex = jnp.exp(lg - jnp.max(lg, axis=0, keepdims=True))
        soft = ex / jnp.sum(ex, axis=0, keepdims=True)
        return jnp.sum(soft[:layer + 1], axis=0, keepdims=True) - soft[0:1]

    scale = HG_D ** -0.5

    def rows_of(n):
        return pl.ds(pl.multiple_of(n * c, c), c)

    lb = lower_bound(0)
    tri = (s <= t).astype(BF16)
    sx = jnp.where(t > s, x, 0)
    st_ref[...] = jnp.zeros((HG_D, HG_D), F32) if zero_init else s0_ref[0].T

    def fwd_body(n, carry):
        rows = rows_of(n)
        qh = _silu(q_ref[rows, :]) * scale
        o_ref[rows, :] = _hg_chunk(qh, ff_ref[rows, :], v_ref[rows, :], lb, st_ref, tri, sx, eye, False)
        return carry

    lax.fori_loop(0, n_chunks, fwd_body, 0)
    sfin_ref[0] = st_ref[...].T

    lb = lower_bound(1)
    tri = (s >= t).astype(BF16)
    sx = jnp.where(t < s, x, 0)
    st_ref[...] = jnp.zeros((HG_D, HG_D), F32) if zero_init else s0_ref[1].T

    def bwd_body(i, carry):
        rows = rows_of(n_chunks - 1 - i)
        qh = _silu(q_ref[rows, :]) * scale
        o = o_ref[rows, :] + _hg_chunk(qh, fb_ref[rows, :], v_ref[rows, :], lb, st_ref, tri, sx, eye, True)
        o_ref[rows, :] = _rms(o) * gain_ref[...] * _silu(g_ref[rows, :])
        return carry

    lax.fori_loop(0, n_chunks, bwd_body, 0)
    sfin_ref[1] = st_ref[...].T


def _hgrn_call(proj, lb_logits, gain, state, layer, n_seq, seq_len):
    rows = proj.shape[0]
    zero_init = state is None

    def col_spec(k):
        return pl.BlockSpec((seq_len, HG_D), lambda b, h: (b, k * HG_HEADS + h))

    in_specs = [col_spec(0), col_spec(1), col_spec(2), col_spec(3), col_spec(4),
                pl.BlockSpec((2, DEPTH, HG_D), lambda b, h: (0, 0, h)),
                pl.BlockSpec((1, HG_D), lambda b, h: (0, 0))]
    args = [proj] * 5 + [lb_logits, gain]
    if not zero_init:
        in_specs.append(pl.BlockSpec((None, None, 2, None, HG_D, HG_D), lambda b, h: (b, layer, 0, h, 0, 0)))
        args.append(state)
    return pl.pallas_call(
        functools.partial(_hgrn_kernel, layer=layer, n_chunks=seq_len // HG_CHUNK, zero_init=zero_init),
        grid=(n_seq, HG_HEADS),
        in_specs=in_specs,
        out_specs=[pl.BlockSpec((seq_len, HG_D), lambda b, h: (b, h)),
                   pl.BlockSpec((None, 2, None, HG_D, HG_D), lambda b, h: (b, 0, h, 0, 0))],
        out_shape=[jax.ShapeDtypeStruct((rows, HG_W), F32),
                   jax.ShapeDtypeStruct((n_seq, 2, HG_HEADS, HG_D, HG_D), F32)],
        scratch_shapes=[pltpu.VMEM((HG_D, HG_D), F32)],
        compiler_params=pltpu.CompilerParams(dimension_semantics=("parallel", "parallel"),
                                             vmem_limit_bytes=VMEM_LIMIT),
        name="hgrn2_mixer",
    )(*args)


def _s5_disc_kernel(lr_ref, li_ref, ldt_ref, ar_ref, ai_ref, zr_ref, zi_ref):
    lr = jnp.minimum(lr_ref[...], -1e-4)
    li = li_ref[...]
    dt = jnp.exp(ldt_ref[...])
    mag = jnp.exp(lr * dt)
    ab_re = mag * jnp.cos(li * dt)
    ab_im = mag * jnp.sin(li * dt)
    nr = ab_re - 1.0
    den = lr * lr + li * li
    ar_ref[...] = ab_re
    ai_ref[...] = ab_im
    zr_ref[...] = (nr * lr + ab_im * li) / den
    zi_ref[...] = (ab_im * lr - nr * li) / den


def _s5_bbar_kernel(zr_ref, zi_ref, br_ref, bi_ref, or_ref, oi_ref):
    zr, zi = zr_ref[...], zi_ref[...]
    br, bi = br_ref[...], bi_ref[...]
    or_ref[...] = zr * br - zi * bi
    oi_ref[...] = zr * bi + zi * br


def _s5_params(lam_re, lam_im, log_dt, b_re, b_im, c_re, c_im):
    n = DEPTH * 2 * S5_GROUPS
    shp = jax.ShapeDtypeStruct((n, S5_P), F32)
    ab_re, ab_im, z_re, z_im = pl.pallas_call(
        _s5_disc_kernel, out_shape=[shp] * 4, name="s5_discretise",
    )(lam_re.reshape(n, S5_P), lam_im.reshape(n, S5_P),
      jnp.broadcast_to(log_dt.reshape(n, 1), (n, S5_P)))
    wide = jax.ShapeDtypeStruct((n, S5_P * S5_CH), F32)
    bb_re, bb_im = pl.pallas_call(
        _s5_bbar_kernel, out_shape=[wide] * 2, name="s5_bbar",
    )(jnp.repeat(z_re, S5_CH, axis=1), jnp.repeat(z_im, S5_CH, axis=1),
      b_re.reshape(n, S5_P * S5_CH), b_im.reshape(n, S5_P * S5_CH))

    eye = jnp.eye(S5_GB, dtype=F32)
    lead = (DEPTH, 2, S5_NGB)

    def b_blockdiag(bb):
        bb = bb.reshape(lead + (S5_GB, S5_P, S5_CH))
        return jnp.einsum('ldbgpc,gh->ldbgchp', bb, eye).reshape(lead + (LANES, S5_SW))

    def c_blockdiag(cc):
        cc = cc.reshape(lead + (S5_GB, S5_CH, S5_P))
        return jnp.einsum('ldbgcp,gh->ldbhpgc', cc, eye).reshape(lead + (S5_SW, LANES))

    bmat = jnp.concatenate([b_blockdiag(bb_re), b_blockdiag(bb_im)], axis=-1).astype(BF16)
    cmat = jnp.concatenate([c_blockdiag(c_re), -c_blockdiag(c_im)], axis=-2).astype(BF16)
    a = jnp.concatenate([ab_re.reshape(lead + (1, S5_SW)), ab_im.reshape(lead + (1, S5_SW))], axis=-1)
    return a, bmat, cmat


def _s5_kernel(*refs, want_y, zero_init):
    refs = list(refs)
    u_ref, bm_ref, cm_ref, a_ref, d_ref = refs[:5]
    rest = refs[5:]
    h0_ref = None if zero_init else rest.pop(0)
    y_ref = rest.pop(0) if want_y else None
    hfin_ref, ubuf, hbuf, hst = rest
    ns, sw = S5_NSEQ, S5_SW
    n_tc = S5_SEQ // S5_TC

    for d in (0, 1):
        rev = d == 1
        ar = jnp.broadcast_to(a_ref[d, :, :sw], (ns, sw))
        ai = jnp.broadcast_to(a_ref[d, :, sw:], (ns, sw))
        hst[...] = jnp.zeros((ns, 2 * sw), F32) if zero_init else h0_ref[d]

        def chunk_body(i, carry, d=d, rev=rev, ar=ar, ai=ai):
            t0 = ((n_tc - 1 - i) if rev else i) * S5_TC
            for j in range(S5_TC):
                ubuf[j * ns:(j + 1) * ns, :] = u_ref[pl.ds(t0 + j, ns, stride=S5_SEQ), :]
            hbuf[...] = _dot(ubuf[...].astype(BF16), bm_ref[d])
            hr, hi = hst[:, :sw], hst[:, sw:]
            for j in (range(S5_TC - 1, -1, -1) if rev else range(S5_TC)):
                r = slice(j * ns, (j + 1) * ns)
                hr, hi = (ar * hr - ai * hi + hbuf[r, :sw], ar * hi + ai * hr + hbuf[r, sw:])
                hbuf[r, :sw] = hr
                hbuf[r, sw:] = hi
            hst[:, :sw] = hr
            hst[:, sw:] = hi
            if want_y:
                y = _dot(hbuf[...].astype(BF16), cm_ref[d])
                for j in range(S5_TC):
                    r = slice(j * ns, (j + 1) * ns)
                    dst = pl.ds(t0 + j, ns, stride=S5_SEQ)
                    if rev:
                        y_ref[dst, :] = y_ref[dst, :] + y[r, :]
                    else:
                        y_ref[dst, :] = y[r, :] + d_ref[...] * ubuf[r, :]
            return carry

        lax.fori_loop(0, n_tc, chunk_body, 0)
        hfin_ref[d] = hst[...]


def _s5_call(proj, a, bmat, cmat, dskip, h0, layer, want_y):
    rows = proj.shape[0]
    zero_init = h0 is None
    u_col0 = (IN_W - S5_W) // LANES
    in_specs = [
        pl.BlockSpec((rows, LANES), lambda g: (0, u_col0 + g)),
        pl.BlockSpec((None, 2, None, LANES, 2 * S5_SW), lambda g: (layer, 0, g, 0, 0)),
        pl.BlockSpec((None, 2, None, 2 * S5_SW, LANES), lambda g: (layer, 0, g, 0, 0)),
        pl.BlockSpec((None, 2, None, 1, 2 * S5_SW), lambda g: (layer, 0, g, 0, 0)),
        pl.BlockSpec((None, None, 1, LANES), lambda g: (layer, g, 0, 0)),
    ]
    args = [proj, bmat, cmat, a, dskip]
    state_spec = pl.BlockSpec((2, None, S5_NSEQ, 2 * S5_SW), lambda g: (0, g, 0, 0))
    state_shape = jax.ShapeDtypeStruct((2, S5_NGB, S5_NSEQ, 2 * S5_SW), F32)
    if not zero_init:
        in_specs.append(state_spec)
        args.append(h0)
    out_specs, out_shape = [], []
    if want_y:
        out_specs.append(pl.BlockSpec((rows, LANES), lambda g: (0, g)))
        out_shape.append(jax.ShapeDtypeStruct((rows, S5_W), F32))
    out_specs.append(state_spec)
    out_shape.append(state_shape)
    res = pl.pallas_call(
        functools.partial(_s5_kernel, want_y=want_y, zero_init=zero_init),
        grid=(S5_NGB,),
        in_specs=in_specs,
        out_specs=out_specs,
        out_shape=out_shape,
        scratch_shapes=[pltpu.VMEM((S5_TC * S5_NSEQ, LANES), F32),
                        pltpu.VMEM((S5_TC * S5_NSEQ, 2 * S5_SW), F32),
                        pltpu.VMEM((S5_NSEQ, 2 * S5_SW), F32)],
        compiler_params=pltpu.CompilerParams(dimension_semantics=("parallel",),
                                             vmem_limit_bytes=VMEM_LIMIT),
        name="s5_scan",
    )(*args)
    return (res[0], res[1]) if want_y else (None, res[0])


def _s5_chain_kernel(z_ref, s0_ref, a_ref, h0_ref, *, pieces):
    sw = S5_SW
    pr, pi = a_ref[0, :, :sw], a_ref[0, :, sw:]
    for _ in range(int(math.log2(S5_SEQ))):
        pr, pi = pr * pr - pi * pi, 2.0 * (pr * pi)
    rev = pl.program_id(0) == 1
    n_long = S5_NSEQ // pieces
    for b in range(n_long):
        for fwd_order in (True, False):
            @pl.when(rev != fwd_order)
            def _(b=b, fwd_order=fwd_order):
                hr, hi = s0_ref[0, b:b + 1, :sw], s0_ref[0, b:b + 1, sw:]
                for k in (range(pieces) if fwd_order else range(pieces - 1, -1, -1)):
                    r = b * pieces + k
                    h0_ref[0, r:r + 1, :sw] = hr
                    h0_ref[0, r:r + 1, sw:] = hi
                    zr, zi = z_ref[0, r:r + 1, :sw], z_ref[0, r:r + 1, sw:]
                    hr, hi = pr * hr - pi * hi + zr, pr * hi + pi * hr + zi


def _s5_chain_call(z, s0, a, layer, pieces):
    n_long = S5_NSEQ // pieces
    return pl.pallas_call(
        functools.partial(_s5_chain_kernel, pieces=pieces),
        grid=(2, S5_NGB),
        in_specs=[pl.BlockSpec((1, None, S5_NSEQ, 2 * S5_SW), lambda d, g: (d, g, 0, 0)),
                  pl.BlockSpec((1, None, n_long, 2 * S5_SW), lambda d, g: (d, g, 0, 0)),
                  pl.BlockSpec((None, 1, None, 1, 2 * S5_SW), lambda d, g: (layer, d, g, 0, 0))],
        out_specs=pl.BlockSpec((1, None, S5_NSEQ, 2 * S5_SW), lambda d, g: (d, g, 0, 0)),
        out_shape=jax.ShapeDtypeStruct((2, S5_NGB, S5_NSEQ, 2 * S5_SW), F32),
        name="s5_chain",
    )(z, s0, a)


def _out_kernel(x_ref, ohg_ref, y5_ref, g1_ref, sh2_ref, sc2_ref, g2_ref, nffn_ref, nfin_ref,
                wglu_ref, wout_ref, wg_ref, wu_ref, wd_ref, o_ref, *, final_norm):
    y = _gelu_tanh(y5_ref[...])
    y = y * _sigmoid(_dot(y.astype(BF16), wglu_ref[...]))
    mix = _dot(ohg_ref[...].astype(BF16), wout_ref[:HG_W, :]) + _dot(y.astype(BF16), wout_ref[HG_W:, :])
    x = x_ref[...] + g1_ref[...] * mix
    h = _rms(x) * nffn_ref[...]
    h = (h * (1.0 + sc2_ref[...]) + sh2_ref[...]).astype(BF16)
    act = (_silu(_dot(h, wg_ref[...])) * _dot(h, wu_ref[...])).astype(BF16)
    x = x + g2_ref[...] * _dot(act, wd_ref[...])
    if final_norm:
        x = _rms(x) * nfin_ref[...]
    o_ref[...] = x


def _out_call(x, ohg, y5, mod4, nffn, nfin, wglu, wout, wg, wu, wd, layer, cond0, rows_per_cond, final_norm):
    rows = x.shape[0]
    tm = ROW_TILE_OUT
    vec = pl.BlockSpec((1, D_MODEL), lambda i: (0, 0))
    return pl.pallas_call(
        functools.partial(_out_kernel, final_norm=final_norm),
        grid=(rows // tm,),
        in_specs=[pl.BlockSpec((tm, D_MODEL), lambda i: (i, 0)),
                  pl.BlockSpec((tm, HG_W), lambda i: (i, 0)),
                  pl.BlockSpec((tm, S5_W), lambda i: (i, 0)),
                  _mod_spec(layer, cond0, rows_per_cond, tm, 2),
                  _mod_spec(layer, cond0, rows_per_cond, tm, 3),
                  _mod_spec(layer, cond0, rows_per_cond, tm, 4),
                  _mod_spec(layer, cond0, rows_per_cond, tm, 5),
                  vec, vec,
                  _const_spec((S5_W, S5_W)), _const_spec((D_MODEL, D_MODEL)),
                  _const_spec((D_MODEL, D_FF)), _const_spec((D_MODEL, D_FF)), _const_spec((D_FF, D_MODEL))],
        out_specs=pl.BlockSpec((tm, D_MODEL), lambda i: (i, 0)),
        out_shape=jax.ShapeDtypeStruct((rows, D_MODEL), F32),
        compiler_params=pltpu.CompilerParams(dimension_semantics=("parallel",),
                                             vmem_limit_bytes=VMEM_LIMIT),
        name="out_ffn",
    )(x, ohg, y5, mod4, mod4, mod4, mod4, nffn, nfin, wglu, wout, wg, wu, wd)


def _grid_pos_embed(n_tokens, dim):
    t = jnp.arange(n_tokens)
    r = (t // GRID_W).astype(F32)
    col = (t % GRID_W).astype(F32)
    nf = dim // 4
    omega = 1.0 / (10000.0 ** (jnp.arange(nf, dtype=F32) / nf))

    def enc(p):
        a = p[:, None] * omega[None, :]
        return jnp.concatenate([jnp.sin(a), jnp.cos(a)], axis=-1)

    return jnp.concatenate([enc(r), enc(col)], axis=-1)


def _s5_state_to_blocks(s):
    n = s.shape[0]
    s = s.reshape(n, 2, S5_NGB, S5_GB, S5_P, 2)
    return jnp.transpose(s, (1, 2, 0, 5, 3, 4)).reshape(2, S5_NGB, n, 2 * S5_SW)


def _s5_blocks_to_state(h):
    n = h.shape[2]
    h = h.reshape(2, S5_NGB, n, 2, S5_GB, S5_P)
    return jnp.transpose(h, (2, 0, 1, 4, 5, 3)).reshape(n, 2, S5_GROUPS, S5_P, 2)


def kernel(x_prompt, x_sample, state_hgrn, state_s5, c, c_ctx, w_mod, b_mod, norm_mix, norm_ffn, norm_final, w_in, w_out, hg_lb_logits, hg_norm, s5_lam_re, s5_lam_im, s5_log_dt, s5_b_re, s5_b_im, s5_c_re, s5_c_im, s5_d, s5_w_glu, w_gate, w_up, w_down):
    n_ctx, ctx_len, _ = x_prompt.shape
    n_dec, dec_len, _ = x_sample.shape
    assert ctx_len == S5_SEQ and n_ctx == S5_NSEQ and n_dec * dec_len == S5_NSEQ * S5_SEQ

    cond = jnp.concatenate([c_ctx[None, :], c, jnp.zeros((SUBLANES - 1 - n_dec, D_MODEL), F32)], axis=0)
    mod4 = _mod_call(cond, w_mod, b_mod).reshape(DEPTH, SUBLANES, 1, 6 * D_MODEL)

    w_in_b, w_out_b = w_in.astype(BF16), w_out.astype(BF16)
    w_glu_b = s5_w_glu.astype(BF16)
    w_gate_b, w_up_b, w_down_b = w_gate.astype(BF16), w_up.astype(BF16), w_down.astype(BF16)
    s5_a, s5_bmat, s5_cmat = _s5_params(s5_lam_re, s5_lam_im, s5_log_dt, s5_b_re, s5_b_im, s5_c_re, s5_c_im)
    s5_dskip = s5_d.reshape(DEPTH, S5_NGB, 1, LANES)
    nfin = norm_final.reshape(1, D_MODEL)
    pos = _grid_pos_embed(dec_len, D_MODEL)

    def run(x, pos, n_seq, seq_len, cond0, hg_state, s5_state):
        rows = x.shape[0]
        rows_per_cond = rows if cond0 == 0 else seq_len
        pieces = seq_len // S5_SEQ
        hg_finals, s5_finals = [], []
        for l in range(DEPTH):
            proj, x = _in_call(x, pos if l == 0 else None, norm_mix[l].reshape(1, D_MODEL), mod4, w_in_b[l],
                               l, cond0, rows_per_cond)
            ohg, hg_fin = _hgrn_call(proj, hg_lb_logits, hg_norm[l].reshape(1, HG_D), hg_state, l, n_seq, seq_len)
            if s5_state is None:
                y5, s5_fin = _s5_call(proj, s5_a, s5_bmat, s5_cmat, s5_dskip, None, l, True)
            else:
                _, z = _s5_call(proj, s5_a, s5_bmat, s5_cmat, s5_dskip, None, l, False)
                h0 = _s5_chain_call(z, _s5_state_to_blocks(s5_state[:, l]), s5_a, l, pieces)
                y5, s5_fin = _s5_call(proj, s5_a, s5_bmat, s5_cmat, s5_dskip, h0, l, True)
            x = _out_call(x, ohg, y5, mod4, norm_ffn[l].reshape(1, D_MODEL), nfin, w_glu_b[l], w_out_b[l],
                          w_gate_b[l], w_up_b[l], w_down_b[l], l, cond0, rows_per_cond, l == DEPTH - 1)
            hg_finals.append(hg_fin)
            s5_finals.append(_s5_blocks_to_state(s5_fin))
        return x, hg_finals, s5_finals

    y_prompt, hg_finals, s5_finals = run(x_prompt.reshape(n_ctx * ctx_len, D_MODEL), None, n_ctx, ctx_len, 0, None, None)
    y_sample, _, _ = run(x_sample.reshape(n_dec * dec_len, D_MODEL), pos, n_dec, dec_len, 1, state_hgrn, state_s5)
    return (y_prompt.reshape(x_prompt.shape), y_sample.reshape(x_sample.shape),
            jnp.stack(hg_finals, axis=1), jnp.stack(s5_finals, axis=1))
```

```python
import functools
import math

import jax
import jax.numpy as jnp
from jax import lax
from jax.experimental import pallas as pl
from jax.experimental.pallas import tpu as pltpu

F32 = jnp.float32
BF16 = jnp.bfloat16

LANES = 128
SUBLANES = 8

D_MODEL = 1024
DEPTH = 2
GRID_W = 64
HG_W = 512
HG_HEADS = 4
HG_D = HG_W // HG_HEADS
S5_W = 512
S5_CH = 16
S5_GROUPS = S5_W // S5_CH
S5_P = 64
S5_GB = LANES // S5_CH
S5_NGB = S5_GROUPS // S5_GB
S5_SW = S5_GB * S5_P
IN_W = 5 * HG_W + S5_W
D_FF = 2816
EPS = 1e-6

HG_CHUNK = 128
HG_LEVELS = (64, 32, 16, 8, 4, 2, 1)
S5_SEQ = 256
S5_NSEQ = 16
S5_TC = 32

ROW_TILE_IN = 512
ROW_TILE_OUT = 256
MOD_TILE_N = 1536
VMEM_LIMIT = 56 * 1024 * 1024


def _sigmoid(x):
    return 1.0 / (1.0 + jnp.exp(-x))


def _silu(x):
    return x * _sigmoid(x)


def _gelu_tanh(x):
    return 0.5 * x * (1.0 + jnp.tanh(math.sqrt(2.0 / math.pi) * (x + 0.044715 * (x * x * x))))


def _rms(x):
    return x * lax.rsqrt(jnp.mean(x * x, axis=-1, keepdims=True) + EPS)


def _dot(a, b):
    return jnp.dot(a, b, preferred_element_type=F32)


def _dot_nt(a, b):
    return lax.dot_general(a, b, (((1,), (1,)), ((), ())), preferred_element_type=F32)


def _dot_tn(a, b):
    return lax.dot_general(a, b, (((0,), (0,)), ((), ())), preferred_element_type=F32)


def _const_spec(shape):
    nd = len(shape)
    return pl.BlockSpec(shape, lambda *_: (0,) * nd, pipeline_mode=pl.Buffered(1))


def _mod_kernel(cond_ref, w_ref, b_ref, o_ref):
    a = _silu(cond_ref[...]).astype(BF16)
    o_ref[0] = _dot(a, w_ref[0].astype(BF16)) + b_ref[0]


def _mod_call(cond, w_mod, b_mod):
    n_cond = cond.shape[0]
    n_out = w_mod.shape[-1]
    return pl.pallas_call(
        _mod_kernel,
        grid=(DEPTH, n_out // MOD_TILE_N),
        in_specs=[
            pl.BlockSpec((n_cond, D_MODEL), lambda l, j: (0, 0)),
            pl.BlockSpec((1, D_MODEL, MOD_TILE_N), lambda l, j: (l, 0, j)),
            pl.BlockSpec((1, 1, MOD_TILE_N), lambda l, j: (l, 0, j)),
        ],
        out_specs=pl.BlockSpec((1, n_cond, MOD_TILE_N), lambda l, j: (l, 0, j)),
        out_shape=jax.ShapeDtypeStruct((DEPTH, n_cond, n_out), F32),
        compiler_params=pltpu.CompilerParams(dimension_semantics=("parallel", "parallel"),
                                             vmem_limit_bytes=VMEM_LIMIT),
        name="adaln_mod",
    )(cond, w_mod, b_mod.reshape(DEPTH, 1, n_out))


def _in_kernel(*refs, add_pos):
    if add_pos:
        x_ref, pos_ref, gain_ref, sh_ref, sc_ref, w_ref, proj_ref, xs_ref = refs
        x = x_ref[...] + pos_ref[...]
        xs_ref[...] = x
    else:
        x_ref, gain_ref, sh_ref, sc_ref, w_ref, proj_ref = refs
        x = x_ref[...]
    h = _rms(x) * gain_ref[...]
    h = h * (1.0 + sc_ref[...]) + sh_ref[...]
    proj_ref[...] = _dot(h.astype(BF16), w_ref[...])


def _mod_spec(layer, cond0, rows_per_cond, tile, col):
    return pl.BlockSpec((None, None, 1, D_MODEL),
                        lambda i: (layer, cond0 + (i * tile) // rows_per_cond, 0, col))


def _in_call(x, pos, gain, mod4, w_in_b, layer, cond0, rows_per_cond):
    rows = x.shape[0]
    tm = ROW_TILE_IN
    add_pos = pos is not None
    row_spec = pl.BlockSpec((tm, D_MODEL), lambda i: (i, 0))
    in_specs = [row_spec]
    args = [x]
    if add_pos:
        pos_tiles = pos.shape[0] // tm
        in_specs.append(pl.BlockSpec((tm, D_MODEL), lambda i: (i % pos_tiles, 0)))
        args.append(pos)
    in_specs += [
        pl.BlockSpec((1, D_MODEL), lambda i: (0, 0)),
        _mod_spec(layer, cond0, rows_per_cond, tm, 0),
        _mod_spec(layer, cond0, rows_per_cond, tm, 1),
        _const_spec((D_MODEL, IN_W)),
    ]
    args += [gain, mod4, mod4, w_in_b]
    out_specs = [pl.BlockSpec((tm, IN_W), lambda i: (i, 0))]
    out_shape = [jax.ShapeDtypeStruct((rows, IN_W), F32)]
    if add_pos:
        out_specs.append(row_spec)
        out_shape.append(jax.ShapeDtypeStruct((rows, D_MODEL), F32))
    res = pl.pallas_call(
        functools.partial(_in_kernel, add_pos=add_pos),
        grid=(rows // tm,),
        in_specs=in_specs,
        out_specs=out_specs,
        out_shape=out_shape,
        compiler_params=pltpu.CompilerParams(dimension_semantics=("parallel",),
                                             vmem_limit_bytes=VMEM_LIMIT),
        name="in_proj",
    )(*args)
    return (res[0], res[1]) if add_pos else (res[0], x)


def _pair_boundary(b, m, rev):
    c = b.shape[0]
    span = 2 * m
    at = m if rev else m - 1
    if span >= SUBLANES:
        b3 = b.reshape(c // span, span, LANES)
        return jnp.broadcast_to(b3[:, at:at + 1, :], b3.shape).reshape(c, LANES)
    b3 = b.reshape(c // SUBLANES, SUBLANES, LANES)
    sub = lax.broadcasted_iota(jnp.int32, b3.shape, 1)
    out = None
    for p in range(SUBLANES // span):
        piece = jnp.broadcast_to(b3[:, p * span + at:p * span + at + 1, :], b3.shape)
        out = piece if out is None else jnp.where(sub >= p * span, piece, out)
    return out.reshape(c, LANES)


def _hg_chunks(chains, code, eye):
    n = len(chains)
    c = chains[0][0].shape[0]
    keys, bs = [], []
    for qh, fl, v, lb, st_ref, tri, rev in chains:
        sig = _sigmoid(fl)
        logf = jnp.log(lb + (1.0 - lb) * sig)
        keys.append((1.0 - lb) * (1.0 - sig))
        hi = logf.astype(BF16)
        r1 = logf - hi.astype(F32)
        mid = r1.astype(BF16)
        lo = (r1 - mid.astype(F32)).astype(BF16)
        parts = _dot(tri, jnp.concatenate([hi, mid, lo], axis=1))
        bs.append(parts[:, :LANES] + parts[:, LANES:2 * LANES] + parts[:, 2 * LANES:])

    vbs = [ch[2].astype(BF16) for ch in chains]
    outs = []
    for (qh, fl, v, lb, st_ref, tri, rev), key, b, vb in zip(chains, keys, bs, vbs):
        b_edge = b[0:1, :] if rev else b[c - 1:c, :]
        st = st_ref[...]
        outs.append(_dot_nt((qh * jnp.exp(b)).astype(BF16), st.astype(BF16)))
        k_end = key * jnp.exp(b_edge - b)
        st_ref[...] = jnp.exp(b_edge) * st + _dot_tn(vb, k_end.astype(BF16))

    scores = [jnp.where(eye, jnp.sum(ch[0] * key, axis=-1, keepdims=True), 0.0) for ch, key in zip(chains, keys)]
    for m in HG_LEVELS:
        k = int(math.log2(m)) + 1
        for i in range(n):
            qh, rev = chains[i][0], chains[i][6]
            e = jnp.exp(-jnp.abs(bs[i] - _pair_boundary(bs[i], m, rev)))
            p = _dot_nt((qh * e).astype(BF16), (keys[i] * e).astype(BF16))
            scores[i] = jnp.where(code == (-k if rev else k), p, scores[i])
    return [o + _dot(sc.astype(BF16), vb) for o, sc, vb in zip(outs, scores, vbs)]


def _hgrn_kernel(*refs, layer, n_chunks, zero_init):
    if zero_init:
        q_ref, ff_ref, fb_ref, v_ref, g_ref, lbl_ref, gain_ref, o_ref, sfin_ref, st_ref, ob_ref = refs
        s0_ref = None
    else:
        q_ref, ff_ref, fb_ref, v_ref, g_ref, lbl_ref, gain_ref, s0_ref, o_ref, sfin_ref, st_ref, ob_ref = refs
    c = HG_CHUNK
    t = lax.broadcasted_iota(jnp.int32, (c, c), 0)
    s = lax.broadcasted_iota(jnp.int32, (c, c), 1)
    eye = t == s
    x = t ^ s
    lvl = sum((x >= (1 << k)).astype(jnp.int32) for k in range(len(HG_LEVELS)))
    code = jnp.where(t > s, lvl, -lvl)
    tri_f = (s <= t).astype(BF16)
    tri_b = (s >= t).astype(BF16)

    def lower_bound(d):
        lg = lbl_ref[d]
        ex = jnp.exp(lg - jnp.max(lg, axis=0, keepdims=True))
        soft = ex / jnp.sum(ex, axis=0, keepdims=True)
        return jnp.sum(soft[:layer + 1], axis=0, keepdims=True) - soft[0:1]

    lb_f, lb_b = lower_bound(0), lower_bound(1)
    scale = HG_D ** -0.5

    def rows_of(n):
        return pl.ds(pl.multiple_of(n * c, c), c)

    for d in (0, 1):
        st_ref[d] = jnp.zeros((HG_D, HG_D), F32) if zero_init else s0_ref[d].T

    def body(i, carry):
        rf = rows_of(i)
        rb = rows_of(n_chunks - 1 - i)
        qf = _silu(q_ref[rf, :]) * scale
        qb = _silu(q_ref[rb, :]) * scale
        of, ob = _hg_chunks([(qf, ff_ref[rf, :], v_ref[rf, :], lb_f, st_ref.at[0], tri_f, False),
                             (qb, fb_ref[rb, :], v_ref[rb, :], lb_b, st_ref.at[1], tri_b, True)], code, eye)
        o_ref[rf, :] = of
        ob_ref[rb, :] = ob
        return carry

    lax.fori_loop(0, n_chunks, body, 0)
    for d in (0, 1):
        sfin_ref[d] = st_ref[d].T

    def finish(n, carry):
        rows = rows_of(n)
        o_ref[rows, :] = _rms(o_ref[rows, :] + ob_ref[rows, :]) * gain_ref[...] * _silu(g_ref[rows, :])
        return carry

    lax.fori_loop(0, n_chunks, finish, 0)


def _hgrn_call(proj, lb_logits, gain, state, layer, n_seq, seq_len):
    rows = proj.shape[0]
    zero_init = state is None

    def col_spec(k):
        return pl.BlockSpec((seq_len, HG_D), lambda b, h: (b, k * HG_HEADS + h))

    in_specs = [col_spec(0), col_spec(1), col_spec(2), col_spec(3), col_spec(4),
                pl.BlockSpec((2, DEPTH, HG_D), lambda b, h: (0, 0, h)),
                pl.BlockSpec((1, HG_D), lambda b, h: (0, 0))]
    args = [proj] * 5 + [lb_logits, gain]
    if not zero_init:
        in_specs.append(pl.BlockSpec((None, None, 2, None, HG_D, HG_D), lambda b, h: (b, layer, 0, h, 0, 0)))
        args.append(state)
    return pl.pallas_call(
        functools.partial(_hgrn_kernel, layer=layer, n_chunks=seq_len // HG_CHUNK, zero_init=zero_init),
        grid=(n_seq, HG_HEADS),
        in_specs=in_specs,
        out_specs=[pl.BlockSpec((seq_len, HG_D), lambda b, h: (b, h)),
                   pl.BlockSpec((None, 2, None, HG_D, HG_D), lambda b, h: (b, 0, h, 0, 0))],
        out_shape=[jax.ShapeDtypeStruct((rows, HG_W), F32),
                   jax.ShapeDtypeStruct((n_seq, 2, HG_HEADS, HG_D, HG_D), F32)],
        scratch_shapes=[pltpu.VMEM((2, HG_D, HG_D), F32), pltpu.VMEM((seq_len, HG_D), F32)],
        compiler_params=pltpu.CompilerParams(dimension_semantics=("parallel", "parallel"),
                                             vmem_limit_bytes=VMEM_LIMIT),
        name="hgrn2_mixer",
    )(*args)


def _s5_disc_kernel(lr_ref, li_ref, ldt_ref, ar_ref, ai_ref, zr_ref, zi_ref):
    lr = jnp.minimum(lr_ref[...], -1e-4)
    li = li_ref[...]
    dt = jnp.exp(ldt_ref[...])
    mag = jnp.exp(lr * dt)
    ab_re = mag * jnp.cos(li * dt)
    ab_im = mag * jnp.sin(li * dt)
    nr = ab_re - 1.0
    den = lr * lr + li * li
    ar_ref[...] = ab_re
    ai_ref[...] = ab_im
    zr_ref[...] = (nr * lr + ab_im * li) / den
    zi_ref[...] = (ab_im * lr - nr * li) / den


def _s5_bbar_kernel(zr_ref, zi_ref, br_ref, bi_ref, or_ref, oi_ref):
    zr, zi = zr_ref[...], zi_ref[...]
    br, bi = br_ref[...], bi_ref[...]
    or_ref[...] = zr * br - zi * bi
    oi_ref[...] = zr * bi + zi * br


def _s5_params(lam_re, lam_im, log_dt, b_re, b_im, c_re, c_im):
    n = DEPTH * 2 * S5_GROUPS
    shp = jax.ShapeDtypeStruct((n, S5_P), F32)
    ab_re, ab_im, z_re, z_im = pl.pallas_call(
        _s5_disc_kernel, out_shape=[shp] * 4, name="s5_discretise",
    )(lam_re.reshape(n, S5_P), lam_im.reshape(n, S5_P),
      jnp.broadcast_to(log_dt.reshape(n, 1), (n, S5_P)))
    wide = jax.ShapeDtypeStruct((n, S5_P * S5_CH), F32)
    bb_re, bb_im = pl.pallas_call(
        _s5_bbar_kernel, out_shape=[wide] * 2, name="s5_bbar",
    )(jnp.repeat(z_re, S5_CH, axis=1), jnp.repeat(z_im, S5_CH, axis=1),
      b_re.reshape(n, S5_P * S5_CH), b_im.reshape(n, S5_P * S5_CH))

    eye = jnp.eye(S5_GB, dtype=F32)
    lead = (DEPTH, 2, S5_NGB)

    def b_blockdiag(bb):
        bb = bb.reshape(lead + (S5_GB, S5_P, S5_CH))
        return jnp.einsum('ldbgpc,gh->ldbgchp', bb, eye).reshape(lead + (LANES, S5_SW))

    def c_blockdiag(cc):
        cc = cc.reshape(lead + (S5_GB, S5_CH, S5_P))
        return jnp.einsum('ldbgcp,gh->ldbhpgc', cc, eye).reshape(lead + (S5_SW, LANES))

    bmat = jnp.concatenate([b_blockdiag(bb_re), b_blockdiag(bb_im)], axis=-1).astype(BF16)
    cmat = jnp.concatenate([c_blockdiag(c_re), -c_blockdiag(c_im)], axis=-2).astype(BF16)
    a = jnp.concatenate([ab_re.reshape(lead + (1, S5_SW)), ab_im.reshape(lead + (1, S5_SW))], axis=-1)
    return a, bmat, cmat


def _s5_kernel(*refs, want_y, zero_init):
    refs = list(refs)
    u_ref, bm_ref, cm_ref, a_ref, d_ref = refs[:5]
    rest = refs[5:]
    h0_ref = None if zero_init else rest.pop(0)
    y_ref = rest.pop(0) if want_y else None
    hfin_ref, ubuf, hbuf, hst = rest
    ns, sw = S5_NSEQ, S5_SW
    n_tc = S5_SEQ // S5_TC

    for d in (0, 1):
        rev = d == 1
        ar = jnp.broadcast_to(a_ref[d, :, :sw], (ns, sw))
        ai = jnp.broadcast_to(a_ref[d, :, sw:], (ns, sw))
        hst[...] = jnp.zeros((ns, 2 * sw), F32) if zero_init else h0_ref[d]

        def chunk_body(i, carry, d=d, rev=rev, ar=ar, ai=ai):
            t0 = ((n_tc - 1 - i) if rev else i) * S5_TC
            for j in range(S5_TC):
                ubuf[j * ns:(j + 1) * ns, :] = u_ref[pl.ds(t0 + j, ns, stride=S5_SEQ), :]
            hbuf[...] = _dot(ubuf[...].astype(BF16), bm_ref[d])
            hr, hi = hst[:, :sw], hst[:, sw:]
            for j in (range(S5_TC - 1, -1, -1) if rev else range(S5_TC)):
                r = slice(j * ns, (j + 1) * ns)
                hr, hi = (ar * hr - ai * hi + hbuf[r, :sw], ar * hi + ai * hr + hbuf[r, sw:])
                hbuf[r, :sw] = hr
                hbuf[r, sw:] = hi
            hst[:, :sw] = hr
            hst[:, sw:] = hi
            if want_y:
                y = _dot(hbuf[...].astype(BF16), cm_ref[d])
                for j in range(S5_TC):
                    r = slice(j * ns, (j + 1) * ns)
                    dst = pl.ds(t0 + j, ns, stride=S5_SEQ)
                    if rev:
                        y_ref[dst, :] = y_ref[dst, :] + y[r, :]
                    else:
                        y_ref[dst, :] = y[r, :] + d_ref[...] * ubuf[r, :]
            return carry

        lax.fori_loop(0, n_tc, chunk_body, 0)
        hfin_ref[d] = hst[...]


def _s5_call(proj, a, bmat, cmat, dskip, h0, layer, want_y):
    rows = proj.shape[0]
    zero_init = h0 is None
    u_col0 = (IN_W - S5_W) // LANES
    in_specs = [
        pl.BlockSpec((rows, LANES), lambda g: (0, u_col0 + g)),
        pl.BlockSpec((None, 2, None, LANES, 2 * S5_SW), lambda g: (layer, 0, g, 0, 0)),
        pl.BlockSpec((None, 2, None, 2 * S5_SW, LANES), lambda g: (layer, 0, g, 0, 0)),
        pl.BlockSpec((None, 2, None, 1, 2 * S5_SW), lambda g: (layer, 0, g, 0, 0)),
        pl.BlockSpec((None, None, 1, LANES), lambda g: (layer, g, 0, 0)),
    ]
    args = [proj, bmat, cmat, a, dskip]
    state_spec = pl.BlockSpec((2, None, S5_NSEQ, 2 * S5_SW), lambda g: (0, g, 0, 0))
    state_shape = jax.ShapeDtypeStruct((2, S5_NGB, S5_NSEQ, 2 * S5_SW), F32)
    if not zero_init:
        in_specs.append(state_spec)
        args.append(h0)
    out_specs, out_shape = [], []
    if want_y:
        out_specs.append(pl.BlockSpec((rows, LANES), lambda g: (0, g)))
        out_shape.append(jax.ShapeDtypeStruct((rows, S5_W), F32))
    out_specs.append(state_spec)
    out_shape.append(state_shape)
    res = pl.pallas_call(
        functools.partial(_s5_kernel, want_y=want_y, zero_init=zero_init),
        grid=(S5_NGB,),
        in_specs=in_specs,
        out_specs=out_specs,
        out_shape=out_shape,
        scratch_shapes=[pltpu.VMEM((S5_TC * S5_NSEQ, LANES), F32),
                        pltpu.VMEM((S5_TC * S5_NSEQ, 2 * S5_SW), F32),
                        pltpu.VMEM((S5_NSEQ, 2 * S5_SW), F32)],
        compiler_params=pltpu.CompilerParams(dimension_semantics=("parallel",),
                                             vmem_limit_bytes=VMEM_LIMIT),
        name="s5_scan",
    )(*args)
    return (res[0], res[1]) if want_y else (None, res[0])


def _s5_chain_kernel(z_ref, s0_ref, a_ref, h0_ref, *, pieces):
    sw = S5_SW
    pr, pi = a_ref[0, :, :sw], a_ref[0, :, sw:]
    for _ in range(int(math.log2(S5_SEQ))):
        pr, pi = pr * pr - pi * pi, 2.0 * (pr * pi)
    rev = pl.program_id(0) == 1
    n_long = S5_NSEQ // pieces
    for b in range(n_long):
        for fwd_order in (True, False):
            @pl.when(rev != fwd_order)
            def _(b=b, fwd_order=fwd_order):
                hr, hi = s0_ref[0, b:b + 1, :sw], s0_ref[0, b:b + 1, sw:]
                for k in (range(pieces) if fwd_order else range(pieces - 1, -1, -1)):
                    r = b * pieces + k
                    h0_ref[0, r:r + 1, :sw] = hr
                    h0_ref[0, r:r + 1, sw:] = hi
                    zr, zi = z_ref[0, r:r + 1, :sw], z_ref[0, r:r + 1, sw:]
                    hr, hi = pr * hr - pi * hi + zr, pr * hi + pi * hr + zi


def _s5_chain_call(z, s0, a, layer, pieces):
    n_long = S5_NSEQ // pieces
    return pl.pallas_call(
        functools.partial(_s5_chain_kernel, pieces=pieces),
        grid=(2, S5_NGB),
        in_specs=[pl.BlockSpec((1, None, S5_NSEQ, 2 * S5_SW), lambda d, g: (d, g, 0, 0)),
                  pl.BlockSpec((1, None, n_long, 2 * S5_SW), lambda d, g: (d, g, 0, 0)),
                  pl.BlockSpec((None, 1, None, 1, 2 * S5_SW), lambda d, g: (layer, d, g, 0, 0))],
        out_specs=pl.BlockSpec((1, None, S5_NSEQ, 2 * S5_SW), lambda d, g: (d, g, 0, 0)),
        out_shape=jax.ShapeDtypeStruct((2, S5_NGB, S5_NSEQ, 2 * S5_SW), F32),
        name="s5_chain",
    )(z, s0, a)


def _out_kernel(x_ref, ohg_ref, y5_ref, g1_ref, sh2_ref, sc2_ref, g2_ref, nffn_ref, nfin_ref,
                wglu_ref, wout_ref, wg_ref, wu_ref, wd_ref, o_ref, *, final_norm):
    y = _gelu_tanh(y5_ref[...])
    y = y * _sigmoid(_dot(y.astype(BF16), wglu_ref[...]))
    mix = _dot(ohg_ref[...].astype(BF16), wout_ref[:HG_W, :]) + _dot(y.astype(BF16), wout_ref[HG_W:, :])
    x = x_ref[...] + g1_ref[...] * mix
    h = _rms(x) * nffn_ref[...]
    h = (h * (1.0 + sc2_ref[...]) + sh2_ref[...]).astype(BF16)
    act = (_silu(_dot(h, wg_ref[...])) * _dot(h, wu_ref[...])).astype(BF16)
    x = x + g2_ref[...] * _dot(act, wd_ref[...])
    if final_norm:
        x = _rms(x) * nfin_ref[...]
    o_ref[...] = x


def _out_call(x, ohg, y5, mod4, nffn, nfin, wglu, wout, wg, wu, wd, layer, cond0, rows_per_cond, final_norm):
    rows = x.shape[0]
    tm = ROW_TILE_OUT
    vec = pl.BlockSpec((1, D_MODEL), lambda i: (0, 0))
    return pl.pallas_call(
        functools.partial(_out_kernel, final_norm=final_norm),
        grid=(rows // tm,),
        in_specs=[pl.BlockSpec((tm, D_MODEL), lambda i: (i, 0)),
                  pl.BlockSpec((tm, HG_W), lambda i: (i, 0)),
                  pl.BlockSpec((tm, S5_W), lambda i: (i, 0)),
                  _mod_spec(layer, cond0, rows_per_cond, tm, 2),
                  _mod_spec(layer, cond0, rows_per_cond, tm, 3),
                  _mod_spec(layer, cond0, rows_per_cond, tm, 4),
                  _mod_spec(layer, cond0, rows_per_cond, tm, 5),
                  vec, vec,
                  _const_spec((S5_W, S5_W)), _const_spec((D_MODEL, D_MODEL)),
                  _const_spec((D_MODEL, D_FF)), _const_spec((D_MODEL, D_FF)), _const_spec((D_FF, D_MODEL))],
        out_specs=pl.BlockSpec((tm, D_MODEL), lambda i: (i, 0)),
        out_shape=jax.ShapeDtypeStruct((rows, D_MODEL), F32),
        compiler_params=pltpu.CompilerParams(dimension_semantics=("parallel",),
                                             vmem_limit_bytes=VMEM_LIMIT),
        name="out_ffn",
    )(x, ohg, y5, mod4, mod4, mod4, mod4, nffn, nfin, wglu, wout, wg, wu, wd)


def _grid_pos_embed(n_tokens, dim):
    t = jnp.arange(n_tokens)
    r = (t // GRID_W).astype(F32)
    col = (t % GRID_W).astype(F32)
    nf = dim // 4
    omega = 1.0 / (10000.0 ** (jnp.arange(nf, dtype=F32) / nf))

    def enc(p):
        a = p[:, None] * omega[None, :]
        return jnp.concatenate([jnp.sin(a), jnp.cos(a)], axis=-1)

    return jnp.concatenate([enc(r), enc(col)], axis=-1)


def _s5_state_to_blocks(s):
    n = s.shape[0]
    s = s.reshape(n, 2, S5_NGB, S5_GB, S5_P, 2)
    return jnp.transpose(s, (1, 2, 0, 5, 3, 4)).reshape(2, S5_NGB, n, 2 * S5_SW)


def _s5_blocks_to_state(h):
    n = h.shape[2]
    h = h.reshape(2, S5_NGB, n, 2, S5_GB, S5_P)
    return jnp.transpose(h, (2, 0, 1, 4, 5, 3)).reshape(n, 2, S5_GROUPS, S5_P, 2)


def kernel(x_prompt, x_sample, state_hgrn, state_s5, c, c_ctx, w_mod, b_mod, norm_mix, norm_ffn, norm_final, w_in, w_out, hg_lb_logits, hg_norm, s5_lam_re, s5_lam_im, s5_log_dt, s5_b_re, s5_b_im, s5_c_re, s5_c_im, s5_d, s5_w_glu, w_gate, w_up, w_down):
    n_ctx, ctx_len, _ = x_prompt.shape
    n_dec, dec_len, _ = x_sample.shape
    assert ctx_len == S5_SEQ and n_ctx == S5_NSEQ and n_dec * dec_len == S5_NSEQ * S5_SEQ

    cond = jnp.concatenate([c_ctx[None, :], c, jnp.zeros((SUBLANES - 1 - n_dec, D_MODEL), F32)], axis=0)
    mod4 = _mod_call(cond, w_mod, b_mod).reshape(DEPTH, SUBLANES, 1, 6 * D_MODEL)

    w_in_b, w_out_b = w_in.astype(BF16), w_out.astype(BF16)
    w_glu_b = s5_w_glu.astype(BF16)
    w_gate_b, w_up_b, w_down_b = w_gate.astype(BF16), w_up.astype(BF16), w_down.astype(BF16)
    s5_a, s5_bmat, s5_cmat = _s5_params(s5_lam_re, s5_lam_im, s5_log_dt, s5_b_re, s5_b_im, s5_c_re, s5_c_im)
    s5_dskip = s5_d.reshape(DEPTH, S5_NGB, 1, LANES)
    nfin = norm_final.reshape(1, D_MODEL)
    pos = _grid_pos_embed(dec_len, D_MODEL)

    def run(x, pos, n_seq, seq_len, cond0, hg_state, s5_state):
        rows = x.shape[0]
        rows_per_cond = rows if cond0 == 0 else seq_len
        pieces = seq_len // S5_SEQ
        hg_finals, s5_finals = [], []
        for l in range(DEPTH):
            proj, x = _in_call(x, pos if l == 0 else None, norm_mix[l].reshape(1, D_MODEL), mod4, w_in_b[l],
                               l, cond0, rows_per_cond)
            ohg, hg_fin = _hgrn_call(proj, hg_lb_logits, hg_norm[l].reshape(1, HG_D), hg_state, l, n_seq, seq_len)
            if s5_state is None:
                y5, s5_fin = _s5_call(proj, s5_a, s5_bmat, s5_cmat, s5_dskip, None, l, True)
            else:
                _, z = _s5_call(proj, s5_a, s5_bmat, s5_cmat, s5_dskip, None, l, False)
                h0 = _s5_chain_call(z, _s5_state_to_blocks(s5_state[:, l]), s5_a, l, pieces)
                y5, s5_fin = _s5_call(proj, s5_a, s5_bmat, s5_cmat, s5_dskip, h0, l, True)
            x = _out_call(x, ohg, y5, mod4, norm_ffn[l].reshape(1, D_MODEL), nfin, w_glu_b[l], w_out_b[l],
                          w_gate_b[l], w_up_b[l], w_down_b[l], l, cond0, rows_per_cond, l == DEPTH - 1)
            hg_finals.append(hg_fin)
            s5_finals.append(_s5_blocks_to_state(s5_fin))
        return x, hg_finals, s5_finals

    y_prompt, hg_finals, s5_finals = run(x_prompt.reshape(n_ctx * ctx_len, D_MODEL), None, n_ctx, ctx_len, 0, None, None)
    y_sample, _, _ = run(x_sample.reshape(n_dec * dec_len, D_MODEL), pos, n_dec, dec_len, 1, state_hgrn, state_s5)
    return (y_prompt.reshape(x_prompt.shape), y_sample.reshape(x_sample.shape),
            jnp.stack(hg_finals, axis=1), jnp.stack(s5_finals, axis=1))
```

```python
import functools
import math

import jax
import jax.numpy as jnp
import numpy as np
from jax import lax
from jax.experimental import pallas as pl
from jax.experimental.pallas import tpu as pltpu

F32 = jnp.float32
BF16 = jnp.bfloat16

LANES = 128
SUBLANES = 8

D_MODEL = 1024
DEPTH = 2
GRID_W = 64
HG_W = 512
HG_HEADS = 4
HG_D = HG_W // HG_HEADS
S5_W = 512
S5_CH = 16
S5_GROUPS = S5_W // S5_CH
S5_P = 64
S5_GB = LANES // S5_CH
S5_NGB = S5_GROUPS // S5_GB
S5_SW = S5_GB * S5_P
IN_W = 5 * HG_W + S5_W
D_FF = 2816
EPS = 1e-6

HG_CHUNK = 128
HG_LEVELS = (64, 32, 16, 8, 4, 2, 1)
S5_SEQ = 256
S5_NSEQ = 16
S5_TC = 32

ROW_TILE_IN = 512
ROW_TILE_OUT = 256
MOD_TILE_N = 1536
VMEM_LIMIT = 56 * 1024 * 1024


def _sigmoid(x):
    return 1.0 / (1.0 + jnp.exp(-x))


def _silu(x):
    return x * _sigmoid(x)


def _gelu_tanh(x):
    return 0.5 * x * (1.0 + jnp.tanh(math.sqrt(2.0 / math.pi) * (x + 0.044715 * (x * x * x))))


def _rms(x):
    return x * lax.rsqrt(jnp.mean(x * x, axis=-1, keepdims=True) + EPS)


def _dot(a, b):
    return jnp.dot(a, b, preferred_element_type=F32)


def _dot_nt(a, b):
    return lax.dot_general(a, b, (((1,), (1,)), ((), ())), preferred_element_type=F32)


def _dot_tn(a, b):
    return lax.dot_general(a, b, (((0,), (0,)), ((), ())), preferred_element_type=F32)


def _const_spec(shape):
    nd = len(shape)
    return pl.BlockSpec(shape, lambda *_: (0,) * nd, pipeline_mode=pl.Buffered(1))


def _mod_kernel(cond_ref, w_ref, b_ref, o_ref):
    a = _silu(cond_ref[...]).astype(BF16)
    o_ref[0] = _dot(a, w_ref[0].astype(BF16)) + b_ref[0]


def _mod_call(cond, w_mod, b_mod):
    n_cond = cond.shape[0]
    n_out = w_mod.shape[-1]
    return pl.pallas_call(
        _mod_kernel,
        grid=(DEPTH, n_out // MOD_TILE_N),
        in_specs=[
            pl.BlockSpec((n_cond, D_MODEL), lambda l, j: (0, 0)),
            pl.BlockSpec((1, D_MODEL, MOD_TILE_N), lambda l, j: (l, 0, j)),
            pl.BlockSpec((1, 1, MOD_TILE_N), lambda l, j: (l, 0, j)),
        ],
        out_specs=pl.BlockSpec((1, n_cond, MOD_TILE_N), lambda l, j: (l, 0, j)),
        out_shape=jax.ShapeDtypeStruct((DEPTH, n_cond, n_out), F32),
        compiler_params=pltpu.CompilerParams(dimension_semantics=("parallel", "parallel"),
                                             vmem_limit_bytes=VMEM_LIMIT),
        name="adaln_mod",
    )(cond, w_mod, b_mod.reshape(DEPTH, 1, n_out))


def _in_kernel(*refs, add_pos):
    if add_pos:
        x_ref, pos_ref, gain_ref, sh_ref, sc_ref, w_ref, proj_ref, xs_ref = refs
        x = x_ref[...] + pos_ref[...]
        xs_ref[...] = x
    else:
        x_ref, gain_ref, sh_ref, sc_ref, w_ref, proj_ref = refs
        x = x_ref[...]
    h = _rms(x) * gain_ref[...]
    h = h * (1.0 + sc_ref[...]) + sh_ref[...]
    proj_ref[...] = _dot(h.astype(BF16), w_ref[...])


def _mod_spec(layer, cond0, rows_per_cond, tile, col):
    return pl.BlockSpec((None, None, 1, D_MODEL),
                        lambda i: (layer, cond0 + (i * tile) // rows_per_cond, 0, col))


def _in_call(x, pos, gain, mod4, w_in_b, layer, cond0, rows_per_cond):
    rows = x.shape[0]
    tm = ROW_TILE_IN
    add_pos = pos is not None
    row_spec = pl.BlockSpec((tm, D_MODEL), lambda i: (i, 0))
    in_specs = [row_spec]
    args = [x]
    if add_pos:
        pos_tiles = pos.shape[0] // tm
        in_specs.append(pl.BlockSpec((tm, D_MODEL), lambda i: (i % pos_tiles, 0)))
        args.append(pos)
    in_specs += [
        pl.BlockSpec((1, D_MODEL), lambda i: (0, 0)),
        _mod_spec(layer, cond0, rows_per_cond, tm, 0),
        _mod_spec(layer, cond0, rows_per_cond, tm, 1),
        _const_spec((D_MODEL, IN_W)),
    ]
    args += [gain, mod4, mod4, w_in_b]
    out_specs = [pl.BlockSpec((tm, IN_W), lambda i: (i, 0))]
    out_shape = [jax.ShapeDtypeStruct((rows, IN_W), F32)]
    if add_pos:
        out_specs.append(row_spec)
        out_shape.append(jax.ShapeDtypeStruct((rows, D_MODEL), F32))
    res = pl.pallas_call(
        functools.partial(_in_kernel, add_pos=add_pos),
        grid=(rows // tm,),
        in_specs=in_specs,
        out_specs=out_specs,
        out_shape=out_shape,
        compiler_params=pltpu.CompilerParams(dimension_semantics=("parallel",),
                                             vmem_limit_bytes=VMEM_LIMIT),
        name="in_proj",
    )(*args)
    return (res[0], res[1]) if add_pos else (res[0], x)


def _pair_boundary(b, m, rev):
    c = b.shape[0]
    span = 2 * m
    at = m if rev else m - 1
    if span >= SUBLANES:
        b3 = b.reshape(c // span, span, LANES)
        return jnp.broadcast_to(b3[:, at:at + 1, :], b3.shape).reshape(c, LANES)
    b3 = b.reshape(c // SUBLANES, SUBLANES, LANES)
    sub = lax.broadcasted_iota(jnp.int32, b3.shape, 1)
    out = None
    for p in range(SUBLANES // span):
        piece = jnp.broadcast_to(b3[:, p * span + at:p * span + at + 1, :], b3.shape)
        out = piece if out is None else jnp.where(sub >= p * span, piece, out)
    return out.reshape(c, LANES)


def _neg_abs(x):
    bits = lax.bitcast_convert_type(x, jnp.uint32) | jnp.uint32(0x80000000)
    return lax.bitcast_convert_type(bits, F32)


def _hg_gates(chains, scale):
    outs = []
    for q, fl, lb, tri in chains:
        sig = _sigmoid(fl)
        logf = jnp.log2(lb + (1.0 - lb) * sig)
        key = (1.0 - lb) * (1.0 - sig)
        hi = logf.astype(BF16)
        r1 = logf - hi.astype(F32)
        mid = r1.astype(BF16)
        lo = (r1 - mid.astype(F32)).astype(BF16)
        parts = _dot(tri, jnp.concatenate([hi, mid, lo], axis=1))
        b2 = parts[:, :LANES] + parts[:, LANES:2 * LANES] + parts[:, 2 * LANES:]
        outs.append((_silu(q) * scale, key, b2))
    return outs


def _hg_scores(chains, code, eye):
    c = chains[0][0].shape[0]
    o_inter = []
    for qh, key, b2, v, st_ref, rev in chains:
        b_edge = b2[0:1, :] if rev else b2[c - 1:c, :]
        st = st_ref[...]
        o_inter.append(_dot_nt((qh * jnp.exp2(b2)).astype(BF16), st.astype(BF16)))
        k_end = key * jnp.exp2(b_edge - b2)
        st_ref[...] = jnp.exp2(b_edge) * st + _dot_tn(v.astype(BF16), k_end.astype(BF16))

    scores = [jnp.where(eye, jnp.sum(ch[0] * ch[1], axis=-1, keepdims=True), 0.0) for ch in chains]
    for m in HG_LEVELS:
        k = int(math.log2(m)) + 1
        for i, (qh, key, b2, v, st_ref, rev) in enumerate(chains):
            e = jnp.exp2(_neg_abs(b2 - _pair_boundary(b2, m, rev)))
            p = _dot_nt((qh * e).astype(BF16), (key * e).astype(BF16))
            scores[i] = jnp.where(code == (-k if rev else k), p, scores[i])
    return [(o, sc.astype(BF16)) for o, sc in zip(o_inter, scores)]


def _hgrn_kernel(*refs, layer, n_chunks, zero_init):
    refs = list(refs)
    q_ref, ff_ref, fb_ref, v_ref, g_ref, lbl_ref, gain_ref, code_ref, tri_ref = refs[:9]
    s0_ref = None if zero_init else refs[9]
    o_ref, sfin_ref, st_ref, ob_ref, ab_ref, oi_ref, sc_ref = refs[-7:]
    c = HG_CHUNK
    code = code_ref[...]
    eye = code == 0
    tri_f, tri_b = tri_ref[0], tri_ref[1]

    def lower_bound(d):
        lg = lbl_ref[d]
        ex = jnp.exp(lg - jnp.max(lg, axis=0, keepdims=True))
        soft = ex / jnp.sum(ex, axis=0, keepdims=True)
        return jnp.sum(soft[:layer + 1], axis=0, keepdims=True) - soft[0:1]

    lb_f, lb_b = lower_bound(0), lower_bound(1)
    scale = HG_D ** -0.5

    def rows_of(n):
        return pl.ds(n * c if isinstance(n, int) else pl.multiple_of(n * c, c), c)

    for d in (0, 1):
        st_ref[d] = jnp.zeros((HG_D, HG_D), F32) if zero_init else s0_ref[d].T

    def gates(i):
        rf, rb = rows_of(i), rows_of(n_chunks - 1 - i)
        return _hg_gates([(q_ref[rf, :], ff_ref[rf, :], lb_f, tri_f),
                          (q_ref[rb, :], fb_ref[rb, :], lb_b, tri_b)], scale)

    def scores(i, ab):
        rf, rb = rows_of(i), rows_of(n_chunks - 1 - i)
        return _hg_scores([ab[0] + (v_ref[rf, :], st_ref.at[0], False),
                           ab[1] + (v_ref[rb, :], st_ref.at[1], True)], code, eye)

    def emit(i, oi_sc):
        rf, rb = rows_of(i), rows_of(n_chunks - 1 - i)
        o_ref[rf, :] = oi_sc[0][0] + _dot(oi_sc[0][1], v_ref[rf, :].astype(BF16))
        ob_ref[rb, :] = oi_sc[1][0] + _dot(oi_sc[1][1], v_ref[rb, :].astype(BF16))

    if n_chunks <= 2:
        ab = [gates(i) for i in range(n_chunks)]
        for i in range(n_chunks):
            emit(i, scores(i, ab[i]))
    else:
        def put_ab(ab):
            for ch in range(2):
                for k in range(3):
                    ab_ref[ch, k] = ab[ch][k]

        put_ab(gates(0))
        oi_ref[...] = jnp.zeros(oi_ref.shape, F32)
        sc_ref[...] = jnp.zeros(sc_ref.shape, BF16)

        def body(i, carry):
            emit(jnp.maximum(i - 1, 0), [(oi_ref[ch], sc_ref[ch]) for ch in range(2)])
            res = scores(i, [tuple(ab_ref[ch, k] for k in range(3)) for ch in range(2)])
            for ch in range(2):
                oi_ref[ch] = res[ch][0]
                sc_ref[ch] = res[ch][1]
            put_ab(gates(jnp.minimum(i + 1, n_chunks - 1)))
            return carry

        lax.fori_loop(0, n_chunks, body, 0)
        emit(n_chunks - 1, [(oi_ref[ch], sc_ref[ch]) for ch in range(2)])
    for d in (0, 1):
        sfin_ref[d] = st_ref[d].T

    def finish(n, carry):
        rows = pl.ds(pl.multiple_of(n * (2 * c), 2 * c), 2 * c)
        o_ref[rows, :] = _rms(o_ref[rows, :] + ob_ref[rows, :]) * gain_ref[...] * _silu(g_ref[rows, :])
        return carry

    lax.fori_loop(0, n_chunks // 2, finish, 0)


def _hgrn_call(proj, lb_logits, gain, state, layer, n_seq, seq_len):
    rows = proj.shape[0]
    zero_init = state is None

    def col_spec(k):
        return pl.BlockSpec((seq_len, HG_D), lambda b, h: (b, k * HG_HEADS + h))

    assert (seq_len // HG_CHUNK) % 2 == 0
    t, s = np.meshgrid(np.arange(HG_CHUNK), np.arange(HG_CHUNK), indexing="ij")
    lvl = np.where(t == s, 0, np.floor(np.log2(np.maximum(t ^ s, 1))).astype(np.int32) + 1)
    code = jnp.asarray(np.where(t > s, lvl, -lvl), jnp.int32)
    tri = jnp.asarray(np.stack([s <= t, s >= t]), BF16)

    in_specs = [col_spec(0), col_spec(1), col_spec(2), col_spec(3), col_spec(4),
                pl.BlockSpec((2, DEPTH, HG_D), lambda b, h: (0, 0, h)),
                pl.BlockSpec((1, HG_D), lambda b, h: (0, 0)),
                pl.BlockSpec((HG_CHUNK, HG_CHUNK), lambda b, h: (0, 0)),
                pl.BlockSpec((2, HG_CHUNK, HG_CHUNK), lambda b, h: (0, 0, 0))]
    args = [proj] * 5 + [lb_logits, gain, code, tri]
    if not zero_init:
        in_specs.append(pl.BlockSpec((None, None, 2, None, HG_D, HG_D), lambda b, h: (b, layer, 0, h, 0, 0)))
        args.append(state)
    return pl.pallas_call(
        functools.partial(_hgrn_kernel, layer=layer, n_chunks=seq_len // HG_CHUNK, zero_init=zero_init),
        grid=(n_seq, HG_HEADS),
        in_specs=in_specs,
        out_specs=[pl.BlockSpec((seq_len, HG_D), lambda b, h: (b, h)),
                   pl.BlockSpec((None, 2, None, HG_D, HG_D), lambda b, h: (b, 0, h, 0, 0))],
        out_shape=[jax.ShapeDtypeStruct((rows, HG_W), F32),
                   jax.ShapeDtypeStruct((n_seq, 2, HG_HEADS, HG_D, HG_D), F32)],
        scratch_shapes=[pltpu.VMEM((2, HG_D, HG_D), F32), pltpu.VMEM((seq_len, HG_D), F32),
                        pltpu.VMEM((2, 3, HG_CHUNK, HG_D), F32), pltpu.VMEM((2, HG_CHUNK, HG_D), F32),
                        pltpu.VMEM((2, HG_CHUNK, HG_CHUNK), BF16)],
        compiler_params=pltpu.CompilerParams(dimension_semantics=("parallel", "parallel"),
                                             vmem_limit_bytes=VMEM_LIMIT),
        name="hgrn2_mixer",
    )(*args)


def _s5_disc_kernel(lr_ref, li_ref, ldt_ref, ar_ref, ai_ref, zr_ref, zi_ref):
    lr = jnp.minimum(lr_ref[...], -1e-4)
    li = li_ref[...]
    dt = jnp.exp(ldt_ref[...])
    mag = jnp.exp(lr * dt)
    ab_re = mag * jnp.cos(li * dt)
    ab_im = mag * jnp.sin(li * dt)
    nr = ab_re - 1.0
    den = lr * lr + li * li
    ar_ref[...] = ab_re
    ai_ref[...] = ab_im
    zr_ref[...] = (nr * lr + ab_im * li) / den
    zi_ref[...] = (ab_im * lr - nr * li) / den


def _s5_bbar_kernel(zr_ref, zi_ref, br_ref, bi_ref, or_ref, oi_ref):
    zr, zi = zr_ref[...], zi_ref[...]
    br, bi = br_ref[...], bi_ref[...]
    or_ref[...] = zr * br - zi * bi
    oi_ref[...] = zr * bi + zi * br


def _s5_params(lam_re, lam_im, log_dt, b_re, b_im, c_re, c_im):
    n = DEPTH * 2 * S5_GROUPS
    shp = jax.ShapeDtypeStruct((n, S5_P), F32)
    ab_re, ab_im, z_re, z_im = pl.pallas_call(
        _s5_disc_kernel, out_shape=[shp] * 4, name="s5_discretise",
    )(lam_re.reshape(n, S5_P), lam_im.reshape(n, S5_P),
      jnp.broadcast_to(log_dt.reshape(n, 1), (n, S5_P)))
    wide = jax.ShapeDtypeStruct((n, S5_P * S5_CH), F32)
    bb_re, bb_im = pl.pallas_call(
        _s5_bbar_kernel, out_shape=[wide] * 2, name="s5_bbar",
    )(jnp.repeat(z_re, S5_CH, axis=1), jnp.repeat(z_im, S5_CH, axis=1),
      b_re.reshape(n, S5_P * S5_CH), b_im.reshape(n, S5_P * S5_CH))

    eye = jnp.eye(S5_GB, dtype=F32)
    lead = (DEPTH, 2, S5_NGB)

    def b_blockdiag(bb):
        bb = bb.reshape(lead + (S5_GB, S5_P, S5_CH))
        return jnp.einsum('ldbgpc,gh->ldbgchp', bb, eye).reshape(lead + (LANES, S5_SW))

    def c_blockdiag(cc):
        cc = cc.reshape(lead + (S5_GB, S5_CH, S5_P))
        return jnp.einsum('ldbgcp,gh->ldbhpgc', cc, eye).reshape(lead + (S5_SW, LANES))

    bmat = jnp.concatenate([b_blockdiag(bb_re), b_blockdiag(bb_im)], axis=-1).astype(BF16)
    cmat = jnp.concatenate([c_blockdiag(c_re), -c_blockdiag(c_im)], axis=-2).astype(BF16)
    a = jnp.concatenate([ab_re.reshape(lead + (1, S5_SW)), ab_im.reshape(lead + (1, S5_SW))], axis=-1)
    return a, bmat, cmat


def _s5_kernel(*refs, want_y, zero_init):
    refs = list(refs)
    u_ref, bm_ref, cm_ref, a_ref, d_ref = refs[:5]
    rest = refs[5:]
    h0_ref = None if zero_init else rest.pop(0)
    y_ref = rest.pop(0) if want_y else None
    hfin_ref, ubuf, hbuf, hst = rest
    ns, sw = S5_NSEQ, S5_SW
    n_tc = S5_SEQ // S5_TC

    for d in (0, 1):
        rev = d == 1
        ar = jnp.broadcast_to(a_ref[d, :, :sw], (ns, sw))
        ai = jnp.broadcast_to(a_ref[d, :, sw:], (ns, sw))
        hst[...] = jnp.zeros((ns, 2 * sw), F32) if zero_init else h0_ref[d]

        def chunk_body(i, carry, d=d, rev=rev, ar=ar, ai=ai):
            t0 = ((n_tc - 1 - i) if rev else i) * S5_TC
            for j in range(S5_TC):
                ubuf[j * ns:(j + 1) * ns, :] = u_ref[pl.ds(t0 + j, ns, stride=S5_SEQ), :]
            hbuf[...] = _dot(ubuf[...].astype(BF16), bm_ref[d])
            hr, hi = hst[:, :sw], hst[:, sw:]
            for j in (range(S5_TC - 1, -1, -1) if rev else range(S5_TC)):
                r = slice(j * ns, (j + 1) * ns)
                hr, hi = (ar * hr - ai * hi + hbuf[r, :sw], ar * hi + ai * hr + hbuf[r, sw:])
                hbuf[r, :sw] = hr
                hbuf[r, sw:] = hi
            hst[:, :sw] = hr
            hst[:, sw:] = hi
            if want_y:
                y = _dot(hbuf[...].astype(BF16), cm_ref[d])
                for j in range(S5_TC):
                    r = slice(j * ns, (j + 1) * ns)
                    dst = pl.ds(t0 + j, ns, stride=S5_SEQ)
                    if rev:
                        y_ref[dst, :] = y_ref[dst, :] + y[r, :]
                    else:
                        y_ref[dst, :] = y[r, :] + d_ref[...] * ubuf[r, :]
            return carry

        lax.fori_loop(0, n_tc, chunk_body, 0)
        hfin_ref[d] = hst[...]


def _s5_call(proj, a, bmat, cmat, dskip, h0, layer, want_y):
    rows = proj.shape[0]
    zero_init = h0 is None
    u_col0 = (IN_W - S5_W) // LANES
    in_specs = [
        pl.BlockSpec((rows, LANES), lambda g: (0, u_col0 + g)),
        pl.BlockSpec((None, 2, None, LANES, 2 * S5_SW), lambda g: (layer, 0, g, 0, 0)),
        pl.BlockSpec((None, 2, None, 2 * S5_SW, LANES), lambda g: (layer, 0, g, 0, 0)),
        pl.BlockSpec((None, 2, None, 1, 2 * S5_SW), lambda g: (layer, 0, g, 0, 0)),
        pl.BlockSpec((None, None, 1, LANES), lambda g: (layer, g, 0, 0)),
    ]
    args = [proj, bmat, cmat, a, dskip]
    state_spec = pl.BlockSpec((2, None, S5_NSEQ, 2 * S5_SW), lambda g: (0, g, 0, 0))
    state_shape = jax.ShapeDtypeStruct((2, S5_NGB, S5_NSEQ, 2 * S5_SW), F32)
    if not zero_init:
        in_specs.append(state_spec)
        args.append(h0)
    out_specs, out_shape = [], []
    if want_y:
        out_specs.append(pl.BlockSpec((rows, LANES), lambda g: (0, g)))
        out_shape.append(jax.ShapeDtypeStruct((rows, S5_W), F32))
    out_specs.append(state_spec)
    out_shape.append(state_shape)
    res = pl.pallas_call(
        functools.partial(_s5_kernel, want_y=want_y, zero_init=zero_init),
        grid=(S5_NGB,),
        in_specs=in_specs,
        out_specs=out_specs,
        out_shape=out_shape,
        scratch_shapes=[pltpu.VMEM((S5_TC * S5_NSEQ, LANES), F32),
                        pltpu.VMEM((S5_TC * S5_NSEQ, 2 * S5_SW), F32),
                        pltpu.VMEM((S5_NSEQ, 2 * S5_SW), F32)],
        compiler_params=pltpu.CompilerParams(dimension_semantics=("parallel",),
                                             vmem_limit_bytes=VMEM_LIMIT),
        name="s5_scan",
    )(*args)
    return (res[0], res[1]) if want_y else (None, res[0])


def _s5_chain_kernel(z_ref, s0_ref, a_ref, h0_ref, *, pieces):
    sw = S5_SW
    pr, pi = a_ref[0, :, :sw], a_ref[0, :, sw:]
    for _ in range(int(math.log2(S5_SEQ))):
        pr, pi = pr * pr - pi * pi, 2.0 * (pr * pi)
    rev = pl.program_id(0) == 1
    n_long = S5_NSEQ // pieces
    for b in range(n_long):
        for fwd_order in (True, False):
            @pl.when(rev != fwd_order)
            def _(b=b, fwd_order=fwd_order):
                hr, hi = s0_ref[0, b:b + 1, :sw], s0_ref[0, b:b + 1, sw:]
                for k in (range(pieces) if fwd_order else range(pieces - 1, -1, -1)):
                    r = b * pieces + k
                    h0_ref[0, r:r + 1, :sw] = hr
                    h0_ref[0, r:r + 1, sw:] = hi
                    zr, zi = z_ref[0, r:r + 1, :sw], z_ref[0, r:r + 1, sw:]
                    hr, hi = pr * hr - pi * hi + zr, pr * hi + pi * hr + zi


def _s5_chain_call(z, s0, a, layer, pieces):
    n_long = S5_NSEQ // pieces
    return pl.pallas_call(
        functools.partial(_s5_chain_kernel, pieces=pieces),
        grid=(2, S5_NGB),
        in_specs=[pl.BlockSpec((1, None, S5_NSEQ, 2 * S5_SW), lambda d, g: (d, g, 0, 0)),
                  pl.BlockSpec((1, None, n_long, 2 * S5_SW), lambda d, g: (d, g, 0, 0)),
                  pl.BlockSpec((None, 1, None, 1, 2 * S5_SW), lambda d, g: (layer, d, g, 0, 0))],
        out_specs=pl.BlockSpec((1, None, S5_NSEQ, 2 * S5_SW), lambda d, g: (d, g, 0, 0)),
        out_shape=jax.ShapeDtypeStruct((2, S5_NGB, S5_NSEQ, 2 * S5_SW), F32),
        name="s5_chain",
    )(z, s0, a)


def _out_kernel(x_ref, ohg_ref, y5_ref, g1_ref, sh2_ref, sc2_ref, g2_ref, nffn_ref, nfin_ref,
                wglu_ref, wout_ref, wg_ref, wu_ref, wd_ref, o_ref, *, final_norm):
    y = _gelu_tanh(y5_ref[...])
    y = y * _sigmoid(_dot(y.astype(BF16), wglu_ref[...]))
    mix = _dot(ohg_ref[...].astype(BF16), wout_ref[:HG_W, :]) + _dot(y.astype(BF16), wout_ref[HG_W:, :])
    x = x_ref[...] + g1_ref[...] * mix
    h = _rms(x) * nffn_ref[...]
    h = (h * (1.0 + sc2_ref[...]) + sh2_ref[...]).astype(BF16)
    act = (_silu(_dot(h, wg_ref[...])) * _dot(h, wu_ref[...])).astype(BF16)
    x = x + g2_ref[...] * _dot(act, wd_ref[...])
    if final_norm:
        x = _rms(x) * nfin_ref[...]
    o_ref[...] = x


def _out_call(x, ohg, y5, mod4, nffn, nfin, wglu, wout, wg, wu, wd, layer, cond0, rows_per_cond, final_norm):
    rows = x.shape[0]
    tm = ROW_TILE_OUT
    vec = pl.BlockSpec((1, D_MODEL), lambda i: (0, 0))
    return pl.pallas_call(
        functools.partial(_out_kernel, final_norm=final_norm),
        grid=(rows // tm,),
        in_specs=[pl.BlockSpec((tm, D_MODEL), lambda i: (i, 0)),
                  pl.BlockSpec((tm, HG_W), lambda i: (i, 0)),
                  pl.BlockSpec((tm, S5_W), lambda i: (i, 0)),
                  _mod_spec(layer, cond0, rows_per_cond, tm, 2),
                  _mod_spec(layer, cond0, rows_per_cond, tm, 3),
                  _mod_spec(layer, cond0, rows_per_cond, tm, 4),
                  _mod_spec(layer, cond0, rows_per_cond, tm, 5),
                  vec, vec,
                  _const_spec((S5_W, S5_W)), _const_spec((D_MODEL, D_MODEL)),
                  _const_spec((D_MODEL, D_FF)), _const_spec((D_MODEL, D_FF)), _const_spec((D_FF, D_MODEL))],
        out_specs=pl.BlockSpec((tm, D_MODEL), lambda i: (i, 0)),
        out_shape=jax.ShapeDtypeStruct((rows, D_MODEL), F32),
        compiler_params=pltpu.CompilerParams(dimension_semantics=("parallel",),
                                             vmem_limit_bytes=VMEM_LIMIT),
        name="out_ffn",
    )(x, ohg, y5, mod4, mod4, mod4, mod4, nffn, nfin, wglu, wout, wg, wu, wd)


def _grid_pos_embed(n_tokens, dim):
    t = jnp.arange(n_tokens)
    r = (t // GRID_W).astype(F32)
    col = (t % GRID_W).astype(F32)
    nf = dim // 4
    omega = 1.0 / (10000.0 ** (jnp.arange(nf, dtype=F32) / nf))

    def enc(p):
        a = p[:, None] * omega[None, :]
        return jnp.concatenate([jnp.sin(a), jnp.cos(a)], axis=-1)

    return jnp.concatenate([enc(r), enc(col)], axis=-1)


def _s5_state_to_blocks(s):
    n = s.shape[0]
    s = s.reshape(n, 2, S5_NGB, S5_GB, S5_P, 2)
    return jnp.transpose(s, (1, 2, 0, 5, 3, 4)).reshape(2, S5_NGB, n, 2 * S5_SW)


def _s5_blocks_to_state(h):
    n = h.shape[2]
    h = h.reshape(2, S5_NGB, n, 2, S5_GB, S5_P)
    return jnp.transpose(h, (2, 0, 1, 4, 5, 3)).reshape(n, 2, S5_GROUPS, S5_P, 2)


def kernel(x_prompt, x_sample, state_hgrn, state_s5, c, c_ctx, w_mod, b_mod, norm_mix, norm_ffn, norm_final, w_in, w_out, hg_lb_logits, hg_norm, s5_lam_re, s5_lam_im, s5_log_dt, s5_b_re, s5_b_im, s5_c_re, s5_c_im, s5_d, s5_w_glu, w_gate, w_up, w_down):
    n_ctx, ctx_len, _ = x_prompt.shape
    n_dec, dec_len, _ = x_sample.shape
    assert ctx_len == S5_SEQ and n_ctx == S5_NSEQ and n_dec * dec_len == S5_NSEQ * S5_SEQ

    cond = jnp.concatenate([c_ctx[None, :], c, jnp.zeros((SUBLANES - 1 - n_dec, D_MODEL), F32)], axis=0)
    mod4 = _mod_call(cond, w_mod, b_mod).reshape(DEPTH, SUBLANES, 1, 6 * D_MODEL)

    w_in_b, w_out_b = w_in.astype(BF16), w_out.astype(BF16)
    w_glu_b = s5_w_glu.astype(BF16)
    w_gate_b, w_up_b, w_down_b = w_gate.astype(BF16), w_up.astype(BF16), w_down.astype(BF16)
    s5_a, s5_bmat, s5_cmat = _s5_params(s5_lam_re, s5_lam_im, s5_log_dt, s5_b_re, s5_b_im, s5_c_re, s5_c_im)
    s5_dskip = s5_d.reshape(DEPTH, S5_NGB, 1, LANES)
    nfin = norm_final.reshape(1, D_MODEL)
    pos = _grid_pos_embed(dec_len, D_MODEL)

    def run(x, pos, n_seq, seq_len, cond0, hg_state, s5_state):
        rows = x.shape[0]
        rows_per_cond = rows if cond0 == 0 else seq_len
        pieces = seq_len // S5_SEQ
        hg_finals, s5_finals = [], []
        for l in range(DEPTH):
            proj, x = _in_call(x, pos if l == 0 else None, norm_mix[l].reshape(1, D_MODEL), mod4, w_in_b[l],
                               l, cond0, rows_per_cond)
            ohg, hg_fin = _hgrn_call(proj, hg_lb_logits, hg_norm[l].reshape(1, HG_D), hg_state, l, n_seq, seq_len)
            if s5_state is None:
                y5, s5_fin = _s5_call(proj, s5_a, s5_bmat, s5_cmat, s5_dskip, None, l, True)
            else:
                _, z = _s5_call(proj, s5_a, s5_bmat, s5_cmat, s5_dskip, None, l, False)
                h0 = _s5_chain_call(z, _s5_state_to_blocks(s5_state[:, l]), s5_a, l, pieces)
                y5, s5_fin = _s5_call(proj, s5_a, s5_bmat, s5_cmat, s5_dskip, h0, l, True)
            x = _out_call(x, ohg, y5, mod4, norm_ffn[l].reshape(1, D_MODEL), nfin, w_glu_b[l], w_out_b[l],
                          w_gate_b[l], w_up_b[l], w_down_b[l], l, cond0, rows_per_cond, l == DEPTH - 1)
            hg_finals.append(hg_fin)
            s5_finals.append(_s5_blocks_to_state(s5_fin))
        return x, hg_finals, s5_finals

    y_prompt, hg_finals, s5_finals = run(x_prompt.reshape(n_ctx * ctx_len, D_MODEL), None, n_ctx, ctx_len, 0, None, None)
    y_sample, _, _ = run(x_sample.reshape(n_dec * dec_len, D_MODEL), pos, n_dec, dec_len, 1, state_hgrn, state_s5)
    return (y_prompt.reshape(x_prompt.shape), y_sample.reshape(x_sample.shape),
            jnp.stack(hg_finals, axis=1), jnp.stack(s5_finals, axis=1))
```

```python
import functools
import math

import jax
import jax.numpy as jnp
import numpy as np
from jax import lax
from jax.experimental import pallas as pl
from jax.experimental.pallas import tpu as pltpu

F32 = jnp.float32
BF16 = jnp.bfloat16

LANES = 128
SUBLANES = 8

D_MODEL = 1024
DEPTH = 2
GRID_W = 64
HG_W = 512
HG_HEADS = 4
HG_D = HG_W // HG_HEADS
S5_W = 512
S5_CH = 16
S5_GROUPS = S5_W // S5_CH
S5_P = 64
S5_GB = LANES // S5_CH
S5_NGB = S5_GROUPS // S5_GB
S5_SW = S5_GB * S5_P
HG_IN_W = 5 * HG_W
IN_W = HG_IN_W + S5_W
D_FF = 2816
EPS = 1e-6

HG_CHUNK = 128
HG_LEVELS = (64, 32, 16, 8, 4, 2, 1)
S5_SEQ = 256
S5_NSEQ = 16
S5_TC = 32

TILE_S = SUBLANES
TILE_T = 32
TILE_ROWS = TILE_S * TILE_T
MOD_TILE_N = 1536
VMEM_LIMIT = 56 * 1024 * 1024


def _sigmoid(x):
    return 1.0 / (1.0 + jnp.exp(-x))


def _silu(x):
    return x * _sigmoid(x)


def _gelu_tanh(x):
    return 0.5 * x * (1.0 + jnp.tanh(math.sqrt(2.0 / math.pi) * (x + 0.044715 * (x * x * x))))


def _rms(x):
    return x * lax.rsqrt(jnp.mean(x * x, axis=-1, keepdims=True) + EPS)


def _dot(a, b):
    return jnp.dot(a, b, preferred_element_type=F32)


def _dot_nt(a, b):
    return lax.dot_general(a, b, (((1,), (1,)), ((), ())), preferred_element_type=F32)


def _dot_tn(a, b):
    return lax.dot_general(a, b, (((0,), (0,)), ((), ())), preferred_element_type=F32)


def _layer_spec(shape, layer):
    nd = len(shape)
    return pl.BlockSpec((None,) + tuple(shape), lambda *_: (layer,) + (0,) * nd, pipeline_mode=pl.Buffered(1))


def _mod_kernel(cond_ref, w_ref, b_ref, o_ref):
    a = _silu(cond_ref[...]).astype(BF16)
    o_ref[0] = _dot(a, w_ref[0].astype(BF16)) + b_ref[0]


def _mod_call(cond, w_mod, b_mod):
    n_cond = cond.shape[0]
    n_out = w_mod.shape[-1]
    return pl.pallas_call(
        _mod_kernel,
        grid=(DEPTH, n_out // MOD_TILE_N),
        in_specs=[
            pl.BlockSpec((n_cond, D_MODEL), lambda l, j: (0, 0)),
            pl.BlockSpec((1, D_MODEL, MOD_TILE_N), lambda l, j: (l, 0, j)),
            pl.BlockSpec((1, 1, MOD_TILE_N), lambda l, j: (l, 0, j)),
        ],
        out_specs=pl.BlockSpec((1, n_cond, MOD_TILE_N), lambda l, j: (l, 0, j)),
        out_shape=jax.ShapeDtypeStruct((DEPTH, n_cond, n_out), F32),
        compiler_params=pltpu.CompilerParams(dimension_semantics=("parallel", "parallel"),
                                             vmem_limit_bytes=VMEM_LIMIT),
        name="adaln_mod",
    )(cond, w_mod, b_mod.reshape(DEPTH, 1, n_out))


def _in_kernel(*refs, add_pos):
    if add_pos:
        x_ref, pos_ref, gain_ref, sh_ref, sc_ref, w_ref, proj_ref, u_ref, xs_ref = refs
        x = x_ref[...] + pos_ref[...]
        xs_ref[...] = x
    else:
        x_ref, gain_ref, sh_ref, sc_ref, w_ref, proj_ref, u_ref = refs
        x = x_ref[...]
    x = x.reshape(TILE_ROWS, D_MODEL)
    h = _rms(x) * gain_ref[...]
    h = (h * (1.0 + sc_ref[...]) + sh_ref[...]).astype(BF16)
    proj_ref[...] = _dot(h, w_ref[:, :HG_IN_W]).reshape(proj_ref.shape)
    u = _dot(h, w_ref[:, HG_IN_W:])
    for s in range(TILE_S):
        u_ref[:, s, :] = u[s * TILE_T:(s + 1) * TILE_T, :]


def _tile_spec(width):
    return pl.BlockSpec((TILE_S, TILE_T, width), lambda sb, tb: (sb, tb, 0))


def _tm_tile_spec(width):
    return pl.BlockSpec((TILE_T, TILE_S, width), lambda sb, tb: (tb, sb, 0))


def _mod_spec(layer, cond0, seqs_per_cond, col):
    return pl.BlockSpec((None, None, 1, D_MODEL),
                        lambda sb, tb: (layer, cond0 + (sb * TILE_S) // seqs_per_cond, 0, col))


def _in_call(x3, pos3, gain, mod4, w_in_b, layer, cond0, seqs_per_cond):
    n_pseq = x3.shape[0]
    add_pos = pos3 is not None
    in_specs = [_tile_spec(D_MODEL)]
    args = [x3]
    if add_pos:
        in_specs.append(pl.BlockSpec((TILE_S, TILE_T, D_MODEL), lambda sb, tb: (0, tb, 0)))
        args.append(pos3)
    in_specs += [
        pl.BlockSpec((1, D_MODEL), lambda sb, tb: (0, 0)),
        _mod_spec(layer, cond0, seqs_per_cond, 0),
        _mod_spec(layer, cond0, seqs_per_cond, 1),
        _layer_spec((D_MODEL, IN_W), layer),
    ]
    args += [gain, mod4, mod4, w_in_b]
    out_specs = [_tile_spec(HG_IN_W), _tm_tile_spec(S5_W)]
    out_shape = [jax.ShapeDtypeStruct((n_pseq, S5_SEQ, HG_IN_W), F32),
                 jax.ShapeDtypeStruct((S5_SEQ, n_pseq, S5_W), F32)]
    if add_pos:
        out_specs.append(_tile_spec(D_MODEL))
        out_shape.append(jax.ShapeDtypeStruct(x3.shape, F32))
    res = pl.pallas_call(
        functools.partial(_in_kernel, add_pos=add_pos),
        grid=(n_pseq // TILE_S, S5_SEQ // TILE_T),
        in_specs=in_specs,
        out_specs=out_specs,
        out_shape=out_shape,
        compiler_params=pltpu.CompilerParams(dimension_semantics=("parallel", "parallel"),
                                             vmem_limit_bytes=VMEM_LIMIT),
        name="in_proj",
    )(*args)
    return (res[0], res[1], res[2]) if add_pos else (res[0], res[1], x3)


def _pair_boundary(b, m, rev):
    c = b.shape[0]
    span = 2 * m
    at = m if rev else m - 1
    if span >= SUBLANES:
        b3 = b.reshape(c // span, span, LANES)
        return jnp.broadcast_to(b3[:, at:at + 1, :], b3.shape).reshape(c, LANES)
    b3 = b.reshape(c // SUBLANES, SUBLANES, LANES)
    sub = lax.broadcasted_iota(jnp.int32, b3.shape, 1)
    out = None
    for p in range(SUBLANES // span):
        piece = jnp.broadcast_to(b3[:, p * span + at:p * span + at + 1, :], b3.shape)
        out = piece if out is None else jnp.where(sub >= p * span, piece, out)
    return out.reshape(c, LANES)


def _neg_abs(x):
    bits = lax.bitcast_convert_type(x, jnp.uint32) | jnp.uint32(0x80000000)
    return lax.bitcast_convert_type(bits, F32)


def _hg_gates(chains, scale):
    outs = []
    for q, fl, lb, tri in chains:
        sig = _sigmoid(fl)
        logf = jnp.log2(lb + (1.0 - lb) * sig)
        key = (1.0 - lb) * (1.0 - sig)
        hi = logf.astype(BF16)
        r1 = logf - hi.astype(F32)
        mid = r1.astype(BF16)
        lo = (r1 - mid.astype(F32)).astype(BF16)
        parts = _dot(tri, jnp.concatenate([hi, mid, lo], axis=1))
        b2 = parts[:, :LANES] + parts[:, LANES:2 * LANES] + parts[:, 2 * LANES:]
        outs.append((_silu(q) * scale, key, b2))
    return outs


def _hg_scores(chains, code, eye):
    c = chains[0][0].shape[0]
    o_inter = []
    for qh, key, b2, v, st_ref, rev in chains:
        b_edge = b2[0:1, :] if rev else b2[c - 1:c, :]
        st = st_ref[...]
        o_inter.append(_dot_nt((qh * jnp.exp2(b2)).astype(BF16), st.astype(BF16)))
        k_end = key * jnp.exp2(b_edge - b2)
        st_ref[...] = jnp.exp2(b_edge) * st + _dot_tn(v.astype(BF16), k_end.astype(BF16))

    scores = [jnp.where(eye, jnp.sum(ch[0] * ch[1], axis=-1, keepdims=True), 0.0) for ch in chains]
    for m in HG_LEVELS:
        k = int(math.log2(m)) + 1
        for i, (qh, key, b2, v, st_ref, rev) in enumerate(chains):
            e = jnp.exp2(_neg_abs(b2 - _pair_boundary(b2, m, rev)))
            p = _dot_nt((qh * e).astype(BF16), (key * e).astype(BF16))
            scores[i] = jnp.where(code == (-k if rev else k), p, scores[i])
    return [(o, sc.astype(BF16)) for o, sc in zip(o_inter, scores)]


def _hgrn_kernel(*refs, layer, n_chunks, zero_init):
    refs = list(refs)
    q_ref, ff_ref, fb_ref, v_ref, g_ref, lbl_ref, gain_ref, code_ref, tri_ref = refs[:9]
    s0_ref = None if zero_init else refs[9]
    o_ref, sfin_ref, st_ref, ob_ref, ab_ref, oi_ref, sc_ref = refs[-7:]
    c = HG_CHUNK
    code = code_ref[...]
    eye = code == 0
    tri_f, tri_b = tri_ref[0], tri_ref[1]

    def lower_bound(d):
        lg = lbl_ref[d]
        ex = jnp.exp(lg - jnp.max(lg, axis=0, keepdims=True))
        soft = ex / jnp.sum(ex, axis=0, keepdims=True)
        return jnp.sum(soft[:layer + 1], axis=0, keepdims=True) - soft[0:1]

    lb_f, lb_b = lower_bound(0), lower_bound(1)
    scale = HG_D ** -0.5

    def rows_of(n):
        return pl.ds(n * c if isinstance(n, int) else pl.multiple_of(n * c, c), c)

    for d in (0, 1):
        st_ref[d] = jnp.zeros((HG_D, HG_D), F32) if zero_init else s0_ref[d].T

    def gates(i):
        rf, rb = rows_of(i), rows_of(n_chunks - 1 - i)
        return _hg_gates([(q_ref[rf, :], ff_ref[rf, :], lb_f, tri_f),
                          (q_ref[rb, :], fb_ref[rb, :], lb_b, tri_b)], scale)

    def scores(i, ab):
        rf, rb = rows_of(i), rows_of(n_chunks - 1 - i)
        return _hg_scores([ab[0] + (v_ref[rf, :], st_ref.at[0], False),
                           ab[1] + (v_ref[rb, :], st_ref.at[1], True)], code, eye)

    def emit(i, oi_sc):
        rf, rb = rows_of(i), rows_of(n_chunks - 1 - i)
        o_ref[rf, :] = oi_sc[0][0] + _dot(oi_sc[0][1], v_ref[rf, :].astype(BF16))
        ob_ref[rb, :] = oi_sc[1][0] + _dot(oi_sc[1][1], v_ref[rb, :].astype(BF16))

    if n_chunks <= 2:
        ab = [gates(i) for i in range(n_chunks)]
        for i in range(n_chunks):
            emit(i, scores(i, ab[i]))
    else:
        def put_ab(ab):
            for ch in range(2):
                for k in range(3):
                    ab_ref[ch, k] = ab[ch][k]

        put_ab(gates(0))
        oi_ref[...] = jnp.zeros(oi_ref.shape, F32)
        sc_ref[...] = jnp.zeros(sc_ref.shape, BF16)

        def body(i, carry):
            emit(jnp.maximum(i - 1, 0), [(oi_ref[ch], sc_ref[ch]) for ch in range(2)])
            res = scores(i, [tuple(ab_ref[ch, k] for k in range(3)) for ch in range(2)])
            for ch in range(2):
                oi_ref[ch] = res[ch][0]
                sc_ref[ch] = res[ch][1]
            put_ab(gates(jnp.minimum(i + 1, n_chunks - 1)))
            return carry

        lax.fori_loop(0, n_chunks, body, 0)
        emit(n_chunks - 1, [(oi_ref[ch], sc_ref[ch]) for ch in range(2)])
    for d in (0, 1):
        sfin_ref[d] = st_ref[d].T

    def finish(n, carry):
        rows = pl.ds(pl.multiple_of(n * (2 * c), 2 * c), 2 * c)
        o_ref[rows, :] = _rms(o_ref[rows, :] + ob_ref[rows, :]) * gain_ref[...] * _silu(g_ref[rows, :])
        return carry

    lax.fori_loop(0, n_chunks // 2, finish, 0)


def _hgrn_call(proj, lb_logits, gain, state, layer, n_seq, seq_len):
    rows = proj.shape[0]
    zero_init = state is None

    def col_spec(k):
        return pl.BlockSpec((seq_len, HG_D), lambda b, h: (b, k * HG_HEADS + h))

    assert (seq_len // HG_CHUNK) % 2 == 0
    t, s = np.meshgrid(np.arange(HG_CHUNK), np.arange(HG_CHUNK), indexing="ij")
    lvl = np.where(t == s, 0, np.floor(np.log2(np.maximum(t ^ s, 1))).astype(np.int32) + 1)
    code = jnp.asarray(np.where(t > s, lvl, -lvl), jnp.int32)
    tri = jnp.asarray(np.stack([s <= t, s >= t]), BF16)

    in_specs = [col_spec(0), col_spec(1), col_spec(2), col_spec(3), col_spec(4),
                pl.BlockSpec((2, DEPTH, HG_D), lambda b, h: (0, 0, h)),
                pl.BlockSpec((1, HG_D), lambda b, h: (0, 0)),
                pl.BlockSpec((HG_CHUNK, HG_CHUNK), lambda b, h: (0, 0)),
                pl.BlockSpec((2, HG_CHUNK, HG_CHUNK), lambda b, h: (0, 0, 0))]
    args = [proj] * 5 + [lb_logits, gain, code, tri]
    if not zero_init:
        in_specs.append(pl.BlockSpec((None, None, 2, None, HG_D, HG_D), lambda b, h: (b, layer, 0, h, 0, 0)))
        args.append(state)
    return pl.pallas_call(
        functools.partial(_hgrn_kernel, layer=layer, n_chunks=seq_len // HG_CHUNK, zero_init=zero_init),
        grid=(n_seq, HG_HEADS),
        in_specs=in_specs,
        out_specs=[pl.BlockSpec((seq_len, HG_D), lambda b, h: (b, h)),
                   pl.BlockSpec((None, 2, None, HG_D, HG_D), lambda b, h: (b, 0, h, 0, 0))],
        out_shape=[jax.ShapeDtypeStruct((rows, HG_W), F32),
                   jax.ShapeDtypeStruct((n_seq, 2, HG_HEADS, HG_D, HG_D), F32)],
        scratch_shapes=[pltpu.VMEM((2, HG_D, HG_D), F32), pltpu.VMEM((seq_len, HG_D), F32),
                        pltpu.VMEM((2, 3, HG_CHUNK, HG_D), F32), pltpu.VMEM((2, HG_CHUNK, HG_D), F32),
                        pltpu.VMEM((2, HG_CHUNK, HG_CHUNK), BF16)],
        compiler_params=pltpu.CompilerParams(dimension_semantics=("parallel", "parallel"),
                                             vmem_limit_bytes=VMEM_LIMIT),
        name="hgrn2_mixer",
    )(*args)


def _s5_disc_kernel(lr_ref, li_ref, ldt_ref, ar_ref, ai_ref, zr_ref, zi_ref):
    lr = jnp.minimum(lr_ref[...], -1e-4)
    li = li_ref[...]
    dt = jnp.exp(ldt_ref[...])
    mag = jnp.exp(lr * dt)
    ab_re = mag * jnp.cos(li * dt)
    ab_im = mag * jnp.sin(li * dt)
    nr = ab_re - 1.0
    den = lr * lr + li * li
    ar_ref[...] = ab_re
    ai_ref[...] = ab_im
    zr_ref[...] = (nr * lr + ab_im * li) / den
    zi_ref[...] = (ab_im * lr - nr * li) / den


def _s5_bbar_kernel(zr_ref, zi_ref, br_ref, bi_ref, or_ref, oi_ref):
    zr, zi = zr_ref[...], zi_ref[...]
    br, bi = br_ref[...], bi_ref[...]
    or_ref[...] = zr * br - zi * bi
    oi_ref[...] = zr * bi + zi * br


def _s5_params(lam_re, lam_im, log_dt, b_re, b_im, c_re, c_im):
    n = DEPTH * 2 * S5_GROUPS
    shp = jax.ShapeDtypeStruct((n, S5_P), F32)
    ab_re, ab_im, z_re, z_im = pl.pallas_call(
        _s5_disc_kernel, out_shape=[shp] * 4, name="s5_discretise",
    )(lam_re.reshape(n, S5_P), lam_im.reshape(n, S5_P),
      jnp.broadcast_to(log_dt.reshape(n, 1), (n, S5_P)))
    wide = jax.ShapeDtypeStruct((n, S5_P * S5_CH), F32)
    bb_re, bb_im = pl.pallas_call(
        _s5_bbar_kernel, out_shape=[wide] * 2, name="s5_bbar",
    )(jnp.repeat(z_re, S5_CH, axis=1), jnp.repeat(z_im, S5_CH, axis=1),
      b_re.reshape(n, S5_P * S5_CH), b_im.reshape(n, S5_P * S5_CH))

    eye = jnp.eye(S5_GB, dtype=F32)
    lead = (DEPTH, 2, S5_NGB)

    def b_blockdiag(bb):
        bb = bb.reshape(lead + (S5_GB, S5_P, S5_CH))
        return jnp.einsum('ldbgpc,gh->ldbgchp', bb, eye).reshape(lead + (LANES, S5_SW))

    def c_blockdiag(cc):
        cc = cc.reshape(lead + (S5_GB, S5_CH, S5_P))
        return jnp.einsum('ldbgcp,gh->ldbhpgc', cc, eye).reshape(lead + (S5_SW, LANES))

    bmat = jnp.concatenate([b_blockdiag(bb_re), b_blockdiag(bb_im)], axis=-1).astype(BF16)
    cmat = jnp.concatenate([c_blockdiag(c_re), -c_blockdiag(c_im)], axis=-2).astype(BF16)
    a = jnp.concatenate([ab_re.reshape(lead + (1, S5_SW)), ab_im.reshape(lead + (1, S5_SW))], axis=-1)
    return a, jnp.broadcast_to(a, lead + (S5_NSEQ, 2 * S5_SW)), bmat, cmat


def _s5_kernel(*refs, want_y, zero_init):
    refs = list(refs)
    u_ref, bm_ref, cm_ref, a_ref, d_ref = refs[:5]
    rest = refs[5:]
    h0_ref = None if zero_init else rest.pop(0)
    y_ref = rest.pop(0) if want_y else None
    hfin_ref, hbuf0, hbuf1, hb0, hb1, hst = rest
    hbufs, hb16s = (hbuf0, hbuf1), (hb0, hb1)
    ns, sw = S5_NSEQ, S5_SW
    n_tc = S5_SEQ // S5_TC
    blk = S5_TC * ns
    dirs = (0, 1)

    for d in dirs:
        hst[d] = jnp.zeros((ns, 2 * sw), F32) if zero_init else h0_ref[d]

    def steps_of(d, i):
        return pl.ds(((n_tc - 1 - i) if d else i) * S5_TC, S5_TC)

    def project(i, slot):
        for d in dirs:
            u = u_ref[steps_of(d, i)].reshape(blk, LANES)
            hbufs[slot][d] = _dot(u.astype(BF16), bm_ref[d])

    def scan(slot):
        h = [(hst[d, :, :sw], hst[d, :, sw:]) for d in dirs]
        for jj in range(S5_TC):
            for d in dirs:
                j = S5_TC - 1 - jj if d else jj
                r = slice(j * ns, (j + 1) * ns)
                ar, ai = a_ref[d, :, :sw], a_ref[d, :, sw:]
                hr, hi = h[d]
                h[d] = (ar * hr - ai * hi + hbufs[slot][d, r, :sw], ar * hi + ai * hr + hbufs[slot][d, r, sw:])
                if want_y:
                    hb16s[slot][d, r, :sw] = h[d][0].astype(BF16)
                    hb16s[slot][d, r, sw:] = h[d][1].astype(BF16)
        for d in dirs:
            hst[d, :, :sw] = h[d][0]
            hst[d, :, sw:] = h[d][1]

    def readout(i, slot):
        for d in dirs:
            steps = steps_of(d, i)
            y = _dot(hb16s[slot][d], cm_ref[d])
            y_ref[steps] = y_ref[steps] + y.reshape(S5_TC, ns, LANES)

    project(0, 0)
    if want_y:
        hb1[...] = jnp.zeros(hb1.shape, BF16)

        def skip(n, carry):
            steps = pl.ds(n * S5_TC, S5_TC)
            y_ref[steps] = d_ref[...] * u_ref[steps]
            return carry

        lax.fori_loop(0, n_tc, skip, 0)

    def body(k, carry):
        i = 2 * k
        if want_y:
            readout(jnp.maximum(i - 1, 0), 1)
        scan(0)
        project(i + 1, 1)
        if want_y:
            readout(i, 0)
        scan(1)
        project(jnp.minimum(i + 2, n_tc - 1), 0)
        return carry

    lax.fori_loop(0, n_tc // 2, body, 0)
    for d in dirs:
        hfin_ref[d] = hst[d]
    if want_y:
        readout(n_tc - 1, 1)


def _s5_call(u_tm, a, bmat, cmat, dskip, h0, layer, want_y):
    zero_init = h0 is None
    tm_spec = pl.BlockSpec((S5_SEQ, S5_NSEQ, LANES), lambda g: (0, 0, g))
    in_specs = [
        tm_spec,
        pl.BlockSpec((None, 2, None, LANES, 2 * S5_SW), lambda g: (layer, 0, g, 0, 0)),
        pl.BlockSpec((None, 2, None, 2 * S5_SW, LANES), lambda g: (layer, 0, g, 0, 0)),
        pl.BlockSpec((None, 2, None, S5_NSEQ, 2 * S5_SW), lambda g: (layer, 0, g, 0, 0)),
        pl.BlockSpec((None, None, 1, LANES), lambda g: (layer, g, 0, 0)),
    ]
    args = [u_tm, bmat, cmat, a, dskip]
    state_spec = pl.BlockSpec((2, None, S5_NSEQ, 2 * S5_SW), lambda g: (0, g, 0, 0))
    state_shape = jax.ShapeDtypeStruct((2, S5_NGB, S5_NSEQ, 2 * S5_SW), F32)
    if not zero_init:
        in_specs.append(state_spec)
        args.append(h0)
    out_specs, out_shape = [], []
    if want_y:
        out_specs.append(tm_spec)
        out_shape.append(jax.ShapeDtypeStruct(u_tm.shape, F32))
    out_specs.append(state_spec)
    out_shape.append(state_shape)
    res = pl.pallas_call(
        functools.partial(_s5_kernel, want_y=want_y, zero_init=zero_init),
        grid=(S5_NGB,),
        in_specs=in_specs,
        out_specs=out_specs,
        out_shape=out_shape,
        scratch_shapes=[pltpu.VMEM((2, S5_TC * S5_NSEQ, 2 * S5_SW), F32),
                        pltpu.VMEM((2, S5_TC * S5_NSEQ, 2 * S5_SW), F32),
                        pltpu.VMEM((2, S5_TC * S5_NSEQ, 2 * S5_SW), BF16),
                        pltpu.VMEM((2, S5_TC * S5_NSEQ, 2 * S5_SW), BF16),
                        pltpu.VMEM((2, S5_NSEQ, 2 * S5_SW), F32)],
        compiler_params=pltpu.CompilerParams(dimension_semantics=("parallel",),
                                             vmem_limit_bytes=VMEM_LIMIT),
        name="s5_scan",
    )(*args)
    return (res[0], res[1]) if want_y else (None, res[0])


def _s5_chain_kernel(z_ref, s0_ref, a_ref, h0_ref, *, pieces):
    sw = S5_SW
    pr, pi = a_ref[0, :, :sw], a_ref[0, :, sw:]
    for _ in range(int(math.log2(S5_SEQ))):
        pr, pi = pr * pr - pi * pi, 2.0 * (pr * pi)
    rev = pl.program_id(0) == 1
    n_long = S5_NSEQ // pieces
    for b in range(n_long):
        for fwd_order in (True, False):
            @pl.when(rev != fwd_order)
            def _(b=b, fwd_order=fwd_order):
                hr, hi = s0_ref[0, b:b + 1, :sw], s0_ref[0, b:b + 1, sw:]
                for k in (range(pieces) if fwd_order else range(pieces - 1, -1, -1)):
                    r = b * pieces + k
                    h0_ref[0, r:r + 1, :sw] = hr
                    h0_ref[0, r:r + 1, sw:] = hi
                    zr, zi = z_ref[0, r:r + 1, :sw], z_ref[0, r:r + 1, sw:]
                    hr, hi = pr * hr - pi * hi + zr, pr * hi + pi * hr + zi


def _s5_chain_call(z, s0, a, layer, pieces):
    n_long = S5_NSEQ // pieces
    return pl.pallas_call(
        functools.partial(_s5_chain_kernel, pieces=pieces),
        grid=(2, S5_NGB),
        in_specs=[pl.BlockSpec((1, None, S5_NSEQ, 2 * S5_SW), lambda d, g: (d, g, 0, 0)),
                  pl.BlockSpec((1, None, n_long, 2 * S5_SW), lambda d, g: (d, g, 0, 0)),
                  pl.BlockSpec((None, 1, None, 1, 2 * S5_SW), lambda d, g: (layer, d, g, 0, 0))],
        out_specs=pl.BlockSpec((1, None, S5_NSEQ, 2 * S5_SW), lambda d, g: (d, g, 0, 0)),
        out_shape=jax.ShapeDtypeStruct((2, S5_NGB, S5_NSEQ, 2 * S5_SW), F32),
        name="s5_chain",
    )(z, s0, a)


def _out_kernel(x_ref, ohg_ref, y5_ref, g1_ref, sh2_ref, sc2_ref, g2_ref, nffn_ref, nfin_ref,
                wglu_ref, wout_ref, wg_ref, wu_ref, wd_ref, o_ref, *, final_norm):
    y = jnp.concatenate([y5_ref[:, s, :] for s in range(TILE_S)], axis=0)
    y = _gelu_tanh(y)
    y = y * _sigmoid(_dot(y.astype(BF16), wglu_ref[...]))
    ohg = ohg_ref[...].reshape(TILE_ROWS, HG_W)
    mix = _dot(ohg.astype(BF16), wout_ref[:HG_W, :]) + _dot(y.astype(BF16), wout_ref[HG_W:, :])
    x = x_ref[...].reshape(TILE_ROWS, D_MODEL) + g1_ref[...] * mix
    h = _rms(x) * nffn_ref[...]
    h = (h * (1.0 + sc2_ref[...]) + sh2_ref[...]).astype(BF16)
    act = (_silu(_dot(h, wg_ref[...])) * _dot(h, wu_ref[...])).astype(BF16)
    x = x + g2_ref[...] * _dot(act, wd_ref[...])
    if final_norm:
        x = _rms(x) * nfin_ref[...]
    o_ref[...] = x.reshape(o_ref.shape)


def _out_call(x3, ohg3, y5_tm, mod4, nffn, nfin, wglu, wout, wg, wu, wd, layer, cond0, seqs_per_cond, final_norm):
    vec = pl.BlockSpec((1, D_MODEL), lambda sb, tb: (0, 0))
    return pl.pallas_call(
        functools.partial(_out_kernel, final_norm=final_norm),
        grid=(x3.shape[0] // TILE_S, S5_SEQ // TILE_T),
        in_specs=[_tile_spec(D_MODEL), _tile_spec(HG_W), _tm_tile_spec(S5_W),
                  _mod_spec(layer, cond0, seqs_per_cond, 2),
                  _mod_spec(layer, cond0, seqs_per_cond, 3),
                  _mod_spec(layer, cond0, seqs_per_cond, 4),
                  _mod_spec(layer, cond0, seqs_per_cond, 5),
                  vec, vec,
                  _layer_spec((S5_W, S5_W), layer), _layer_spec((D_MODEL, D_MODEL), layer),
                  _layer_spec((D_MODEL, D_FF), layer), _layer_spec((D_MODEL, D_FF), layer),
                  _layer_spec((D_FF, D_MODEL), layer)],
        out_specs=_tile_spec(D_MODEL),
        out_shape=jax.ShapeDtypeStruct(x3.shape, F32),
        compiler_params=pltpu.CompilerParams(dimension_semantics=("parallel", "parallel"),
                                             vmem_limit_bytes=VMEM_LIMIT),
        name="out_ffn",
    )(x3, ohg3, y5_tm, mod4, mod4, mod4, mod4, nffn, nfin, wglu, wout, wg, wu, wd)


def _grid_pos_embed(n_tokens, dim):
    t = jnp.arange(n_tokens)
    r = (t // GRID_W).astype(F32)
    col = (t % GRID_W).astype(F32)
    nf = dim // 4
    omega = 1.0 / (10000.0 ** (jnp.arange(nf, dtype=F32) / nf))

    def enc(p):
        a = p[:, None] * omega[None, :]
        return jnp.concatenate([jnp.sin(a), jnp.cos(a)], axis=-1)

    return jnp.concatenate([enc(r), enc(col)], axis=-1)


def _s5_state_to_blocks(s):
    n = s.shape[0]
    s = s.reshape(n, 2, S5_NGB, S5_GB, S5_P, 2)
    return jnp.transpose(s, (1, 2, 0, 5, 3, 4)).reshape(2, S5_NGB, n, 2 * S5_SW)


def _s5_blocks_to_state(h):
    n = h.shape[2]
    h = h.reshape(2, S5_NGB, n, 2, S5_GB, S5_P)
    return jnp.transpose(h, (2, 0, 1, 4, 5, 3)).reshape(n, 2, S5_GROUPS, S5_P, 2)


def kernel(x_prompt, x_sample, state_hgrn, state_s5, c, c_ctx, w_mod, b_mod, norm_mix, norm_ffn, norm_final, w_in, w_out, hg_lb_logits, hg_norm, s5_lam_re, s5_lam_im, s5_log_dt, s5_b_re, s5_b_im, s5_c_re, s5_c_im, s5_d, s5_w_glu, w_gate, w_up, w_down):
    n_ctx, ctx_len, _ = x_prompt.shape
    n_dec, dec_len, _ = x_sample.shape
    assert ctx_len == S5_SEQ and n_ctx == S5_NSEQ and n_dec * dec_len == S5_NSEQ * S5_SEQ

    cond = jnp.concatenate([c_ctx[None, :], c, jnp.zeros((SUBLANES - 1 - n_dec, D_MODEL), F32)], axis=0)
    mod4 = _mod_call(cond, w_mod, b_mod).reshape(DEPTH, SUBLANES, 1, 6 * D_MODEL)

    w_in_b, w_out_b = w_in.astype(BF16), w_out.astype(BF16)
    w_glu_b = s5_w_glu.astype(BF16)
    w_gate_b, w_up_b, w_down_b = w_gate.astype(BF16), w_up.astype(BF16), w_down.astype(BF16)
    s5_a, s5_a_rows, s5_bmat, s5_cmat = _s5_params(s5_lam_re, s5_lam_im, s5_log_dt, s5_b_re, s5_b_im, s5_c_re, s5_c_im)
    s5_dskip = s5_d.reshape(DEPTH, S5_NGB, 1, LANES)
    nfin = norm_final.reshape(1, D_MODEL)
    pos = _grid_pos_embed(dec_len, D_MODEL)

    def run(x3, pos3, n_seq, seq_len, cond0, hg_state, s5_state):
        pieces = seq_len // S5_SEQ
        seqs_per_cond = S5_NSEQ if cond0 == 0 else pieces
        rows = S5_NSEQ * S5_SEQ
        hg_finals, s5_finals = [], []
        for l in range(DEPTH):
            proj3, u_tm, x3 = _in_call(x3, pos3 if l == 0 else None, norm_mix[l].reshape(1, D_MODEL), mod4, w_in_b,
                                       l, cond0, seqs_per_cond)
            ohg, hg_fin = _hgrn_call(proj3.reshape(rows, HG_IN_W), hg_lb_logits, hg_norm[l].reshape(1, HG_D),
                                     hg_state, l, n_seq, seq_len)
            if s5_state is None:
                y5, s5_fin = _s5_call(u_tm, s5_a_rows, s5_bmat, s5_cmat, s5_dskip, None, l, True)
            else:
                _, z = _s5_call(u_tm, s5_a_rows, s5_bmat, s5_cmat, s5_dskip, None, l, False)
                h0 = _s5_chain_call(z, _s5_state_to_blocks(s5_state[:, l]), s5_a, l, pieces)
                y5, s5_fin = _s5_call(u_tm, s5_a_rows, s5_bmat, s5_cmat, s5_dskip, h0, l, True)
            x3 = _out_call(x3, ohg.reshape(S5_NSEQ, S5_SEQ, HG_W), y5, mod4, norm_ffn[l].reshape(1, D_MODEL), nfin,
                           w_glu_b, w_out_b, w_gate_b, w_up_b, w_down_b, l, cond0, seqs_per_cond, l == DEPTH - 1)
            hg_finals.append(hg_fin)
            s5_finals.append(_s5_blocks_to_state(s5_fin))
        return x3, hg_finals, s5_finals

    tok = (S5_NSEQ, S5_SEQ, D_MODEL)
    y_prompt, hg_finals, s5_finals = run(x_prompt.reshape(tok), None, n_ctx, ctx_len, 0, None, None)
    y_sample, _, _ = run(x_sample.reshape(tok), pos.reshape(dec_len // S5_SEQ, S5_SEQ, D_MODEL), n_dec, dec_len, 1,
                         state_hgrn, state_s5)
    return (y_prompt.reshape(x_prompt.shape), y_sample.reshape(x_sample.shape),
            jnp.stack(hg_finals, axis=1), jnp.stack(s5_finals, axis=1))
```

```python
import functools
import math

import jax
import jax.numpy as jnp
import numpy as np
from jax import lax
from jax.experimental import pallas as pl
from jax.experimental.pallas import tpu as pltpu

F32 = jnp.float32
BF16 = jnp.bfloat16

LANES = 128
SUBLANES = 8

D_MODEL = 1024
DEPTH = 2
GRID_W = 64
HG_W = 512
HG_HEADS = 4
HG_D = HG_W // HG_HEADS
S5_W = 512
S5_CH = 16
S5_GROUPS = S5_W // S5_CH
S5_P = 64
S5_GB = LANES // S5_CH
S5_NGB = S5_GROUPS // S5_GB
S5_SW = S5_GB * S5_P
HG_IN_W = 5 * HG_W
IN_W = HG_IN_W + S5_W
D_FF = 2816
EPS = 1e-6

HG_CHUNK = 128
HG_LEVELS = (64, 32, 16, 8, 4, 2, 1)
S5_SEQ = 256
S5_NSEQ = 16
S5_TC = 32

TILE_S = SUBLANES
TILE_T = 32
TILE_ROWS = TILE_S * TILE_T
CAST_ROWS = 128
MOD_TILE_N = 1536
VMEM_LIMIT = 56 * 1024 * 1024


def _sigmoid(x):
    return 1.0 / (1.0 + jnp.exp(-x))


def _silu(x):
    return x * _sigmoid(x)


def _gelu_tanh(x):
    return 0.5 * x * (1.0 + jnp.tanh(math.sqrt(2.0 / math.pi) * (x + 0.044715 * (x * x * x))))


def _rms(x):
    return x * lax.rsqrt(jnp.mean(x * x, axis=-1, keepdims=True) + EPS)


def _dot(a, b):
    return jnp.dot(a, b, preferred_element_type=F32)


def _dot_nt(a, b):
    return lax.dot_general(a, b, (((1,), (1,)), ((), ())), preferred_element_type=F32)


def _dot_tn(a, b):
    return lax.dot_general(a, b, (((0,), (0,)), ((), ())), preferred_element_type=F32)


def _layer_spec(shape, layer):
    nd = len(shape)
    return pl.BlockSpec((None,) + tuple(shape), lambda *_: (layer,) + (0,) * nd, pipeline_mode=pl.Buffered(1))


def _mod_kernel(cond_ref, w_ref, b_ref, o_ref):
    a = _silu(cond_ref[...]).astype(BF16)
    o_ref[0] = _dot(a, w_ref[0].astype(BF16)) + b_ref[0]


def _mod_call(cond, w_mod, b_mod):
    n_cond = cond.shape[0]
    n_out = w_mod.shape[-1]
    return pl.pallas_call(
        _mod_kernel,
        grid=(DEPTH, n_out // MOD_TILE_N),
        in_specs=[
            pl.BlockSpec((n_cond, D_MODEL), lambda l, j: (0, 0)),
            pl.BlockSpec((1, D_MODEL, MOD_TILE_N), lambda l, j: (l, 0, j)),
            pl.BlockSpec((1, 1, MOD_TILE_N), lambda l, j: (l, 0, j)),
        ],
        out_specs=pl.BlockSpec((1, n_cond, MOD_TILE_N), lambda l, j: (l, 0, j)),
        out_shape=jax.ShapeDtypeStruct((DEPTH, n_cond, n_out), F32),
        compiler_params=pltpu.CompilerParams(dimension_semantics=("parallel", "parallel"),
                                             vmem_limit_bytes=VMEM_LIMIT),
        name="adaln_mod",
    )(cond, w_mod, b_mod.reshape(DEPTH, 1, n_out))


def _first_step():
    return jnp.logical_and(pl.program_id(0) == 0, pl.program_id(1) == 0)


def _cast_rows(src_ref, dst_ref):
    for r in range(0, src_ref.shape[0], CAST_ROWS):
        dst_ref[r:r + CAST_ROWS, :] = src_ref[r:r + CAST_ROWS, :].astype(BF16)


def _in_kernel(*refs, add_pos):
    if add_pos:
        x_ref, pos_ref, gain_ref, sh_ref, sc_ref, w_ref, proj_ref, u_ref, xs_ref, wb_ref = refs
        x = x_ref[...] + pos_ref[...]
        xs_ref[...] = x
    else:
        x_ref, gain_ref, sh_ref, sc_ref, w_ref, proj_ref, u_ref, wb_ref = refs
        x = x_ref[...]

    @pl.when(_first_step())
    def _():
        _cast_rows(w_ref, wb_ref)

    x = x.reshape(TILE_ROWS, D_MODEL)
    h = _rms(x) * gain_ref[...]
    h = (h * (1.0 + sc_ref[...]) + sh_ref[...]).astype(BF16)
    proj_ref[...] = _dot(h, wb_ref[:, :HG_IN_W]).reshape(proj_ref.shape)
    u = _dot(h, wb_ref[:, HG_IN_W:])
    for s in range(TILE_S):
        u_ref[:, s, :] = u[s * TILE_T:(s + 1) * TILE_T, :]


def _tile_spec(width):
    return pl.BlockSpec((TILE_S, TILE_T, width), lambda sb, tb: (sb, tb, 0))


def _tm_tile_spec(width):
    return pl.BlockSpec((TILE_T, TILE_S, width), lambda sb, tb: (tb, sb, 0))


def _mod_spec(layer, cond0, seqs_per_cond, col):
    return pl.BlockSpec((None, None, 1, D_MODEL),
                        lambda sb, tb: (layer, cond0 + (sb * TILE_S) // seqs_per_cond, 0, col))


def _in_call(x3, pos3, gain, mod4, w_in_b, layer, cond0, seqs_per_cond):
    n_pseq = x3.shape[0]
    add_pos = pos3 is not None
    in_specs = [_tile_spec(D_MODEL)]
    args = [x3]
    if add_pos:
        in_specs.append(pl.BlockSpec((TILE_S, TILE_T, D_MODEL), lambda sb, tb: (0, tb, 0)))
        args.append(pos3)
    in_specs += [
        pl.BlockSpec((1, D_MODEL), lambda sb, tb: (0, 0)),
        _mod_spec(layer, cond0, seqs_per_cond, 0),
        _mod_spec(layer, cond0, seqs_per_cond, 1),
        _layer_spec((D_MODEL, IN_W), layer),
    ]
    args += [gain, mod4, mod4, w_in_b]
    out_specs = [_tile_spec(HG_IN_W), _tm_tile_spec(S5_W)]
    out_shape = [jax.ShapeDtypeStruct((n_pseq, S5_SEQ, HG_IN_W), F32),
                 jax.ShapeDtypeStruct((S5_SEQ, n_pseq, S5_W), F32)]
    if add_pos:
        out_specs.append(_tile_spec(D_MODEL))
        out_shape.append(jax.ShapeDtypeStruct(x3.shape, F32))
    res = pl.pallas_call(
        functools.partial(_in_kernel, add_pos=add_pos),
        grid=(n_pseq // TILE_S, S5_SEQ // TILE_T),
        in_specs=in_specs,
        out_specs=out_specs,
        out_shape=out_shape,
        scratch_shapes=[pltpu.VMEM((D_MODEL, IN_W), BF16)],
        compiler_params=pltpu.CompilerParams(dimension_semantics=("arbitrary", "arbitrary"),
                                             vmem_limit_bytes=VMEM_LIMIT),
        name="in_proj",
    )(*args)
    return (res[0], res[1], res[2]) if add_pos else (res[0], res[1], x3)


def _pair_boundary(b, m, rev):
    c = b.shape[0]
    span = 2 * m
    at = m if rev else m - 1
    if span >= SUBLANES:
        b3 = b.reshape(c // span, span, LANES)
        return jnp.broadcast_to(b3[:, at:at + 1, :], b3.shape).reshape(c, LANES)
    b3 = b.reshape(c // SUBLANES, SUBLANES, LANES)
    sub = lax.broadcasted_iota(jnp.int32, b3.shape, 1)
    out = None
    for p in range(SUBLANES // span):
        piece = jnp.broadcast_to(b3[:, p * span + at:p * span + at + 1, :], b3.shape)
        out = piece if out is None else jnp.where(sub >= p * span, piece, out)
    return out.reshape(c, LANES)


def _neg_abs(x):
    bits = lax.bitcast_convert_type(x, jnp.uint32) | jnp.uint32(0x80000000)
    return lax.bitcast_convert_type(bits, F32)


def _hg_gates(chains, scale):
    outs = []
    for q, fl, lb, tri in chains:
        sig = _sigmoid(fl)
        logf = jnp.log2(lb + (1.0 - lb) * sig)
        key = (1.0 - lb) * (1.0 - sig)
        hi = logf.astype(BF16)
        r1 = logf - hi.astype(F32)
        mid = r1.astype(BF16)
        lo = (r1 - mid.astype(F32)).astype(BF16)
        parts = _dot(tri, jnp.concatenate([hi, mid, lo], axis=1))
        b2 = parts[:, :LANES] + parts[:, LANES:2 * LANES] + parts[:, 2 * LANES:]
        outs.append((_silu(q) * scale, key, b2))
    return outs


def _hg_scores(chains, code, eye):
    c = chains[0][0].shape[0]
    o_inter = []
    for qh, key, b2, v, st_ref, rev in chains:
        b_edge = b2[0:1, :] if rev else b2[c - 1:c, :]
        st = st_ref[...]
        o_inter.append(_dot_nt((qh * jnp.exp2(b2)).astype(BF16), st.astype(BF16)))
        k_end = key * jnp.exp2(b_edge - b2)
        st_ref[...] = jnp.exp2(b_edge) * st + _dot_tn(v.astype(BF16), k_end.astype(BF16))

    scores = [jnp.where(eye, jnp.sum(ch[0] * ch[1], axis=-1, keepdims=True), 0.0) for ch in chains]
    for m in HG_LEVELS:
        k = int(math.log2(m)) + 1
        for i, (qh, key, b2, v, st_ref, rev) in enumerate(chains):
            e = jnp.exp2(_neg_abs(b2 - _pair_boundary(b2, m, rev)))
            p = _dot_nt((qh * e).astype(BF16), (key * e).astype(BF16))
            scores[i] = jnp.where(code == (-k if rev else k), p, scores[i])
    return [(o, sc.astype(BF16)) for o, sc in zip(o_inter, scores)]


def _hgrn_kernel(*refs, layer, n_chunks, zero_init):
    refs = list(refs)
    q_ref, ff_ref, fb_ref, v_ref, g_ref, lbl_ref, gain_ref, code_ref, tri_ref = refs[:9]
    s0_ref = None if zero_init else refs[9]
    o_ref, sfin_ref, st_ref, ob_ref, ab_ref, oi_ref, sc_ref = refs[-7:]
    c = HG_CHUNK
    code = code_ref[...]
    eye = code == 0
    tri_f, tri_b = tri_ref[0], tri_ref[1]

    def lower_bound(d):
        lg = lbl_ref[d]
        ex = jnp.exp(lg - jnp.max(lg, axis=0, keepdims=True))
        soft = ex / jnp.sum(ex, axis=0, keepdims=True)
        return jnp.sum(soft[:layer + 1], axis=0, keepdims=True) - soft[0:1]

    lb_f, lb_b = lower_bound(0), lower_bound(1)
    scale = HG_D ** -0.5

    def rows_of(n):
        return pl.ds(n * c if isinstance(n, int) else pl.multiple_of(n * c, c), c)

    for d in (0, 1):
        st_ref[d] = jnp.zeros((HG_D, HG_D), F32) if zero_init else s0_ref[d].T

    def gates(i):
        rf, rb = rows_of(i), rows_of(n_chunks - 1 - i)
        return _hg_gates([(q_ref[rf, :], ff_ref[rf, :], lb_f, tri_f),
                          (q_ref[rb, :], fb_ref[rb, :], lb_b, tri_b)], scale)

    def scores(i, ab):
        rf, rb = rows_of(i), rows_of(n_chunks - 1 - i)
        return _hg_scores([ab[0] + (v_ref[rf, :], st_ref.at[0], False),
                           ab[1] + (v_ref[rb, :], st_ref.at[1], True)], code, eye)

    def emit(i, oi_sc):
        rf, rb = rows_of(i), rows_of(n_chunks - 1 - i)
        o_ref[rf, :] = oi_sc[0][0] + _dot(oi_sc[0][1], v_ref[rf, :].astype(BF16))
        ob_ref[rb, :] = oi_sc[1][0] + _dot(oi_sc[1][1], v_ref[rb, :].astype(BF16))

    if n_chunks <= 2:
        ab = [gates(i) for i in range(n_chunks)]
        for i in range(n_chunks):
            emit(i, scores(i, ab[i]))
    else:
        def put_ab(ab):
            for ch in range(2):
                for k in range(3):
                    ab_ref[ch, k] = ab[ch][k]

        put_ab(gates(0))
        oi_ref[...] = jnp.zeros(oi_ref.shape, F32)
        sc_ref[...] = jnp.zeros(sc_ref.shape, BF16)

        def body(i, carry):
            emit(jnp.maximum(i - 1, 0), [(oi_ref[ch], sc_ref[ch]) for ch in range(2)])
            res = scores(i, [tuple(ab_ref[ch, k] for k in range(3)) for ch in range(2)])
            for ch in range(2):
                oi_ref[ch] = res[ch][0]
                sc_ref[ch] = res[ch][1]
            put_ab(gates(jnp.minimum(i + 1, n_chunks - 1)))
            return carry

        lax.fori_loop(0, n_chunks, body, 0)
        emit(n_chunks - 1, [(oi_ref[ch], sc_ref[ch]) for ch in range(2)])
    for d in (0, 1):
        sfin_ref[d] = st_ref[d].T

    def finish(n, carry):
        rows = pl.ds(pl.multiple_of(n * (2 * c), 2 * c), 2 * c)
        o_ref[rows, :] = _rms(o_ref[rows, :] + ob_ref[rows, :]) * gain_ref[...] * _silu(g_ref[rows, :])
        return carry

    lax.fori_loop(0, n_chunks // 2, finish, 0)


def _hgrn_call(proj, lb_logits, gain, state, layer, n_seq, seq_len):
    rows = proj.shape[0]
    zero_init = state is None

    def col_spec(k):
        return pl.BlockSpec((seq_len, HG_D), lambda b, h: (b, k * HG_HEADS + h))

    assert (seq_len // HG_CHUNK) % 2 == 0
    t, s = np.meshgrid(np.arange(HG_CHUNK), np.arange(HG_CHUNK), indexing="ij")
    lvl = np.where(t == s, 0, np.floor(np.log2(np.maximum(t ^ s, 1))).astype(np.int32) + 1)
    code = jnp.asarray(np.where(t > s, lvl, -lvl), jnp.int32)
    tri = jnp.asarray(np.stack([s <= t, s >= t]), BF16)

    in_specs = [col_spec(0), col_spec(1), col_spec(2), col_spec(3), col_spec(4),
                pl.BlockSpec((2, DEPTH, HG_D), lambda b, h: (0, 0, h)),
                pl.BlockSpec((1, HG_D), lambda b, h: (0, 0)),
                pl.BlockSpec((HG_CHUNK, HG_CHUNK), lambda b, h: (0, 0)),
                pl.BlockSpec((2, HG_CHUNK, HG_CHUNK), lambda b, h: (0, 0, 0))]
    args = [proj] * 5 + [lb_logits, gain, code, tri]
    if not zero_init:
        in_specs.append(pl.BlockSpec((None, None, 2, None, HG_D, HG_D), lambda b, h: (b, layer, 0, h, 0, 0)))
        args.append(state)
    return pl.pallas_call(
        functools.partial(_hgrn_kernel, layer=layer, n_chunks=seq_len // HG_CHUNK, zero_init=zero_init),
        grid=(n_seq, HG_HEADS),
        in_specs=in_specs,
        out_specs=[pl.BlockSpec((seq_len, HG_D), lambda b, h: (b, h)),
                   pl.BlockSpec((None, 2, None, HG_D, HG_D), lambda b, h: (b, 0, h, 0, 0))],
        out_shape=[jax.ShapeDtypeStruct((rows, HG_W), F32),
                   jax.ShapeDtypeStruct((n_seq, 2, HG_HEADS, HG_D, HG_D), F32)],
        scratch_shapes=[pltpu.VMEM((2, HG_D, HG_D), F32), pltpu.VMEM((seq_len, HG_D), F32),
                        pltpu.VMEM((2, 3, HG_CHUNK, HG_D), F32), pltpu.VMEM((2, HG_CHUNK, HG_D), F32),
                        pltpu.VMEM((2, HG_CHUNK, HG_CHUNK), BF16)],
        compiler_params=pltpu.CompilerParams(dimension_semantics=("parallel", "parallel"),
                                             vmem_limit_bytes=VMEM_LIMIT),
        name="hgrn2_mixer",
    )(*args)


def _s5_disc_kernel(lr_ref, li_ref, ldt_ref, ar_ref, ai_ref, zr_ref, zi_ref):
    lr = jnp.minimum(lr_ref[...], -1e-4)
    li = li_ref[...]
    dt = jnp.exp(ldt_ref[...])
    mag = jnp.exp(lr * dt)
    ab_re = mag * jnp.cos(li * dt)
    ab_im = mag * jnp.sin(li * dt)
    nr = ab_re - 1.0
    den = lr * lr + li * li
    ar_ref[...] = ab_re
    ai_ref[...] = ab_im
    zr_ref[...] = (nr * lr + ab_im * li) / den
    zi_ref[...] = (ab_im * lr - nr * li) / den


def _s5_bbar_kernel(zr_ref, zi_ref, br_ref, bi_ref, or_ref, oi_ref):
    zr, zi = zr_ref[...], zi_ref[...]
    br, bi = br_ref[...], bi_ref[...]
    or_ref[...] = zr * br - zi * bi
    oi_ref[...] = zr * bi + zi * br


def _s5_params(lam_re, lam_im, log_dt, b_re, b_im, c_re, c_im):
    n = DEPTH * 2 * S5_GROUPS
    shp = jax.ShapeDtypeStruct((n, S5_P), F32)
    ab_re, ab_im, z_re, z_im = pl.pallas_call(
        _s5_disc_kernel, out_shape=[shp] * 4, name="s5_discretise",
    )(lam_re.reshape(n, S5_P), lam_im.reshape(n, S5_P),
      jnp.broadcast_to(log_dt.reshape(n, 1), (n, S5_P)))
    wide = jax.ShapeDtypeStruct((n, S5_P * S5_CH), F32)
    bb_re, bb_im = pl.pallas_call(
        _s5_bbar_kernel, out_shape=[wide] * 2, name="s5_bbar",
    )(jnp.repeat(z_re, S5_CH, axis=1), jnp.repeat(z_im, S5_CH, axis=1),
      b_re.reshape(n, S5_P * S5_CH), b_im.reshape(n, S5_P * S5_CH))

    eye = jnp.eye(S5_GB, dtype=F32)
    lead = (DEPTH, 2, S5_NGB)

    def b_blockdiag(bb):
        bb = bb.reshape(lead + (S5_GB, S5_P, 1, S5_CH))
        return (bb * eye[:, None, :, None]).astype(BF16).reshape(lead + (S5_SW, LANES))

    def c_blockdiag(cc):
        cc = cc.reshape(lead + (S5_GB, S5_CH, 1, S5_P))
        return (cc * eye[:, None, :, None]).astype(BF16).reshape(lead + (LANES, S5_SW))

    bmat = jnp.concatenate([b_blockdiag(bb_re), b_blockdiag(bb_im)], axis=-2)
    cmat = jnp.concatenate([c_blockdiag(c_re), -c_blockdiag(c_im)], axis=-1)
    a = jnp.concatenate([ab_re.reshape(lead + (1, S5_SW)), ab_im.reshape(lead + (1, S5_SW))], axis=-1)
    return a, jnp.broadcast_to(a, lead + (S5_NSEQ, 2 * S5_SW)), bmat, cmat


def _s5_kernel(*refs, want_y, zero_init):
    refs = list(refs)
    u_ref, bm_ref, cm_ref, a_ref, d_ref = refs[:5]
    rest = refs[5:]
    h0_ref = None if zero_init else rest.pop(0)
    y_ref = rest.pop(0) if want_y else None
    hfin_ref, hbuf0, hbuf1, hb0, hb1, hst = rest
    hbufs, hb16s = (hbuf0, hbuf1), (hb0, hb1)
    ns, sw = S5_NSEQ, S5_SW
    n_tc = S5_SEQ // S5_TC
    blk = S5_TC * ns
    dirs = (0, 1)

    for d in dirs:
        hst[d] = jnp.zeros((ns, 2 * sw), F32) if zero_init else h0_ref[d]

    def steps_of(d, i):
        return pl.ds(((n_tc - 1 - i) if d else i) * S5_TC, S5_TC)

    def project(i, slot):
        for d in dirs:
            u = u_ref[steps_of(d, i)].reshape(blk, LANES)
            hbufs[slot][d] = _dot_nt(u.astype(BF16), bm_ref[d])

    def scan(slot):
        h = [(hst[d, :, :sw], hst[d, :, sw:]) for d in dirs]
        for jj in range(S5_TC):
            for d in dirs:
                j = S5_TC - 1 - jj if d else jj
                r = slice(j * ns, (j + 1) * ns)
                ar, ai = a_ref[d, :, :sw], a_ref[d, :, sw:]
                hr, hi = h[d]
                h[d] = (ar * hr - ai * hi + hbufs[slot][d, r, :sw], ar * hi + ai * hr + hbufs[slot][d, r, sw:])
                if want_y:
                    hb16s[slot][d, r, :sw] = h[d][0].astype(BF16)
                    hb16s[slot][d, r, sw:] = h[d][1].astype(BF16)
        for d in dirs:
            hst[d, :, :sw] = h[d][0]
            hst[d, :, sw:] = h[d][1]

    def readout(i, slot):
        for d in dirs:
            steps = steps_of(d, i)
            y = _dot_nt(hb16s[slot][d], cm_ref[d])
            y_ref[steps] = y_ref[steps] + y.reshape(S5_TC, ns, LANES)

    project(0, 0)
    if want_y:
        hb1[...] = jnp.zeros(hb1.shape, BF16)

        def skip(n, carry):
            steps = pl.ds(n * S5_TC, S5_TC)
            y_ref[steps] = d_ref[...] * u_ref[steps]
            return carry

        lax.fori_loop(0, n_tc, skip, 0)

    def body(k, carry):
        i = 2 * k
        if want_y:
            readout(jnp.maximum(i - 1, 0), 1)
        scan(0)
        project(i + 1, 1)
        if want_y:
            readout(i, 0)
        scan(1)
        project(jnp.minimum(i + 2, n_tc - 1), 0)
        return carry

    lax.fori_loop(0, n_tc // 2, body, 0)
    for d in dirs:
        hfin_ref[d] = hst[d]
    if want_y:
        readout(n_tc - 1, 1)


def _s5_call(u_tm, a, bmat, cmat, dskip, h0, layer, want_y):
    zero_init = h0 is None
    tm_spec = pl.BlockSpec((S5_SEQ, S5_NSEQ, LANES), lambda g: (0, 0, g))
    in_specs = [
        tm_spec,
        pl.BlockSpec((None, 2, None, 2 * S5_SW, LANES), lambda g: (layer, 0, g, 0, 0)),
        pl.BlockSpec((None, 2, None, LANES, 2 * S5_SW), lambda g: (layer, 0, g, 0, 0)),
        pl.BlockSpec((None, 2, None, S5_NSEQ, 2 * S5_SW), lambda g: (layer, 0, g, 0, 0)),
        pl.BlockSpec((None, None, 1, LANES), lambda g: (layer, g, 0, 0)),
    ]
    args = [u_tm, bmat, cmat, a, dskip]
    state_spec = pl.BlockSpec((2, None, S5_NSEQ, 2 * S5_SW), lambda g: (0, g, 0, 0))
    state_shape = jax.ShapeDtypeStruct((2, S5_NGB, S5_NSEQ, 2 * S5_SW), F32)
    if not zero_init:
        in_specs.append(state_spec)
        args.append(h0)
    out_specs, out_shape = [], []
    if want_y:
        out_specs.append(tm_spec)
        out_shape.append(jax.ShapeDtypeStruct(u_tm.shape, F32))
    out_specs.append(state_spec)
    out_shape.append(state_shape)
    res = pl.pallas_call(
        functools.partial(_s5_kernel, want_y=want_y, zero_init=zero_init),
        grid=(S5_NGB,),
        in_specs=in_specs,
        out_specs=out_specs,
        out_shape=out_shape,
        scratch_shapes=[pltpu.VMEM((2, S5_TC * S5_NSEQ, 2 * S5_SW), F32),
                        pltpu.VMEM((2, S5_TC * S5_NSEQ, 2 * S5_SW), F32),
                        pltpu.VMEM((2, S5_TC * S5_NSEQ, 2 * S5_SW), BF16),
                        pltpu.VMEM((2, S5_TC * S5_NSEQ, 2 * S5_SW), BF16),
                        pltpu.VMEM((2, S5_NSEQ, 2 * S5_SW), F32)],
        compiler_params=pltpu.CompilerParams(dimension_semantics=("parallel",),
                                             vmem_limit_bytes=VMEM_LIMIT),
        name="s5_scan",
    )(*args)
    return (res[0], res[1]) if want_y else (None, res[0])


def _s5_chain_kernel(z_ref, s0_ref, a_ref, h0_ref, *, pieces):
    sw = S5_SW
    pr, pi = a_ref[0, :, :sw], a_ref[0, :, sw:]
    for _ in range(int(math.log2(S5_SEQ))):
        pr, pi = pr * pr - pi * pi, 2.0 * (pr * pi)
    rev = pl.program_id(0) == 1
    n_long = S5_NSEQ // pieces
    for b in range(n_long):
        for fwd_order in (True, False):
            @pl.when(rev != fwd_order)
            def _(b=b, fwd_order=fwd_order):
                hr, hi = s0_ref[0, b:b + 1, :sw], s0_ref[0, b:b + 1, sw:]
                for k in (range(pieces) if fwd_order else range(pieces - 1, -1, -1)):
                    r = b * pieces + k
                    h0_ref[0, r:r + 1, :sw] = hr
                    h0_ref[0, r:r + 1, sw:] = hi
                    zr, zi = z_ref[0, r:r + 1, :sw], z_ref[0, r:r + 1, sw:]
                    hr, hi = pr * hr - pi * hi + zr, pr * hi + pi * hr + zi


def _s5_chain_call(z, s0, a, layer, pieces):
    n_long = S5_NSEQ // pieces
    return pl.pallas_call(
        functools.partial(_s5_chain_kernel, pieces=pieces),
        grid=(2, S5_NGB),
        in_specs=[pl.BlockSpec((1, None, S5_NSEQ, 2 * S5_SW), lambda d, g: (d, g, 0, 0)),
                  pl.BlockSpec((1, None, n_long, 2 * S5_SW), lambda d, g: (d, g, 0, 0)),
                  pl.BlockSpec((None, 1, None, 1, 2 * S5_SW), lambda d, g: (layer, d, g, 0, 0))],
        out_specs=pl.BlockSpec((1, None, S5_NSEQ, 2 * S5_SW), lambda d, g: (d, g, 0, 0)),
        out_shape=jax.ShapeDtypeStruct((2, S5_NGB, S5_NSEQ, 2 * S5_SW), F32),
        name="s5_chain",
    )(z, s0, a)


def _out_kernel(x_ref, ohg_ref, y5_ref, g1_ref, sh2_ref, sc2_ref, g2_ref, nffn_ref, nfin_ref,
                wglu_ref, wout_ref, wg_ref, wu_ref, wd_ref, o_ref, wglu_b, wout_b, *, final_norm):
    @pl.when(_first_step())
    def _():
        _cast_rows(wglu_ref, wglu_b)
        _cast_rows(wout_ref, wout_b)

    y = jnp.concatenate([y5_ref[:, s, :] for s in range(TILE_S)], axis=0)
    y = _gelu_tanh(y)
    y = y * _sigmoid(_dot(y.astype(BF16), wglu_b[...]))
    ohg = ohg_ref[...].reshape(TILE_ROWS, HG_W)
    mix = _dot(ohg.astype(BF16), wout_b[:HG_W, :]) + _dot(y.astype(BF16), wout_b[HG_W:, :])
    x = x_ref[...].reshape(TILE_ROWS, D_MODEL) + g1_ref[...] * mix
    h = _rms(x) * nffn_ref[...]
    h = (h * (1.0 + sc2_ref[...]) + sh2_ref[...]).astype(BF16)
    act = (_silu(_dot(h, wg_ref[...])) * _dot(h, wu_ref[...])).astype(BF16)
    x = x + g2_ref[...] * _dot(act, wd_ref[...])
    if final_norm:
        x = _rms(x) * nfin_ref[...]
    o_ref[...] = x.reshape(o_ref.shape)


def _out_call(x3, ohg3, y5_tm, mod4, nffn, nfin, wglu, wout, wg, wu, wd, layer, cond0, seqs_per_cond, final_norm):
    vec = pl.BlockSpec((1, D_MODEL), lambda sb, tb: (0, 0))
    return pl.pallas_call(
        functools.partial(_out_kernel, final_norm=final_norm),
        grid=(x3.shape[0] // TILE_S, S5_SEQ // TILE_T),
        in_specs=[_tile_spec(D_MODEL), _tile_spec(HG_W), _tm_tile_spec(S5_W),
                  _mod_spec(layer, cond0, seqs_per_cond, 2),
                  _mod_spec(layer, cond0, seqs_per_cond, 3),
                  _mod_spec(layer, cond0, seqs_per_cond, 4),
                  _mod_spec(layer, cond0, seqs_per_cond, 5),
                  vec, vec,
                  _layer_spec((S5_W, S5_W), layer), _layer_spec((D_MODEL, D_MODEL), layer),
                  _layer_spec((D_MODEL, D_FF), layer), _layer_spec((D_MODEL, D_FF), layer),
                  _layer_spec((D_FF, D_MODEL), layer)],
        out_specs=_tile_spec(D_MODEL),
        out_shape=jax.ShapeDtypeStruct(x3.shape, F32),
        scratch_shapes=[pltpu.VMEM((S5_W, S5_W), BF16), pltpu.VMEM((D_MODEL, D_MODEL), BF16)],
        compiler_params=pltpu.CompilerParams(dimension_semantics=("arbitrary", "arbitrary"),
                                             vmem_limit_bytes=VMEM_LIMIT),
        name="out_ffn",
    )(x3, ohg3, y5_tm, mod4, mod4, mod4, mod4, nffn, nfin, wglu, wout, wg, wu, wd)


def _grid_pos_embed(n_tokens, dim):
    t = np.arange(n_tokens)
    r = (t // GRID_W).astype(np.float32)
    col = (t % GRID_W).astype(np.float32)
    nf = dim // 4
    omega = (1.0 / (np.float32(10000.0) ** (np.arange(nf, dtype=np.float32) / np.float32(nf)))).astype(np.float32)

    def enc(p):
        a = p[:, None] * omega[None, :]
        return np.concatenate([np.sin(a), np.cos(a)], axis=-1)

    return jnp.asarray(np.concatenate([enc(r), enc(col)], axis=-1), F32)


def _s5_state_to_blocks(s):
    n = s.shape[0]
    s = s.reshape(n, 2, S5_NGB, S5_GB, S5_P, 2)
    return jnp.transpose(s, (1, 2, 0, 5, 3, 4)).reshape(2, S5_NGB, n, 2 * S5_SW)


def _s5_blocks_to_state(h):
    n = h.shape[2]
    h = h.reshape(2, S5_NGB, n, 2, S5_GB, S5_P)
    return jnp.transpose(h, (2, 0, 1, 4, 5, 3)).reshape(n, 2, S5_GROUPS, S5_P, 2)


def kernel(x_prompt, x_sample, state_hgrn, state_s5, c, c_ctx, w_mod, b_mod, norm_mix, norm_ffn, norm_final, w_in, w_out, hg_lb_logits, hg_norm, s5_lam_re, s5_lam_im, s5_log_dt, s5_b_re, s5_b_im, s5_c_re, s5_c_im, s5_d, s5_w_glu, w_gate, w_up, w_down):
    n_ctx, ctx_len, _ = x_prompt.shape
    n_dec, dec_len, _ = x_sample.shape
    assert ctx_len == S5_SEQ and n_ctx == S5_NSEQ and n_dec * dec_len == S5_NSEQ * S5_SEQ

    cond = jnp.concatenate([c_ctx[None, :], c, jnp.zeros((SUBLANES - 1 - n_dec, D_MODEL), F32)], axis=0)
    mod4 = _mod_call(cond, w_mod, b_mod).reshape(DEPTH, SUBLANES, 1, 6 * D_MODEL)

    w_gate_b, w_up_b, w_down_b = w_gate.astype(BF16), w_up.astype(BF16), w_down.astype(BF16)
    s5_a, s5_a_rows, s5_bmat, s5_cmat = _s5_params(s5_lam_re, s5_lam_im, s5_log_dt, s5_b_re, s5_b_im, s5_c_re, s5_c_im)
    s5_dskip = s5_d.reshape(DEPTH, S5_NGB, 1, LANES)
    nfin = norm_final.reshape(1, D_MODEL)
    pos = _grid_pos_embed(dec_len, D_MODEL)

    def run(x3, pos3, n_seq, seq_len, cond0, hg_state, s5_state):
        pieces = seq_len // S5_SEQ
        seqs_per_cond = S5_NSEQ if cond0 == 0 else pieces
        rows = S5_NSEQ * S5_SEQ
        hg_finals, s5_finals = [], []
        for l in range(DEPTH):
            proj3, u_tm, x3 = _in_call(x3, pos3 if l == 0 else None, norm_mix[l].reshape(1, D_MODEL), mod4, w_in,
                                       l, cond0, seqs_per_cond)
            ohg, hg_fin = _hgrn_call(proj3.reshape(rows, HG_IN_W), hg_lb_logits, hg_norm[l].reshape(1, HG_D),
                                     hg_state, l, n_seq, seq_len)
            if s5_state is None:
                y5, s5_fin = _s5_call(u_tm, s5_a_rows, s5_bmat, s5_cmat, s5_dskip, None, l, True)
            else:
                _, z = _s5_call(u_tm, s5_a_rows, s5_bmat, s5_cmat, s5_dskip, None, l, False)
                h0 = _s5_chain_call(z, _s5_state_to_blocks(s5_state[:, l]), s5_a, l, pieces)
                y5, s5_fin = _s5_call(u_tm, s5_a_rows, s5_bmat, s5_cmat, s5_dskip, h0, l, True)
            x3 = _out_call(x3, ohg.reshape(S5_NSEQ, S5_SEQ, HG_W), y5, mod4, norm_ffn[l].reshape(1, D_MODEL), nfin,
                           s5_w_glu, w_out, w_gate_b, w_up_b, w_down_b, l, cond0, seqs_per_cond, l == DEPTH - 1)
            hg_finals.append(hg_fin)
            s5_finals.append(_s5_blocks_to_state(s5_fin))
        return x3, hg_finals, s5_finals

    tok = (S5_NSEQ, S5_SEQ, D_MODEL)
    y_prompt, hg_finals, s5_finals = run(x_prompt.reshape(tok), None, n_ctx, ctx_len, 0, None, None)
    y_sample, _, _ = run(x_sample.reshape(tok), pos.reshape(dec_len // S5_SEQ, S5_SEQ, D_MODEL), n_dec, dec_len, 1,
                         state_hgrn, state_s5)
    return (y_prompt.reshape(x_prompt.shape), y_sample.reshape(x_sample.shape),
            jnp.stack(hg_finals, axis=1), jnp.stack(s5_finals, axis=1))
```

```python
import functools
import math

import jax
import jax.numpy as jnp
import numpy as np
from jax import lax
from jax.experimental import pallas as pl
from jax.experimental.pallas import tpu as pltpu

F32 = jnp.float32
BF16 = jnp.bfloat16

LANES = 128
SUBLANES = 8

D_MODEL = 1024
DEPTH = 2
GRID_W = 64
HG_W = 512
HG_HEADS = 4
HG_D = HG_W // HG_HEADS
S5_W = 512
S5_CH = 16
S5_GROUPS = S5_W // S5_CH
S5_P = 64
S5_GB = LANES // S5_CH
S5_NGB = S5_GROUPS // S5_GB
S5_SW = S5_GB * S5_P
HG_IN_W = 5 * HG_W
IN_W = HG_IN_W + S5_W
D_FF = 2816
EPS = 1e-6

HG_CHUNK = 128
HG_LEVELS = (64, 32, 16, 8, 4, 2, 1)
S5_SEQ = 256
S5_NSEQ = 16
S5_TC = 32

TILE_S = SUBLANES
TILE_T = 32
TILE_ROWS = TILE_S * TILE_T
CAST_ROWS = 128
MOD_TILE_N = 1536
VMEM_LIMIT = 56 * 1024 * 1024


def _sigmoid(x):
    return 1.0 / (1.0 + jnp.exp(-x))


def _silu(x):
    return x * _sigmoid(x)


def _gelu_tanh(x):
    return 0.5 * x * (1.0 + jnp.tanh(math.sqrt(2.0 / math.pi) * (x + 0.044715 * (x * x * x))))


def _rms(x):
    return x * lax.rsqrt(jnp.mean(x * x, axis=-1, keepdims=True) + EPS)


def _dot(a, b):
    return jnp.dot(a, b, preferred_element_type=F32)


def _dot_nt(a, b):
    return lax.dot_general(a, b, (((1,), (1,)), ((), ())), preferred_element_type=F32)


def _dot_tn(a, b):
    return lax.dot_general(a, b, (((0,), (0,)), ((), ())), preferred_element_type=F32)


def _layer_spec(shape, layer):
    nd = len(shape)
    return pl.BlockSpec((None,) + tuple(shape), lambda *_: (layer,) + (0,) * nd, pipeline_mode=pl.Buffered(1))


def _mod_kernel(cond_ref, w_ref, b_ref, o_ref):
    a = _silu(cond_ref[...]).astype(BF16)
    o_ref[0] = _dot(a, w_ref[0].astype(BF16)) + b_ref[0]


def _mod_call(cond, w_mod, b_mod):
    n_cond = cond.shape[0]
    n_out = w_mod.shape[-1]
    return pl.pallas_call(
        _mod_kernel,
        grid=(DEPTH, n_out // MOD_TILE_N),
        in_specs=[
            pl.BlockSpec((n_cond, D_MODEL), lambda l, j: (0, 0)),
            pl.BlockSpec((1, D_MODEL, MOD_TILE_N), lambda l, j: (l, 0, j)),
            pl.BlockSpec((1, 1, MOD_TILE_N), lambda l, j: (l, 0, j)),
        ],
        out_specs=pl.BlockSpec((1, n_cond, MOD_TILE_N), lambda l, j: (l, 0, j)),
        out_shape=jax.ShapeDtypeStruct((DEPTH, n_cond, n_out), F32),
        compiler_params=pltpu.CompilerParams(dimension_semantics=("parallel", "parallel"),
                                             vmem_limit_bytes=VMEM_LIMIT),
        name="adaln_mod",
    )(cond, w_mod, b_mod.reshape(DEPTH, 1, n_out))


def _first_step():
    return jnp.logical_and(pl.program_id(0) == 0, pl.program_id(1) == 0)


def _cast_rows(src_ref, dst_ref):
    for r in range(0, src_ref.shape[0], CAST_ROWS):
        dst_ref[r:r + CAST_ROWS, :] = src_ref[r:r + CAST_ROWS, :].astype(BF16)


def _grid_pos_tile(omega, tb):
    nf = omega.shape[-1]
    s_idx = lax.broadcasted_iota(jnp.int32, (TILE_S, nf), 0)
    j_idx = lax.broadcasted_iota(jnp.int32, (TILE_T, nf), 0)
    t0 = tb * TILE_T
    row = (s_idx * (S5_SEQ // GRID_W) + t0 // GRID_W).astype(F32) * omega
    col = (j_idx + t0 % GRID_W).astype(F32) * omega
    enc_r = jnp.concatenate([jnp.sin(row), jnp.cos(row)], axis=-1)
    enc_c = jnp.concatenate([jnp.sin(col), jnp.cos(col)], axis=-1)
    shape = (TILE_S, TILE_T, 2 * nf)
    return jnp.concatenate([jnp.broadcast_to(enc_r[:, None, :], shape),
                            jnp.broadcast_to(enc_c[None, :, :], shape)], axis=-1)


def _in_kernel(*refs, add_pos):
    if add_pos:
        x_ref, om_ref, gain_ref, sh_ref, sc_ref, w_ref, proj_ref, u_ref, xs_ref, wb_ref = refs
        x = x_ref[...] + _grid_pos_tile(om_ref[...], pl.program_id(1))
        xs_ref[...] = x
    else:
        x_ref, gain_ref, sh_ref, sc_ref, w_ref, proj_ref, u_ref, wb_ref = refs
        x = x_ref[...]

    @pl.when(_first_step())
    def _():
        _cast_rows(w_ref, wb_ref)

    x = x.reshape(TILE_ROWS, D_MODEL)
    h = _rms(x) * gain_ref[...]
    h = (h * (1.0 + sc_ref[...]) + sh_ref[...]).astype(BF16)
    proj_ref[...] = _dot(h, wb_ref[:, :HG_IN_W]).reshape(proj_ref.shape)
    u = _dot(h, wb_ref[:, HG_IN_W:])
    for s in range(TILE_S):
        u_ref[:, s, :] = u[s * TILE_T:(s + 1) * TILE_T, :]


def _tile_spec(width):
    return pl.BlockSpec((TILE_S, TILE_T, width), lambda sb, tb: (sb, tb, 0))


def _tm_tile_spec(width):
    return pl.BlockSpec((TILE_T, TILE_S, width), lambda sb, tb: (tb, sb, 0))


def _mod_spec(layer, cond0, seqs_per_cond, col):
    return pl.BlockSpec((None, None, 1, D_MODEL),
                        lambda sb, tb: (layer, cond0 + (sb * TILE_S) // seqs_per_cond, 0, col))


def _in_call(x3, add_pos, gain, mod4, w_in_b, layer, cond0, seqs_per_cond):
    n_pseq = x3.shape[0]
    in_specs = [_tile_spec(D_MODEL)]
    args = [x3]
    if add_pos:
        assert seqs_per_cond == TILE_S and GRID_W % TILE_T == 0 and S5_SEQ % GRID_W == 0
        nf = D_MODEL // 4
        omega = 1.0 / (np.float32(10000.0) ** (np.arange(nf, dtype=np.float32) / np.float32(nf)))
        in_specs.append(pl.BlockSpec((1, nf), lambda sb, tb: (0, 0)))
        args.append(jnp.asarray(omega.reshape(1, nf), F32))
    in_specs += [
        pl.BlockSpec((1, D_MODEL), lambda sb, tb: (0, 0)),
        _mod_spec(layer, cond0, seqs_per_cond, 0),
        _mod_spec(layer, cond0, seqs_per_cond, 1),
        _layer_spec((D_MODEL, IN_W), layer),
    ]
    args += [gain, mod4, mod4, w_in_b]
    out_specs = [_tile_spec(HG_IN_W), _tm_tile_spec(S5_W)]
    out_shape = [jax.ShapeDtypeStruct((n_pseq, S5_SEQ, HG_IN_W), F32),
                 jax.ShapeDtypeStruct((S5_SEQ, n_pseq, S5_W), F32)]
    if add_pos:
        out_specs.append(_tile_spec(D_MODEL))
        out_shape.append(jax.ShapeDtypeStruct(x3.shape, F32))
    res = pl.pallas_call(
        functools.partial(_in_kernel, add_pos=add_pos),
        grid=(n_pseq // TILE_S, S5_SEQ // TILE_T),
        in_specs=in_specs,
        out_specs=out_specs,
        out_shape=out_shape,
        scratch_shapes=[pltpu.VMEM((D_MODEL, IN_W), BF16)],
        compiler_params=pltpu.CompilerParams(dimension_semantics=("arbitrary", "arbitrary"),
                                             vmem_limit_bytes=VMEM_LIMIT),
        name="in_proj",
    )(*args)
    return (res[0], res[1], res[2]) if add_pos else (res[0], res[1], x3)


def _pair_boundary(b, m, rev):
    c = b.shape[0]
    span = 2 * m
    at = m if rev else m - 1
    if span >= SUBLANES:
        b3 = b.reshape(c // span, span, LANES)
        return jnp.broadcast_to(b3[:, at:at + 1, :], b3.shape).reshape(c, LANES)
    b3 = b.reshape(c // SUBLANES, SUBLANES, LANES)
    sub = lax.broadcasted_iota(jnp.int32, b3.shape, 1)
    out = None
    for p in range(SUBLANES // span):
        piece = jnp.broadcast_to(b3[:, p * span + at:p * span + at + 1, :], b3.shape)
        out = piece if out is None else jnp.where(sub >= p * span, piece, out)
    return out.reshape(c, LANES)


def _neg_abs(x):
    bits = lax.bitcast_convert_type(x, jnp.uint32) | jnp.uint32(0x80000000)
    return lax.bitcast_convert_type(bits, F32)


def _hg_gates(chains, scale):
    outs = []
    for q, fl, lb, tri in chains:
        sig = _sigmoid(fl)
        logf = jnp.log2(lb + (1.0 - lb) * sig)
        key = (1.0 - lb) * (1.0 - sig)
        hi = logf.astype(BF16)
        r1 = logf - hi.astype(F32)
        mid = r1.astype(BF16)
        lo = (r1 - mid.astype(F32)).astype(BF16)
        parts = _dot(tri, jnp.concatenate([hi, mid, lo], axis=1))
        b2 = parts[:, :LANES] + parts[:, LANES:2 * LANES] + parts[:, 2 * LANES:]
        outs.append((_silu(q) * scale, key, b2))
    return outs


def _hg_scores(chains, code, eye):
    c = chains[0][0].shape[0]
    o_inter = []
    for qh, key, b2, v, st_ref, rev in chains:
        b_edge = b2[0:1, :] if rev else b2[c - 1:c, :]
        st = st_ref[...]
        o_inter.append(_dot_nt((qh * jnp.exp2(b2)).astype(BF16), st.astype(BF16)))
        k_end = key * jnp.exp2(b_edge - b2)
        st_ref[...] = jnp.exp2(b_edge) * st + _dot_tn(v.astype(BF16), k_end.astype(BF16))

    scores = [jnp.where(eye, jnp.sum(ch[0] * ch[1], axis=-1, keepdims=True), 0.0) for ch in chains]
    for m in HG_LEVELS:
        k = int(math.log2(m)) + 1
        for i, (qh, key, b2, v, st_ref, rev) in enumerate(chains):
            e = jnp.exp2(_neg_abs(b2 - _pair_boundary(b2, m, rev)))
            p = _dot_nt((qh * e).astype(BF16), (key * e).astype(BF16))
            scores[i] = jnp.where(code == (-k if rev else k), p, scores[i])
    return [(o, sc.astype(BF16)) for o, sc in zip(o_inter, scores)]


def _hgrn_kernel(*refs, layer, n_chunks, zero_init):
    refs = list(refs)
    q_ref, ff_ref, fb_ref, v_ref, g_ref, lbl_ref, gain_ref, code_ref, tri_ref = refs[:9]
    s0_ref = None if zero_init else refs[9]
    o_ref, sfin_ref, st_ref, ob_ref, ab_ref, oi_ref, sc_ref = refs[-7:]
    c = HG_CHUNK
    code = code_ref[...]
    eye = code == 0
    tri_f, tri_b = tri_ref[0], tri_ref[1]

    def lower_bound(d):
        lg = lbl_ref[d]
        ex = jnp.exp(lg - jnp.max(lg, axis=0, keepdims=True))
        soft = ex / jnp.sum(ex, axis=0, keepdims=True)
        return jnp.sum(soft[:layer + 1], axis=0, keepdims=True) - soft[0:1]

    lb_f, lb_b = lower_bound(0), lower_bound(1)
    scale = HG_D ** -0.5

    def rows_of(n):
        return pl.ds(n * c if isinstance(n, int) else pl.multiple_of(n * c, c), c)

    for d in (0, 1):
        st_ref[d] = jnp.zeros((HG_D, HG_D), F32) if zero_init else s0_ref[d].T

    def gates(i):
        rf, rb = rows_of(i), rows_of(n_chunks - 1 - i)
        return _hg_gates([(q_ref[rf, :], ff_ref[rf, :], lb_f, tri_f),
                          (q_ref[rb, :], fb_ref[rb, :], lb_b, tri_b)], scale)

    def scores(i, ab):
        rf, rb = rows_of(i), rows_of(n_chunks - 1 - i)
        return _hg_scores([ab[0] + (v_ref[rf, :], st_ref.at[0], False),
                           ab[1] + (v_ref[rb, :], st_ref.at[1], True)], code, eye)

    def emit(i, oi_sc):
        rf, rb = rows_of(i), rows_of(n_chunks - 1 - i)
        o_ref[rf, :] = oi_sc[0][0] + _dot(oi_sc[0][1], v_ref[rf, :].astype(BF16))
        ob_ref[rb, :] = oi_sc[1][0] + _dot(oi_sc[1][1], v_ref[rb, :].astype(BF16))

    if n_chunks <= 2:
        ab = [gates(i) for i in range(n_chunks)]
        for i in range(n_chunks):
            emit(i, scores(i, ab[i]))
    else:
        def put_ab(ab):
            for ch in range(2):
                for k in range(3):
                    ab_ref[ch, k] = ab[ch][k]

        put_ab(gates(0))
        oi_ref[...] = jnp.zeros(oi_ref.shape, F32)
        sc_ref[...] = jnp.zeros(sc_ref.shape, BF16)

        def body(i, carry):
            emit(jnp.maximum(i - 1, 0), [(oi_ref[ch], sc_ref[ch]) for ch in range(2)])
            res = scores(i, [tuple(ab_ref[ch, k] for k in range(3)) for ch in range(2)])
            for ch in range(2):
                oi_ref[ch] = res[ch][0]
                sc_ref[ch] = res[ch][1]
            put_ab(gates(jnp.minimum(i + 1, n_chunks - 1)))
            return carry

        lax.fori_loop(0, n_chunks, body, 0)
        emit(n_chunks - 1, [(oi_ref[ch], sc_ref[ch]) for ch in range(2)])
    for d in (0, 1):
        sfin_ref[d] = st_ref[d].T

    def finish(n, carry):
        rows = pl.ds(pl.multiple_of(n * (2 * c), 2 * c), 2 * c)
        o_ref[rows, :] = _rms(o_ref[rows, :] + ob_ref[rows, :]) * gain_ref[...] * _silu(g_ref[rows, :])
        return carry

    lax.fori_loop(0, n_chunks // 2, finish, 0)


def _hgrn_call(proj, lb_logits, gain, state, finals, layer, n_seq, seq_len):
    rows = proj.shape[0]
    zero_init = state is None

    def col_spec(k):
        return pl.BlockSpec((seq_len, HG_D), lambda b, h: (b, k * HG_HEADS + h))

    assert (seq_len // HG_CHUNK) % 2 == 0
    t, s = np.meshgrid(np.arange(HG_CHUNK), np.arange(HG_CHUNK), indexing="ij")
    lvl = np.where(t == s, 0, np.floor(np.log2(np.maximum(t ^ s, 1))).astype(np.int32) + 1)
    code = jnp.asarray(np.where(t > s, lvl, -lvl), jnp.int32)
    tri = jnp.asarray(np.stack([s <= t, s >= t]), BF16)

    in_specs = [col_spec(0), col_spec(1), col_spec(2), col_spec(3), col_spec(4),
                pl.BlockSpec((2, DEPTH, HG_D), lambda b, h: (0, 0, h)),
                pl.BlockSpec((1, HG_D), lambda b, h: (0, 0)),
                pl.BlockSpec((HG_CHUNK, HG_CHUNK), lambda b, h: (0, 0)),
                pl.BlockSpec((2, HG_CHUNK, HG_CHUNK), lambda b, h: (0, 0, 0))]
    args = [proj] * 5 + [lb_logits, gain, code, tri]
    if not zero_init:
        in_specs.append(pl.BlockSpec((None, None, 2, None, HG_D, HG_D), lambda b, h: (b, layer, 0, h, 0, 0)))
        args.append(state)
    aliases = {}
    if finals is not None:
        aliases[len(args)] = 1
        in_specs.append(pl.BlockSpec(memory_space=pl.ANY))
        args.append(finals)
    return pl.pallas_call(
        functools.partial(_hgrn_kernel, layer=layer, n_chunks=seq_len // HG_CHUNK, zero_init=zero_init),
        grid=(n_seq, HG_HEADS),
        in_specs=in_specs,
        out_specs=[pl.BlockSpec((seq_len, HG_D), lambda b, h: (b, h)),
                   pl.BlockSpec((None, None, 2, None, HG_D, HG_D), lambda b, h: (b, layer, 0, h, 0, 0))],
        out_shape=[jax.ShapeDtypeStruct((rows, HG_W), F32),
                   jax.ShapeDtypeStruct((n_seq, DEPTH, 2, HG_HEADS, HG_D, HG_D), F32)],
        input_output_aliases=aliases,
        scratch_shapes=[pltpu.VMEM((2, HG_D, HG_D), F32), pltpu.VMEM((seq_len, HG_D), F32),
                        pltpu.VMEM((2, 3, HG_CHUNK, HG_D), F32), pltpu.VMEM((2, HG_CHUNK, HG_D), F32),
                        pltpu.VMEM((2, HG_CHUNK, HG_CHUNK), BF16)],
        compiler_params=pltpu.CompilerParams(dimension_semantics=("parallel", "parallel"),
                                             vmem_limit_bytes=VMEM_LIMIT),
        name="hgrn2_mixer",
    )(*args)


def _s5_params_kernel(lr_ref, li_ref, ldt_ref, btr_ref, bti_ref, cr_ref, ci_ref, a_ref, bm_ref, cm_ref):
    sw = S5_SW
    lr = jnp.minimum(lr_ref[...], -1e-4)
    li = li_ref[...]
    dt = jnp.exp(ldt_ref[...])
    mag = jnp.exp(lr * dt)
    ab_re = mag * jnp.cos(li * dt)
    ab_im = mag * jnp.sin(li * dt)
    nr = ab_re - 1.0
    den = lr * lr + li * li
    z_re = (nr * lr + ab_im * li) / den
    z_im = (ab_im * lr - nr * li) / den

    p_idx = lax.broadcasted_iota(jnp.int32, (S5_P, sw), 0)
    col = lax.broadcasted_iota(jnp.int32, (S5_P, sw), 1)
    for g in range(S5_GB):
        place = (col == p_idx + g * S5_P).astype(BF16)
        zr, zi = z_re[g:g + 1, :], z_im[g:g + 1, :]
        btr, bti = btr_ref[g], bti_ref[g]
        rows = slice(g * S5_CH, (g + 1) * S5_CH)
        bm_ref[rows, :sw] = _dot((zr * btr - zi * bti).astype(BF16), place).astype(BF16)
        bm_ref[rows, sw:] = _dot((zr * bti + zi * btr).astype(BF16), place).astype(BF16)
        cm_ref[rows, :sw] = _dot(cr_ref[g].astype(BF16), place).astype(BF16)
        cm_ref[rows, sw:] = _dot((-ci_ref[g]).astype(BF16), place).astype(BF16)
        a_ref[:, g * S5_P:(g + 1) * S5_P] = jnp.broadcast_to(ab_re[g:g + 1, :], (S5_NSEQ, S5_P))
        a_ref[:, sw + g * S5_P:sw + (g + 1) * S5_P] = jnp.broadcast_to(ab_im[g:g + 1, :], (S5_NSEQ, S5_P))


def _s5_params(lam_re, lam_im, log_dt, b_re, b_im, c_re, c_im):
    nb = DEPTH * 2 * S5_NGB
    gp = (nb, S5_GB, S5_P)
    gcp = (nb, S5_GB, S5_CH, S5_P)
    bt_re = jnp.swapaxes(b_re, -1, -2).reshape(gcp)
    bt_im = jnp.swapaxes(b_im, -1, -2).reshape(gcp)
    ldt = jnp.broadcast_to(log_dt.reshape(nb, S5_GB, 1), gp)
    gp_spec = pl.BlockSpec((None, S5_GB, S5_P), lambda i: (i, 0, 0))
    gcp_spec = pl.BlockSpec((None, S5_GB, S5_CH, S5_P), lambda i: (i, 0, 0, 0))
    a, bmat, cmat = pl.pallas_call(
        _s5_params_kernel,
        grid=(nb,),
        in_specs=[gp_spec] * 3 + [gcp_spec] * 4,
        out_specs=[pl.BlockSpec((None, S5_NSEQ, 2 * S5_SW), lambda i: (i, 0, 0)),
                   pl.BlockSpec((None, LANES, 2 * S5_SW), lambda i: (i, 0, 0)),
                   pl.BlockSpec((None, LANES, 2 * S5_SW), lambda i: (i, 0, 0))],
        out_shape=[jax.ShapeDtypeStruct((nb, S5_NSEQ, 2 * S5_SW), F32),
                   jax.ShapeDtypeStruct((nb, LANES, 2 * S5_SW), BF16),
                   jax.ShapeDtypeStruct((nb, LANES, 2 * S5_SW), BF16)],
        compiler_params=pltpu.CompilerParams(dimension_semantics=("parallel",)),
        name="s5_params",
    )(lam_re.reshape(gp), lam_im.reshape(gp), ldt, bt_re, bt_im, c_re.reshape(gcp), c_im.reshape(gcp))
    lead = (DEPTH, 2, S5_NGB)
    return (a.reshape(lead + a.shape[1:]), bmat.reshape(lead + bmat.shape[1:]), cmat.reshape(lead + cmat.shape[1:]))


def _s5_kernel(*refs, want_y, zero_init):
    refs = list(refs)
    u_ref, bm_ref, cm_ref, a_ref, d_ref = refs[:5]
    rest = refs[5:]
    h0_ref = None if zero_init else rest.pop(0)
    y_ref = rest.pop(0) if want_y else None
    hfin_ref, hbuf0, hbuf1, hb0, hb1, hst = rest
    hbufs, hb16s = (hbuf0, hbuf1), (hb0, hb1)
    ns, sw = S5_NSEQ, S5_SW
    n_tc = S5_SEQ // S5_TC
    blk = S5_TC * ns
    dirs = (0, 1)

    for d in dirs:
        hst[d] = jnp.zeros((ns, 2 * sw), F32) if zero_init else h0_ref[d]

    def steps_of(d, i):
        return pl.ds(((n_tc - 1 - i) if d else i) * S5_TC, S5_TC)

    def project(i, slot):
        for d in dirs:
            u = u_ref[steps_of(d, i)].reshape(blk, LANES)
            hbufs[slot][d] = _dot(u.astype(BF16), bm_ref[d])

    def scan(slot):
        h = [(hst[d, :, :sw], hst[d, :, sw:]) for d in dirs]
        for jj in range(S5_TC):
            for d in dirs:
                j = S5_TC - 1 - jj if d else jj
                r = slice(j * ns, (j + 1) * ns)
                ar, ai = a_ref[d, :, :sw], a_ref[d, :, sw:]
                hr, hi = h[d]
                h[d] = (ar * hr - ai * hi + hbufs[slot][d, r, :sw], ar * hi + ai * hr + hbufs[slot][d, r, sw:])
                if want_y:
                    hb16s[slot][d, r, :sw] = h[d][0].astype(BF16)
                    hb16s[slot][d, r, sw:] = h[d][1].astype(BF16)
        for d in dirs:
            hst[d, :, :sw] = h[d][0]
            hst[d, :, sw:] = h[d][1]

    def readout(i, slot):
        for d in dirs:
            steps = steps_of(d, i)
            y = _dot_nt(hb16s[slot][d], cm_ref[d])
            y_ref[steps] = y_ref[steps] + y.reshape(S5_TC, ns, LANES)

    project(0, 0)
    if want_y:
        hb1[...] = jnp.zeros(hb1.shape, BF16)

        def skip(n, carry):
            steps = pl.ds(n * S5_TC, S5_TC)
            y_ref[steps] = d_ref[...] * u_ref[steps]
            return carry

        lax.fori_loop(0, n_tc, skip, 0)

    def body(k, carry):
        i = 2 * k
        if want_y:
            readout(jnp.maximum(i - 1, 0), 1)
        scan(0)
        project(i + 1, 1)
        if want_y:
            readout(i, 0)
        scan(1)
        project(jnp.minimum(i + 2, n_tc - 1), 0)
        return carry

    lax.fori_loop(0, n_tc // 2, body, 0)
    for d in dirs:
        hfin_ref[d] = hst[d]
    if want_y:
        readout(n_tc - 1, 1)


def _s5_call(u_tm, a, bmat, cmat, dskip, h0, layer, want_y):
    zero_init = h0 is None
    tm_spec = pl.BlockSpec((S5_SEQ, S5_NSEQ, LANES), lambda g: (0, 0, g))
    in_specs = [
        tm_spec,
        pl.BlockSpec((None, 2, None, LANES, 2 * S5_SW), lambda g: (layer, 0, g, 0, 0)),
        pl.BlockSpec((None, 2, None, LANES, 2 * S5_SW), lambda g: (layer, 0, g, 0, 0)),
        pl.BlockSpec((None, 2, None, S5_NSEQ, 2 * S5_SW), lambda g: (layer, 0, g, 0, 0)),
        pl.BlockSpec((None, None, 1, LANES), lambda g: (layer, g, 0, 0)),
    ]
    args = [u_tm, bmat, cmat, a, dskip]
    state_spec = pl.BlockSpec((2, None, S5_NSEQ, 2 * S5_SW), lambda g: (0, g, 0, 0))
    state_shape = jax.ShapeDtypeStruct((2, S5_NGB, S5_NSEQ, 2 * S5_SW), F32)
    if not zero_init:
        in_specs.append(state_spec)
        args.append(h0)
    out_specs, out_shape = [], []
    if want_y:
        out_specs.append(tm_spec)
        out_shape.append(jax.ShapeDtypeStruct(u_tm.shape, F32))
    out_specs.append(state_spec)
    out_shape.append(state_shape)
    res = pl.pallas_call(
        functools.partial(_s5_kernel, want_y=want_y, zero_init=zero_init),
        grid=(S5_NGB,),
        in_specs=in_specs,
        out_specs=out_specs,
        out_shape=out_shape,
        scratch_shapes=[pltpu.VMEM((2, S5_TC * S5_NSEQ, 2 * S5_SW), F32),
                        pltpu.VMEM((2, S5_TC * S5_NSEQ, 2 * S5_SW), F32),
                        pltpu.VMEM((2, S5_TC * S5_NSEQ, 2 * S5_SW), BF16),
                        pltpu.VMEM((2, S5_TC * S5_NSEQ, 2 * S5_SW), BF16),
                        pltpu.VMEM((2, S5_NSEQ, 2 * S5_SW), F32)],
        compiler_params=pltpu.CompilerParams(dimension_semantics=("parallel",),
                                             vmem_limit_bytes=VMEM_LIMIT),
        name="s5_scan",
    )(*args)
    return (res[0], res[1]) if want_y else (None, res[0])


def _s5_chain_kernel(z_ref, s0_ref, a_ref, h0_ref, *, pieces):
    sw = S5_SW
    pr, pi = a_ref[0, 0:1, :sw], a_ref[0, 0:1, sw:]
    for _ in range(int(math.log2(S5_SEQ))):
        pr, pi = pr * pr - pi * pi, 2.0 * (pr * pi)
    rev = pl.program_id(0) == 1
    n_long = S5_NSEQ // pieces
    for b in range(n_long):
        for fwd_order in (True, False):
            @pl.when(rev != fwd_order)
            def _(b=b, fwd_order=fwd_order):
                hr, hi = s0_ref[0, b:b + 1, :sw], s0_ref[0, b:b + 1, sw:]
                for k in (range(pieces) if fwd_order else range(pieces - 1, -1, -1)):
                    r = b * pieces + k
                    h0_ref[0, r:r + 1, :sw] = hr
                    h0_ref[0, r:r + 1, sw:] = hi
                    zr, zi = z_ref[0, r:r + 1, :sw], z_ref[0, r:r + 1, sw:]
                    hr, hi = pr * hr - pi * hi + zr, pr * hi + pi * hr + zi


def _s5_chain_call(z, s0, a, layer, pieces):
    n_long = S5_NSEQ // pieces
    return pl.pallas_call(
        functools.partial(_s5_chain_kernel, pieces=pieces),
        grid=(2, S5_NGB),
        in_specs=[pl.BlockSpec((1, None, S5_NSEQ, 2 * S5_SW), lambda d, g: (d, g, 0, 0)),
                  pl.BlockSpec((1, None, n_long, 2 * S5_SW), lambda d, g: (d, g, 0, 0)),
                  pl.BlockSpec((None, 1, None, S5_NSEQ, 2 * S5_SW), lambda d, g: (layer, d, g, 0, 0))],
        out_specs=pl.BlockSpec((1, None, S5_NSEQ, 2 * S5_SW), lambda d, g: (d, g, 0, 0)),
        out_shape=jax.ShapeDtypeStruct((2, S5_NGB, S5_NSEQ, 2 * S5_SW), F32),
        name="s5_chain",
    )(z, s0, a)


def _out_kernel(x_ref, ohg_ref, y5_ref, g1_ref, sh2_ref, sc2_ref, g2_ref, nffn_ref, nfin_ref,
                wglu_ref, wout_ref, wg_ref, wu_ref, wd_ref, o_ref, wglu_b, wout_b, *, final_norm):
    @pl.when(_first_step())
    def _():
        _cast_rows(wglu_ref, wglu_b)
        _cast_rows(wout_ref, wout_b)

    y = jnp.concatenate([y5_ref[:, s, :] for s in range(TILE_S)], axis=0)
    y = _gelu_tanh(y)
    y = y * _sigmoid(_dot(y.astype(BF16), wglu_b[...]))
    ohg = ohg_ref[...].reshape(TILE_ROWS, HG_W)
    mix = _dot(ohg.astype(BF16), wout_b[:HG_W, :]) + _dot(y.astype(BF16), wout_b[HG_W:, :])
    x = x_ref[...].reshape(TILE_ROWS, D_MODEL) + g1_ref[...] * mix
    h = _rms(x) * nffn_ref[...]
    h = (h * (1.0 + sc2_ref[...]) + sh2_ref[...]).astype(BF16)
    act = (_silu(_dot(h, wg_ref[...])) * _dot(h, wu_ref[...])).astype(BF16)
    x = x + g2_ref[...] * _dot(act, wd_ref[...])
    if final_norm:
        x = _rms(x) * nfin_ref[...]
    o_ref[...] = x.reshape(o_ref.shape)


def _out_call(x3, ohg3, y5_tm, mod4, nffn, nfin, wglu, wout, wg, wu, wd, layer, cond0, seqs_per_cond, final_norm):
    vec = pl.BlockSpec((1, D_MODEL), lambda sb, tb: (0, 0))
    return pl.pallas_call(
        functools.partial(_out_kernel, final_norm=final_norm),
        grid=(x3.shape[0] // TILE_S, S5_SEQ // TILE_T),
        in_specs=[_tile_spec(D_MODEL), _tile_spec(HG_W), _tm_tile_spec(S5_W),
                  _mod_spec(layer, cond0, seqs_per_cond, 2),
                  _mod_spec(layer, cond0, seqs_per_cond, 3),
                  _mod_spec(layer, cond0, seqs_per_cond, 4),
                  _mod_spec(layer, cond0, seqs_per_cond, 5),
                  vec, vec,
                  _layer_spec((S5_W, S5_W), layer), _layer_spec((D_MODEL, D_MODEL), layer),
                  _layer_spec((D_MODEL, D_FF), layer), _layer_spec((D_MODEL, D_FF), layer),
                  _layer_spec((D_FF, D_MODEL), layer)],
        out_specs=_tile_spec(D_MODEL),
        out_shape=jax.ShapeDtypeStruct(x3.shape, F32),
        scratch_shapes=[pltpu.VMEM((S5_W, S5_W), BF16), pltpu.VMEM((D_MODEL, D_MODEL), BF16)],
        compiler_params=pltpu.CompilerParams(dimension_semantics=("arbitrary", "arbitrary"),
                                             vmem_limit_bytes=VMEM_LIMIT),
        name="out_ffn",
    )(x3, ohg3, y5_tm, mod4, mod4, mod4, mod4, nffn, nfin, wglu, wout, wg, wu, wd)


def _s5_state_to_blocks(s):
    n = s.shape[0]
    s = s.reshape(n, 2, S5_NGB, S5_GB, S5_P, 2)
    return jnp.transpose(s, (1, 2, 0, 5, 3, 4)).reshape(2, S5_NGB, n, 2 * S5_SW)


def _s5_blocks_to_state(h):
    n = h.shape[2]
    h = h.reshape(2, S5_NGB, n, 2, S5_GB, S5_P)
    return jnp.transpose(h, (2, 0, 1, 4, 5, 3)).reshape(n, 2, S5_GROUPS, S5_P, 2)


def kernel(x_prompt, x_sample, state_hgrn, state_s5, c, c_ctx, w_mod, b_mod, norm_mix, norm_ffn, norm_final, w_in, w_out, hg_lb_logits, hg_norm, s5_lam_re, s5_lam_im, s5_log_dt, s5_b_re, s5_b_im, s5_c_re, s5_c_im, s5_d, s5_w_glu, w_gate, w_up, w_down):
    n_ctx, ctx_len, _ = x_prompt.shape
    n_dec, dec_len, _ = x_sample.shape
    assert ctx_len == S5_SEQ and n_ctx == S5_NSEQ and n_dec * dec_len == S5_NSEQ * S5_SEQ

    cond = jnp.concatenate([c_ctx[None, :], c, jnp.zeros((SUBLANES - 1 - n_dec, D_MODEL), F32)], axis=0)
    mod4 = _mod_call(cond, w_mod, b_mod).reshape(DEPTH, SUBLANES, 1, 6 * D_MODEL)

    w_gate_b, w_up_b, w_down_b = w_gate.astype(BF16), w_up.astype(BF16), w_down.astype(BF16)
    s5_a, s5_bmat, s5_cmat = _s5_params(s5_lam_re, s5_lam_im, s5_log_dt, s5_b_re, s5_b_im, s5_c_re, s5_c_im)
    s5_dskip = s5_d.reshape(DEPTH, S5_NGB, 1, LANES)
    nfin = norm_final.reshape(1, D_MODEL)

    def run(x3, add_pos, n_seq, seq_len, cond0, hg_state, s5_state):
        pieces = seq_len // S5_SEQ
        seqs_per_cond = S5_NSEQ if cond0 == 0 else pieces
        rows = S5_NSEQ * S5_SEQ
        hg_fin, s5_finals = None, []
        for l in range(DEPTH):
            proj3, u_tm, x3 = _in_call(x3, add_pos and l == 0, norm_mix[l].reshape(1, D_MODEL), mod4, w_in,
                                       l, cond0, seqs_per_cond)
            ohg, hg_fin = _hgrn_call(proj3.reshape(rows, HG_IN_W), hg_lb_logits, hg_norm[l].reshape(1, HG_D),
                                     hg_state, hg_fin, l, n_seq, seq_len)
            if s5_state is None:
                y5, s5_fin = _s5_call(u_tm, s5_a, s5_bmat, s5_cmat, s5_dskip, None, l, True)
            else:
                _, z = _s5_call(u_tm, s5_a, s5_bmat, s5_cmat, s5_dskip, None, l, False)
                h0 = _s5_chain_call(z, _s5_state_to_blocks(s5_state[:, l]), s5_a, l, pieces)
                y5, s5_fin = _s5_call(u_tm, s5_a, s5_bmat, s5_cmat, s5_dskip, h0, l, True)
            x3 = _out_call(x3, ohg.reshape(S5_NSEQ, S5_SEQ, HG_W), y5, mod4, norm_ffn[l].reshape(1, D_MODEL), nfin,
                           s5_w_glu, w_out, w_gate_b, w_up_b, w_down_b, l, cond0, seqs_per_cond, l == DEPTH - 1)
            s5_finals.append(_s5_blocks_to_state(s5_fin))
        return x3, hg_fin, s5_finals

    tok = (S5_NSEQ, S5_SEQ, D_MODEL)
    y_prompt, new_state_hgrn, s5_finals = run(x_prompt.reshape(tok), False, n_ctx, ctx_len, 0, None, None)
    y_sample, _, _ = run(x_sample.reshape(tok), True, n_dec, dec_len, 1, state_hgrn, state_s5)
    return (y_prompt.reshape(x_prompt.shape), y_sample.reshape(x_sample.shape),
            new_state_hgrn, jnp.stack(s5_finals, axis=1))
```

```python
import functools
import math

import jax
import jax.numpy as jnp
import numpy as np
from jax import lax
from jax.experimental import pallas as pl
from jax.experimental.pallas import tpu as pltpu

F32 = jnp.float32
BF16 = jnp.bfloat16

LANES = 128
SUBLANES = 8

D_MODEL = 1024
DEPTH = 2
GRID_W = 64
HG_W = 512
HG_HEADS = 4
HG_D = HG_W // HG_HEADS
S5_W = 512
S5_CH = 16
S5_GROUPS = S5_W // S5_CH
S5_P = 64
S5_GB = LANES // S5_CH
S5_NGB = S5_GROUPS // S5_GB
S5_SW = S5_GB * S5_P
HG_IN_W = 5 * HG_W
IN_W = HG_IN_W + S5_W
D_FF = 2816
EPS = 1e-6

HG_CHUNK = 128
HG_LEVELS = (64, 32, 16, 8, 4, 2, 1)
S5_SEQ = 256
S5_NSEQ = 16
S5_TC = 32

TILE_S = SUBLANES
TILE_T = 32
TILE_ROWS = TILE_S * TILE_T
CAST_ROWS = 128
MOD_TILE_N = 1536
VMEM_LIMIT = 56 * 1024 * 1024


def _sigmoid(x):
    return 1.0 / (1.0 + jnp.exp(-x))


def _silu(x):
    return x * _sigmoid(x)


def _gelu_tanh(x):
    return 0.5 * x * (1.0 + jnp.tanh(math.sqrt(2.0 / math.pi) * (x + 0.044715 * (x * x * x))))


def _rms(x):
    return x * lax.rsqrt(jnp.mean(x * x, axis=-1, keepdims=True) + EPS)


def _dot(a, b):
    return jnp.dot(a, b, preferred_element_type=F32)


def _dot_nt(a, b):
    return lax.dot_general(a, b, (((1,), (1,)), ((), ())), preferred_element_type=F32)


def _dot_tn(a, b):
    return lax.dot_general(a, b, (((0,), (0,)), ((), ())), preferred_element_type=F32)


def _layer_spec(shape, layer):
    nd = len(shape)
    return pl.BlockSpec((None,) + tuple(shape), lambda *_: (layer,) + (0,) * nd, pipeline_mode=pl.Buffered(1))


def _mod_kernel(cond_ref, w_ref, b_ref, o_ref):
    a = _silu(cond_ref[...]).astype(BF16)
    o_ref[0] = _dot(a, w_ref[0].astype(BF16)) + b_ref[0]


def _mod_call(cond, w_mod, b_mod):
    n_cond = cond.shape[0]
    n_out = w_mod.shape[-1]
    return pl.pallas_call(
        _mod_kernel,
        grid=(DEPTH, n_out // MOD_TILE_N),
        in_specs=[
            pl.BlockSpec((n_cond, D_MODEL), lambda l, j: (0, 0)),
            pl.BlockSpec((1, D_MODEL, MOD_TILE_N), lambda l, j: (l, 0, j)),
            pl.BlockSpec((1, 1, MOD_TILE_N), lambda l, j: (l, 0, j)),
        ],
        out_specs=pl.BlockSpec((1, n_cond, MOD_TILE_N), lambda l, j: (l, 0, j)),
        out_shape=jax.ShapeDtypeStruct((DEPTH, n_cond, n_out), F32),
        compiler_params=pltpu.CompilerParams(dimension_semantics=("parallel", "parallel"),
                                             vmem_limit_bytes=VMEM_LIMIT),
        name="adaln_mod",
    )(cond, w_mod, b_mod.reshape(DEPTH, 1, n_out))


def _first_step():
    return jnp.logical_and(pl.program_id(0) == 0, pl.program_id(1) == 0)


def _cast_rows(src_ref, dst_ref):
    for r in range(0, src_ref.shape[0], CAST_ROWS):
        dst_ref[r:r + CAST_ROWS, :] = src_ref[r:r + CAST_ROWS, :].astype(BF16)


def _grid_pos_tile(omega, tb):
    nf = omega.shape[-1]
    s_idx = lax.broadcasted_iota(jnp.int32, (TILE_S, nf), 0)
    j_idx = lax.broadcasted_iota(jnp.int32, (TILE_T, nf), 0)
    t0 = tb * TILE_T
    row = (s_idx * (S5_SEQ // GRID_W) + t0 // GRID_W).astype(F32) * omega
    col = (j_idx + t0 % GRID_W).astype(F32) * omega
    enc_r = jnp.concatenate([jnp.sin(row), jnp.cos(row)], axis=-1)
    enc_c = jnp.concatenate([jnp.sin(col), jnp.cos(col)], axis=-1)
    shape = (TILE_S, TILE_T, 2 * nf)
    return jnp.concatenate([jnp.broadcast_to(enc_r[:, None, :], shape),
                            jnp.broadcast_to(enc_c[None, :, :], shape)], axis=-1)


def _in_kernel(*refs, add_pos):
    if add_pos:
        x_ref, om_ref, gain_ref, sh_ref, sc_ref, w_ref, proj_ref, u_ref, xs_ref, wb_ref = refs
        x = x_ref[...] + _grid_pos_tile(om_ref[...], pl.program_id(1))
        xs_ref[...] = x
    else:
        x_ref, gain_ref, sh_ref, sc_ref, w_ref, proj_ref, u_ref, wb_ref = refs
        x = x_ref[...]

    @pl.when(_first_step())
    def _():
        _cast_rows(w_ref, wb_ref)

    x = x.reshape(TILE_ROWS, D_MODEL)
    h = _rms(x) * gain_ref[...]
    h = (h * (1.0 + sc_ref[...]) + sh_ref[...]).astype(BF16)
    proj_ref[...] = _dot(h, wb_ref[:, :HG_IN_W]).reshape(proj_ref.shape)
    u = _dot(h, wb_ref[:, HG_IN_W:])
    for s in range(TILE_S):
        u_ref[:, s, :] = u[s * TILE_T:(s + 1) * TILE_T, :]


def _tile_spec(width):
    return pl.BlockSpec((TILE_S, TILE_T, width), lambda sb, tb: (sb, tb, 0))


def _tm_tile_spec(width):
    return pl.BlockSpec((TILE_T, TILE_S, width), lambda sb, tb: (tb, sb, 0))


def _mod_spec(layer, cond0, seqs_per_cond, col):
    return pl.BlockSpec((None, None, 1, D_MODEL),
                        lambda sb, tb: (layer, cond0 + (sb * TILE_S) // seqs_per_cond, 0, col))


def _in_call(x3, add_pos, gain, mod4, w_in_b, layer, cond0, seqs_per_cond):
    n_pseq = x3.shape[0]
    in_specs = [_tile_spec(D_MODEL)]
    args = [x3]
    if add_pos:
        assert seqs_per_cond == TILE_S and GRID_W % TILE_T == 0 and S5_SEQ % GRID_W == 0
        nf = D_MODEL // 4
        omega = 1.0 / (np.float32(10000.0) ** (np.arange(nf, dtype=np.float32) / np.float32(nf)))
        in_specs.append(pl.BlockSpec((1, nf), lambda sb, tb: (0, 0)))
        args.append(jnp.asarray(omega.reshape(1, nf), F32))
    in_specs += [
        pl.BlockSpec((1, D_MODEL), lambda sb, tb: (0, 0)),
        _mod_spec(layer, cond0, seqs_per_cond, 0),
        _mod_spec(layer, cond0, seqs_per_cond, 1),
        _layer_spec((D_MODEL, IN_W), layer),
    ]
    args += [gain, mod4, mod4, w_in_b]
    out_specs = [_tile_spec(HG_IN_W), _tm_tile_spec(S5_W)]
    out_shape = [jax.ShapeDtypeStruct((n_pseq, S5_SEQ, HG_IN_W), F32),
                 jax.ShapeDtypeStruct((S5_SEQ, n_pseq, S5_W), F32)]
    if add_pos:
        out_specs.append(_tile_spec(D_MODEL))
        out_shape.append(jax.ShapeDtypeStruct(x3.shape, F32))
    res = pl.pallas_call(
        functools.partial(_in_kernel, add_pos=add_pos),
        grid=(n_pseq // TILE_S, S5_SEQ // TILE_T),
        in_specs=in_specs,
        out_specs=out_specs,
        out_shape=out_shape,
        scratch_shapes=[pltpu.VMEM((D_MODEL, IN_W), BF16)],
        compiler_params=pltpu.CompilerParams(dimension_semantics=("arbitrary", "arbitrary"),
                                             vmem_limit_bytes=VMEM_LIMIT),
        name="in_proj",
    )(*args)
    return (res[0], res[1], res[2]) if add_pos else (res[0], res[1], x3)


def _pair_boundary(b, m, rev):
    c = b.shape[0]
    span = 2 * m
    at = m if rev else m - 1
    if span >= SUBLANES:
        b3 = b.reshape(c // span, span, LANES)
        return jnp.broadcast_to(b3[:, at:at + 1, :], b3.shape).reshape(c, LANES)
    b3 = b.reshape(c // SUBLANES, SUBLANES, LANES)
    sub = lax.broadcasted_iota(jnp.int32, b3.shape, 1)
    out = None
    for p in range(SUBLANES // span):
        piece = jnp.broadcast_to(b3[:, p * span + at:p * span + at + 1, :], b3.shape)
        out = piece if out is None else jnp.where(sub >= p * span, piece, out)
    return out.reshape(c, LANES)


def _neg_abs(x):
    bits = lax.bitcast_convert_type(x, jnp.uint32) | jnp.uint32(0x80000000)
    return lax.bitcast_convert_type(bits, F32)


def _hg_gates(chains, scale):
    outs = []
    for q, fl, lb, tri in chains:
        sig = _sigmoid(fl)
        forget = lb + (1.0 - lb) * sig
        logf = jnp.log2(forget)
        key = (1.0 - lb) * (1.0 - sig)
        hi = logf.astype(BF16)
        r1 = logf - hi.astype(F32)
        mid = r1.astype(BF16)
        lo = (r1 - mid.astype(F32)).astype(BF16)
        parts = _dot(tri, jnp.concatenate([hi, mid, lo], axis=1))
        b2 = parts[:, :LANES] + parts[:, LANES:2 * LANES] + parts[:, 2 * LANES:]
        outs.append((_silu(q) * scale, key, b2, forget))
    return outs


def _hg_scores(chains, code, eye):
    c = chains[0][0].shape[0]
    o_inter = []
    for qh, key, b2, forget, v, st_ref, rev in chains:
        b_edge = b2[0:1, :] if rev else b2[c - 1:c, :]
        st = st_ref[...]
        o_inter.append(_dot_nt((qh * jnp.exp2(b2)).astype(BF16), st.astype(BF16)))
        k_end = key * jnp.exp2(b_edge - b2)
        st_ref[...] = jnp.exp2(b_edge) * st + _dot_tn(v.astype(BF16), k_end.astype(BF16))

    scores = [jnp.where(eye, jnp.sum(ch[0] * ch[1], axis=-1, keepdims=True), 0.0) for ch in chains]
    qkb = [(ch[0].astype(BF16), ch[1].astype(BF16)) for ch in chains]
    for m in HG_LEVELS:
        k = int(math.log2(m)) + 1
        for i, (qh, key, b2, forget, v, st_ref, rev) in enumerate(chains):
            if m == 1:
                p = _dot_nt((qh * forget).astype(BF16), qkb[i][1])
            else:
                e = jnp.exp2(_neg_abs(b2 - _pair_boundary(b2, m, rev))).astype(BF16)
                p = _dot_nt(qkb[i][0] * e, qkb[i][1] * e)
            scores[i] = jnp.where(code == (-k if rev else k), p, scores[i])
    return [(o, sc.astype(BF16)) for o, sc in zip(o_inter, scores)]


def _hgrn_kernel(*refs, layer, n_chunks, zero_init):
    refs = list(refs)
    q_ref, ff_ref, fb_ref, v_ref, g_ref, lbl_ref, gain_ref, code_ref, tri_ref = refs[:9]
    s0_ref = None if zero_init else refs[9]
    o_ref, sfin_ref, st_ref, ob_ref, ab_ref, oi_ref, sc_ref = refs[-7:]
    c = HG_CHUNK
    code = code_ref[...]
    eye = code == 0
    tri_f, tri_b = tri_ref[0], tri_ref[1]

    def lower_bound(d):
        lg = lbl_ref[d]
        ex = jnp.exp(lg - jnp.max(lg, axis=0, keepdims=True))
        soft = ex / jnp.sum(ex, axis=0, keepdims=True)
        return jnp.sum(soft[:layer + 1], axis=0, keepdims=True) - soft[0:1]

    lb_f, lb_b = lower_bound(0), lower_bound(1)
    scale = HG_D ** -0.5

    def rows_of(n):
        return pl.ds(n * c if isinstance(n, int) else pl.multiple_of(n * c, c), c)

    for d in (0, 1):
        st_ref[d] = jnp.zeros((HG_D, HG_D), F32) if zero_init else s0_ref[d].T

    def gates(i):
        rf, rb = rows_of(i), rows_of(n_chunks - 1 - i)
        return _hg_gates([(q_ref[rf, :], ff_ref[rf, :], lb_f, tri_f),
                          (q_ref[rb, :], fb_ref[rb, :], lb_b, tri_b)], scale)

    def scores(i, ab):
        rf, rb = rows_of(i), rows_of(n_chunks - 1 - i)
        return _hg_scores([ab[0] + (v_ref[rf, :], st_ref.at[0], False),
                           ab[1] + (v_ref[rb, :], st_ref.at[1], True)], code, eye)

    def emit(i, oi_sc):
        rf, rb = rows_of(i), rows_of(n_chunks - 1 - i)
        o_ref[rf, :] = oi_sc[0][0] + _dot(oi_sc[0][1], v_ref[rf, :].astype(BF16))
        ob_ref[rb, :] = oi_sc[1][0] + _dot(oi_sc[1][1], v_ref[rb, :].astype(BF16))

    if n_chunks <= 2:
        ab = [gates(i) for i in range(n_chunks)]
        for i in range(n_chunks):
            emit(i, scores(i, ab[i]))
    else:
        n_ab = ab_ref.shape[1]

        def put_ab(ab):
            for ch in range(2):
                for k in range(n_ab):
                    ab_ref[ch, k] = ab[ch][k]

        put_ab(gates(0))
        oi_ref[...] = jnp.zeros(oi_ref.shape, F32)
        sc_ref[...] = jnp.zeros(sc_ref.shape, BF16)

        def body(i, carry):
            emit(jnp.maximum(i - 1, 0), [(oi_ref[ch], sc_ref[ch]) for ch in range(2)])
            res = scores(i, [tuple(ab_ref[ch, k] for k in range(n_ab)) for ch in range(2)])
            for ch in range(2):
                oi_ref[ch] = res[ch][0]
                sc_ref[ch] = res[ch][1]
            put_ab(gates(jnp.minimum(i + 1, n_chunks - 1)))
            return carry

        lax.fori_loop(0, n_chunks, body, 0)
        emit(n_chunks - 1, [(oi_ref[ch], sc_ref[ch]) for ch in range(2)])
    if len(sfin_ref.shape) == 4:
        for other in range(sfin_ref.shape[0]):
            if other != layer:
                sfin_ref[other] = jnp.zeros(sfin_ref.shape[1:], F32)
        sfin_ref = sfin_ref.at[layer]
    for d in (0, 1):
        sfin_ref[d] = st_ref[d].T

    def finish(n, carry):
        rows = pl.ds(pl.multiple_of(n * (2 * c), 2 * c), 2 * c)
        o_ref[rows, :] = _rms(o_ref[rows, :] + ob_ref[rows, :]) * gain_ref[...] * _silu(g_ref[rows, :])
        return carry

    lax.fori_loop(0, n_chunks // 2, finish, 0)


def _hgrn_call(proj, lb_logits, gain, state, finals, layer, n_seq, seq_len):
    rows = proj.shape[0]
    zero_init = state is None

    def col_spec(k):
        return pl.BlockSpec((seq_len, HG_D), lambda b, h: (b, k * HG_HEADS + h))

    assert (seq_len // HG_CHUNK) % 2 == 0
    t, s = np.meshgrid(np.arange(HG_CHUNK), np.arange(HG_CHUNK), indexing="ij")
    lvl = np.where(t == s, 0, np.floor(np.log2(np.maximum(t ^ s, 1))).astype(np.int32) + 1)
    code = jnp.asarray(np.where(t > s, lvl, -lvl), jnp.int32)
    tri = jnp.asarray(np.stack([s <= t, s >= t]), BF16)

    in_specs = [col_spec(0), col_spec(1), col_spec(2), col_spec(3), col_spec(4),
                pl.BlockSpec((2, DEPTH, HG_D), lambda b, h: (0, 0, h)),
                pl.BlockSpec((1, HG_D), lambda b, h: (0, 0)),
                pl.BlockSpec((HG_CHUNK, HG_CHUNK), lambda b, h: (0, 0)),
                pl.BlockSpec((2, HG_CHUNK, HG_CHUNK), lambda b, h: (0, 0, 0))]
    args = [proj] * 5 + [lb_logits, gain, code, tri]
    if not zero_init:
        in_specs.append(pl.BlockSpec((None, None, 2, None, HG_D, HG_D), lambda b, h: (b, layer, 0, h, 0, 0)))
        args.append(state)
    aliases = {}
    if finals is None:
        fin_spec = pl.BlockSpec((None, DEPTH, 2, None, HG_D, HG_D), lambda b, h: (b, 0, 0, h, 0, 0))
    else:
        fin_spec = pl.BlockSpec((None, None, 2, None, HG_D, HG_D), lambda b, h: (b, layer, 0, h, 0, 0))
        aliases[len(args)] = 1
        in_specs.append(pl.BlockSpec(memory_space=pl.ANY))
        args.append(finals)
    return pl.pallas_call(
        functools.partial(_hgrn_kernel, layer=layer, n_chunks=seq_len // HG_CHUNK, zero_init=zero_init),
        grid=(n_seq, HG_HEADS),
        in_specs=in_specs,
        out_specs=[pl.BlockSpec((seq_len, HG_D), lambda b, h: (b, h)), fin_spec],
        out_shape=[jax.ShapeDtypeStruct((rows, HG_W), F32),
                   jax.ShapeDtypeStruct((n_seq, DEPTH, 2, HG_HEADS, HG_D, HG_D), F32)],
        input_output_aliases=aliases,
        scratch_shapes=[pltpu.VMEM((2, HG_D, HG_D), F32), pltpu.VMEM((seq_len, HG_D), F32),
                        pltpu.VMEM((2, 4, HG_CHUNK, HG_D), F32), pltpu.VMEM((2, HG_CHUNK, HG_D), F32),
                        pltpu.VMEM((2, HG_CHUNK, HG_CHUNK), BF16)],
        compiler_params=pltpu.CompilerParams(dimension_semantics=("parallel", "parallel"),
                                             vmem_limit_bytes=VMEM_LIMIT),
        name="hgrn2_mixer",
    )(*args)


def _s5_params_kernel(lr_ref, li_ref, ldt_ref, btr_ref, bti_ref, cr_ref, ci_ref, a_ref, bm_ref, cm_ref):
    sw = S5_SW
    lr = jnp.minimum(lr_ref[...], -1e-4)
    li = li_ref[...]
    dt = jnp.exp(ldt_ref[...])
    mag = jnp.exp(lr * dt)
    ab_re = mag * jnp.cos(li * dt)
    ab_im = mag * jnp.sin(li * dt)
    nr = ab_re - 1.0
    den = lr * lr + li * li
    z_re = (nr * lr + ab_im * li) / den
    z_im = (ab_im * lr - nr * li) / den

    p_idx = lax.broadcasted_iota(jnp.int32, (S5_P, sw), 0)
    col = lax.broadcasted_iota(jnp.int32, (S5_P, sw), 1)
    for g in range(S5_GB):
        place = (col == p_idx + g * S5_P).astype(BF16)
        zr, zi = z_re[g:g + 1, :], z_im[g:g + 1, :]
        btr, bti = btr_ref[g], bti_ref[g]
        rows = slice(g * S5_CH, (g + 1) * S5_CH)
        bm_ref[rows, :sw] = _dot((zr * btr - zi * bti).astype(BF16), place).astype(BF16)
        bm_ref[rows, sw:] = _dot((zr * bti + zi * btr).astype(BF16), place).astype(BF16)
        cm_ref[rows, :sw] = _dot(cr_ref[g].astype(BF16), place).astype(BF16)
        cm_ref[rows, sw:] = _dot((-ci_ref[g]).astype(BF16), place).astype(BF16)
        a_ref[:, g * S5_P:(g + 1) * S5_P] = jnp.broadcast_to(ab_re[g:g + 1, :], (S5_NSEQ, S5_P))
        a_ref[:, sw + g * S5_P:sw + (g + 1) * S5_P] = jnp.broadcast_to(ab_im[g:g + 1, :], (S5_NSEQ, S5_P))


def _s5_params(lam_re, lam_im, log_dt, b_re, b_im, c_re, c_im):
    nb = DEPTH * 2 * S5_NGB
    gp = (nb, S5_GB, S5_P)
    gcp = (nb, S5_GB, S5_CH, S5_P)
    bt_re = jnp.swapaxes(b_re, -1, -2).reshape(gcp)
    bt_im = jnp.swapaxes(b_im, -1, -2).reshape(gcp)
    ldt = jnp.broadcast_to(log_dt.reshape(nb, S5_GB, 1), gp)
    gp_spec = pl.BlockSpec((None, S5_GB, S5_P), lambda i: (i, 0, 0))
    gcp_spec = pl.BlockSpec((None, S5_GB, S5_CH, S5_P), lambda i: (i, 0, 0, 0))
    a, bmat, cmat = pl.pallas_call(
        _s5_params_kernel,
        grid=(nb,),
        in_specs=[gp_spec] * 3 + [gcp_spec] * 4,
        out_specs=[pl.BlockSpec((None, S5_NSEQ, 2 * S5_SW), lambda i: (i, 0, 0)),
                   pl.BlockSpec((None, LANES, 2 * S5_SW), lambda i: (i, 0, 0)),
                   pl.BlockSpec((None, LANES, 2 * S5_SW), lambda i: (i, 0, 0))],
        out_shape=[jax.ShapeDtypeStruct((nb, S5_NSEQ, 2 * S5_SW), F32),
                   jax.ShapeDtypeStruct((nb, LANES, 2 * S5_SW), BF16),
                   jax.ShapeDtypeStruct((nb, LANES, 2 * S5_SW), BF16)],
        compiler_params=pltpu.CompilerParams(dimension_semantics=("parallel",)),
        name="s5_params",
    )(lam_re.reshape(gp), lam_im.reshape(gp), ldt, bt_re, bt_im, c_re.reshape(gcp), c_im.reshape(gcp))
    lead = (DEPTH, 2, S5_NGB)
    return (a.reshape(lead + a.shape[1:]), bmat.reshape(lead + bmat.shape[1:]), cmat.reshape(lead + cmat.shape[1:]))


def _s5_kernel(*refs, want_y, zero_init):
    refs = list(refs)
    u_ref, bm_ref, cm_ref, a_ref, d_ref = refs[:5]
    rest = refs[5:]
    h0_ref = None if zero_init else rest.pop(0)
    y_ref = rest.pop(0) if want_y else None
    hfin_ref, hbuf0, hbuf1, hb0, hb1, hst = rest
    hbufs, hb16s = (hbuf0, hbuf1), (hb0, hb1)
    ns, sw = S5_NSEQ, S5_SW
    half = ns // 2
    n_tc = S5_SEQ // S5_TC
    blk = S5_TC * ns
    dirs = (0, 1)

    for d in dirs:
        hst[d] = jnp.zeros((ns, 2 * sw), F32) if zero_init else h0_ref[d]

    def steps_of(d, i):
        return pl.ds(((n_tc - 1 - i) if d else i) * S5_TC, S5_TC)

    def project(i, slot):
        for d in dirs:
            u = u_ref[steps_of(d, i)].reshape(blk, LANES)
            hbufs[slot][d] = _dot(u.astype(BF16), bm_ref[d])

    def scan(slot):
        a = [(a_ref[d, :half, :sw], a_ref[d, :half, sw:]) for d in dirs]
        h = [[(hst[d, k * half:(k + 1) * half, :sw], hst[d, k * half:(k + 1) * half, sw:]) for k in range(2)]
             for d in dirs]
        for jj in range(S5_TC):
            for d in dirs:
                j = S5_TC - 1 - jj if d else jj
                ar, ai = a[d]
                for k in range(2):
                    r = slice(j * ns + k * half, j * ns + (k + 1) * half)
                    hr, hi = h[d][k]
                    h[d][k] = (ar * hr - ai * hi + hbufs[slot][d, r, :sw], ar * hi + ai * hr + hbufs[slot][d, r, sw:])
                if want_y:
                    r = slice(j * ns, (j + 1) * ns)
                    hb16s[slot][d, r, :sw] = jnp.concatenate([h[d][0][0], h[d][1][0]], axis=0).astype(BF16)
                    hb16s[slot][d, r, sw:] = jnp.concatenate([h[d][0][1], h[d][1][1]], axis=0).astype(BF16)
        for d in dirs:
            for k in range(2):
                hst[d, k * half:(k + 1) * half, :sw] = h[d][k][0]
                hst[d, k * half:(k + 1) * half, sw:] = h[d][k][1]

    def readout(i, slot):
        for d in dirs:
            steps = steps_of(d, i)
            y = _dot_nt(hb16s[slot][d], cm_ref[d])
            y_ref[steps] = y_ref[steps] + y.reshape(S5_TC, ns, LANES)

    project(0, 0)
    if want_y:
        hb1[...] = jnp.zeros(hb1.shape, BF16)

        def skip(n, carry):
            steps = pl.ds(n * S5_TC, S5_TC)
            y_ref[steps] = d_ref[...] * u_ref[steps]
            return carry

        lax.fori_loop(0, n_tc, skip, 0)

    def body(k, carry):
        i = 2 * k
        if want_y:
            readout(jnp.maximum(i - 1, 0), 1)
        scan(0)
        project(i + 1, 1)
        if want_y:
            readout(i, 0)
        scan(1)
        project(jnp.minimum(i + 2, n_tc - 1), 0)
        return carry

    lax.fori_loop(0, n_tc // 2, body, 0)
    for d in dirs:
        hfin_ref[d] = hst[d]
    if want_y:
        readout(n_tc - 1, 1)


def _s5_call(u_tm, a, bmat, cmat, dskip, h0, layer, want_y):
    zero_init = h0 is None
    tm_spec = pl.BlockSpec((S5_SEQ, S5_NSEQ, LANES), lambda g: (0, 0, g))

    def mat_spec(rows):
        return pl.BlockSpec((None, 2, None, rows, 2 * S5_SW), lambda g: (layer, 0, g, 0, 0))

    in_specs = [tm_spec, mat_spec(LANES), mat_spec(LANES), mat_spec(S5_NSEQ),
                pl.BlockSpec((None, None, 1, LANES), lambda g: (layer, g, 0, 0))]
    args = [u_tm, bmat, cmat, a, dskip]
    state_spec = pl.BlockSpec((2, None, S5_NSEQ, 2 * S5_SW), lambda g: (0, g, 0, 0))
    state_shape = jax.ShapeDtypeStruct((2, S5_NGB, S5_NSEQ, 2 * S5_SW), F32)
    if not zero_init:
        in_specs.append(state_spec)
        args.append(h0)
    out_specs, out_shape = [], []
    if want_y:
        out_specs.append(tm_spec)
        out_shape.append(jax.ShapeDtypeStruct(u_tm.shape, F32))
    out_specs.append(state_spec)
    out_shape.append(state_shape)
    res = pl.pallas_call(
        functools.partial(_s5_kernel, want_y=want_y, zero_init=zero_init),
        grid=(S5_NGB,),
        in_specs=in_specs,
        out_specs=out_specs,
        out_shape=out_shape,
        scratch_shapes=[pltpu.VMEM((2, S5_TC * S5_NSEQ, 2 * S5_SW), F32),
                        pltpu.VMEM((2, S5_TC * S5_NSEQ, 2 * S5_SW), F32),
                        pltpu.VMEM((2, S5_TC * S5_NSEQ, 2 * S5_SW), BF16),
                        pltpu.VMEM((2, S5_TC * S5_NSEQ, 2 * S5_SW), BF16),
                        pltpu.VMEM((2, S5_NSEQ, 2 * S5_SW), F32)],
        compiler_params=pltpu.CompilerParams(dimension_semantics=("parallel",),
                                             vmem_limit_bytes=VMEM_LIMIT),
        name="s5_scan",
    )(*args)
    return (res[0], res[1]) if want_y else (None, res[0])


def _s5_chain_kernel(z_ref, s0_ref, a_ref, h0_ref, *, pieces):
    sw = S5_SW
    pr, pi = a_ref[0, 0:1, :sw], a_ref[0, 0:1, sw:]
    for _ in range(int(math.log2(S5_SEQ))):
        pr, pi = pr * pr - pi * pi, 2.0 * (pr * pi)
    rev = pl.program_id(0) == 1
    n_long = S5_NSEQ // pieces
    for b in range(n_long):
        for fwd_order in (True, False):
            @pl.when(rev != fwd_order)
            def _(b=b, fwd_order=fwd_order):
                hr, hi = s0_ref[0, b:b + 1, :sw], s0_ref[0, b:b + 1, sw:]
                for k in (range(pieces) if fwd_order else range(pieces - 1, -1, -1)):
                    r = b * pieces + k
                    h0_ref[0, r:r + 1, :sw] = hr
                    h0_ref[0, r:r + 1, sw:] = hi
                    zr, zi = z_ref[0, r:r + 1, :sw], z_ref[0, r:r + 1, sw:]
                    hr, hi = pr * hr - pi * hi + zr, pr * hi + pi * hr + zi


def _s5_chain_call(z, s0, a, layer, pieces):
    n_long = S5_NSEQ // pieces
    return pl.pallas_call(
        functools.partial(_s5_chain_kernel, pieces=pieces),
        grid=(2, S5_NGB),
        in_specs=[pl.BlockSpec((1, None, S5_NSEQ, 2 * S5_SW), lambda d, g: (d, g, 0, 0)),
                  pl.BlockSpec((1, None, n_long, 2 * S5_SW), lambda d, g: (d, g, 0, 0)),
                  pl.BlockSpec((None, 1, None, S5_NSEQ, 2 * S5_SW), lambda d, g: (layer, d, g, 0, 0))],
        out_specs=pl.BlockSpec((1, None, S5_NSEQ, 2 * S5_SW), lambda d, g: (d, g, 0, 0)),
        out_shape=jax.ShapeDtypeStruct((2, S5_NGB, S5_NSEQ, 2 * S5_SW), F32),
        name="s5_chain",
    )(z, s0, a)


def _out_kernel(x_ref, ohg_ref, y5_ref, g1_ref, sh2_ref, sc2_ref, g2_ref, nffn_ref, nfin_ref,
                wglu_ref, wout_ref, wg_ref, wu_ref, wd_ref, o_ref, wglu_b, wout_b, *, final_norm):
    @pl.when(_first_step())
    def _():
        _cast_rows(wglu_ref, wglu_b)
        _cast_rows(wout_ref, wout_b)

    y = jnp.concatenate([y5_ref[:, s, :] for s in range(TILE_S)], axis=0)
    y = _gelu_tanh(y)
    y = y * _sigmoid(_dot(y.astype(BF16), wglu_b[...]))
    ohg = ohg_ref[...].reshape(TILE_ROWS, HG_W)
    mix = _dot(ohg.astype(BF16), wout_b[:HG_W, :]) + _dot(y.astype(BF16), wout_b[HG_W:, :])
    x = x_ref[...].reshape(TILE_ROWS, D_MODEL) + g1_ref[...] * mix
    h = _rms(x) * nffn_ref[...]
    h = (h * (1.0 + sc2_ref[...]) + sh2_ref[...]).astype(BF16)
    act = (_silu(_dot(h, wg_ref[...])) * _dot(h, wu_ref[...])).astype(BF16)
    x = x + g2_ref[...] * _dot(act, wd_ref[...])
    if final_norm:
        x = _rms(x) * nfin_ref[...]
    o_ref[...] = x.reshape(o_ref.shape)


def _out_call(x3, ohg3, y5_tm, mod4, nffn, nfin, wglu, wout, wg, wu, wd, layer, cond0, seqs_per_cond, final_norm):
    vec = pl.BlockSpec((1, D_MODEL), lambda sb, tb: (0, 0))
    return pl.pallas_call(
        functools.partial(_out_kernel, final_norm=final_norm),
        grid=(x3.shape[0] // TILE_S, S5_SEQ // TILE_T),
        in_specs=[_tile_spec(D_MODEL), _tile_spec(HG_W), _tm_tile_spec(S5_W),
                  _mod_spec(layer, cond0, seqs_per_cond, 2),
                  _mod_spec(layer, cond0, seqs_per_cond, 3),
                  _mod_spec(layer, cond0, seqs_per_cond, 4),
                  _mod_spec(layer, cond0, seqs_per_cond, 5),
                  vec, vec,
                  _layer_spec((S5_W, S5_W), layer), _layer_spec((D_MODEL, D_MODEL), layer),
                  _layer_spec((D_MODEL, D_FF), layer), _layer_spec((D_MODEL, D_FF), layer),
                  _layer_spec((D_FF, D_MODEL), layer)],
        out_specs=_tile_spec(D_MODEL),
        out_shape=jax.ShapeDtypeStruct(x3.shape, F32),
        scratch_shapes=[pltpu.VMEM((S5_W, S5_W), BF16), pltpu.VMEM((D_MODEL, D_MODEL), BF16)],
        compiler_params=pltpu.CompilerParams(dimension_semantics=("arbitrary", "arbitrary"),
                                             vmem_limit_bytes=VMEM_LIMIT),
        name="out_ffn",
    )(x3, ohg3, y5_tm, mod4, mod4, mod4, mod4, nffn, nfin, wglu, wout, wg, wu, wd)


def _s5_state_to_blocks(s):
    n = s.shape[0]
    s = s.reshape(n, 2, S5_NGB, S5_GB, S5_P, 2)
    return jnp.transpose(s, (1, 2, 0, 5, 3, 4)).reshape(2, S5_NGB, n, 2 * S5_SW)


def _s5_blocks_to_state(h):
    n = h.shape[2]
    h = h.reshape(2, S5_NGB, n, 2, S5_GB, S5_P)
    return jnp.transpose(h, (2, 0, 1, 4, 5, 3)).reshape(n, 2, S5_GROUPS, S5_P, 2)


def kernel(x_prompt, x_sample, state_hgrn, state_s5, c, c_ctx, w_mod, b_mod, norm_mix, norm_ffn, norm_final, w_in, w_out, hg_lb_logits, hg_norm, s5_lam_re, s5_lam_im, s5_log_dt, s5_b_re, s5_b_im, s5_c_re, s5_c_im, s5_d, s5_w_glu, w_gate, w_up, w_down):
    n_ctx, ctx_len, _ = x_prompt.shape
    n_dec, dec_len, _ = x_sample.shape
    assert ctx_len == S5_SEQ and n_ctx == S5_NSEQ and n_dec * dec_len == S5_NSEQ * S5_SEQ

    cond = jnp.concatenate([c_ctx[None, :], c, jnp.zeros((SUBLANES - 1 - n_dec, D_MODEL), F32)], axis=0)
    mod4 = _mod_call(cond, w_mod, b_mod).reshape(DEPTH, SUBLANES, 1, 6 * D_MODEL)

    w_gate_b, w_up_b, w_down_b = w_gate.astype(BF16), w_up.astype(BF16), w_down.astype(BF16)
    s5_a, s5_bmat, s5_cmat = _s5_params(s5_lam_re, s5_lam_im, s5_log_dt, s5_b_re, s5_b_im, s5_c_re, s5_c_im)
    s5_dskip = s5_d.reshape(DEPTH, S5_NGB, 1, LANES)
    nfin = norm_final.reshape(1, D_MODEL)

    def run(x3, add_pos, n_seq, seq_len, cond0, hg_state, s5_state):
        pieces = seq_len // S5_SEQ
        seqs_per_cond = S5_NSEQ if cond0 == 0 else pieces
        rows = S5_NSEQ * S5_SEQ
        hg_fin, s5_finals = None, []
        for l in range(DEPTH):
            proj3, u_tm, x3 = _in_call(x3, add_pos and l == 0, norm_mix[l].reshape(1, D_MODEL), mod4, w_in,
                                       l, cond0, seqs_per_cond)
            ohg, hg_fin = _hgrn_call(proj3.reshape(rows, HG_IN_W), hg_lb_logits, hg_norm[l].reshape(1, HG_D),
                                     hg_state, hg_fin, l, n_seq, seq_len)
            if s5_state is None:
                y5, s5_fin = _s5_call(u_tm, s5_a, s5_bmat, s5_cmat, s5_dskip, None, l, True)
            else:
                _, z = _s5_call(u_tm, s5_a, s5_bmat, s5_cmat, s5_dskip, None, l, False)
                h0 = _s5_chain_call(z, _s5_state_to_blocks(s5_state[:, l]), s5_a, l, pieces)
                y5, s5_fin = _s5_call(u_tm, s5_a, s5_bmat, s5_cmat, s5_dskip, h0, l, True)
            x3 = _out_call(x3, ohg.reshape(S5_NSEQ, S5_SEQ, HG_W), y5, mod4, norm_ffn[l].reshape(1, D_MODEL), nfin,
                           s5_w_glu, w_out, w_gate_b, w_up_b, w_down_b, l, cond0, seqs_per_cond, l == DEPTH - 1)
            s5_finals.append(_s5_blocks_to_state(s5_fin))
        return x3, hg_fin, s5_finals

    tok = (S5_NSEQ, S5_SEQ, D_MODEL)
    y_prompt, new_state_hgrn, s5_finals = run(x_prompt.reshape(tok), False, n_ctx, ctx_len, 0, None, None)
    y_sample, _, _ = run(x_sample.reshape(tok), True, n_dec, dec_len, 1, state_hgrn, state_s5)
    return (y_prompt.reshape(x_prompt.shape), y_sample.reshape(x_sample.shape),
            new_state_hgrn, jnp.stack(s5_finals, axis=1))
```

```python
import functools
import math

import jax
import jax.numpy as jnp
import numpy as np
from jax import lax
from jax.experimental import pallas as pl
from jax.experimental.pallas import tpu as pltpu

F32 = jnp.float32
BF16 = jnp.bfloat16

LANES = 128
SUBLANES = 8

D_MODEL = 1024
DEPTH = 2
GRID_W = 64
HG_W = 512
HG_HEADS = 4
HG_D = HG_W // HG_HEADS
S5_W = 512
S5_CH = 16
S5_GROUPS = S5_W // S5_CH
S5_P = 64
S5_GB = LANES // S5_CH
S5_NGB = S5_GROUPS // S5_GB
S5_SW = S5_GB * S5_P
HG_IN_W = 5 * HG_W
IN_W = HG_IN_W + S5_W
D_FF = 2816
EPS = 1e-6

HG_CHUNK = 128
HG_LEVELS = (64, 32, 16, 8, 4, 2, 1)
S5_SEQ = 256
S5_NSEQ = 16
S5_TC = 32

TILE_S = SUBLANES
TILE_T = 64
TILE_ROWS = TILE_S * TILE_T
CAST_ROWS = 128
MOD_TILE_N = 1536
VMEM_LIMIT = 56 * 1024 * 1024


def _sigmoid(x):
    return 1.0 / (1.0 + jnp.exp(-x))


def _silu(x):
    return x * _sigmoid(x)


def _gelu_tanh(x):
    return 0.5 * x * (1.0 + jnp.tanh(math.sqrt(2.0 / math.pi) * (x + 0.044715 * (x * x * x))))


def _rms(x):
    return x * lax.rsqrt(jnp.mean(x * x, axis=-1, keepdims=True) + EPS)


def _dot(a, b):
    return jnp.dot(a, b, preferred_element_type=F32)


def _dot_nt(a, b):
    return lax.dot_general(a, b, (((1,), (1,)), ((), ())), preferred_element_type=F32)


def _dot_tn(a, b):
    return lax.dot_general(a, b, (((0,), (0,)), ((), ())), preferred_element_type=F32)


def _layer_spec(shape, layer):
    nd = len(shape)
    return pl.BlockSpec((None,) + tuple(shape), lambda *_: (layer,) + (0,) * nd, pipeline_mode=pl.Buffered(1))


def _mod_kernel(cond_ref, w_ref, b_ref, o_ref):
    a = _silu(cond_ref[...]).astype(BF16)
    o_ref[0] = _dot(a, w_ref[0].astype(BF16)) + b_ref[0]


def _mod_call(cond, w_mod, b_mod):
    n_cond = cond.shape[0]
    n_out = w_mod.shape[-1]
    return pl.pallas_call(
        _mod_kernel,
        grid=(DEPTH, n_out // MOD_TILE_N),
        in_specs=[
            pl.BlockSpec((n_cond, D_MODEL), lambda l, j: (0, 0)),
            pl.BlockSpec((1, D_MODEL, MOD_TILE_N), lambda l, j: (l, 0, j)),
            pl.BlockSpec((1, 1, MOD_TILE_N), lambda l, j: (l, 0, j)),
        ],
        out_specs=pl.BlockSpec((1, n_cond, MOD_TILE_N), lambda l, j: (l, 0, j)),
        out_shape=jax.ShapeDtypeStruct((DEPTH, n_cond, n_out), F32),
        compiler_params=pltpu.CompilerParams(dimension_semantics=("parallel", "parallel"),
                                             vmem_limit_bytes=VMEM_LIMIT),
        name="adaln_mod",
    )(cond, w_mod, b_mod.reshape(DEPTH, 1, n_out))


def _first_step():
    return jnp.logical_and(pl.program_id(0) == 0, pl.program_id(1) == 0)


def _cast_rows(src_ref, dst_ref):
    for r in range(0, src_ref.shape[0], CAST_ROWS):
        dst_ref[r:r + CAST_ROWS, :] = src_ref[r:r + CAST_ROWS, :].astype(BF16)


def _grid_pos_tile(omega, tb):
    nf = omega.shape[-1]
    s_idx = lax.broadcasted_iota(jnp.int32, (TILE_S, nf), 0)
    j_idx = lax.broadcasted_iota(jnp.int32, (TILE_T, nf), 0)
    t0 = tb * TILE_T
    row = (s_idx * (S5_SEQ // GRID_W) + t0 // GRID_W).astype(F32) * omega
    col = (j_idx + t0 % GRID_W).astype(F32) * omega
    enc_r = jnp.concatenate([jnp.sin(row), jnp.cos(row)], axis=-1)
    enc_c = jnp.concatenate([jnp.sin(col), jnp.cos(col)], axis=-1)
    shape = (TILE_S, TILE_T, 2 * nf)
    return jnp.concatenate([jnp.broadcast_to(enc_r[:, None, :], shape),
                            jnp.broadcast_to(enc_c[None, :, :], shape)], axis=-1)


def _in_kernel(*refs, add_pos):
    if add_pos:
        x_ref, om_ref, gain_ref, sh_ref, sc_ref, w_ref, proj_ref, u_ref, xs_ref, wb_ref = refs
        x = x_ref[...] + _grid_pos_tile(om_ref[...], pl.program_id(1))
        xs_ref[...] = x
    else:
        x_ref, gain_ref, sh_ref, sc_ref, w_ref, proj_ref, u_ref, wb_ref = refs
        x = x_ref[...]

    @pl.when(_first_step())
    def _():
        _cast_rows(w_ref, wb_ref)

    x = x.reshape(TILE_ROWS, D_MODEL)
    h = _rms(x) * gain_ref[...]
    h = (h * (1.0 + sc_ref[...]) + sh_ref[...]).astype(BF16)
    proj_ref[...] = _dot(h, wb_ref[:, :HG_IN_W]).reshape(proj_ref.shape)
    u = _dot(h, wb_ref[:, HG_IN_W:])
    for s in range(TILE_S):
        u_ref[:, s, :] = u[s * TILE_T:(s + 1) * TILE_T, :]


def _tile_spec(width):
    return pl.BlockSpec((TILE_S, TILE_T, width), lambda sb, tb: (sb, tb, 0))


def _tm_tile_spec(width):
    return pl.BlockSpec((TILE_T, TILE_S, width), lambda sb, tb: (tb, sb, 0))


def _mod_spec(layer, cond0, seqs_per_cond, col):
    return pl.BlockSpec((None, None, 1, D_MODEL),
                        lambda sb, tb: (layer, cond0 + (sb * TILE_S) // seqs_per_cond, 0, col))


def _in_call(x3, add_pos, gain, mod4, w_in_b, layer, cond0, seqs_per_cond):
    n_pseq = x3.shape[0]
    in_specs = [_tile_spec(D_MODEL)]
    args = [x3]
    if add_pos:
        assert seqs_per_cond == TILE_S and GRID_W % TILE_T == 0 and S5_SEQ % GRID_W == 0
        nf = D_MODEL // 4
        omega = 1.0 / (np.float32(10000.0) ** (np.arange(nf, dtype=np.float32) / np.float32(nf)))
        in_specs.append(pl.BlockSpec((1, nf), lambda sb, tb: (0, 0)))
        args.append(jnp.asarray(omega.reshape(1, nf), F32))
    in_specs += [
        pl.BlockSpec((1, D_MODEL), lambda sb, tb: (0, 0)),
        _mod_spec(layer, cond0, seqs_per_cond, 0),
        _mod_spec(layer, cond0, seqs_per_cond, 1),
        _layer_spec((D_MODEL, IN_W), layer),
    ]
    args += [gain, mod4, mod4, w_in_b]
    out_specs = [_tile_spec(HG_IN_W), _tm_tile_spec(S5_W)]
    out_shape = [jax.ShapeDtypeStruct((n_pseq, S5_SEQ, HG_IN_W), F32),
                 jax.ShapeDtypeStruct((S5_SEQ, n_pseq, S5_W), F32)]
    if add_pos:
        out_specs.append(_tile_spec(D_MODEL))
        out_shape.append(jax.ShapeDtypeStruct(x3.shape, F32))
    res = pl.pallas_call(
        functools.partial(_in_kernel, add_pos=add_pos),
        grid=(n_pseq // TILE_S, S5_SEQ // TILE_T),
        in_specs=in_specs,
        out_specs=out_specs,
        out_shape=out_shape,
        scratch_shapes=[pltpu.VMEM((D_MODEL, IN_W), BF16)],
        compiler_params=pltpu.CompilerParams(dimension_semantics=("arbitrary", "arbitrary"),
                                             vmem_limit_bytes=VMEM_LIMIT),
        name="in_proj",
    )(*args)
    return (res[0], res[1], res[2]) if add_pos else (res[0], res[1], x3)


def _pair_boundary(b, m, rev):
    c = b.shape[0]
    span = 2 * m
    at = m if rev else m - 1
    if span >= SUBLANES:
        b3 = b.reshape(c // span, span, LANES)
        return jnp.broadcast_to(b3[:, at:at + 1, :], b3.shape).reshape(c, LANES)
    b3 = b.reshape(c // SUBLANES, SUBLANES, LANES)
    sub = lax.broadcasted_iota(jnp.int32, b3.shape, 1)
    out = None
    for p in range(SUBLANES // span):
        piece = jnp.broadcast_to(b3[:, p * span + at:p * span + at + 1, :], b3.shape)
        out = piece if out is None else jnp.where(sub >= p * span, piece, out)
    return out.reshape(c, LANES)


def _neg_abs(x):
    bits = lax.bitcast_convert_type(x, jnp.uint32) | jnp.uint32(0x80000000)
    return lax.bitcast_convert_type(bits, F32)


def _hg_gates(chains, scale):
    outs = []
    for q, fl, lb, tri in chains:
        sig = _sigmoid(fl)
        forget = lb + (1.0 - lb) * sig
        logf = jnp.log2(forget)
        key = (1.0 - lb) * (1.0 - sig)
        hi = logf.astype(BF16)
        r1 = logf - hi.astype(F32)
        mid = r1.astype(BF16)
        lo = (r1 - mid.astype(F32)).astype(BF16)
        parts = _dot(tri, jnp.concatenate([hi, mid, lo], axis=1))
        b2 = parts[:, :LANES] + parts[:, LANES:2 * LANES] + parts[:, 2 * LANES:]
        outs.append((_silu(q) * scale, key, b2, forget))
    return outs


def _hg_scores(chains, code, eye):
    c = chains[0][0].shape[0]
    o_inter = []
    for qh, key, b2, forget, v, st_ref, rev in chains:
        b_edge = b2[0:1, :] if rev else b2[c - 1:c, :]
        st = st_ref[...]
        o_inter.append(_dot_nt((qh * jnp.exp2(b2)).astype(BF16), st.astype(BF16)))
        k_end = key * jnp.exp2(b_edge - b2)
        st_ref[...] = jnp.exp2(b_edge) * st + _dot_tn(v.astype(BF16), k_end.astype(BF16))

    scores = [jnp.where(eye, jnp.sum(ch[0] * ch[1], axis=-1, keepdims=True), 0.0) for ch in chains]
    qkb = [(ch[0].astype(BF16), ch[1].astype(BF16)) for ch in chains]
    for m in HG_LEVELS:
        k = int(math.log2(m)) + 1
        for i, (qh, key, b2, forget, v, st_ref, rev) in enumerate(chains):
            if m == 1:
                p = _dot_nt((qh * forget).astype(BF16), qkb[i][1])
            else:
                e = jnp.exp2(_neg_abs(b2 - _pair_boundary(b2, m, rev))).astype(BF16)
                p = _dot_nt(qkb[i][0] * e, qkb[i][1] * e)
            scores[i] = jnp.where(code == (-k if rev else k), p, scores[i])
    return [(o, sc.astype(BF16)) for o, sc in zip(o_inter, scores)]


def _hgrn_kernel(*refs, layer, n_chunks, zero_init):
    refs = list(refs)
    q_ref, ff_ref, fb_ref, v_ref, g_ref, lbl_ref, gain_ref, code_ref, tri_ref = refs[:9]
    s0_ref = None if zero_init else refs[9]
    o_ref, sfin_ref, st_ref, ob_ref, ab_ref, oi_ref, sc_ref = refs[-7:]
    c = HG_CHUNK
    code = code_ref[...]
    eye = code == 0
    tri_f, tri_b = tri_ref[0], tri_ref[1]

    def lower_bound(d):
        lg = lbl_ref[d]
        ex = jnp.exp(lg - jnp.max(lg, axis=0, keepdims=True))
        soft = ex / jnp.sum(ex, axis=0, keepdims=True)
        return jnp.sum(soft[:layer + 1], axis=0, keepdims=True) - soft[0:1]

    lb_f, lb_b = lower_bound(0), lower_bound(1)
    scale = HG_D ** -0.5

    def rows_of(n):
        return pl.ds(n * c if isinstance(n, int) else pl.multiple_of(n * c, c), c)

    for d in (0, 1):
        st_ref[d] = jnp.zeros((HG_D, HG_D), F32) if zero_init else s0_ref[d].T

    def gates(i):
        rf, rb = rows_of(i), rows_of(n_chunks - 1 - i)
        return _hg_gates([(q_ref[rf, :], ff_ref[rf, :], lb_f, tri_f),
                          (q_ref[rb, :], fb_ref[rb, :], lb_b, tri_b)], scale)

    def scores(i, ab):
        rf, rb = rows_of(i), rows_of(n_chunks - 1 - i)
        return _hg_scores([ab[0] + (v_ref[rf, :], st_ref.at[0], False),
                           ab[1] + (v_ref[rb, :], st_ref.at[1], True)], code, eye)

    def emit(i, oi_sc):
        rf, rb = rows_of(i), rows_of(n_chunks - 1 - i)
        o_ref[rf, :] = oi_sc[0][0] + _dot(oi_sc[0][1], v_ref[rf, :].astype(BF16))
        ob_ref[rb, :] = oi_sc[1][0] + _dot(oi_sc[1][1], v_ref[rb, :].astype(BF16))

    if n_chunks <= 2:
        ab = [gates(i) for i in range(n_chunks)]
        for i in range(n_chunks):
            emit(i, scores(i, ab[i]))
    else:
        n_ab = ab_ref.shape[1]

        def put_ab(ab):
            for ch in range(2):
                for k in range(n_ab):
                    ab_ref[ch, k] = ab[ch][k]

        put_ab(gates(0))
        oi_ref[...] = jnp.zeros(oi_ref.shape, F32)
        sc_ref[...] = jnp.zeros(sc_ref.shape, BF16)

        def body(i, carry):
            emit(jnp.maximum(i - 1, 0), [(oi_ref[ch], sc_ref[ch]) for ch in range(2)])
            res = scores(i, [tuple(ab_ref[ch, k] for k in range(n_ab)) for ch in range(2)])
            for ch in range(2):
                oi_ref[ch] = res[ch][0]
                sc_ref[ch] = res[ch][1]
            put_ab(gates(jnp.minimum(i + 1, n_chunks - 1)))
            return carry

        lax.fori_loop(0, n_chunks, body, 0)
        emit(n_chunks - 1, [(oi_ref[ch], sc_ref[ch]) for ch in range(2)])
    if len(sfin_ref.shape) == 4:
        for other in range(sfin_ref.shape[0]):
            if other != layer:
                sfin_ref[other] = jnp.zeros(sfin_ref.shape[1:], F32)
        sfin_ref = sfin_ref.at[layer]
    for d in (0, 1):
        sfin_ref[d] = st_ref[d].T

    def finish(n, carry):
        rows = pl.ds(pl.multiple_of(n * (2 * c), 2 * c), 2 * c)
        o_ref[rows, :] = _rms(o_ref[rows, :] + ob_ref[rows, :]) * gain_ref[...] * _silu(g_ref[rows, :])
        return carry

    lax.fori_loop(0, n_chunks // 2, finish, 0)


def _hgrn_call(proj, lb_logits, gain, state, finals, layer, n_seq, seq_len):
    rows = proj.shape[0]
    zero_init = state is None

    def col_spec(k):
        return pl.BlockSpec((seq_len, HG_D), lambda b, h: (b, k * HG_HEADS + h))

    assert (seq_len // HG_CHUNK) % 2 == 0
    t, s = np.meshgrid(np.arange(HG_CHUNK), np.arange(HG_CHUNK), indexing="ij")
    lvl = np.where(t == s, 0, np.floor(np.log2(np.maximum(t ^ s, 1))).astype(np.int32) + 1)
    code = jnp.asarray(np.where(t > s, lvl, -lvl), jnp.int32)
    tri = jnp.asarray(np.stack([s <= t, s >= t]), BF16)

    in_specs = [col_spec(0), col_spec(1), col_spec(2), col_spec(3), col_spec(4),
                pl.BlockSpec((2, DEPTH, HG_D), lambda b, h: (0, 0, h)),
                pl.BlockSpec((1, HG_D), lambda b, h: (0, 0)),
                pl.BlockSpec((HG_CHUNK, HG_CHUNK), lambda b, h: (0, 0)),
                pl.BlockSpec((2, HG_CHUNK, HG_CHUNK), lambda b, h: (0, 0, 0))]
    args = [proj] * 5 + [lb_logits, gain, code, tri]
    if not zero_init:
        in_specs.append(pl.BlockSpec((None, None, 2, None, HG_D, HG_D), lambda b, h: (b, layer, 0, h, 0, 0)))
        args.append(state)
    aliases = {}
    if finals is None:
        fin_spec = pl.BlockSpec((None, DEPTH, 2, None, HG_D, HG_D), lambda b, h: (b, 0, 0, h, 0, 0))
    else:
        fin_spec = pl.BlockSpec((None, None, 2, None, HG_D, HG_D), lambda b, h: (b, layer, 0, h, 0, 0))
        aliases[len(args)] = 1
        in_specs.append(pl.BlockSpec(memory_space=pl.ANY))
        args.append(finals)
    return pl.pallas_call(
        functools.partial(_hgrn_kernel, layer=layer, n_chunks=seq_len // HG_CHUNK, zero_init=zero_init),
        grid=(n_seq, HG_HEADS),
        in_specs=in_specs,
        out_specs=[pl.BlockSpec((seq_len, HG_D), lambda b, h: (b, h)), fin_spec],
        out_shape=[jax.ShapeDtypeStruct((rows, HG_W), F32),
                   jax.ShapeDtypeStruct((n_seq, DEPTH, 2, HG_HEADS, HG_D, HG_D), F32)],
        input_output_aliases=aliases,
        scratch_shapes=[pltpu.VMEM((2, HG_D, HG_D), F32), pltpu.VMEM((seq_len, HG_D), F32),
                        pltpu.VMEM((2, 4, HG_CHUNK, HG_D), F32), pltpu.VMEM((2, HG_CHUNK, HG_D), F32),
                        pltpu.VMEM((2, HG_CHUNK, HG_CHUNK), BF16)],
        compiler_params=pltpu.CompilerParams(dimension_semantics=("parallel", "parallel"),
                                             vmem_limit_bytes=VMEM_LIMIT),
        name="hgrn2_mixer",
    )(*args)


def _s5_params_kernel(lr_ref, li_ref, ldt_ref, btr_ref, bti_ref, cr_ref, ci_ref, a_ref, bm_ref, cm_ref):
    sw = S5_SW
    lr = jnp.minimum(lr_ref[...], -1e-4)
    li = li_ref[...]
    dt = jnp.exp(ldt_ref[...])
    mag = jnp.exp(lr * dt)
    ab_re = mag * jnp.cos(li * dt)
    ab_im = mag * jnp.sin(li * dt)
    nr = ab_re - 1.0
    den = lr * lr + li * li
    z_re = (nr * lr + ab_im * li) / den
    z_im = (ab_im * lr - nr * li) / den

    p_idx = lax.broadcasted_iota(jnp.int32, (S5_P, sw), 0)
    col = lax.broadcasted_iota(jnp.int32, (S5_P, sw), 1)
    for g in range(S5_GB):
        place = (col == p_idx + g * S5_P).astype(BF16)
        zr, zi = z_re[g:g + 1, :], z_im[g:g + 1, :]
        btr, bti = btr_ref[g], bti_ref[g]
        rows = slice(g * S5_CH, (g + 1) * S5_CH)
        bm_ref[rows, :sw] = _dot((zr * btr - zi * bti).astype(BF16), place).astype(BF16)
        bm_ref[rows, sw:] = _dot((zr * bti + zi * btr).astype(BF16), place).astype(BF16)
        cm_ref[rows, :sw] = _dot(cr_ref[g].astype(BF16), place).astype(BF16)
        cm_ref[rows, sw:] = _dot((-ci_ref[g]).astype(BF16), place).astype(BF16)
        a_ref[:, g * S5_P:(g + 1) * S5_P] = jnp.broadcast_to(ab_re[g:g + 1, :], (S5_NSEQ, S5_P))
        a_ref[:, sw + g * S5_P:sw + (g + 1) * S5_P] = jnp.broadcast_to(ab_im[g:g + 1, :], (S5_NSEQ, S5_P))


def _s5_params(lam_re, lam_im, log_dt, b_re, b_im, c_re, c_im):
    nb = DEPTH * 2 * S5_NGB
    gp = (nb, S5_GB, S5_P)
    gcp = (nb, S5_GB, S5_CH, S5_P)
    bt_re = jnp.swapaxes(b_re, -1, -2).reshape(gcp)
    bt_im = jnp.swapaxes(b_im, -1, -2).reshape(gcp)
    ldt = jnp.broadcast_to(log_dt.reshape(nb, S5_GB, 1), gp)
    gp_spec = pl.BlockSpec((None, S5_GB, S5_P), lambda i: (i, 0, 0))
    gcp_spec = pl.BlockSpec((None, S5_GB, S5_CH, S5_P), lambda i: (i, 0, 0, 0))
    a, bmat, cmat = pl.pallas_call(
        _s5_params_kernel,
        grid=(nb,),
        in_specs=[gp_spec] * 3 + [gcp_spec] * 4,
        out_specs=[pl.BlockSpec((None, S5_NSEQ, 2 * S5_SW), lambda i: (i, 0, 0)),
                   pl.BlockSpec((None, LANES, 2 * S5_SW), lambda i: (i, 0, 0)),
                   pl.BlockSpec((None, LANES, 2 * S5_SW), lambda i: (i, 0, 0))],
        out_shape=[jax.ShapeDtypeStruct((nb, S5_NSEQ, 2 * S5_SW), F32),
                   jax.ShapeDtypeStruct((nb, LANES, 2 * S5_SW), BF16),
                   jax.ShapeDtypeStruct((nb, LANES, 2 * S5_SW), BF16)],
        compiler_params=pltpu.CompilerParams(dimension_semantics=("parallel",)),
        name="s5_params",
    )(lam_re.reshape(gp), lam_im.reshape(gp), ldt, bt_re, bt_im, c_re.reshape(gcp), c_im.reshape(gcp))
    lead = (DEPTH, 2, S5_NGB)
    return (a.reshape(lead + a.shape[1:]), bmat.reshape(lead + bmat.shape[1:]), cmat.reshape(lead + cmat.shape[1:]))


def _s5_kernel(*refs, want_y, zero_init):
    refs = list(refs)
    u_ref, bm_ref, cm_ref, a_ref, d_ref = refs[:5]
    rest = refs[5:]
    z_ref, s0_ref = (None, None) if zero_init else (rest.pop(0), rest.pop(0))
    y_ref = rest.pop(0) if want_y else None
    hfin_ref, hbuf0, hbuf1, hb0, hb1, hst = rest
    hbufs, hb16s = (hbuf0, hbuf1), (hb0, hb1)
    ns, sw = S5_NSEQ, S5_SW
    half = ns // 2
    n_tc = S5_SEQ // S5_TC
    blk = S5_TC * ns
    dirs = (0, 1)

    if zero_init:
        for d in dirs:
            hst[d] = jnp.zeros((ns, 2 * sw), F32)
    else:
        n_long = s0_ref.shape[1]
        pieces = ns // n_long
        for d in dirs:
            pr, pi = a_ref[d, 0:1, :sw], a_ref[d, 0:1, sw:]
            for _ in range(int(math.log2(S5_SEQ))):
                pr, pi = pr * pr - pi * pi, 2.0 * (pr * pi)
            for b in range(n_long):
                hr, hi = s0_ref[d, b:b + 1, :sw], s0_ref[d, b:b + 1, sw:]
                for k in (range(pieces - 1, -1, -1) if d else range(pieces)):
                    r = b * pieces + k
                    hst[d, r:r + 1, :sw] = hr
                    hst[d, r:r + 1, sw:] = hi
                    zr, zi = z_ref[d, r:r + 1, :sw], z_ref[d, r:r + 1, sw:]
                    hr, hi = pr * hr - pi * hi + zr, pr * hi + pi * hr + zi

    def steps_of(d, i):
        return pl.ds(((n_tc - 1 - i) if d else i) * S5_TC, S5_TC)

    def project(i, slot):
        for d in dirs:
            u = u_ref[steps_of(d, i)].reshape(blk, LANES)
            hbufs[slot][d] = _dot(u.astype(BF16), bm_ref[d])

    def scan(slot):
        a = [(a_ref[d, :half, :sw], a_ref[d, :half, sw:]) for d in dirs]
        h = [[(hst[d, k * half:(k + 1) * half, :sw], hst[d, k * half:(k + 1) * half, sw:]) for k in range(2)]
             for d in dirs]
        for jj in range(S5_TC):
            for d in dirs:
                j = S5_TC - 1 - jj if d else jj
                ar, ai = a[d]
                for k in range(2):
                    r = slice(j * ns + k * half, j * ns + (k + 1) * half)
                    hr, hi = h[d][k]
                    h[d][k] = (ar * hr - ai * hi + hbufs[slot][d, r, :sw], ar * hi + ai * hr + hbufs[slot][d, r, sw:])
                if want_y:
                    r = slice(j * ns, (j + 1) * ns)
                    hb16s[slot][d, r, :sw] = jnp.concatenate([h[d][0][0], h[d][1][0]], axis=0).astype(BF16)
                    hb16s[slot][d, r, sw:] = jnp.concatenate([h[d][0][1], h[d][1][1]], axis=0).astype(BF16)
        for d in dirs:
            for k in range(2):
                hst[d, k * half:(k + 1) * half, :sw] = h[d][k][0]
                hst[d, k * half:(k + 1) * half, sw:] = h[d][k][1]

    def readout(i, slot):
        for d in dirs:
            steps = steps_of(d, i)
            y = _dot_nt(hb16s[slot][d], cm_ref[d])
            y_ref[steps] = y_ref[steps] + y.reshape(S5_TC, ns, LANES)

    project(0, 0)
    if want_y:
        hb1[...] = jnp.zeros(hb1.shape, BF16)

        def skip(n, carry):
            steps = pl.ds(n * S5_TC, S5_TC)
            y_ref[steps] = d_ref[...] * u_ref[steps]
            return carry

        lax.fori_loop(0, n_tc, skip, 0)

    def body(k, carry):
        i = 2 * k
        scan(0)
        if want_y:
            readout(jnp.maximum(i - 1, 0), 1)
        project(i + 1, 1)
        scan(1)
        if want_y:
            readout(i, 0)
        project(jnp.minimum(i + 2, n_tc - 1), 0)
        return carry

    lax.fori_loop(0, n_tc // 2, body, 0)
    for d in dirs:
        hfin_ref[d] = hst[d]
    if want_y:
        readout(n_tc - 1, 1)


def _s5_call(u_tm, a, bmat, cmat, dskip, chain, layer, want_y):
    zero_init = chain is None
    tm_spec = pl.BlockSpec((S5_SEQ, S5_NSEQ, LANES), lambda g: (0, 0, g))

    def mat_spec(rows):
        return pl.BlockSpec((None, 2, None, rows, 2 * S5_SW), lambda g: (layer, 0, g, 0, 0))

    in_specs = [tm_spec, mat_spec(LANES), mat_spec(LANES), mat_spec(S5_NSEQ),
                pl.BlockSpec((None, None, 1, LANES), lambda g: (layer, g, 0, 0))]
    args = [u_tm, bmat, cmat, a, dskip]
    state_spec = pl.BlockSpec((2, None, S5_NSEQ, 2 * S5_SW), lambda g: (0, g, 0, 0))
    state_shape = jax.ShapeDtypeStruct((2, S5_NGB, S5_NSEQ, 2 * S5_SW), F32)
    if not zero_init:
        z, s0 = chain
        in_specs += [state_spec, pl.BlockSpec((2, None, s0.shape[2], 2 * S5_SW), lambda g: (0, g, 0, 0))]
        args += [z, s0]
    out_specs, out_shape = [], []
    if want_y:
        out_specs.append(tm_spec)
        out_shape.append(jax.ShapeDtypeStruct(u_tm.shape, F32))
    out_specs.append(state_spec)
    out_shape.append(state_shape)
    res = pl.pallas_call(
        functools.partial(_s5_kernel, want_y=want_y, zero_init=zero_init),
        grid=(S5_NGB,),
        in_specs=in_specs,
        out_specs=out_specs,
        out_shape=out_shape,
        scratch_shapes=[pltpu.VMEM((2, S5_TC * S5_NSEQ, 2 * S5_SW), F32),
                        pltpu.VMEM((2, S5_TC * S5_NSEQ, 2 * S5_SW), F32),
                        pltpu.VMEM((2, S5_TC * S5_NSEQ, 2 * S5_SW), BF16),
                        pltpu.VMEM((2, S5_TC * S5_NSEQ, 2 * S5_SW), BF16),
                        pltpu.VMEM((2, S5_NSEQ, 2 * S5_SW), F32)],
        compiler_params=pltpu.CompilerParams(dimension_semantics=("parallel",),
                                             vmem_limit_bytes=VMEM_LIMIT),
        name="s5_scan",
    )(*args)
    return (res[0], res[1]) if want_y else (None, res[0])


def _out_kernel(x_ref, ohg_ref, y5_ref, g1_ref, sh2_ref, sc2_ref, g2_ref, nffn_ref, nfin_ref,
                wglu_ref, wout_ref, wg_ref, wu_ref, wd_ref, o_ref, wglu_b, wout_b, *, final_norm):
    @pl.when(_first_step())
    def _():
        _cast_rows(wglu_ref, wglu_b)
        _cast_rows(wout_ref, wout_b)

    y = jnp.concatenate([y5_ref[:, s, :] for s in range(TILE_S)], axis=0)
    y = _gelu_tanh(y)
    y = y * _sigmoid(_dot(y.astype(BF16), wglu_b[...]))
    ohg = ohg_ref[...].reshape(TILE_ROWS, HG_W)
    mix = _dot(ohg.astype(BF16), wout_b[:HG_W, :]) + _dot(y.astype(BF16), wout_b[HG_W:, :])
    x = x_ref[...].reshape(TILE_ROWS, D_MODEL) + g1_ref[...] * mix
    h = _rms(x) * nffn_ref[...]
    h = (h * (1.0 + sc2_ref[...]) + sh2_ref[...]).astype(BF16)
    act = (_silu(_dot(h, wg_ref[...])) * _dot(h, wu_ref[...])).astype(BF16)
    x = x + g2_ref[...] * _dot(act, wd_ref[...])
    if final_norm:
        x = _rms(x) * nfin_ref[...]
    o_ref[...] = x.reshape(o_ref.shape)


def _out_call(x3, ohg3, y5_tm, mod4, nffn, nfin, wglu, wout, wg, wu, wd, layer, cond0, seqs_per_cond, final_norm):
    vec = pl.BlockSpec((1, D_MODEL), lambda sb, tb: (0, 0))
    return pl.pallas_call(
        functools.partial(_out_kernel, final_norm=final_norm),
        grid=(x3.shape[0] // TILE_S, S5_SEQ // TILE_T),
        in_specs=[_tile_spec(D_MODEL), _tile_spec(HG_W), _tm_tile_spec(S5_W),
                  _mod_spec(layer, cond0, seqs_per_cond, 2),
                  _mod_spec(layer, cond0, seqs_per_cond, 3),
                  _mod_spec(layer, cond0, seqs_per_cond, 4),
                  _mod_spec(layer, cond0, seqs_per_cond, 5),
                  vec, vec,
                  _layer_spec((S5_W, S5_W), layer), _layer_spec((D_MODEL, D_MODEL), layer),
                  _layer_spec((D_MODEL, D_FF), layer), _layer_spec((D_MODEL, D_FF), layer),
                  _layer_spec((D_FF, D_MODEL), layer)],
        out_specs=_tile_spec(D_MODEL),
        out_shape=jax.ShapeDtypeStruct(x3.shape, F32),
        scratch_shapes=[pltpu.VMEM((S5_W, S5_W), BF16), pltpu.VMEM((D_MODEL, D_MODEL), BF16)],
        compiler_params=pltpu.CompilerParams(dimension_semantics=("arbitrary", "arbitrary"),
                                             vmem_limit_bytes=VMEM_LIMIT),
        name="out_ffn",
    )(x3, ohg3, y5_tm, mod4, mod4, mod4, mod4, nffn, nfin, wglu, wout, wg, wu, wd)


def _s5_state_to_blocks(s):
    n = s.shape[0]
    s = s.reshape(n, 2, S5_NGB, S5_GB, S5_P, 2)
    return jnp.transpose(s, (1, 2, 0, 5, 3, 4)).reshape(2, S5_NGB, n, 2 * S5_SW)


def _s5_blocks_to_state(h):
    n = h.shape[2]
    h = h.reshape(2, S5_NGB, n, 2, S5_GB, S5_P)
    return jnp.transpose(h, (2, 0, 1, 4, 5, 3)).reshape(n, 2, S5_GROUPS, S5_P, 2)


def kernel(x_prompt, x_sample, state_hgrn, state_s5, c, c_ctx, w_mod, b_mod, norm_mix, norm_ffn, norm_final, w_in, w_out, hg_lb_logits, hg_norm, s5_lam_re, s5_lam_im, s5_log_dt, s5_b_re, s5_b_im, s5_c_re, s5_c_im, s5_d, s5_w_glu, w_gate, w_up, w_down):
    n_ctx, ctx_len, _ = x_prompt.shape
    n_dec, dec_len, _ = x_sample.shape
    assert ctx_len == S5_SEQ and n_ctx == S5_NSEQ and n_dec * dec_len == S5_NSEQ * S5_SEQ

    cond = jnp.concatenate([c_ctx[None, :], c, jnp.zeros((SUBLANES - 1 - n_dec, D_MODEL), F32)], axis=0)
    mod4 = _mod_call(cond, w_mod, b_mod).reshape(DEPTH, SUBLANES, 1, 6 * D_MODEL)

    w_gate_b, w_up_b, w_down_b = w_gate.astype(BF16), w_up.astype(BF16), w_down.astype(BF16)
    s5_a, s5_bmat, s5_cmat = _s5_params(s5_lam_re, s5_lam_im, s5_log_dt, s5_b_re, s5_b_im, s5_c_re, s5_c_im)
    s5_dskip = s5_d.reshape(DEPTH, S5_NGB, 1, LANES)
    nfin = norm_final.reshape(1, D_MODEL)

    def run(x3, add_pos, n_seq, seq_len, cond0, hg_state, s5_state):
        pieces = seq_len // S5_SEQ
        seqs_per_cond = S5_NSEQ if cond0 == 0 else pieces
        rows = S5_NSEQ * S5_SEQ
        hg_fin, s5_finals = None, []
        for l in range(DEPTH):
            proj3, u_tm, x3 = _in_call(x3, add_pos and l == 0, norm_mix[l].reshape(1, D_MODEL), mod4, w_in,
                                       l, cond0, seqs_per_cond)
            ohg, hg_fin = _hgrn_call(proj3.reshape(rows, HG_IN_W), hg_lb_logits, hg_norm[l].reshape(1, HG_D),
                                     hg_state, hg_fin, l, n_seq, seq_len)
            if s5_state is None:
                y5, s5_fin = _s5_call(u_tm, s5_a, s5_bmat, s5_cmat, s5_dskip, None, l, True)
            else:
                _, z = _s5_call(u_tm, s5_a, s5_bmat, s5_cmat, s5_dskip, None, l, False)
                y5, s5_fin = _s5_call(u_tm, s5_a, s5_bmat, s5_cmat, s5_dskip,
                                      (z, _s5_state_to_blocks(s5_state[:, l])), l, True)
            x3 = _out_call(x3, ohg.reshape(S5_NSEQ, S5_SEQ, HG_W), y5, mod4, norm_ffn[l].reshape(1, D_MODEL), nfin,
                           s5_w_glu, w_out, w_gate_b, w_up_b, w_down_b, l, cond0, seqs_per_cond, l == DEPTH - 1)
            s5_finals.append(_s5_blocks_to_state(s5_fin))
        return x3, hg_fin, s5_finals

    tok = (S5_NSEQ, S5_SEQ, D_MODEL)
    y_prompt, new_state_hgrn, s5_finals = run(x_prompt.reshape(tok), False, n_ctx, ctx_len, 0, None, None)
    y_sample, _, _ = run(x_sample.reshape(tok), True, n_dec, dec_len, 1, state_hgrn, state_s5)
    return (y_prompt.reshape(x_prompt.shape), y_sample.reshape(x_sample.shape),
            new_state_hgrn, jnp.stack(s5_finals, axis=1))
```

```python
import functools
import math

import jax
import jax.numpy as jnp
import numpy as np
from jax import lax
from jax.experimental import pallas as pl
from jax.experimental.pallas import tpu as pltpu

F32 = jnp.float32
BF16 = jnp.bfloat16

LANES = 128
SUBLANES = 8

D_MODEL = 1024
DEPTH = 2
GRID_W = 64
HG_W = 512
HG_HEADS = 4
HG_D = HG_W // HG_HEADS
S5_W = 512
S5_CH = 16
S5_GROUPS = S5_W // S5_CH
S5_P = 64
S5_GB = LANES // S5_CH
S5_NGB = S5_GROUPS // S5_GB
S5_SW = S5_GB * S5_P
HG_IN_W = 5 * HG_W
IN_W = HG_IN_W + S5_W
D_FF = 2816
EPS = 1e-6

HG_CHUNK = 128
HG_LEVELS = (64, 32, 16, 8, 4, 2, 1)
S5_SEQ = 256
S5_NSEQ = 16
S5_TC = 32

TILE_S = SUBLANES
TILE_T = 64
TILE_ROWS = TILE_S * TILE_T
CAST_ROWS = 128
MOD_TILE_N = 1536
VMEM_LIMIT = 56 * 1024 * 1024


def _sigmoid(x):
    return 1.0 / (1.0 + jnp.exp(-x))


def _silu(x):
    return x * _sigmoid(x)


def _gelu_tanh(x):
    return 0.5 * x * (1.0 + jnp.tanh(math.sqrt(2.0 / math.pi) * (x + 0.044715 * (x * x * x))))


def _rms(x):
    return x * lax.rsqrt(jnp.mean(x * x, axis=-1, keepdims=True) + EPS)


def _dot(a, b):
    return jnp.dot(a, b, preferred_element_type=F32)


def _dot_nt(a, b):
    return lax.dot_general(a, b, (((1,), (1,)), ((), ())), preferred_element_type=F32)


def _dot_tn(a, b):
    return lax.dot_general(a, b, (((0,), (0,)), ((), ())), preferred_element_type=F32)


def _layer_spec(shape, layer):
    nd = len(shape)
    return pl.BlockSpec((None,) + tuple(shape), lambda *_: (layer,) + (0,) * nd, pipeline_mode=pl.Buffered(1))


def _mod_kernel(cond_ref, w_ref, b_ref, o_ref):
    a = _silu(cond_ref[...]).astype(BF16)
    o_ref[0] = _dot(a, w_ref[0].astype(BF16)) + b_ref[0]


def _mod_call(cond, w_mod, b_mod):
    n_cond = cond.shape[0]
    n_out = w_mod.shape[-1]
    return pl.pallas_call(
        _mod_kernel,
        grid=(DEPTH, n_out // MOD_TILE_N),
        in_specs=[
            pl.BlockSpec((n_cond, D_MODEL), lambda l, j: (0, 0)),
            pl.BlockSpec((1, D_MODEL, MOD_TILE_N), lambda l, j: (l, 0, j)),
            pl.BlockSpec((1, 1, MOD_TILE_N), lambda l, j: (l, 0, j)),
        ],
        out_specs=pl.BlockSpec((1, n_cond, MOD_TILE_N), lambda l, j: (l, 0, j)),
        out_shape=jax.ShapeDtypeStruct((DEPTH, n_cond, n_out), F32),
        compiler_params=pltpu.CompilerParams(dimension_semantics=("parallel", "parallel"),
                                             vmem_limit_bytes=VMEM_LIMIT),
        name="adaln_mod",
    )(cond, w_mod, b_mod.reshape(DEPTH, 1, n_out))


def _first_step():
    return jnp.logical_and(pl.program_id(0) == 0, pl.program_id(1) == 0)


def _cast_rows(src_ref, dst_ref):
    for r in range(0, src_ref.shape[0], CAST_ROWS):
        dst_ref[r:r + CAST_ROWS, :] = src_ref[r:r + CAST_ROWS, :].astype(BF16)


def _grid_pos_tile(omega, tb):
    nf = omega.shape[-1]
    s_idx = lax.broadcasted_iota(jnp.int32, (TILE_S, nf), 0)
    j_idx = lax.broadcasted_iota(jnp.int32, (TILE_T, nf), 0)
    t0 = tb * TILE_T
    row = (s_idx * (S5_SEQ // GRID_W) + t0 // GRID_W).astype(F32) * omega
    col = (j_idx + t0 % GRID_W).astype(F32) * omega
    enc_r = jnp.concatenate([jnp.sin(row), jnp.cos(row)], axis=-1)
    enc_c = jnp.concatenate([jnp.sin(col), jnp.cos(col)], axis=-1)
    shape = (TILE_S, TILE_T, 2 * nf)
    return jnp.concatenate([jnp.broadcast_to(enc_r[:, None, :], shape),
                            jnp.broadcast_to(enc_c[None, :, :], shape)], axis=-1)


def _in_kernel(*refs, add_pos):
    if add_pos:
        x_ref, om_ref, gain_ref, sh_ref, sc_ref, w_ref, proj_ref, u_ref, xs_ref, wb_ref = refs
        x = x_ref[...] + _grid_pos_tile(om_ref[...], pl.program_id(1))
        xs_ref[...] = x
    else:
        x_ref, gain_ref, sh_ref, sc_ref, w_ref, proj_ref, u_ref, wb_ref = refs
        x = x_ref[...]

    @pl.when(_first_step())
    def _():
        _cast_rows(w_ref, wb_ref)

    x = x.reshape(TILE_ROWS, D_MODEL)
    h = _rms(x) * gain_ref[...]
    h = (h * (1.0 + sc_ref[...]) + sh_ref[...]).astype(BF16)
    proj_ref[...] = _dot(h, wb_ref[:, :HG_IN_W]).reshape(proj_ref.shape)
    u = _dot(h, wb_ref[:, HG_IN_W:])
    for s in range(TILE_S):
        u_ref[:, s, :] = u[s * TILE_T:(s + 1) * TILE_T, :]


def _tile_spec(width):
    return pl.BlockSpec((TILE_S, TILE_T, width), lambda sb, tb: (sb, tb, 0))


def _tm_tile_spec(width):
    return pl.BlockSpec((TILE_T, TILE_S, width), lambda sb, tb: (tb, sb, 0))


def _mod_spec(layer, cond0, seqs_per_cond, col):
    return pl.BlockSpec((None, None, 1, D_MODEL),
                        lambda sb, tb: (layer, cond0 + (sb * TILE_S) // seqs_per_cond, 0, col))


def _in_call(x3, add_pos, gain, mod4, w_in_b, layer, cond0, seqs_per_cond):
    n_pseq = x3.shape[0]
    in_specs = [_tile_spec(D_MODEL)]
    args = [x3]
    if add_pos:
        assert seqs_per_cond == TILE_S and GRID_W % TILE_T == 0 and S5_SEQ % GRID_W == 0
        nf = D_MODEL // 4
        omega = 1.0 / (np.float32(10000.0) ** (np.arange(nf, dtype=np.float32) / np.float32(nf)))
        in_specs.append(pl.BlockSpec((1, nf), lambda sb, tb: (0, 0)))
        args.append(jnp.asarray(omega.reshape(1, nf), F32))
    in_specs += [
        pl.BlockSpec((1, D_MODEL), lambda sb, tb: (0, 0)),
        _mod_spec(layer, cond0, seqs_per_cond, 0),
        _mod_spec(layer, cond0, seqs_per_cond, 1),
        _layer_spec((D_MODEL, IN_W), layer),
    ]
    args += [gain, mod4, mod4, w_in_b]
    out_specs = [_tile_spec(HG_IN_W), _tm_tile_spec(S5_W)]
    out_shape = [jax.ShapeDtypeStruct((n_pseq, S5_SEQ, HG_IN_W), F32),
                 jax.ShapeDtypeStruct((S5_SEQ, n_pseq, S5_W), F32)]
    if add_pos:
        out_specs.append(_tile_spec(D_MODEL))
        out_shape.append(jax.ShapeDtypeStruct(x3.shape, F32))
    res = pl.pallas_call(
        functools.partial(_in_kernel, add_pos=add_pos),
        grid=(n_pseq // TILE_S, S5_SEQ // TILE_T),
        in_specs=in_specs,
        out_specs=out_specs,
        out_shape=out_shape,
        scratch_shapes=[pltpu.VMEM((D_MODEL, IN_W), BF16)],
        compiler_params=pltpu.CompilerParams(dimension_semantics=("arbitrary", "arbitrary"),
                                             vmem_limit_bytes=VMEM_LIMIT),
        name="in_proj",
    )(*args)
    return (res[0], res[1], res[2]) if add_pos else (res[0], res[1], x3)


def _pair_boundary(b, m, rev):
    c = b.shape[0]
    span = 2 * m
    at = m if rev else m - 1
    if span >= SUBLANES:
        b3 = b.reshape(c // span, span, LANES)
        return jnp.broadcast_to(b3[:, at:at + 1, :], b3.shape).reshape(c, LANES)
    b3 = b.reshape(c // SUBLANES, SUBLANES, LANES)
    sub = lax.broadcasted_iota(jnp.int32, b3.shape, 1)
    out = None
    for p in range(SUBLANES // span):
        piece = jnp.broadcast_to(b3[:, p * span + at:p * span + at + 1, :], b3.shape)
        out = piece if out is None else jnp.where(sub >= p * span, piece, out)
    return out.reshape(c, LANES)


def _neg_abs(x):
    bits = lax.bitcast_convert_type(x, jnp.uint32) | jnp.uint32(0x80000000)
    return lax.bitcast_convert_type(bits, F32)


def _hg_gates(chains, scale):
    outs = []
    for q, fl, lb, tri in chains:
        sig = _sigmoid(fl)
        forget = lb + (1.0 - lb) * sig
        logf = jnp.log2(forget)
        key = (1.0 - lb) * (1.0 - sig)
        hi = logf.astype(BF16)
        r1 = logf - hi.astype(F32)
        mid = r1.astype(BF16)
        lo = (r1 - mid.astype(F32)).astype(BF16)
        parts = _dot(tri, jnp.concatenate([hi, mid, lo], axis=1))
        b2 = parts[:, :LANES] + parts[:, LANES:2 * LANES] + parts[:, 2 * LANES:]
        outs.append((_silu(q) * scale, key, b2, forget))
    return outs


def _hg_scores(chains, code, eye):
    c = chains[0][0].shape[0]
    o_inter = []
    for qh, key, b2, forget, v, st_ref, rev in chains:
        b_edge = b2[0:1, :] if rev else b2[c - 1:c, :]
        st = st_ref[...]
        o_inter.append(_dot_nt((qh * jnp.exp2(b2)).astype(BF16), st.astype(BF16)))
        k_end = key * jnp.exp2(b_edge - b2)
        st_ref[...] = jnp.exp2(b_edge) * st + _dot_tn(v.astype(BF16), k_end.astype(BF16))

    scores = [jnp.where(eye, jnp.sum(ch[0] * ch[1], axis=-1, keepdims=True), 0.0) for ch in chains]
    qkb = [(ch[0].astype(BF16), ch[1].astype(BF16)) for ch in chains]
    for m in HG_LEVELS:
        k = int(math.log2(m)) + 1
        for i, (qh, key, b2, forget, v, st_ref, rev) in enumerate(chains):
            if m == 1:
                p = _dot_nt((qh * forget).astype(BF16), qkb[i][1])
            else:
                e = jnp.exp2(_neg_abs(b2 - _pair_boundary(b2, m, rev))).astype(BF16)
                p = _dot_nt(qkb[i][0] * e, qkb[i][1] * e)
            scores[i] = jnp.where(code == (-k if rev else k), p, scores[i])
    return [(o, sc.astype(BF16)) for o, sc in zip(o_inter, scores)]


def _hgrn_kernel(*refs, layer, n_chunks, n_heads, zero_init):
    refs = list(refs)
    q_ref, ff_ref, fb_ref, v_ref, g_ref, lbl_ref, gain_ref, code_ref, tri_ref = refs[:9]
    s0_ref = None if zero_init else refs[9]
    o_ref, sfin_ref, st_ref, ob_ref, ab_ref, oi_ref, sc_ref = refs[-7:]
    c = HG_CHUNK
    code = code_ref[...]
    eye = code == 0
    chains = [(hd, d) for hd in range(n_heads) for d in (0, 1)]

    def lanes(hd):
        return slice(hd * HG_D, (hd + 1) * HG_D)

    def lower_bound(hd, d):
        lg = lbl_ref[d, :, lanes(hd)]
        ex = jnp.exp(lg - jnp.max(lg, axis=0, keepdims=True))
        soft = ex / jnp.sum(ex, axis=0, keepdims=True)
        return jnp.sum(soft[:layer + 1], axis=0, keepdims=True) - soft[0:1]

    lb = [lower_bound(hd, d) for hd, d in chains]
    scale = HG_D ** -0.5

    def rows_of(d, i):
        n = (n_chunks - 1 - i) if d else i
        return pl.ds(n * c if isinstance(n, int) else pl.multiple_of(n * c, c), c)

    for ch, (hd, d) in enumerate(chains):
        st_ref[ch] = jnp.zeros((HG_D, HG_D), F32) if zero_init else s0_ref[d, hd].T

    def gates(i):
        return _hg_gates([(q_ref[rows_of(d, i), lanes(hd)], (fb_ref if d else ff_ref)[rows_of(d, i), lanes(hd)],
                           lb[ch], tri_ref[d]) for ch, (hd, d) in enumerate(chains)], scale)

    def scores(i, ab):
        return _hg_scores([ab[ch] + (v_ref[rows_of(d, i), lanes(hd)], st_ref.at[ch], bool(d))
                           for ch, (hd, d) in enumerate(chains)], code, eye)

    def emit(i, oi_sc):
        for ch, (hd, d) in enumerate(chains):
            rows = rows_of(d, i)
            (ob_ref if d else o_ref)[rows, lanes(hd)] = (
                oi_sc[ch][0] + _dot(oi_sc[ch][1], v_ref[rows, lanes(hd)].astype(BF16)))

    if n_chunks <= 2:
        ab = [gates(i) for i in range(n_chunks)]
        for i in range(n_chunks):
            emit(i, scores(i, ab[i]))
    else:
        n_ab = ab_ref.shape[1]

        def put_ab(ab):
            for ch in range(len(chains)):
                for k in range(n_ab):
                    ab_ref[ch, k] = ab[ch][k]

        def get_oi_sc():
            return [(oi_ref[ch], sc_ref[ch]) for ch in range(len(chains))]

        put_ab(gates(0))
        oi_ref[...] = jnp.zeros(oi_ref.shape, F32)
        sc_ref[...] = jnp.zeros(sc_ref.shape, BF16)

        def body(i, carry):
            emit(jnp.maximum(i - 1, 0), get_oi_sc())
            res = scores(i, [tuple(ab_ref[ch, k] for k in range(n_ab)) for ch in range(len(chains))])
            for ch in range(len(chains)):
                oi_ref[ch] = res[ch][0]
                sc_ref[ch] = res[ch][1]
            put_ab(gates(jnp.minimum(i + 1, n_chunks - 1)))
            return carry

        lax.fori_loop(0, n_chunks, body, 0)
        emit(n_chunks - 1, get_oi_sc())
    if len(sfin_ref.shape) == 5:
        for other in range(sfin_ref.shape[0]):
            if other != layer:
                sfin_ref[other] = jnp.zeros(sfin_ref.shape[1:], F32)
        sfin_ref = sfin_ref.at[layer]
    for ch, (hd, d) in enumerate(chains):
        sfin_ref[d, hd] = st_ref[ch].T

    def finish(n, carry):
        rows = pl.ds(pl.multiple_of(n * (2 * c), 2 * c), 2 * c)
        for hd in range(n_heads):
            o_ref[rows, lanes(hd)] = (_rms(o_ref[rows, lanes(hd)] + ob_ref[rows, lanes(hd)]) * gain_ref[...]
                                      * _silu(g_ref[rows, lanes(hd)]))
        return carry

    lax.fori_loop(0, n_chunks // 2, finish, 0)


def _hgrn_call(proj, lb_logits, gain, state, finals, layer, n_seq, seq_len):
    rows = proj.shape[0]
    zero_init = state is None
    n_chunks = seq_len // HG_CHUNK
    assert n_chunks % 2 == 0
    nh = 2 if n_chunks <= 2 else 1
    hw = nh * HG_D
    n_hb = HG_HEADS // nh

    def col_spec(k):
        return pl.BlockSpec((seq_len, hw), lambda b, h: (b, k * n_hb + h))

    t, s = np.meshgrid(np.arange(HG_CHUNK), np.arange(HG_CHUNK), indexing="ij")
    lvl = np.where(t == s, 0, np.floor(np.log2(np.maximum(t ^ s, 1))).astype(np.int32) + 1)
    code = jnp.asarray(np.where(t > s, lvl, -lvl), jnp.int32)
    tri = jnp.asarray(np.stack([s <= t, s >= t]), BF16)

    in_specs = [col_spec(0), col_spec(1), col_spec(2), col_spec(3), col_spec(4),
                pl.BlockSpec((2, DEPTH, hw), lambda b, h: (0, 0, h)),
                pl.BlockSpec((1, HG_D), lambda b, h: (0, 0)),
                pl.BlockSpec((HG_CHUNK, HG_CHUNK), lambda b, h: (0, 0)),
                pl.BlockSpec((2, HG_CHUNK, HG_CHUNK), lambda b, h: (0, 0, 0))]
    args = [proj] * 5 + [lb_logits, gain, code, tri]
    if not zero_init:
        in_specs.append(pl.BlockSpec((None, None, 2, nh, HG_D, HG_D), lambda b, h: (b, layer, 0, h, 0, 0)))
        args.append(state)
    aliases = {}
    if finals is None:
        fin_spec = pl.BlockSpec((None, DEPTH, 2, nh, HG_D, HG_D), lambda b, h: (b, 0, 0, h, 0, 0))
    else:
        fin_spec = pl.BlockSpec((None, None, 2, nh, HG_D, HG_D), lambda b, h: (b, layer, 0, h, 0, 0))
        aliases[len(args)] = 1
        in_specs.append(pl.BlockSpec(memory_space=pl.ANY))
        args.append(finals)
    n_ch = 2 * nh
    return pl.pallas_call(
        functools.partial(_hgrn_kernel, layer=layer, n_chunks=n_chunks, n_heads=nh, zero_init=zero_init),
        grid=(n_seq, n_hb),
        in_specs=in_specs,
        out_specs=[pl.BlockSpec((seq_len, hw), lambda b, h: (b, h)), fin_spec],
        out_shape=[jax.ShapeDtypeStruct((rows, HG_W), F32),
                   jax.ShapeDtypeStruct((n_seq, DEPTH, 2, HG_HEADS, HG_D, HG_D), F32)],
        input_output_aliases=aliases,
        scratch_shapes=[pltpu.VMEM((n_ch, HG_D, HG_D), F32), pltpu.VMEM((seq_len, hw), F32),
                        pltpu.VMEM((n_ch, 4, HG_CHUNK, HG_D), F32), pltpu.VMEM((n_ch, HG_CHUNK, HG_D), F32),
                        pltpu.VMEM((n_ch, HG_CHUNK, HG_CHUNK), BF16)],
        compiler_params=pltpu.CompilerParams(dimension_semantics=("parallel", "parallel"),
                                             vmem_limit_bytes=VMEM_LIMIT),
        name="hgrn2_mixer",
    )(*args)


def _s5_params_kernel(lr_ref, li_ref, ldt_ref, btr_ref, bti_ref, cr_ref, ci_ref, a_ref, bm_ref, cm_ref):
    sw = S5_SW
    lr = jnp.minimum(lr_ref[...], -1e-4)
    li = li_ref[...]
    dt = jnp.exp(ldt_ref[...])
    mag = jnp.exp(lr * dt)
    ab_re = mag * jnp.cos(li * dt)
    ab_im = mag * jnp.sin(li * dt)
    nr = ab_re - 1.0
    den = lr * lr + li * li
    z_re = (nr * lr + ab_im * li) / den
    z_im = (ab_im * lr - nr * li) / den

    p_idx = lax.broadcasted_iota(jnp.int32, (S5_P, sw), 0)
    col = lax.broadcasted_iota(jnp.int32, (S5_P, sw), 1)
    for g in range(S5_GB):
        place = (col == p_idx + g * S5_P).astype(BF16)
        zr, zi = z_re[g:g + 1, :], z_im[g:g + 1, :]
        btr, bti = btr_ref[g], bti_ref[g]
        rows = slice(g * S5_CH, (g + 1) * S5_CH)
        bm_ref[rows, :sw] = _dot((zr * btr - zi * bti).astype(BF16), place).astype(BF16)
        bm_ref[rows, sw:] = _dot((zr * bti + zi * btr).astype(BF16), place).astype(BF16)
        cm_ref[rows, :sw] = _dot(cr_ref[g].astype(BF16), place).astype(BF16)
        cm_ref[rows, sw:] = _dot((-ci_ref[g]).astype(BF16), place).astype(BF16)
        a_ref[:, g * S5_P:(g + 1) * S5_P] = jnp.broadcast_to(ab_re[g:g + 1, :], (S5_NSEQ, S5_P))
        a_ref[:, sw + g * S5_P:sw + (g + 1) * S5_P] = jnp.broadcast_to(ab_im[g:g + 1, :], (S5_NSEQ, S5_P))


def _s5_params(lam_re, lam_im, log_dt, b_re, b_im, c_re, c_im):
    nb = DEPTH * 2 * S5_NGB
    gp = (nb, S5_GB, S5_P)
    gcp = (nb, S5_GB, S5_CH, S5_P)
    bt_re = jnp.swapaxes(b_re, -1, -2).reshape(gcp)
    bt_im = jnp.swapaxes(b_im, -1, -2).reshape(gcp)
    ldt = jnp.broadcast_to(log_dt.reshape(nb, S5_GB, 1), gp)
    gp_spec = pl.BlockSpec((None, S5_GB, S5_P), lambda i: (i, 0, 0))
    gcp_spec = pl.BlockSpec((None, S5_GB, S5_CH, S5_P), lambda i: (i, 0, 0, 0))
    a, bmat, cmat = pl.pallas_call(
        _s5_params_kernel,
        grid=(nb,),
        in_specs=[gp_spec] * 3 + [gcp_spec] * 4,
        out_specs=[pl.BlockSpec((None, S5_NSEQ, 2 * S5_SW), lambda i: (i, 0, 0)),
                   pl.BlockSpec((None, LANES, 2 * S5_SW), lambda i: (i, 0, 0)),
                   pl.BlockSpec((None, LANES, 2 * S5_SW), lambda i: (i, 0, 0))],
        out_shape=[jax.ShapeDtypeStruct((nb, S5_NSEQ, 2 * S5_SW), F32),
                   jax.ShapeDtypeStruct((nb, LANES, 2 * S5_SW), BF16),
                   jax.ShapeDtypeStruct((nb, LANES, 2 * S5_SW), BF16)],
        compiler_params=pltpu.CompilerParams(dimension_semantics=("parallel",)),
        name="s5_params",
    )(lam_re.reshape(gp), lam_im.reshape(gp), ldt, bt_re, bt_im, c_re.reshape(gcp), c_im.reshape(gcp))
    lead = (DEPTH, 2, S5_NGB)
    return (a.reshape(lead + a.shape[1:]), bmat.reshape(lead + bmat.shape[1:]), cmat.reshape(lead + cmat.shape[1:]))


def _s5_kernel(*refs, want_y, zero_init):
    refs = list(refs)
    u_ref, bm_ref, cm_ref, a_ref, d_ref = refs[:5]
    rest = refs[5:]
    z_ref, s0_ref = (None, None) if zero_init else (rest.pop(0), rest.pop(0))
    y_ref = rest.pop(0) if want_y else None
    hfin_ref, hbuf0, hbuf1, hb0, hb1, hst = rest
    hbufs, hb16s = (hbuf0, hbuf1), (hb0, hb1)
    ns, sw = S5_NSEQ, S5_SW
    half = ns // 2
    n_tc = S5_SEQ // S5_TC
    blk = S5_TC * ns
    dirs = (0, 1)

    if zero_init:
        for d in dirs:
            hst[d] = jnp.zeros((ns, 2 * sw), F32)
    else:
        n_long = s0_ref.shape[1]
        pieces = ns // n_long
        for d in dirs:
            pr, pi = a_ref[d, 0:1, :sw], a_ref[d, 0:1, sw:]
            for _ in range(int(math.log2(S5_SEQ))):
                pr, pi = pr * pr - pi * pi, 2.0 * (pr * pi)
            for b in range(n_long):
                hr, hi = s0_ref[d, b:b + 1, :sw], s0_ref[d, b:b + 1, sw:]
                for k in (range(pieces - 1, -1, -1) if d else range(pieces)):
                    r = b * pieces + k
                    hst[d, r:r + 1, :sw] = hr
                    hst[d, r:r + 1, sw:] = hi
                    zr, zi = z_ref[d, r:r + 1, :sw], z_ref[d, r:r + 1, sw:]
                    hr, hi = pr * hr - pi * hi + zr, pr * hi + pi * hr + zi

    def steps_of(d, i):
        return pl.ds(((n_tc - 1 - i) if d else i) * S5_TC, S5_TC)

    def project(i, slot):
        for d in dirs:
            u = u_ref[steps_of(d, i)].reshape(blk, LANES)
            hbufs[slot][d] = _dot(u.astype(BF16), bm_ref[d])

    def scan(slot):
        a = [(a_ref[d, :half, :sw], a_ref[d, :half, sw:]) for d in dirs]
        h = [[(hst[d, k * half:(k + 1) * half, :sw], hst[d, k * half:(k + 1) * half, sw:]) for k in range(2)]
             for d in dirs]
        for jj in range(S5_TC):
            for d in dirs:
                j = S5_TC - 1 - jj if d else jj
                ar, ai = a[d]
                for k in range(2):
                    r = slice(j * ns + k * half, j * ns + (k + 1) * half)
                    hr, hi = h[d][k]
                    h[d][k] = (ar * hr - ai * hi + hbufs[slot][d, r, :sw], ar * hi + ai * hr + hbufs[slot][d, r, sw:])
                if want_y:
                    r = slice(j * ns, (j + 1) * ns)
                    hb16s[slot][d, r, :sw] = jnp.concatenate([h[d][0][0], h[d][1][0]], axis=0).astype(BF16)
                    hb16s[slot][d, r, sw:] = jnp.concatenate([h[d][0][1], h[d][1][1]], axis=0).astype(BF16)
        for d in dirs:
            for k in range(2):
                hst[d, k * half:(k + 1) * half, :sw] = h[d][k][0]
                hst[d, k * half:(k + 1) * half, sw:] = h[d][k][1]

    def readout(i, slot):
        for d in dirs:
            steps = steps_of(d, i)
            y = _dot_nt(hb16s[slot][d], cm_ref[d])
            y_ref[steps] = y_ref[steps] + y.reshape(S5_TC, ns, LANES)

    project(0, 0)
    if want_y:
        hb1[...] = jnp.zeros(hb1.shape, BF16)

        def skip(n, carry):
            steps = pl.ds(n * S5_TC, S5_TC)
            y_ref[steps] = d_ref[...] * u_ref[steps]
            return carry

        lax.fori_loop(0, n_tc, skip, 0)

    def body(k, carry):
        i = 2 * k
        scan(0)
        if want_y:
            readout(jnp.maximum(i - 1, 0), 1)
        project(i + 1, 1)
        scan(1)
        if want_y:
            readout(i, 0)
        project(jnp.minimum(i + 2, n_tc - 1), 0)
        return carry

    lax.fori_loop(0, n_tc // 2, body, 0)
    for d in dirs:
        hfin_ref[d] = hst[d]
    if want_y:
        readout(n_tc - 1, 1)


def _s5_call(u_tm, a, bmat, cmat, dskip, chain, layer, want_y):
    zero_init = chain is None
    tm_spec = pl.BlockSpec((S5_SEQ, S5_NSEQ, LANES), lambda g: (0, 0, g))

    def mat_spec(rows):
        return pl.BlockSpec((None, 2, None, rows, 2 * S5_SW), lambda g: (layer, 0, g, 0, 0))

    in_specs = [tm_spec, mat_spec(LANES), mat_spec(LANES), mat_spec(S5_NSEQ),
                pl.BlockSpec((None, None, 1, LANES), lambda g: (layer, g, 0, 0))]
    args = [u_tm, bmat, cmat, a, dskip]
    state_spec = pl.BlockSpec((2, None, S5_NSEQ, 2 * S5_SW), lambda g: (0, g, 0, 0))
    state_shape = jax.ShapeDtypeStruct((2, S5_NGB, S5_NSEQ, 2 * S5_SW), F32)
    if not zero_init:
        z, s0 = chain
        in_specs += [state_spec, pl.BlockSpec((2, None, s0.shape[2], 2 * S5_SW), lambda g: (0, g, 0, 0))]
        args += [z, s0]
    out_specs, out_shape = [], []
    if want_y:
        out_specs.append(tm_spec)
        out_shape.append(jax.ShapeDtypeStruct(u_tm.shape, F32))
    out_specs.append(state_spec)
    out_shape.append(state_shape)
    res = pl.pallas_call(
        functools.partial(_s5_kernel, want_y=want_y, zero_init=zero_init),
        grid=(S5_NGB,),
        in_specs=in_specs,
        out_specs=out_specs,
        out_shape=out_shape,
        scratch_shapes=[pltpu.VMEM((2, S5_TC * S5_NSEQ, 2 * S5_SW), F32),
                        pltpu.VMEM((2, S5_TC * S5_NSEQ, 2 * S5_SW), F32),
                        pltpu.VMEM((2, S5_TC * S5_NSEQ, 2 * S5_SW), BF16),
                        pltpu.VMEM((2, S5_TC * S5_NSEQ, 2 * S5_SW), BF16),
                        pltpu.VMEM((2, S5_NSEQ, 2 * S5_SW), F32)],
        compiler_params=pltpu.CompilerParams(dimension_semantics=("parallel",),
                                             vmem_limit_bytes=VMEM_LIMIT),
        name="s5_scan",
    )(*args)
    return (res[0], res[1]) if want_y else (None, res[0])


def _out_kernel(x_ref, ohg_ref, y5_ref, g1_ref, sh2_ref, sc2_ref, g2_ref, nffn_ref, nfin_ref,
                wglu_ref, wout_ref, wg_ref, wu_ref, wd_ref, o_ref, wglu_b, wout_b, *, final_norm):
    @pl.when(_first_step())
    def _():
        _cast_rows(wglu_ref, wglu_b)
        _cast_rows(wout_ref, wout_b)

    y = jnp.concatenate([y5_ref[:, s, :] for s in range(TILE_S)], axis=0)
    y = _gelu_tanh(y)
    y = y * _sigmoid(_dot(y.astype(BF16), wglu_b[...]))
    ohg = ohg_ref[...].reshape(TILE_ROWS, HG_W)
    mix = _dot(ohg.astype(BF16), wout_b[:HG_W, :]) + _dot(y.astype(BF16), wout_b[HG_W:, :])
    x = x_ref[...].reshape(TILE_ROWS, D_MODEL) + g1_ref[...] * mix
    h = _rms(x) * nffn_ref[...]
    h = (h * (1.0 + sc2_ref[...]) + sh2_ref[...]).astype(BF16)
    act = (_silu(_dot(h, wg_ref[...])) * _dot(h, wu_ref[...])).astype(BF16)
    x = x + g2_ref[...] * _dot(act, wd_ref[...])
    if final_norm:
        x = _rms(x) * nfin_ref[...]
    o_ref[...] = x.reshape(o_ref.shape)


def _out_call(x3, ohg3, y5_tm, mod4, nffn, nfin, wglu, wout, wg, wu, wd, layer, cond0, seqs_per_cond, final_norm):
    vec = pl.BlockSpec((1, D_MODEL), lambda sb, tb: (0, 0))
    return pl.pallas_call(
        functools.partial(_out_kernel, final_norm=final_norm),
        grid=(x3.shape[0] // TILE_S, S5_SEQ // TILE_T),
        in_specs=[_tile_spec(D_MODEL), _tile_spec(HG_W), _tm_tile_spec(S5_W),
                  _mod_spec(layer, cond0, seqs_per_cond, 2),
                  _mod_spec(layer, cond0, seqs_per_cond, 3),
                  _mod_spec(layer, cond0, seqs_per_cond, 4),
                  _mod_spec(layer, cond0, seqs_per_cond, 5),
                  vec, vec,
                  _layer_spec((S5_W, S5_W), layer), _layer_spec((D_MODEL, D_MODEL), layer),
                  _layer_spec((D_MODEL, D_FF), layer), _layer_spec((D_MODEL, D_FF), layer),
                  _layer_spec((D_FF, D_MODEL), layer)],
        out_specs=_tile_spec(D_MODEL),
        out_shape=jax.ShapeDtypeStruct(x3.shape, F32),
        scratch_shapes=[pltpu.VMEM((S5_W, S5_W), BF16), pltpu.VMEM((D_MODEL, D_MODEL), BF16)],
        compiler_params=pltpu.CompilerParams(dimension_semantics=("arbitrary", "arbitrary"),
                                             vmem_limit_bytes=VMEM_LIMIT),
        name="out_ffn",
    )(x3, ohg3, y5_tm, mod4, mod4, mod4, mod4, nffn, nfin, wglu, wout, wg, wu, wd)


def _s5_state_to_blocks(s):
    n = s.shape[0]
    s = s.reshape(n, 2, S5_NGB, S5_GB, S5_P, 2)
    return jnp.transpose(s, (1, 2, 0, 5, 3, 4)).reshape(2, S5_NGB, n, 2 * S5_SW)


def _s5_blocks_to_state(h):
    n = h.shape[2]
    h = h.reshape(2, S5_NGB, n, 2, S5_GB, S5_P)
    return jnp.transpose(h, (2, 0, 1, 4, 5, 3)).reshape(n, 2, S5_GROUPS, S5_P, 2)


def kernel(x_prompt, x_sample, state_hgrn, state_s5, c, c_ctx, w_mod, b_mod, norm_mix, norm_ffn, norm_final, w_in, w_out, hg_lb_logits, hg_norm, s5_lam_re, s5_lam_im, s5_log_dt, s5_b_re, s5_b_im, s5_c_re, s5_c_im, s5_d, s5_w_glu, w_gate, w_up, w_down):
    n_ctx, ctx_len, _ = x_prompt.shape
    n_dec, dec_len, _ = x_sample.shape
    assert ctx_len == S5_SEQ and n_ctx == S5_NSEQ and n_dec * dec_len == S5_NSEQ * S5_SEQ

    cond = jnp.concatenate([c_ctx[None, :], c, jnp.zeros((SUBLANES - 1 - n_dec, D_MODEL), F32)], axis=0)
    mod4 = _mod_call(cond, w_mod, b_mod).reshape(DEPTH, SUBLANES, 1, 6 * D_MODEL)

    w_gate_b, w_up_b, w_down_b = w_gate.astype(BF16), w_up.astype(BF16), w_down.astype(BF16)
    s5_a, s5_bmat, s5_cmat = _s5_params(s5_lam_re, s5_lam_im, s5_log_dt, s5_b_re, s5_b_im, s5_c_re, s5_c_im)
    s5_dskip = s5_d.reshape(DEPTH, S5_NGB, 1, LANES)
    nfin = norm_final.reshape(1, D_MODEL)

    def run(x3, add_pos, n_seq, seq_len, cond0, hg_state, s5_state):
        pieces = seq_len // S5_SEQ
        seqs_per_cond = S5_NSEQ if cond0 == 0 else pieces
        rows = S5_NSEQ * S5_SEQ
        hg_fin, s5_finals = None, []
        for l in range(DEPTH):
            proj3, u_tm, x3 = _in_call(x3, add_pos and l == 0, norm_mix[l].reshape(1, D_MODEL), mod4, w_in,
                                       l, cond0, seqs_per_cond)
            ohg, hg_fin = _hgrn_call(proj3.reshape(rows, HG_IN_W), hg_lb_logits, hg_norm[l].reshape(1, HG_D),
                                     hg_state, hg_fin, l, n_seq, seq_len)
            if s5_state is None:
                y5, s5_fin = _s5_call(u_tm, s5_a, s5_bmat, s5_cmat, s5_dskip, None, l, True)
            else:
                _, z = _s5_call(u_tm, s5_a, s5_bmat, s5_cmat, s5_dskip, None, l, False)
                y5, s5_fin = _s5_call(u_tm, s5_a, s5_bmat, s5_cmat, s5_dskip,
                                      (z, _s5_state_to_blocks(s5_state[:, l])), l, True)
            x3 = _out_call(x3, ohg.reshape(S5_NSEQ, S5_SEQ, HG_W), y5, mod4, norm_ffn[l].reshape(1, D_MODEL), nfin,
                           s5_w_glu, w_out, w_gate_b, w_up_b, w_down_b, l, cond0, seqs_per_cond, l == DEPTH - 1)
            s5_finals.append(_s5_blocks_to_state(s5_fin))
        return x3, hg_fin, s5_finals

    tok = (S5_NSEQ, S5_SEQ, D_MODEL)
    y_prompt, new_state_hgrn, s5_finals = run(x_prompt.reshape(tok), False, n_ctx, ctx_len, 0, None, None)
    y_sample, _, _ = run(x_sample.reshape(tok), True, n_dec, dec_len, 1, state_hgrn, state_s5)
    return (y_prompt.reshape(x_prompt.shape), y_sample.reshape(x_sample.shape),
            new_state_hgrn, jnp.stack(s5_finals, axis=1))
```

```python
import functools
import math

import jax
import jax.numpy as jnp
import numpy as np
from jax import lax
from jax.experimental import pallas as pl
from jax.experimental.pallas import tpu as pltpu

F32 = jnp.float32
BF16 = jnp.bfloat16

LANES = 128
SUBLANES = 8

D_MODEL = 1024
DEPTH = 2
GRID_W = 64
HG_W = 512
HG_HEADS = 4
HG_D = HG_W // HG_HEADS
S5_W = 512
S5_CH = 16
S5_GROUPS = S5_W // S5_CH
S5_P = 64
S5_GB = LANES // S5_CH
S5_NGB = S5_GROUPS // S5_GB
S5_SW = S5_GB * S5_P
HG_IN_W = 5 * HG_W
IN_W = HG_IN_W + S5_W
D_FF = 2816
EPS = 1e-6

HG_CHUNK = 128
HG_LEVELS = (64, 32, 16, 8, 4, 2, 1)
S5_SEQ = 256
S5_NSEQ = 16
S5_TC = 32

TILE_S = SUBLANES
TILE_T = 64
TILE_ROWS = TILE_S * TILE_T
CAST_ROWS = 128
MOD_TILE_N = 1536
VMEM_LIMIT = 56 * 1024 * 1024


def _sigmoid(x):
    return 1.0 / (1.0 + jnp.exp(-x))


def _silu(x):
    return x * _sigmoid(x)


def _gelu_tanh(x):
    return 0.5 * x * (1.0 + jnp.tanh(math.sqrt(2.0 / math.pi) * (x + 0.044715 * (x * x * x))))


def _rms(x):
    return x * lax.rsqrt(jnp.mean(x * x, axis=-1, keepdims=True) + EPS)


def _dot(a, b):
    return jnp.dot(a, b, preferred_element_type=F32)


def _dot_nt(a, b):
    return lax.dot_general(a, b, (((1,), (1,)), ((), ())), preferred_element_type=F32)


def _dot_tn(a, b):
    return lax.dot_general(a, b, (((0,), (0,)), ((), ())), preferred_element_type=F32)


def _layer_spec(shape, layer):
    nd = len(shape)
    return pl.BlockSpec((None,) + tuple(shape), lambda *_: (layer,) + (0,) * nd, pipeline_mode=pl.Buffered(1))


def _mod_kernel(cond_ref, w_ref, b_ref, o_ref):
    a = _silu(cond_ref[...]).astype(BF16)
    o_ref[0] = _dot(a, w_ref[0].astype(BF16)) + b_ref[0]


def _mod_call(cond, w_mod, b_mod):
    n_cond = cond.shape[0]
    n_out = w_mod.shape[-1]
    return pl.pallas_call(
        _mod_kernel,
        grid=(DEPTH, n_out // MOD_TILE_N),
        in_specs=[
            pl.BlockSpec((n_cond, D_MODEL), lambda l, j: (0, 0)),
            pl.BlockSpec((1, D_MODEL, MOD_TILE_N), lambda l, j: (l, 0, j)),
            pl.BlockSpec((1, 1, MOD_TILE_N), lambda l, j: (l, 0, j)),
        ],
        out_specs=pl.BlockSpec((1, n_cond, MOD_TILE_N), lambda l, j: (l, 0, j)),
        out_shape=jax.ShapeDtypeStruct((DEPTH, n_cond, n_out), F32),
        compiler_params=pltpu.CompilerParams(dimension_semantics=("parallel", "parallel"),
                                             vmem_limit_bytes=VMEM_LIMIT),
        name="adaln_mod",
    )(cond, w_mod, b_mod.reshape(DEPTH, 1, n_out))


def _first_step():
    return jnp.logical_and(pl.program_id(0) == 0, pl.program_id(1) == 0)


def _cast_rows(src_ref, dst_ref):
    for r in range(0, src_ref.shape[0], CAST_ROWS):
        dst_ref[r:r + CAST_ROWS, :] = src_ref[r:r + CAST_ROWS, :].astype(BF16)


def _grid_pos_tile(omega, tb):
    nf = omega.shape[-1]
    s_idx = lax.broadcasted_iota(jnp.int32, (TILE_S, nf), 0)
    j_idx = lax.broadcasted_iota(jnp.int32, (TILE_T, nf), 0)
    t0 = tb * TILE_T
    row = (s_idx * (S5_SEQ // GRID_W) + t0 // GRID_W).astype(F32) * omega
    col = (j_idx + t0 % GRID_W).astype(F32) * omega
    enc_r = jnp.concatenate([jnp.sin(row), jnp.cos(row)], axis=-1)
    enc_c = jnp.concatenate([jnp.sin(col), jnp.cos(col)], axis=-1)
    shape = (TILE_S, TILE_T, 2 * nf)
    return jnp.concatenate([jnp.broadcast_to(enc_r[:, None, :], shape),
                            jnp.broadcast_to(enc_c[None, :, :], shape)], axis=-1)


def _in_kernel(*refs, add_pos):
    if add_pos:
        x_ref, om_ref, gain_ref, sh_ref, sc_ref, w_ref, proj_ref, u_ref, xs_ref, wb_ref = refs
        x = x_ref[...] + _grid_pos_tile(om_ref[...], pl.program_id(1))
        xs_ref[...] = x
    else:
        x_ref, gain_ref, sh_ref, sc_ref, w_ref, proj_ref, u_ref, wb_ref = refs
        x = x_ref[...]

    @pl.when(_first_step())
    def _():
        _cast_rows(w_ref, wb_ref)

    x = x.reshape(TILE_ROWS, D_MODEL)
    h = _rms(x) * gain_ref[...]
    h = (h * (1.0 + sc_ref[...]) + sh_ref[...]).astype(BF16)
    proj_ref[...] = _dot(h, wb_ref[:, :HG_IN_W]).reshape(proj_ref.shape)
    u = _dot(h, wb_ref[:, HG_IN_W:])
    for s in range(TILE_S):
        u_ref[:, s, :] = u[s * TILE_T:(s + 1) * TILE_T, :]


def _tile_spec(width):
    return pl.BlockSpec((TILE_S, TILE_T, width), lambda sb, tb: (sb, tb, 0))


def _tm_tile_spec(width):
    return pl.BlockSpec((TILE_T, TILE_S, width), lambda sb, tb: (tb, sb, 0))


def _mod_spec(layer, cond0, seqs_per_cond, col):
    return pl.BlockSpec((None, None, 1, D_MODEL),
                        lambda sb, tb: (layer, cond0 + (sb * TILE_S) // seqs_per_cond, 0, col))


def _in_call(x3, add_pos, gain, mod4, w_in_b, layer, cond0, seqs_per_cond):
    n_pseq = x3.shape[0]
    in_specs = [_tile_spec(D_MODEL)]
    args = [x3]
    if add_pos:
        assert seqs_per_cond == TILE_S and GRID_W % TILE_T == 0 and S5_SEQ % GRID_W == 0
        nf = D_MODEL // 4
        omega = 1.0 / (np.float32(10000.0) ** (np.arange(nf, dtype=np.float32) / np.float32(nf)))
        in_specs.append(pl.BlockSpec((1, nf), lambda sb, tb: (0, 0)))
        args.append(jnp.asarray(omega.reshape(1, nf), F32))
    in_specs += [
        pl.BlockSpec((1, D_MODEL), lambda sb, tb: (0, 0)),
        _mod_spec(layer, cond0, seqs_per_cond, 0),
        _mod_spec(layer, cond0, seqs_per_cond, 1),
        _layer_spec((D_MODEL, IN_W), layer),
    ]
    args += [gain, mod4, mod4, w_in_b]
    out_specs = [_tile_spec(HG_IN_W), _tm_tile_spec(S5_W)]
    out_shape = [jax.ShapeDtypeStruct((n_pseq, S5_SEQ, HG_IN_W), F32),
                 jax.ShapeDtypeStruct((S5_SEQ, n_pseq, S5_W), F32)]
    if add_pos:
        out_specs.append(_tile_spec(D_MODEL))
        out_shape.append(jax.ShapeDtypeStruct(x3.shape, F32))
    res = pl.pallas_call(
        functools.partial(_in_kernel, add_pos=add_pos),
        grid=(n_pseq // TILE_S, S5_SEQ // TILE_T),
        in_specs=in_specs,
        out_specs=out_specs,
        out_shape=out_shape,
        scratch_shapes=[pltpu.VMEM((D_MODEL, IN_W), BF16)],
        compiler_params=pltpu.CompilerParams(dimension_semantics=("arbitrary", "arbitrary"),
                                             vmem_limit_bytes=VMEM_LIMIT),
        name="in_proj",
    )(*args)
    return (res[0], res[1], res[2]) if add_pos else (res[0], res[1], x3)


def _pair_boundary(b, m, rev):
    c = b.shape[0]
    span = 2 * m
    at = m if rev else m - 1
    if span >= SUBLANES:
        b3 = b.reshape(c // span, span, LANES)
        return jnp.broadcast_to(b3[:, at:at + 1, :], b3.shape).reshape(c, LANES)
    b3 = b.reshape(c // SUBLANES, SUBLANES, LANES)
    sub = lax.broadcasted_iota(jnp.int32, b3.shape, 1)
    out = None
    for p in range(SUBLANES // span):
        piece = jnp.broadcast_to(b3[:, p * span + at:p * span + at + 1, :], b3.shape)
        out = piece if out is None else jnp.where(sub >= p * span, piece, out)
    return out.reshape(c, LANES)


def _neg_abs(x):
    bits = lax.bitcast_convert_type(x, jnp.uint32) | jnp.uint32(0x80000000)
    return lax.bitcast_convert_type(bits, F32)


def _hg_gates(chains, scale):
    outs = []
    for q, fl, lb, tri in chains:
        sig = _sigmoid(fl)
        forget = lb + (1.0 - lb) * sig
        logf = jnp.log2(forget)
        key = (1.0 - lb) * (1.0 - sig)
        hi = logf.astype(BF16)
        r1 = logf - hi.astype(F32)
        mid = r1.astype(BF16)
        lo = (r1 - mid.astype(F32)).astype(BF16)
        parts = _dot(tri, jnp.concatenate([hi, mid, lo], axis=1))
        b2 = parts[:, :LANES] + parts[:, LANES:2 * LANES] + parts[:, 2 * LANES:]
        outs.append((_silu(q) * scale, key, b2, forget))
    return outs


def _hg_scores(chains, code, eye):
    c = chains[0][0].shape[0]
    o_inter = []
    for qh, key, b2, forget, v, st_ref, rev in chains:
        b_edge = b2[0:1, :] if rev else b2[c - 1:c, :]
        st = st_ref[...]
        o_inter.append(_dot_nt((qh * jnp.exp2(b2)).astype(BF16), st.astype(BF16)))
        k_end = key * jnp.exp2(b_edge - b2)
        st_ref[...] = jnp.exp2(b_edge) * st + _dot_tn(v.astype(BF16), k_end.astype(BF16))

    scores = [jnp.where(eye, jnp.sum(ch[0] * ch[1], axis=-1, keepdims=True), 0.0) for ch in chains]
    qkb = [(ch[0].astype(BF16), ch[1].astype(BF16)) for ch in chains]
    for m in HG_LEVELS:
        k = int(math.log2(m)) + 1
        for i, (qh, key, b2, forget, v, st_ref, rev) in enumerate(chains):
            if m == 1:
                p = _dot_nt((qh * forget).astype(BF16), qkb[i][1])
            else:
                e = jnp.exp2(_neg_abs(b2 - _pair_boundary(b2, m, rev))).astype(BF16)
                p = _dot_nt(qkb[i][0] * e, qkb[i][1] * e)
            scores[i] = jnp.where(code == (-k if rev else k), p, scores[i])
    return [(o, sc.astype(BF16)) for o, sc in zip(o_inter, scores)]


def _hgrn_kernel(*refs, layer, n_chunks, n_heads, zero_init):
    refs = list(refs)
    q_ref, ff_ref, fb_ref, v_ref, g_ref, lbl_ref, gain_ref, code_ref, tri_ref = refs[:9]
    s0_ref = None if zero_init else refs[9]
    o_ref, sfin_ref, st_ref, ob_ref, ab_ref, oi_ref, sc_ref = refs[-7:]
    c = HG_CHUNK
    code = code_ref[...]
    eye = code == 0
    chains = [(hd, d) for hd in range(n_heads) for d in (0, 1)]

    def lanes(hd):
        return slice(hd * HG_D, (hd + 1) * HG_D)

    def lower_bound(hd, d):
        lg = lbl_ref[d, :, lanes(hd)]
        ex = jnp.exp(lg - jnp.max(lg, axis=0, keepdims=True))
        soft = ex / jnp.sum(ex, axis=0, keepdims=True)
        return jnp.sum(soft[:layer + 1], axis=0, keepdims=True) - soft[0:1]

    lb = [lower_bound(hd, d) for hd, d in chains]
    scale = HG_D ** -0.5

    def rows_of(d, i):
        n = (n_chunks - 1 - i) if d else i
        return pl.ds(n * c if isinstance(n, int) else pl.multiple_of(n * c, c), c)

    for ch, (hd, d) in enumerate(chains):
        st_ref[ch] = jnp.zeros((HG_D, HG_D), F32) if zero_init else s0_ref[d, hd].T

    def gates(i):
        return _hg_gates([(q_ref[rows_of(d, i), lanes(hd)], (fb_ref if d else ff_ref)[rows_of(d, i), lanes(hd)],
                           lb[ch], tri_ref[d]) for ch, (hd, d) in enumerate(chains)], scale)

    def scores(i, ab):
        return _hg_scores([ab[ch] + (v_ref[rows_of(d, i), lanes(hd)], st_ref.at[ch], bool(d))
                           for ch, (hd, d) in enumerate(chains)], code, eye)

    def emit(i, oi_sc):
        for ch, (hd, d) in enumerate(chains):
            rows = rows_of(d, i)
            (ob_ref if d else o_ref)[rows, lanes(hd)] = (
                oi_sc[ch][0] + _dot(oi_sc[ch][1], v_ref[rows, lanes(hd)].astype(BF16)))

    if n_chunks <= 2:
        ab = [gates(i) for i in range(n_chunks)]
        for i in range(n_chunks):
            emit(i, scores(i, ab[i]))
    else:
        n_ab = ab_ref.shape[1]

        def put_ab(ab):
            for ch in range(len(chains)):
                for k in range(n_ab):
                    ab_ref[ch, k] = ab[ch][k]

        def get_oi_sc():
            return [(oi_ref[ch], sc_ref[ch]) for ch in range(len(chains))]

        put_ab(gates(0))
        oi_ref[...] = jnp.zeros(oi_ref.shape, F32)
        sc_ref[...] = jnp.zeros(sc_ref.shape, BF16)

        def body(i, carry):
            emit(jnp.maximum(i - 1, 0), get_oi_sc())
            res = scores(i, [tuple(ab_ref[ch, k] for k in range(n_ab)) for ch in range(len(chains))])
            for ch in range(len(chains)):
                oi_ref[ch] = res[ch][0]
                sc_ref[ch] = res[ch][1]
            put_ab(gates(jnp.minimum(i + 1, n_chunks - 1)))
            return carry

        lax.fori_loop(0, n_chunks, body, 0)
        emit(n_chunks - 1, get_oi_sc())
    if len(sfin_ref.shape) == 5:
        for other in range(sfin_ref.shape[0]):
            if other != layer:
                sfin_ref[other] = jnp.zeros(sfin_ref.shape[1:], F32)
        sfin_ref = sfin_ref.at[layer]
    for ch, (hd, d) in enumerate(chains):
        sfin_ref[d, hd] = st_ref[ch].T

    def finish(n, carry):
        rows = pl.ds(pl.multiple_of(n * (2 * c), 2 * c), 2 * c)
        for hd in range(n_heads):
            o_ref[rows, lanes(hd)] = (_rms(o_ref[rows, lanes(hd)] + ob_ref[rows, lanes(hd)]) * gain_ref[...]
                                      * _silu(g_ref[rows, lanes(hd)]))
        return carry

    lax.fori_loop(0, n_chunks // 2, finish, 0)


def _hgrn_call(proj, lb_logits, gain, state, finals, layer, n_seq, seq_len):
    rows = proj.shape[0]
    zero_init = state is None
    n_chunks = seq_len // HG_CHUNK
    assert n_chunks % 2 == 0
    nh = 2
    hw = nh * HG_D
    n_hb = HG_HEADS // nh

    def col_spec(k):
        return pl.BlockSpec((seq_len, hw), lambda b, h: (b, k * n_hb + h))

    t, s = np.meshgrid(np.arange(HG_CHUNK), np.arange(HG_CHUNK), indexing="ij")
    lvl = np.where(t == s, 0, np.floor(np.log2(np.maximum(t ^ s, 1))).astype(np.int32) + 1)
    code = jnp.asarray(np.where(t > s, lvl, -lvl), jnp.int32)
    tri = jnp.asarray(np.stack([s <= t, s >= t]), BF16)

    in_specs = [col_spec(0), col_spec(1), col_spec(2), col_spec(3), col_spec(4),
                pl.BlockSpec((2, DEPTH, hw), lambda b, h: (0, 0, h)),
                pl.BlockSpec((1, HG_D), lambda b, h: (0, 0)),
                pl.BlockSpec((HG_CHUNK, HG_CHUNK), lambda b, h: (0, 0)),
                pl.BlockSpec((2, HG_CHUNK, HG_CHUNK), lambda b, h: (0, 0, 0))]
    args = [proj] * 5 + [lb_logits, gain, code, tri]
    if not zero_init:
        in_specs.append(pl.BlockSpec((None, None, 2, nh, HG_D, HG_D), lambda b, h: (b, layer, 0, h, 0, 0)))
        args.append(state)
    aliases = {}
    if finals is None:
        fin_spec = pl.BlockSpec((None, DEPTH, 2, nh, HG_D, HG_D), lambda b, h: (b, 0, 0, h, 0, 0))
    else:
        fin_spec = pl.BlockSpec((None, None, 2, nh, HG_D, HG_D), lambda b, h: (b, layer, 0, h, 0, 0))
        aliases[len(args)] = 1
        in_specs.append(pl.BlockSpec(memory_space=pl.ANY))
        args.append(finals)
    n_ch = 2 * nh
    return pl.pallas_call(
        functools.partial(_hgrn_kernel, layer=layer, n_chunks=n_chunks, n_heads=nh, zero_init=zero_init),
        grid=(n_seq, n_hb),
        in_specs=in_specs,
        out_specs=[pl.BlockSpec((seq_len, hw), lambda b, h: (b, h)), fin_spec],
        out_shape=[jax.ShapeDtypeStruct((rows, HG_W), F32),
                   jax.ShapeDtypeStruct((n_seq, DEPTH, 2, HG_HEADS, HG_D, HG_D), F32)],
        input_output_aliases=aliases,
        scratch_shapes=[pltpu.VMEM((n_ch, HG_D, HG_D), F32), pltpu.VMEM((seq_len, hw), F32),
                        pltpu.VMEM((n_ch, 4, HG_CHUNK, HG_D), F32), pltpu.VMEM((n_ch, HG_CHUNK, HG_D), F32),
                        pltpu.VMEM((n_ch, HG_CHUNK, HG_CHUNK), BF16)],
        compiler_params=pltpu.CompilerParams(dimension_semantics=("parallel", "parallel"),
                                             vmem_limit_bytes=VMEM_LIMIT),
        name="hgrn2_mixer",
    )(*args)


def _s5_params_kernel(lr_ref, li_ref, ldt_ref, btr_ref, bti_ref, cr_ref, ci_ref, a_ref, bm_ref, cm_ref):
    sw = S5_SW
    lr = jnp.minimum(lr_ref[...], -1e-4)
    li = li_ref[...]
    dt = jnp.exp(ldt_ref[...])
    mag = jnp.exp(lr * dt)
    ab_re = mag * jnp.cos(li * dt)
    ab_im = mag * jnp.sin(li * dt)
    nr = ab_re - 1.0
    den = lr * lr + li * li
    z_re = (nr * lr + ab_im * li) / den
    z_im = (ab_im * lr - nr * li) / den

    p_idx = lax.broadcasted_iota(jnp.int32, (S5_P, sw), 0)
    col = lax.broadcasted_iota(jnp.int32, (S5_P, sw), 1)
    for g in range(S5_GB):
        place = (col == p_idx + g * S5_P).astype(BF16)
        zr, zi = z_re[g:g + 1, :], z_im[g:g + 1, :]
        btr, bti = btr_ref[g], bti_ref[g]
        rows = slice(g * S5_CH, (g + 1) * S5_CH)
        bm_ref[rows, :sw] = _dot((zr * btr - zi * bti).astype(BF16), place).astype(BF16)
        bm_ref[rows, sw:] = _dot((zr * bti + zi * btr).astype(BF16), place).astype(BF16)
        cm_ref[rows, :sw] = _dot(cr_ref[g].astype(BF16), place).astype(BF16)
        cm_ref[rows, sw:] = _dot((-ci_ref[g]).astype(BF16), place).astype(BF16)
        a_ref[:, g * S5_P:(g + 1) * S5_P] = jnp.broadcast_to(ab_re[g:g + 1, :], (S5_NSEQ, S5_P))
        a_ref[:, sw + g * S5_P:sw + (g + 1) * S5_P] = jnp.broadcast_to(ab_im[g:g + 1, :], (S5_NSEQ, S5_P))


def _s5_params(lam_re, lam_im, log_dt, b_re, b_im, c_re, c_im):
    nb = DEPTH * 2 * S5_NGB
    gp = (nb, S5_GB, S5_P)
    gcp = (nb, S5_GB, S5_CH, S5_P)
    bt_re = jnp.swapaxes(b_re, -1, -2).reshape(gcp)
    bt_im = jnp.swapaxes(b_im, -1, -2).reshape(gcp)
    ldt = jnp.broadcast_to(log_dt.reshape(nb, S5_GB, 1), gp)
    gp_spec = pl.BlockSpec((None, S5_GB, S5_P), lambda i: (i, 0, 0))
    gcp_spec = pl.BlockSpec((None, S5_GB, S5_CH, S5_P), lambda i: (i, 0, 0, 0))
    a, bmat, cmat = pl.pallas_call(
        _s5_params_kernel,
        grid=(nb,),
        in_specs=[gp_spec] * 3 + [gcp_spec] * 4,
        out_specs=[pl.BlockSpec((None, S5_NSEQ, 2 * S5_SW), lambda i: (i, 0, 0)),
                   pl.BlockSpec((None, LANES, 2 * S5_SW), lambda i: (i, 0, 0)),
                   pl.BlockSpec((None, LANES, 2 * S5_SW), lambda i: (i, 0, 0))],
        out_shape=[jax.ShapeDtypeStruct((nb, S5_NSEQ, 2 * S5_SW), F32),
                   jax.ShapeDtypeStruct((nb, LANES, 2 * S5_SW), BF16),
                   jax.ShapeDtypeStruct((nb, LANES, 2 * S5_SW), BF16)],
        compiler_params=pltpu.CompilerParams(dimension_semantics=("parallel",)),
        name="s5_params",
    )(lam_re.reshape(gp), lam_im.reshape(gp), ldt, bt_re, bt_im, c_re.reshape(gcp), c_im.reshape(gcp))
    lead = (DEPTH, 2, S5_NGB)
    return (a.reshape(lead + a.shape[1:]), bmat.reshape(lead + bmat.shape[1:]), cmat.reshape(lead + cmat.shape[1:]))


def _s5_kernel(*refs, want_y, zero_init):
    refs = list(refs)
    u_ref, bm_ref, cm_ref, a_ref, d_ref = refs[:5]
    rest = refs[5:]
    z_ref, s0_ref = (None, None) if zero_init else (rest.pop(0), rest.pop(0))
    y_ref = rest.pop(0) if want_y else None
    hfin_ref, hbuf0, hbuf1, hb0, hb1, hst = rest
    hbufs, hb16s = (hbuf0, hbuf1), (hb0, hb1)
    ns, sw = S5_NSEQ, S5_SW
    half = ns // 2
    n_tc = S5_SEQ // S5_TC
    blk = S5_TC * ns
    dirs = (0, 1)

    if zero_init:
        for d in dirs:
            hst[d] = jnp.zeros((ns, 2 * sw), F32)
    else:
        n_long = s0_ref.shape[1]
        pieces = ns // n_long
        for d in dirs:
            pr, pi = a_ref[d, 0:1, :sw], a_ref[d, 0:1, sw:]
            for _ in range(int(math.log2(S5_SEQ))):
                pr, pi = pr * pr - pi * pi, 2.0 * (pr * pi)
            for b in range(n_long):
                hr, hi = s0_ref[d, b:b + 1, :sw], s0_ref[d, b:b + 1, sw:]
                for k in (range(pieces - 1, -1, -1) if d else range(pieces)):
                    r = b * pieces + k
                    hst[d, r:r + 1, :sw] = hr
                    hst[d, r:r + 1, sw:] = hi
                    zr, zi = z_ref[d, r:r + 1, :sw], z_ref[d, r:r + 1, sw:]
                    hr, hi = pr * hr - pi * hi + zr, pr * hi + pi * hr + zi

    def steps_of(d, i):
        return pl.ds(((n_tc - 1 - i) if d else i) * S5_TC, S5_TC)

    def project(i, slot):
        for d in dirs:
            u = u_ref[steps_of(d, i)].reshape(blk, LANES)
            hbufs[slot][d] = _dot(u.astype(BF16), bm_ref[d])

    def scan(slot):
        a = [(a_ref[d, :half, :sw], a_ref[d, :half, sw:]) for d in dirs]
        h = [[(hst[d, k * half:(k + 1) * half, :sw], hst[d, k * half:(k + 1) * half, sw:]) for k in range(2)]
             for d in dirs]
        for jj in range(S5_TC):
            for d in dirs:
                j = S5_TC - 1 - jj if d else jj
                ar, ai = a[d]
                for k in range(2):
                    r = slice(j * ns + k * half, j * ns + (k + 1) * half)
                    hr, hi = h[d][k]
                    h[d][k] = (ar * hr - ai * hi + hbufs[slot][d, r, :sw], ar * hi + ai * hr + hbufs[slot][d, r, sw:])
                if want_y:
                    r = slice(j * ns, (j + 1) * ns)
                    hb16s[slot][d, r, :sw] = jnp.concatenate([h[d][0][0], h[d][1][0]], axis=0).astype(BF16)
                    hb16s[slot][d, r, sw:] = jnp.concatenate([h[d][0][1], h[d][1][1]], axis=0).astype(BF16)
        for d in dirs:
            for k in range(2):
                hst[d, k * half:(k + 1) * half, :sw] = h[d][k][0]
                hst[d, k * half:(k + 1) * half, sw:] = h[d][k][1]

    def readout(i, slot):
        for d in dirs:
            steps = steps_of(d, i)
            y = _dot_nt(hb16s[slot][d], cm_ref[d])
            y_ref[steps] = y_ref[steps] + y.reshape(S5_TC, ns, LANES)

    project(0, 0)
    if want_y:
        hb1[...] = jnp.zeros(hb1.shape, BF16)

        def skip(n, carry):
            steps = pl.ds(n * S5_TC, S5_TC)
            y_ref[steps] = d_ref[...] * u_ref[steps]
            return carry

        lax.fori_loop(0, n_tc, skip, 0)

    def body(k, carry):
        i = 2 * k
        scan(0)
        if want_y:
            readout(jnp.maximum(i - 1, 0), 1)
        project(i + 1, 1)
        scan(1)
        if want_y:
            readout(i, 0)
        project(jnp.minimum(i + 2, n_tc - 1), 0)
        return carry

    lax.fori_loop(0, n_tc // 2, body, 0)
    for d in dirs:
        hfin_ref[d] = hst[d]
    if want_y:
        readout(n_tc - 1, 1)


def _s5_call(u_tm, a, bmat, cmat, dskip, chain, layer, want_y):
    zero_init = chain is None
    tm_spec = pl.BlockSpec((S5_SEQ, S5_NSEQ, LANES), lambda g: (0, 0, g))

    def mat_spec(rows):
        return pl.BlockSpec((None, 2, None, rows, 2 * S5_SW), lambda g: (layer, 0, g, 0, 0))

    in_specs = [tm_spec, mat_spec(LANES), mat_spec(LANES), mat_spec(S5_NSEQ),
                pl.BlockSpec((None, None, 1, LANES), lambda g: (layer, g, 0, 0))]
    args = [u_tm, bmat, cmat, a, dskip]
    state_spec = pl.BlockSpec((2, None, S5_NSEQ, 2 * S5_SW), lambda g: (0, g, 0, 0))
    state_shape = jax.ShapeDtypeStruct((2, S5_NGB, S5_NSEQ, 2 * S5_SW), F32)
    if not zero_init:
        z, s0 = chain
        in_specs += [state_spec, pl.BlockSpec((2, None, s0.shape[2], 2 * S5_SW), lambda g: (0, g, 0, 0))]
        args += [z, s0]
    out_specs, out_shape = [], []
    if want_y:
        out_specs.append(tm_spec)
        out_shape.append(jax.ShapeDtypeStruct(u_tm.shape, F32))
    out_specs.append(state_spec)
    out_shape.append(state_shape)
    res = pl.pallas_call(
        functools.partial(_s5_kernel, want_y=want_y, zero_init=zero_init),
        grid=(S5_NGB,),
        in_specs=in_specs,
        out_specs=out_specs,
        out_shape=out_shape,
        scratch_shapes=[pltpu.VMEM((2, S5_TC * S5_NSEQ, 2 * S5_SW), F32),
                        pltpu.VMEM((2, S5_TC * S5_NSEQ, 2 * S5_SW), F32),
                        pltpu.VMEM((2, S5_TC * S5_NSEQ, 2 * S5_SW), BF16),
                        pltpu.VMEM((2, S5_TC * S5_NSEQ, 2 * S5_SW), BF16),
                        pltpu.VMEM((2, S5_NSEQ, 2 * S5_SW), F32)],
        compiler_params=pltpu.CompilerParams(dimension_semantics=("parallel",),
                                             vmem_limit_bytes=VMEM_LIMIT),
        name="s5_scan",
    )(*args)
    return (res[0], res[1]) if want_y else (None, res[0])


def _out_kernel(x_ref, ohg_ref, y5_ref, g1_ref, sh2_ref, sc2_ref, g2_ref, nffn_ref, nfin_ref,
                wglu_ref, wout_ref, wg_ref, wu_ref, wd_ref, o_ref, wglu_b, wout_b, *, final_norm):
    @pl.when(_first_step())
    def _():
        _cast_rows(wglu_ref, wglu_b)
        _cast_rows(wout_ref, wout_b)

    y = jnp.concatenate([y5_ref[:, s, :] for s in range(TILE_S)], axis=0)
    y = _gelu_tanh(y)
    y = y * _sigmoid(_dot(y.astype(BF16), wglu_b[...]))
    ohg = ohg_ref[...].reshape(TILE_ROWS, HG_W)
    mix = _dot(ohg.astype(BF16), wout_b[:HG_W, :]) + _dot(y.astype(BF16), wout_b[HG_W:, :])
    x = x_ref[...].reshape(TILE_ROWS, D_MODEL) + g1_ref[...] * mix
    h = _rms(x) * nffn_ref[...]
    h = (h * (1.0 + sc2_ref[...]) + sh2_ref[...]).astype(BF16)
    act = (_silu(_dot(h, wg_ref[...])) * _dot(h, wu_ref[...])).astype(BF16)
    x = x + g2_ref[...] * _dot(act, wd_ref[...])
    if final_norm:
        x = _rms(x) * nfin_ref[...]
    o_ref[...] = x.reshape(o_ref.shape)


def _out_call(x3, ohg3, y5_tm, mod4, nffn, nfin, wglu, wout, wg, wu, wd, layer, cond0, seqs_per_cond, final_norm):
    vec = pl.BlockSpec((1, D_MODEL), lambda sb, tb: (0, 0))
    return pl.pallas_call(
        functools.partial(_out_kernel, final_norm=final_norm),
        grid=(x3.shape[0] // TILE_S, S5_SEQ // TILE_T),
        in_specs=[_tile_spec(D_MODEL), _tile_spec(HG_W), _tm_tile_spec(S5_W),
                  _mod_spec(layer, cond0, seqs_per_cond, 2),
                  _mod_spec(layer, cond0, seqs_per_cond, 3),
                  _mod_spec(layer, cond0, seqs_per_cond, 4),
                  _mod_spec(layer, cond0, seqs_per_cond, 5),
                  vec, vec,
                  _layer_spec((S5_W, S5_W), layer), _layer_spec((D_MODEL, D_MODEL), layer),
                  _layer_spec((D_MODEL, D_FF), layer), _layer_spec((D_MODEL, D_FF), layer),
                  _layer_spec((D_FF, D_MODEL), layer)],
        out_specs=_tile_spec(D_MODEL),
        out_shape=jax.ShapeDtypeStruct(x3.shape, F32),
        scratch_shapes=[pltpu.VMEM((S5_W, S5_W), BF16), pltpu.VMEM((D_MODEL, D_MODEL), BF16)],
        compiler_params=pltpu.CompilerParams(dimension_semantics=("arbitrary", "arbitrary"),
                                             vmem_limit_bytes=VMEM_LIMIT),
        name="out_ffn",
    )(x3, ohg3, y5_tm, mod4, mod4, mod4, mod4, nffn, nfin, wglu, wout, wg, wu, wd)


def _s5_state_to_blocks(s):
    n = s.shape[0]
    s = s.reshape(n, 2, S5_NGB, S5_GB, S5_P, 2)
    return jnp.transpose(s, (1, 2, 0, 5, 3, 4)).reshape(2, S5_NGB, n, 2 * S5_SW)


def _s5_blocks_to_state(h):
    n = h.shape[2]
    h = h.reshape(2, S5_NGB, n, 2, S5_GB, S5_P)
    return jnp.transpose(h, (2, 0, 1, 4, 5, 3)).reshape(n, 2, S5_GROUPS, S5_P, 2)


def kernel(x_prompt, x_sample, state_hgrn, state_s5, c, c_ctx, w_mod, b_mod, norm_mix, norm_ffn, norm_final, w_in, w_out, hg_lb_logits, hg_norm, s5_lam_re, s5_lam_im, s5_log_dt, s5_b_re, s5_b_im, s5_c_re, s5_c_im, s5_d, s5_w_glu, w_gate, w_up, w_down):
    n_ctx, ctx_len, _ = x_prompt.shape
    n_dec, dec_len, _ = x_sample.shape
    assert ctx_len == S5_SEQ and n_ctx == S5_NSEQ and n_dec * dec_len == S5_NSEQ * S5_SEQ

    cond = jnp.concatenate([c_ctx[None, :], c, jnp.zeros((SUBLANES - 1 - n_dec, D_MODEL), F32)], axis=0)
    mod4 = _mod_call(cond, w_mod, b_mod).reshape(DEPTH, SUBLANES, 1, 6 * D_MODEL)

    w_gate_b, w_up_b, w_down_b = w_gate.astype(BF16), w_up.astype(BF16), w_down.astype(BF16)
    s5_a, s5_bmat, s5_cmat = _s5_params(s5_lam_re, s5_lam_im, s5_log_dt, s5_b_re, s5_b_im, s5_c_re, s5_c_im)
    s5_dskip = s5_d.reshape(DEPTH, S5_NGB, 1, LANES)
    nfin = norm_final.reshape(1, D_MODEL)

    def run(x3, add_pos, n_seq, seq_len, cond0, hg_state, s5_state):
        pieces = seq_len // S5_SEQ
        seqs_per_cond = S5_NSEQ if cond0 == 0 else pieces
        rows = S5_NSEQ * S5_SEQ
        hg_fin, s5_finals = None, []
        for l in range(DEPTH):
            proj3, u_tm, x3 = _in_call(x3, add_pos and l == 0, norm_mix[l].reshape(1, D_MODEL), mod4, w_in,
                                       l, cond0, seqs_per_cond)
            ohg, hg_fin = _hgrn_call(proj3.reshape(rows, HG_IN_W), hg_lb_logits, hg_norm[l].reshape(1, HG_D),
                                     hg_state, hg_fin, l, n_seq, seq_len)
            if s5_state is None:
                y5, s5_fin = _s5_call(u_tm, s5_a, s5_bmat, s5_cmat, s5_dskip, None, l, True)
            else:
                _, z = _s5_call(u_tm, s5_a, s5_bmat, s5_cmat, s5_dskip, None, l, False)
                y5, s5_fin = _s5_call(u_tm, s5_a, s5_bmat, s5_cmat, s5_dskip,
                                      (z, _s5_state_to_blocks(s5_state[:, l])), l, True)
            x3 = _out_call(x3, ohg.reshape(S5_NSEQ, S5_SEQ, HG_W), y5, mod4, norm_ffn[l].reshape(1, D_MODEL), nfin,
                           s5_w_glu, w_out, w_gate_b, w_up_b, w_down_b, l, cond0, seqs_per_cond, l == DEPTH - 1)
            s5_finals.append(_s5_blocks_to_state(s5_fin))
        return x3, hg_fin, s5_finals

    tok = (S5_NSEQ, S5_SEQ, D_MODEL)
    y_prompt, new_state_hgrn, s5_finals = run(x_prompt.reshape(tok), False, n_ctx, ctx_len, 0, None, None)
    y_sample, _, _ = run(x_sample.reshape(tok), True, n_dec, dec_len, 1, state_hgrn, state_s5)
    return (y_prompt.reshape(x_prompt.shape), y_sample.reshape(x_sample.shape),
            new_state_hgrn, jnp.stack(s5_finals, axis=1))
```

```python
import functools
import math

import jax
import jax.numpy as jnp
import numpy as np
from jax import lax
from jax.experimental import pallas as pl
from jax.experimental.pallas import tpu as pltpu

F32 = jnp.float32
BF16 = jnp.bfloat16

LANES = 128
SUBLANES = 8

D_MODEL = 1024
DEPTH = 2
GRID_W = 64
HG_W = 512
HG_HEADS = 4
HG_D = HG_W // HG_HEADS
S5_W = 512
S5_CH = 16
S5_GROUPS = S5_W // S5_CH
S5_P = 64
S5_GB = LANES // S5_CH
S5_NGB = S5_GROUPS // S5_GB
S5_SW = S5_GB * S5_P
HG_IN_W = 5 * HG_W
IN_W = HG_IN_W + S5_W
D_FF = 2816
EPS = 1e-6

HG_CHUNK = 128
HG_LEVELS = (64, 32, 16, 8, 4, 2, 1)
S5_SEQ = 256
S5_NSEQ = 16
S5_TC = 32

TILE_S = SUBLANES
TILE_T = 64
TILE_ROWS = TILE_S * TILE_T
TIME_TILES = S5_SEQ // TILE_T
ALL_SEQ = 2 * S5_NSEQ
CTX_TILES = S5_NSEQ // TILE_S
CAST_ROWS = 128
MOD_TILE_N = 1536
VMEM_LIMIT = 56 * 1024 * 1024


def _sigmoid(x):
    return 1.0 / (1.0 + jnp.exp(-x))


def _silu(x):
    return x * _sigmoid(x)


def _gelu_tanh(x):
    return 0.5 * x * (1.0 + jnp.tanh(math.sqrt(2.0 / math.pi) * (x + 0.044715 * (x * x * x))))


def _rms(x):
    return x * lax.rsqrt(jnp.mean(x * x, axis=-1, keepdims=True) + EPS)


def _dot(a, b):
    return jnp.dot(a, b, preferred_element_type=F32)


def _dot_nt(a, b):
    return lax.dot_general(a, b, (((1,), (1,)), ((), ())), preferred_element_type=F32)


def _dot_tn(a, b):
    return lax.dot_general(a, b, (((0,), (0,)), ((), ())), preferred_element_type=F32)


def _layer_spec(shape, layer):
    nd = len(shape)
    return pl.BlockSpec((None,) + tuple(shape), lambda *_: (layer,) + (0,) * nd, pipeline_mode=pl.Buffered(1))


def _mod_kernel(cond_ref, w_ref, b_ref, o_ref):
    a = _silu(cond_ref[...]).astype(BF16)
    o_ref[0] = _dot(a, w_ref[0].astype(BF16)) + b_ref[0]


def _mod_call(cond, w_mod, b_mod):
    n_cond = cond.shape[0]
    n_out = w_mod.shape[-1]
    return pl.pallas_call(
        _mod_kernel,
        grid=(DEPTH, n_out // MOD_TILE_N),
        in_specs=[
            pl.BlockSpec((n_cond, D_MODEL), lambda l, j: (0, 0)),
            pl.BlockSpec((1, D_MODEL, MOD_TILE_N), lambda l, j: (l, 0, j)),
            pl.BlockSpec((1, 1, MOD_TILE_N), lambda l, j: (l, 0, j)),
        ],
        out_specs=pl.BlockSpec((1, n_cond, MOD_TILE_N), lambda l, j: (l, 0, j)),
        out_shape=jax.ShapeDtypeStruct((DEPTH, n_cond, n_out), F32),
        compiler_params=pltpu.CompilerParams(dimension_semantics=("parallel", "parallel"),
                                             vmem_limit_bytes=VMEM_LIMIT),
        name="adaln_mod",
    )(cond, w_mod, b_mod.reshape(DEPTH, 1, n_out))


def _first_step():
    return jnp.logical_and(pl.program_id(0) == 0, pl.program_id(1) == 0)


def _cast_rows(src_ref, dst_ref):
    for r in range(0, src_ref.shape[0], CAST_ROWS):
        dst_ref[r:r + CAST_ROWS, :] = src_ref[r:r + CAST_ROWS, :].astype(BF16)


def _grid_pos_tile(omega, tb):
    nf = omega.shape[-1]
    s_idx = lax.broadcasted_iota(jnp.int32, (TILE_S, nf), 0)
    j_idx = lax.broadcasted_iota(jnp.int32, (TILE_T, nf), 0)
    t0 = tb * TILE_T
    row = (s_idx * (S5_SEQ // GRID_W) + t0 // GRID_W).astype(F32) * omega
    col = (j_idx + t0 % GRID_W).astype(F32) * omega
    enc_r = jnp.concatenate([jnp.sin(row), jnp.cos(row)], axis=-1)
    enc_c = jnp.concatenate([jnp.sin(col), jnp.cos(col)], axis=-1)
    shape = (TILE_S, TILE_T, 2 * nf)
    return jnp.concatenate([jnp.broadcast_to(enc_r[:, None, :], shape),
                            jnp.broadcast_to(enc_c[None, :, :], shape)], axis=-1)


def _is_sample_tile():
    return pl.program_id(0) >= CTX_TILES


def _in_kernel(*refs, first_layer):
    if first_layer:
        xc_ref, xs_ref, om_ref, gain_ref, sh_ref, sc_ref, w_ref, proj_ref, u_ref, xo_ref, wb_ref = refs
        x = jnp.where(_is_sample_tile(), xs_ref[...] + _grid_pos_tile(om_ref[...], pl.program_id(1)), xc_ref[...])
        xo_ref[...] = x
    else:
        x_ref, gain_ref, sh_ref, sc_ref, w_ref, proj_ref, u_ref, wb_ref = refs
        x = x_ref[...]

    @pl.when(_first_step())
    def _():
        _cast_rows(w_ref, wb_ref)

    x = x.reshape(TILE_ROWS, D_MODEL)
    h = _rms(x) * gain_ref[...]
    h = (h * (1.0 + sc_ref[...]) + sh_ref[...]).astype(BF16)
    proj_ref[...] = _dot(h, wb_ref[:, :HG_IN_W]).reshape(proj_ref.shape)
    u = _dot(h, wb_ref[:, HG_IN_W:])
    for s in range(TILE_S):
        u_ref[:, s, :] = u[s * TILE_T:(s + 1) * TILE_T, :]


def _tile_spec(width):
    return pl.BlockSpec((TILE_S, TILE_T, width), lambda sb, tb: (sb, tb, 0))


def _tm_tile_spec(width):
    return pl.BlockSpec((TILE_T, TILE_S, width), lambda sb, tb: (tb, sb, 0))


def _ctx_index(sb, tb):
    on = sb < CTX_TILES
    return jnp.where(on, sb, CTX_TILES - 1), jnp.where(on, tb, TIME_TILES - 1)


def _smp_index(sb, tb):
    on = sb >= CTX_TILES
    return jnp.where(on, sb - CTX_TILES, 0), jnp.where(on, tb, 0)


def _part_tile_spec(width, index):
    return pl.BlockSpec((TILE_S, TILE_T, width), lambda sb, tb: index(sb, tb) + (0,))


def _part_tm_tile_spec(width, index):
    return pl.BlockSpec((TILE_T, TILE_S, width), lambda sb, tb: index(sb, tb)[::-1] + (0,))


def _mod_spec(layer, col):
    return pl.BlockSpec((None, None, 1, D_MODEL),
                        lambda sb, tb: (layer, jnp.maximum(sb - (CTX_TILES - 1), 0), 0, col))


def _in_call(xs, gain, mod4, w_in, layer):
    first_layer = len(xs) == 2
    if first_layer:
        assert GRID_W % TILE_T == 0 and S5_SEQ % GRID_W == 0
        nf = D_MODEL // 4
        omega = 1.0 / (np.float32(10000.0) ** (np.arange(nf, dtype=np.float32) / np.float32(nf)))
        in_specs = [_part_tile_spec(D_MODEL, _ctx_index), _part_tile_spec(D_MODEL, _smp_index),
                    pl.BlockSpec((1, nf), lambda sb, tb: (0, 0))]
        args = list(xs) + [jnp.asarray(omega.reshape(1, nf), F32)]
    else:
        in_specs = [_tile_spec(D_MODEL)]
        args = list(xs)
    in_specs += [
        pl.BlockSpec((1, D_MODEL), lambda sb, tb: (0, 0)),
        _mod_spec(layer, 0),
        _mod_spec(layer, 1),
        _layer_spec((D_MODEL, IN_W), layer),
    ]
    args += [gain, mod4, mod4, w_in]
    out_specs = [_tile_spec(HG_IN_W), _tm_tile_spec(S5_W)]
    out_shape = [jax.ShapeDtypeStruct((ALL_SEQ, S5_SEQ, HG_IN_W), F32),
                 jax.ShapeDtypeStruct((S5_SEQ, ALL_SEQ, S5_W), F32)]
    if first_layer:
        out_specs.append(_tile_spec(D_MODEL))
        out_shape.append(jax.ShapeDtypeStruct((ALL_SEQ, S5_SEQ, D_MODEL), F32))
    res = pl.pallas_call(
        functools.partial(_in_kernel, first_layer=first_layer),
        grid=(ALL_SEQ // TILE_S, TIME_TILES),
        in_specs=in_specs,
        out_specs=out_specs,
        out_shape=out_shape,
        scratch_shapes=[pltpu.VMEM((D_MODEL, IN_W), BF16)],
        compiler_params=pltpu.CompilerParams(dimension_semantics=("arbitrary", "arbitrary"),
                                             vmem_limit_bytes=VMEM_LIMIT),
        name="in_proj",
    )(*args)
    return (res[0], res[1], res[2]) if first_layer else (res[0], res[1], xs[0])


def _pair_boundary(b, m, rev):
    c = b.shape[0]
    span = 2 * m
    at = m if rev else m - 1
    if span >= SUBLANES:
        b3 = b.reshape(c // span, span, LANES)
        return jnp.broadcast_to(b3[:, at:at + 1, :], b3.shape).reshape(c, LANES)
    b3 = b.reshape(c // SUBLANES, SUBLANES, LANES)
    sub = lax.broadcasted_iota(jnp.int32, b3.shape, 1)
    out = None
    for p in range(SUBLANES // span):
        piece = jnp.broadcast_to(b3[:, p * span + at:p * span + at + 1, :], b3.shape)
        out = piece if out is None else jnp.where(sub >= p * span, piece, out)
    return out.reshape(c, LANES)


def _neg_abs(x):
    bits = lax.bitcast_convert_type(x, jnp.uint32) | jnp.uint32(0x80000000)
    return lax.bitcast_convert_type(bits, F32)


def _hg_gates(chains, scale):
    outs = []
    for q, fl, lb, tri in chains:
        sig = _sigmoid(fl)
        forget = lb + (1.0 - lb) * sig
        logf = jnp.log2(forget)
        key = (1.0 - lb) * (1.0 - sig)
        hi = logf.astype(BF16)
        r1 = logf - hi.astype(F32)
        mid = r1.astype(BF16)
        lo = (r1 - mid.astype(F32)).astype(BF16)
        parts = _dot(tri, jnp.concatenate([hi, mid, lo], axis=1))
        b2 = parts[:, :LANES] + parts[:, LANES:2 * LANES] + parts[:, 2 * LANES:]
        outs.append((_silu(q) * scale, key, b2, forget))
    return outs


def _hg_scores(chains, code, eye):
    c = chains[0][0].shape[0]
    o_inter = []
    for qh, key, b2, forget, v, st_ref, rev in chains:
        b_edge = b2[0:1, :] if rev else b2[c - 1:c, :]
        st = st_ref[...]
        o_inter.append(_dot_nt((qh * jnp.exp2(b2)).astype(BF16), st.astype(BF16)))
        k_end = key * jnp.exp2(b_edge - b2)
        st_ref[...] = jnp.exp2(b_edge) * st + _dot_tn(v.astype(BF16), k_end.astype(BF16))

    out = []
    for o, (qh, key, b2, forget, v, st_ref, rev) in zip(o_inter, chains):
        scores = jnp.where(eye, jnp.sum(qh * key, axis=-1, keepdims=True), 0.0)
        qb, kb = qh.astype(BF16), key.astype(BF16)
        for m in HG_LEVELS:
            k = int(math.log2(m)) + 1
            if m == 1:
                p = _dot_nt((qh * forget).astype(BF16), kb)
            else:
                e = jnp.exp2(_neg_abs(b2 - _pair_boundary(b2, m, rev))).astype(BF16)
                p = _dot_nt(qb * e, kb * e)
            scores = jnp.where(code == (-k if rev else k), p, scores)
        out.append((o, scores.astype(BF16)))
    return out


def _hgrn_kernel(*refs, layer, n_chunks, n_heads, zero_init):
    refs = list(refs)
    q_ref, ff_ref, fb_ref, v_ref, g_ref, lbl_ref, gain_ref, code_ref, tri_ref = refs[:9]
    s0_ref = None if zero_init else refs[9]
    o_ref, sfin_ref, st_ref, ob_ref, ab_ref, oi_ref, sc_ref = refs[-7:]
    c = HG_CHUNK
    code = code_ref[...]
    eye = code == 0
    chains = [(hd, d) for hd in range(n_heads) for d in (0, 1)]

    def lanes(hd):
        return slice(hd * HG_D, (hd + 1) * HG_D)

    def lower_bound(hd, d):
        lg = lbl_ref[d, :, lanes(hd)]
        ex = jnp.exp(lg - jnp.max(lg, axis=0, keepdims=True))
        soft = ex / jnp.sum(ex, axis=0, keepdims=True)
        return jnp.sum(soft[:layer + 1], axis=0, keepdims=True) - soft[0:1]

    lb = [lower_bound(hd, d) for hd, d in chains]
    scale = HG_D ** -0.5

    def rows_of(d, i):
        n = (n_chunks - 1 - i) if d else i
        return pl.ds(n * c if isinstance(n, int) else pl.multiple_of(n * c, c), c)

    for ch, (hd, d) in enumerate(chains):
        st_ref[ch] = jnp.zeros((HG_D, HG_D), F32) if zero_init else s0_ref[d, hd].T

    def gates(i):
        return _hg_gates([(q_ref[rows_of(d, i), lanes(hd)], (fb_ref if d else ff_ref)[rows_of(d, i), lanes(hd)],
                           lb[ch], tri_ref[d]) for ch, (hd, d) in enumerate(chains)], scale)

    def scores(i, ab):
        return _hg_scores([ab[ch] + (v_ref[rows_of(d, i), lanes(hd)], st_ref.at[ch], bool(d))
                           for ch, (hd, d) in enumerate(chains)], code, eye)

    def emit(i, oi_sc):
        for ch, (hd, d) in enumerate(chains):
            rows = rows_of(d, i)
            (ob_ref if d else o_ref)[rows, lanes(hd)] = (
                oi_sc[ch][0] + _dot(oi_sc[ch][1], v_ref[rows, lanes(hd)].astype(BF16)))

    if n_chunks <= 2:
        ab = [gates(i) for i in range(n_chunks)]
        for i in range(n_chunks):
            emit(i, scores(i, ab[i]))
    else:
        n_ab = ab_ref.shape[1]

        def put_ab(ab):
            for ch in range(len(chains)):
                for k in range(n_ab):
                    ab_ref[ch, k] = ab[ch][k]

        def get_oi_sc():
            return [(oi_ref[ch], sc_ref[ch]) for ch in range(len(chains))]

        put_ab(gates(0))
        oi_ref[...] = jnp.zeros(oi_ref.shape, F32)
        sc_ref[...] = jnp.zeros(sc_ref.shape, BF16)

        def body(i, carry):
            emit(jnp.maximum(i - 1, 0), get_oi_sc())
            res = scores(i, [tuple(ab_ref[ch, k] for k in range(n_ab)) for ch in range(len(chains))])
            for ch in range(len(chains)):
                oi_ref[ch] = res[ch][0]
                sc_ref[ch] = res[ch][1]
            put_ab(gates(jnp.minimum(i + 1, n_chunks - 1)))
            return carry

        lax.fori_loop(0, n_chunks, body, 0)
        emit(n_chunks - 1, get_oi_sc())
    if len(sfin_ref.shape) == 5:
        for other in range(sfin_ref.shape[0]):
            if other != layer:
                sfin_ref[other] = jnp.zeros(sfin_ref.shape[1:], F32)
        sfin_ref = sfin_ref.at[layer]
    for ch, (hd, d) in enumerate(chains):
        sfin_ref[d, hd] = st_ref[ch].T

    def finish(n, carry):
        rows = pl.ds(pl.multiple_of(n * (2 * c), 2 * c), 2 * c)
        for hd in range(n_heads):
            o_ref[rows, lanes(hd)] = (_rms(o_ref[rows, lanes(hd)] + ob_ref[rows, lanes(hd)]) * gain_ref[...]
                                      * _silu(g_ref[rows, lanes(hd)]))
        return carry

    lax.fori_loop(0, n_chunks // 2, finish, 0)


def _hgrn_call(proj, row0, lb_logits, gain, state, finals, layer, n_seq, seq_len):
    rows = n_seq * seq_len
    seq0 = row0 // seq_len
    zero_init = state is None
    n_chunks = seq_len // HG_CHUNK
    assert n_chunks % 2 == 0
    nh = 2
    hw = nh * HG_D
    n_hb = HG_HEADS // nh

    def col_spec(k):
        return pl.BlockSpec((seq_len, hw), lambda b, h: (seq0 + b, k * n_hb + h))

    t, s = np.meshgrid(np.arange(HG_CHUNK), np.arange(HG_CHUNK), indexing="ij")
    lvl = np.where(t == s, 0, np.floor(np.log2(np.maximum(t ^ s, 1))).astype(np.int32) + 1)
    code = jnp.asarray(np.where(t > s, lvl, -lvl), jnp.int32)
    tri = jnp.asarray(np.stack([s <= t, s >= t]), BF16)

    in_specs = [col_spec(0), col_spec(1), col_spec(2), col_spec(3), col_spec(4),
                pl.BlockSpec((2, DEPTH, hw), lambda b, h: (0, 0, h)),
                pl.BlockSpec((1, HG_D), lambda b, h: (0, 0)),
                pl.BlockSpec((HG_CHUNK, HG_CHUNK), lambda b, h: (0, 0)),
                pl.BlockSpec((2, HG_CHUNK, HG_CHUNK), lambda b, h: (0, 0, 0))]
    args = [proj] * 5 + [lb_logits, gain, code, tri]
    if not zero_init:
        in_specs.append(pl.BlockSpec((None, None, 2, nh, HG_D, HG_D), lambda b, h: (b, layer, 0, h, 0, 0)))
        args.append(state)
    aliases = {}
    if finals is None:
        fin_spec = pl.BlockSpec((None, DEPTH, 2, nh, HG_D, HG_D), lambda b, h: (b, 0, 0, h, 0, 0))
    else:
        fin_spec = pl.BlockSpec((None, None, 2, nh, HG_D, HG_D), lambda b, h: (b, layer, 0, h, 0, 0))
        aliases[len(args)] = 1
        in_specs.append(pl.BlockSpec(memory_space=pl.ANY))
        args.append(finals)
    n_ch = 2 * nh
    return pl.pallas_call(
        functools.partial(_hgrn_kernel, layer=layer, n_chunks=n_chunks, n_heads=nh, zero_init=zero_init),
        grid=(n_seq, n_hb),
        in_specs=in_specs,
        out_specs=[pl.BlockSpec((seq_len, hw), lambda b, h: (b, h)), fin_spec],
        out_shape=[jax.ShapeDtypeStruct((rows, HG_W), F32),
                   jax.ShapeDtypeStruct((n_seq, DEPTH, 2, HG_HEADS, HG_D, HG_D), F32)],
        input_output_aliases=aliases,
        scratch_shapes=[pltpu.VMEM((n_ch, HG_D, HG_D), F32), pltpu.VMEM((seq_len, hw), F32),
                        pltpu.VMEM((n_ch, 4, HG_CHUNK, HG_D), F32), pltpu.VMEM((n_ch, HG_CHUNK, HG_D), F32),
                        pltpu.VMEM((n_ch, HG_CHUNK, HG_CHUNK), BF16)],
        compiler_params=pltpu.CompilerParams(dimension_semantics=("parallel", "parallel"),
                                             vmem_limit_bytes=VMEM_LIMIT),
        name="hgrn2_mixer",
    )(*args)


def _s5_params_kernel(lr_ref, li_ref, ldt_ref, btr_ref, bti_ref, cr_ref, ci_ref, a_ref, bm_ref, cm_ref):
    sw = S5_SW
    lr = jnp.minimum(lr_ref[...], -1e-4)
    li = li_ref[...]
    dt = jnp.exp(ldt_ref[...])
    mag = jnp.exp(lr * dt)
    ab_re = mag * jnp.cos(li * dt)
    ab_im = mag * jnp.sin(li * dt)
    nr = ab_re - 1.0
    den = lr * lr + li * li
    z_re = (nr * lr + ab_im * li) / den
    z_im = (ab_im * lr - nr * li) / den

    p_idx = lax.broadcasted_iota(jnp.int32, (S5_P, sw), 0)
    col = lax.broadcasted_iota(jnp.int32, (S5_P, sw), 1)
    for g in range(S5_GB):
        place = (col == p_idx + g * S5_P).astype(BF16)
        zr, zi = z_re[g:g + 1, :], z_im[g:g + 1, :]
        btr, bti = btr_ref[g], bti_ref[g]
        rows = slice(g * S5_CH, (g + 1) * S5_CH)
        bm_ref[rows, :sw] = _dot((zr * btr - zi * bti).astype(BF16), place).astype(BF16)
        bm_ref[rows, sw:] = _dot((zr * bti + zi * btr).astype(BF16), place).astype(BF16)
        cm_ref[rows, :sw] = _dot(cr_ref[g].astype(BF16), place).astype(BF16)
        cm_ref[rows, sw:] = _dot((-ci_ref[g]).astype(BF16), place).astype(BF16)
        a_ref[:, g * S5_P:(g + 1) * S5_P] = jnp.broadcast_to(ab_re[g:g + 1, :], (S5_NSEQ, S5_P))
        a_ref[:, sw + g * S5_P:sw + (g + 1) * S5_P] = jnp.broadcast_to(ab_im[g:g + 1, :], (S5_NSEQ, S5_P))


def _s5_params(lam_re, lam_im, log_dt, b_re, b_im, c_re, c_im):
    nb = DEPTH * 2 * S5_NGB
    gp = (nb, S5_GB, S5_P)
    gcp = (nb, S5_GB, S5_CH, S5_P)
    bt_re = jnp.swapaxes(b_re, -1, -2).reshape(gcp)
    bt_im = jnp.swapaxes(b_im, -1, -2).reshape(gcp)
    ldt = jnp.broadcast_to(log_dt.reshape(nb, S5_GB, 1), gp)
    gp_spec = pl.BlockSpec((None, S5_GB, S5_P), lambda i: (i, 0, 0))
    gcp_spec = pl.BlockSpec((None, S5_GB, S5_CH, S5_P), lambda i: (i, 0, 0, 0))
    a, bmat, cmat = pl.pallas_call(
        _s5_params_kernel,
        grid=(nb,),
        in_specs=[gp_spec] * 3 + [gcp_spec] * 4,
        out_specs=[pl.BlockSpec((None, S5_NSEQ, 2 * S5_SW), lambda i: (i, 0, 0)),
                   pl.BlockSpec((None, LANES, 2 * S5_SW), lambda i: (i, 0, 0)),
                   pl.BlockSpec((None, LANES, 2 * S5_SW), lambda i: (i, 0, 0))],
        out_shape=[jax.ShapeDtypeStruct((nb, S5_NSEQ, 2 * S5_SW), F32),
                   jax.ShapeDtypeStruct((nb, LANES, 2 * S5_SW), BF16),
                   jax.ShapeDtypeStruct((nb, LANES, 2 * S5_SW), BF16)],
        compiler_params=pltpu.CompilerParams(dimension_semantics=("parallel",)),
        name="s5_params",
    )(lam_re.reshape(gp), lam_im.reshape(gp), ldt, bt_re, bt_im, c_re.reshape(gcp), c_im.reshape(gcp))
    lead = (DEPTH, 2, S5_NGB)
    return (a.reshape(lead + a.shape[1:]), bmat.reshape(lead + bmat.shape[1:]), cmat.reshape(lead + cmat.shape[1:]))


def _s5_kernel(*refs, want_y, zero_init):
    refs = list(refs)
    u_ref, bm_ref, cm_ref, a_ref, d_ref = refs[:5]
    rest = refs[5:]
    z_ref, s0_ref = (None, None) if zero_init else (rest.pop(0), rest.pop(0))
    y_ref = rest.pop(0) if want_y else None
    hfin_ref, hbuf0, hbuf1, hb0, hb1, hst = rest
    hbufs, hb16s = (hbuf0, hbuf1), (hb0, hb1)
    ns, sw = S5_NSEQ, S5_SW
    half = ns // 2
    n_tc = S5_SEQ // S5_TC
    blk = S5_TC * ns
    dirs = (0, 1)

    if zero_init:
        for d in dirs:
            hst[d] = jnp.zeros((ns, 2 * sw), F32)
    else:
        n_long = s0_ref.shape[1]
        pieces = ns // n_long
        for d in dirs:
            pr, pi = a_ref[d, 0:1, :sw], a_ref[d, 0:1, sw:]
            for _ in range(int(math.log2(S5_SEQ))):
                pr, pi = pr * pr - pi * pi, 2.0 * (pr * pi)
            for b in range(n_long):
                hr, hi = s0_ref[d, b:b + 1, :sw], s0_ref[d, b:b + 1, sw:]
                for k in (range(pieces - 1, -1, -1) if d else range(pieces)):
                    r = b * pieces + k
                    hst[d, r:r + 1, :sw] = hr
                    hst[d, r:r + 1, sw:] = hi
                    zr, zi = z_ref[d, r:r + 1, :sw], z_ref[d, r:r + 1, sw:]
                    hr, hi = pr * hr - pi * hi + zr, pr * hi + pi * hr + zi

    def steps_of(d, i):
        return pl.ds(((n_tc - 1 - i) if d else i) * S5_TC, S5_TC)

    def project(i, slot):
        for d in dirs:
            u = u_ref[steps_of(d, i)].reshape(blk, LANES)
            hbufs[slot][d] = _dot(u.astype(BF16), bm_ref[d])

    def scan(slot):
        a = [(a_ref[d, :half, :sw], a_ref[d, :half, sw:]) for d in dirs]
        h = [[(hst[d, k * half:(k + 1) * half, :sw], hst[d, k * half:(k + 1) * half, sw:]) for k in range(2)]
             for d in dirs]
        for jj in range(S5_TC):
            for d in dirs:
                j = S5_TC - 1 - jj if d else jj
                ar, ai = a[d]
                for k in range(2):
                    r = slice(j * ns + k * half, j * ns + (k + 1) * half)
                    hr, hi = h[d][k]
                    h[d][k] = (ar * hr - ai * hi + hbufs[slot][d, r, :sw], ar * hi + ai * hr + hbufs[slot][d, r, sw:])
                if want_y:
                    r = slice(j * ns, (j + 1) * ns)
                    hb16s[slot][d, r, :sw] = jnp.concatenate([h[d][0][0], h[d][1][0]], axis=0).astype(BF16)
                    hb16s[slot][d, r, sw:] = jnp.concatenate([h[d][0][1], h[d][1][1]], axis=0).astype(BF16)
        for d in dirs:
            for k in range(2):
                hst[d, k * half:(k + 1) * half, :sw] = h[d][k][0]
                hst[d, k * half:(k + 1) * half, sw:] = h[d][k][1]

    def readout(i, slot):
        for d in dirs:
            steps = steps_of(d, i)
            y = _dot_nt(hb16s[slot][d], cm_ref[d])
            y_ref[steps] = y_ref[steps] + y.reshape(S5_TC, ns, LANES)

    project(0, 0)
    if want_y:
        hb1[...] = jnp.zeros(hb1.shape, BF16)

        def skip(n, carry):
            steps = pl.ds(n * S5_TC, S5_TC)
            y_ref[steps] = d_ref[...] * u_ref[steps]
            return carry

        lax.fori_loop(0, n_tc, skip, 0)

    def body(k, carry):
        i = 2 * k
        scan(0)
        if want_y:
            readout(jnp.maximum(i - 1, 0), 1)
        project(i + 1, 1)
        scan(1)
        if want_y:
            readout(i, 0)
        project(jnp.minimum(i + 2, n_tc - 1), 0)
        return carry

    lax.fori_loop(0, n_tc // 2, body, 0)
    for d in dirs:
        hfin_ref[d] = hst[d]
    if want_y:
        readout(n_tc - 1, 1)


def _s5_call(u_tm, part, a, bmat, cmat, dskip, chain, layer, want_y):
    zero_init = chain is None
    tm_spec = pl.BlockSpec((S5_SEQ, S5_NSEQ, LANES), lambda g: (0, 0, g))

    def mat_spec(rows):
        return pl.BlockSpec((None, 2, None, rows, 2 * S5_SW), lambda g: (layer, 0, g, 0, 0))

    in_specs = [pl.BlockSpec((S5_SEQ, S5_NSEQ, LANES), lambda g: (0, part, g)),
                mat_spec(LANES), mat_spec(LANES), mat_spec(S5_NSEQ),
                pl.BlockSpec((None, None, 1, LANES), lambda g: (layer, g, 0, 0))]
    args = [u_tm, bmat, cmat, a, dskip]
    state_spec = pl.BlockSpec((2, None, S5_NSEQ, 2 * S5_SW), lambda g: (0, g, 0, 0))
    state_shape = jax.ShapeDtypeStruct((2, S5_NGB, S5_NSEQ, 2 * S5_SW), F32)
    if not zero_init:
        z, s0 = chain
        in_specs += [state_spec, pl.BlockSpec((2, None, s0.shape[2], 2 * S5_SW), lambda g: (0, g, 0, 0))]
        args += [z, s0]
    out_specs, out_shape = [], []
    if want_y:
        out_specs.append(tm_spec)
        out_shape.append(jax.ShapeDtypeStruct((S5_SEQ, S5_NSEQ, S5_W), F32))
    out_specs.append(state_spec)
    out_shape.append(state_shape)
    res = pl.pallas_call(
        functools.partial(_s5_kernel, want_y=want_y, zero_init=zero_init),
        grid=(S5_NGB,),
        in_specs=in_specs,
        out_specs=out_specs,
        out_shape=out_shape,
        scratch_shapes=[pltpu.VMEM((2, S5_TC * S5_NSEQ, 2 * S5_SW), F32),
                        pltpu.VMEM((2, S5_TC * S5_NSEQ, 2 * S5_SW), F32),
                        pltpu.VMEM((2, S5_TC * S5_NSEQ, 2 * S5_SW), BF16),
                        pltpu.VMEM((2, S5_TC * S5_NSEQ, 2 * S5_SW), BF16),
                        pltpu.VMEM((2, S5_NSEQ, 2 * S5_SW), F32)],
        compiler_params=pltpu.CompilerParams(dimension_semantics=("parallel",),
                                             vmem_limit_bytes=VMEM_LIMIT),
        name="s5_scan",
    )(*args)
    return (res[0], res[1]) if want_y else (None, res[0])


def _out_kernel(x_ref, ohc_ref, ohs_ref, y5c_ref, y5s_ref, g1_ref, sh2_ref, sc2_ref, g2_ref, nffn_ref, nfin_ref,
                wglu_ref, wout_ref, wg_ref, wu_ref, wd_ref, *rest, last_layer):
    if last_layer:
        oc_ref, os_ref, wglu_b, wout_b = rest
    else:
        o_ref, wglu_b, wout_b = rest

    @pl.when(_first_step())
    def _():
        _cast_rows(wglu_ref, wglu_b)
        _cast_rows(wout_ref, wout_b)

    smp = _is_sample_tile()
    y = jnp.concatenate([jnp.where(smp, y5s_ref[:, s, :], y5c_ref[:, s, :]) for s in range(TILE_S)],
                        axis=0)
    y = _gelu_tanh(y)
    y = y * _sigmoid(_dot(y.astype(BF16), wglu_b[...]))
    ohg = jnp.where(smp, ohs_ref[...], ohc_ref[...]).reshape(TILE_ROWS, HG_W)
    mix = _dot(ohg.astype(BF16), wout_b[:HG_W, :]) + _dot(y.astype(BF16), wout_b[HG_W:, :])
    x = x_ref[...].reshape(TILE_ROWS, D_MODEL) + g1_ref[...] * mix
    h = _rms(x) * nffn_ref[...]
    h = (h * (1.0 + sc2_ref[...]) + sh2_ref[...]).astype(BF16)
    act = (_silu(_dot(h, wg_ref[...])) * _dot(h, wu_ref[...])).astype(BF16)
    x = x + g2_ref[...] * _dot(act, wd_ref[...])
    if not last_layer:
        o_ref[...] = x.reshape(o_ref.shape)
    else:
        x = (_rms(x) * nfin_ref[...]).reshape(oc_ref.shape)

        @pl.when(smp)
        def _():
            os_ref[...] = x

        @pl.when(jnp.logical_not(smp))
        def _():
            oc_ref[...] = x


def _out_call(x3, ohg_c, ohg_s, y5_c, y5_s, mod4, nffn, nfin, wglu, wout, wg, wu, wd, layer, last_layer):
    vec = pl.BlockSpec((1, D_MODEL), lambda sb, tb: (0, 0))
    part_shape = jax.ShapeDtypeStruct((S5_NSEQ, S5_SEQ, D_MODEL), F32)
    if last_layer:
        out_specs = [_part_tile_spec(D_MODEL, _ctx_index), _part_tile_spec(D_MODEL, _smp_index)]
        out_shape = [part_shape, part_shape]
    else:
        out_specs = _tile_spec(D_MODEL)
        out_shape = jax.ShapeDtypeStruct(x3.shape, F32)
    return pl.pallas_call(
        functools.partial(_out_kernel, last_layer=last_layer),
        grid=(ALL_SEQ // TILE_S, TIME_TILES),
        in_specs=[_tile_spec(D_MODEL),
                  _part_tile_spec(HG_W, _ctx_index), _part_tile_spec(HG_W, _smp_index),
                  _part_tm_tile_spec(S5_W, _ctx_index), _part_tm_tile_spec(S5_W, _smp_index),
                  _mod_spec(layer, 2), _mod_spec(layer, 3), _mod_spec(layer, 4), _mod_spec(layer, 5),
                  vec, vec,
                  _layer_spec((S5_W, S5_W), layer), _layer_spec((D_MODEL, D_MODEL), layer),
                  _layer_spec((D_MODEL, D_FF), layer), _layer_spec((D_MODEL, D_FF), layer),
                  _layer_spec((D_FF, D_MODEL), layer)],
        out_specs=out_specs,
        out_shape=out_shape,
        scratch_shapes=[pltpu.VMEM((S5_W, S5_W), BF16), pltpu.VMEM((D_MODEL, D_MODEL), BF16)],
        compiler_params=pltpu.CompilerParams(dimension_semantics=("arbitrary", "arbitrary"),
                                             vmem_limit_bytes=VMEM_LIMIT),
        name="out_ffn",
    )(x3, ohg_c, ohg_s, y5_c, y5_s, mod4, mod4, mod4, mod4, nffn, nfin, wglu, wout, wg, wu, wd)


def _s5_state_to_blocks(s):
    n = s.shape[0]
    s = s.reshape(n, 2, S5_NGB, S5_GB, S5_P, 2)
    return jnp.transpose(s, (1, 2, 0, 5, 3, 4)).reshape(2, S5_NGB, n, 2 * S5_SW)


def _s5_blocks_to_state(h):
    n = h.shape[2]
    h = h.reshape(2, S5_NGB, n, 2, S5_GB, S5_P)
    return jnp.transpose(h, (2, 0, 1, 4, 5, 3)).reshape(n, 2, S5_GROUPS, S5_P, 2)


def kernel(x_prompt, x_sample, state_hgrn, state_s5, c, c_ctx, w_mod, b_mod, norm_mix, norm_ffn, norm_final, w_in, w_out, hg_lb_logits, hg_norm, s5_lam_re, s5_lam_im, s5_log_dt, s5_b_re, s5_b_im, s5_c_re, s5_c_im, s5_d, s5_w_glu, w_gate, w_up, w_down):
    n_ctx, ctx_len, _ = x_prompt.shape
    n_dec, dec_len, _ = x_sample.shape
    assert ctx_len == S5_SEQ and n_ctx == S5_NSEQ and n_dec * dec_len == S5_NSEQ * S5_SEQ

    cond = jnp.concatenate([c_ctx[None, :], c, jnp.zeros((SUBLANES - 1 - n_dec, D_MODEL), F32)], axis=0)
    mod4 = _mod_call(cond, w_mod, b_mod).reshape(DEPTH, SUBLANES, 1, 6 * D_MODEL)

    w_gate_b, w_up_b, w_down_b = w_gate.astype(BF16), w_up.astype(BF16), w_down.astype(BF16)
    s5_a, s5_bmat, s5_cmat = _s5_params(s5_lam_re, s5_lam_im, s5_log_dt, s5_b_re, s5_b_im, s5_c_re, s5_c_im)
    s5_dskip = s5_d.reshape(DEPTH, S5_NGB, 1, LANES)
    nfin = norm_final.reshape(1, D_MODEL)

    assert dec_len // S5_SEQ == TILE_S and n_dec + 1 <= SUBLANES
    tok = (S5_NSEQ, S5_SEQ, D_MODEL)
    ctx_rows = S5_NSEQ * S5_SEQ
    xs = (x_prompt.reshape(tok), x_sample.reshape(tok))
    ctx_fin, smp_fin, s5_finals = None, None, []
    for l in range(DEPTH):
        proj3, u_tm, x_all = _in_call(xs, norm_mix[l].reshape(1, D_MODEL), mod4, w_in, l)
        proj = proj3.reshape(ALL_SEQ * S5_SEQ, HG_IN_W)
        gain = hg_norm[l].reshape(1, HG_D)
        ohg_c, ctx_fin = _hgrn_call(proj, 0, hg_lb_logits, gain, None, ctx_fin, l, n_ctx, ctx_len)
        ohg_s, smp_fin = _hgrn_call(proj, ctx_rows, hg_lb_logits, gain, state_hgrn, smp_fin, l, n_dec, dec_len)
        y5_c, s5_fin = _s5_call(u_tm, 0, s5_a, s5_bmat, s5_cmat, s5_dskip, None, l, True)
        _, z = _s5_call(u_tm, 1, s5_a, s5_bmat, s5_cmat, s5_dskip, None, l, False)
        y5_s, _ = _s5_call(u_tm, 1, s5_a, s5_bmat, s5_cmat, s5_dskip,
                           (z, _s5_state_to_blocks(state_s5[:, l])), l, True)
        last = l == DEPTH - 1
        res = _out_call(x_all, ohg_c.reshape(S5_NSEQ, S5_SEQ, HG_W), ohg_s.reshape(S5_NSEQ, S5_SEQ, HG_W),
                        y5_c, y5_s, mod4, norm_ffn[l].reshape(1, D_MODEL), nfin,
                        s5_w_glu, w_out, w_gate_b, w_up_b, w_down_b, l, last)
        xs = res if last else (res,)
        s5_finals.append(_s5_blocks_to_state(s5_fin))
    y_prompt, y_sample = xs
    return (y_prompt.reshape(x_prompt.shape), y_sample.reshape(x_sample.shape),
            ctx_fin, jnp.stack(s5_finals, axis=1))
```

```python
import functools
import math

import jax
import jax.numpy as jnp
import numpy as np
from jax import lax
from jax.experimental import pallas as pl
from jax.experimental.pallas import tpu as pltpu

F32 = jnp.float32
BF16 = jnp.bfloat16

LANES = 128
SUBLANES = 8

D_MODEL = 1024
DEPTH = 2
GRID_W = 64
HG_W = 512
HG_HEADS = 4
HG_D = HG_W // HG_HEADS
S5_W = 512
S5_CH = 16
S5_GROUPS = S5_W // S5_CH
S5_P = 64
S5_GB = LANES // S5_CH
S5_NGB = S5_GROUPS // S5_GB
S5_SW = S5_GB * S5_P
HG_IN_W = 5 * HG_W
IN_W = HG_IN_W + S5_W
D_FF = 2816
EPS = 1e-6

HG_CHUNK = 128
HG_LEVELS = (64, 32, 16, 8, 4, 2, 1)
S5_SEQ = 256
S5_NSEQ = 16
S5_TC = 32

TILE_S = SUBLANES
TILE_T = 64
TILE_ROWS = TILE_S * TILE_T
TIME_TILES = S5_SEQ // TILE_T
ALL_SEQ = 2 * S5_NSEQ
CTX_TILES = S5_NSEQ // TILE_S
CAST_ROWS = 128
FETCH_ROWS = 256
MOD_TILE_N = 1536
VMEM_LIMIT = 56 * 1024 * 1024


def _sigmoid(x):
    return 1.0 / (1.0 + jnp.exp(-x))


def _silu(x):
    return x * _sigmoid(x)


def _gelu_tanh(x):
    return 0.5 * x * (1.0 + jnp.tanh(math.sqrt(2.0 / math.pi) * (x + 0.044715 * (x * x * x))))


def _rms(x):
    return x * lax.rsqrt(jnp.mean(x * x, axis=-1, keepdims=True) + EPS)


def _dot(a, b):
    return jnp.dot(a, b, preferred_element_type=F32)


def _dot_nt(a, b):
    return lax.dot_general(a, b, (((1,), (1,)), ((), ())), preferred_element_type=F32)


def _dot_tn(a, b):
    return lax.dot_general(a, b, (((0,), (0,)), ((), ())), preferred_element_type=F32)


def _layer_spec(shape, layer):
    nd = len(shape)
    return pl.BlockSpec((None,) + tuple(shape), lambda *_: (layer,) + (0,) * nd, pipeline_mode=pl.Buffered(1))


def _mod_kernel(cond_ref, w_ref, b_ref, o_ref):
    a = _silu(cond_ref[...]).astype(BF16)
    o_ref[0] = _dot(a, w_ref[0].astype(BF16)) + b_ref[0]


def _mod_call(cond, w_mod, b_mod):
    n_cond = cond.shape[0]
    n_out = w_mod.shape[-1]
    return pl.pallas_call(
        _mod_kernel,
        grid=(DEPTH, n_out // MOD_TILE_N),
        in_specs=[
            pl.BlockSpec((n_cond, D_MODEL), lambda l, j: (0, 0)),
            pl.BlockSpec((1, D_MODEL, MOD_TILE_N), lambda l, j: (l, 0, j)),
            pl.BlockSpec((1, 1, MOD_TILE_N), lambda l, j: (l, 0, j)),
        ],
        out_specs=pl.BlockSpec((1, n_cond, MOD_TILE_N), lambda l, j: (l, 0, j)),
        out_shape=jax.ShapeDtypeStruct((DEPTH, n_cond, n_out), F32),
        compiler_params=pltpu.CompilerParams(dimension_semantics=("parallel", "parallel"),
                                             vmem_limit_bytes=VMEM_LIMIT),
        name="adaln_mod",
    )(cond, w_mod, b_mod.reshape(DEPTH, 1, n_out))


def _first_step():
    return jnp.logical_and(pl.program_id(0) == 0, pl.program_id(1) == 0)


def _cast_rows(src_ref, dst_ref):
    for r in range(0, src_ref.shape[0], CAST_ROWS):
        dst_ref[r:r + CAST_ROWS, :] = src_ref[r:r + CAST_ROWS, :].astype(BF16)


def _fetch_cast(w_hbm, layer, dst_ref, stage, sem):
    n_rows, n_cols = dst_ref.shape
    rows = stage.shape[1]
    assert n_rows % rows == 0 and stage.shape[2] == n_cols

    def copy(i):
        return pltpu.make_async_copy(w_hbm.at[layer, pl.ds(i * rows, rows), :], stage.at[i % 2], sem.at[i % 2])

    n = n_rows // rows
    copy(0).start()
    for i in range(n):
        if i + 1 < n:
            copy(i + 1).start()
        copy(i).wait()
        dst_ref[i * rows:(i + 1) * rows, :] = stage[i % 2].astype(BF16)


def _grid_pos_tile(omega, tb):
    nf = omega.shape[-1]
    s_idx = lax.broadcasted_iota(jnp.int32, (TILE_S, nf), 0)
    j_idx = lax.broadcasted_iota(jnp.int32, (TILE_T, nf), 0)
    t0 = tb * TILE_T
    row = (s_idx * (S5_SEQ // GRID_W) + t0 // GRID_W).astype(F32) * omega
    col = (j_idx + t0 % GRID_W).astype(F32) * omega
    enc_r = jnp.concatenate([jnp.sin(row), jnp.cos(row)], axis=-1)
    enc_c = jnp.concatenate([jnp.sin(col), jnp.cos(col)], axis=-1)
    shape = (TILE_S, TILE_T, 2 * nf)
    return jnp.concatenate([jnp.broadcast_to(enc_r[:, None, :], shape),
                            jnp.broadcast_to(enc_c[None, :, :], shape)], axis=-1)


def _is_sample_tile():
    return pl.program_id(0) >= CTX_TILES


def _in_kernel(*refs, first_layer):
    if first_layer:
        xc_ref, xs_ref, om_ref, gain_ref, sh_ref, sc_ref, w_ref, proj_ref, u_ref, xo_ref, wb_ref = refs
        x = jnp.where(_is_sample_tile(), xs_ref[...] + _grid_pos_tile(om_ref[...], pl.program_id(1)), xc_ref[...])
        xo_ref[...] = x
    else:
        x_ref, gain_ref, sh_ref, sc_ref, w_ref, proj_ref, u_ref, wb_ref = refs
        x = x_ref[...]

    @pl.when(_first_step())
    def _():
        _cast_rows(w_ref, wb_ref)

    x = x.reshape(TILE_ROWS, D_MODEL)
    h = _rms(x) * gain_ref[...]
    h = (h * (1.0 + sc_ref[...]) + sh_ref[...]).astype(BF16)
    proj_ref[...] = _dot(h, wb_ref[:, :HG_IN_W]).reshape(proj_ref.shape)
    u = _dot(h, wb_ref[:, HG_IN_W:])
    for s in range(TILE_S):
        u_ref[:, s, :] = u[s * TILE_T:(s + 1) * TILE_T, :]


def _tile_spec(width):
    return pl.BlockSpec((TILE_S, TILE_T, width), lambda sb, tb: (sb, tb, 0))


def _tm_tile_spec(width):
    return pl.BlockSpec((TILE_T, TILE_S, width), lambda sb, tb: (tb, sb, 0))


def _ctx_index(sb, tb):
    on = sb < CTX_TILES
    return jnp.where(on, sb, CTX_TILES - 1), jnp.where(on, tb, TIME_TILES - 1)


def _smp_index(sb, tb):
    on = sb >= CTX_TILES
    return jnp.where(on, sb - CTX_TILES, 0), jnp.where(on, tb, 0)


def _part_tile_spec(width, index):
    return pl.BlockSpec((TILE_S, TILE_T, width), lambda sb, tb: index(sb, tb) + (0,))


def _part_tm_tile_spec(width, index):
    return pl.BlockSpec((TILE_T, TILE_S, width), lambda sb, tb: index(sb, tb)[::-1] + (0,))


def _mod_spec(layer, col):
    return pl.BlockSpec((None, None, 1, D_MODEL),
                        lambda sb, tb: (layer, jnp.maximum(sb - (CTX_TILES - 1), 0), 0, col))


def _in_call(xs, gain, mod4, w_in, layer):
    first_layer = len(xs) == 2
    if first_layer:
        assert GRID_W % TILE_T == 0 and S5_SEQ % GRID_W == 0
        nf = D_MODEL // 4
        omega = 1.0 / (np.float32(10000.0) ** (np.arange(nf, dtype=np.float32) / np.float32(nf)))
        in_specs = [_part_tile_spec(D_MODEL, _ctx_index), _part_tile_spec(D_MODEL, _smp_index),
                    pl.BlockSpec((1, nf), lambda sb, tb: (0, 0))]
        args = list(xs) + [jnp.asarray(omega.reshape(1, nf), F32)]
    else:
        in_specs = [_tile_spec(D_MODEL)]
        args = list(xs)
    in_specs += [
        pl.BlockSpec((1, D_MODEL), lambda sb, tb: (0, 0)),
        _mod_spec(layer, 0),
        _mod_spec(layer, 1),
        _layer_spec((D_MODEL, IN_W), layer),
    ]
    args += [gain, mod4, mod4, w_in]
    out_specs = [_tile_spec(HG_IN_W), _tm_tile_spec(S5_W)]
    out_shape = [jax.ShapeDtypeStruct((ALL_SEQ, S5_SEQ, HG_IN_W), F32),
                 jax.ShapeDtypeStruct((S5_SEQ, ALL_SEQ, S5_W), F32)]
    if first_layer:
        out_specs.append(_tile_spec(D_MODEL))
        out_shape.append(jax.ShapeDtypeStruct((ALL_SEQ, S5_SEQ, D_MODEL), F32))
    res = pl.pallas_call(
        functools.partial(_in_kernel, first_layer=first_layer),
        grid=(ALL_SEQ // TILE_S, TIME_TILES),
        in_specs=in_specs,
        out_specs=out_specs,
        out_shape=out_shape,
        scratch_shapes=[pltpu.VMEM((D_MODEL, IN_W), BF16)],
        compiler_params=pltpu.CompilerParams(dimension_semantics=("arbitrary", "arbitrary"),
                                             vmem_limit_bytes=VMEM_LIMIT),
        name="in_proj",
    )(*args)
    return (res[0], res[1], res[2]) if first_layer else (res[0], res[1], xs[0])


def _pair_boundary(b, m, rev):
    c = b.shape[0]
    span = 2 * m
    at = m if rev else m - 1
    if span >= SUBLANES:
        b3 = b.reshape(c // span, span, LANES)
        return jnp.broadcast_to(b3[:, at:at + 1, :], b3.shape).reshape(c, LANES)
    b3 = b.reshape(c // SUBLANES, SUBLANES, LANES)
    sub = lax.broadcasted_iota(jnp.int32, b3.shape, 1)
    out = None
    for p in range(SUBLANES // span):
        piece = jnp.broadcast_to(b3[:, p * span + at:p * span + at + 1, :], b3.shape)
        out = piece if out is None else jnp.where(sub >= p * span, piece, out)
    return out.reshape(c, LANES)


def _neg_abs(x):
    bits = lax.bitcast_convert_type(x, jnp.uint32) | jnp.uint32(0x80000000)
    return lax.bitcast_convert_type(bits, F32)


def _hg_gates(chains, scale):
    outs = []
    for q, fl, lb, tri in chains:
        sig = _sigmoid(fl)
        forget = lb + (1.0 - lb) * sig
        logf = jnp.log2(forget)
        key = (1.0 - lb) * (1.0 - sig)
        hi = logf.astype(BF16)
        r1 = logf - hi.astype(F32)
        mid = r1.astype(BF16)
        lo = (r1 - mid.astype(F32)).astype(BF16)
        parts = _dot(tri, jnp.concatenate([hi, mid, lo], axis=1))
        b2 = parts[:, :LANES] + parts[:, LANES:2 * LANES] + parts[:, 2 * LANES:]
        outs.append((_silu(q) * scale, key, b2, forget))
    return outs


def _hg_scores(chains, code, eye):
    c = chains[0][0].shape[0]
    o_inter = []
    for qh, key, b2, forget, v, st_ref, rev in chains:
        b_edge = b2[0:1, :] if rev else b2[c - 1:c, :]
        st = st_ref[...]
        o_inter.append(_dot_nt((qh * jnp.exp2(b2)).astype(BF16), st.astype(BF16)))
        k_end = key * jnp.exp2(b_edge - b2)
        st_ref[...] = jnp.exp2(b_edge) * st + _dot_tn(v.astype(BF16), k_end.astype(BF16))

    out = []
    for o, (qh, key, b2, forget, v, st_ref, rev) in zip(o_inter, chains):
        scores = jnp.where(eye, jnp.sum(qh * key, axis=-1, keepdims=True), 0.0)
        qb, kb = qh.astype(BF16), key.astype(BF16)
        for m in HG_LEVELS:
            k = int(math.log2(m)) + 1
            if m == 1:
                p = _dot_nt((qh * forget).astype(BF16), kb)
            else:
                e = jnp.exp2(_neg_abs(b2 - _pair_boundary(b2, m, rev))).astype(BF16)
                p = _dot_nt(qb * e, kb * e)
            scores = jnp.where(code == (-k if rev else k), p, scores)
        out.append((o, scores.astype(BF16)))
    return out


def _hgrn_kernel(*refs, layer, n_chunks, n_heads, zero_init):
    refs = list(refs)
    q_ref, ff_ref, fb_ref, v_ref, g_ref, lbl_ref, gain_ref, code_ref, tri_ref = refs[:9]
    s0_ref = None if zero_init else refs[9]
    o_ref, sfin_ref, st_ref, ob_ref, ab_ref, oi_ref, sc_ref = refs[-7:]
    c = HG_CHUNK
    code = code_ref[...]
    eye = code == 0
    chains = [(hd, d) for hd in range(n_heads) for d in (0, 1)]

    def lanes(hd):
        return slice(hd * HG_D, (hd + 1) * HG_D)

    def lower_bound(hd, d):
        lg = lbl_ref[d, :, lanes(hd)]
        ex = jnp.exp(lg - jnp.max(lg, axis=0, keepdims=True))
        soft = ex / jnp.sum(ex, axis=0, keepdims=True)
        return jnp.sum(soft[:layer + 1], axis=0, keepdims=True) - soft[0:1]

    lb = [lower_bound(hd, d) for hd, d in chains]
    scale = HG_D ** -0.5

    def rows_of(d, i):
        n = (n_chunks - 1 - i) if d else i
        return pl.ds(n * c if isinstance(n, int) else pl.multiple_of(n * c, c), c)

    for ch, (hd, d) in enumerate(chains):
        st_ref[ch] = jnp.zeros((HG_D, HG_D), F32) if zero_init else s0_ref[d, hd].T

    def gates(i):
        return _hg_gates([(q_ref[rows_of(d, i), lanes(hd)], (fb_ref if d else ff_ref)[rows_of(d, i), lanes(hd)],
                           lb[ch], tri_ref[d]) for ch, (hd, d) in enumerate(chains)], scale)

    def scores(i, ab):
        return _hg_scores([ab[ch] + (v_ref[rows_of(d, i), lanes(hd)], st_ref.at[ch], bool(d))
                           for ch, (hd, d) in enumerate(chains)], code, eye)

    def emit(i, oi_sc):
        for ch, (hd, d) in enumerate(chains):
            rows = rows_of(d, i)
            (ob_ref if d else o_ref)[rows, lanes(hd)] = (
                oi_sc[ch][0] + _dot(oi_sc[ch][1], v_ref[rows, lanes(hd)].astype(BF16)))

    if n_chunks <= 2:
        ab = [gates(i) for i in range(n_chunks)]
        for i in range(n_chunks):
            emit(i, scores(i, ab[i]))
    else:
        n_ab = ab_ref.shape[1]

        def put_ab(ab):
            for ch in range(len(chains)):
                for k in range(n_ab):
                    ab_ref[ch, k] = ab[ch][k]

        def get_oi_sc():
            return [(oi_ref[ch], sc_ref[ch]) for ch in range(len(chains))]

        put_ab(gates(0))
        oi_ref[...] = jnp.zeros(oi_ref.shape, F32)
        sc_ref[...] = jnp.zeros(sc_ref.shape, BF16)

        def body(i, carry):
            emit(jnp.maximum(i - 1, 0), get_oi_sc())
            res = scores(i, [tuple(ab_ref[ch, k] for k in range(n_ab)) for ch in range(len(chains))])
            for ch in range(len(chains)):
                oi_ref[ch] = res[ch][0]
                sc_ref[ch] = res[ch][1]
            put_ab(gates(jnp.minimum(i + 1, n_chunks - 1)))
            return carry

        lax.fori_loop(0, n_chunks, body, 0)
        emit(n_chunks - 1, get_oi_sc())
    if len(sfin_ref.shape) == 5:
        for other in range(sfin_ref.shape[0]):
            if other != layer:
                sfin_ref[other] = jnp.zeros(sfin_ref.shape[1:], F32)
        sfin_ref = sfin_ref.at[layer]
    for ch, (hd, d) in enumerate(chains):
        sfin_ref[d, hd] = st_ref[ch].T

    def finish(n, carry):
        rows = pl.ds(pl.multiple_of(n * (2 * c), 2 * c), 2 * c)
        for hd in range(n_heads):
            o_ref[rows, lanes(hd)] = (_rms(o_ref[rows, lanes(hd)] + ob_ref[rows, lanes(hd)]) * gain_ref[...]
                                      * _silu(g_ref[rows, lanes(hd)]))
        return carry

    lax.fori_loop(0, n_chunks // 2, finish, 0)


def _hgrn_call(proj, row0, lb_logits, gain, state, finals, layer, n_seq, seq_len):
    rows = n_seq * seq_len
    seq0 = row0 // seq_len
    zero_init = state is None
    n_chunks = seq_len // HG_CHUNK
    assert n_chunks % 2 == 0
    nh = 2
    hw = nh * HG_D
    n_hb = HG_HEADS // nh

    def col_spec(k):
        return pl.BlockSpec((seq_len, hw), lambda b, h: (seq0 + b, k * n_hb + h))

    t, s = np.meshgrid(np.arange(HG_CHUNK), np.arange(HG_CHUNK), indexing="ij")
    lvl = np.where(t == s, 0, np.floor(np.log2(np.maximum(t ^ s, 1))).astype(np.int32) + 1)
    code = jnp.asarray(np.where(t > s, lvl, -lvl), jnp.int32)
    tri = jnp.asarray(np.stack([s <= t, s >= t]), BF16)

    in_specs = [col_spec(0), col_spec(1), col_spec(2), col_spec(3), col_spec(4),
                pl.BlockSpec((2, DEPTH, hw), lambda b, h: (0, 0, h)),
                pl.BlockSpec((1, HG_D), lambda b, h: (0, 0)),
                pl.BlockSpec((HG_CHUNK, HG_CHUNK), lambda b, h: (0, 0)),
                pl.BlockSpec((2, HG_CHUNK, HG_CHUNK), lambda b, h: (0, 0, 0))]
    args = [proj] * 5 + [lb_logits, gain, code, tri]
    if not zero_init:
        in_specs.append(pl.BlockSpec((None, None, 2, nh, HG_D, HG_D), lambda b, h: (b, layer, 0, h, 0, 0)))
        args.append(state)
    aliases = {}
    if finals is None:
        fin_spec = pl.BlockSpec((None, DEPTH, 2, nh, HG_D, HG_D), lambda b, h: (b, 0, 0, h, 0, 0))
    else:
        fin_spec = pl.BlockSpec((None, None, 2, nh, HG_D, HG_D), lambda b, h: (b, layer, 0, h, 0, 0))
        aliases[len(args)] = 1
        in_specs.append(pl.BlockSpec(memory_space=pl.ANY))
        args.append(finals)
    n_ch = 2 * nh
    return pl.pallas_call(
        functools.partial(_hgrn_kernel, layer=layer, n_chunks=n_chunks, n_heads=nh, zero_init=zero_init),
        grid=(n_seq, n_hb),
        in_specs=in_specs,
        out_specs=[pl.BlockSpec((seq_len, hw), lambda b, h: (b, h)), fin_spec],
        out_shape=[jax.ShapeDtypeStruct((rows, HG_W), F32),
                   jax.ShapeDtypeStruct((n_seq, DEPTH, 2, HG_HEADS, HG_D, HG_D), F32)],
        input_output_aliases=aliases,
        scratch_shapes=[pltpu.VMEM((n_ch, HG_D, HG_D), F32), pltpu.VMEM((seq_len, hw), F32),
                        pltpu.VMEM((n_ch, 4, HG_CHUNK, HG_D), F32), pltpu.VMEM((n_ch, HG_CHUNK, HG_D), F32),
                        pltpu.VMEM((n_ch, HG_CHUNK, HG_CHUNK), BF16)],
        compiler_params=pltpu.CompilerParams(dimension_semantics=("parallel", "parallel"),
                                             vmem_limit_bytes=VMEM_LIMIT),
        name="hgrn2_mixer",
    )(*args)


def _s5_params_kernel(lr_ref, li_ref, ldt_ref, btr_ref, bti_ref, cr_ref, ci_ref, a_ref, bm_ref, cm_ref):
    sw = S5_SW
    lr = jnp.minimum(lr_ref[...], -1e-4)
    li = li_ref[...]
    dt = jnp.exp(ldt_ref[...])
    mag = jnp.exp(lr * dt)
    ab_re = mag * jnp.cos(li * dt)
    ab_im = mag * jnp.sin(li * dt)
    nr = ab_re - 1.0
    den = lr * lr + li * li
    z_re = (nr * lr + ab_im * li) / den
    z_im = (ab_im * lr - nr * li) / den

    p_idx = lax.broadcasted_iota(jnp.int32, (S5_P, sw), 0)
    col = lax.broadcasted_iota(jnp.int32, (S5_P, sw), 1)
    for g in range(S5_GB):
        place = (col == p_idx + g * S5_P).astype(BF16)
        zr, zi = z_re[g:g + 1, :], z_im[g:g + 1, :]
        btr, bti = btr_ref[g], bti_ref[g]
        rows = slice(g * S5_CH, (g + 1) * S5_CH)
        bm_ref[rows, :sw] = _dot((zr * btr - zi * bti).astype(BF16), place).astype(BF16)
        bm_ref[rows, sw:] = _dot((zr * bti + zi * btr).astype(BF16), place).astype(BF16)
        cm_ref[rows, :sw] = _dot(cr_ref[g].astype(BF16), place).astype(BF16)
        cm_ref[rows, sw:] = _dot((-ci_ref[g]).astype(BF16), place).astype(BF16)
        a_ref[:, g * S5_P:(g + 1) * S5_P] = jnp.broadcast_to(ab_re[g:g + 1, :], (S5_NSEQ, S5_P))
        a_ref[:, sw + g * S5_P:sw + (g + 1) * S5_P] = jnp.broadcast_to(ab_im[g:g + 1, :], (S5_NSEQ, S5_P))


def _s5_params(lam_re, lam_im, log_dt, b_re, b_im, c_re, c_im):
    nb = DEPTH * 2 * S5_NGB
    gp = (nb, S5_GB, S5_P)
    gcp = (nb, S5_GB, S5_CH, S5_P)
    bt_re = jnp.swapaxes(b_re, -1, -2).reshape(gcp)
    bt_im = jnp.swapaxes(b_im, -1, -2).reshape(gcp)
    ldt = jnp.broadcast_to(log_dt.reshape(nb, S5_GB, 1), gp)
    gp_spec = pl.BlockSpec((None, S5_GB, S5_P), lambda i: (i, 0, 0))
    gcp_spec = pl.BlockSpec((None, S5_GB, S5_CH, S5_P), lambda i: (i, 0, 0, 0))
    a, bmat, cmat = pl.pallas_call(
        _s5_params_kernel,
        grid=(nb,),
        in_specs=[gp_spec] * 3 + [gcp_spec] * 4,
        out_specs=[pl.BlockSpec((None, S5_NSEQ, 2 * S5_SW), lambda i: (i, 0, 0)),
                   pl.BlockSpec((None, LANES, 2 * S5_SW), lambda i: (i, 0, 0)),
                   pl.BlockSpec((None, LANES, 2 * S5_SW), lambda i: (i, 0, 0))],
        out_shape=[jax.ShapeDtypeStruct((nb, S5_NSEQ, 2 * S5_SW), F32),
                   jax.ShapeDtypeStruct((nb, LANES, 2 * S5_SW), BF16),
                   jax.ShapeDtypeStruct((nb, LANES, 2 * S5_SW), BF16)],
        compiler_params=pltpu.CompilerParams(dimension_semantics=("parallel",)),
        name="s5_params",
    )(lam_re.reshape(gp), lam_im.reshape(gp), ldt, bt_re, bt_im, c_re.reshape(gcp), c_im.reshape(gcp))
    lead = (DEPTH, 2, S5_NGB)
    return (a.reshape(lead + a.shape[1:]), bmat.reshape(lead + bmat.shape[1:]), cmat.reshape(lead + cmat.shape[1:]))


def _s5_kernel(*refs, want_y, zero_init):
    refs = list(refs)
    u_ref, bm_ref, cm_ref, a_ref, d_ref = refs[:5]
    rest = refs[5:]
    z_ref, s0_ref = (None, None) if zero_init else (rest.pop(0), rest.pop(0))
    y_ref = rest.pop(0) if want_y else None
    hfin_ref, hbuf0, hbuf1, hb0, hb1, hst = rest
    hbufs, hb16s = (hbuf0, hbuf1), (hb0, hb1)
    ns, sw = S5_NSEQ, S5_SW
    half = ns // 2
    n_tc = S5_SEQ // S5_TC
    blk = S5_TC * ns
    dirs = (0, 1)

    if zero_init:
        for d in dirs:
            hst[d] = jnp.zeros((ns, 2 * sw), F32)
    else:
        n_long = s0_ref.shape[1]
        pieces = ns // n_long
        for d in dirs:
            pr, pi = a_ref[d, 0:1, :sw], a_ref[d, 0:1, sw:]
            for _ in range(int(math.log2(S5_SEQ))):
                pr, pi = pr * pr - pi * pi, 2.0 * (pr * pi)
            for b in range(n_long):
                hr, hi = s0_ref[d, b:b + 1, :sw], s0_ref[d, b:b + 1, sw:]
                for k in (range(pieces - 1, -1, -1) if d else range(pieces)):
                    r = b * pieces + k
                    hst[d, r:r + 1, :sw] = hr
                    hst[d, r:r + 1, sw:] = hi
                    zr, zi = z_ref[d, r:r + 1, :sw], z_ref[d, r:r + 1, sw:]
                    hr, hi = pr * hr - pi * hi + zr, pr * hi + pi * hr + zi

    def steps_of(d, i):
        return pl.ds(((n_tc - 1 - i) if d else i) * S5_TC, S5_TC)

    def project(i, slot):
        for d in dirs:
            u = u_ref[steps_of(d, i)].reshape(blk, LANES)
            hbufs[slot][d] = _dot(u.astype(BF16), bm_ref[d])

    def scan(slot):
        a = [(a_ref[d, :half, :sw], a_ref[d, :half, sw:]) for d in dirs]
        h = [[(hst[d, k * half:(k + 1) * half, :sw], hst[d, k * half:(k + 1) * half, sw:]) for k in range(2)]
             for d in dirs]
        for jj in range(S5_TC):
            for d in dirs:
                j = S5_TC - 1 - jj if d else jj
                ar, ai = a[d]
                for k in range(2):
                    r = slice(j * ns + k * half, j * ns + (k + 1) * half)
                    hr, hi = h[d][k]
                    h[d][k] = (ar * hr - ai * hi + hbufs[slot][d, r, :sw], ar * hi + ai * hr + hbufs[slot][d, r, sw:])
                if want_y:
                    r = slice(j * ns, (j + 1) * ns)
                    hb16s[slot][d, r, :sw] = jnp.concatenate([h[d][0][0], h[d][1][0]], axis=0).astype(BF16)
                    hb16s[slot][d, r, sw:] = jnp.concatenate([h[d][0][1], h[d][1][1]], axis=0).astype(BF16)
        for d in dirs:
            for k in range(2):
                hst[d, k * half:(k + 1) * half, :sw] = h[d][k][0]
                hst[d, k * half:(k + 1) * half, sw:] = h[d][k][1]

    def readout(i, slot):
        for d in dirs:
            steps = steps_of(d, i)
            y = _dot_nt(hb16s[slot][d], cm_ref[d])
            y_ref[steps] = y_ref[steps] + y.reshape(S5_TC, ns, LANES)

    project(0, 0)
    if want_y:
        hb1[...] = jnp.zeros(hb1.shape, BF16)

        def skip(n, carry):
            steps = pl.ds(n * S5_TC, S5_TC)
            y_ref[steps] = d_ref[...] * u_ref[steps]
            return carry

        lax.fori_loop(0, n_tc, skip, 0)

    def body(k, carry):
        i = 2 * k
        scan(0)
        if want_y:
            readout(jnp.maximum(i - 1, 0), 1)
        project(i + 1, 1)
        scan(1)
        if want_y:
            readout(i, 0)
        project(jnp.minimum(i + 2, n_tc - 1), 0)
        return carry

    lax.fori_loop(0, n_tc // 2, body, 0)
    for d in dirs:
        hfin_ref[d] = hst[d]
    if want_y:
        readout(n_tc - 1, 1)


def _s5_call(u_tm, part, a, bmat, cmat, dskip, chain, layer, want_y):
    zero_init = chain is None
    tm_spec = pl.BlockSpec((S5_SEQ, S5_NSEQ, LANES), lambda g: (0, 0, g))

    def mat_spec(rows):
        return pl.BlockSpec((None, 2, None, rows, 2 * S5_SW), lambda g: (layer, 0, g, 0, 0))

    in_specs = [pl.BlockSpec((S5_SEQ, S5_NSEQ, LANES), lambda g: (0, part, g)),
                mat_spec(LANES), mat_spec(LANES), mat_spec(S5_NSEQ),
                pl.BlockSpec((None, None, 1, LANES), lambda g: (layer, g, 0, 0))]
    args = [u_tm, bmat, cmat, a, dskip]
    state_spec = pl.BlockSpec((2, None, S5_NSEQ, 2 * S5_SW), lambda g: (0, g, 0, 0))
    state_shape = jax.ShapeDtypeStruct((2, S5_NGB, S5_NSEQ, 2 * S5_SW), F32)
    if not zero_init:
        z, s0 = chain
        in_specs += [state_spec, pl.BlockSpec((2, None, s0.shape[2], 2 * S5_SW), lambda g: (0, g, 0, 0))]
        args += [z, s0]
    out_specs, out_shape = [], []
    if want_y:
        out_specs.append(tm_spec)
        out_shape.append(jax.ShapeDtypeStruct((S5_SEQ, S5_NSEQ, S5_W), F32))
    out_specs.append(state_spec)
    out_shape.append(state_shape)
    res = pl.pallas_call(
        functools.partial(_s5_kernel, want_y=want_y, zero_init=zero_init),
        grid=(S5_NGB,),
        in_specs=in_specs,
        out_specs=out_specs,
        out_shape=out_shape,
        scratch_shapes=[pltpu.VMEM((2, S5_TC * S5_NSEQ, 2 * S5_SW), F32),
                        pltpu.VMEM((2, S5_TC * S5_NSEQ, 2 * S5_SW), F32),
                        pltpu.VMEM((2, S5_TC * S5_NSEQ, 2 * S5_SW), BF16),
                        pltpu.VMEM((2, S5_TC * S5_NSEQ, 2 * S5_SW), BF16),
                        pltpu.VMEM((2, S5_NSEQ, 2 * S5_SW), F32)],
        compiler_params=pltpu.CompilerParams(dimension_semantics=("parallel",),
                                             vmem_limit_bytes=VMEM_LIMIT),
        name="s5_scan",
    )(*args)
    return (res[0], res[1]) if want_y else (None, res[0])


def _out_kernel(x_ref, ohc_ref, ohs_ref, y5c_ref, y5s_ref, g1_ref, sh2_ref, sc2_ref, g2_ref, nffn_ref, nfin_ref,
                wglu_hbm, wout_hbm, wg_hbm, wu_hbm, wd_hbm, *rest, layer, last_layer):
    n_out = 2 if last_layer else 1
    outs, (wglu_b, wout_b, wg_ref, wu_ref, wd_ref, st_glu, st_sq, st_ff, sem) = rest[:n_out], rest[n_out:]

    @pl.when(_first_step())
    def _():
        _fetch_cast(wglu_hbm, layer, wglu_b, st_glu, sem)
        _fetch_cast(wout_hbm, layer, wout_b, st_sq, sem)
        _fetch_cast(wg_hbm, layer, wg_ref, st_ff, sem)
        _fetch_cast(wu_hbm, layer, wu_ref, st_ff, sem)
        _fetch_cast(wd_hbm, layer, wd_ref, st_sq, sem)

    smp = _is_sample_tile()
    y = jnp.concatenate([jnp.where(smp, y5s_ref[:, s, :], y5c_ref[:, s, :]) for s in range(TILE_S)],
                        axis=0)
    y = _gelu_tanh(y)
    y = y * _sigmoid(_dot(y.astype(BF16), wglu_b[...]))
    ohg = jnp.where(smp, ohs_ref[...], ohc_ref[...]).reshape(TILE_ROWS, HG_W)
    mix = _dot(ohg.astype(BF16), wout_b[:HG_W, :]) + _dot(y.astype(BF16), wout_b[HG_W:, :])
    x = x_ref[...].reshape(TILE_ROWS, D_MODEL) + g1_ref[...] * mix
    h = _rms(x) * nffn_ref[...]
    h = (h * (1.0 + sc2_ref[...]) + sh2_ref[...]).astype(BF16)
    act = (_silu(_dot(h, wg_ref[...])) * _dot(h, wu_ref[...])).astype(BF16)
    x = x + g2_ref[...] * _dot(act, wd_ref[...])
    if not last_layer:
        outs[0][...] = x.reshape(outs[0].shape)
    else:
        oc_ref, os_ref = outs
        x = (_rms(x) * nfin_ref[...]).reshape(oc_ref.shape)

        @pl.when(smp)
        def _():
            os_ref[...] = x

        @pl.when(jnp.logical_not(smp))
        def _():
            oc_ref[...] = x


def _out_call(x3, ohg_c, ohg_s, y5_c, y5_s, mod4, nffn, nfin, wglu, wout, wg, wu, wd, layer, last_layer):
    vec = pl.BlockSpec((1, D_MODEL), lambda sb, tb: (0, 0))
    part_shape = jax.ShapeDtypeStruct((S5_NSEQ, S5_SEQ, D_MODEL), F32)
    if last_layer:
        out_specs = [_part_tile_spec(D_MODEL, _ctx_index), _part_tile_spec(D_MODEL, _smp_index)]
        out_shape = [part_shape, part_shape]
    else:
        out_specs = _tile_spec(D_MODEL)
        out_shape = jax.ShapeDtypeStruct(x3.shape, F32)
    hbm = pl.BlockSpec(memory_space=pl.ANY)
    return pl.pallas_call(
        functools.partial(_out_kernel, layer=layer, last_layer=last_layer),
        grid=(ALL_SEQ // TILE_S, TIME_TILES),
        in_specs=[_tile_spec(D_MODEL),
                  _part_tile_spec(HG_W, _ctx_index), _part_tile_spec(HG_W, _smp_index),
                  _part_tm_tile_spec(S5_W, _ctx_index), _part_tm_tile_spec(S5_W, _smp_index),
                  _mod_spec(layer, 2), _mod_spec(layer, 3), _mod_spec(layer, 4), _mod_spec(layer, 5),
                  vec, vec, hbm, hbm, hbm, hbm, hbm],
        out_specs=out_specs,
        out_shape=out_shape,
        scratch_shapes=[pltpu.VMEM((S5_W, S5_W), BF16), pltpu.VMEM((D_MODEL, D_MODEL), BF16),
                        pltpu.VMEM((D_MODEL, D_FF), BF16), pltpu.VMEM((D_MODEL, D_FF), BF16),
                        pltpu.VMEM((D_FF, D_MODEL), BF16),
                        pltpu.VMEM((2, FETCH_ROWS, S5_W), F32), pltpu.VMEM((2, FETCH_ROWS, D_MODEL), F32),
                        pltpu.VMEM((2, FETCH_ROWS // 2, D_FF), F32), pltpu.SemaphoreType.DMA((2,))],
        compiler_params=pltpu.CompilerParams(dimension_semantics=("arbitrary", "arbitrary"),
                                             vmem_limit_bytes=VMEM_LIMIT),
        name="out_ffn",
    )(x3, ohg_c, ohg_s, y5_c, y5_s, mod4, mod4, mod4, mod4, nffn, nfin, wglu, wout, wg, wu, wd)


def _s5_state_to_blocks(s):
    n = s.shape[0]
    s = s.reshape(n, 2, S5_NGB, S5_GB, S5_P, 2)
    return jnp.transpose(s, (1, 2, 0, 5, 3, 4)).reshape(2, S5_NGB, n, 2 * S5_SW)


def _s5_blocks_to_state(h):
    n = h.shape[2]
    h = h.reshape(2, S5_NGB, n, 2, S5_GB, S5_P)
    return jnp.transpose(h, (2, 0, 1, 4, 5, 3)).reshape(n, 2, S5_GROUPS, S5_P, 2)


def kernel(x_prompt, x_sample, state_hgrn, state_s5, c, c_ctx, w_mod, b_mod, norm_mix, norm_ffn, norm_final, w_in, w_out, hg_lb_logits, hg_norm, s5_lam_re, s5_lam_im, s5_log_dt, s5_b_re, s5_b_im, s5_c_re, s5_c_im, s5_d, s5_w_glu, w_gate, w_up, w_down):
    n_ctx, ctx_len, _ = x_prompt.shape
    n_dec, dec_len, _ = x_sample.shape
    assert ctx_len == S5_SEQ and n_ctx == S5_NSEQ and n_dec * dec_len == S5_NSEQ * S5_SEQ

    cond = jnp.concatenate([c_ctx[None, :], c, jnp.zeros((SUBLANES - 1 - n_dec, D_MODEL), F32)], axis=0)
    mod4 = _mod_call(cond, w_mod, b_mod).reshape(DEPTH, SUBLANES, 1, 6 * D_MODEL)

    s5_a, s5_bmat, s5_cmat = _s5_params(s5_lam_re, s5_lam_im, s5_log_dt, s5_b_re, s5_b_im, s5_c_re, s5_c_im)
    s5_dskip = s5_d.reshape(DEPTH, S5_NGB, 1, LANES)
    nfin = norm_final.reshape(1, D_MODEL)

    assert dec_len // S5_SEQ == TILE_S and n_dec + 1 <= SUBLANES
    tok = (S5_NSEQ, S5_SEQ, D_MODEL)
    ctx_rows = S5_NSEQ * S5_SEQ
    xs = (x_prompt.reshape(tok), x_sample.reshape(tok))
    ctx_fin, smp_fin, s5_finals = None, None, []
    for l in range(DEPTH):
        proj3, u_tm, x_all = _in_call(xs, norm_mix[l].reshape(1, D_MODEL), mod4, w_in, l)
        proj = proj3.reshape(ALL_SEQ * S5_SEQ, HG_IN_W)
        gain = hg_norm[l].reshape(1, HG_D)
        ohg_c, ctx_fin = _hgrn_call(proj, 0, hg_lb_logits, gain, None, ctx_fin, l, n_ctx, ctx_len)
        ohg_s, smp_fin = _hgrn_call(proj, ctx_rows, hg_lb_logits, gain, state_hgrn, smp_fin, l, n_dec, dec_len)
        y5_c, s5_fin = _s5_call(u_tm, 0, s5_a, s5_bmat, s5_cmat, s5_dskip, None, l, True)
        _, z = _s5_call(u_tm, 1, s5_a, s5_bmat, s5_cmat, s5_dskip, None, l, False)
        y5_s, _ = _s5_call(u_tm, 1, s5_a, s5_bmat, s5_cmat, s5_dskip,
                           (z, _s5_state_to_blocks(state_s5[:, l])), l, True)
        last = l == DEPTH - 1
        res = _out_call(x_all, ohg_c.reshape(S5_NSEQ, S5_SEQ, HG_W), ohg_s.reshape(S5_NSEQ, S5_SEQ, HG_W),
                        y5_c, y5_s, mod4, norm_ffn[l].reshape(1, D_MODEL), nfin,
                        s5_w_glu, w_out, w_gate, w_up, w_down, l, last)
        xs = res if last else (res,)
        s5_finals.append(_s5_blocks_to_state(s5_fin))
    y_prompt, y_sample = xs
    return (y_prompt.reshape(x_prompt.shape), y_sample.reshape(x_sample.shape),
            ctx_fin, jnp.stack(s5_finals, axis=1))
```

```python
import functools
import math

import jax
import jax.numpy as jnp
import numpy as np
from jax import lax
from jax.experimental import pallas as pl
from jax.experimental.pallas import tpu as pltpu

F32 = jnp.float32
BF16 = jnp.bfloat16

LANES = 128
SUBLANES = 8

D_MODEL = 1024
DEPTH = 2
GRID_W = 64
HG_W = 512
HG_HEADS = 4
HG_D = HG_W // HG_HEADS
S5_W = 512
S5_CH = 16
S5_GROUPS = S5_W // S5_CH
S5_P = 64
S5_GB = LANES // S5_CH
S5_NGB = S5_GROUPS // S5_GB
S5_SW = S5_GB * S5_P
HG_IN_W = 5 * HG_W
IN_W = HG_IN_W + S5_W
D_FF = 2816
EPS = 1e-6

HG_CHUNK = 128
HG_LEVELS = (64, 32, 16, 8, 4, 2, 1)
S5_SEQ = 256
S5_NSEQ = 16
S5_TC = 16

TILE_S = SUBLANES
TILE_T = 64
TILE_ROWS = TILE_S * TILE_T
TIME_TILES = S5_SEQ // TILE_T
ALL_SEQ = 2 * S5_NSEQ
CTX_TILES = S5_NSEQ // TILE_S
CAST_ROWS = 128
MOD_TILE_N = 1536
VMEM_LIMIT = 56 * 1024 * 1024


def _sigmoid(x):
    return 1.0 / (1.0 + jnp.exp(-x))


def _silu(x):
    return x * _sigmoid(x)


def _gelu_tanh(x):
    return 0.5 * x * (1.0 + jnp.tanh(math.sqrt(2.0 / math.pi) * (x + 0.044715 * (x * x * x))))


def _rms(x):
    return x * lax.rsqrt(jnp.mean(x * x, axis=-1, keepdims=True) + EPS)


def _dot(a, b):
    return jnp.dot(a, b, preferred_element_type=F32)


def _dot_nt(a, b):
    return lax.dot_general(a, b, (((1,), (1,)), ((), ())), preferred_element_type=F32)


def _dot_tn(a, b):
    return lax.dot_general(a, b, (((0,), (0,)), ((), ())), preferred_element_type=F32)


def _layer_spec(shape, layer):
    nd = len(shape)
    return pl.BlockSpec((None,) + tuple(shape), lambda *_: (layer,) + (0,) * nd, pipeline_mode=pl.Buffered(1))


def _mod_kernel(cond_ref, w_ref, b_ref, o_ref):
    a = _silu(cond_ref[...]).astype(BF16)
    o_ref[0] = _dot(a, w_ref[0].astype(BF16)) + b_ref[0]


def _mod_call(cond, w_mod, b_mod):
    n_cond = cond.shape[0]
    n_out = w_mod.shape[-1]
    return pl.pallas_call(
        _mod_kernel,
        grid=(DEPTH, n_out // MOD_TILE_N),
        in_specs=[
            pl.BlockSpec((n_cond, D_MODEL), lambda l, j: (0, 0)),
            pl.BlockSpec((1, D_MODEL, MOD_TILE_N), lambda l, j: (l, 0, j)),
            pl.BlockSpec((1, 1, MOD_TILE_N), lambda l, j: (l, 0, j)),
        ],
        out_specs=pl.BlockSpec((1, n_cond, MOD_TILE_N), lambda l, j: (l, 0, j)),
        out_shape=jax.ShapeDtypeStruct((DEPTH, n_cond, n_out), F32),
        compiler_params=pltpu.CompilerParams(dimension_semantics=("parallel", "parallel"),
                                             vmem_limit_bytes=VMEM_LIMIT),
        name="adaln_mod",
    )(cond, w_mod, b_mod.reshape(DEPTH, 1, n_out))


def _first_step():
    return jnp.logical_and(pl.program_id(0) == 0, pl.program_id(1) == 0)


def _cast_rows(src_ref, dst_ref):
    for r in range(0, src_ref.shape[0], CAST_ROWS):
        dst_ref[r:r + CAST_ROWS, :] = src_ref[r:r + CAST_ROWS, :].astype(BF16)


def _grid_pos_tile(omega, tb):
    nf = omega.shape[-1]
    s_idx = lax.broadcasted_iota(jnp.int32, (TILE_S, nf), 0)
    j_idx = lax.broadcasted_iota(jnp.int32, (TILE_T, nf), 0)
    t0 = tb * TILE_T
    row = (s_idx * (S5_SEQ // GRID_W) + t0 // GRID_W).astype(F32) * omega
    col = (j_idx + t0 % GRID_W).astype(F32) * omega
    enc_r = jnp.concatenate([jnp.sin(row), jnp.cos(row)], axis=-1)
    enc_c = jnp.concatenate([jnp.sin(col), jnp.cos(col)], axis=-1)
    shape = (TILE_S, TILE_T, 2 * nf)
    return jnp.concatenate([jnp.broadcast_to(enc_r[:, None, :], shape),
                            jnp.broadcast_to(enc_c[None, :, :], shape)], axis=-1)


def _is_sample_tile():
    return pl.program_id(0) >= CTX_TILES


def _in_kernel(*refs, first_layer):
    if first_layer:
        xc_ref, xs_ref, om_ref, gain_ref, sh_ref, sc_ref, w_ref, proj_ref, u_ref, xo_ref, wb_ref = refs
        x = jnp.where(_is_sample_tile(), xs_ref[...] + _grid_pos_tile(om_ref[...], pl.program_id(1)), xc_ref[...])
        xo_ref[...] = x
    else:
        x_ref, gain_ref, sh_ref, sc_ref, w_ref, proj_ref, u_ref, wb_ref = refs
        x = x_ref[...]

    @pl.when(_first_step())
    def _():
        _cast_rows(w_ref, wb_ref)

    x = x.reshape(TILE_ROWS, D_MODEL)
    h = _rms(x) * gain_ref[...]
    h = (h * (1.0 + sc_ref[...]) + sh_ref[...]).astype(BF16)
    proj_ref[...] = _dot(h, wb_ref[:, :HG_IN_W]).reshape(proj_ref.shape)
    u = _dot(h, wb_ref[:, HG_IN_W:])
    for s in range(TILE_S):
        u_ref[:, s, :] = u[s * TILE_T:(s + 1) * TILE_T, :]


def _tile_spec(width):
    return pl.BlockSpec((TILE_S, TILE_T, width), lambda sb, tb: (sb, tb, 0))


def _tm_tile_spec(width):
    return pl.BlockSpec((TILE_T, TILE_S, width), lambda sb, tb: (tb, sb, 0))


def _ctx_index(sb, tb):
    on = sb < CTX_TILES
    return jnp.where(on, sb, CTX_TILES - 1), jnp.where(on, tb, TIME_TILES - 1)


def _smp_index(sb, tb):
    on = sb >= CTX_TILES
    return jnp.where(on, sb - CTX_TILES, 0), jnp.where(on, tb, 0)


def _part_tile_spec(width, index):
    return pl.BlockSpec((TILE_S, TILE_T, width), lambda sb, tb: index(sb, tb) + (0,))


def _part_tm_tile_spec(width, index):
    return pl.BlockSpec((TILE_T, TILE_S, width), lambda sb, tb: index(sb, tb)[::-1] + (0,))


def _mod_spec(layer, col):
    return pl.BlockSpec((None, None, 1, D_MODEL),
                        lambda sb, tb: (layer, jnp.maximum(sb - (CTX_TILES - 1), 0), 0, col))


def _in_call(xs, gain, mod4, w_in, layer):
    first_layer = len(xs) == 2
    if first_layer:
        assert GRID_W % TILE_T == 0 and S5_SEQ % GRID_W == 0
        nf = D_MODEL // 4
        omega = 1.0 / (np.float32(10000.0) ** (np.arange(nf, dtype=np.float32) / np.float32(nf)))
        in_specs = [_part_tile_spec(D_MODEL, _ctx_index), _part_tile_spec(D_MODEL, _smp_index),
                    pl.BlockSpec((1, nf), lambda sb, tb: (0, 0))]
        args = list(xs) + [jnp.asarray(omega.reshape(1, nf), F32)]
    else:
        in_specs = [_tile_spec(D_MODEL)]
        args = list(xs)
    in_specs += [
        pl.BlockSpec((1, D_MODEL), lambda sb, tb: (0, 0)),
        _mod_spec(layer, 0),
        _mod_spec(layer, 1),
        _layer_spec((D_MODEL, IN_W), layer),
    ]
    args += [gain, mod4, mod4, w_in]
    out_specs = [_tile_spec(HG_IN_W), _tm_tile_spec(S5_W)]
    out_shape = [jax.ShapeDtypeStruct((ALL_SEQ, S5_SEQ, HG_IN_W), F32),
                 jax.ShapeDtypeStruct((S5_SEQ, ALL_SEQ, S5_W), F32)]
    if first_layer:
        out_specs.append(_tile_spec(D_MODEL))
        out_shape.append(jax.ShapeDtypeStruct((ALL_SEQ, S5_SEQ, D_MODEL), F32))
    res = pl.pallas_call(
        functools.partial(_in_kernel, first_layer=first_layer),
        grid=(ALL_SEQ // TILE_S, TIME_TILES),
        in_specs=in_specs,
        out_specs=out_specs,
        out_shape=out_shape,
        scratch_shapes=[pltpu.VMEM((D_MODEL, IN_W), BF16)],
        compiler_params=pltpu.CompilerParams(dimension_semantics=("arbitrary", "arbitrary"),
                                             vmem_limit_bytes=VMEM_LIMIT),
        name="in_proj",
    )(*args)
    return (res[0], res[1], res[2]) if first_layer else (res[0], res[1], xs[0])


def _pair_boundary(b, m, rev):
    c = b.shape[0]
    span = 2 * m
    at = m if rev else m - 1
    if span >= SUBLANES:
        b3 = b.reshape(c // span, span, LANES)
        return jnp.broadcast_to(b3[:, at:at + 1, :], b3.shape).reshape(c, LANES)
    b3 = b.reshape(c // SUBLANES, SUBLANES, LANES)
    sub = lax.broadcasted_iota(jnp.int32, b3.shape, 1)
    out = None
    for p in range(SUBLANES // span):
        piece = jnp.broadcast_to(b3[:, p * span + at:p * span + at + 1, :], b3.shape)
        out = piece if out is None else jnp.where(sub >= p * span, piece, out)
    return out.reshape(c, LANES)


def _neg_abs(x):
    bits = lax.bitcast_convert_type(x, jnp.uint32) | jnp.uint32(0x80000000)
    return lax.bitcast_convert_type(bits, F32)


def _hg_gates(chains, scale):
    outs = []
    for q, fl, lb, tri in chains:
        sig = _sigmoid(fl)
        forget = lb + (1.0 - lb) * sig
        logf = jnp.log2(forget)
        key = (1.0 - lb) * (1.0 - sig)
        hi = logf.astype(BF16)
        r1 = logf - hi.astype(F32)
        mid = r1.astype(BF16)
        lo = (r1 - mid.astype(F32)).astype(BF16)
        parts = _dot(tri, jnp.concatenate([hi, mid, lo], axis=1))
        b2 = parts[:, :LANES] + parts[:, LANES:2 * LANES] + parts[:, 2 * LANES:]
        outs.append((_silu(q) * scale, key, b2, forget))
    return outs


def _hg_scores(chains, code, eye):
    c = chains[0][0].shape[0]
    o_inter = []
    for qh, key, b2, forget, v, st_ref, rev in chains:
        b_edge = b2[0:1, :] if rev else b2[c - 1:c, :]
        st = st_ref[...]
        o_inter.append(_dot_nt((qh * jnp.exp2(b2)).astype(BF16), st.astype(BF16)))
        k_end = key * jnp.exp2(b_edge - b2)
        st_ref[...] = jnp.exp2(b_edge) * st + _dot_tn(v.astype(BF16), k_end.astype(BF16))

    out = []
    for o, (qh, key, b2, forget, v, st_ref, rev) in zip(o_inter, chains):
        scores = jnp.where(eye, jnp.sum(qh * key, axis=-1, keepdims=True), 0.0)
        qb, kb = qh.astype(BF16), key.astype(BF16)
        for m in HG_LEVELS:
            k = int(math.log2(m)) + 1
            if m == 1:
                p = _dot_nt((qh * forget).astype(BF16), kb)
            else:
                e = jnp.exp2(_neg_abs(b2 - _pair_boundary(b2, m, rev))).astype(BF16)
                p = _dot_nt(qb * e, kb * e)
            scores = jnp.where(code == (-k if rev else k), p, scores)
        out.append((o, scores.astype(BF16)))
    return out


def _hgrn_kernel(*refs, layer, n_chunks, n_heads, zero_init):
    refs = list(refs)
    q_ref, ff_ref, fb_ref, v_ref, g_ref, lbl_ref, gain_ref, code_ref, tri_ref = refs[:9]
    s0_ref = None if zero_init else refs[9]
    o_ref, sfin_ref, st_ref, ob_ref, ab_ref, oi_ref, sc_ref = refs[-7:]
    c = HG_CHUNK
    code = code_ref[...]
    eye = code == 0
    chains = [(hd, d) for hd in range(n_heads) for d in (0, 1)]

    def lanes(hd):
        return slice(hd * HG_D, (hd + 1) * HG_D)

    def lower_bound(hd, d):
        lg = lbl_ref[d, :, lanes(hd)]
        ex = jnp.exp(lg - jnp.max(lg, axis=0, keepdims=True))
        soft = ex / jnp.sum(ex, axis=0, keepdims=True)
        return jnp.sum(soft[:layer + 1], axis=0, keepdims=True) - soft[0:1]

    lb = [lower_bound(hd, d) for hd, d in chains]
    scale = HG_D ** -0.5

    def rows_of(d, i):
        n = (n_chunks - 1 - i) if d else i
        return pl.ds(n * c if isinstance(n, int) else pl.multiple_of(n * c, c), c)

    for ch, (hd, d) in enumerate(chains):
        st_ref[ch] = jnp.zeros((HG_D, HG_D), F32) if zero_init else s0_ref[d, hd].T

    def gates(i):
        return _hg_gates([(q_ref[rows_of(d, i), lanes(hd)], (fb_ref if d else ff_ref)[rows_of(d, i), lanes(hd)],
                           lb[ch], tri_ref[d]) for ch, (hd, d) in enumerate(chains)], scale)

    def scores(i, ab):
        return _hg_scores([ab[ch] + (v_ref[rows_of(d, i), lanes(hd)], st_ref.at[ch], bool(d))
                           for ch, (hd, d) in enumerate(chains)], code, eye)

    def emit(i, oi_sc):
        for ch, (hd, d) in enumerate(chains):
            rows = rows_of(d, i)
            (ob_ref if d else o_ref)[rows, lanes(hd)] = (
                oi_sc[ch][0] + _dot(oi_sc[ch][1], v_ref[rows, lanes(hd)].astype(BF16)))

    if n_chunks <= 2:
        ab = [gates(i) for i in range(n_chunks)]
        for i in range(n_chunks):
            emit(i, scores(i, ab[i]))
    else:
        n_ab = ab_ref.shape[1]

        def put_ab(ab):
            for ch in range(len(chains)):
                for k in range(n_ab):
                    ab_ref[ch, k] = ab[ch][k]

        def get_oi_sc():
            return [(oi_ref[ch], sc_ref[ch]) for ch in range(len(chains))]

        put_ab(gates(0))
        oi_ref[...] = jnp.zeros(oi_ref.shape, F32)
        sc_ref[...] = jnp.zeros(sc_ref.shape, BF16)

        def body(i, carry):
            emit(jnp.maximum(i - 1, 0), get_oi_sc())
            res = scores(i, [tuple(ab_ref[ch, k] for k in range(n_ab)) for ch in range(len(chains))])
            for ch in range(len(chains)):
                oi_ref[ch] = res[ch][0]
                sc_ref[ch] = res[ch][1]
            put_ab(gates(jnp.minimum(i + 1, n_chunks - 1)))
            return carry

        lax.fori_loop(0, n_chunks, body, 0)
        emit(n_chunks - 1, get_oi_sc())
    if len(sfin_ref.shape) == 5:
        for other in range(sfin_ref.shape[0]):
            if other != layer:
                sfin_ref[other] = jnp.zeros(sfin_ref.shape[1:], F32)
        sfin_ref = sfin_ref.at[layer]
    for ch, (hd, d) in enumerate(chains):
        sfin_ref[d, hd] = st_ref[ch].T

    def finish(n, carry):
        rows = pl.ds(pl.multiple_of(n * (2 * c), 2 * c), 2 * c)
        for hd in range(n_heads):
            o_ref[rows, lanes(hd)] = (_rms(o_ref[rows, lanes(hd)] + ob_ref[rows, lanes(hd)]) * gain_ref[...]
                                      * _silu(g_ref[rows, lanes(hd)]))
        return carry

    lax.fori_loop(0, n_chunks // 2, finish, 0)


def _hgrn_call(proj, row0, lb_logits, gain, state, finals, layer, n_seq, seq_len):
    rows = n_seq * seq_len
    seq0 = row0 // seq_len
    zero_init = state is None
    n_chunks = seq_len // HG_CHUNK
    assert n_chunks % 2 == 0
    nh = 2
    hw = nh * HG_D
    n_hb = HG_HEADS // nh

    def col_spec(k):
        return pl.BlockSpec((seq_len, hw), lambda b, h: (seq0 + b, k * n_hb + h))

    t, s = np.meshgrid(np.arange(HG_CHUNK), np.arange(HG_CHUNK), indexing="ij")
    lvl = np.where(t == s, 0, np.floor(np.log2(np.maximum(t ^ s, 1))).astype(np.int32) + 1)
    code = jnp.asarray(np.where(t > s, lvl, -lvl), jnp.int32)
    tri = jnp.asarray(np.stack([s <= t, s >= t]), BF16)

    in_specs = [col_spec(0), col_spec(1), col_spec(2), col_spec(3), col_spec(4),
                pl.BlockSpec((2, DEPTH, hw), lambda b, h: (0, 0, h)),
                pl.BlockSpec((1, HG_D), lambda b, h: (0, 0)),
                pl.BlockSpec((HG_CHUNK, HG_CHUNK), lambda b, h: (0, 0)),
                pl.BlockSpec((2, HG_CHUNK, HG_CHUNK), lambda b, h: (0, 0, 0))]
    args = [proj] * 5 + [lb_logits, gain, code, tri]
    if not zero_init:
        in_specs.append(pl.BlockSpec((None, None, 2, nh, HG_D, HG_D), lambda b, h: (b, layer, 0, h, 0, 0)))
        args.append(state)
    aliases = {}
    if finals is None:
        fin_spec = pl.BlockSpec((None, DEPTH, 2, nh, HG_D, HG_D), lambda b, h: (b, 0, 0, h, 0, 0))
    else:
        fin_spec = pl.BlockSpec((None, None, 2, nh, HG_D, HG_D), lambda b, h: (b, layer, 0, h, 0, 0))
        aliases[len(args)] = 1
        in_specs.append(pl.BlockSpec(memory_space=pl.ANY))
        args.append(finals)
    n_ch = 2 * nh
    return pl.pallas_call(
        functools.partial(_hgrn_kernel, layer=layer, n_chunks=n_chunks, n_heads=nh, zero_init=zero_init),
        grid=(n_seq, n_hb),
        in_specs=in_specs,
        out_specs=[pl.BlockSpec((seq_len, hw), lambda b, h: (b, h)), fin_spec],
        out_shape=[jax.ShapeDtypeStruct((rows, HG_W), F32),
                   jax.ShapeDtypeStruct((n_seq, DEPTH, 2, HG_HEADS, HG_D, HG_D), F32)],
        input_output_aliases=aliases,
        scratch_shapes=[pltpu.VMEM((n_ch, HG_D, HG_D), F32), pltpu.VMEM((seq_len, hw), F32),
                        pltpu.VMEM((n_ch, 4, HG_CHUNK, HG_D), F32), pltpu.VMEM((n_ch, HG_CHUNK, HG_D), F32),
                        pltpu.VMEM((n_ch, HG_CHUNK, HG_CHUNK), BF16)],
        compiler_params=pltpu.CompilerParams(dimension_semantics=("parallel", "parallel"),
                                             vmem_limit_bytes=VMEM_LIMIT),
        name="hgrn2_mixer",
    )(*args)


def _s5_params_kernel(lr_ref, li_ref, ldt_ref, btr_ref, bti_ref, cr_ref, ci_ref, a_ref, bm_ref, cm_ref):
    for k in range(lr_ref.shape[0]):
        _s5_params_block(*(r.at[k] for r in (lr_ref, li_ref, ldt_ref, btr_ref, bti_ref, cr_ref, ci_ref,
                                             a_ref, bm_ref, cm_ref)))


def _s5_params_block(lr_ref, li_ref, ldt_ref, btr_ref, bti_ref, cr_ref, ci_ref, a_ref, bm_ref, cm_ref):
    sw = S5_SW
    lr = jnp.minimum(lr_ref[...], -1e-4)
    li = li_ref[...]
    dt = jnp.exp(ldt_ref[...])
    mag = jnp.exp(lr * dt)
    ab_re = mag * jnp.cos(li * dt)
    ab_im = mag * jnp.sin(li * dt)
    nr = ab_re - 1.0
    den = lr * lr + li * li
    z_re = (nr * lr + ab_im * li) / den
    z_im = (ab_im * lr - nr * li) / den

    p_idx = lax.broadcasted_iota(jnp.int32, (S5_P, sw), 0)
    col = lax.broadcasted_iota(jnp.int32, (S5_P, sw), 1)
    for g in range(S5_GB):
        place = (col == p_idx + g * S5_P).astype(BF16)
        zr, zi = z_re[g:g + 1, :], z_im[g:g + 1, :]
        btr, bti = btr_ref[g], bti_ref[g]
        rows = slice(g * S5_CH, (g + 1) * S5_CH)
        bm_ref[rows, :sw] = _dot((zr * btr - zi * bti).astype(BF16), place).astype(BF16)
        bm_ref[rows, sw:] = _dot((zr * bti + zi * btr).astype(BF16), place).astype(BF16)
        cm_ref[rows, :sw] = _dot(cr_ref[g].astype(BF16), place).astype(BF16)
        cm_ref[rows, sw:] = _dot((-ci_ref[g]).astype(BF16), place).astype(BF16)
        a_ref[:, g * S5_P:(g + 1) * S5_P] = jnp.broadcast_to(ab_re[g:g + 1, :], (S5_NSEQ, S5_P))
        a_ref[:, sw + g * S5_P:sw + (g + 1) * S5_P] = jnp.broadcast_to(ab_im[g:g + 1, :], (S5_NSEQ, S5_P))


def _s5_params(lam_re, lam_im, log_dt, b_re, b_im, c_re, c_im):
    nb = DEPTH * 2 * S5_NGB
    gp = (nb, S5_GB, S5_P)
    gcp = (nb, S5_GB, S5_CH, S5_P)
    bt_re = jnp.swapaxes(b_re, -1, -2).reshape(gcp)
    bt_im = jnp.swapaxes(b_im, -1, -2).reshape(gcp)
    ldt = jnp.broadcast_to(log_dt.reshape(nb, S5_GB, 1), gp)
    per = S5_NGB
    gp_spec = pl.BlockSpec((per, S5_GB, S5_P), lambda i: (i, 0, 0))
    gcp_spec = pl.BlockSpec((per, S5_GB, S5_CH, S5_P), lambda i: (i, 0, 0, 0))
    a, bmat, cmat = pl.pallas_call(
        _s5_params_kernel,
        grid=(nb // per,),
        in_specs=[gp_spec] * 3 + [gcp_spec] * 4,
        out_specs=[pl.BlockSpec((per, S5_NSEQ, 2 * S5_SW), lambda i: (i, 0, 0)),
                   pl.BlockSpec((per, LANES, 2 * S5_SW), lambda i: (i, 0, 0)),
                   pl.BlockSpec((per, LANES, 2 * S5_SW), lambda i: (i, 0, 0))],
        out_shape=[jax.ShapeDtypeStruct((nb, S5_NSEQ, 2 * S5_SW), F32),
                   jax.ShapeDtypeStruct((nb, LANES, 2 * S5_SW), BF16),
                   jax.ShapeDtypeStruct((nb, LANES, 2 * S5_SW), BF16)],
        compiler_params=pltpu.CompilerParams(dimension_semantics=("parallel",)),
        name="s5_params",
    )(lam_re.reshape(gp), lam_im.reshape(gp), ldt, bt_re, bt_im, c_re.reshape(gcp), c_im.reshape(gcp))
    lead = (DEPTH, 2, S5_NGB)
    return (a.reshape(lead + a.shape[1:]), bmat.reshape(lead + bmat.shape[1:]), cmat.reshape(lead + cmat.shape[1:]))


def _s5_kernel(*refs, want_y, zero_init):
    refs = list(refs)
    u_ref, bm_ref, cm_ref, a_ref, d_ref = refs[:5]
    rest = refs[5:]
    z_ref, s0_ref = (None, None) if zero_init else (rest.pop(0), rest.pop(0))
    y_ref = rest.pop(0) if want_y else None
    hfin_ref, hbuf0, hbuf1, hb0, hb1, hst = rest
    hbufs, hb16s = (hbuf0, hbuf1), (hb0, hb1)
    ns, sw = S5_NSEQ, S5_SW
    half = ns // 2
    n_tc = S5_SEQ // S5_TC
    blk = S5_TC * ns
    dirs = (0, 1)

    if zero_init:
        for d in dirs:
            hst[d] = jnp.zeros((ns, 2 * sw), F32)
    else:
        n_long = s0_ref.shape[1]
        pieces = ns // n_long
        for d in dirs:
            pr, pi = a_ref[d, 0:1, :sw], a_ref[d, 0:1, sw:]
            for _ in range(int(math.log2(S5_SEQ))):
                pr, pi = pr * pr - pi * pi, 2.0 * (pr * pi)
            for b in range(n_long):
                hr, hi = s0_ref[d, b:b + 1, :sw], s0_ref[d, b:b + 1, sw:]
                for k in (range(pieces - 1, -1, -1) if d else range(pieces)):
                    r = b * pieces + k
                    hst[d, r:r + 1, :sw] = hr
                    hst[d, r:r + 1, sw:] = hi
                    zr, zi = z_ref[d, r:r + 1, :sw], z_ref[d, r:r + 1, sw:]
                    hr, hi = pr * hr - pi * hi + zr, pr * hi + pi * hr + zi

    def steps_of(d, i):
        return pl.ds(((n_tc - 1 - i) if d else i) * S5_TC, S5_TC)

    def project(i, slot):
        for d in dirs:
            u = u_ref[steps_of(d, i)].reshape(blk, LANES)
            hbufs[slot][d] = _dot(u.astype(BF16), bm_ref[d])

    def scan(slot):
        a = [(a_ref[d, :half, :sw], a_ref[d, :half, sw:]) for d in dirs]
        h = [[(hst[d, k * half:(k + 1) * half, :sw], hst[d, k * half:(k + 1) * half, sw:]) for k in range(2)]
             for d in dirs]
        for jj in range(S5_TC):
            for d in dirs:
                j = S5_TC - 1 - jj if d else jj
                ar, ai = a[d]
                for k in range(2):
                    r = slice(j * ns + k * half, j * ns + (k + 1) * half)
                    hr, hi = h[d][k]
                    h[d][k] = (ar * hr - ai * hi + hbufs[slot][d, r, :sw], ar * hi + ai * hr + hbufs[slot][d, r, sw:])
                if want_y:
                    r = slice(j * ns, (j + 1) * ns)
                    hb16s[slot][d, r, :sw] = jnp.concatenate([h[d][0][0], h[d][1][0]], axis=0).astype(BF16)
                    hb16s[slot][d, r, sw:] = jnp.concatenate([h[d][0][1], h[d][1][1]], axis=0).astype(BF16)
        for d in dirs:
            for k in range(2):
                hst[d, k * half:(k + 1) * half, :sw] = h[d][k][0]
                hst[d, k * half:(k + 1) * half, sw:] = h[d][k][1]

    def readout(i, slot):
        for d in dirs:
            steps = steps_of(d, i)
            y = _dot_nt(hb16s[slot][d], cm_ref[d])
            y_ref[steps] = y_ref[steps] + y.reshape(S5_TC, ns, LANES)

    project(0, 0)
    if want_y:
        hb1[...] = jnp.zeros(hb1.shape, BF16)

        def skip(n, carry):
            steps = pl.ds(n * S5_TC, S5_TC)
            y_ref[steps] = d_ref[...] * u_ref[steps]
            return carry

        lax.fori_loop(0, n_tc, skip, 0)

    def body(k, carry):
        i = 2 * k
        scan(0)
        if want_y:
            readout(jnp.maximum(i - 1, 0), 1)
        project(i + 1, 1)
        scan(1)
        if want_y:
            readout(i, 0)
        project(jnp.minimum(i + 2, n_tc - 1), 0)
        return carry

    lax.fori_loop(0, n_tc // 2, body, 0)
    for d in dirs:
        hfin_ref[d] = hst[d]
    if want_y:
        readout(n_tc - 1, 1)


def _s5_call(u_tm, part, a, bmat, cmat, dskip, chain, layer, want_y):
    zero_init = chain is None
    tm_spec = pl.BlockSpec((S5_SEQ, S5_NSEQ, LANES), lambda g: (0, 0, g))

    def mat_spec(rows):
        return pl.BlockSpec((None, 2, None, rows, 2 * S5_SW), lambda g: (layer, 0, g, 0, 0))

    in_specs = [pl.BlockSpec((S5_SEQ, S5_NSEQ, LANES), lambda g: (0, part, g)),
                mat_spec(LANES), mat_spec(LANES), mat_spec(S5_NSEQ),
                pl.BlockSpec((None, None, 1, LANES), lambda g: (layer, g, 0, 0))]
    args = [u_tm, bmat, cmat, a, dskip]
    state_spec = pl.BlockSpec((2, None, S5_NSEQ, 2 * S5_SW), lambda g: (0, g, 0, 0))
    state_shape = jax.ShapeDtypeStruct((2, S5_NGB, S5_NSEQ, 2 * S5_SW), F32)
    if not zero_init:
        z, s0 = chain
        in_specs += [state_spec, pl.BlockSpec((2, None, s0.shape[2], 2 * S5_SW), lambda g: (0, g, 0, 0))]
        args += [z, s0]
    out_specs, out_shape = [], []
    if want_y:
        out_specs.append(tm_spec)
        out_shape.append(jax.ShapeDtypeStruct((S5_SEQ, S5_NSEQ, S5_W), F32))
    out_specs.append(state_spec)
    out_shape.append(state_shape)
    res = pl.pallas_call(
        functools.partial(_s5_kernel, want_y=want_y, zero_init=zero_init),
        grid=(S5_NGB,),
        in_specs=in_specs,
        out_specs=out_specs,
        out_shape=out_shape,
        scratch_shapes=[pltpu.VMEM((2, S5_TC * S5_NSEQ, 2 * S5_SW), F32),
                        pltpu.VMEM((2, S5_TC * S5_NSEQ, 2 * S5_SW), F32),
                        pltpu.VMEM((2, S5_TC * S5_NSEQ, 2 * S5_SW), BF16),
                        pltpu.VMEM((2, S5_TC * S5_NSEQ, 2 * S5_SW), BF16),
                        pltpu.VMEM((2, S5_NSEQ, 2 * S5_SW), F32)],
        compiler_params=pltpu.CompilerParams(dimension_semantics=("parallel",),
                                             vmem_limit_bytes=VMEM_LIMIT),
        name="s5_scan",
    )(*args)
    return (res[0], res[1]) if want_y else (None, res[0])


def _out_kernel(x_ref, ohc_ref, ohs_ref, y5c_ref, y5s_ref, g1_ref, sh2_ref, sc2_ref, g2_ref, nffn_ref, nfin_ref,
                wglu_ref, wout_ref, wg_ref, wu_ref, wd_ref, *rest, last_layer):
    if last_layer:
        oc_ref, os_ref, wglu_b, wout_b = rest
    else:
        o_ref, wglu_b, wout_b = rest

    @pl.when(_first_step())
    def _():
        _cast_rows(wglu_ref, wglu_b)
        _cast_rows(wout_ref, wout_b)

    smp = _is_sample_tile()
    y = jnp.concatenate([jnp.where(smp, y5s_ref[:, s, :], y5c_ref[:, s, :]) for s in range(TILE_S)],
                        axis=0)
    y = _gelu_tanh(y)
    y = y * _sigmoid(_dot(y.astype(BF16), wglu_b[...]))
    ohg = jnp.where(smp, ohs_ref[...], ohc_ref[...]).reshape(TILE_ROWS, HG_W)
    mix = _dot(ohg.astype(BF16), wout_b[:HG_W, :]) + _dot(y.astype(BF16), wout_b[HG_W:, :])
    x = x_ref[...].reshape(TILE_ROWS, D_MODEL) + g1_ref[...] * mix
    h = _rms(x) * nffn_ref[...]
    h = (h * (1.0 + sc2_ref[...]) + sh2_ref[...]).astype(BF16)
    act = (_silu(_dot(h, wg_ref[...])) * _dot(h, wu_ref[...])).astype(BF16)
    x = x + g2_ref[...] * _dot(act, wd_ref[...])
    if not last_layer:
        o_ref[...] = x.reshape(o_ref.shape)
    else:
        x = (_rms(x) * nfin_ref[...]).reshape(oc_ref.shape)

        @pl.when(smp)
        def _():
            os_ref[...] = x

        @pl.when(jnp.logical_not(smp))
        def _():
            oc_ref[...] = x


def _out_call(x3, ohg_c, ohg_s, y5_c, y5_s, mod4, nffn, nfin, wglu, wout, wg, wu, wd, layer, last_layer):
    vec = pl.BlockSpec((1, D_MODEL), lambda sb, tb: (0, 0))
    part_shape = jax.ShapeDtypeStruct((S5_NSEQ, S5_SEQ, D_MODEL), F32)
    if last_layer:
        out_specs = [_part_tile_spec(D_MODEL, _ctx_index), _part_tile_spec(D_MODEL, _smp_index)]
        out_shape = [part_shape, part_shape]
    else:
        out_specs = _tile_spec(D_MODEL)
        out_shape = jax.ShapeDtypeStruct(x3.shape, F32)
    return pl.pallas_call(
        functools.partial(_out_kernel, last_layer=last_layer),
        grid=(ALL_SEQ // TILE_S, TIME_TILES),
        in_specs=[_tile_spec(D_MODEL),
                  _part_tile_spec(HG_W, _ctx_index), _part_tile_spec(HG_W, _smp_index),
                  _part_tm_tile_spec(S5_W, _ctx_index), _part_tm_tile_spec(S5_W, _smp_index),
                  _mod_spec(layer, 2), _mod_spec(layer, 3), _mod_spec(layer, 4), _mod_spec(layer, 5),
                  vec, vec,
                  _layer_spec((S5_W, S5_W), layer), _layer_spec((D_MODEL, D_MODEL), layer),
                  _layer_spec((D_MODEL, D_FF), layer), _layer_spec((D_MODEL, D_FF), layer),
                  _layer_spec((D_FF, D_MODEL), layer)],
        out_specs=out_specs,
        out_shape=out_shape,
        scratch_shapes=[pltpu.VMEM((S5_W, S5_W), BF16), pltpu.VMEM((D_MODEL, D_MODEL), BF16)],
        compiler_params=pltpu.CompilerParams(dimension_semantics=("arbitrary", "arbitrary"),
                                             vmem_limit_bytes=VMEM_LIMIT),
        name="out_ffn",
    )(x3, ohg_c, ohg_s, y5_c, y5_s, mod4, mod4, mod4, mod4, nffn, nfin, wglu, wout, wg, wu, wd)


def _s5_state_to_blocks(s):
    n = s.shape[0]
    s = s.reshape(n, 2, S5_NGB, S5_GB, S5_P, 2)
    return jnp.transpose(s, (1, 2, 0, 5, 3, 4)).reshape(2, S5_NGB, n, 2 * S5_SW)


def _s5_blocks_to_state(h):
    n = h.shape[2]
    h = h.reshape(2, S5_NGB, n, 2, S5_GB, S5_P)
    return jnp.transpose(h, (2, 0, 1, 4, 5, 3)).reshape(n, 2, S5_GROUPS, S5_P, 2)


def kernel(x_prompt, x_sample, state_hgrn, state_s5, c, c_ctx, w_mod, b_mod, norm_mix, norm_ffn, norm_final, w_in, w_out, hg_lb_logits, hg_norm, s5_lam_re, s5_lam_im, s5_log_dt, s5_b_re, s5_b_im, s5_c_re, s5_c_im, s5_d, s5_w_glu, w_gate, w_up, w_down):
    n_ctx, ctx_len, _ = x_prompt.shape
    n_dec, dec_len, _ = x_sample.shape
    assert ctx_len == S5_SEQ and n_ctx == S5_NSEQ and n_dec * dec_len == S5_NSEQ * S5_SEQ

    cond = jnp.concatenate([c_ctx[None, :], c, jnp.zeros((SUBLANES - 1 - n_dec, D_MODEL), F32)], axis=0)
    mod4 = _mod_call(cond, w_mod, b_mod).reshape(DEPTH, SUBLANES, 1, 6 * D_MODEL)

    w_gate_b, w_up_b, w_down_b = w_gate.astype(BF16), w_up.astype(BF16), w_down.astype(BF16)
    s5_a, s5_bmat, s5_cmat = _s5_params(s5_lam_re, s5_lam_im, s5_log_dt, s5_b_re, s5_b_im, s5_c_re, s5_c_im)
    s5_dskip = s5_d.reshape(DEPTH, S5_NGB, 1, LANES)
    nfin = norm_final.reshape(1, D_MODEL)

    assert dec_len // S5_SEQ == TILE_S and n_dec + 1 <= SUBLANES
    tok = (S5_NSEQ, S5_SEQ, D_MODEL)
    ctx_rows = S5_NSEQ * S5_SEQ
    xs = (x_prompt.reshape(tok), x_sample.reshape(tok))
    ctx_fin, smp_fin, s5_finals = None, None, []
    for l in range(DEPTH):
        proj3, u_tm, x_all = _in_call(xs, norm_mix[l].reshape(1, D_MODEL), mod4, w_in, l)
        proj = proj3.reshape(ALL_SEQ * S5_SEQ, HG_IN_W)
        gain = hg_norm[l].reshape(1, HG_D)
        ohg_c, ctx_fin = _hgrn_call(proj, 0, hg_lb_logits, gain, None, ctx_fin, l, n_ctx, ctx_len)
        ohg_s, smp_fin = _hgrn_call(proj, ctx_rows, hg_lb_logits, gain, state_hgrn, smp_fin, l, n_dec, dec_len)
        y5_c, s5_fin = _s5_call(u_tm, 0, s5_a, s5_bmat, s5_cmat, s5_dskip, None, l, True)
        _, z = _s5_call(u_tm, 1, s5_a, s5_bmat, s5_cmat, s5_dskip, None, l, False)
        y5_s, _ = _s5_call(u_tm, 1, s5_a, s5_bmat, s5_cmat, s5_dskip,
                           (z, _s5_state_to_blocks(state_s5[:, l])), l, True)
        last = l == DEPTH - 1
        res = _out_call(x_all, ohg_c.reshape(S5_NSEQ, S5_SEQ, HG_W), ohg_s.reshape(S5_NSEQ, S5_SEQ, HG_W),
                        y5_c, y5_s, mod4, norm_ffn[l].reshape(1, D_MODEL), nfin,
                        s5_w_glu, w_out, w_gate_b, w_up_b, w_down_b, l, last)
        xs = res if last else (res,)
        s5_finals.append(_s5_blocks_to_state(s5_fin))
    y_prompt, y_sample = xs
    return (y_prompt.reshape(x_prompt.shape), y_sample.reshape(x_sample.shape),
            ctx_fin, jnp.stack(s5_finals, axis=1))
```

```python
import functools
import math

import jax
import jax.numpy as jnp
import numpy as np
from jax import lax
from jax.experimental import pallas as pl
from jax.experimental.pallas import tpu as pltpu

F32 = jnp.float32
BF16 = jnp.bfloat16

LANES = 128
SUBLANES = 8

D_MODEL = 1024
DEPTH = 2
GRID_W = 64
HG_W = 512
HG_HEADS = 4
HG_D = HG_W // HG_HEADS
S5_W = 512
S5_CH = 16
S5_GROUPS = S5_W // S5_CH
S5_P = 64
S5_GB = LANES // S5_CH
S5_NGB = S5_GROUPS // S5_GB
S5_SW = S5_GB * S5_P
HG_IN_W = 5 * HG_W
IN_W = HG_IN_W + S5_W
D_FF = 2816
EPS = 1e-6

HG_CHUNK = 128
HG_LEVELS = (64, 32, 16, 8, 4, 2, 1)
S5_SEQ = 256
S5_NSEQ = 16
S5_TC = 16

TILE_S = SUBLANES
TILE_T = 64
TILE_ROWS = TILE_S * TILE_T
TIME_TILES = S5_SEQ // TILE_T
ALL_SEQ = 2 * S5_NSEQ
CTX_TILES = S5_NSEQ // TILE_S
CAST_ROWS = 128
MOD_TILE_N = 1536
VMEM_LIMIT = 56 * 1024 * 1024


def _sigmoid(x):
    return 1.0 / (1.0 + jnp.exp(-x))


def _silu(x):
    return x * _sigmoid(x)


def _gelu_tanh(x):
    return 0.5 * x * (1.0 + jnp.tanh(math.sqrt(2.0 / math.pi) * (x + 0.044715 * (x * x * x))))


def _rms(x):
    return x * lax.rsqrt(jnp.mean(x * x, axis=-1, keepdims=True) + EPS)


def _dot(a, b):
    return jnp.dot(a, b, preferred_element_type=F32)


def _dot_nt(a, b):
    return lax.dot_general(a, b, (((1,), (1,)), ((), ())), preferred_element_type=F32)


def _dot_tn(a, b):
    return lax.dot_general(a, b, (((0,), (0,)), ((), ())), preferred_element_type=F32)


def _layer_spec(shape, layer):
    nd = len(shape)
    return pl.BlockSpec((None,) + tuple(shape), lambda *_: (layer,) + (0,) * nd, pipeline_mode=pl.Buffered(1))


def _mod_kernel(cond_ref, w_ref, b_ref, o_ref):
    a = _silu(cond_ref[...]).astype(BF16)
    o_ref[0] = _dot(a, w_ref[0].astype(BF16)) + b_ref[0]


def _mod_call(cond, w_mod, b_mod):
    n_cond = cond.shape[0]
    n_out = w_mod.shape[-1]
    return pl.pallas_call(
        _mod_kernel,
        grid=(DEPTH, n_out // MOD_TILE_N),
        in_specs=[
            pl.BlockSpec((n_cond, D_MODEL), lambda l, j: (0, 0)),
            pl.BlockSpec((1, D_MODEL, MOD_TILE_N), lambda l, j: (l, 0, j)),
            pl.BlockSpec((1, 1, MOD_TILE_N), lambda l, j: (l, 0, j)),
        ],
        out_specs=pl.BlockSpec((1, n_cond, MOD_TILE_N), lambda l, j: (l, 0, j)),
        out_shape=jax.ShapeDtypeStruct((DEPTH, n_cond, n_out), F32),
        compiler_params=pltpu.CompilerParams(dimension_semantics=("parallel", "parallel"),
                                             vmem_limit_bytes=VMEM_LIMIT),
        name="adaln_mod",
    )(cond, w_mod, b_mod.reshape(DEPTH, 1, n_out))


def _first_step():
    return jnp.logical_and(pl.program_id(0) == 0, pl.program_id(1) == 0)


def _cast_rows(src_ref, dst_ref):
    for r in range(0, src_ref.shape[0], CAST_ROWS):
        dst_ref[r:r + CAST_ROWS, :] = src_ref[r:r + CAST_ROWS, :].astype(BF16)


def _grid_pos_tile(omega, tb):
    nf = omega.shape[-1]
    s_idx = lax.broadcasted_iota(jnp.int32, (TILE_S, nf), 0)
    j_idx = lax.broadcasted_iota(jnp.int32, (TILE_T, nf), 0)
    t0 = tb * TILE_T
    row = (s_idx * (S5_SEQ // GRID_W) + t0 // GRID_W).astype(F32) * omega
    col = (j_idx + t0 % GRID_W).astype(F32) * omega
    enc_r = jnp.concatenate([jnp.sin(row), jnp.cos(row)], axis=-1)
    enc_c = jnp.concatenate([jnp.sin(col), jnp.cos(col)], axis=-1)
    shape = (TILE_S, TILE_T, 2 * nf)
    return jnp.concatenate([jnp.broadcast_to(enc_r[:, None, :], shape),
                            jnp.broadcast_to(enc_c[None, :, :], shape)], axis=-1)


def _is_sample_tile():
    return pl.program_id(0) >= CTX_TILES


def _in_kernel(*refs, first_layer):
    if first_layer:
        xc_ref, xs_ref, om_ref, gain_ref, sh_ref, sc_ref, w_ref, proj_ref, u_ref, xo_ref, wb_ref = refs
        x = jnp.where(_is_sample_tile(), xs_ref[...] + _grid_pos_tile(om_ref[...], pl.program_id(1)), xc_ref[...])
        xo_ref[...] = x
    else:
        x_ref, gain_ref, sh_ref, sc_ref, w_ref, proj_ref, u_ref, wb_ref = refs
        x = x_ref[...]

    @pl.when(_first_step())
    def _():
        _cast_rows(w_ref, wb_ref)

    x = x.reshape(TILE_ROWS, D_MODEL)
    h = _rms(x) * gain_ref[...]
    h = (h * (1.0 + sc_ref[...]) + sh_ref[...]).astype(BF16)
    proj_ref[...] = _dot(h, wb_ref[:, :HG_IN_W]).reshape(proj_ref.shape)
    u = _dot(h, wb_ref[:, HG_IN_W:])
    for s in range(TILE_S):
        u_ref[:, s, :] = u[s * TILE_T:(s + 1) * TILE_T, :]


def _tile_spec(width):
    return pl.BlockSpec((TILE_S, TILE_T, width), lambda sb, tb: (sb, tb, 0))


def _tm_tile_spec(width):
    return pl.BlockSpec((TILE_T, TILE_S, width), lambda sb, tb: (tb, sb, 0))


def _ctx_index(sb, tb):
    on = sb < CTX_TILES
    return jnp.where(on, sb, CTX_TILES - 1), jnp.where(on, tb, TIME_TILES - 1)


def _smp_index(sb, tb):
    on = sb >= CTX_TILES
    return jnp.where(on, sb - CTX_TILES, 0), jnp.where(on, tb, 0)


def _part_tile_spec(width, index):
    return pl.BlockSpec((TILE_S, TILE_T, width), lambda sb, tb: index(sb, tb) + (0,))


def _part_tm_tile_spec(width, index):
    return pl.BlockSpec((TILE_T, TILE_S, width), lambda sb, tb: index(sb, tb)[::-1] + (0,))


def _mod_spec(layer, col):
    return pl.BlockSpec((None, None, 1, D_MODEL),
                        lambda sb, tb: (layer, jnp.maximum(sb - (CTX_TILES - 1), 0), 0, col))


def _in_call(xs, gain, mod4, w_in, layer):
    first_layer = len(xs) == 2
    if first_layer:
        assert GRID_W % TILE_T == 0 and S5_SEQ % GRID_W == 0
        nf = D_MODEL // 4
        omega = 1.0 / (np.float32(10000.0) ** (np.arange(nf, dtype=np.float32) / np.float32(nf)))
        in_specs = [_part_tile_spec(D_MODEL, _ctx_index), _part_tile_spec(D_MODEL, _smp_index),
                    pl.BlockSpec((1, nf), lambda sb, tb: (0, 0))]
        args = list(xs) + [jnp.asarray(omega.reshape(1, nf), F32)]
    else:
        in_specs = [_tile_spec(D_MODEL)]
        args = list(xs)
    in_specs += [
        pl.BlockSpec((1, D_MODEL), lambda sb, tb: (0, 0)),
        _mod_spec(layer, 0),
        _mod_spec(layer, 1),
        _layer_spec((D_MODEL, IN_W), layer),
    ]
    args += [gain, mod4, mod4, w_in]
    out_specs = [_tile_spec(HG_IN_W), _tm_tile_spec(S5_W)]
    out_shape = [jax.ShapeDtypeStruct((ALL_SEQ, S5_SEQ, HG_IN_W), F32),
                 jax.ShapeDtypeStruct((S5_SEQ, ALL_SEQ, S5_W), F32)]
    if first_layer:
        out_specs.append(_tile_spec(D_MODEL))
        out_shape.append(jax.ShapeDtypeStruct((ALL_SEQ, S5_SEQ, D_MODEL), F32))
    res = pl.pallas_call(
        functools.partial(_in_kernel, first_layer=first_layer),
        grid=(ALL_SEQ // TILE_S, TIME_TILES),
        in_specs=in_specs,
        out_specs=out_specs,
        out_shape=out_shape,
        scratch_shapes=[pltpu.VMEM((D_MODEL, IN_W), BF16)],
        compiler_params=pltpu.CompilerParams(dimension_semantics=("arbitrary", "arbitrary"),
                                             vmem_limit_bytes=VMEM_LIMIT),
        name="in_proj",
    )(*args)
    return (res[0], res[1], res[2]) if first_layer else (res[0], res[1], xs[0])


def _pair_boundary(b, m, rev):
    c = b.shape[0]
    span = 2 * m
    at = m if rev else m - 1
    if span >= SUBLANES:
        b3 = b.reshape(c // span, span, LANES)
        return jnp.broadcast_to(b3[:, at:at + 1, :], b3.shape).reshape(c, LANES)
    b3 = b.reshape(c // SUBLANES, SUBLANES, LANES)
    sub = lax.broadcasted_iota(jnp.int32, b3.shape, 1)
    out = None
    for p in range(SUBLANES // span):
        piece = jnp.broadcast_to(b3[:, p * span + at:p * span + at + 1, :], b3.shape)
        out = piece if out is None else jnp.where(sub >= p * span, piece, out)
    return out.reshape(c, LANES)


def _neg_abs(x):
    bits = lax.bitcast_convert_type(x, jnp.uint32) | jnp.uint32(0x80000000)
    return lax.bitcast_convert_type(bits, F32)


def _hg_gates(chains, scale):
    outs = []
    for q, fl, lb, tri in chains:
        sig = _sigmoid(fl)
        forget = lb + (1.0 - lb) * sig
        logf = jnp.log2(forget)
        key = (1.0 - lb) * (1.0 - sig)
        hi = logf.astype(BF16)
        r1 = logf - hi.astype(F32)
        mid = r1.astype(BF16)
        lo = (r1 - mid.astype(F32)).astype(BF16)
        parts = _dot(tri, jnp.concatenate([hi, mid, lo], axis=1))
        b2 = parts[:, :LANES] + parts[:, LANES:2 * LANES] + parts[:, 2 * LANES:]
        outs.append((_silu(q) * scale, key, b2, forget))
    return outs


def _hg_scores(chains, code, eye):
    c = chains[0][0].shape[0]
    o_inter = []
    for qh, key, b2, forget, v, st_ref, rev in chains:
        b_edge = b2[0:1, :] if rev else b2[c - 1:c, :]
        st = st_ref[...]
        o_inter.append(_dot_nt((qh * jnp.exp2(b2)).astype(BF16), st.astype(BF16)))
        k_end = key * jnp.exp2(b_edge - b2)
        st_ref[...] = jnp.exp2(b_edge) * st + _dot_tn(v.astype(BF16), k_end.astype(BF16))

    out = []
    for o, (qh, key, b2, forget, v, st_ref, rev) in zip(o_inter, chains):
        scores = jnp.where(eye, jnp.sum(qh * key, axis=-1, keepdims=True), 0.0)
        qb, kb = qh.astype(BF16), key.astype(BF16)
        for m in HG_LEVELS:
            k = int(math.log2(m)) + 1
            if m == 1:
                p = _dot_nt((qh * forget).astype(BF16), kb)
            else:
                e = jnp.exp2(_neg_abs(b2 - _pair_boundary(b2, m, rev))).astype(BF16)
                p = _dot_nt(qb * e, kb * e)
            scores = jnp.where(code == (-k if rev else k), p, scores)
        out.append((o, scores.astype(BF16)))
    return out


def _hgrn_kernel(*refs, layer, n_chunks, n_heads, zero_init):
    refs = list(refs)
    q_ref, ff_ref, fb_ref, v_ref, g_ref, lbl_ref, gain_ref, code_ref, tri_ref = refs[:9]
    s0_ref = None if zero_init else refs[9]
    o_ref, sfin_ref, st_ref, ob_ref, ab_ref, oi_ref, sc_ref = refs[-7:]
    c = HG_CHUNK
    code = code_ref[...]
    eye = code == 0
    chains = [(hd, d) for hd in range(n_heads) for d in (0, 1)]

    def lanes(hd):
        return slice(hd * HG_D, (hd + 1) * HG_D)

    def lower_bound(hd, d):
        lg = lbl_ref[d, :, lanes(hd)]
        ex = jnp.exp(lg - jnp.max(lg, axis=0, keepdims=True))
        soft = ex / jnp.sum(ex, axis=0, keepdims=True)
        return jnp.sum(soft[:layer + 1], axis=0, keepdims=True) - soft[0:1]

    lb = [lower_bound(hd, d) for hd, d in chains]
    scale = HG_D ** -0.5

    def rows_of(d, i):
        n = (n_chunks - 1 - i) if d else i
        return pl.ds(n * c if isinstance(n, int) else pl.multiple_of(n * c, c), c)

    for ch, (hd, d) in enumerate(chains):
        st_ref[ch] = jnp.zeros((HG_D, HG_D), F32) if zero_init else s0_ref[d, hd].T

    def gates(i):
        return _hg_gates([(q_ref[rows_of(d, i), lanes(hd)], (fb_ref if d else ff_ref)[rows_of(d, i), lanes(hd)],
                           lb[ch], tri_ref[d]) for ch, (hd, d) in enumerate(chains)], scale)

    def scores(i, ab):
        return _hg_scores([ab[ch] + (v_ref[rows_of(d, i), lanes(hd)], st_ref.at[ch], bool(d))
                           for ch, (hd, d) in enumerate(chains)], code, eye)

    def emit(i, oi_sc):
        for ch, (hd, d) in enumerate(chains):
            rows = rows_of(d, i)
            (ob_ref if d else o_ref)[rows, lanes(hd)] = (
                oi_sc[ch][0] + _dot(oi_sc[ch][1], v_ref[rows, lanes(hd)].astype(BF16)))

    if n_chunks <= 2:
        ab = [gates(i) for i in range(n_chunks)]
        for i in range(n_chunks):
            emit(i, scores(i, ab[i]))
    else:
        n_ab = ab_ref.shape[1]

        def put_ab(ab):
            for ch in range(len(chains)):
                for k in range(n_ab):
                    ab_ref[ch, k] = ab[ch][k]

        def get_oi_sc():
            return [(oi_ref[ch], sc_ref[ch]) for ch in range(len(chains))]

        put_ab(gates(0))
        oi_ref[...] = jnp.zeros(oi_ref.shape, F32)
        sc_ref[...] = jnp.zeros(sc_ref.shape, BF16)

        def body(i, carry):
            emit(jnp.maximum(i - 1, 0), get_oi_sc())
            res = scores(i, [tuple(ab_ref[ch, k] for k in range(n_ab)) for ch in range(len(chains))])
            for ch in range(len(chains)):
                oi_ref[ch] = res[ch][0]
                sc_ref[ch] = res[ch][1]
            put_ab(gates(jnp.minimum(i + 1, n_chunks - 1)))
            return carry

        lax.fori_loop(0, n_chunks, body, 0)
        emit(n_chunks - 1, get_oi_sc())
    if len(sfin_ref.shape) == 5:
        for other in range(sfin_ref.shape[0]):
            if other != layer:
                sfin_ref[other] = jnp.zeros(sfin_ref.shape[1:], F32)
        sfin_ref = sfin_ref.at[layer]
    for ch, (hd, d) in enumerate(chains):
        sfin_ref[d, hd] = st_ref[ch].T

    def finish(n, carry):
        rows = pl.ds(pl.multiple_of(n * (2 * c), 2 * c), 2 * c)
        for hd in range(n_heads):
            o_ref[rows, lanes(hd)] = (_rms(o_ref[rows, lanes(hd)] + ob_ref[rows, lanes(hd)]) * gain_ref[...]
                                      * _silu(g_ref[rows, lanes(hd)]))
        return carry

    lax.fori_loop(0, n_chunks // 2, finish, 0)


def _hgrn_call(proj, row0, lb_logits, gain, state, finals, layer, n_seq, seq_len):
    rows = n_seq * seq_len
    seq0 = row0 // seq_len
    zero_init = state is None
    n_chunks = seq_len // HG_CHUNK
    assert n_chunks % 2 == 0
    nh = HG_HEADS if n_chunks <= 2 else 2
    hw = nh * HG_D
    n_hb = HG_HEADS // nh

    def col_spec(k):
        return pl.BlockSpec((seq_len, hw), lambda b, h: (seq0 + b, k * n_hb + h))

    t, s = np.meshgrid(np.arange(HG_CHUNK), np.arange(HG_CHUNK), indexing="ij")
    lvl = np.where(t == s, 0, np.floor(np.log2(np.maximum(t ^ s, 1))).astype(np.int32) + 1)
    code = jnp.asarray(np.where(t > s, lvl, -lvl), jnp.int32)
    tri = jnp.asarray(np.stack([s <= t, s >= t]), BF16)

    in_specs = [col_spec(0), col_spec(1), col_spec(2), col_spec(3), col_spec(4),
                pl.BlockSpec((2, DEPTH, hw), lambda b, h: (0, 0, h)),
                pl.BlockSpec((1, HG_D), lambda b, h: (0, 0)),
                pl.BlockSpec((HG_CHUNK, HG_CHUNK), lambda b, h: (0, 0)),
                pl.BlockSpec((2, HG_CHUNK, HG_CHUNK), lambda b, h: (0, 0, 0))]
    args = [proj] * 5 + [lb_logits, gain, code, tri]
    if not zero_init:
        in_specs.append(pl.BlockSpec((None, None, 2, nh, HG_D, HG_D), lambda b, h: (b, layer, 0, h, 0, 0)))
        args.append(state)
    aliases = {}
    if finals is None:
        fin_spec = pl.BlockSpec((None, DEPTH, 2, nh, HG_D, HG_D), lambda b, h: (b, 0, 0, h, 0, 0))
    else:
        fin_spec = pl.BlockSpec((None, None, 2, nh, HG_D, HG_D), lambda b, h: (b, layer, 0, h, 0, 0))
        aliases[len(args)] = 1
        in_specs.append(pl.BlockSpec(memory_space=pl.ANY))
        args.append(finals)
    n_ch = 2 * nh
    return pl.pallas_call(
        functools.partial(_hgrn_kernel, layer=layer, n_chunks=n_chunks, n_heads=nh, zero_init=zero_init),
        grid=(n_seq, n_hb),
        in_specs=in_specs,
        out_specs=[pl.BlockSpec((seq_len, hw), lambda b, h: (b, h)), fin_spec],
        out_shape=[jax.ShapeDtypeStruct((rows, HG_W), F32),
                   jax.ShapeDtypeStruct((n_seq, DEPTH, 2, HG_HEADS, HG_D, HG_D), F32)],
        input_output_aliases=aliases,
        scratch_shapes=[pltpu.VMEM((n_ch, HG_D, HG_D), F32), pltpu.VMEM((seq_len, hw), F32),
                        pltpu.VMEM((n_ch, 4, HG_CHUNK, HG_D), F32), pltpu.VMEM((n_ch, HG_CHUNK, HG_D), F32),
                        pltpu.VMEM((n_ch, HG_CHUNK, HG_CHUNK), BF16)],
        compiler_params=pltpu.CompilerParams(dimension_semantics=("parallel", "parallel"),
                                             vmem_limit_bytes=VMEM_LIMIT),
        name="hgrn2_mixer",
    )(*args)


def _s5_params_kernel(lr_ref, li_ref, ldt_ref, btr_ref, bti_ref, cr_ref, ci_ref, a_ref, bm_ref, cm_ref):
    for k in range(lr_ref.shape[0]):
        _s5_params_block(*(r.at[k] for r in (lr_ref, li_ref, ldt_ref, btr_ref, bti_ref, cr_ref, ci_ref,
                                             a_ref, bm_ref, cm_ref)))


def _s5_params_block(lr_ref, li_ref, ldt_ref, btr_ref, bti_ref, cr_ref, ci_ref, a_ref, bm_ref, cm_ref):
    sw = S5_SW
    lr = jnp.minimum(lr_ref[...], -1e-4)
    li = li_ref[...]
    dt = jnp.exp(ldt_ref[...])
    mag = jnp.exp(lr * dt)
    ab_re = mag * jnp.cos(li * dt)
    ab_im = mag * jnp.sin(li * dt)
    nr = ab_re - 1.0
    den = lr * lr + li * li
    z_re = (nr * lr + ab_im * li) / den
    z_im = (ab_im * lr - nr * li) / den

    p_idx = lax.broadcasted_iota(jnp.int32, (S5_P, sw), 0)
    col = lax.broadcasted_iota(jnp.int32, (S5_P, sw), 1)
    for g in range(S5_GB):
        place = (col == p_idx + g * S5_P).astype(BF16)
        zr, zi = z_re[g:g + 1, :], z_im[g:g + 1, :]
        btr, bti = btr_ref[g], bti_ref[g]
        rows = slice(g * S5_CH, (g + 1) * S5_CH)
        bm_ref[rows, :sw] = _dot((zr * btr - zi * bti).astype(BF16), place).astype(BF16)
        bm_ref[rows, sw:] = _dot((zr * bti + zi * btr).astype(BF16), place).astype(BF16)
        cm_ref[rows, :sw] = _dot(cr_ref[g].astype(BF16), place).astype(BF16)
        cm_ref[rows, sw:] = _dot((-ci_ref[g]).astype(BF16), place).astype(BF16)
        a_ref[:, g * S5_P:(g + 1) * S5_P] = jnp.broadcast_to(ab_re[g:g + 1, :], (S5_NSEQ, S5_P))
        a_ref[:, sw + g * S5_P:sw + (g + 1) * S5_P] = jnp.broadcast_to(ab_im[g:g + 1, :], (S5_NSEQ, S5_P))


def _s5_params(lam_re, lam_im, log_dt, b_re, b_im, c_re, c_im):
    nb = DEPTH * 2 * S5_NGB
    gp = (nb, S5_GB, S5_P)
    gcp = (nb, S5_GB, S5_CH, S5_P)
    bt_re = jnp.swapaxes(b_re, -1, -2).reshape(gcp)
    bt_im = jnp.swapaxes(b_im, -1, -2).reshape(gcp)
    ldt = jnp.broadcast_to(log_dt.reshape(nb, S5_GB, 1), gp)
    per = S5_NGB
    gp_spec = pl.BlockSpec((per, S5_GB, S5_P), lambda i: (i, 0, 0))
    gcp_spec = pl.BlockSpec((per, S5_GB, S5_CH, S5_P), lambda i: (i, 0, 0, 0))
    a, bmat, cmat = pl.pallas_call(
        _s5_params_kernel,
        grid=(nb // per,),
        in_specs=[gp_spec] * 3 + [gcp_spec] * 4,
        out_specs=[pl.BlockSpec((per, S5_NSEQ, 2 * S5_SW), lambda i: (i, 0, 0)),
                   pl.BlockSpec((per, LANES, 2 * S5_SW), lambda i: (i, 0, 0)),
                   pl.BlockSpec((per, LANES, 2 * S5_SW), lambda i: (i, 0, 0))],
        out_shape=[jax.ShapeDtypeStruct((nb, S5_NSEQ, 2 * S5_SW), F32),
                   jax.ShapeDtypeStruct((nb, LANES, 2 * S5_SW), BF16),
                   jax.ShapeDtypeStruct((nb, LANES, 2 * S5_SW), BF16)],
        compiler_params=pltpu.CompilerParams(dimension_semantics=("parallel",)),
        name="s5_params",
    )(lam_re.reshape(gp), lam_im.reshape(gp), ldt, bt_re, bt_im, c_re.reshape(gcp), c_im.reshape(gcp))
    lead = (DEPTH, 2, S5_NGB)
    return (a.reshape(lead + a.shape[1:]), bmat.reshape(lead + bmat.shape[1:]), cmat.reshape(lead + cmat.shape[1:]))


def _s5_kernel(*refs, want_y, zero_init):
    refs = list(refs)
    u_ref, bm_ref, cm_ref, a_ref, d_ref = refs[:5]
    rest = refs[5:]
    z_ref, s0_ref = (None, None) if zero_init else (rest.pop(0), rest.pop(0))
    y_ref = rest.pop(0) if want_y else None
    hfin_ref, hbuf0, hbuf1, hb0, hb1, hst = rest
    hbufs, hb16s = (hbuf0, hbuf1), (hb0, hb1)
    ns, sw = S5_NSEQ, S5_SW
    half = ns // 2
    n_tc = S5_SEQ // S5_TC
    blk = S5_TC * ns
    dirs = (0, 1)

    if zero_init:
        for d in dirs:
            hst[d] = jnp.zeros((ns, 2 * sw), F32)
    else:
        n_long = s0_ref.shape[1]
        pieces = ns // n_long
        for d in dirs:
            pr, pi = a_ref[d, 0:1, :sw], a_ref[d, 0:1, sw:]
            for _ in range(int(math.log2(S5_SEQ))):
                pr, pi = pr * pr - pi * pi, 2.0 * (pr * pi)
            for b in range(n_long):
                hr, hi = s0_ref[d, b:b + 1, :sw], s0_ref[d, b:b + 1, sw:]
                for k in (range(pieces - 1, -1, -1) if d else range(pieces)):
                    r = b * pieces + k
                    hst[d, r:r + 1, :sw] = hr
                    hst[d, r:r + 1, sw:] = hi
                    zr, zi = z_ref[d, r:r + 1, :sw], z_ref[d, r:r + 1, sw:]
                    hr, hi = pr * hr - pi * hi + zr, pr * hi + pi * hr + zi

    def steps_of(d, i):
        return pl.ds(((n_tc - 1 - i) if d else i) * S5_TC, S5_TC)

    def project(i, slot):
        for d in dirs:
            u = u_ref[steps_of(d, i)].reshape(blk, LANES)
            hbufs[slot][d] = _dot(u.astype(BF16), bm_ref[d])

    def scan(slot):
        a = [(a_ref[d, :half, :sw], a_ref[d, :half, sw:]) for d in dirs]
        h = [[(hst[d, k * half:(k + 1) * half, :sw], hst[d, k * half:(k + 1) * half, sw:]) for k in range(2)]
             for d in dirs]
        for jj in range(S5_TC):
            for d in dirs:
                j = S5_TC - 1 - jj if d else jj
                ar, ai = a[d]
                for k in range(2):
                    r = slice(j * ns + k * half, j * ns + (k + 1) * half)
                    hr, hi = h[d][k]
                    h[d][k] = (ar * hr - ai * hi + hbufs[slot][d, r, :sw], ar * hi + ai * hr + hbufs[slot][d, r, sw:])
                if want_y:
                    r = slice(j * ns, (j + 1) * ns)
                    hb16s[slot][d, r, :sw] = jnp.concatenate([h[d][0][0], h[d][1][0]], axis=0).astype(BF16)
                    hb16s[slot][d, r, sw:] = jnp.concatenate([h[d][0][1], h[d][1][1]], axis=0).astype(BF16)
        for d in dirs:
            for k in range(2):
                hst[d, k * half:(k + 1) * half, :sw] = h[d][k][0]
                hst[d, k * half:(k + 1) * half, sw:] = h[d][k][1]

    def readout(i, slot):
        for d in dirs:
            steps = steps_of(d, i)
            y = _dot_nt(hb16s[slot][d], cm_ref[d])
            y_ref[steps] = y_ref[steps] + y.reshape(S5_TC, ns, LANES)

    project(0, 0)
    if want_y:
        hb1[...] = jnp.zeros(hb1.shape, BF16)

        def skip(n, carry):
            steps = pl.ds(n * S5_TC, S5_TC)
            y_ref[steps] = d_ref[...] * u_ref[steps]
            return carry

        lax.fori_loop(0, n_tc, skip, 0)

    def body(k, carry):
        i = 2 * k
        scan(0)
        if want_y:
            readout(jnp.maximum(i - 1, 0), 1)
        project(i + 1, 1)
        scan(1)
        if want_y:
            readout(i, 0)
        project(jnp.minimum(i + 2, n_tc - 1), 0)
        return carry

    lax.fori_loop(0, n_tc // 2, body, 0)
    for d in dirs:
        hfin_ref[d] = hst[d]
    if want_y:
        readout(n_tc - 1, 1)


def _s5_call(u_tm, part, a, bmat, cmat, dskip, chain, layer, want_y):
    zero_init = chain is None
    tm_spec = pl.BlockSpec((S5_SEQ, S5_NSEQ, LANES), lambda g: (0, 0, g))

    def mat_spec(rows):
        return pl.BlockSpec((None, 2, None, rows, 2 * S5_SW), lambda g: (layer, 0, g, 0, 0))

    in_specs = [pl.BlockSpec((S5_SEQ, S5_NSEQ, LANES), lambda g: (0, part, g)),
                mat_spec(LANES), mat_spec(LANES), mat_spec(S5_NSEQ),
                pl.BlockSpec((None, None, 1, LANES), lambda g: (layer, g, 0, 0))]
    args = [u_tm, bmat, cmat, a, dskip]
    state_spec = pl.BlockSpec((2, None, S5_NSEQ, 2 * S5_SW), lambda g: (0, g, 0, 0))
    state_shape = jax.ShapeDtypeStruct((2, S5_NGB, S5_NSEQ, 2 * S5_SW), F32)
    if not zero_init:
        z, s0 = chain
        in_specs += [state_spec, pl.BlockSpec((2, None, s0.shape[2], 2 * S5_SW), lambda g: (0, g, 0, 0))]
        args += [z, s0]
    out_specs, out_shape = [], []
    if want_y:
        out_specs.append(tm_spec)
        out_shape.append(jax.ShapeDtypeStruct((S5_SEQ, S5_NSEQ, S5_W), F32))
    out_specs.append(state_spec)
    out_shape.append(state_shape)
    res = pl.pallas_call(
        functools.partial(_s5_kernel, want_y=want_y, zero_init=zero_init),
        grid=(S5_NGB,),
        in_specs=in_specs,
        out_specs=out_specs,
        out_shape=out_shape,
        scratch_shapes=[pltpu.VMEM((2, S5_TC * S5_NSEQ, 2 * S5_SW), F32),
                        pltpu.VMEM((2, S5_TC * S5_NSEQ, 2 * S5_SW), F32),
                        pltpu.VMEM((2, S5_TC * S5_NSEQ, 2 * S5_SW), BF16),
                        pltpu.VMEM((2, S5_TC * S5_NSEQ, 2 * S5_SW), BF16),
                        pltpu.VMEM((2, S5_NSEQ, 2 * S5_SW), F32)],
        compiler_params=pltpu.CompilerParams(dimension_semantics=("parallel",),
                                             vmem_limit_bytes=VMEM_LIMIT),
        name="s5_scan",
    )(*args)
    return (res[0], res[1]) if want_y else (None, res[0])


def _out_kernel(x_ref, ohc_ref, ohs_ref, y5c_ref, y5s_ref, g1_ref, sh2_ref, sc2_ref, g2_ref, nffn_ref, nfin_ref,
                wglu_ref, wout_ref, wg_ref, wu_ref, wd_ref, *rest, last_layer):
    if last_layer:
        oc_ref, os_ref, wglu_b, wout_b = rest
    else:
        o_ref, wglu_b, wout_b = rest

    @pl.when(_first_step())
    def _():
        _cast_rows(wglu_ref, wglu_b)
        _cast_rows(wout_ref, wout_b)

    smp = _is_sample_tile()
    y = jnp.concatenate([jnp.where(smp, y5s_ref[:, s, :], y5c_ref[:, s, :]) for s in range(TILE_S)],
                        axis=0)
    y = _gelu_tanh(y)
    y = y * _sigmoid(_dot(y.astype(BF16), wglu_b[...]))
    ohg = jnp.where(smp, ohs_ref[...], ohc_ref[...]).reshape(TILE_ROWS, HG_W)
    mix = _dot(ohg.astype(BF16), wout_b[:HG_W, :]) + _dot(y.astype(BF16), wout_b[HG_W:, :])
    x = x_ref[...].reshape(TILE_ROWS, D_MODEL) + g1_ref[...] * mix
    h = _rms(x) * nffn_ref[...]
    h = (h * (1.0 + sc2_ref[...]) + sh2_ref[...]).astype(BF16)
    act = (_silu(_dot(h, wg_ref[...])) * _dot(h, wu_ref[...])).astype(BF16)
    x = x + g2_ref[...] * _dot(act, wd_ref[...])
    if not last_layer:
        o_ref[...] = x.reshape(o_ref.shape)
    else:
        x = (_rms(x) * nfin_ref[...]).reshape(oc_ref.shape)

        @pl.when(smp)
        def _():
            os_ref[...] = x

        @pl.when(jnp.logical_not(smp))
        def _():
            oc_ref[...] = x


def _out_call(x3, ohg_c, ohg_s, y5_c, y5_s, mod4, nffn, nfin, wglu, wout, wg, wu, wd, layer, last_layer):
    vec = pl.BlockSpec((1, D_MODEL), lambda sb, tb: (0, 0))
    part_shape = jax.ShapeDtypeStruct((S5_NSEQ, S5_SEQ, D_MODEL), F32)
    if last_layer:
        out_specs = [_part_tile_spec(D_MODEL, _ctx_index), _part_tile_spec(D_MODEL, _smp_index)]
        out_shape = [part_shape, part_shape]
    else:
        out_specs = _tile_spec(D_MODEL)
        out_shape = jax.ShapeDtypeStruct(x3.shape, F32)
    return pl.pallas_call(
        functools.partial(_out_kernel, last_layer=last_layer),
        grid=(ALL_SEQ // TILE_S, TIME_TILES),
        in_specs=[_tile_spec(D_MODEL),
                  _part_tile_spec(HG_W, _ctx_index), _part_tile_spec(HG_W, _smp_index),
                  _part_tm_tile_spec(S5_W, _ctx_index), _part_tm_tile_spec(S5_W, _smp_index),
                  _mod_spec(layer, 2), _mod_spec(layer, 3), _mod_spec(layer, 4), _mod_spec(layer, 5),
                  vec, vec,
                  _layer_spec((S5_W, S5_W), layer), _layer_spec((D_MODEL, D_MODEL), layer),
                  _layer_spec((D_MODEL, D_FF), layer), _layer_spec((D_MODEL, D_FF), layer),
                  _layer_spec((D_FF, D_MODEL), layer)],
        out_specs=out_specs,
        out_shape=out_shape,
        scratch_shapes=[pltpu.VMEM((S5_W, S5_W), BF16), pltpu.VMEM((D_MODEL, D_MODEL), BF16)],
        compiler_params=pltpu.CompilerParams(dimension_semantics=("arbitrary", "arbitrary"),
                                             vmem_limit_bytes=VMEM_LIMIT),
        name="out_ffn",
    )(x3, ohg_c, ohg_s, y5_c, y5_s, mod4, mod4, mod4, mod4, nffn, nfin, wglu, wout, wg, wu, wd)


def _s5_state_to_blocks(s):
    n = s.shape[0]
    s = s.reshape(n, 2, S5_NGB, S5_GB, S5_P, 2)
    return jnp.transpose(s, (1, 2, 0, 5, 3, 4)).reshape(2, S5_NGB, n, 2 * S5_SW)


def _s5_blocks_to_state(h):
    n = h.shape[2]
    h = h.reshape(2, S5_NGB, n, 2, S5_GB, S5_P)
    return jnp.transpose(h, (2, 0, 1, 4, 5, 3)).reshape(n, 2, S5_GROUPS, S5_P, 2)


def kernel(x_prompt, x_sample, state_hgrn, state_s5, c, c_ctx, w_mod, b_mod, norm_mix, norm_ffn, norm_final, w_in, w_out, hg_lb_logits, hg_norm, s5_lam_re, s5_lam_im, s5_log_dt, s5_b_re, s5_b_im, s5_c_re, s5_c_im, s5_d, s5_w_glu, w_gate, w_up, w_down):
    n_ctx, ctx_len, _ = x_prompt.shape
    n_dec, dec_len, _ = x_sample.shape
    assert ctx_len == S5_SEQ and n_ctx == S5_NSEQ and n_dec * dec_len == S5_NSEQ * S5_SEQ

    cond = jnp.concatenate([c_ctx[None, :], c, jnp.zeros((SUBLANES - 1 - n_dec, D_MODEL), F32)], axis=0)
    mod4 = _mod_call(cond, w_mod, b_mod).reshape(DEPTH, SUBLANES, 1, 6 * D_MODEL)

    w_gate_b, w_up_b, w_down_b = w_gate.astype(BF16), w_up.astype(BF16), w_down.astype(BF16)
    s5_a, s5_bmat, s5_cmat = _s5_params(s5_lam_re, s5_lam_im, s5_log_dt, s5_b_re, s5_b_im, s5_c_re, s5_c_im)
    s5_dskip = s5_d.reshape(DEPTH, S5_NGB, 1, LANES)
    nfin = norm_final.reshape(1, D_MODEL)

    assert dec_len // S5_SEQ == TILE_S and n_dec + 1 <= SUBLANES
    tok = (S5_NSEQ, S5_SEQ, D_MODEL)
    ctx_rows = S5_NSEQ * S5_SEQ
    xs = (x_prompt.reshape(tok), x_sample.reshape(tok))
    ctx_fin, smp_fin, s5_finals = None, None, []
    for l in range(DEPTH):
        proj3, u_tm, x_all = _in_call(xs, norm_mix[l].reshape(1, D_MODEL), mod4, w_in, l)
        proj = proj3.reshape(ALL_SEQ * S5_SEQ, HG_IN_W)
        gain = hg_norm[l].reshape(1, HG_D)
        ohg_c, ctx_fin = _hgrn_call(proj, 0, hg_lb_logits, gain, None, ctx_fin, l, n_ctx, ctx_len)
        ohg_s, smp_fin = _hgrn_call(proj, ctx_rows, hg_lb_logits, gain, state_hgrn, smp_fin, l, n_dec, dec_len)
        y5_c, s5_fin = _s5_call(u_tm, 0, s5_a, s5_bmat, s5_cmat, s5_dskip, None, l, True)
        _, z = _s5_call(u_tm, 1, s5_a, s5_bmat, s5_cmat, s5_dskip, None, l, False)
        y5_s, _ = _s5_call(u_tm, 1, s5_a, s5_bmat, s5_cmat, s5_dskip,
                           (z, _s5_state_to_blocks(state_s5[:, l])), l, True)
        last = l == DEPTH - 1
        res = _out_call(x_all, ohg_c.reshape(S5_NSEQ, S5_SEQ, HG_W), ohg_s.reshape(S5_NSEQ, S5_SEQ, HG_W),
                        y5_c, y5_s, mod4, norm_ffn[l].reshape(1, D_MODEL), nfin,
                        s5_w_glu, w_out, w_gate_b, w_up_b, w_down_b, l, last)
        xs = res if last else (res,)
        s5_finals.append(_s5_blocks_to_state(s5_fin))
    y_prompt, y_sample = xs
    return (y_prompt.reshape(x_prompt.shape), y_sample.reshape(x_sample.shape),
            ctx_fin, jnp.stack(s5_finals, axis=1))
```

```python
import functools
import math

import jax
import jax.numpy as jnp
import numpy as np
from jax import lax
from jax.experimental import pallas as pl
from jax.experimental.pallas import tpu as pltpu

F32 = jnp.float32
BF16 = jnp.bfloat16

LANES = 128
SUBLANES = 8

D_MODEL = 1024
DEPTH = 2
GRID_W = 64
HG_W = 512
HG_HEADS = 4
HG_D = HG_W // HG_HEADS
S5_W = 512
S5_CH = 16
S5_GROUPS = S5_W // S5_CH
S5_P = 64
S5_GB = LANES // S5_CH
S5_NGB = S5_GROUPS // S5_GB
S5_SW = S5_GB * S5_P
HG_IN_W = 5 * HG_W
IN_W = HG_IN_W + S5_W
D_FF = 2816
EPS = 1e-6

HG_CHUNK = 128
HG_LEVELS = (64, 32, 16, 8, 4, 2, 1)
S5_SEQ = 256
S5_NSEQ = 16
S5_TC = 16

TILE_S = SUBLANES
TILE_T = 64
TILE_ROWS = TILE_S * TILE_T
TIME_TILES = S5_SEQ // TILE_T
ALL_SEQ = 2 * S5_NSEQ
CTX_TILES = S5_NSEQ // TILE_S
CAST_ROWS = 128
MOD_TILE_N = 1536
VMEM_LIMIT = 56 * 1024 * 1024


def _sigmoid(x):
    return 1.0 / (1.0 + jnp.exp(-x))


def _silu(x):
    return x * _sigmoid(x)


def _gelu_tanh(x):
    return 0.5 * x * (1.0 + jnp.tanh(math.sqrt(2.0 / math.pi) * (x + 0.044715 * (x * x * x))))


def _rms(x):
    return x * lax.rsqrt(jnp.mean(x * x, axis=-1, keepdims=True) + EPS)


def _dot(a, b):
    return jnp.dot(a, b, preferred_element_type=F32)


def _dot_nt(a, b):
    return lax.dot_general(a, b, (((1,), (1,)), ((), ())), preferred_element_type=F32)


def _dot_tn(a, b):
    return lax.dot_general(a, b, (((0,), (0,)), ((), ())), preferred_element_type=F32)


def _layer_spec(shape, layer):
    nd = len(shape)
    return pl.BlockSpec((None,) + tuple(shape), lambda *_: (layer,) + (0,) * nd, pipeline_mode=pl.Buffered(1))


def _mod_kernel(cond_ref, w_ref, b_ref, o_ref):
    a = _silu(cond_ref[...]).astype(BF16)
    o_ref[0] = _dot(a, w_ref[0].astype(BF16)) + b_ref[0]


def _mod_call(cond, w_mod, b_mod):
    n_cond = cond.shape[0]
    n_out = w_mod.shape[-1]
    return pl.pallas_call(
        _mod_kernel,
        grid=(DEPTH, n_out // MOD_TILE_N),
        in_specs=[
            pl.BlockSpec((n_cond, D_MODEL), lambda l, j: (0, 0)),
            pl.BlockSpec((1, D_MODEL, MOD_TILE_N), lambda l, j: (l, 0, j)),
            pl.BlockSpec((1, 1, MOD_TILE_N), lambda l, j: (l, 0, j)),
        ],
        out_specs=pl.BlockSpec((1, n_cond, MOD_TILE_N), lambda l, j: (l, 0, j)),
        out_shape=jax.ShapeDtypeStruct((DEPTH, n_cond, n_out), F32),
        compiler_params=pltpu.CompilerParams(dimension_semantics=("parallel", "parallel"),
                                             vmem_limit_bytes=VMEM_LIMIT),
        name="adaln_mod",
    )(cond, w_mod, b_mod.reshape(DEPTH, 1, n_out))


def _first_step():
    return jnp.logical_and(pl.program_id(0) == 0, pl.program_id(1) == 0)


def _cast_rows(src_ref, dst_ref):
    for r in range(0, src_ref.shape[0], CAST_ROWS):
        dst_ref[r:r + CAST_ROWS, :] = src_ref[r:r + CAST_ROWS, :].astype(BF16)


def _grid_pos_tile(omega, tb):
    nf = omega.shape[-1]
    s_idx = lax.broadcasted_iota(jnp.int32, (TILE_S, nf), 0)
    j_idx = lax.broadcasted_iota(jnp.int32, (TILE_T, nf), 0)
    t0 = tb * TILE_T
    row = (s_idx * (S5_SEQ // GRID_W) + t0 // GRID_W).astype(F32) * omega
    col = (j_idx + t0 % GRID_W).astype(F32) * omega
    enc_r = jnp.concatenate([jnp.sin(row), jnp.cos(row)], axis=-1)
    enc_c = jnp.concatenate([jnp.sin(col), jnp.cos(col)], axis=-1)
    shape = (TILE_S, TILE_T, 2 * nf)
    return jnp.concatenate([jnp.broadcast_to(enc_r[:, None, :], shape),
                            jnp.broadcast_to(enc_c[None, :, :], shape)], axis=-1)


def _is_sample_tile():
    return pl.program_id(0) >= CTX_TILES


def _in_kernel(*refs, first_layer):
    if first_layer:
        xc_ref, xs_ref, om_ref, gain_ref, sh_ref, sc_ref, w_ref, proj_ref, u_ref, xo_ref, wb_ref = refs
        x = jnp.where(_is_sample_tile(), xs_ref[...] + _grid_pos_tile(om_ref[...], pl.program_id(1)), xc_ref[...])
        xo_ref[...] = x
    else:
        x_ref, gain_ref, sh_ref, sc_ref, w_ref, proj_ref, u_ref, wb_ref = refs
        x = x_ref[...]

    @pl.when(_first_step())
    def _():
        _cast_rows(w_ref, wb_ref)

    x = x.reshape(TILE_ROWS, D_MODEL)
    h = _rms(x) * gain_ref[...]
    h = (h * (1.0 + sc_ref[...]) + sh_ref[...]).astype(BF16)
    proj_ref[...] = _dot(h, wb_ref[:, :HG_IN_W]).reshape(proj_ref.shape)
    u = _dot(h, wb_ref[:, HG_IN_W:])
    for s in range(TILE_S):
        u_ref[:, s, :] = u[s * TILE_T:(s + 1) * TILE_T, :]


def _tile_spec(width):
    return pl.BlockSpec((TILE_S, TILE_T, width), lambda sb, tb: (sb, tb, 0))


def _tm_tile_spec(width):
    return pl.BlockSpec((TILE_T, TILE_S, width), lambda sb, tb: (tb, sb, 0))


def _ctx_index(sb, tb):
    on = sb < CTX_TILES
    return jnp.where(on, sb, CTX_TILES - 1), jnp.where(on, tb, TIME_TILES - 1)


def _smp_index(sb, tb):
    on = sb >= CTX_TILES
    return jnp.where(on, sb - CTX_TILES, 0), jnp.where(on, tb, 0)


def _part_tile_spec(width, index):
    return pl.BlockSpec((TILE_S, TILE_T, width), lambda sb, tb: index(sb, tb) + (0,))


def _part_tm_tile_spec(width, index):
    return pl.BlockSpec((TILE_T, TILE_S, width), lambda sb, tb: index(sb, tb)[::-1] + (0,))


def _mod_spec(layer, col):
    return pl.BlockSpec((None, None, 1, D_MODEL),
                        lambda sb, tb: (layer, jnp.maximum(sb - (CTX_TILES - 1), 0), 0, col))


def _in_call(xs, gain, mod4, w_in, layer):
    first_layer = len(xs) == 2
    if first_layer:
        assert GRID_W % TILE_T == 0 and S5_SEQ % GRID_W == 0
        nf = D_MODEL // 4
        omega = 1.0 / (np.float32(10000.0) ** (np.arange(nf, dtype=np.float32) / np.float32(nf)))
        in_specs = [_part_tile_spec(D_MODEL, _ctx_index), _part_tile_spec(D_MODEL, _smp_index),
                    pl.BlockSpec((1, nf), lambda sb, tb: (0, 0))]
        args = list(xs) + [jnp.asarray(omega.reshape(1, nf), F32)]
    else:
        in_specs = [_tile_spec(D_MODEL)]
        args = list(xs)
    in_specs += [
        pl.BlockSpec((1, D_MODEL), lambda sb, tb: (0, 0)),
        _mod_spec(layer, 0),
        _mod_spec(layer, 1),
        _layer_spec((D_MODEL, IN_W), layer),
    ]
    args += [gain, mod4, mod4, w_in]
    out_specs = [_tile_spec(HG_IN_W), _tm_tile_spec(S5_W)]
    out_shape = [jax.ShapeDtypeStruct((ALL_SEQ, S5_SEQ, HG_IN_W), F32),
                 jax.ShapeDtypeStruct((S5_SEQ, ALL_SEQ, S5_W), F32)]
    if first_layer:
        out_specs.append(_tile_spec(D_MODEL))
        out_shape.append(jax.ShapeDtypeStruct((ALL_SEQ, S5_SEQ, D_MODEL), F32))
    res = pl.pallas_call(
        functools.partial(_in_kernel, first_layer=first_layer),
        grid=(ALL_SEQ // TILE_S, TIME_TILES),
        in_specs=in_specs,
        out_specs=out_specs,
        out_shape=out_shape,
        scratch_shapes=[pltpu.VMEM((D_MODEL, IN_W), BF16)],
        compiler_params=pltpu.CompilerParams(dimension_semantics=("arbitrary", "arbitrary"),
                                             vmem_limit_bytes=VMEM_LIMIT),
        name="in_proj",
    )(*args)
    return (res[0], res[1], res[2]) if first_layer else (res[0], res[1], xs[0])


def _pair_boundary(b, m, rev):
    c = b.shape[0]
    span = 2 * m
    at = m if rev else m - 1
    if span >= SUBLANES:
        b3 = b.reshape(c // span, span, LANES)
        return jnp.broadcast_to(b3[:, at:at + 1, :], b3.shape).reshape(c, LANES)
    b3 = b.reshape(c // SUBLANES, SUBLANES, LANES)
    sub = lax.broadcasted_iota(jnp.int32, b3.shape, 1)
    out = None
    for p in range(SUBLANES // span):
        piece = jnp.broadcast_to(b3[:, p * span + at:p * span + at + 1, :], b3.shape)
        out = piece if out is None else jnp.where(sub >= p * span, piece, out)
    return out.reshape(c, LANES)


def _neg_abs(x):
    bits = lax.bitcast_convert_type(x, jnp.uint32) | jnp.uint32(0x80000000)
    return lax.bitcast_convert_type(bits, F32)


def _hg_gates(chains, scale):
    outs = []
    for q, fl, lb, tri in chains:
        sig = _sigmoid(fl)
        forget = lb + (1.0 - lb) * sig
        logf = jnp.log2(forget)
        key = (1.0 - lb) * (1.0 - sig)
        hi = logf.astype(BF16)
        r1 = logf - hi.astype(F32)
        mid = r1.astype(BF16)
        lo = (r1 - mid.astype(F32)).astype(BF16)
        parts = _dot(tri, jnp.concatenate([hi, mid, lo], axis=1))
        b2 = parts[:, :LANES] + parts[:, LANES:2 * LANES] + parts[:, 2 * LANES:]
        outs.append((_silu(q) * scale, key, b2, forget))
    return outs


def _hg_scores(chains, code, eye):
    c = chains[0][0].shape[0]
    o_inter = []
    for qh, key, b2, forget, v, st_ref, rev in chains:
        b_edge = b2[0:1, :] if rev else b2[c - 1:c, :]
        st = st_ref[...]
        o_inter.append(_dot_nt((qh * jnp.exp2(b2)).astype(BF16), st.astype(BF16)))
        k_end = key * jnp.exp2(b_edge - b2)
        st_ref[...] = jnp.exp2(b_edge) * st + _dot_tn(v.astype(BF16), k_end.astype(BF16))

    out = []
    for o, (qh, key, b2, forget, v, st_ref, rev) in zip(o_inter, chains):
        scores = jnp.where(eye, jnp.sum(qh * key, axis=-1, keepdims=True), 0.0)
        qb, kb = qh.astype(BF16), key.astype(BF16)
        for m in HG_LEVELS:
            k = int(math.log2(m)) + 1
            if m == 1:
                p = _dot_nt((qh * forget).astype(BF16), kb)
            else:
                e = jnp.exp2(_neg_abs(b2 - _pair_boundary(b2, m, rev))).astype(BF16)
                p = _dot_nt(qb * e, kb * e)
            scores = jnp.where(code == (-k if rev else k), p, scores)
        out.append((o, scores.astype(BF16)))
    return out


def _hgrn_kernel(*refs, layer, n_chunks, n_heads, zero_init):
    refs = list(refs)
    q_ref, ff_ref, fb_ref, v_ref, g_ref, lbl_ref, gain_ref, code_ref, tri_ref = refs[:9]
    s0_ref = None if zero_init else refs[9]
    o_ref, sfin_ref, st_ref, ob_ref, ab_ref, oi_ref, sc_ref = refs[-7:]
    c = HG_CHUNK
    code = code_ref[...]
    eye = code == 0
    chains = [(hd, d) for hd in range(n_heads) for d in (0, 1)]

    def lanes(hd):
        return slice(hd * HG_D, (hd + 1) * HG_D)

    def lower_bound(hd, d):
        lg = lbl_ref[d, :, lanes(hd)]
        ex = jnp.exp(lg - jnp.max(lg, axis=0, keepdims=True))
        soft = ex / jnp.sum(ex, axis=0, keepdims=True)
        return jnp.sum(soft[:layer + 1], axis=0, keepdims=True) - soft[0:1]

    lb = [lower_bound(hd, d) for hd, d in chains]
    scale = HG_D ** -0.5

    def rows_of(d, i):
        n = (n_chunks - 1 - i) if d else i
        return pl.ds(n * c if isinstance(n, int) else pl.multiple_of(n * c, c), c)

    for ch, (hd, d) in enumerate(chains):
        st_ref[ch] = jnp.zeros((HG_D, HG_D), F32) if zero_init else s0_ref[d, hd].T

    def gates(i):
        return _hg_gates([(q_ref[rows_of(d, i), lanes(hd)], (fb_ref if d else ff_ref)[rows_of(d, i), lanes(hd)],
                           lb[ch], tri_ref[d]) for ch, (hd, d) in enumerate(chains)], scale)

    def scores(i, ab):
        return _hg_scores([ab[ch] + (v_ref[rows_of(d, i), lanes(hd)], st_ref.at[ch], bool(d))
                           for ch, (hd, d) in enumerate(chains)], code, eye)

    def emit(i, oi_sc):
        for ch, (hd, d) in enumerate(chains):
            rows = rows_of(d, i)
            (ob_ref if d else o_ref)[rows, lanes(hd)] = (
                oi_sc[ch][0] + _dot(oi_sc[ch][1], v_ref[rows, lanes(hd)].astype(BF16)))

    if n_chunks <= 2:
        ab = [gates(i) for i in range(n_chunks)]
        for i in range(n_chunks):
            emit(i, scores(i, ab[i]))
    else:
        n_ab = ab_ref.shape[1]

        def put_ab(ab):
            for ch in range(len(chains)):
                for k in range(n_ab):
                    ab_ref[ch, k] = ab[ch][k]

        def get_oi_sc():
            return [(oi_ref[ch], sc_ref[ch]) for ch in range(len(chains))]

        put_ab(gates(0))
        oi_ref[...] = jnp.zeros(oi_ref.shape, F32)
        sc_ref[...] = jnp.zeros(sc_ref.shape, BF16)

        def body(i, carry):
            emit(jnp.maximum(i - 1, 0), get_oi_sc())
            res = scores(i, [tuple(ab_ref[ch, k] for k in range(n_ab)) for ch in range(len(chains))])
            for ch in range(len(chains)):
                oi_ref[ch] = res[ch][0]
                sc_ref[ch] = res[ch][1]
            put_ab(gates(jnp.minimum(i + 1, n_chunks - 1)))
            return carry

        lax.fori_loop(0, n_chunks, body, 0)
        emit(n_chunks - 1, get_oi_sc())
    if len(sfin_ref.shape) == 5:
        for other in range(sfin_ref.shape[0]):
            if other != layer:
                sfin_ref[other] = jnp.zeros(sfin_ref.shape[1:], F32)
        sfin_ref = sfin_ref.at[layer]
    for ch, (hd, d) in enumerate(chains):
        sfin_ref[d, hd] = st_ref[ch].T

    def finish(n, carry):
        rows = pl.ds(pl.multiple_of(n * (2 * c), 2 * c), 2 * c)
        for hd in range(n_heads):
            o_ref[rows, lanes(hd)] = (_rms(o_ref[rows, lanes(hd)] + ob_ref[rows, lanes(hd)]) * gain_ref[...]
                                      * _silu(g_ref[rows, lanes(hd)]))
        return carry

    lax.fori_loop(0, n_chunks // 2, finish, 0)


def _hgrn_call(proj, row0, lb_logits, gain, state, finals, layer, n_seq, seq_len):
    rows = n_seq * seq_len
    seq0 = row0 // seq_len
    zero_init = state is None
    n_chunks = seq_len // HG_CHUNK
    assert n_chunks % 2 == 0
    nh = HG_HEADS if n_chunks <= 2 else 2
    hw = nh * HG_D
    n_hb = HG_HEADS // nh

    def col_spec(k):
        return pl.BlockSpec((seq_len, hw), lambda b, h: (seq0 + b, k * n_hb + h))

    t, s = np.meshgrid(np.arange(HG_CHUNK), np.arange(HG_CHUNK), indexing="ij")
    lvl = np.where(t == s, 0, np.floor(np.log2(np.maximum(t ^ s, 1))).astype(np.int32) + 1)
    code = jnp.asarray(np.where(t > s, lvl, -lvl), jnp.int32)
    tri = jnp.asarray(np.stack([s <= t, s >= t]), BF16)

    in_specs = [col_spec(0), col_spec(1), col_spec(2), col_spec(3), col_spec(4),
                pl.BlockSpec((2, DEPTH, hw), lambda b, h: (0, 0, h)),
                pl.BlockSpec((1, HG_D), lambda b, h: (0, 0)),
                pl.BlockSpec((HG_CHUNK, HG_CHUNK), lambda b, h: (0, 0)),
                pl.BlockSpec((2, HG_CHUNK, HG_CHUNK), lambda b, h: (0, 0, 0))]
    args = [proj] * 5 + [lb_logits, gain, code, tri]
    if not zero_init:
        in_specs.append(pl.BlockSpec((None, None, 2, nh, HG_D, HG_D), lambda b, h: (b, layer, 0, h, 0, 0)))
        args.append(state)
    aliases = {}
    if finals is None:
        fin_spec = pl.BlockSpec((None, DEPTH, 2, nh, HG_D, HG_D), lambda b, h: (b, 0, 0, h, 0, 0))
    else:
        fin_spec = pl.BlockSpec((None, None, 2, nh, HG_D, HG_D), lambda b, h: (b, layer, 0, h, 0, 0))
        aliases[len(args)] = 1
        in_specs.append(pl.BlockSpec(memory_space=pl.ANY))
        args.append(finals)
    n_ch = 2 * nh
    return pl.pallas_call(
        functools.partial(_hgrn_kernel, layer=layer, n_chunks=n_chunks, n_heads=nh, zero_init=zero_init),
        grid=(n_seq, n_hb),
        in_specs=in_specs,
        out_specs=[pl.BlockSpec((seq_len, hw), lambda b, h: (b, h)), fin_spec],
        out_shape=[jax.ShapeDtypeStruct((rows, HG_W), F32),
                   jax.ShapeDtypeStruct((n_seq, DEPTH, 2, HG_HEADS, HG_D, HG_D), F32)],
        input_output_aliases=aliases,
        scratch_shapes=[pltpu.VMEM((n_ch, HG_D, HG_D), F32), pltpu.VMEM((seq_len, hw), F32),
                        pltpu.VMEM((n_ch, 4, HG_CHUNK, HG_D), F32), pltpu.VMEM((n_ch, HG_CHUNK, HG_D), F32),
                        pltpu.VMEM((n_ch, HG_CHUNK, HG_CHUNK), BF16)],
        compiler_params=pltpu.CompilerParams(dimension_semantics=("parallel", "parallel"),
                                             vmem_limit_bytes=VMEM_LIMIT),
        name="hgrn2_mixer",
    )(*args)


def _s5_params_kernel(lr_ref, li_ref, ldt_ref, btr_ref, bti_ref, cr_ref, ci_ref, a_ref, bm_ref, cm_ref):
    for k in range(lr_ref.shape[0]):
        _s5_params_block(*(r.at[k] for r in (lr_ref, li_ref, ldt_ref, btr_ref, bti_ref, cr_ref, ci_ref,
                                             a_ref, bm_ref, cm_ref)))


def _s5_params_block(lr_ref, li_ref, ldt_ref, btr_ref, bti_ref, cr_ref, ci_ref, a_ref, bm_ref, cm_ref):
    sw = S5_SW
    lr = jnp.minimum(lr_ref[...], -1e-4)
    li = li_ref[...]
    dt = jnp.exp(ldt_ref[...])
    mag = jnp.exp(lr * dt)
    ab_re = mag * jnp.cos(li * dt)
    ab_im = mag * jnp.sin(li * dt)
    nr = ab_re - 1.0
    den = lr * lr + li * li
    z_re = (nr * lr + ab_im * li) / den
    z_im = (ab_im * lr - nr * li) / den

    p_idx = lax.broadcasted_iota(jnp.int32, (S5_P, sw), 0)
    col = lax.broadcasted_iota(jnp.int32, (S5_P, sw), 1)
    for g in range(S5_GB):
        place = (col == p_idx + g * S5_P).astype(BF16)
        zr, zi = z_re[g:g + 1, :], z_im[g:g + 1, :]
        btr, bti = btr_ref[g], bti_ref[g]
        rows = slice(g * S5_CH, (g + 1) * S5_CH)
        bm_ref[rows, :sw] = _dot((zr * btr - zi * bti).astype(BF16), place).astype(BF16)
        bm_ref[rows, sw:] = _dot((zr * bti + zi * btr).astype(BF16), place).astype(BF16)
        cm_ref[rows, :sw] = _dot(cr_ref[g].astype(BF16), place).astype(BF16)
        cm_ref[rows, sw:] = _dot((-ci_ref[g]).astype(BF16), place).astype(BF16)
        a_ref[:, g * S5_P:(g + 1) * S5_P] = jnp.broadcast_to(ab_re[g:g + 1, :], (S5_NSEQ, S5_P))
        a_ref[:, sw + g * S5_P:sw + (g + 1) * S5_P] = jnp.broadcast_to(ab_im[g:g + 1, :], (S5_NSEQ, S5_P))


def _s5_params(lam_re, lam_im, log_dt, b_re, b_im, c_re, c_im):
    nb = DEPTH * 2 * S5_NGB
    gp = (nb, S5_GB, S5_P)
    gcp = (nb, S5_GB, S5_CH, S5_P)
    bt_re = jnp.swapaxes(b_re, -1, -2).reshape(gcp)
    bt_im = jnp.swapaxes(b_im, -1, -2).reshape(gcp)
    ldt = jnp.broadcast_to(log_dt.reshape(nb, S5_GB, 1), gp)
    per = S5_NGB
    gp_spec = pl.BlockSpec((per, S5_GB, S5_P), lambda i: (i, 0, 0))
    gcp_spec = pl.BlockSpec((per, S5_GB, S5_CH, S5_P), lambda i: (i, 0, 0, 0))
    a, bmat, cmat = pl.pallas_call(
        _s5_params_kernel,
        grid=(nb // per,),
        in_specs=[gp_spec] * 3 + [gcp_spec] * 4,
        out_specs=[pl.BlockSpec((per, S5_NSEQ, 2 * S5_SW), lambda i: (i, 0, 0)),
                   pl.BlockSpec((per, LANES, 2 * S5_SW), lambda i: (i, 0, 0)),
                   pl.BlockSpec((per, LANES, 2 * S5_SW), lambda i: (i, 0, 0))],
        out_shape=[jax.ShapeDtypeStruct((nb, S5_NSEQ, 2 * S5_SW), F32),
                   jax.ShapeDtypeStruct((nb, LANES, 2 * S5_SW), BF16),
                   jax.ShapeDtypeStruct((nb, LANES, 2 * S5_SW), BF16)],
        compiler_params=pltpu.CompilerParams(dimension_semantics=("parallel",)),
        name="s5_params",
    )(lam_re.reshape(gp), lam_im.reshape(gp), ldt, bt_re, bt_im, c_re.reshape(gcp), c_im.reshape(gcp))
    lead = (DEPTH, 2, S5_NGB)
    return (a.reshape(lead + a.shape[1:]), bmat.reshape(lead + bmat.shape[1:]), cmat.reshape(lead + cmat.shape[1:]))


def _s5_kernel(*refs, chained):
    refs = list(refs)
    u_ref, bm_ref, cm_ref, a_ref, d_ref = refs[:5]
    rest = refs[5:]
    s0_ref = rest.pop(0) if chained else None
    y_ref = rest.pop(0)
    hfin_ref = None if chained else rest.pop(0)
    hbuf_e, hbuf_o, hb0, hb1, hst = rest
    hbufs, hb16s = (hbuf_e, hbuf_o), (hb0, hb1)
    ns, sw = S5_NSEQ, S5_SW
    half = ns // 2
    n_tc = S5_SEQ // S5_TC
    blk = S5_TC * ns
    dirs = (0, 1)

    for d in dirs:
        hst[d] = jnp.zeros((ns, 2 * sw), F32)

    def steps_of(d, i):
        return pl.ds(((n_tc - 1 - i) if d else i) * S5_TC, S5_TC)

    def kept(i):
        return i // 2 if chained else 0

    def project(i, par):
        for d in dirs:
            u = u_ref[steps_of(d, i)].reshape(blk, LANES)
            hbufs[par][d, kept(i)] = _dot(u.astype(BF16), bm_ref[d])

    def scan(i, par, emit):
        a = [(a_ref[d, :half, :sw], a_ref[d, :half, sw:]) for d in dirs]
        h = [[(hst[d, k * half:(k + 1) * half, :sw], hst[d, k * half:(k + 1) * half, sw:]) for k in range(2)]
             for d in dirs]
        for jj in range(S5_TC):
            for d in dirs:
                j = S5_TC - 1 - jj if d else jj
                ar, ai = a[d]
                bu = hbufs[par].at[d, kept(i)]
                for k in range(2):
                    r = slice(j * ns + k * half, j * ns + (k + 1) * half)
                    hr, hi = h[d][k]
                    h[d][k] = (ar * hr - ai * hi + bu[r, :sw], ar * hi + ai * hr + bu[r, sw:])
                if emit:
                    r = slice(j * ns, (j + 1) * ns)
                    hb16s[par][d, r, :sw] = jnp.concatenate([h[d][0][0], h[d][1][0]], axis=0).astype(BF16)
                    hb16s[par][d, r, sw:] = jnp.concatenate([h[d][0][1], h[d][1][1]], axis=0).astype(BF16)
        for d in dirs:
            for k in range(2):
                hst[d, k * half:(k + 1) * half, :sw] = h[d][k][0]
                hst[d, k * half:(k + 1) * half, sw:] = h[d][k][1]

    def readout(i, par):
        for d in dirs:
            steps = steps_of(d, i)
            y = _dot_nt(hb16s[par][d], cm_ref[d])
            y_ref[steps] = y_ref[steps] + y.reshape(S5_TC, ns, LANES)

    def skip(n, carry):
        steps = pl.ds(n * S5_TC, S5_TC)
        y_ref[steps] = d_ref[...] * u_ref[steps]
        return carry

    project(0, 0)
    if chained:
        def sweep(k, carry):
            i = 2 * k
            scan(i, 0, False)
            project(i + 1, 1)
            scan(i + 1, 1, False)
            project(i + 2, 0)
            return carry

        lax.fori_loop(0, n_tc // 2 - 1, sweep, 0)
        scan(n_tc - 2, 0, False)
        project(n_tc - 1, 1)
        scan(n_tc - 1, 1, False)

        n_long = s0_ref.shape[1]
        pieces = ns // n_long
        for d in dirs:
            pr, pi = a_ref[d, 0:1, :sw], a_ref[d, 0:1, sw:]
            for _ in range(int(math.log2(S5_SEQ))):
                pr, pi = pr * pr - pi * pi, 2.0 * (pr * pi)
            for b in range(n_long):
                hr, hi = s0_ref[d, b:b + 1, :sw], s0_ref[d, b:b + 1, sw:]
                for k in (range(pieces - 1, -1, -1) if d else range(pieces)):
                    r = b * pieces + k
                    zr, zi = hst[d, r:r + 1, :sw], hst[d, r:r + 1, sw:]
                    hst[d, r:r + 1, :sw] = hr
                    hst[d, r:r + 1, sw:] = hi
                    hr, hi = pr * hr - pi * hi + zr, pr * hi + pi * hr + zi

    hb1[...] = jnp.zeros(hb1.shape, BF16)
    lax.fori_loop(0, n_tc, skip, 0)

    def body(k, carry):
        i = 2 * k
        scan(i, 0, True)
        readout(jnp.maximum(i - 1, 0), 1)
        if not chained:
            project(i + 1, 1)
        scan(i + 1, 1, True)
        readout(i, 0)
        if not chained:
            project(jnp.minimum(i + 2, n_tc - 1), 0)
        return carry

    lax.fori_loop(0, n_tc // 2, body, 0)
    if not chained:
        for d in dirs:
            hfin_ref[d] = hst[d]
    readout(n_tc - 1, 1)


def _s5_call(u_tm, part, a, bmat, cmat, dskip, s0, layer):
    chained = s0 is not None
    tm_spec = pl.BlockSpec((S5_SEQ, S5_NSEQ, LANES), lambda g: (0, 0, g))

    def mat_spec(rows):
        return pl.BlockSpec((None, 2, None, rows, 2 * S5_SW), lambda g: (layer, 0, g, 0, 0))

    in_specs = [pl.BlockSpec((S5_SEQ, S5_NSEQ, LANES), lambda g: (0, part, g)),
                mat_spec(LANES), mat_spec(LANES), mat_spec(S5_NSEQ),
                pl.BlockSpec((None, None, 1, LANES), lambda g: (layer, g, 0, 0))]
    args = [u_tm, bmat, cmat, a, dskip]
    out_specs = [tm_spec]
    out_shape = [jax.ShapeDtypeStruct((S5_SEQ, S5_NSEQ, S5_W), F32)]
    if chained:
        in_specs.append(pl.BlockSpec((2, None, s0.shape[2], 2 * S5_SW), lambda g: (0, g, 0, 0)))
        args.append(s0)
    else:
        out_specs.append(pl.BlockSpec((2, None, S5_NSEQ, 2 * S5_SW), lambda g: (0, g, 0, 0)))
        out_shape.append(jax.ShapeDtypeStruct((2, S5_NGB, S5_NSEQ, 2 * S5_SW), F32))
    blk = S5_TC * S5_NSEQ
    kept = S5_SEQ // S5_TC // 2 if chained else 1
    res = pl.pallas_call(
        functools.partial(_s5_kernel, chained=chained),
        grid=(S5_NGB,),
        in_specs=in_specs,
        out_specs=out_specs,
        out_shape=out_shape,
        scratch_shapes=[pltpu.VMEM((2, kept, blk, 2 * S5_SW), F32),
                        pltpu.VMEM((2, kept, blk, 2 * S5_SW), F32),
                        pltpu.VMEM((2, blk, 2 * S5_SW), BF16),
                        pltpu.VMEM((2, blk, 2 * S5_SW), BF16),
                        pltpu.VMEM((2, S5_NSEQ, 2 * S5_SW), F32)],
        compiler_params=pltpu.CompilerParams(dimension_semantics=("parallel",),
                                             vmem_limit_bytes=VMEM_LIMIT),
        name="s5_scan",
    )(*args)
    return (res[0], None) if chained else (res[0], res[1])


def _out_kernel(x_ref, ohc_ref, ohs_ref, y5c_ref, y5s_ref, g1_ref, sh2_ref, sc2_ref, g2_ref, nffn_ref, nfin_ref,
                wglu_ref, wout_ref, wg_ref, wu_ref, wd_ref, *rest, last_layer):
    if last_layer:
        oc_ref, os_ref, wglu_b, wout_b = rest
    else:
        o_ref, wglu_b, wout_b = rest

    @pl.when(_first_step())
    def _():
        _cast_rows(wglu_ref, wglu_b)
        _cast_rows(wout_ref, wout_b)

    smp = _is_sample_tile()
    y = jnp.concatenate([jnp.where(smp, y5s_ref[:, s, :], y5c_ref[:, s, :]) for s in range(TILE_S)],
                        axis=0)
    y = _gelu_tanh(y)
    y = y * _sigmoid(_dot(y.astype(BF16), wglu_b[...]))
    ohg = jnp.where(smp, ohs_ref[...], ohc_ref[...]).reshape(TILE_ROWS, HG_W)
    mix = _dot(ohg.astype(BF16), wout_b[:HG_W, :]) + _dot(y.astype(BF16), wout_b[HG_W:, :])
    x = x_ref[...].reshape(TILE_ROWS, D_MODEL) + g1_ref[...] * mix
    h = _rms(x) * nffn_ref[...]
    h = (h * (1.0 + sc2_ref[...]) + sh2_ref[...]).astype(BF16)
    act = (_silu(_dot(h, wg_ref[...])) * _dot(h, wu_ref[...])).astype(BF16)
    x = x + g2_ref[...] * _dot(act, wd_ref[...])
    if not last_layer:
        o_ref[...] = x.reshape(o_ref.shape)
    else:
        x = (_rms(x) * nfin_ref[...]).reshape(oc_ref.shape)

        @pl.when(smp)
        def _():
            os_ref[...] = x

        @pl.when(jnp.logical_not(smp))
        def _():
            oc_ref[...] = x


def _out_call(x3, ohg_c, ohg_s, y5_c, y5_s, mod4, nffn, nfin, wglu, wout, wg, wu, wd, layer, last_layer):
    vec = pl.BlockSpec((1, D_MODEL), lambda sb, tb: (0, 0))
    part_shape = jax.ShapeDtypeStruct((S5_NSEQ, S5_SEQ, D_MODEL), F32)
    if last_layer:
        out_specs = [_part_tile_spec(D_MODEL, _ctx_index), _part_tile_spec(D_MODEL, _smp_index)]
        out_shape = [part_shape, part_shape]
    else:
        out_specs = _tile_spec(D_MODEL)
        out_shape = jax.ShapeDtypeStruct(x3.shape, F32)
    return pl.pallas_call(
        functools.partial(_out_kernel, last_layer=last_layer),
        grid=(ALL_SEQ // TILE_S, TIME_TILES),
        in_specs=[_tile_spec(D_MODEL),
                  _part_tile_spec(HG_W, _ctx_index), _part_tile_spec(HG_W, _smp_index),
                  _part_tm_tile_spec(S5_W, _ctx_index), _part_tm_tile_spec(S5_W, _smp_index),
                  _mod_spec(layer, 2), _mod_spec(layer, 3), _mod_spec(layer, 4), _mod_spec(layer, 5),
                  vec, vec,
                  _layer_spec((S5_W, S5_W), layer), _layer_spec((D_MODEL, D_MODEL), layer),
                  _layer_spec((D_MODEL, D_FF), layer), _layer_spec((D_MODEL, D_FF), layer),
                  _layer_spec((D_FF, D_MODEL), layer)],
        out_specs=out_specs,
        out_shape=out_shape,
        scratch_shapes=[pltpu.VMEM((S5_W, S5_W), BF16), pltpu.VMEM((D_MODEL, D_MODEL), BF16)],
        compiler_params=pltpu.CompilerParams(dimension_semantics=("arbitrary", "arbitrary"),
                                             vmem_limit_bytes=VMEM_LIMIT),
        name="out_ffn",
    )(x3, ohg_c, ohg_s, y5_c, y5_s, mod4, mod4, mod4, mod4, nffn, nfin, wglu, wout, wg, wu, wd)


def _s5_state_to_blocks(s):
    n = s.shape[0]
    s = s.reshape(n, 2, S5_NGB, S5_GB, S5_P, 2)
    return jnp.transpose(s, (1, 2, 0, 5, 3, 4)).reshape(2, S5_NGB, n, 2 * S5_SW)


def _s5_blocks_to_state(h):
    n = h.shape[2]
    h = h.reshape(2, S5_NGB, n, 2, S5_GB, S5_P)
    return jnp.transpose(h, (2, 0, 1, 4, 5, 3)).reshape(n, 2, S5_GROUPS, S5_P, 2)


def kernel(x_prompt, x_sample, state_hgrn, state_s5, c, c_ctx, w_mod, b_mod, norm_mix, norm_ffn, norm_final, w_in, w_out, hg_lb_logits, hg_norm, s5_lam_re, s5_lam_im, s5_log_dt, s5_b_re, s5_b_im, s5_c_re, s5_c_im, s5_d, s5_w_glu, w_gate, w_up, w_down):
    n_ctx, ctx_len, _ = x_prompt.shape
    n_dec, dec_len, _ = x_sample.shape
    assert ctx_len == S5_SEQ and n_ctx == S5_NSEQ and n_dec * dec_len == S5_NSEQ * S5_SEQ

    cond = jnp.concatenate([c_ctx[None, :], c, jnp.zeros((SUBLANES - 1 - n_dec, D_MODEL), F32)], axis=0)
    mod4 = _mod_call(cond, w_mod, b_mod).reshape(DEPTH, SUBLANES, 1, 6 * D_MODEL)

    w_gate_b, w_up_b, w_down_b = w_gate.astype(BF16), w_up.astype(BF16), w_down.astype(BF16)
    s5_a, s5_bmat, s5_cmat = _s5_params(s5_lam_re, s5_lam_im, s5_log_dt, s5_b_re, s5_b_im, s5_c_re, s5_c_im)
    s5_dskip = s5_d.reshape(DEPTH, S5_NGB, 1, LANES)
    nfin = norm_final.reshape(1, D_MODEL)

    assert dec_len // S5_SEQ == TILE_S and n_dec + 1 <= SUBLANES
    tok = (S5_NSEQ, S5_SEQ, D_MODEL)
    ctx_rows = S5_NSEQ * S5_SEQ
    xs = (x_prompt.reshape(tok), x_sample.reshape(tok))
    ctx_fin, smp_fin, s5_finals = None, None, []
    for l in range(DEPTH):
        proj3, u_tm, x_all = _in_call(xs, norm_mix[l].reshape(1, D_MODEL), mod4, w_in, l)
        proj = proj3.reshape(ALL_SEQ * S5_SEQ, HG_IN_W)
        gain = hg_norm[l].reshape(1, HG_D)
        ohg_c, ctx_fin = _hgrn_call(proj, 0, hg_lb_logits, gain, None, ctx_fin, l, n_ctx, ctx_len)
        ohg_s, smp_fin = _hgrn_call(proj, ctx_rows, hg_lb_logits, gain, state_hgrn, smp_fin, l, n_dec, dec_len)
        y5_c, s5_fin = _s5_call(u_tm, 0, s5_a, s5_bmat, s5_cmat, s5_dskip, None, l)
        y5_s, _ = _s5_call(u_tm, 1, s5_a, s5_bmat, s5_cmat, s5_dskip, _s5_state_to_blocks(state_s5[:, l]), l)
        last = l == DEPTH - 1
        res = _out_call(x_all, ohg_c.reshape(S5_NSEQ, S5_SEQ, HG_W), ohg_s.reshape(S5_NSEQ, S5_SEQ, HG_W),
                        y5_c, y5_s, mod4, norm_ffn[l].reshape(1, D_MODEL), nfin,
                        s5_w_glu, w_out, w_gate_b, w_up_b, w_down_b, l, last)
        xs = res if last else (res,)
        s5_finals.append(_s5_blocks_to_state(s5_fin))
    y_prompt, y_sample = xs
    return (y_prompt.reshape(x_prompt.shape), y_sample.reshape(x_sample.shape),
            ctx_fin, jnp.stack(s5_finals, axis=1))
```

```python
import functools
import math

import jax
import jax.numpy as jnp
import numpy as np
from jax import lax
from jax.experimental import pallas as pl
from jax.experimental.pallas import tpu as pltpu

F32 = jnp.float32
BF16 = jnp.bfloat16

LANES = 128
SUBLANES = 8

D_MODEL = 1024
DEPTH = 2
GRID_W = 64
HG_W = 512
HG_HEADS = 4
HG_D = HG_W // HG_HEADS
S5_W = 512
S5_CH = 16
S5_GROUPS = S5_W // S5_CH
S5_P = 64
S5_GB = LANES // S5_CH
S5_NGB = S5_GROUPS // S5_GB
S5_SW = S5_GB * S5_P
HG_IN_W = 5 * HG_W
IN_W = HG_IN_W + S5_W
D_FF = 2816
EPS = 1e-6

HG_CHUNK = 128
HG_LEVELS = (64, 32, 16, 8, 4, 2, 1)
S5_SEQ = 256
S5_NSEQ = 16
S5_TC = 16

TILE_S = SUBLANES
TILE_T = 64
TILE_ROWS = TILE_S * TILE_T
TIME_TILES = S5_SEQ // TILE_T
ALL_SEQ = 2 * S5_NSEQ
CTX_TILES = S5_NSEQ // TILE_S
CAST_ROWS = 128
MOD_TILE_N = 3072
VMEM_LIMIT = 56 * 1024 * 1024


def _sigmoid(x):
    return 1.0 / (1.0 + jnp.exp(-x))


def _silu(x):
    return x * _sigmoid(x)


def _gelu_tanh(x):
    return 0.5 * x * (1.0 + jnp.tanh(math.sqrt(2.0 / math.pi) * (x + 0.044715 * (x * x * x))))


def _rms(x):
    return x * lax.rsqrt(jnp.mean(x * x, axis=-1, keepdims=True) + EPS)


def _dot(a, b):
    return jnp.dot(a, b, preferred_element_type=F32)


def _dot_nt(a, b):
    return lax.dot_general(a, b, (((1,), (1,)), ((), ())), preferred_element_type=F32)


def _dot_tn(a, b):
    return lax.dot_general(a, b, (((0,), (0,)), ((), ())), preferred_element_type=F32)


def _layer_spec(shape, layer):
    nd = len(shape)
    return pl.BlockSpec((None,) + tuple(shape), lambda *_: (layer,) + (0,) * nd, pipeline_mode=pl.Buffered(1))


def _mod_kernel(cond_ref, w_ref, b_ref, o_ref):
    a = _silu(cond_ref[...]).astype(BF16)
    o_ref[0] = _dot(a, w_ref[0].astype(BF16)) + b_ref[0]


def _mod_call(cond, w_mod, b_mod):
    n_cond = cond.shape[0]
    n_out = w_mod.shape[-1]
    return pl.pallas_call(
        _mod_kernel,
        grid=(DEPTH, n_out // MOD_TILE_N),
        in_specs=[
            pl.BlockSpec((n_cond, D_MODEL), lambda l, j: (0, 0)),
            pl.BlockSpec((1, D_MODEL, MOD_TILE_N), lambda l, j: (l, 0, j)),
            pl.BlockSpec((1, 1, MOD_TILE_N), lambda l, j: (l, 0, j)),
        ],
        out_specs=pl.BlockSpec((1, n_cond, MOD_TILE_N), lambda l, j: (l, 0, j)),
        out_shape=jax.ShapeDtypeStruct((DEPTH, n_cond, n_out), F32),
        compiler_params=pltpu.CompilerParams(dimension_semantics=("parallel", "parallel"),
                                             vmem_limit_bytes=VMEM_LIMIT),
        name="adaln_mod",
    )(cond, w_mod, b_mod.reshape(DEPTH, 1, n_out))


def _first_step():
    return jnp.logical_and(pl.program_id(0) == 0, pl.program_id(1) == 0)


def _cast_rows(src_ref, dst_ref):
    for r in range(0, src_ref.shape[0], CAST_ROWS):
        dst_ref[r:r + CAST_ROWS, :] = src_ref[r:r + CAST_ROWS, :].astype(BF16)


def _grid_pos_tile(omega, tb):
    nf = omega.shape[-1]
    s_idx = lax.broadcasted_iota(jnp.int32, (TILE_S, nf), 0)
    j_idx = lax.broadcasted_iota(jnp.int32, (TILE_T, nf), 0)
    t0 = tb * TILE_T
    row = (s_idx * (S5_SEQ // GRID_W) + t0 // GRID_W).astype(F32) * omega
    col = (j_idx + t0 % GRID_W).astype(F32) * omega
    enc_r = jnp.concatenate([jnp.sin(row), jnp.cos(row)], axis=-1)
    enc_c = jnp.concatenate([jnp.sin(col), jnp.cos(col)], axis=-1)
    shape = (TILE_S, TILE_T, 2 * nf)
    return jnp.concatenate([jnp.broadcast_to(enc_r[:, None, :], shape),
                            jnp.broadcast_to(enc_c[None, :, :], shape)], axis=-1)


def _is_sample_tile():
    return pl.program_id(0) >= CTX_TILES


def _in_kernel(*refs, first_layer):
    if first_layer:
        xc_ref, xs_ref, om_ref, gain_ref, sh_ref, sc_ref, w_ref, proj_ref, u_ref, xo_ref, wb_ref = refs
        x = jnp.where(_is_sample_tile(), xs_ref[...] + _grid_pos_tile(om_ref[...], pl.program_id(1)), xc_ref[...])
        xo_ref[...] = x
    else:
        x_ref, gain_ref, sh_ref, sc_ref, w_ref, proj_ref, u_ref, wb_ref = refs
        x = x_ref[...]

    @pl.when(_first_step())
    def _():
        _cast_rows(w_ref, wb_ref)

    x = x.reshape(TILE_ROWS, D_MODEL)
    h = _rms(x) * gain_ref[...]
    h = (h * (1.0 + sc_ref[...]) + sh_ref[...]).astype(BF16)
    proj_ref[...] = _dot(h, wb_ref[:, :HG_IN_W]).reshape(proj_ref.shape)
    u = _dot(h, wb_ref[:, HG_IN_W:])
    for s in range(TILE_S):
        u_ref[:, s, :] = u[s * TILE_T:(s + 1) * TILE_T, :]


def _tile_spec(width):
    return pl.BlockSpec((TILE_S, TILE_T, width), lambda sb, tb: (sb, tb, 0))


def _tm_tile_spec(width):
    return pl.BlockSpec((TILE_T, TILE_S, width), lambda sb, tb: (tb, sb, 0))


def _ctx_index(sb, tb):
    on = sb < CTX_TILES
    return jnp.where(on, sb, CTX_TILES - 1), jnp.where(on, tb, TIME_TILES - 1)


def _smp_index(sb, tb):
    on = sb >= CTX_TILES
    return jnp.where(on, sb - CTX_TILES, 0), jnp.where(on, tb, 0)


def _part_tile_spec(width, index):
    return pl.BlockSpec((TILE_S, TILE_T, width), lambda sb, tb: index(sb, tb) + (0,))


def _part_tm_tile_spec(width, index):
    return pl.BlockSpec((TILE_T, TILE_S, width), lambda sb, tb: index(sb, tb)[::-1] + (0,))


def _mod_spec(layer, col):
    return pl.BlockSpec((None, None, 1, D_MODEL),
                        lambda sb, tb: (layer, jnp.maximum(sb - (CTX_TILES - 1), 0), 0, col))


def _in_call(xs, gain, mod4, w_in, layer):
    first_layer = len(xs) == 2
    if first_layer:
        assert GRID_W % TILE_T == 0 and S5_SEQ % GRID_W == 0
        nf = D_MODEL // 4
        omega = 1.0 / (np.float32(10000.0) ** (np.arange(nf, dtype=np.float32) / np.float32(nf)))
        in_specs = [_part_tile_spec(D_MODEL, _ctx_index), _part_tile_spec(D_MODEL, _smp_index),
                    pl.BlockSpec((1, nf), lambda sb, tb: (0, 0))]
        args = list(xs) + [jnp.asarray(omega.reshape(1, nf), F32)]
    else:
        in_specs = [_tile_spec(D_MODEL)]
        args = list(xs)
    in_specs += [
        pl.BlockSpec((1, D_MODEL), lambda sb, tb: (0, 0)),
        _mod_spec(layer, 0),
        _mod_spec(layer, 1),
        _layer_spec((D_MODEL, IN_W), layer),
    ]
    args += [gain, mod4, mod4, w_in]
    out_specs = [_tile_spec(HG_IN_W), _tm_tile_spec(S5_W)]
    out_shape = [jax.ShapeDtypeStruct((ALL_SEQ, S5_SEQ, HG_IN_W), F32),
                 jax.ShapeDtypeStruct((S5_SEQ, ALL_SEQ, S5_W), F32)]
    if first_layer:
        out_specs.append(_tile_spec(D_MODEL))
        out_shape.append(jax.ShapeDtypeStruct((ALL_SEQ, S5_SEQ, D_MODEL), F32))
    res = pl.pallas_call(
        functools.partial(_in_kernel, first_layer=first_layer),
        grid=(ALL_SEQ // TILE_S, TIME_TILES),
        in_specs=in_specs,
        out_specs=out_specs,
        out_shape=out_shape,
        scratch_shapes=[pltpu.VMEM((D_MODEL, IN_W), BF16)],
        compiler_params=pltpu.CompilerParams(dimension_semantics=("arbitrary", "arbitrary"),
                                             vmem_limit_bytes=VMEM_LIMIT),
        name="in_proj",
    )(*args)
    return (res[0], res[1], res[2]) if first_layer else (res[0], res[1], xs[0])


def _pair_boundary(b, m, rev):
    c = b.shape[0]
    span = 2 * m
    at = m if rev else m - 1
    if span >= SUBLANES:
        b3 = b.reshape(c // span, span, LANES)
        return jnp.broadcast_to(b3[:, at:at + 1, :], b3.shape).reshape(c, LANES)
    b3 = b.reshape(c // SUBLANES, SUBLANES, LANES)
    sub = lax.broadcasted_iota(jnp.int32, b3.shape, 1)
    out = None
    for p in range(SUBLANES // span):
        piece = jnp.broadcast_to(b3[:, p * span + at:p * span + at + 1, :], b3.shape)
        out = piece if out is None else jnp.where(sub >= p * span, piece, out)
    return out.reshape(c, LANES)


def _neg_abs(x):
    bits = lax.bitcast_convert_type(x, jnp.uint32) | jnp.uint32(0x80000000)
    return lax.bitcast_convert_type(bits, F32)


def _hg_gates(chains, scale):
    outs = []
    for q, fl, lb, tri in chains:
        sig = _sigmoid(fl)
        forget = lb + (1.0 - lb) * sig
        logf = jnp.log2(forget)
        key = (1.0 - lb) * (1.0 - sig)
        hi = logf.astype(BF16)
        r1 = logf - hi.astype(F32)
        mid = r1.astype(BF16)
        lo = (r1 - mid.astype(F32)).astype(BF16)
        parts = _dot(tri, jnp.concatenate([hi, mid, lo], axis=1))
        b2 = parts[:, :LANES] + parts[:, LANES:2 * LANES] + parts[:, 2 * LANES:]
        outs.append((_silu(q) * scale, key, b2, forget))
    return outs


def _hg_scores(chains, code, eye):
    c = chains[0][0].shape[0]
    o_inter = []
    for qh, key, b2, forget, v, st_ref, rev in chains:
        b_edge = b2[0:1, :] if rev else b2[c - 1:c, :]
        st = st_ref[...]
        o_inter.append(_dot_nt((qh * jnp.exp2(b2)).astype(BF16), st.astype(BF16)))
        k_end = key * jnp.exp2(b_edge - b2)
        st_ref[...] = jnp.exp2(b_edge) * st + _dot_tn(v.astype(BF16), k_end.astype(BF16))

    out = []
    for o, (qh, key, b2, forget, v, st_ref, rev) in zip(o_inter, chains):
        scores = jnp.where(eye, jnp.sum(qh * key, axis=-1, keepdims=True), 0.0)
        qb, kb = qh.astype(BF16), key.astype(BF16)
        for m in HG_LEVELS:
            k = int(math.log2(m)) + 1
            if m == 1:
                p = _dot_nt((qh * forget).astype(BF16), kb)
            elif m >= SUBLANES:
                g = c // (2 * m)
                t_half = slice(0, m) if rev else slice(m, 2 * m)
                s_half = slice(m, 2 * m) if rev else slice(0, m)
                at = m if rev else m - 1
                b3, q3, k3 = (x.reshape(g, 2 * m, LANES) for x in (b2, qh, key))
                seam = b3[:, at:at + 1, :]
                qt = (q3[:, t_half] * jnp.exp2(b3[:, t_half] - seam)).reshape(c // 2, LANES)
                ks = k3[:, s_half] * jnp.exp2(seam - b3[:, s_half])
                kk = jnp.concatenate([k3[:, t_half], ks] if rev else [ks, k3[:, t_half]], axis=1).reshape(c, LANES)
                p = _dot_nt(qt.astype(BF16), kk.astype(BF16)).reshape(g, m, c)
                s3 = scores.reshape(g, 2 * m, c)
                hit = code.reshape(g, 2 * m, c)[:, t_half] == (-k if rev else k)
                st = jnp.where(hit, p, s3[:, t_half])
                scores = jnp.concatenate([st, s3[:, s_half]] if rev else [s3[:, s_half], st], axis=1).reshape(c, c)
                continue
            else:
                e = jnp.exp2(_neg_abs(b2 - _pair_boundary(b2, m, rev))).astype(BF16)
                p = _dot_nt(qb * e, kb * e)
            scores = jnp.where(code == (-k if rev else k), p, scores)
        out.append((o, scores.astype(BF16)))
    return out


def _hgrn_kernel(*refs, layer, n_chunks, n_heads, zero_init):
    refs = list(refs)
    q_ref, ff_ref, fb_ref, v_ref, g_ref, lbl_ref, gain_ref, code_ref, tri_ref = refs[:9]
    s0_ref = None if zero_init else refs[9]
    o_ref, sfin_ref, st_ref, ob_ref, ab_ref, oi_ref, sc_ref = refs[-7:]
    c = HG_CHUNK
    code = code_ref[...]
    eye = code == 0
    chains = [(hd, d) for hd in range(n_heads) for d in (0, 1)]

    def lanes(hd):
        return slice(hd * HG_D, (hd + 1) * HG_D)

    def lower_bound(hd, d):
        lg = lbl_ref[d, :, lanes(hd)]
        ex = jnp.exp(lg - jnp.max(lg, axis=0, keepdims=True))
        soft = ex / jnp.sum(ex, axis=0, keepdims=True)
        return jnp.sum(soft[:layer + 1], axis=0, keepdims=True) - soft[0:1]

    lb = [lower_bound(hd, d) for hd, d in chains]
    scale = HG_D ** -0.5

    def rows_of(d, i):
        n = (n_chunks - 1 - i) if d else i
        return pl.ds(n * c if isinstance(n, int) else pl.multiple_of(n * c, c), c)

    for ch, (hd, d) in enumerate(chains):
        st_ref[ch] = jnp.zeros((HG_D, HG_D), F32) if zero_init else s0_ref[d, hd].T

    def gates(i):
        return _hg_gates([(q_ref[rows_of(d, i), lanes(hd)], (fb_ref if d else ff_ref)[rows_of(d, i), lanes(hd)],
                           lb[ch], tri_ref[d]) for ch, (hd, d) in enumerate(chains)], scale)

    def scores(i, ab):
        return _hg_scores([ab[ch] + (v_ref[rows_of(d, i), lanes(hd)], st_ref.at[ch], bool(d))
                           for ch, (hd, d) in enumerate(chains)], code, eye)

    def emit(i, oi_sc):
        for ch, (hd, d) in enumerate(chains):
            rows = rows_of(d, i)
            (ob_ref if d else o_ref)[rows, lanes(hd)] = (
                oi_sc[ch][0] + _dot(oi_sc[ch][1], v_ref[rows, lanes(hd)].astype(BF16)))

    if n_chunks <= 2:
        ab = [gates(i) for i in range(n_chunks)]
        for i in range(n_chunks):
            emit(i, scores(i, ab[i]))
    else:
        n_ab = ab_ref.shape[1]

        def put_ab(ab):
            for ch in range(len(chains)):
                for k in range(n_ab):
                    ab_ref[ch, k] = ab[ch][k]

        def get_oi_sc():
            return [(oi_ref[ch], sc_ref[ch]) for ch in range(len(chains))]

        put_ab(gates(0))
        oi_ref[...] = jnp.zeros(oi_ref.shape, F32)
        sc_ref[...] = jnp.zeros(sc_ref.shape, BF16)

        def body(i, carry):
            emit(jnp.maximum(i - 1, 0), get_oi_sc())
            res = scores(i, [tuple(ab_ref[ch, k] for k in range(n_ab)) for ch in range(len(chains))])
            for ch in range(len(chains)):
                oi_ref[ch] = res[ch][0]
                sc_ref[ch] = res[ch][1]
            put_ab(gates(jnp.minimum(i + 1, n_chunks - 1)))
            return carry

        lax.fori_loop(0, n_chunks, body, 0)
        emit(n_chunks - 1, get_oi_sc())
    if len(sfin_ref.shape) == 5:
        for other in range(sfin_ref.shape[0]):
            if other != layer:
                sfin_ref[other] = jnp.zeros(sfin_ref.shape[1:], F32)
        sfin_ref = sfin_ref.at[layer]
    for ch, (hd, d) in enumerate(chains):
        sfin_ref[d, hd] = st_ref[ch].T

    def finish(n, carry):
        rows = pl.ds(pl.multiple_of(n * (2 * c), 2 * c), 2 * c)
        for hd in range(n_heads):
            o_ref[rows, lanes(hd)] = (_rms(o_ref[rows, lanes(hd)] + ob_ref[rows, lanes(hd)]) * gain_ref[...]
                                      * _silu(g_ref[rows, lanes(hd)]))
        return carry

    lax.fori_loop(0, n_chunks // 2, finish, 0)


def _hgrn_call(proj, row0, lb_logits, gain, state, finals, layer, n_seq, seq_len):
    rows = n_seq * seq_len
    seq0 = row0 // seq_len
    zero_init = state is None
    n_chunks = seq_len // HG_CHUNK
    assert n_chunks % 2 == 0
    nh = HG_HEADS if n_chunks <= 2 else 2
    hw = nh * HG_D
    n_hb = HG_HEADS // nh

    def col_spec(k):
        return pl.BlockSpec((seq_len, hw), lambda b, h: (seq0 + b, k * n_hb + h))

    t, s = np.meshgrid(np.arange(HG_CHUNK), np.arange(HG_CHUNK), indexing="ij")
    lvl = np.where(t == s, 0, np.floor(np.log2(np.maximum(t ^ s, 1))).astype(np.int32) + 1)
    code = jnp.asarray(np.where(t > s, lvl, -lvl), jnp.int32)
    tri = jnp.asarray(np.stack([s <= t, s >= t]), BF16)

    in_specs = [col_spec(0), col_spec(1), col_spec(2), col_spec(3), col_spec(4),
                pl.BlockSpec((2, DEPTH, hw), lambda b, h: (0, 0, h)),
                pl.BlockSpec((1, HG_D), lambda b, h: (0, 0)),
                pl.BlockSpec((HG_CHUNK, HG_CHUNK), lambda b, h: (0, 0)),
                pl.BlockSpec((2, HG_CHUNK, HG_CHUNK), lambda b, h: (0, 0, 0))]
    args = [proj] * 5 + [lb_logits, gain, code, tri]
    if not zero_init:
        in_specs.append(pl.BlockSpec((None, None, 2, nh, HG_D, HG_D), lambda b, h: (b, layer, 0, h, 0, 0)))
        args.append(state)
    aliases = {}
    if finals is None:
        fin_spec = pl.BlockSpec((None, DEPTH, 2, nh, HG_D, HG_D), lambda b, h: (b, 0, 0, h, 0, 0))
    else:
        fin_spec = pl.BlockSpec((None, None, 2, nh, HG_D, HG_D), lambda b, h: (b, layer, 0, h, 0, 0))
        aliases[len(args)] = 1
        in_specs.append(pl.BlockSpec(memory_space=pl.ANY))
        args.append(finals)
    n_ch = 2 * nh
    return pl.pallas_call(
        functools.partial(_hgrn_kernel, layer=layer, n_chunks=n_chunks, n_heads=nh, zero_init=zero_init),
        grid=(n_seq, n_hb),
        in_specs=in_specs,
        out_specs=[pl.BlockSpec((seq_len, hw), lambda b, h: (b, h)), fin_spec],
        out_shape=[jax.ShapeDtypeStruct((rows, HG_W), F32),
                   jax.ShapeDtypeStruct((n_seq, DEPTH, 2, HG_HEADS, HG_D, HG_D), F32)],
        input_output_aliases=aliases,
        scratch_shapes=[pltpu.VMEM((n_ch, HG_D, HG_D), F32), pltpu.VMEM((seq_len, hw), F32),
                        pltpu.VMEM((n_ch, 4, HG_CHUNK, HG_D), F32), pltpu.VMEM((n_ch, HG_CHUNK, HG_D), F32),
                        pltpu.VMEM((n_ch, HG_CHUNK, HG_CHUNK), BF16)],
        compiler_params=pltpu.CompilerParams(dimension_semantics=("parallel", "parallel"),
                                             vmem_limit_bytes=VMEM_LIMIT),
        name="hgrn2_mixer",
    )(*args)


def _s5_params_kernel(lr_ref, li_ref, ldt_ref, btr_ref, bti_ref, cr_ref, ci_ref, a_ref, bm_ref, cm_ref):
    for k in range(lr_ref.shape[0]):
        _s5_params_block(*(r.at[k] for r in (lr_ref, li_ref, ldt_ref, btr_ref, bti_ref, cr_ref, ci_ref,
                                             a_ref, bm_ref, cm_ref)))


def _s5_params_block(lr_ref, li_ref, ldt_ref, btr_ref, bti_ref, cr_ref, ci_ref, a_ref, bm_ref, cm_ref):
    sw = S5_SW
    lr = jnp.minimum(lr_ref[...], -1e-4)
    li = li_ref[...]
    dt = jnp.exp(ldt_ref[...])
    mag = jnp.exp(lr * dt)
    ab_re = mag * jnp.cos(li * dt)
    ab_im = mag * jnp.sin(li * dt)
    nr = ab_re - 1.0
    den = lr * lr + li * li
    z_re = (nr * lr + ab_im * li) / den
    z_im = (ab_im * lr - nr * li) / den

    p_idx = lax.broadcasted_iota(jnp.int32, (S5_P, sw), 0)
    col = lax.broadcasted_iota(jnp.int32, (S5_P, sw), 1)
    for g in range(S5_GB):
        place = (col == p_idx + g * S5_P).astype(BF16)
        zr, zi = z_re[g:g + 1, :], z_im[g:g + 1, :]
        btr, bti = btr_ref[g], bti_ref[g]
        rows = slice(g * S5_CH, (g + 1) * S5_CH)
        bm_ref[rows, :sw] = _dot((zr * btr - zi * bti).astype(BF16), place).astype(BF16)
        bm_ref[rows, sw:] = _dot((zr * bti + zi * btr).astype(BF16), place).astype(BF16)
        cm_ref[rows, :sw] = _dot(cr_ref[g].astype(BF16), place).astype(BF16)
        cm_ref[rows, sw:] = _dot((-ci_ref[g]).astype(BF16), place).astype(BF16)
        a_ref[:, g * S5_P:(g + 1) * S5_P] = jnp.broadcast_to(ab_re[g:g + 1, :], (S5_NSEQ, S5_P))
        a_ref[:, sw + g * S5_P:sw + (g + 1) * S5_P] = jnp.broadcast_to(ab_im[g:g + 1, :], (S5_NSEQ, S5_P))


def _s5_params(lam_re, lam_im, log_dt, b_re, b_im, c_re, c_im):
    nb = DEPTH * 2 * S5_NGB
    gp = (nb, S5_GB, S5_P)
    gcp = (nb, S5_GB, S5_CH, S5_P)
    bt_re = jnp.swapaxes(b_re, -1, -2).reshape(gcp)
    bt_im = jnp.swapaxes(b_im, -1, -2).reshape(gcp)
    ldt = jnp.broadcast_to(log_dt.reshape(nb, S5_GB, 1), gp)
    per = S5_NGB
    gp_spec = pl.BlockSpec((per, S5_GB, S5_P), lambda i: (i, 0, 0))
    gcp_spec = pl.BlockSpec((per, S5_GB, S5_CH, S5_P), lambda i: (i, 0, 0, 0))
    a, bmat, cmat = pl.pallas_call(
        _s5_params_kernel,
        grid=(nb // per,),
        in_specs=[gp_spec] * 3 + [gcp_spec] * 4,
        out_specs=[pl.BlockSpec((per, S5_NSEQ, 2 * S5_SW), lambda i: (i, 0, 0)),
                   pl.BlockSpec((per, LANES, 2 * S5_SW), lambda i: (i, 0, 0)),
                   pl.BlockSpec((per, LANES, 2 * S5_SW), lambda i: (i, 0, 0))],
        out_shape=[jax.ShapeDtypeStruct((nb, S5_NSEQ, 2 * S5_SW), F32),
                   jax.ShapeDtypeStruct((nb, LANES, 2 * S5_SW), BF16),
                   jax.ShapeDtypeStruct((nb, LANES, 2 * S5_SW), BF16)],
        compiler_params=pltpu.CompilerParams(dimension_semantics=("parallel",)),
        name="s5_params",
    )(lam_re.reshape(gp), lam_im.reshape(gp), ldt, bt_re, bt_im, c_re.reshape(gcp), c_im.reshape(gcp))
    lead = (DEPTH, 2, S5_NGB)
    return (a.reshape(lead + a.shape[1:]), bmat.reshape(lead + bmat.shape[1:]), cmat.reshape(lead + cmat.shape[1:]))


def _s5_kernel(*refs, chained):
    refs = list(refs)
    u_ref, bm_ref, cm_ref, a_ref, d_ref = refs[:5]
    rest = refs[5:]
    s0_ref = rest.pop(0) if chained else None
    y_ref = rest.pop(0)
    hfin_ref = None if chained else rest.pop(0)
    hbuf_e, hbuf_o, hb0, hb1, hst = rest
    hbufs, hb16s = (hbuf_e, hbuf_o), (hb0, hb1)
    ns, sw = S5_NSEQ, S5_SW
    half = ns // 2
    n_tc = S5_SEQ // S5_TC
    blk = S5_TC * ns
    dirs = (0, 1)

    for d in dirs:
        hst[d] = jnp.zeros((ns, 2 * sw), F32)

    def steps_of(d, i):
        return pl.ds(((n_tc - 1 - i) if d else i) * S5_TC, S5_TC)

    def kept(i):
        return i // 2 if chained else 0

    def project(i, par):
        for d in dirs:
            u = u_ref[steps_of(d, i)].reshape(blk, LANES)
            hbufs[par][d, kept(i)] = _dot(u.astype(BF16), bm_ref[d])

    def scan(i, par, emit):
        a = [(a_ref[d, :half, :sw], a_ref[d, :half, sw:]) for d in dirs]
        h = [[(hst[d, k * half:(k + 1) * half, :sw], hst[d, k * half:(k + 1) * half, sw:]) for k in range(2)]
             for d in dirs]
        for jj in range(S5_TC):
            for d in dirs:
                j = S5_TC - 1 - jj if d else jj
                ar, ai = a[d]
                bu = hbufs[par].at[d, kept(i)]
                for k in range(2):
                    r = slice(j * ns + k * half, j * ns + (k + 1) * half)
                    hr, hi = h[d][k]
                    h[d][k] = (ar * hr - ai * hi + bu[r, :sw], ar * hi + ai * hr + bu[r, sw:])
                if emit:
                    r = slice(j * ns, (j + 1) * ns)
                    hb16s[par][d, r, :sw] = jnp.concatenate([h[d][0][0], h[d][1][0]], axis=0).astype(BF16)
                    hb16s[par][d, r, sw:] = jnp.concatenate([h[d][0][1], h[d][1][1]], axis=0).astype(BF16)
        for d in dirs:
            for k in range(2):
                hst[d, k * half:(k + 1) * half, :sw] = h[d][k][0]
                hst[d, k * half:(k + 1) * half, sw:] = h[d][k][1]

    def readout(i, par):
        for d in dirs:
            steps = steps_of(d, i)
            y = _dot_nt(hb16s[par][d], cm_ref[d])
            y_ref[steps] = y_ref[steps] + y.reshape(S5_TC, ns, LANES)

    def skip(n, carry):
        steps = pl.ds(n * S5_TC, S5_TC)
        y_ref[steps] = d_ref[...] * u_ref[steps]
        return carry

    project(0, 0)
    if chained:
        def sweep(k, carry):
            i = 2 * k
            scan(i, 0, False)
            project(i + 1, 1)
            scan(i + 1, 1, False)
            project(i + 2, 0)
            return carry

        lax.fori_loop(0, n_tc // 2 - 1, sweep, 0)
        scan(n_tc - 2, 0, False)
        project(n_tc - 1, 1)
        scan(n_tc - 1, 1, False)

        n_long = s0_ref.shape[1]
        pieces = ns // n_long
        for d in dirs:
            pr, pi = a_ref[d, 0:1, :sw], a_ref[d, 0:1, sw:]
            for _ in range(int(math.log2(S5_SEQ))):
                pr, pi = pr * pr - pi * pi, 2.0 * (pr * pi)
            for b in range(n_long):
                hr, hi = s0_ref[d, b:b + 1, :sw], s0_ref[d, b:b + 1, sw:]
                for k in (range(pieces - 1, -1, -1) if d else range(pieces)):
                    r = b * pieces + k
                    zr, zi = hst[d, r:r + 1, :sw], hst[d, r:r + 1, sw:]
                    hst[d, r:r + 1, :sw] = hr
                    hst[d, r:r + 1, sw:] = hi
                    hr, hi = pr * hr - pi * hi + zr, pr * hi + pi * hr + zi

    hb1[...] = jnp.zeros(hb1.shape, BF16)
    lax.fori_loop(0, n_tc, skip, 0)

    def body(k, carry):
        i = 2 * k
        scan(i, 0, True)
        readout(jnp.maximum(i - 1, 0), 1)
        if not chained:
            project(i + 1, 1)
        scan(i + 1, 1, True)
        readout(i, 0)
        if not chained:
            project(jnp.minimum(i + 2, n_tc - 1), 0)
        return carry

    lax.fori_loop(0, n_tc // 2, body, 0)
    if not chained:
        for d in dirs:
            hfin_ref[d] = hst[d]
    readout(n_tc - 1, 1)


def _s5_call(u_tm, part, a, bmat, cmat, dskip, s0, layer):
    chained = s0 is not None
    tm_spec = pl.BlockSpec((S5_SEQ, S5_NSEQ, LANES), lambda g: (0, 0, g))

    def mat_spec(rows):
        return pl.BlockSpec((None, 2, None, rows, 2 * S5_SW), lambda g: (layer, 0, g, 0, 0))

    in_specs = [pl.BlockSpec((S5_SEQ, S5_NSEQ, LANES), lambda g: (0, part, g)),
                mat_spec(LANES), mat_spec(LANES), mat_spec(S5_NSEQ),
                pl.BlockSpec((None, None, 1, LANES), lambda g: (layer, g, 0, 0))]
    args = [u_tm, bmat, cmat, a, dskip]
    out_specs = [tm_spec]
    out_shape = [jax.ShapeDtypeStruct((S5_SEQ, S5_NSEQ, S5_W), F32)]
    if chained:
        in_specs.append(pl.BlockSpec((2, None, s0.shape[2], 2 * S5_SW), lambda g: (0, g, 0, 0)))
        args.append(s0)
    else:
        out_specs.append(pl.BlockSpec((2, None, S5_NSEQ, 2 * S5_SW), lambda g: (0, g, 0, 0)))
        out_shape.append(jax.ShapeDtypeStruct((2, S5_NGB, S5_NSEQ, 2 * S5_SW), F32))
    blk = S5_TC * S5_NSEQ
    kept = S5_SEQ // S5_TC // 2 if chained else 1
    res = pl.pallas_call(
        functools.partial(_s5_kernel, chained=chained),
        grid=(S5_NGB,),
        in_specs=in_specs,
        out_specs=out_specs,
        out_shape=out_shape,
        scratch_shapes=[pltpu.VMEM((2, kept, blk, 2 * S5_SW), F32),
                        pltpu.VMEM((2, kept, blk, 2 * S5_SW), F32),
                        pltpu.VMEM((2, blk, 2 * S5_SW), BF16),
                        pltpu.VMEM((2, blk, 2 * S5_SW), BF16),
                        pltpu.VMEM((2, S5_NSEQ, 2 * S5_SW), F32)],
        compiler_params=pltpu.CompilerParams(dimension_semantics=("parallel",),
                                             vmem_limit_bytes=VMEM_LIMIT),
        name="s5_scan",
    )(*args)
    return (res[0], None) if chained else (res[0], res[1])


def _out_kernel(x_ref, ohc_ref, ohs_ref, y5c_ref, y5s_ref, g1_ref, sh2_ref, sc2_ref, g2_ref, nffn_ref, nfin_ref,
                wglu_ref, wout_ref, wg_ref, wu_ref, wd_ref, *rest, last_layer):
    if last_layer:
        oc_ref, os_ref, wglu_b, wout_b = rest
    else:
        o_ref, wglu_b, wout_b = rest

    @pl.when(_first_step())
    def _():
        _cast_rows(wglu_ref, wglu_b)
        _cast_rows(wout_ref, wout_b)

    smp = _is_sample_tile()
    y = jnp.concatenate([jnp.where(smp, y5s_ref[:, s, :], y5c_ref[:, s, :]) for s in range(TILE_S)],
                        axis=0)
    y = _gelu_tanh(y)
    y = y * _sigmoid(_dot(y.astype(BF16), wglu_b[...]))
    ohg = jnp.where(smp, ohs_ref[...], ohc_ref[...]).reshape(TILE_ROWS, HG_W)
    mix = _dot(ohg.astype(BF16), wout_b[:HG_W, :]) + _dot(y.astype(BF16), wout_b[HG_W:, :])
    x = x_ref[...].reshape(TILE_ROWS, D_MODEL) + g1_ref[...] * mix
    h = _rms(x) * nffn_ref[...]
    h = (h * (1.0 + sc2_ref[...]) + sh2_ref[...]).astype(BF16)
    act = (_silu(_dot(h, wg_ref[...])) * _dot(h, wu_ref[...])).astype(BF16)
    x = x + g2_ref[...] * _dot(act, wd_ref[...])
    if not last_layer:
        o_ref[...] = x.reshape(o_ref.shape)
    else:
        x = (_rms(x) * nfin_ref[...]).reshape(oc_ref.shape)

        @pl.when(smp)
        def _():
            os_ref[...] = x

        @pl.when(jnp.logical_not(smp))
        def _():
            oc_ref[...] = x


def _out_call(x3, ohg_c, ohg_s, y5_c, y5_s, mod4, nffn, nfin, wglu, wout, wg, wu, wd, layer, last_layer):
    vec = pl.BlockSpec((1, D_MODEL), lambda sb, tb: (0, 0))
    part_shape = jax.ShapeDtypeStruct((S5_NSEQ, S5_SEQ, D_MODEL), F32)
    if last_layer:
        out_specs = [_part_tile_spec(D_MODEL, _ctx_index), _part_tile_spec(D_MODEL, _smp_index)]
        out_shape = [part_shape, part_shape]
    else:
        out_specs = _tile_spec(D_MODEL)
        out_shape = jax.ShapeDtypeStruct(x3.shape, F32)
    return pl.pallas_call(
        functools.partial(_out_kernel, last_layer=last_layer),
        grid=(ALL_SEQ // TILE_S, TIME_TILES),
        in_specs=[_tile_spec(D_MODEL),
                  _part_tile_spec(HG_W, _ctx_index), _part_tile_spec(HG_W, _smp_index),
                  _part_tm_tile_spec(S5_W, _ctx_index), _part_tm_tile_spec(S5_W, _smp_index),
                  _mod_spec(layer, 2), _mod_spec(layer, 3), _mod_spec(layer, 4), _mod_spec(layer, 5),
                  vec, vec,
                  _layer_spec((S5_W, S5_W), layer), _layer_spec((D_MODEL, D_MODEL), layer),
                  _layer_spec((D_MODEL, D_FF), layer), _layer_spec((D_MODEL, D_FF), layer),
                  _layer_spec((D_FF, D_MODEL), layer)],
        out_specs=out_specs,
        out_shape=out_shape,
        scratch_shapes=[pltpu.VMEM((S5_W, S5_W), BF16), pltpu.VMEM((D_MODEL, D_MODEL), BF16)],
        compiler_params=pltpu.CompilerParams(dimension_semantics=("arbitrary", "arbitrary"),
                                             vmem_limit_bytes=VMEM_LIMIT),
        name="out_ffn",
    )(x3, ohg_c, ohg_s, y5_c, y5_s, mod4, mod4, mod4, mod4, nffn, nfin, wglu, wout, wg, wu, wd)


def _s5_state_to_blocks(s):
    n = s.shape[0]
    s = s.reshape(n, 2, S5_NGB, S5_GB, S5_P, 2)
    return jnp.transpose(s, (1, 2, 0, 5, 3, 4)).reshape(2, S5_NGB, n, 2 * S5_SW)


def _s5_blocks_to_state(h):
    n = h.shape[2]
    h = h.reshape(2, S5_NGB, n, 2, S5_GB, S5_P)
    return jnp.transpose(h, (2, 0, 1, 4, 5, 3)).reshape(n, 2, S5_GROUPS, S5_P, 2)


def kernel(x_prompt, x_sample, state_hgrn, state_s5, c, c_ctx, w_mod, b_mod, norm_mix, norm_ffn, norm_final, w_in, w_out, hg_lb_logits, hg_norm, s5_lam_re, s5_lam_im, s5_log_dt, s5_b_re, s5_b_im, s5_c_re, s5_c_im, s5_d, s5_w_glu, w_gate, w_up, w_down):
    n_ctx, ctx_len, _ = x_prompt.shape
    n_dec, dec_len, _ = x_sample.shape
    assert ctx_len == S5_SEQ and n_ctx == S5_NSEQ and n_dec * dec_len == S5_NSEQ * S5_SEQ

    cond = jnp.concatenate([c_ctx[None, :], c, jnp.zeros((SUBLANES - 1 - n_dec, D_MODEL), F32)], axis=0)
    mod4 = _mod_call(cond, w_mod, b_mod).reshape(DEPTH, SUBLANES, 1, 6 * D_MODEL)

    w_gate_b, w_up_b, w_down_b = w_gate.astype(BF16), w_up.astype(BF16), w_down.astype(BF16)
    s5_a, s5_bmat, s5_cmat = _s5_params(s5_lam_re, s5_lam_im, s5_log_dt, s5_b_re, s5_b_im, s5_c_re, s5_c_im)
    s5_dskip = s5_d.reshape(DEPTH, S5_NGB, 1, LANES)
    nfin = norm_final.reshape(1, D_MODEL)

    assert dec_len // S5_SEQ == TILE_S and n_dec + 1 <= SUBLANES
    tok = (S5_NSEQ, S5_SEQ, D_MODEL)
    ctx_rows = S5_NSEQ * S5_SEQ
    xs = (x_prompt.reshape(tok), x_sample.reshape(tok))
    ctx_fin, smp_fin, s5_finals = None, None, []
    for l in range(DEPTH):
        proj3, u_tm, x_all = _in_call(xs, norm_mix[l].reshape(1, D_MODEL), mod4, w_in, l)
        proj = proj3.reshape(ALL_SEQ * S5_SEQ, HG_IN_W)
        gain = hg_norm[l].reshape(1, HG_D)
        ohg_c, ctx_fin = _hgrn_call(proj, 0, hg_lb_logits, gain, None, ctx_fin, l, n_ctx, ctx_len)
        ohg_s, smp_fin = _hgrn_call(proj, ctx_rows, hg_lb_logits, gain, state_hgrn, smp_fin, l, n_dec, dec_len)
        y5_c, s5_fin = _s5_call(u_tm, 0, s5_a, s5_bmat, s5_cmat, s5_dskip, None, l)
        y5_s, _ = _s5_call(u_tm, 1, s5_a, s5_bmat, s5_cmat, s5_dskip, _s5_state_to_blocks(state_s5[:, l]), l)
        last = l == DEPTH - 1
        res = _out_call(x_all, ohg_c.reshape(S5_NSEQ, S5_SEQ, HG_W), ohg_s.reshape(S5_NSEQ, S5_SEQ, HG_W),
                        y5_c, y5_s, mod4, norm_ffn[l].reshape(1, D_MODEL), nfin,
                        s5_w_glu, w_out, w_gate_b, w_up_b, w_down_b, l, last)
        xs = res if last else (res,)
        s5_finals.append(_s5_blocks_to_state(s5_fin))
    y_prompt, y_sample = xs
    return (y_prompt.reshape(x_prompt.shape), y_sample.reshape(x_sample.shape),
            ctx_fin, jnp.stack(s5_finals, axis=1))
```

```python
import functools
import math

import jax
import jax.numpy as jnp
import numpy as np
from jax import lax
from jax.experimental import pallas as pl
from jax.experimental.pallas import tpu as pltpu

F32 = jnp.float32
BF16 = jnp.bfloat16

LANES = 128
SUBLANES = 8

D_MODEL = 1024
DEPTH = 2
GRID_W = 64
HG_W = 512
HG_HEADS = 4
HG_D = HG_W // HG_HEADS
S5_W = 512
S5_CH = 16
S5_GROUPS = S5_W // S5_CH
S5_P = 64
S5_GB = LANES // S5_CH
S5_NGB = S5_GROUPS // S5_GB
S5_SW = S5_GB * S5_P
HG_IN_W = 5 * HG_W
IN_W = HG_IN_W + S5_W
D_FF = 2816
EPS = 1e-6

HG_CHUNK = 128
HG_LEVELS = (64, 32, 16, 8, 4, 2, 1)
S5_SEQ = 256
S5_NSEQ = 16
S5_TC = 16

TILE_S = SUBLANES
TILE_T = 64
TILE_ROWS = TILE_S * TILE_T
TIME_TILES = S5_SEQ // TILE_T
ALL_SEQ = 2 * S5_NSEQ
CTX_TILES = S5_NSEQ // TILE_S
CAST_ROWS = 128
MOD_TILE_N = 1536
VMEM_LIMIT = 56 * 1024 * 1024


def _sigmoid(x):
    return 1.0 / (1.0 + jnp.exp(-x))


def _silu(x):
    return x * _sigmoid(x)


def _gelu_tanh(x):
    return 0.5 * x * (1.0 + jnp.tanh(math.sqrt(2.0 / math.pi) * (x + 0.044715 * (x * x * x))))


def _rms(x):
    return x * lax.rsqrt(jnp.mean(x * x, axis=-1, keepdims=True) + EPS)


def _dot(a, b):
    return jnp.dot(a, b, preferred_element_type=F32)


def _dot_nt(a, b):
    return lax.dot_general(a, b, (((1,), (1,)), ((), ())), preferred_element_type=F32)


def _dot_tn(a, b):
    return lax.dot_general(a, b, (((0,), (0,)), ((), ())), preferred_element_type=F32)


def _whole_spec(shape):
    return pl.BlockSpec(tuple(shape), lambda *_: (0,) * len(shape), pipeline_mode=pl.Buffered(1))


def _layer_spec(shape, layer):
    nd = len(shape)
    return pl.BlockSpec((None,) + tuple(shape), lambda *_: (layer,) + (0,) * nd, pipeline_mode=pl.Buffered(1))


def _mod_kernel(cond_ref, w_ref, b_ref, o_ref):
    a = _silu(cond_ref[...]).astype(BF16)
    o_ref[0] = _dot(a, w_ref[0].astype(BF16)) + b_ref[0]


def _mod_call(cond, w_mod, b_mod):
    n_cond = cond.shape[0]
    n_out = w_mod.shape[-1]
    return pl.pallas_call(
        _mod_kernel,
        grid=(DEPTH, n_out // MOD_TILE_N),
        in_specs=[
            pl.BlockSpec((n_cond, D_MODEL), lambda l, j: (0, 0)),
            pl.BlockSpec((1, D_MODEL, MOD_TILE_N), lambda l, j: (l, 0, j)),
            pl.BlockSpec((1, 1, MOD_TILE_N), lambda l, j: (l, 0, j)),
        ],
        out_specs=pl.BlockSpec((1, n_cond, MOD_TILE_N), lambda l, j: (l, 0, j)),
        out_shape=jax.ShapeDtypeStruct((DEPTH, n_cond, n_out), F32),
        compiler_params=pltpu.CompilerParams(dimension_semantics=("parallel", "parallel"),
                                             vmem_limit_bytes=VMEM_LIMIT),
        name="adaln_mod",
    )(cond, w_mod, b_mod.reshape(DEPTH, 1, n_out))


def _first_step():
    return jnp.logical_and(pl.program_id(0) == 0, pl.program_id(1) == 0)


def _cast_rows(src_ref, dst_ref):
    for r in range(0, src_ref.shape[0], CAST_ROWS):
        dst_ref[r:r + CAST_ROWS, :] = src_ref[r:r + CAST_ROWS, :].astype(BF16)


def _grid_pos_tile(omega, tb):
    nf = omega.shape[-1]
    s_idx = lax.broadcasted_iota(jnp.int32, (TILE_S, nf), 0)
    j_idx = lax.broadcasted_iota(jnp.int32, (TILE_T, nf), 0)
    t0 = tb * TILE_T
    row = (s_idx * (S5_SEQ // GRID_W) + t0 // GRID_W).astype(F32) * omega
    col = (j_idx + t0 % GRID_W).astype(F32) * omega
    enc_r = jnp.concatenate([jnp.sin(row), jnp.cos(row)], axis=-1)
    enc_c = jnp.concatenate([jnp.sin(col), jnp.cos(col)], axis=-1)
    shape = (TILE_S, TILE_T, 2 * nf)
    return jnp.concatenate([jnp.broadcast_to(enc_r[:, None, :], shape),
                            jnp.broadcast_to(enc_c[None, :, :], shape)], axis=-1)


def _is_sample_tile():
    return pl.program_id(0) >= CTX_TILES


def _in_kernel(*refs, first_layer):
    n_in = 10 if first_layer else 8
    for src, dst in zip(refs[n_in - 3:n_in], refs[-4:-1]):
        dst[...] = src[...].astype(BF16)
    refs = refs[:n_in - 3] + refs[n_in:-4] + refs[-1:]

    if first_layer:
        xc_ref, xs_ref, om_ref, gain_ref, sh_ref, sc_ref, w_ref, proj_ref, u_ref, xo_ref, wb_ref = refs
        x = jnp.where(_is_sample_tile(), xs_ref[...] + _grid_pos_tile(om_ref[...], pl.program_id(1)), xc_ref[...])
        xo_ref[...] = x
    else:
        x_ref, gain_ref, sh_ref, sc_ref, w_ref, proj_ref, u_ref, wb_ref = refs
        x = x_ref[...]

    @pl.when(_first_step())
    def _():
        _cast_rows(w_ref, wb_ref)

    x = x.reshape(TILE_ROWS, D_MODEL)
    h = _rms(x) * gain_ref[...]
    h = (h * (1.0 + sc_ref[...]) + sh_ref[...]).astype(BF16)
    proj_ref[...] = _dot(h, wb_ref[:, :HG_IN_W]).reshape(proj_ref.shape)
    u = _dot(h, wb_ref[:, HG_IN_W:])
    for s in range(TILE_S):
        u_ref[:, s, :] = u[s * TILE_T:(s + 1) * TILE_T, :]


def _tile_spec(width):
    return pl.BlockSpec((TILE_S, TILE_T, width), lambda sb, tb: (sb, tb, 0))


def _tm_tile_spec(width):
    return pl.BlockSpec((TILE_T, TILE_S, width), lambda sb, tb: (tb, sb, 0))


def _ctx_index(sb, tb):
    on = sb < CTX_TILES
    return jnp.where(on, sb, CTX_TILES - 1), jnp.where(on, tb, TIME_TILES - 1)


def _smp_index(sb, tb):
    on = sb >= CTX_TILES
    return jnp.where(on, sb - CTX_TILES, 0), jnp.where(on, tb, 0)


def _part_tile_spec(width, index):
    return pl.BlockSpec((TILE_S, TILE_T, width), lambda sb, tb: index(sb, tb) + (0,))


def _part_tm_tile_spec(width, index):
    return pl.BlockSpec((TILE_T, TILE_S, width), lambda sb, tb: index(sb, tb)[::-1] + (0,))


def _mod_spec(layer, col):
    return pl.BlockSpec((None, None, 1, D_MODEL),
                        lambda sb, tb: (layer, jnp.maximum(sb - (CTX_TILES - 1), 0), 0, col))


def _in_call(xs, gain, mod4, w_in, ffn_w, layer):
    first_layer = len(xs) == 2
    n_steps = (ALL_SEQ // TILE_S) * TIME_TILES
    if first_layer:
        assert GRID_W % TILE_T == 0 and S5_SEQ % GRID_W == 0
        nf = D_MODEL // 4
        omega = 1.0 / (np.float32(10000.0) ** (np.arange(nf, dtype=np.float32) / np.float32(nf)))
        in_specs = [_part_tile_spec(D_MODEL, _ctx_index), _part_tile_spec(D_MODEL, _smp_index),
                    pl.BlockSpec((1, nf), lambda sb, tb: (0, 0))]
        args = list(xs) + [jnp.asarray(omega.reshape(1, nf), F32)]
    else:
        in_specs = [_tile_spec(D_MODEL)]
        args = list(xs)
    in_specs += [
        pl.BlockSpec((1, D_MODEL), lambda sb, tb: (0, 0)),
        _mod_spec(layer, 0),
        _mod_spec(layer, 1),
        _layer_spec((D_MODEL, IN_W), layer),
    ]
    args += [gain, mod4, mod4, w_in]
    out_specs = [_tile_spec(HG_IN_W), _tm_tile_spec(S5_W)]
    out_shape = [jax.ShapeDtypeStruct((ALL_SEQ, S5_SEQ, HG_IN_W), F32),
                 jax.ShapeDtypeStruct((S5_SEQ, ALL_SEQ, S5_W), F32)]
    if first_layer:
        out_specs.append(_tile_spec(D_MODEL))
        out_shape.append(jax.ShapeDtypeStruct((ALL_SEQ, S5_SEQ, D_MODEL), F32))
    for w in ffn_w:
        _, n_rows, n_cols = w.shape
        slab = n_rows // n_steps
        assert slab * n_steps == n_rows and slab % (2 * SUBLANES) == 0
        in_specs.append(pl.BlockSpec((None, slab, n_cols), lambda sb, tb: (layer, sb * TIME_TILES + tb, 0)))
        args.append(w)
        out_specs.append(pl.BlockSpec((slab, n_cols), lambda sb, tb: (sb * TIME_TILES + tb, 0)))
        out_shape.append(jax.ShapeDtypeStruct((n_rows, n_cols), BF16))
    res = pl.pallas_call(
        functools.partial(_in_kernel, first_layer=first_layer),
        grid=(ALL_SEQ // TILE_S, TIME_TILES),
        in_specs=in_specs,
        out_specs=out_specs,
        out_shape=out_shape,
        scratch_shapes=[pltpu.VMEM((D_MODEL, IN_W), BF16)],
        compiler_params=pltpu.CompilerParams(dimension_semantics=("arbitrary", "arbitrary"),
                                             vmem_limit_bytes=VMEM_LIMIT),
        name="in_proj",
    )(*args)
    n_w = len(ffn_w)
    return res[0], res[1], (res[2] if first_layer else xs[0]), res[len(res) - n_w:]


def _pair_boundary(b, m, rev):
    c = b.shape[0]
    span = 2 * m
    at = m if rev else m - 1
    if span >= SUBLANES:
        b3 = b.reshape(c // span, span, LANES)
        return jnp.broadcast_to(b3[:, at:at + 1, :], b3.shape).reshape(c, LANES)
    b3 = b.reshape(c // SUBLANES, SUBLANES, LANES)
    sub = lax.broadcasted_iota(jnp.int32, b3.shape, 1)
    out = None
    for p in range(SUBLANES // span):
        piece = jnp.broadcast_to(b3[:, p * span + at:p * span + at + 1, :], b3.shape)
        out = piece if out is None else jnp.where(sub >= p * span, piece, out)
    return out.reshape(c, LANES)


def _neg_abs(x):
    bits = lax.bitcast_convert_type(x, jnp.uint32) | jnp.uint32(0x80000000)
    return lax.bitcast_convert_type(bits, F32)


def _hg_gates(chains, scale):
    outs = []
    for q, fl, lb, tri in chains:
        sig = _sigmoid(fl)
        forget = lb + (1.0 - lb) * sig
        logf = jnp.log2(forget)
        key = (1.0 - lb) * (1.0 - sig)
        hi = logf.astype(BF16)
        r1 = logf - hi.astype(F32)
        mid = r1.astype(BF16)
        lo = (r1 - mid.astype(F32)).astype(BF16)
        parts = _dot(tri, jnp.concatenate([hi, mid, lo], axis=1))
        b2 = parts[:, :LANES] + parts[:, LANES:2 * LANES] + parts[:, 2 * LANES:]
        outs.append((_silu(q) * scale, key, b2, forget))
    return outs


def _hg_scores(chains, code, eye):
    c = chains[0][0].shape[0]
    o_inter = []
    for qh, key, b2, forget, v, st_ref, rev in chains:
        b_edge = b2[0:1, :] if rev else b2[c - 1:c, :]
        st = st_ref[...]
        o_inter.append(_dot_nt((qh * jnp.exp2(b2)).astype(BF16), st.astype(BF16)))
        k_end = key * jnp.exp2(b_edge - b2)
        st_ref[...] = jnp.exp2(b_edge) * st + _dot_tn(v.astype(BF16), k_end.astype(BF16))

    out = []
    for o, (qh, key, b2, forget, v, st_ref, rev) in zip(o_inter, chains):
        scores = jnp.where(eye, jnp.sum(qh * key, axis=-1, keepdims=True), 0.0)
        qb, kb = qh.astype(BF16), key.astype(BF16)
        for m in HG_LEVELS:
            k = int(math.log2(m)) + 1
            if m == 1:
                p = _dot_nt((qh * forget).astype(BF16), kb)
            elif m >= SUBLANES:
                g = c // (2 * m)
                t_half = slice(0, m) if rev else slice(m, 2 * m)
                s_half = slice(m, 2 * m) if rev else slice(0, m)
                at = m if rev else m - 1
                b3, q3, k3 = (x.reshape(g, 2 * m, LANES) for x in (b2, qh, key))
                seam = b3[:, at:at + 1, :]
                qt = (q3[:, t_half] * jnp.exp2(b3[:, t_half] - seam)).reshape(c // 2, LANES)
                ks = k3[:, s_half] * jnp.exp2(seam - b3[:, s_half])
                kk = jnp.concatenate([k3[:, t_half], ks] if rev else [ks, k3[:, t_half]], axis=1).reshape(c, LANES)
                p = _dot_nt(qt.astype(BF16), kk.astype(BF16)).reshape(g, m, c)
                s3 = scores.reshape(g, 2 * m, c)
                hit = code.reshape(g, 2 * m, c)[:, t_half] == (-k if rev else k)
                st = jnp.where(hit, p, s3[:, t_half])
                scores = jnp.concatenate([st, s3[:, s_half]] if rev else [s3[:, s_half], st], axis=1).reshape(c, c)
                continue
            else:
                e = jnp.exp2(_neg_abs(b2 - _pair_boundary(b2, m, rev))).astype(BF16)
                p = _dot_nt(qb * e, kb * e)
            scores = jnp.where(code == (-k if rev else k), p, scores)
        out.append((o, scores.astype(BF16)))
    return out


def _hgrn_kernel(*refs, layer, n_chunks, n_heads, zero_init):
    refs = list(refs)
    q_ref, ff_ref, fb_ref, v_ref, g_ref, lbl_ref, gain_ref, code_ref, tri_ref = refs[:9]
    s0_ref = None if zero_init else refs[9]
    o_ref, sfin_ref, st_ref, ob_ref, ab_ref, oi_ref, sc_ref = refs[-7:]
    c = HG_CHUNK
    code = code_ref[...]
    eye = code == 0
    chains = [(hd, d) for hd in range(n_heads) for d in (0, 1)]

    def lanes(hd):
        return slice(hd * HG_D, (hd + 1) * HG_D)

    def lower_bound(hd, d):
        lg = lbl_ref[d, :, lanes(hd)]
        ex = jnp.exp(lg - jnp.max(lg, axis=0, keepdims=True))
        soft = ex / jnp.sum(ex, axis=0, keepdims=True)
        return jnp.sum(soft[:layer + 1], axis=0, keepdims=True) - soft[0:1]

    lb = [lower_bound(hd, d) for hd, d in chains]
    scale = HG_D ** -0.5

    def rows_of(d, i):
        n = (n_chunks - 1 - i) if d else i
        return pl.ds(n * c if isinstance(n, int) else pl.multiple_of(n * c, c), c)

    for ch, (hd, d) in enumerate(chains):
        st_ref[ch] = jnp.zeros((HG_D, HG_D), F32) if zero_init else s0_ref[d, hd].T

    def gates(i):
        return _hg_gates([(q_ref[rows_of(d, i), lanes(hd)], (fb_ref if d else ff_ref)[rows_of(d, i), lanes(hd)],
                           lb[ch], tri_ref[d]) for ch, (hd, d) in enumerate(chains)], scale)

    def scores(i, ab):
        return _hg_scores([ab[ch] + (v_ref[rows_of(d, i), lanes(hd)], st_ref.at[ch], bool(d))
                           for ch, (hd, d) in enumerate(chains)], code, eye)

    def emit(i, oi_sc):
        for ch, (hd, d) in enumerate(chains):
            rows = rows_of(d, i)
            (ob_ref if d else o_ref)[rows, lanes(hd)] = (
                oi_sc[ch][0] + _dot(oi_sc[ch][1], v_ref[rows, lanes(hd)].astype(BF16)))

    if n_chunks <= 2:
        ab = [gates(i) for i in range(n_chunks)]
        for i in range(n_chunks):
            emit(i, scores(i, ab[i]))
    else:
        n_ab = ab_ref.shape[1]

        def put_ab(ab):
            for ch in range(len(chains)):
                for k in range(n_ab):
                    ab_ref[ch, k] = ab[ch][k]

        def get_oi_sc():
            return [(oi_ref[ch], sc_ref[ch]) for ch in range(len(chains))]

        put_ab(gates(0))
        oi_ref[...] = jnp.zeros(oi_ref.shape, F32)
        sc_ref[...] = jnp.zeros(sc_ref.shape, BF16)

        def body(i, carry):
            emit(jnp.maximum(i - 1, 0), get_oi_sc())
            res = scores(i, [tuple(ab_ref[ch, k] for k in range(n_ab)) for ch in range(len(chains))])
            for ch in range(len(chains)):
                oi_ref[ch] = res[ch][0]
                sc_ref[ch] = res[ch][1]
            put_ab(gates(jnp.minimum(i + 1, n_chunks - 1)))
            return carry

        lax.fori_loop(0, n_chunks, body, 0)
        emit(n_chunks - 1, get_oi_sc())
    if len(sfin_ref.shape) == 5:
        for other in range(sfin_ref.shape[0]):
            if other != layer:
                sfin_ref[other] = jnp.zeros(sfin_ref.shape[1:], F32)
        sfin_ref = sfin_ref.at[layer]
    for ch, (hd, d) in enumerate(chains):
        sfin_ref[d, hd] = st_ref[ch].T

    def finish(n, carry):
        rows = pl.ds(pl.multiple_of(n * (2 * c), 2 * c), 2 * c)
        for hd in range(n_heads):
            o_ref[rows, lanes(hd)] = (_rms(o_ref[rows, lanes(hd)] + ob_ref[rows, lanes(hd)]) * gain_ref[...]
                                      * _silu(g_ref[rows, lanes(hd)]))
        return carry

    lax.fori_loop(0, n_chunks // 2, finish, 0)


def _hgrn_call(proj, row0, lb_logits, gain, state, finals, layer, n_seq, seq_len):
    rows = n_seq * seq_len
    seq0 = row0 // seq_len
    zero_init = state is None
    n_chunks = seq_len // HG_CHUNK
    assert n_chunks % 2 == 0
    nh = HG_HEADS if n_chunks <= 2 else 2
    hw = nh * HG_D
    n_hb = HG_HEADS // nh

    def col_spec(k):
        return pl.BlockSpec((seq_len, hw), lambda b, h: (seq0 + b, k * n_hb + h))

    t, s = np.meshgrid(np.arange(HG_CHUNK), np.arange(HG_CHUNK), indexing="ij")
    lvl = np.where(t == s, 0, np.floor(np.log2(np.maximum(t ^ s, 1))).astype(np.int32) + 1)
    code = jnp.asarray(np.where(t > s, lvl, -lvl), jnp.int32)
    tri = jnp.asarray(np.stack([s <= t, s >= t]), BF16)

    in_specs = [col_spec(0), col_spec(1), col_spec(2), col_spec(3), col_spec(4),
                pl.BlockSpec((2, DEPTH, hw), lambda b, h: (0, 0, h)),
                pl.BlockSpec((1, HG_D), lambda b, h: (0, 0)),
                pl.BlockSpec((HG_CHUNK, HG_CHUNK), lambda b, h: (0, 0)),
                pl.BlockSpec((2, HG_CHUNK, HG_CHUNK), lambda b, h: (0, 0, 0))]
    args = [proj] * 5 + [lb_logits, gain, code, tri]
    if not zero_init:
        in_specs.append(pl.BlockSpec((None, None, 2, nh, HG_D, HG_D), lambda b, h: (b, layer, 0, h, 0, 0)))
        args.append(state)
    aliases = {}
    if finals is None:
        fin_spec = pl.BlockSpec((None, DEPTH, 2, nh, HG_D, HG_D), lambda b, h: (b, 0, 0, h, 0, 0))
    else:
        fin_spec = pl.BlockSpec((None, None, 2, nh, HG_D, HG_D), lambda b, h: (b, layer, 0, h, 0, 0))
        aliases[len(args)] = 1
        in_specs.append(pl.BlockSpec(memory_space=pl.ANY))
        args.append(finals)
    n_ch = 2 * nh
    return pl.pallas_call(
        functools.partial(_hgrn_kernel, layer=layer, n_chunks=n_chunks, n_heads=nh, zero_init=zero_init),
        grid=(n_seq, n_hb),
        in_specs=in_specs,
        out_specs=[pl.BlockSpec((seq_len, hw), lambda b, h: (b, h)), fin_spec],
        out_shape=[jax.ShapeDtypeStruct((rows, HG_W), F32),
                   jax.ShapeDtypeStruct((n_seq, DEPTH, 2, HG_HEADS, HG_D, HG_D), F32)],
        input_output_aliases=aliases,
        scratch_shapes=[pltpu.VMEM((n_ch, HG_D, HG_D), F32), pltpu.VMEM((seq_len, hw), F32),
                        pltpu.VMEM((n_ch, 4, HG_CHUNK, HG_D), F32), pltpu.VMEM((n_ch, HG_CHUNK, HG_D), F32),
                        pltpu.VMEM((n_ch, HG_CHUNK, HG_CHUNK), BF16)],
        compiler_params=pltpu.CompilerParams(dimension_semantics=("parallel", "parallel"),
                                             vmem_limit_bytes=VMEM_LIMIT),
        name="hgrn2_mixer",
    )(*args)


def _s5_params_kernel(lr_ref, li_ref, ldt_ref, btr_ref, bti_ref, cr_ref, ci_ref, a_ref, bm_ref, cm_ref):
    for k in range(lr_ref.shape[0]):
        _s5_params_block(*(r.at[k] for r in (lr_ref, li_ref, ldt_ref, btr_ref, bti_ref, cr_ref, ci_ref,
                                             a_ref, bm_ref, cm_ref)))


def _s5_params_block(lr_ref, li_ref, ldt_ref, btr_ref, bti_ref, cr_ref, ci_ref, a_ref, bm_ref, cm_ref):
    sw = S5_SW
    lr = jnp.minimum(lr_ref[...], -1e-4)
    li = li_ref[...]
    dt = jnp.exp(ldt_ref[...])
    mag = jnp.exp(lr * dt)
    ab_re = mag * jnp.cos(li * dt)
    ab_im = mag * jnp.sin(li * dt)
    nr = ab_re - 1.0
    den = lr * lr + li * li
    z_re = (nr * lr + ab_im * li) / den
    z_im = (ab_im * lr - nr * li) / den

    p_idx = lax.broadcasted_iota(jnp.int32, (S5_P, sw), 0)
    col = lax.broadcasted_iota(jnp.int32, (S5_P, sw), 1)
    for g in range(S5_GB):
        place = (col == p_idx + g * S5_P).astype(BF16)
        zr, zi = z_re[g:g + 1, :], z_im[g:g + 1, :]
        btr, bti = btr_ref[g], bti_ref[g]
        rows = slice(g * S5_CH, (g + 1) * S5_CH)
        bm_ref[rows, :sw] = _dot((zr * btr - zi * bti).astype(BF16), place).astype(BF16)
        bm_ref[rows, sw:] = _dot((zr * bti + zi * btr).astype(BF16), place).astype(BF16)
        cm_ref[rows, :sw] = _dot(cr_ref[g].astype(BF16), place).astype(BF16)
        cm_ref[rows, sw:] = _dot((-ci_ref[g]).astype(BF16), place).astype(BF16)
        a_ref[:, g * S5_P:(g + 1) * S5_P] = jnp.broadcast_to(ab_re[g:g + 1, :], (S5_NSEQ, S5_P))
        a_ref[:, sw + g * S5_P:sw + (g + 1) * S5_P] = jnp.broadcast_to(ab_im[g:g + 1, :], (S5_NSEQ, S5_P))


def _s5_params(lam_re, lam_im, log_dt, b_re, b_im, c_re, c_im):
    nb = DEPTH * 2 * S5_NGB
    gp = (nb, S5_GB, S5_P)
    gcp = (nb, S5_GB, S5_CH, S5_P)
    bt_re = jnp.swapaxes(b_re, -1, -2).reshape(gcp)
    bt_im = jnp.swapaxes(b_im, -1, -2).reshape(gcp)
    ldt = jnp.broadcast_to(log_dt.reshape(nb, S5_GB, 1), gp)
    per = S5_NGB
    gp_spec = pl.BlockSpec((per, S5_GB, S5_P), lambda i: (i, 0, 0))
    gcp_spec = pl.BlockSpec((per, S5_GB, S5_CH, S5_P), lambda i: (i, 0, 0, 0))
    a, bmat, cmat = pl.pallas_call(
        _s5_params_kernel,
        grid=(nb // per,),
        in_specs=[gp_spec] * 3 + [gcp_spec] * 4,
        out_specs=[pl.BlockSpec((per, S5_NSEQ, 2 * S5_SW), lambda i: (i, 0, 0)),
                   pl.BlockSpec((per, LANES, 2 * S5_SW), lambda i: (i, 0, 0)),
                   pl.BlockSpec((per, LANES, 2 * S5_SW), lambda i: (i, 0, 0))],
        out_shape=[jax.ShapeDtypeStruct((nb, S5_NSEQ, 2 * S5_SW), F32),
                   jax.ShapeDtypeStruct((nb, LANES, 2 * S5_SW), BF16),
                   jax.ShapeDtypeStruct((nb, LANES, 2 * S5_SW), BF16)],
        compiler_params=pltpu.CompilerParams(dimension_semantics=("parallel",)),
        name="s5_params",
    )(lam_re.reshape(gp), lam_im.reshape(gp), ldt, bt_re, bt_im, c_re.reshape(gcp), c_im.reshape(gcp))
    lead = (DEPTH, 2, S5_NGB)
    return (a.reshape(lead + a.shape[1:]), bmat.reshape(lead + bmat.shape[1:]), cmat.reshape(lead + cmat.shape[1:]))


def _s5_kernel(*refs, chained):
    refs = list(refs)
    u_ref, bm_ref, cm_ref, a_ref, d_ref = refs[:5]
    rest = refs[5:]
    s0_ref = rest.pop(0) if chained else None
    y_ref = rest.pop(0)
    hfin_ref = None if chained else rest.pop(0)
    hbuf_e, hbuf_o, hb0, hb1, hst = rest
    hbufs, hb16s = (hbuf_e, hbuf_o), (hb0, hb1)
    ns, sw = S5_NSEQ, S5_SW
    half = ns // 2
    n_tc = S5_SEQ // S5_TC
    blk = S5_TC * ns
    dirs = (0, 1)

    for d in dirs:
        hst[d] = jnp.zeros((ns, 2 * sw), F32)

    def steps_of(d, i):
        return pl.ds(((n_tc - 1 - i) if d else i) * S5_TC, S5_TC)

    def kept(i):
        return i // 2 if chained else 0

    def project(i, par):
        for d in dirs:
            u = u_ref[steps_of(d, i)].reshape(blk, LANES)
            hbufs[par][d, kept(i)] = _dot(u.astype(BF16), bm_ref[d])

    def scan(i, par, emit):
        a = [(a_ref[d, :half, :sw], a_ref[d, :half, sw:]) for d in dirs]
        h = [[(hst[d, k * half:(k + 1) * half, :sw], hst[d, k * half:(k + 1) * half, sw:]) for k in range(2)]
             for d in dirs]
        for jj in range(S5_TC):
            for d in dirs:
                j = S5_TC - 1 - jj if d else jj
                ar, ai = a[d]
                bu = hbufs[par].at[d, kept(i)]
                for k in range(2):
                    r = slice(j * ns + k * half, j * ns + (k + 1) * half)
                    hr, hi = h[d][k]
                    h[d][k] = (ar * hr - ai * hi + bu[r, :sw], ar * hi + ai * hr + bu[r, sw:])
                if emit:
                    r = slice(j * ns, (j + 1) * ns)
                    hb16s[par][d, r, :sw] = jnp.concatenate([h[d][0][0], h[d][1][0]], axis=0).astype(BF16)
                    hb16s[par][d, r, sw:] = jnp.concatenate([h[d][0][1], h[d][1][1]], axis=0).astype(BF16)
        for d in dirs:
            for k in range(2):
                hst[d, k * half:(k + 1) * half, :sw] = h[d][k][0]
                hst[d, k * half:(k + 1) * half, sw:] = h[d][k][1]

    def readout(i, par):
        for d in dirs:
            steps = steps_of(d, i)
            y = _dot_nt(hb16s[par][d], cm_ref[d])
            y_ref[steps] = y_ref[steps] + y.reshape(S5_TC, ns, LANES)

    def skip(n, carry):
        steps = pl.ds(n * S5_TC, S5_TC)
        y_ref[steps] = d_ref[...] * u_ref[steps]
        return carry

    project(0, 0)
    if chained:
        def sweep(k, carry):
            i = 2 * k
            scan(i, 0, False)
            project(i + 1, 1)
            scan(i + 1, 1, False)
            project(i + 2, 0)
            return carry

        lax.fori_loop(0, n_tc // 2 - 1, sweep, 0)
        scan(n_tc - 2, 0, False)
        project(n_tc - 1, 1)
        scan(n_tc - 1, 1, False)

        n_long = s0_ref.shape[1]
        pieces = ns // n_long
        for d in dirs:
            pr, pi = a_ref[d, 0:1, :sw], a_ref[d, 0:1, sw:]
            for _ in range(int(math.log2(S5_SEQ))):
                pr, pi = pr * pr - pi * pi, 2.0 * (pr * pi)
            for b in range(n_long):
                hr, hi = s0_ref[d, b:b + 1, :sw], s0_ref[d, b:b + 1, sw:]
                for k in (range(pieces - 1, -1, -1) if d else range(pieces)):
                    r = b * pieces + k
                    zr, zi = hst[d, r:r + 1, :sw], hst[d, r:r + 1, sw:]
                    hst[d, r:r + 1, :sw] = hr
                    hst[d, r:r + 1, sw:] = hi
                    hr, hi = pr * hr - pi * hi + zr, pr * hi + pi * hr + zi

    hb1[...] = jnp.zeros(hb1.shape, BF16)
    lax.fori_loop(0, n_tc, skip, 0)

    def body(k, carry):
        i = 2 * k
        scan(i, 0, True)
        readout(jnp.maximum(i - 1, 0), 1)
        if not chained:
            project(i + 1, 1)
        scan(i + 1, 1, True)
        readout(i, 0)
        if not chained:
            project(jnp.minimum(i + 2, n_tc - 1), 0)
        return carry

    lax.fori_loop(0, n_tc // 2, body, 0)
    if not chained:
        for d in dirs:
            hfin_ref[d] = hst[d]
    readout(n_tc - 1, 1)


def _s5_call(u_tm, part, a, bmat, cmat, dskip, s0, layer):
    chained = s0 is not None
    tm_spec = pl.BlockSpec((S5_SEQ, S5_NSEQ, LANES), lambda g: (0, 0, g))

    def mat_spec(rows):
        return pl.BlockSpec((None, 2, None, rows, 2 * S5_SW), lambda g: (layer, 0, g, 0, 0))

    in_specs = [pl.BlockSpec((S5_SEQ, S5_NSEQ, LANES), lambda g: (0, part, g)),
                mat_spec(LANES), mat_spec(LANES), mat_spec(S5_NSEQ),
                pl.BlockSpec((None, None, 1, LANES), lambda g: (layer, g, 0, 0))]
    args = [u_tm, bmat, cmat, a, dskip]
    out_specs = [tm_spec]
    out_shape = [jax.ShapeDtypeStruct((S5_SEQ, S5_NSEQ, S5_W), F32)]
    if chained:
        in_specs.append(pl.BlockSpec((2, None, s0.shape[2], 2 * S5_SW), lambda g: (0, g, 0, 0)))
        args.append(s0)
    else:
        out_specs.append(pl.BlockSpec((2, None, S5_NSEQ, 2 * S5_SW), lambda g: (0, g, 0, 0)))
        out_shape.append(jax.ShapeDtypeStruct((2, S5_NGB, S5_NSEQ, 2 * S5_SW), F32))
    blk = S5_TC * S5_NSEQ
    kept = S5_SEQ // S5_TC // 2 if chained else 1
    res = pl.pallas_call(
        functools.partial(_s5_kernel, chained=chained),
        grid=(S5_NGB,),
        in_specs=in_specs,
        out_specs=out_specs,
        out_shape=out_shape,
        scratch_shapes=[pltpu.VMEM((2, kept, blk, 2 * S5_SW), F32),
                        pltpu.VMEM((2, kept, blk, 2 * S5_SW), F32),
                        pltpu.VMEM((2, blk, 2 * S5_SW), BF16),
                        pltpu.VMEM((2, blk, 2 * S5_SW), BF16),
                        pltpu.VMEM((2, S5_NSEQ, 2 * S5_SW), F32)],
        compiler_params=pltpu.CompilerParams(dimension_semantics=("parallel",),
                                             vmem_limit_bytes=VMEM_LIMIT),
        name="s5_scan",
    )(*args)
    return (res[0], None) if chained else (res[0], res[1])


def _out_kernel(x_ref, ohc_ref, ohs_ref, y5c_ref, y5s_ref, g1_ref, sh2_ref, sc2_ref, g2_ref, nffn_ref, nfin_ref,
                wglu_ref, wout_ref, wg_ref, wu_ref, wd_ref, *rest, last_layer):
    if last_layer:
        oc_ref, os_ref, wglu_b, wout_b = rest
    else:
        o_ref, wglu_b, wout_b = rest

    @pl.when(_first_step())
    def _():
        _cast_rows(wglu_ref, wglu_b)
        _cast_rows(wout_ref, wout_b)

    smp = _is_sample_tile()
    y = jnp.concatenate([jnp.where(smp, y5s_ref[:, s, :], y5c_ref[:, s, :]) for s in range(TILE_S)],
                        axis=0)
    y = _gelu_tanh(y)
    y = y * _sigmoid(_dot(y.astype(BF16), wglu_b[...]))
    ohg = jnp.where(smp, ohs_ref[...], ohc_ref[...]).reshape(TILE_ROWS, HG_W)
    mix = _dot(ohg.astype(BF16), wout_b[:HG_W, :]) + _dot(y.astype(BF16), wout_b[HG_W:, :])
    x = x_ref[...].reshape(TILE_ROWS, D_MODEL) + g1_ref[...] * mix
    h = _rms(x) * nffn_ref[...]
    h = (h * (1.0 + sc2_ref[...]) + sh2_ref[...]).astype(BF16)
    act = (_silu(_dot(h, wg_ref[...])) * _dot(h, wu_ref[...])).astype(BF16)
    x = x + g2_ref[...] * _dot(act, wd_ref[...])
    if not last_layer:
        o_ref[...] = x.reshape(o_ref.shape)
    else:
        x = (_rms(x) * nfin_ref[...]).reshape(oc_ref.shape)

        @pl.when(smp)
        def _():
            os_ref[...] = x

        @pl.when(jnp.logical_not(smp))
        def _():
            oc_ref[...] = x


def _out_call(x3, ohg_c, ohg_s, y5_c, y5_s, mod4, nffn, nfin, wglu, wout, wg, wu, wd, layer, last_layer):
    vec = pl.BlockSpec((1, D_MODEL), lambda sb, tb: (0, 0))
    part_shape = jax.ShapeDtypeStruct((S5_NSEQ, S5_SEQ, D_MODEL), F32)
    if last_layer:
        out_specs = [_part_tile_spec(D_MODEL, _ctx_index), _part_tile_spec(D_MODEL, _smp_index)]
        out_shape = [part_shape, part_shape]
    else:
        out_specs = _tile_spec(D_MODEL)
        out_shape = jax.ShapeDtypeStruct(x3.shape, F32)
    return pl.pallas_call(
        functools.partial(_out_kernel, last_layer=last_layer),
        grid=(ALL_SEQ // TILE_S, TIME_TILES),
        in_specs=[_tile_spec(D_MODEL),
                  _part_tile_spec(HG_W, _ctx_index), _part_tile_spec(HG_W, _smp_index),
                  _part_tm_tile_spec(S5_W, _ctx_index), _part_tm_tile_spec(S5_W, _smp_index),
                  _mod_spec(layer, 2), _mod_spec(layer, 3), _mod_spec(layer, 4), _mod_spec(layer, 5),
                  vec, vec,
                  _layer_spec((S5_W, S5_W), layer), _layer_spec((D_MODEL, D_MODEL), layer),
                  _whole_spec((D_MODEL, D_FF)), _whole_spec((D_MODEL, D_FF)), _whole_spec((D_FF, D_MODEL))],
        out_specs=out_specs,
        out_shape=out_shape,
        scratch_shapes=[pltpu.VMEM((S5_W, S5_W), BF16), pltpu.VMEM((D_MODEL, D_MODEL), BF16)],
        compiler_params=pltpu.CompilerParams(dimension_semantics=("arbitrary", "arbitrary"),
                                             vmem_limit_bytes=VMEM_LIMIT),
        name="out_ffn",
    )(x3, ohg_c, ohg_s, y5_c, y5_s, mod4, mod4, mod4, mod4, nffn, nfin, wglu, wout, wg, wu, wd)


def _s5_state_to_blocks(s):
    n = s.shape[0]
    s = s.reshape(n, 2, S5_NGB, S5_GB, S5_P, 2)
    return jnp.transpose(s, (1, 2, 0, 5, 3, 4)).reshape(2, S5_NGB, n, 2 * S5_SW)


def _s5_blocks_to_state(h):
    n = h.shape[2]
    h = h.reshape(2, S5_NGB, n, 2, S5_GB, S5_P)
    return jnp.transpose(h, (2, 0, 1, 4, 5, 3)).reshape(n, 2, S5_GROUPS, S5_P, 2)


def kernel(x_prompt, x_sample, state_hgrn, state_s5, c, c_ctx, w_mod, b_mod, norm_mix, norm_ffn, norm_final, w_in, w_out, hg_lb_logits, hg_norm, s5_lam_re, s5_lam_im, s5_log_dt, s5_b_re, s5_b_im, s5_c_re, s5_c_im, s5_d, s5_w_glu, w_gate, w_up, w_down):
    n_ctx, ctx_len, _ = x_prompt.shape
    n_dec, dec_len, _ = x_sample.shape
    assert ctx_len == S5_SEQ and n_ctx == S5_NSEQ and n_dec * dec_len == S5_NSEQ * S5_SEQ

    cond = jnp.concatenate([c_ctx[None, :], c, jnp.zeros((SUBLANES - 1 - n_dec, D_MODEL), F32)], axis=0)
    mod4 = _mod_call(cond, w_mod, b_mod).reshape(DEPTH, SUBLANES, 1, 6 * D_MODEL)

    s5_a, s5_bmat, s5_cmat = _s5_params(s5_lam_re, s5_lam_im, s5_log_dt, s5_b_re, s5_b_im, s5_c_re, s5_c_im)
    s5_dskip = s5_d.reshape(DEPTH, S5_NGB, 1, LANES)
    nfin = norm_final.reshape(1, D_MODEL)

    assert dec_len // S5_SEQ == TILE_S and n_dec + 1 <= SUBLANES
    tok = (S5_NSEQ, S5_SEQ, D_MODEL)
    ctx_rows = S5_NSEQ * S5_SEQ
    xs = (x_prompt.reshape(tok), x_sample.reshape(tok))
    ctx_fin, smp_fin, s5_finals = None, None, []
    for l in range(DEPTH):
        proj3, u_tm, x_all, (w_gate_b, w_up_b, w_down_b) = _in_call(
            xs, norm_mix[l].reshape(1, D_MODEL), mod4, w_in, (w_gate, w_up, w_down), l)
        proj = proj3.reshape(ALL_SEQ * S5_SEQ, HG_IN_W)
        gain = hg_norm[l].reshape(1, HG_D)
        ohg_c, ctx_fin = _hgrn_call(proj, 0, hg_lb_logits, gain, None, ctx_fin, l, n_ctx, ctx_len)
        ohg_s, smp_fin = _hgrn_call(proj, ctx_rows, hg_lb_logits, gain, state_hgrn, smp_fin, l, n_dec, dec_len)
        y5_c, s5_fin = _s5_call(u_tm, 0, s5_a, s5_bmat, s5_cmat, s5_dskip, None, l)
        y5_s, _ = _s5_call(u_tm, 1, s5_a, s5_bmat, s5_cmat, s5_dskip, _s5_state_to_blocks(state_s5[:, l]), l)
        last = l == DEPTH - 1
        res = _out_call(x_all, ohg_c.reshape(S5_NSEQ, S5_SEQ, HG_W), ohg_s.reshape(S5_NSEQ, S5_SEQ, HG_W),
                        y5_c, y5_s, mod4, norm_ffn[l].reshape(1, D_MODEL), nfin,
                        s5_w_glu, w_out, w_gate_b, w_up_b, w_down_b, l, last)
        xs = res if last else (res,)
        s5_finals.append(_s5_blocks_to_state(s5_fin))
    y_prompt, y_sample = xs
    return (y_prompt.reshape(x_prompt.shape), y_sample.reshape(x_sample.shape),
            ctx_fin, jnp.stack(s5_finals, axis=1))
```

```python
import functools
import math

import jax
import jax.numpy as jnp
import numpy as np
from jax import lax
from jax.experimental import pallas as pl
from jax.experimental.pallas import tpu as pltpu

F32 = jnp.float32
BF16 = jnp.bfloat16

LANES = 128
SUBLANES = 8

D_MODEL = 1024
DEPTH = 2
GRID_W = 64
HG_W = 512
HG_HEADS = 4
HG_D = HG_W // HG_HEADS
S5_W = 512
S5_CH = 16
S5_GROUPS = S5_W // S5_CH
S5_P = 64
S5_GB = LANES // S5_CH
S5_NGB = S5_GROUPS // S5_GB
S5_SW = S5_GB * S5_P
HG_IN_W = 5 * HG_W
IN_W = HG_IN_W + S5_W
D_FF = 2816
EPS = 1e-6

HG_CHUNK = 128
HG_LEVELS = (64, 32, 16, 8, 4, 2, 1)
S5_SEQ = 256
S5_NSEQ = 16
S5_TC = 16

TILE_S = SUBLANES
TILE_T = 64
TILE_ROWS = TILE_S * TILE_T
TIME_TILES = S5_SEQ // TILE_T
ALL_SEQ = 2 * S5_NSEQ
CTX_TILES = S5_NSEQ // TILE_S
CAST_ROWS = 128
MOD_TILE_N = 1536
VMEM_LIMIT = 56 * 1024 * 1024


def _sigmoid(x):
    return 1.0 / (1.0 + jnp.exp(-x))


def _silu(x):
    return x * _sigmoid(x)


def _gelu_tanh(x):
    return 0.5 * x * (1.0 + jnp.tanh(math.sqrt(2.0 / math.pi) * (x + 0.044715 * (x * x * x))))


def _rms(x):
    return x * lax.rsqrt(jnp.mean(x * x, axis=-1, keepdims=True) + EPS)


def _dot(a, b):
    return jnp.dot(a, b, preferred_element_type=F32)


def _dot_nt(a, b):
    return lax.dot_general(a, b, (((1,), (1,)), ((), ())), preferred_element_type=F32)


def _dot_tn(a, b):
    return lax.dot_general(a, b, (((0,), (0,)), ((), ())), preferred_element_type=F32)


def _whole_spec(shape):
    return pl.BlockSpec(tuple(shape), lambda *_: (0,) * len(shape), pipeline_mode=pl.Buffered(1))


def _layer_spec(shape, layer):
    nd = len(shape)
    return pl.BlockSpec((None,) + tuple(shape), lambda *_: (layer,) + (0,) * nd, pipeline_mode=pl.Buffered(1))


def _mod_kernel(cond_ref, w_ref, b_ref, o_ref):
    a = _silu(cond_ref[...]).astype(BF16)
    o_ref[0] = _dot(a, w_ref[0].astype(BF16)) + b_ref[0]


def _mod_call(cond, w_mod, b_mod):
    n_cond = cond.shape[0]
    n_out = w_mod.shape[-1]
    return pl.pallas_call(
        _mod_kernel,
        grid=(DEPTH, n_out // MOD_TILE_N),
        in_specs=[
            pl.BlockSpec((n_cond, D_MODEL), lambda l, j: (0, 0)),
            pl.BlockSpec((1, D_MODEL, MOD_TILE_N), lambda l, j: (l, 0, j)),
            pl.BlockSpec((1, 1, MOD_TILE_N), lambda l, j: (l, 0, j)),
        ],
        out_specs=pl.BlockSpec((1, n_cond, MOD_TILE_N), lambda l, j: (l, 0, j)),
        out_shape=jax.ShapeDtypeStruct((DEPTH, n_cond, n_out), F32),
        compiler_params=pltpu.CompilerParams(dimension_semantics=("parallel", "parallel"),
                                             vmem_limit_bytes=VMEM_LIMIT),
        name="adaln_mod",
    )(cond, w_mod, b_mod.reshape(DEPTH, 1, n_out))


def _first_step():
    return jnp.logical_and(pl.program_id(0) == 0, pl.program_id(1) == 0)


def _cast_rows(src_ref, dst_ref):
    for r in range(0, src_ref.shape[0], CAST_ROWS):
        dst_ref[r:r + CAST_ROWS, :] = src_ref[r:r + CAST_ROWS, :].astype(BF16)


def _grid_pos_tile(omega, tb):
    nf = omega.shape[-1]
    s_idx = lax.broadcasted_iota(jnp.int32, (TILE_S, nf), 0)
    j_idx = lax.broadcasted_iota(jnp.int32, (TILE_T, nf), 0)
    t0 = tb * TILE_T
    row = (s_idx * (S5_SEQ // GRID_W) + t0 // GRID_W).astype(F32) * omega
    col = (j_idx + t0 % GRID_W).astype(F32) * omega
    enc_r = jnp.concatenate([jnp.sin(row), jnp.cos(row)], axis=-1)
    enc_c = jnp.concatenate([jnp.sin(col), jnp.cos(col)], axis=-1)
    shape = (TILE_S, TILE_T, 2 * nf)
    return jnp.concatenate([jnp.broadcast_to(enc_r[:, None, :], shape),
                            jnp.broadcast_to(enc_c[None, :, :], shape)], axis=-1)


def _is_sample_tile():
    return pl.program_id(0) >= CTX_TILES


def _in_kernel(*refs, first_layer):
    n_in = 10 if first_layer else 8
    for src, dst in zip(refs[n_in - 3:n_in], refs[-4:-1]):
        dst[...] = src[...].astype(BF16)
    refs = refs[:n_in - 3] + refs[n_in:-4] + refs[-1:]

    if first_layer:
        xc_ref, xs_ref, om_ref, gain_ref, sh_ref, sc_ref, w_ref, proj_ref, u_ref, xo_ref, wb_ref = refs
        x = jnp.where(_is_sample_tile(), xs_ref[...] + _grid_pos_tile(om_ref[...], pl.program_id(1)), xc_ref[...])
        xo_ref[...] = x
    else:
        x_ref, gain_ref, sh_ref, sc_ref, w_ref, proj_ref, u_ref, wb_ref = refs
        x = x_ref[...]

    @pl.when(_first_step())
    def _():
        _cast_rows(w_ref, wb_ref)

    x = x.reshape(TILE_ROWS, D_MODEL)
    h = _rms(x) * gain_ref[...]
    h = (h * (1.0 + sc_ref[...]) + sh_ref[...]).astype(BF16)
    u = _dot(h, wb_ref[:, HG_IN_W:])
    for s in range(TILE_S):
        u_ref[:, s, :] = u[s * TILE_T:(s + 1) * TILE_T, :]
    proj_ref[...] = _dot(h, wb_ref[:, :HG_IN_W]).reshape(proj_ref.shape)


def _tile_spec(width):
    return pl.BlockSpec((TILE_S, TILE_T, width), lambda sb, tb: (sb, tb, 0))


def _tm_tile_spec(width):
    return pl.BlockSpec((TILE_T, TILE_S, width), lambda sb, tb: (tb, sb, 0))


def _ctx_index(sb, tb):
    on = sb < CTX_TILES
    return jnp.where(on, sb, CTX_TILES - 1), jnp.where(on, tb, TIME_TILES - 1)


def _smp_index(sb, tb):
    on = sb >= CTX_TILES
    return jnp.where(on, sb - CTX_TILES, 0), jnp.where(on, tb, 0)


def _part_tile_spec(width, index):
    return pl.BlockSpec((TILE_S, TILE_T, width), lambda sb, tb: index(sb, tb) + (0,))


def _part_tm_tile_spec(width, index):
    return pl.BlockSpec((TILE_T, TILE_S, width), lambda sb, tb: index(sb, tb)[::-1] + (0,))


def _mod_spec(layer, col):
    return pl.BlockSpec((None, None, 1, D_MODEL),
                        lambda sb, tb: (layer, jnp.maximum(sb - (CTX_TILES - 1), 0), 0, col))


def _in_call(xs, gain, mod4, w_in, ffn_w, layer):
    first_layer = len(xs) == 2
    n_steps = (ALL_SEQ // TILE_S) * TIME_TILES
    if first_layer:
        assert GRID_W % TILE_T == 0 and S5_SEQ % GRID_W == 0
        nf = D_MODEL // 4
        omega = 1.0 / (np.float32(10000.0) ** (np.arange(nf, dtype=np.float32) / np.float32(nf)))
        in_specs = [_part_tile_spec(D_MODEL, _ctx_index), _part_tile_spec(D_MODEL, _smp_index),
                    pl.BlockSpec((1, nf), lambda sb, tb: (0, 0))]
        args = list(xs) + [jnp.asarray(omega.reshape(1, nf), F32)]
    else:
        in_specs = [_tile_spec(D_MODEL)]
        args = list(xs)
    in_specs += [
        pl.BlockSpec((1, D_MODEL), lambda sb, tb: (0, 0)),
        _mod_spec(layer, 0),
        _mod_spec(layer, 1),
        _layer_spec((D_MODEL, IN_W), layer),
    ]
    args += [gain, mod4, mod4, w_in]
    out_specs = [_tile_spec(HG_IN_W), _tm_tile_spec(S5_W)]
    out_shape = [jax.ShapeDtypeStruct((ALL_SEQ, S5_SEQ, HG_IN_W), F32),
                 jax.ShapeDtypeStruct((S5_SEQ, ALL_SEQ, S5_W), F32)]
    if first_layer:
        out_specs.append(_tile_spec(D_MODEL))
        out_shape.append(jax.ShapeDtypeStruct((ALL_SEQ, S5_SEQ, D_MODEL), F32))
    for w in ffn_w:
        _, n_rows, n_cols = w.shape
        slab = n_rows // n_steps
        assert slab * n_steps == n_rows and slab % (2 * SUBLANES) == 0
        in_specs.append(pl.BlockSpec((None, slab, n_cols), lambda sb, tb: (layer, sb * TIME_TILES + tb, 0)))
        args.append(w)
        out_specs.append(pl.BlockSpec((slab, n_cols), lambda sb, tb: (sb * TIME_TILES + tb, 0)))
        out_shape.append(jax.ShapeDtypeStruct((n_rows, n_cols), BF16))
    res = pl.pallas_call(
        functools.partial(_in_kernel, first_layer=first_layer),
        grid=(ALL_SEQ // TILE_S, TIME_TILES),
        in_specs=in_specs,
        out_specs=out_specs,
        out_shape=out_shape,
        scratch_shapes=[pltpu.VMEM((D_MODEL, IN_W), BF16)],
        compiler_params=pltpu.CompilerParams(dimension_semantics=("arbitrary", "arbitrary"),
                                             vmem_limit_bytes=VMEM_LIMIT),
        name="in_proj",
    )(*args)
    n_w = len(ffn_w)
    return res[0], res[1], (res[2] if first_layer else xs[0]), res[len(res) - n_w:]


def _pair_boundary(b, m, rev):
    c = b.shape[0]
    span = 2 * m
    at = m if rev else m - 1
    if span >= SUBLANES:
        b3 = b.reshape(c // span, span, LANES)
        return jnp.broadcast_to(b3[:, at:at + 1, :], b3.shape).reshape(c, LANES)
    b3 = b.reshape(c // SUBLANES, SUBLANES, LANES)
    sub = lax.broadcasted_iota(jnp.int32, b3.shape, 1)
    out = None
    for p in range(SUBLANES // span):
        piece = jnp.broadcast_to(b3[:, p * span + at:p * span + at + 1, :], b3.shape)
        out = piece if out is None else jnp.where(sub >= p * span, piece, out)
    return out.reshape(c, LANES)


def _neg_abs(x):
    bits = lax.bitcast_convert_type(x, jnp.uint32) | jnp.uint32(0x80000000)
    return lax.bitcast_convert_type(bits, F32)


def _hg_gates(chains, scale):
    outs = []
    for q, fl, lb, tri in chains:
        sig = _sigmoid(fl)
        forget = lb + (1.0 - lb) * sig
        logf = jnp.log2(forget)
        key = (1.0 - lb) * (1.0 - sig)
        hi = logf.astype(BF16)
        r1 = logf - hi.astype(F32)
        mid = r1.astype(BF16)
        lo = (r1 - mid.astype(F32)).astype(BF16)
        parts = _dot(tri, jnp.concatenate([hi, mid, lo], axis=1))
        b2 = parts[:, :LANES] + parts[:, LANES:2 * LANES] + parts[:, 2 * LANES:]
        outs.append((_silu(q) * scale, key, b2, forget))
    return outs


def _hg_scores(chains, code, eye, from_zero=False):
    c = chains[0][0].shape[0]
    o_inter = []
    for qh, key, b2, forget, v, st_ref, rev in chains:
        b_edge = b2[0:1, :] if rev else b2[c - 1:c, :]
        k_end = key * jnp.exp2(b_edge - b2)
        grown = _dot_tn(v.astype(BF16), k_end.astype(BF16))
        if from_zero:
            o_inter.append(None)
            st_ref[...] = grown
        else:
            st = st_ref[...]
            o_inter.append(_dot_nt((qh * jnp.exp2(b2)).astype(BF16), st.astype(BF16)))
            st_ref[...] = jnp.exp2(b_edge) * st + grown

    out = []
    for o, (qh, key, b2, forget, v, st_ref, rev) in zip(o_inter, chains):
        scores = jnp.where(eye, jnp.sum(qh * key, axis=-1, keepdims=True), 0.0)
        qb, kb = qh.astype(BF16), key.astype(BF16)
        for m in HG_LEVELS:
            k = int(math.log2(m)) + 1
            if m == 1:
                p = _dot_nt((qh * forget).astype(BF16), kb)
            elif m >= SUBLANES:
                g = c // (2 * m)
                t_half = slice(0, m) if rev else slice(m, 2 * m)
                s_half = slice(m, 2 * m) if rev else slice(0, m)
                at = m if rev else m - 1
                b3, q3, k3 = (x.reshape(g, 2 * m, LANES) for x in (b2, qh, key))
                seam = b3[:, at:at + 1, :]
                qt = (q3[:, t_half] * jnp.exp2(b3[:, t_half] - seam)).reshape(c // 2, LANES)
                ks = k3[:, s_half] * jnp.exp2(seam - b3[:, s_half])
                kk = jnp.concatenate([k3[:, t_half], ks] if rev else [ks, k3[:, t_half]], axis=1).reshape(c, LANES)
                p = _dot_nt(qt.astype(BF16), kk.astype(BF16)).reshape(g, m, c)
                s3 = scores.reshape(g, 2 * m, c)
                hit = code.reshape(g, 2 * m, c)[:, t_half] == (-k if rev else k)
                st = jnp.where(hit, p, s3[:, t_half])
                scores = jnp.concatenate([st, s3[:, s_half]] if rev else [s3[:, s_half], st], axis=1).reshape(c, c)
                continue
            else:
                e = jnp.exp2(_neg_abs(b2 - _pair_boundary(b2, m, rev))).astype(BF16)
                p = _dot_nt(qb * e, kb * e)
            scores = jnp.where(code == (-k if rev else k), p, scores)
        out.append((o, scores.astype(BF16)))
    return out


def _hgrn_kernel(*refs, layer, n_chunks, n_heads, zero_init):
    refs = list(refs)
    q_ref, ff_ref, fb_ref, v_ref, g_ref, lbl_ref, gain_ref, code_ref, tri_ref = refs[:9]
    s0_ref = None if zero_init else refs[9]
    o_ref, sfin_ref, st_ref, ob_ref, ab_ref, oi_ref, sc_ref = refs[-7:]
    c = HG_CHUNK
    code = code_ref[...]
    eye = code == 0
    chains = [(hd, d) for hd in range(n_heads) for d in (0, 1)]

    def lanes(hd):
        return slice(hd * HG_D, (hd + 1) * HG_D)

    def lower_bound(hd, d):
        lg = lbl_ref[d, :, lanes(hd)]
        ex = jnp.exp(lg - jnp.max(lg, axis=0, keepdims=True))
        soft = ex / jnp.sum(ex, axis=0, keepdims=True)
        return jnp.sum(soft[:layer + 1], axis=0, keepdims=True) - soft[0:1]

    lb = [lower_bound(hd, d) for hd, d in chains]
    scale = HG_D ** -0.5

    def rows_of(d, i):
        n = (n_chunks - 1 - i) if d else i
        return pl.ds(n * c if isinstance(n, int) else pl.multiple_of(n * c, c), c)

    for ch, (hd, d) in enumerate(chains):
        st_ref[ch] = jnp.zeros((HG_D, HG_D), F32) if zero_init else s0_ref[d, hd].T

    def gates(i):
        return _hg_gates([(q_ref[rows_of(d, i), lanes(hd)], (fb_ref if d else ff_ref)[rows_of(d, i), lanes(hd)],
                           lb[ch], tri_ref[d]) for ch, (hd, d) in enumerate(chains)], scale)

    def scores(i, ab, from_zero=False):
        return _hg_scores([ab[ch] + (v_ref[rows_of(d, i), lanes(hd)], st_ref.at[ch], bool(d))
                           for ch, (hd, d) in enumerate(chains)], code, eye, from_zero)

    def emit(i, oi_sc):
        for ch, (hd, d) in enumerate(chains):
            rows = rows_of(d, i)
            o_intra = _dot(oi_sc[ch][1], v_ref[rows, lanes(hd)].astype(BF16))
            (ob_ref if d else o_ref)[rows, lanes(hd)] = o_intra if oi_sc[ch][0] is None else oi_sc[ch][0] + o_intra

    if n_chunks <= 2:
        ab = [gates(i) for i in range(n_chunks)]
        for i in range(n_chunks):
            emit(i, scores(i, ab[i], from_zero=zero_init and i == 0))
    else:
        n_ab = ab_ref.shape[1]

        def put_ab(ab):
            for ch in range(len(chains)):
                for k in range(n_ab):
                    ab_ref[ch, k] = ab[ch][k]

        def get_oi_sc():
            return [(oi_ref[ch], sc_ref[ch]) for ch in range(len(chains))]

        put_ab(gates(0))
        oi_ref[...] = jnp.zeros(oi_ref.shape, F32)
        sc_ref[...] = jnp.zeros(sc_ref.shape, BF16)

        def body(i, carry):
            emit(jnp.maximum(i - 1, 0), get_oi_sc())
            res = scores(i, [tuple(ab_ref[ch, k] for k in range(n_ab)) for ch in range(len(chains))])
            for ch in range(len(chains)):
                oi_ref[ch] = res[ch][0]
                sc_ref[ch] = res[ch][1]
            put_ab(gates(jnp.minimum(i + 1, n_chunks - 1)))
            return carry

        lax.fori_loop(0, n_chunks, body, 0)
        emit(n_chunks - 1, get_oi_sc())
    if len(sfin_ref.shape) == 5:
        for other in range(sfin_ref.shape[0]):
            if other != layer:
                sfin_ref[other] = jnp.zeros(sfin_ref.shape[1:], F32)
        sfin_ref = sfin_ref.at[layer]
    for ch, (hd, d) in enumerate(chains):
        sfin_ref[d, hd] = st_ref[ch].T

    def finish(n, carry):
        rows = pl.ds(pl.multiple_of(n * (2 * c), 2 * c), 2 * c)
        for hd in range(n_heads):
            o_ref[rows, lanes(hd)] = (_rms(o_ref[rows, lanes(hd)] + ob_ref[rows, lanes(hd)]) * gain_ref[...]
                                      * _silu(g_ref[rows, lanes(hd)]))
        return carry

    lax.fori_loop(0, n_chunks // 2, finish, 0)


def _hgrn_call(proj, row0, lb_logits, gain, state, finals, layer, n_seq, seq_len):
    rows = n_seq * seq_len
    seq0 = row0 // seq_len
    zero_init = state is None
    n_chunks = seq_len // HG_CHUNK
    assert n_chunks % 2 == 0
    nh = HG_HEADS if n_chunks <= 2 else 2
    hw = nh * HG_D
    n_hb = HG_HEADS // nh

    def col_spec(k):
        return pl.BlockSpec((seq_len, hw), lambda b, h: (seq0 + b, k * n_hb + h))

    t, s = np.meshgrid(np.arange(HG_CHUNK), np.arange(HG_CHUNK), indexing="ij")
    lvl = np.where(t == s, 0, np.floor(np.log2(np.maximum(t ^ s, 1))).astype(np.int32) + 1)
    code = jnp.asarray(np.where(t > s, lvl, -lvl), jnp.int32)
    tri = jnp.asarray(np.stack([s <= t, s >= t]), BF16)

    in_specs = [col_spec(0), col_spec(1), col_spec(2), col_spec(3), col_spec(4),
                pl.BlockSpec((2, DEPTH, hw), lambda b, h: (0, 0, h)),
                pl.BlockSpec((1, HG_D), lambda b, h: (0, 0)),
                pl.BlockSpec((HG_CHUNK, HG_CHUNK), lambda b, h: (0, 0)),
                pl.BlockSpec((2, HG_CHUNK, HG_CHUNK), lambda b, h: (0, 0, 0))]
    args = [proj] * 5 + [lb_logits, gain, code, tri]
    if not zero_init:
        in_specs.append(pl.BlockSpec((None, None, 2, nh, HG_D, HG_D), lambda b, h: (b, layer, 0, h, 0, 0)))
        args.append(state)
    aliases = {}
    if finals is None:
        fin_spec = pl.BlockSpec((None, DEPTH, 2, nh, HG_D, HG_D), lambda b, h: (b, 0, 0, h, 0, 0))
    else:
        fin_spec = pl.BlockSpec((None, None, 2, nh, HG_D, HG_D), lambda b, h: (b, layer, 0, h, 0, 0))
        aliases[len(args)] = 1
        in_specs.append(pl.BlockSpec(memory_space=pl.ANY))
        args.append(finals)
    n_ch = 2 * nh
    return pl.pallas_call(
        functools.partial(_hgrn_kernel, layer=layer, n_chunks=n_chunks, n_heads=nh, zero_init=zero_init),
        grid=(n_seq, n_hb),
        in_specs=in_specs,
        out_specs=[pl.BlockSpec((seq_len, hw), lambda b, h: (b, h)), fin_spec],
        out_shape=[jax.ShapeDtypeStruct((rows, HG_W), F32),
                   jax.ShapeDtypeStruct((n_seq, DEPTH, 2, HG_HEADS, HG_D, HG_D), F32)],
        input_output_aliases=aliases,
        scratch_shapes=[pltpu.VMEM((n_ch, HG_D, HG_D), F32), pltpu.VMEM((seq_len, hw), F32),
                        pltpu.VMEM((n_ch, 4, HG_CHUNK, HG_D), F32), pltpu.VMEM((n_ch, HG_CHUNK, HG_D), F32),
                        pltpu.VMEM((n_ch, HG_CHUNK, HG_CHUNK), BF16)],
        compiler_params=pltpu.CompilerParams(dimension_semantics=("parallel", "parallel"),
                                             vmem_limit_bytes=VMEM_LIMIT),
        name="hgrn2_mixer",
    )(*args)


def _s5_params_kernel(lr_ref, li_ref, ldt_ref, btr_ref, bti_ref, cr_ref, ci_ref, a_ref, bm_ref, cm_ref):
    for k in range(lr_ref.shape[0]):
        _s5_params_block(*(r.at[k] for r in (lr_ref, li_ref, ldt_ref, btr_ref, bti_ref, cr_ref, ci_ref,
                                             a_ref, bm_ref, cm_ref)))


def _s5_params_block(lr_ref, li_ref, ldt_ref, btr_ref, bti_ref, cr_ref, ci_ref, a_ref, bm_ref, cm_ref):
    sw = S5_SW
    lr = jnp.minimum(lr_ref[...], -1e-4)
    li = li_ref[...]
    dt = jnp.exp(ldt_ref[...])
    mag = jnp.exp(lr * dt)
    ab_re = mag * jnp.cos(li * dt)
    ab_im = mag * jnp.sin(li * dt)
    nr = ab_re - 1.0
    den = lr * lr + li * li
    z_re = (nr * lr + ab_im * li) / den
    z_im = (ab_im * lr - nr * li) / den

    p_idx = lax.broadcasted_iota(jnp.int32, (S5_P, sw), 0)
    col = lax.broadcasted_iota(jnp.int32, (S5_P, sw), 1)
    for g in range(S5_GB):
        place = (col == p_idx + g * S5_P).astype(BF16)
        zr, zi = z_re[g:g + 1, :], z_im[g:g + 1, :]
        btr, bti = btr_ref[g], bti_ref[g]
        rows = slice(g * S5_CH, (g + 1) * S5_CH)
        bm_ref[rows, :sw] = _dot((zr * btr - zi * bti).astype(BF16), place).astype(BF16)
        bm_ref[rows, sw:] = _dot((zr * bti + zi * btr).astype(BF16), place).astype(BF16)
        cm_ref[rows, :sw] = _dot(cr_ref[g].astype(BF16), place).astype(BF16)
        cm_ref[rows, sw:] = _dot((-ci_ref[g]).astype(BF16), place).astype(BF16)
        a_ref[:, g * S5_P:(g + 1) * S5_P] = jnp.broadcast_to(ab_re[g:g + 1, :], (S5_NSEQ, S5_P))
        a_ref[:, sw + g * S5_P:sw + (g + 1) * S5_P] = jnp.broadcast_to(ab_im[g:g + 1, :], (S5_NSEQ, S5_P))


def _s5_params(lam_re, lam_im, log_dt, b_re, b_im, c_re, c_im):
    nb = DEPTH * 2 * S5_NGB
    gp = (nb, S5_GB, S5_P)
    gcp = (nb, S5_GB, S5_CH, S5_P)
    bt_re = jnp.swapaxes(b_re, -1, -2).reshape(gcp)
    bt_im = jnp.swapaxes(b_im, -1, -2).reshape(gcp)
    ldt = jnp.broadcast_to(log_dt.reshape(nb, S5_GB, 1), gp)
    per = S5_NGB
    gp_spec = pl.BlockSpec((per, S5_GB, S5_P), lambda i: (i, 0, 0))
    gcp_spec = pl.BlockSpec((per, S5_GB, S5_CH, S5_P), lambda i: (i, 0, 0, 0))
    a, bmat, cmat = pl.pallas_call(
        _s5_params_kernel,
        grid=(nb // per,),
        in_specs=[gp_spec] * 3 + [gcp_spec] * 4,
        out_specs=[pl.BlockSpec((per, S5_NSEQ, 2 * S5_SW), lambda i: (i, 0, 0)),
                   pl.BlockSpec((per, LANES, 2 * S5_SW), lambda i: (i, 0, 0)),
                   pl.BlockSpec((per, LANES, 2 * S5_SW), lambda i: (i, 0, 0))],
        out_shape=[jax.ShapeDtypeStruct((nb, S5_NSEQ, 2 * S5_SW), F32),
                   jax.ShapeDtypeStruct((nb, LANES, 2 * S5_SW), BF16),
                   jax.ShapeDtypeStruct((nb, LANES, 2 * S5_SW), BF16)],
        compiler_params=pltpu.CompilerParams(dimension_semantics=("parallel",)),
        name="s5_params",
    )(lam_re.reshape(gp), lam_im.reshape(gp), ldt, bt_re, bt_im, c_re.reshape(gcp), c_im.reshape(gcp))
    lead = (DEPTH, 2, S5_NGB)
    return (a.reshape(lead + a.shape[1:]), bmat.reshape(lead + bmat.shape[1:]), cmat.reshape(lead + cmat.shape[1:]))


def _s5_kernel(*refs, chained):
    refs = list(refs)
    u_ref, bm_ref, cm_ref, a_ref, d_ref = refs[:5]
    rest = refs[5:]
    s0_ref = rest.pop(0) if chained else None
    y_ref = rest.pop(0)
    hfin_ref = None if chained else rest.pop(0)
    hbuf_e, hbuf_o, hb0, hb1, hst = rest
    hbufs, hb16s = (hbuf_e, hbuf_o), (hb0, hb1)
    ns, sw = S5_NSEQ, S5_SW
    half = ns // 2
    n_tc = S5_SEQ // S5_TC
    blk = S5_TC * ns
    dirs = (0, 1)

    for d in dirs:
        hst[d] = jnp.zeros((ns, 2 * sw), F32)

    def steps_of(d, i):
        return pl.ds(((n_tc - 1 - i) if d else i) * S5_TC, S5_TC)

    def kept(i):
        return i // 2 if chained else 0

    def project(i, par):
        for d in dirs:
            u = u_ref[steps_of(d, i)].reshape(blk, LANES)
            hbufs[par][d, kept(i)] = _dot(u.astype(BF16), bm_ref[d])

    def scan(i, par, emit):
        a = [(a_ref[d, :half, :sw], a_ref[d, :half, sw:]) for d in dirs]
        h = [[(hst[d, k * half:(k + 1) * half, :sw], hst[d, k * half:(k + 1) * half, sw:]) for k in range(2)]
             for d in dirs]
        for jj in range(S5_TC):
            for d in dirs:
                j = S5_TC - 1 - jj if d else jj
                ar, ai = a[d]
                bu = hbufs[par].at[d, kept(i)]
                for k in range(2):
                    r = slice(j * ns + k * half, j * ns + (k + 1) * half)
                    hr, hi = h[d][k]
                    h[d][k] = (ar * hr - ai * hi + bu[r, :sw], ar * hi + ai * hr + bu[r, sw:])
                if emit:
                    r = slice(j * ns, (j + 1) * ns)
                    hb16s[par][d, r, :sw] = jnp.concatenate([h[d][0][0], h[d][1][0]], axis=0).astype(BF16)
                    hb16s[par][d, r, sw:] = jnp.concatenate([h[d][0][1], h[d][1][1]], axis=0).astype(BF16)
        for d in dirs:
            for k in range(2):
                hst[d, k * half:(k + 1) * half, :sw] = h[d][k][0]
                hst[d, k * half:(k + 1) * half, sw:] = h[d][k][1]

    def readout(i, par):
        for d in dirs:
            steps = steps_of(d, i)
            y = _dot_nt(hb16s[par][d], cm_ref[d])
            y_ref[steps] = y_ref[steps] + y.reshape(S5_TC, ns, LANES)

    def skip(n, carry):
        steps = pl.ds(n * S5_TC, S5_TC)
        y_ref[steps] = d_ref[...] * u_ref[steps]
        return carry

    project(0, 0)
    if chained:
        def sweep(k, carry):
            i = 2 * k
            scan(i, 0, False)
            project(i + 1, 1)
            scan(i + 1, 1, False)
            project(i + 2, 0)
            return carry

        lax.fori_loop(0, n_tc // 2 - 1, sweep, 0)
        scan(n_tc - 2, 0, False)
        project(n_tc - 1, 1)
        scan(n_tc - 1, 1, False)

        n_long = s0_ref.shape[1]
        pieces = ns // n_long
        for d in dirs:
            pr, pi = a_ref[d, 0:1, :sw], a_ref[d, 0:1, sw:]
            for _ in range(int(math.log2(S5_SEQ))):
                pr, pi = pr * pr - pi * pi, 2.0 * (pr * pi)
            for b in range(n_long):
                hr, hi = s0_ref[d, b:b + 1, :sw], s0_ref[d, b:b + 1, sw:]
                for k in (range(pieces - 1, -1, -1) if d else range(pieces)):
                    r = b * pieces + k
                    zr, zi = hst[d, r:r + 1, :sw], hst[d, r:r + 1, sw:]
                    hst[d, r:r + 1, :sw] = hr
                    hst[d, r:r + 1, sw:] = hi
                    hr, hi = pr * hr - pi * hi + zr, pr * hi + pi * hr + zi

    hb1[...] = jnp.zeros(hb1.shape, BF16)
    lax.fori_loop(0, n_tc, skip, 0)

    def body(k, carry):
        i = 2 * k
        scan(i, 0, True)
        readout(jnp.maximum(i - 1, 0), 1)
        if not chained:
            project(i + 1, 1)
        scan(i + 1, 1, True)
        readout(i, 0)
        if not chained:
            project(jnp.minimum(i + 2, n_tc - 1), 0)
        return carry

    lax.fori_loop(0, n_tc // 2, body, 0)
    if not chained:
        for d in dirs:
            hfin_ref[d] = hst[d]
    readout(n_tc - 1, 1)


def _s5_call(u_tm, part, a, bmat, cmat, dskip, s0, layer):
    chained = s0 is not None
    tm_spec = pl.BlockSpec((S5_SEQ, S5_NSEQ, LANES), lambda g: (0, 0, g))

    def mat_spec(rows):
        return pl.BlockSpec((None, 2, None, rows, 2 * S5_SW), lambda g: (layer, 0, g, 0, 0))

    in_specs = [pl.BlockSpec((S5_SEQ, S5_NSEQ, LANES), lambda g: (0, part, g)),
                mat_spec(LANES), mat_spec(LANES), mat_spec(S5_NSEQ),
                pl.BlockSpec((None, None, 1, LANES), lambda g: (layer, g, 0, 0))]
    args = [u_tm, bmat, cmat, a, dskip]
    out_specs = [tm_spec]
    out_shape = [jax.ShapeDtypeStruct((S5_SEQ, S5_NSEQ, S5_W), F32)]
    if chained:
        in_specs.append(pl.BlockSpec((2, None, s0.shape[2], 2 * S5_SW), lambda g: (0, g, 0, 0)))
        args.append(s0)
    else:
        out_specs.append(pl.BlockSpec((2, None, S5_NSEQ, 2 * S5_SW), lambda g: (0, g, 0, 0)))
        out_shape.append(jax.ShapeDtypeStruct((2, S5_NGB, S5_NSEQ, 2 * S5_SW), F32))
    blk = S5_TC * S5_NSEQ
    kept = S5_SEQ // S5_TC // 2 if chained else 1
    res = pl.pallas_call(
        functools.partial(_s5_kernel, chained=chained),
        grid=(S5_NGB,),
        in_specs=in_specs,
        out_specs=out_specs,
        out_shape=out_shape,
        scratch_shapes=[pltpu.VMEM((2, kept, blk, 2 * S5_SW), F32),
                        pltpu.VMEM((2, kept, blk, 2 * S5_SW), F32),
                        pltpu.VMEM((2, blk, 2 * S5_SW), BF16),
                        pltpu.VMEM((2, blk, 2 * S5_SW), BF16),
                        pltpu.VMEM((2, S5_NSEQ, 2 * S5_SW), F32)],
        compiler_params=pltpu.CompilerParams(dimension_semantics=("parallel",),
                                             vmem_limit_bytes=VMEM_LIMIT),
        name="s5_scan",
    )(*args)
    return (res[0], None) if chained else (res[0], res[1])


def _out_kernel(x_ref, ohc_ref, ohs_ref, y5c_ref, y5s_ref, g1_ref, sh2_ref, sc2_ref, g2_ref, nffn_ref, nfin_ref,
                wglu_ref, wout_ref, wg_ref, wu_ref, wd_ref, *rest, last_layer):
    if last_layer:
        oc_ref, os_ref, wglu_b, wout_b = rest
    else:
        o_ref, wglu_b, wout_b = rest

    @pl.when(_first_step())
    def _():
        _cast_rows(wglu_ref, wglu_b)
        _cast_rows(wout_ref, wout_b)

    smp = _is_sample_tile()
    y = jnp.concatenate([jnp.where(smp, y5s_ref[:, s, :], y5c_ref[:, s, :]) for s in range(TILE_S)],
                        axis=0)
    y = _gelu_tanh(y)
    y = y * _sigmoid(_dot(y.astype(BF16), wglu_b[...]))
    ohg = jnp.where(smp, ohs_ref[...], ohc_ref[...]).reshape(TILE_ROWS, HG_W)
    mix = _dot(ohg.astype(BF16), wout_b[:HG_W, :]) + _dot(y.astype(BF16), wout_b[HG_W:, :])
    x = x_ref[...].reshape(TILE_ROWS, D_MODEL) + g1_ref[...] * mix
    h = _rms(x) * nffn_ref[...]
    h = (h * (1.0 + sc2_ref[...]) + sh2_ref[...]).astype(BF16)
    act = (_silu(_dot(h, wg_ref[...])) * _dot(h, wu_ref[...])).astype(BF16)
    x = x + g2_ref[...] * _dot(act, wd_ref[...])
    if not last_layer:
        o_ref[...] = x.reshape(o_ref.shape)
    else:
        x = (_rms(x) * nfin_ref[...]).reshape(oc_ref.shape)

        @pl.when(smp)
        def _():
            os_ref[...] = x

        @pl.when(jnp.logical_not(smp))
        def _():
            oc_ref[...] = x


def _out_call(x3, ohg_c, ohg_s, y5_c, y5_s, mod4, nffn, nfin, wglu, wout, wg, wu, wd, layer, last_layer):
    vec = pl.BlockSpec((1, D_MODEL), lambda sb, tb: (0, 0))
    part_shape = jax.ShapeDtypeStruct((S5_NSEQ, S5_SEQ, D_MODEL), F32)
    if last_layer:
        out_specs = [_part_tile_spec(D_MODEL, _ctx_index), _part_tile_spec(D_MODEL, _smp_index)]
        out_shape = [part_shape, part_shape]
    else:
        out_specs = _tile_spec(D_MODEL)
        out_shape = jax.ShapeDtypeStruct(x3.shape, F32)
    return pl.pallas_call(
        functools.partial(_out_kernel, last_layer=last_layer),
        grid=(ALL_SEQ // TILE_S, TIME_TILES),
        in_specs=[_tile_spec(D_MODEL),
                  _part_tile_spec(HG_W, _ctx_index), _part_tile_spec(HG_W, _smp_index),
                  _part_tm_tile_spec(S5_W, _ctx_index), _part_tm_tile_spec(S5_W, _smp_index),
                  _mod_spec(layer, 2), _mod_spec(layer, 3), _mod_spec(layer, 4), _mod_spec(layer, 5),
                  vec, vec,
                  _layer_spec((S5_W, S5_W), layer), _layer_spec((D_MODEL, D_MODEL), layer),
                  _whole_spec((D_MODEL, D_FF)), _whole_spec((D_MODEL, D_FF)), _whole_spec((D_FF, D_MODEL))],
        out_specs=out_specs,
        out_shape=out_shape,
        scratch_shapes=[pltpu.VMEM((S5_W, S5_W), BF16), pltpu.VMEM((D_MODEL, D_MODEL), BF16)],
        compiler_params=pltpu.CompilerParams(dimension_semantics=("arbitrary", "arbitrary"),
                                             vmem_limit_bytes=VMEM_LIMIT),
        name="out_ffn",
    )(x3, ohg_c, ohg_s, y5_c, y5_s, mod4, mod4, mod4, mod4, nffn, nfin, wglu, wout, wg, wu, wd)


def _s5_state_to_blocks(s):
    n = s.shape[0]
    s = s.reshape(n, 2, S5_NGB, S5_GB, S5_P, 2)
    return jnp.transpose(s, (1, 2, 0, 5, 3, 4)).reshape(2, S5_NGB, n, 2 * S5_SW)


def _s5_blocks_to_state(h):
    n = h.shape[2]
    h = h.reshape(2, S5_NGB, n, 2, S5_GB, S5_P)
    return jnp.transpose(h, (2, 0, 1, 4, 5, 3)).reshape(n, 2, S5_GROUPS, S5_P, 2)


def kernel(x_prompt, x_sample, state_hgrn, state_s5, c, c_ctx, w_mod, b_mod, norm_mix, norm_ffn, norm_final, w_in, w_out, hg_lb_logits, hg_norm, s5_lam_re, s5_lam_im, s5_log_dt, s5_b_re, s5_b_im, s5_c_re, s5_c_im, s5_d, s5_w_glu, w_gate, w_up, w_down):
    n_ctx, ctx_len, _ = x_prompt.shape
    n_dec, dec_len, _ = x_sample.shape
    assert ctx_len == S5_SEQ and n_ctx == S5_NSEQ and n_dec * dec_len == S5_NSEQ * S5_SEQ

    cond = jnp.concatenate([c_ctx[None, :], c, jnp.zeros((SUBLANES - 1 - n_dec, D_MODEL), F32)], axis=0)
    mod4 = _mod_call(cond, w_mod, b_mod).reshape(DEPTH, SUBLANES, 1, 6 * D_MODEL)

    s5_a, s5_bmat, s5_cmat = _s5_params(s5_lam_re, s5_lam_im, s5_log_dt, s5_b_re, s5_b_im, s5_c_re, s5_c_im)
    s5_dskip = s5_d.reshape(DEPTH, S5_NGB, 1, LANES)
    nfin = norm_final.reshape(1, D_MODEL)

    assert dec_len // S5_SEQ == TILE_S and n_dec + 1 <= SUBLANES
    tok = (S5_NSEQ, S5_SEQ, D_MODEL)
    ctx_rows = S5_NSEQ * S5_SEQ
    xs = (x_prompt.reshape(tok), x_sample.reshape(tok))
    ctx_fin, smp_fin, s5_finals = None, None, []
    for l in range(DEPTH):
        proj3, u_tm, x_all, (w_gate_b, w_up_b, w_down_b) = _in_call(
            xs, norm_mix[l].reshape(1, D_MODEL), mod4, w_in, (w_gate, w_up, w_down), l)
        proj = proj3.reshape(ALL_SEQ * S5_SEQ, HG_IN_W)
        gain = hg_norm[l].reshape(1, HG_D)
        ohg_c, ctx_fin = _hgrn_call(proj, 0, hg_lb_logits, gain, None, ctx_fin, l, n_ctx, ctx_len)
        ohg_s, smp_fin = _hgrn_call(proj, ctx_rows, hg_lb_logits, gain, state_hgrn, smp_fin, l, n_dec, dec_len)
        y5_c, s5_fin = _s5_call(u_tm, 0, s5_a, s5_bmat, s5_cmat, s5_dskip, None, l)
        y5_s, _ = _s5_call(u_tm, 1, s5_a, s5_bmat, s5_cmat, s5_dskip, _s5_state_to_blocks(state_s5[:, l]), l)
        last = l == DEPTH - 1
        res = _out_call(x_all, ohg_c.reshape(S5_NSEQ, S5_SEQ, HG_W), ohg_s.reshape(S5_NSEQ, S5_SEQ, HG_W),
                        y5_c, y5_s, mod4, norm_ffn[l].reshape(1, D_MODEL), nfin,
                        s5_w_glu, w_out, w_gate_b, w_up_b, w_down_b, l, last)
        xs = res if last else (res,)
        s5_finals.append(_s5_blocks_to_state(s5_fin))
    y_prompt, y_sample = xs
    return (y_prompt.reshape(x_prompt.shape), y_sample.reshape(x_sample.shape),
            ctx_fin, jnp.stack(s5_finals, axis=1))
```

```python
import functools
import math

import jax
import jax.numpy as jnp
import numpy as np
from jax import lax
from jax.experimental import pallas as pl
from jax.experimental.pallas import tpu as pltpu

F32 = jnp.float32
BF16 = jnp.bfloat16

LANES = 128
SUBLANES = 8

D_MODEL = 1024
DEPTH = 2
GRID_W = 64
HG_W = 512
HG_HEADS = 4
HG_D = HG_W // HG_HEADS
S5_W = 512
S5_CH = 16
S5_GROUPS = S5_W // S5_CH
S5_P = 64
S5_GB = LANES // S5_CH
S5_NGB = S5_GROUPS // S5_GB
S5_SW = S5_GB * S5_P
HG_IN_W = 5 * HG_W
IN_W = HG_IN_W + S5_W
D_FF = 2816
EPS = 1e-6

HG_CHUNK = 128
HG_LEVELS = (64, 32, 16, 8, 4, 2, 1)
S5_SEQ = 256
S5_NSEQ = 16
S5_TC = 16

TILE_S = SUBLANES
TILE_T = 64
TILE_ROWS = TILE_S * TILE_T
TIME_TILES = S5_SEQ // TILE_T
ALL_SEQ = 2 * S5_NSEQ
CTX_TILES = S5_NSEQ // TILE_S
CAST_ROWS = 128
MOD_TILE_N = 1536
VMEM_LIMIT = 56 * 1024 * 1024


def _sigmoid(x):
    return 1.0 / (1.0 + jnp.exp(-x))


def _silu(x):
    return x * _sigmoid(x)


def _gelu_tanh(x):
    return 0.5 * x * (1.0 + jnp.tanh(math.sqrt(2.0 / math.pi) * (x + 0.044715 * (x * x * x))))


def _rms(x):
    return x * lax.rsqrt(jnp.mean(x * x, axis=-1, keepdims=True) + EPS)


def _dot(a, b):
    return jnp.dot(a, b, preferred_element_type=F32)


def _dot_nt(a, b):
    return lax.dot_general(a, b, (((1,), (1,)), ((), ())), preferred_element_type=F32)


def _dot_tn(a, b):
    return lax.dot_general(a, b, (((0,), (0,)), ((), ())), preferred_element_type=F32)


def _whole_spec(shape):
    return pl.BlockSpec(tuple(shape), lambda *_: (0,) * len(shape), pipeline_mode=pl.Buffered(1))


def _layer_spec(shape, layer):
    nd = len(shape)
    return pl.BlockSpec((None,) + tuple(shape), lambda *_: (layer,) + (0,) * nd, pipeline_mode=pl.Buffered(1))


def _mod_kernel(cond_ref, w_ref, b_ref, o_ref):
    a = _silu(cond_ref[...]).astype(BF16)
    o_ref[0] = _dot(a, w_ref[0].astype(BF16)) + b_ref[0]


def _mod_call(cond, w_mod, b_mod):
    n_cond = cond.shape[0]
    n_out = w_mod.shape[-1]
    return pl.pallas_call(
        _mod_kernel,
        grid=(DEPTH, n_out // MOD_TILE_N),
        in_specs=[
            pl.BlockSpec((n_cond, D_MODEL), lambda l, j: (0, 0)),
            pl.BlockSpec((1, D_MODEL, MOD_TILE_N), lambda l, j: (l, 0, j)),
            pl.BlockSpec((1, 1, MOD_TILE_N), lambda l, j: (l, 0, j)),
        ],
        out_specs=pl.BlockSpec((1, n_cond, MOD_TILE_N), lambda l, j: (l, 0, j)),
        out_shape=jax.ShapeDtypeStruct((DEPTH, n_cond, n_out), F32),
        compiler_params=pltpu.CompilerParams(dimension_semantics=("parallel", "parallel"),
                                             vmem_limit_bytes=VMEM_LIMIT),
        name="adaln_mod",
    )(cond, w_mod, b_mod.reshape(DEPTH, 1, n_out))


def _first_step():
    return jnp.logical_and(pl.program_id(0) == 0, pl.program_id(1) == 0)


def _cast_rows(src_ref, dst_ref):
    for r in range(0, src_ref.shape[0], CAST_ROWS):
        dst_ref[r:r + CAST_ROWS, :] = src_ref[r:r + CAST_ROWS, :].astype(BF16)


def _grid_pos_tile(omega, tb):
    nf = omega.shape[-1]
    s_idx = lax.broadcasted_iota(jnp.int32, (TILE_S, nf), 0)
    j_idx = lax.broadcasted_iota(jnp.int32, (TILE_T, nf), 0)
    t0 = tb * TILE_T
    row = (s_idx * (S5_SEQ // GRID_W) + t0 // GRID_W).astype(F32) * omega
    col = (j_idx + t0 % GRID_W).astype(F32) * omega
    enc_r = jnp.concatenate([jnp.sin(row), jnp.cos(row)], axis=-1)
    enc_c = jnp.concatenate([jnp.sin(col), jnp.cos(col)], axis=-1)
    shape = (TILE_S, TILE_T, 2 * nf)
    return jnp.concatenate([jnp.broadcast_to(enc_r[:, None, :], shape),
                            jnp.broadcast_to(enc_c[None, :, :], shape)], axis=-1)


def _is_sample_tile():
    return pl.program_id(0) >= CTX_TILES


def _stream_tile(refs):
    if len(refs) == 1:
        return refs[0][...]
    xc_ref, xs_ref, om_ref = refs
    return jnp.where(_is_sample_tile(), xs_ref[...] + _grid_pos_tile(om_ref[...], pl.program_id(1)), xc_ref[...])


def _stream_specs(xs):
    if len(xs) == 1:
        return [_tile_spec(D_MODEL)], list(xs)
    assert GRID_W % TILE_T == 0 and S5_SEQ % GRID_W == 0
    nf = D_MODEL // 4
    omega = 1.0 / (np.float32(10000.0) ** (np.arange(nf, dtype=np.float32) / np.float32(nf)))
    specs = [_part_tile_spec(D_MODEL, _ctx_index), _part_tile_spec(D_MODEL, _smp_index),
             pl.BlockSpec((1, nf), lambda sb, tb: (0, 0))]
    return specs, list(xs) + [jnp.asarray(omega.reshape(1, nf), F32)]


def _in_kernel(*refs, first_layer):
    n_in = 10 if first_layer else 8
    for src, dst in zip(refs[n_in - 3:n_in], refs[-4:-1]):
        dst[...] = src[...].astype(BF16)
    refs = refs[:n_in - 3] + refs[n_in:-4] + refs[-1:]

    n_x = 3 if first_layer else 1
    x = _stream_tile(refs[:n_x])
    gain_ref, sh_ref, sc_ref, w_ref, proj_ref, u_ref, wb_ref = refs[n_x:]

    @pl.when(_first_step())
    def _():
        _cast_rows(w_ref, wb_ref)

    x = x.reshape(TILE_ROWS, D_MODEL)
    h = _rms(x) * gain_ref[...]
    h = (h * (1.0 + sc_ref[...]) + sh_ref[...]).astype(BF16)
    u = _dot(h, wb_ref[:, HG_IN_W:])
    for s in range(TILE_S):
        u_ref[:, s, :] = u[s * TILE_T:(s + 1) * TILE_T, :]
    proj_ref[...] = _dot(h, wb_ref[:, :HG_IN_W]).reshape(proj_ref.shape)


def _tile_spec(width):
    return pl.BlockSpec((TILE_S, TILE_T, width), lambda sb, tb: (sb, tb, 0))


def _tm_tile_spec(width):
    return pl.BlockSpec((TILE_T, TILE_S, width), lambda sb, tb: (tb, sb, 0))


def _ctx_index(sb, tb):
    on = sb < CTX_TILES
    return jnp.where(on, sb, CTX_TILES - 1), jnp.where(on, tb, TIME_TILES - 1)


def _smp_index(sb, tb):
    on = sb >= CTX_TILES
    return jnp.where(on, sb - CTX_TILES, 0), jnp.where(on, tb, 0)


def _part_tile_spec(width, index):
    return pl.BlockSpec((TILE_S, TILE_T, width), lambda sb, tb: index(sb, tb) + (0,))


def _part_tm_tile_spec(width, index):
    return pl.BlockSpec((TILE_T, TILE_S, width), lambda sb, tb: index(sb, tb)[::-1] + (0,))


def _mod_spec(layer, col):
    return pl.BlockSpec((None, None, 1, D_MODEL),
                        lambda sb, tb: (layer, jnp.maximum(sb - (CTX_TILES - 1), 0), 0, col))


def _in_call(xs, gain, mod4, w_in, ffn_w, layer):
    first_layer = len(xs) == 2
    n_steps = (ALL_SEQ // TILE_S) * TIME_TILES
    in_specs, args = _stream_specs(xs)
    in_specs += [
        pl.BlockSpec((1, D_MODEL), lambda sb, tb: (0, 0)),
        _mod_spec(layer, 0),
        _mod_spec(layer, 1),
        _layer_spec((D_MODEL, IN_W), layer),
    ]
    args += [gain, mod4, mod4, w_in]
    out_specs = [_tile_spec(HG_IN_W), _tm_tile_spec(S5_W)]
    out_shape = [jax.ShapeDtypeStruct((ALL_SEQ, S5_SEQ, HG_IN_W), F32),
                 jax.ShapeDtypeStruct((S5_SEQ, ALL_SEQ, S5_W), F32)]
    for w in ffn_w:
        _, n_rows, n_cols = w.shape
        slab = n_rows // n_steps
        assert slab * n_steps == n_rows and slab % (2 * SUBLANES) == 0
        in_specs.append(pl.BlockSpec((None, slab, n_cols), lambda sb, tb: (layer, sb * TIME_TILES + tb, 0)))
        args.append(w)
        out_specs.append(pl.BlockSpec((slab, n_cols), lambda sb, tb: (sb * TIME_TILES + tb, 0)))
        out_shape.append(jax.ShapeDtypeStruct((n_rows, n_cols), BF16))
    res = pl.pallas_call(
        functools.partial(_in_kernel, first_layer=first_layer),
        grid=(ALL_SEQ // TILE_S, TIME_TILES),
        in_specs=in_specs,
        out_specs=out_specs,
        out_shape=out_shape,
        scratch_shapes=[pltpu.VMEM((D_MODEL, IN_W), BF16)],
        compiler_params=pltpu.CompilerParams(dimension_semantics=("arbitrary", "arbitrary"),
                                             vmem_limit_bytes=VMEM_LIMIT),
        name="in_proj",
    )(*args)
    return res[0], res[1], res[2:]


def _pair_boundary(b, m, rev):
    c = b.shape[0]
    span = 2 * m
    at = m if rev else m - 1
    if span >= SUBLANES:
        b3 = b.reshape(c // span, span, LANES)
        return jnp.broadcast_to(b3[:, at:at + 1, :], b3.shape).reshape(c, LANES)
    b3 = b.reshape(c // SUBLANES, SUBLANES, LANES)
    sub = lax.broadcasted_iota(jnp.int32, b3.shape, 1)
    out = None
    for p in range(SUBLANES // span):
        piece = jnp.broadcast_to(b3[:, p * span + at:p * span + at + 1, :], b3.shape)
        out = piece if out is None else jnp.where(sub >= p * span, piece, out)
    return out.reshape(c, LANES)


def _neg_abs(x):
    bits = lax.bitcast_convert_type(x, jnp.uint32) | jnp.uint32(0x80000000)
    return lax.bitcast_convert_type(bits, F32)


def _hg_gates(chains, scale):
    outs = []
    for q, fl, lb, tri in chains:
        sig = _sigmoid(fl)
        forget = lb + (1.0 - lb) * sig
        logf = jnp.log2(forget)
        key = (1.0 - lb) * (1.0 - sig)
        hi = logf.astype(BF16)
        r1 = logf - hi.astype(F32)
        mid = r1.astype(BF16)
        lo = (r1 - mid.astype(F32)).astype(BF16)
        parts = _dot(tri, jnp.concatenate([hi, mid, lo], axis=1))
        b2 = parts[:, :LANES] + parts[:, LANES:2 * LANES] + parts[:, 2 * LANES:]
        outs.append((_silu(q) * scale, key, b2, forget))
    return outs


def _hg_scores(chains, code, eye, from_zero=False):
    c = chains[0][0].shape[0]
    o_inter = []
    for qh, key, b2, forget, v, st_ref, rev in chains:
        b_edge = b2[0:1, :] if rev else b2[c - 1:c, :]
        k_end = key * jnp.exp2(b_edge - b2)
        grown = _dot_tn(v.astype(BF16), k_end.astype(BF16))
        if from_zero:
            o_inter.append(None)
            st_ref[...] = grown
        else:
            st = st_ref[...]
            o_inter.append(_dot_nt((qh * jnp.exp2(b2)).astype(BF16), st.astype(BF16)))
            st_ref[...] = jnp.exp2(b_edge) * st + grown

    out = []
    for o, (qh, key, b2, forget, v, st_ref, rev) in zip(o_inter, chains):
        scores = jnp.where(eye, jnp.sum(qh * key, axis=-1, keepdims=True), 0.0)
        qb, kb = qh.astype(BF16), key.astype(BF16)
        for m in HG_LEVELS:
            k = int(math.log2(m)) + 1
            if m == 1:
                p = _dot_nt((qh * forget).astype(BF16), kb)
            elif m >= SUBLANES:
                g = c // (2 * m)
                t_half = slice(0, m) if rev else slice(m, 2 * m)
                s_half = slice(m, 2 * m) if rev else slice(0, m)
                at = m if rev else m - 1
                b3, q3, k3 = (x.reshape(g, 2 * m, LANES) for x in (b2, qh, key))
                seam = b3[:, at:at + 1, :]
                qt = (q3[:, t_half] * jnp.exp2(b3[:, t_half] - seam)).reshape(c // 2, LANES)
                ks = k3[:, s_half] * jnp.exp2(seam - b3[:, s_half])
                kk = jnp.concatenate([k3[:, t_half], ks] if rev else [ks, k3[:, t_half]], axis=1).reshape(c, LANES)
                p = _dot_nt(qt.astype(BF16), kk.astype(BF16)).reshape(g, m, c)
                s3 = scores.reshape(g, 2 * m, c)
                hit = code.reshape(g, 2 * m, c)[:, t_half] == (-k if rev else k)
                st = jnp.where(hit, p, s3[:, t_half])
                scores = jnp.concatenate([st, s3[:, s_half]] if rev else [s3[:, s_half], st], axis=1).reshape(c, c)
                continue
            else:
                e = jnp.exp2(_neg_abs(b2 - _pair_boundary(b2, m, rev))).astype(BF16)
                p = _dot_nt(qb * e, kb * e)
            scores = jnp.where(code == (-k if rev else k), p, scores)
        out.append((o, scores.astype(BF16)))
    return out


def _hgrn_kernel(*refs, layer, n_chunks, n_heads, zero_init):
    refs = list(refs)
    q_ref, ff_ref, fb_ref, v_ref, g_ref, lbl_ref, gain_ref, code_ref, tri_ref = refs[:9]
    s0_ref = None if zero_init else refs[9]
    o_ref, sfin_ref, st_ref, ob_ref, ab_ref, oi_ref, sc_ref = refs[-7:]
    c = HG_CHUNK
    code = code_ref[...]
    eye = code == 0
    chains = [(hd, d) for hd in range(n_heads) for d in (0, 1)]

    def lanes(hd):
        return slice(hd * HG_D, (hd + 1) * HG_D)

    def lower_bound(hd, d):
        lg = lbl_ref[d, :, lanes(hd)]
        ex = jnp.exp(lg - jnp.max(lg, axis=0, keepdims=True))
        soft = ex / jnp.sum(ex, axis=0, keepdims=True)
        return jnp.sum(soft[:layer + 1], axis=0, keepdims=True) - soft[0:1]

    lb = [lower_bound(hd, d) for hd, d in chains]
    scale = HG_D ** -0.5

    def rows_of(d, i):
        n = (n_chunks - 1 - i) if d else i
        return pl.ds(n * c if isinstance(n, int) else pl.multiple_of(n * c, c), c)

    for ch, (hd, d) in enumerate(chains):
        st_ref[ch] = jnp.zeros((HG_D, HG_D), F32) if zero_init else s0_ref[d, hd].T

    def gates(i):
        return _hg_gates([(q_ref[rows_of(d, i), lanes(hd)], (fb_ref if d else ff_ref)[rows_of(d, i), lanes(hd)],
                           lb[ch], tri_ref[d]) for ch, (hd, d) in enumerate(chains)], scale)

    def scores(i, ab, from_zero=False):
        return _hg_scores([ab[ch] + (v_ref[rows_of(d, i), lanes(hd)], st_ref.at[ch], bool(d))
                           for ch, (hd, d) in enumerate(chains)], code, eye, from_zero)

    def emit(i, oi_sc):
        for ch, (hd, d) in enumerate(chains):
            rows = rows_of(d, i)
            o_intra = _dot(oi_sc[ch][1], v_ref[rows, lanes(hd)].astype(BF16))
            (ob_ref if d else o_ref)[rows, lanes(hd)] = o_intra if oi_sc[ch][0] is None else oi_sc[ch][0] + o_intra

    if n_chunks <= 2:
        ab = [gates(i) for i in range(n_chunks)]
        for i in range(n_chunks):
            emit(i, scores(i, ab[i], from_zero=zero_init and i == 0))
    else:
        n_ab = ab_ref.shape[1]

        def put_ab(ab):
            for ch in range(len(chains)):
                for k in range(n_ab):
                    ab_ref[ch, k] = ab[ch][k]

        def get_oi_sc():
            return [(oi_ref[ch], sc_ref[ch]) for ch in range(len(chains))]

        put_ab(gates(0))
        oi_ref[...] = jnp.zeros(oi_ref.shape, F32)
        sc_ref[...] = jnp.zeros(sc_ref.shape, BF16)

        def body(i, carry):
            emit(jnp.maximum(i - 1, 0), get_oi_sc())
            res = scores(i, [tuple(ab_ref[ch, k] for k in range(n_ab)) for ch in range(len(chains))])
            for ch in range(len(chains)):
                oi_ref[ch] = res[ch][0]
                sc_ref[ch] = res[ch][1]
            put_ab(gates(jnp.minimum(i + 1, n_chunks - 1)))
            return carry

        lax.fori_loop(0, n_chunks, body, 0)
        emit(n_chunks - 1, get_oi_sc())
    if len(sfin_ref.shape) == 5:
        for other in range(sfin_ref.shape[0]):
            if other != layer:
                sfin_ref[other] = jnp.zeros(sfin_ref.shape[1:], F32)
        sfin_ref = sfin_ref.at[layer]
    for ch, (hd, d) in enumerate(chains):
        sfin_ref[d, hd] = st_ref[ch].T

    def finish(n, carry):
        rows = pl.ds(pl.multiple_of(n * (2 * c), 2 * c), 2 * c)
        for hd in range(n_heads):
            o_ref[rows, lanes(hd)] = (_rms(o_ref[rows, lanes(hd)] + ob_ref[rows, lanes(hd)]) * gain_ref[...]
                                      * _silu(g_ref[rows, lanes(hd)]))
        return carry

    lax.fori_loop(0, n_chunks // 2, finish, 0)


def _hgrn_call(proj, row0, lb_logits, gain, state, finals, layer, n_seq, seq_len):
    rows = n_seq * seq_len
    seq0 = row0 // seq_len
    zero_init = state is None
    n_chunks = seq_len // HG_CHUNK
    assert n_chunks % 2 == 0
    nh = HG_HEADS if n_chunks <= 2 else 2
    hw = nh * HG_D
    n_hb = HG_HEADS // nh

    def col_spec(k):
        return pl.BlockSpec((seq_len, hw), lambda b, h: (seq0 + b, k * n_hb + h))

    t, s = np.meshgrid(np.arange(HG_CHUNK), np.arange(HG_CHUNK), indexing="ij")
    lvl = np.where(t == s, 0, np.floor(np.log2(np.maximum(t ^ s, 1))).astype(np.int32) + 1)
    code = jnp.asarray(np.where(t > s, lvl, -lvl), jnp.int32)
    tri = jnp.asarray(np.stack([s <= t, s >= t]), BF16)

    in_specs = [col_spec(0), col_spec(1), col_spec(2), col_spec(3), col_spec(4),
                pl.BlockSpec((2, DEPTH, hw), lambda b, h: (0, 0, h)),
                pl.BlockSpec((1, HG_D), lambda b, h: (0, 0)),
                pl.BlockSpec((HG_CHUNK, HG_CHUNK), lambda b, h: (0, 0)),
                pl.BlockSpec((2, HG_CHUNK, HG_CHUNK), lambda b, h: (0, 0, 0))]
    args = [proj] * 5 + [lb_logits, gain, code, tri]
    if not zero_init:
        in_specs.append(pl.BlockSpec((None, None, 2, nh, HG_D, HG_D), lambda b, h: (b, layer, 0, h, 0, 0)))
        args.append(state)
    aliases = {}
    if finals is None:
        fin_spec = pl.BlockSpec((None, DEPTH, 2, nh, HG_D, HG_D), lambda b, h: (b, 0, 0, h, 0, 0))
    else:
        fin_spec = pl.BlockSpec((None, None, 2, nh, HG_D, HG_D), lambda b, h: (b, layer, 0, h, 0, 0))
        aliases[len(args)] = 1
        in_specs.append(pl.BlockSpec(memory_space=pl.ANY))
        args.append(finals)
    n_ch = 2 * nh
    return pl.pallas_call(
        functools.partial(_hgrn_kernel, layer=layer, n_chunks=n_chunks, n_heads=nh, zero_init=zero_init),
        grid=(n_seq, n_hb),
        in_specs=in_specs,
        out_specs=[pl.BlockSpec((seq_len, hw), lambda b, h: (b, h)), fin_spec],
        out_shape=[jax.ShapeDtypeStruct((rows, HG_W), F32),
                   jax.ShapeDtypeStruct((n_seq, DEPTH, 2, HG_HEADS, HG_D, HG_D), F32)],
        input_output_aliases=aliases,
        scratch_shapes=[pltpu.VMEM((n_ch, HG_D, HG_D), F32), pltpu.VMEM((seq_len, hw), F32),
                        pltpu.VMEM((n_ch, 4, HG_CHUNK, HG_D), F32), pltpu.VMEM((n_ch, HG_CHUNK, HG_D), F32),
                        pltpu.VMEM((n_ch, HG_CHUNK, HG_CHUNK), BF16)],
        compiler_params=pltpu.CompilerParams(dimension_semantics=("parallel", "parallel"),
                                             vmem_limit_bytes=VMEM_LIMIT),
        name="hgrn2_mixer",
    )(*args)


def _s5_params_kernel(lr_ref, li_ref, ldt_ref, btr_ref, bti_ref, cr_ref, ci_ref, a_ref, bm_ref, cm_ref):
    for k in range(lr_ref.shape[0]):
        _s5_params_block(*(r.at[k] for r in (lr_ref, li_ref, ldt_ref, btr_ref, bti_ref, cr_ref, ci_ref,
                                             a_ref, bm_ref, cm_ref)))


def _s5_params_block(lr_ref, li_ref, ldt_ref, btr_ref, bti_ref, cr_ref, ci_ref, a_ref, bm_ref, cm_ref):
    sw = S5_SW
    lr = jnp.minimum(lr_ref[...], -1e-4)
    li = li_ref[...]
    dt = jnp.exp(ldt_ref[...])
    mag = jnp.exp(lr * dt)
    ab_re = mag * jnp.cos(li * dt)
    ab_im = mag * jnp.sin(li * dt)
    nr = ab_re - 1.0
    den = lr * lr + li * li
    z_re = (nr * lr + ab_im * li) / den
    z_im = (ab_im * lr - nr * li) / den

    p_idx = lax.broadcasted_iota(jnp.int32, (S5_P, sw), 0)
    col = lax.broadcasted_iota(jnp.int32, (S5_P, sw), 1)
    for g in range(S5_GB):
        place = (col == p_idx + g * S5_P).astype(BF16)
        zr, zi = z_re[g:g + 1, :], z_im[g:g + 1, :]
        btr, bti = btr_ref[g], bti_ref[g]
        rows = slice(g * S5_CH, (g + 1) * S5_CH)
        bm_ref[rows, :sw] = _dot((zr * btr - zi * bti).astype(BF16), place).astype(BF16)
        bm_ref[rows, sw:] = _dot((zr * bti + zi * btr).astype(BF16), place).astype(BF16)
        cm_ref[rows, :sw] = _dot(cr_ref[g].astype(BF16), place).astype(BF16)
        cm_ref[rows, sw:] = _dot((-ci_ref[g]).astype(BF16), place).astype(BF16)
        a_ref[:, g * S5_P:(g + 1) * S5_P] = jnp.broadcast_to(ab_re[g:g + 1, :], (S5_NSEQ, S5_P))
        a_ref[:, sw + g * S5_P:sw + (g + 1) * S5_P] = jnp.broadcast_to(ab_im[g:g + 1, :], (S5_NSEQ, S5_P))


def _s5_params(lam_re, lam_im, log_dt, b_re, b_im, c_re, c_im):
    nb = DEPTH * 2 * S5_NGB
    gp = (nb, S5_GB, S5_P)
    gcp = (nb, S5_GB, S5_CH, S5_P)
    bt_re = jnp.swapaxes(b_re, -1, -2).reshape(gcp)
    bt_im = jnp.swapaxes(b_im, -1, -2).reshape(gcp)
    ldt = jnp.broadcast_to(log_dt.reshape(nb, S5_GB, 1), gp)
    per = S5_NGB
    gp_spec = pl.BlockSpec((per, S5_GB, S5_P), lambda i: (i, 0, 0))
    gcp_spec = pl.BlockSpec((per, S5_GB, S5_CH, S5_P), lambda i: (i, 0, 0, 0))
    a, bmat, cmat = pl.pallas_call(
        _s5_params_kernel,
        grid=(nb // per,),
        in_specs=[gp_spec] * 3 + [gcp_spec] * 4,
        out_specs=[pl.BlockSpec((per, S5_NSEQ, 2 * S5_SW), lambda i: (i, 0, 0)),
                   pl.BlockSpec((per, LANES, 2 * S5_SW), lambda i: (i, 0, 0)),
                   pl.BlockSpec((per, LANES, 2 * S5_SW), lambda i: (i, 0, 0))],
        out_shape=[jax.ShapeDtypeStruct((nb, S5_NSEQ, 2 * S5_SW), F32),
                   jax.ShapeDtypeStruct((nb, LANES, 2 * S5_SW), BF16),
                   jax.ShapeDtypeStruct((nb, LANES, 2 * S5_SW), BF16)],
        compiler_params=pltpu.CompilerParams(dimension_semantics=("parallel",)),
        name="s5_params",
    )(lam_re.reshape(gp), lam_im.reshape(gp), ldt, bt_re, bt_im, c_re.reshape(gcp), c_im.reshape(gcp))
    lead = (DEPTH, 2, S5_NGB)
    return (a.reshape(lead + a.shape[1:]), bmat.reshape(lead + bmat.shape[1:]), cmat.reshape(lead + cmat.shape[1:]))


def _s5_kernel(*refs, chained):
    refs = list(refs)
    u_ref, bm_ref, cm_ref, a_ref, d_ref = refs[:5]
    rest = refs[5:]
    s0_ref = rest.pop(0) if chained else None
    y_ref = rest.pop(0)
    hfin_ref = None if chained else rest.pop(0)
    hbuf_e, hbuf_o, hb0, hb1, hst = rest
    hbufs, hb16s = (hbuf_e, hbuf_o), (hb0, hb1)
    ns, sw = S5_NSEQ, S5_SW
    half = ns // 2
    n_tc = S5_SEQ // S5_TC
    blk = S5_TC * ns
    dirs = (0, 1)

    for d in dirs:
        hst[d] = jnp.zeros((ns, 2 * sw), F32)

    def steps_of(d, i):
        return pl.ds(((n_tc - 1 - i) if d else i) * S5_TC, S5_TC)

    def kept(i):
        return i // 2 if chained else 0

    def project(i, par):
        for d in dirs:
            u = u_ref[steps_of(d, i)].reshape(blk, LANES)
            hbufs[par][d, kept(i)] = _dot(u.astype(BF16), bm_ref[d])

    def scan(i, par, emit):
        a = [(a_ref[d, :half, :sw], a_ref[d, :half, sw:]) for d in dirs]
        h = [[(hst[d, k * half:(k + 1) * half, :sw], hst[d, k * half:(k + 1) * half, sw:]) for k in range(2)]
             for d in dirs]
        for jj in range(S5_TC):
            for d in dirs:
                j = S5_TC - 1 - jj if d else jj
                ar, ai = a[d]
                bu = hbufs[par].at[d, kept(i)]
                for k in range(2):
                    r = slice(j * ns + k * half, j * ns + (k + 1) * half)
                    hr, hi = h[d][k]
                    h[d][k] = (ar * hr - ai * hi + bu[r, :sw], ar * hi + ai * hr + bu[r, sw:])
                if emit:
                    r = slice(j * ns, (j + 1) * ns)
                    hb16s[par][d, r, :sw] = jnp.concatenate([h[d][0][0], h[d][1][0]], axis=0).astype(BF16)
                    hb16s[par][d, r, sw:] = jnp.concatenate([h[d][0][1], h[d][1][1]], axis=0).astype(BF16)
        for d in dirs:
            for k in range(2):
                hst[d, k * half:(k + 1) * half, :sw] = h[d][k][0]
                hst[d, k * half:(k + 1) * half, sw:] = h[d][k][1]

    def readout(i, par):
        for d in dirs:
            steps = steps_of(d, i)
            y = _dot_nt(hb16s[par][d], cm_ref[d])
            y_ref[steps] = y_ref[steps] + y.reshape(S5_TC, ns, LANES)

    def skip(n, carry):
        steps = pl.ds(n * S5_TC, S5_TC)
        y_ref[steps] = d_ref[...] * u_ref[steps]
        return carry

    project(0, 0)
    if chained:
        def sweep(k, carry):
            i = 2 * k
            scan(i, 0, False)
            project(i + 1, 1)
            scan(i + 1, 1, False)
            project(i + 2, 0)
            return carry

        lax.fori_loop(0, n_tc // 2 - 1, sweep, 0)
        scan(n_tc - 2, 0, False)
        project(n_tc - 1, 1)
        scan(n_tc - 1, 1, False)

        n_long = s0_ref.shape[1]
        pieces = ns // n_long
        for d in dirs:
            pr, pi = a_ref[d, 0:1, :sw], a_ref[d, 0:1, sw:]
            for _ in range(int(math.log2(S5_SEQ))):
                pr, pi = pr * pr - pi * pi, 2.0 * (pr * pi)
            for b in range(n_long):
                hr, hi = s0_ref[d, b:b + 1, :sw], s0_ref[d, b:b + 1, sw:]
                for k in (range(pieces - 1, -1, -1) if d else range(pieces)):
                    r = b * pieces + k
                    zr, zi = hst[d, r:r + 1, :sw], hst[d, r:r + 1, sw:]
                    hst[d, r:r + 1, :sw] = hr
                    hst[d, r:r + 1, sw:] = hi
                    hr, hi = pr * hr - pi * hi + zr, pr * hi + pi * hr + zi

    hb1[...] = jnp.zeros(hb1.shape, BF16)
    lax.fori_loop(0, n_tc, skip, 0)

    def body(k, carry):
        i = 2 * k
        scan(i, 0, True)
        readout(jnp.maximum(i - 1, 0), 1)
        if not chained:
            project(i + 1, 1)
        scan(i + 1, 1, True)
        readout(i, 0)
        if not chained:
            project(jnp.minimum(i + 2, n_tc - 1), 0)
        return carry

    lax.fori_loop(0, n_tc // 2, body, 0)
    if not chained:
        for d in dirs:
            hfin_ref[d] = hst[d]
    readout(n_tc - 1, 1)


def _s5_call(u_tm, part, a, bmat, cmat, dskip, s0, layer):
    chained = s0 is not None
    tm_spec = pl.BlockSpec((S5_SEQ, S5_NSEQ, LANES), lambda g: (0, 0, g))

    def mat_spec(rows):
        return pl.BlockSpec((None, 2, None, rows, 2 * S5_SW), lambda g: (layer, 0, g, 0, 0))

    in_specs = [pl.BlockSpec((S5_SEQ, S5_NSEQ, LANES), lambda g: (0, part, g)),
                mat_spec(LANES), mat_spec(LANES), mat_spec(S5_NSEQ),
                pl.BlockSpec((None, None, 1, LANES), lambda g: (layer, g, 0, 0))]
    args = [u_tm, bmat, cmat, a, dskip]
    out_specs = [tm_spec]
    out_shape = [jax.ShapeDtypeStruct((S5_SEQ, S5_NSEQ, S5_W), F32)]
    if chained:
        in_specs.append(pl.BlockSpec((2, None, s0.shape[2], 2 * S5_SW), lambda g: (0, g, 0, 0)))
        args.append(s0)
    else:
        out_specs.append(pl.BlockSpec((2, None, S5_NSEQ, 2 * S5_SW), lambda g: (0, g, 0, 0)))
        out_shape.append(jax.ShapeDtypeStruct((2, S5_NGB, S5_NSEQ, 2 * S5_SW), F32))
    blk = S5_TC * S5_NSEQ
    kept = S5_SEQ // S5_TC // 2 if chained else 1
    res = pl.pallas_call(
        functools.partial(_s5_kernel, chained=chained),
        grid=(S5_NGB,),
        in_specs=in_specs,
        out_specs=out_specs,
        out_shape=out_shape,
        scratch_shapes=[pltpu.VMEM((2, kept, blk, 2 * S5_SW), F32),
                        pltpu.VMEM((2, kept, blk, 2 * S5_SW), F32),
                        pltpu.VMEM((2, blk, 2 * S5_SW), BF16),
                        pltpu.VMEM((2, blk, 2 * S5_SW), BF16),
                        pltpu.VMEM((2, S5_NSEQ, 2 * S5_SW), F32)],
        compiler_params=pltpu.CompilerParams(dimension_semantics=("parallel",),
                                             vmem_limit_bytes=VMEM_LIMIT),
        name="s5_scan",
    )(*args)
    return (res[0], None) if chained else (res[0], res[1])


def _out_kernel(*refs, n_x, last_layer):
    x_refs = refs[:n_x]
    (ohc_ref, ohs_ref, y5c_ref, y5s_ref, g1_ref, sh2_ref, sc2_ref, g2_ref, nffn_ref, nfin_ref,
     wglu_ref, wout_ref, wg_ref, wu_ref, wd_ref) = refs[n_x:n_x + 15]
    rest = refs[n_x + 15:]
    if last_layer:
        oc_ref, os_ref, wglu_b, wout_b = rest
    else:
        o_ref, wglu_b, wout_b = rest

    @pl.when(_first_step())
    def _():
        _cast_rows(wglu_ref, wglu_b)
        _cast_rows(wout_ref, wout_b)

    smp = _is_sample_tile()
    y = jnp.concatenate([jnp.where(smp, y5s_ref[:, s, :], y5c_ref[:, s, :]) for s in range(TILE_S)],
                        axis=0)
    y = _gelu_tanh(y)
    y = y * _sigmoid(_dot(y.astype(BF16), wglu_b[...]))
    ohg = jnp.where(smp, ohs_ref[...], ohc_ref[...]).reshape(TILE_ROWS, HG_W)
    mix = _dot(ohg.astype(BF16), wout_b[:HG_W, :]) + _dot(y.astype(BF16), wout_b[HG_W:, :])
    x = _stream_tile(x_refs).reshape(TILE_ROWS, D_MODEL) + g1_ref[...] * mix
    h = _rms(x) * nffn_ref[...]
    h = (h * (1.0 + sc2_ref[...]) + sh2_ref[...]).astype(BF16)
    act = (_silu(_dot(h, wg_ref[...])) * _dot(h, wu_ref[...])).astype(BF16)
    x = x + g2_ref[...] * _dot(act, wd_ref[...])
    if not last_layer:
        o_ref[...] = x.reshape(o_ref.shape)
    else:
        x = (_rms(x) * nfin_ref[...]).reshape(oc_ref.shape)

        @pl.when(smp)
        def _():
            os_ref[...] = x

        @pl.when(jnp.logical_not(smp))
        def _():
            oc_ref[...] = x


def _out_call(xs, ohg_c, ohg_s, y5_c, y5_s, mod4, nffn, nfin, wglu, wout, wg, wu, wd, layer, last_layer):
    vec = pl.BlockSpec((1, D_MODEL), lambda sb, tb: (0, 0))
    part_shape = jax.ShapeDtypeStruct((S5_NSEQ, S5_SEQ, D_MODEL), F32)
    if last_layer:
        out_specs = [_part_tile_spec(D_MODEL, _ctx_index), _part_tile_spec(D_MODEL, _smp_index)]
        out_shape = [part_shape, part_shape]
    else:
        out_specs = _tile_spec(D_MODEL)
        out_shape = jax.ShapeDtypeStruct((ALL_SEQ, S5_SEQ, D_MODEL), F32)
    x_specs, x_args = _stream_specs(xs)
    return pl.pallas_call(
        functools.partial(_out_kernel, n_x=len(x_args), last_layer=last_layer),
        grid=(ALL_SEQ // TILE_S, TIME_TILES),
        in_specs=x_specs + [
                  _part_tile_spec(HG_W, _ctx_index), _part_tile_spec(HG_W, _smp_index),
                  _part_tm_tile_spec(S5_W, _ctx_index), _part_tm_tile_spec(S5_W, _smp_index),
                  _mod_spec(layer, 2), _mod_spec(layer, 3), _mod_spec(layer, 4), _mod_spec(layer, 5),
                  vec, vec,
                  _layer_spec((S5_W, S5_W), layer), _layer_spec((D_MODEL, D_MODEL), layer),
                  _whole_spec((D_MODEL, D_FF)), _whole_spec((D_MODEL, D_FF)), _whole_spec((D_FF, D_MODEL))],
        out_specs=out_specs,
        out_shape=out_shape,
        scratch_shapes=[pltpu.VMEM((S5_W, S5_W), BF16), pltpu.VMEM((D_MODEL, D_MODEL), BF16)],
        compiler_params=pltpu.CompilerParams(dimension_semantics=("arbitrary", "arbitrary"),
                                             vmem_limit_bytes=VMEM_LIMIT),
        name="out_ffn",
    )(*x_args, ohg_c, ohg_s, y5_c, y5_s, mod4, mod4, mod4, mod4, nffn, nfin, wglu, wout, wg, wu, wd)


def _s5_state_to_blocks(s):
    n = s.shape[0]
    s = s.reshape(n, 2, S5_NGB, S5_GB, S5_P, 2)
    return jnp.transpose(s, (1, 2, 0, 5, 3, 4)).reshape(2, S5_NGB, n, 2 * S5_SW)


def _s5_blocks_to_state(h):
    n = h.shape[2]
    h = h.reshape(2, S5_NGB, n, 2, S5_GB, S5_P)
    return jnp.transpose(h, (2, 0, 1, 4, 5, 3)).reshape(n, 2, S5_GROUPS, S5_P, 2)


def kernel(x_prompt, x_sample, state_hgrn, state_s5, c, c_ctx, w_mod, b_mod, norm_mix, norm_ffn, norm_final, w_in, w_out, hg_lb_logits, hg_norm, s5_lam_re, s5_lam_im, s5_log_dt, s5_b_re, s5_b_im, s5_c_re, s5_c_im, s5_d, s5_w_glu, w_gate, w_up, w_down):
    n_ctx, ctx_len, _ = x_prompt.shape
    n_dec, dec_len, _ = x_sample.shape
    assert ctx_len == S5_SEQ and n_ctx == S5_NSEQ and n_dec * dec_len == S5_NSEQ * S5_SEQ

    cond = jnp.concatenate([c_ctx[None, :], c, jnp.zeros((SUBLANES - 1 - n_dec, D_MODEL), F32)], axis=0)
    mod4 = _mod_call(cond, w_mod, b_mod).reshape(DEPTH, SUBLANES, 1, 6 * D_MODEL)

    s5_a, s5_bmat, s5_cmat = _s5_params(s5_lam_re, s5_lam_im, s5_log_dt, s5_b_re, s5_b_im, s5_c_re, s5_c_im)
    s5_dskip = s5_d.reshape(DEPTH, S5_NGB, 1, LANES)
    nfin = norm_final.reshape(1, D_MODEL)

    assert dec_len // S5_SEQ == TILE_S and n_dec + 1 <= SUBLANES
    tok = (S5_NSEQ, S5_SEQ, D_MODEL)
    ctx_rows = S5_NSEQ * S5_SEQ
    xs = (x_prompt.reshape(tok), x_sample.reshape(tok))
    ctx_fin, smp_fin, s5_finals = None, None, []
    for l in range(DEPTH):
        proj3, u_tm, (w_gate_b, w_up_b, w_down_b) = _in_call(
            xs, norm_mix[l].reshape(1, D_MODEL), mod4, w_in, (w_gate, w_up, w_down), l)
        proj = proj3.reshape(ALL_SEQ * S5_SEQ, HG_IN_W)
        gain = hg_norm[l].reshape(1, HG_D)
        ohg_c, ctx_fin = _hgrn_call(proj, 0, hg_lb_logits, gain, None, ctx_fin, l, n_ctx, ctx_len)
        ohg_s, smp_fin = _hgrn_call(proj, ctx_rows, hg_lb_logits, gain, state_hgrn, smp_fin, l, n_dec, dec_len)
        y5_c, s5_fin = _s5_call(u_tm, 0, s5_a, s5_bmat, s5_cmat, s5_dskip, None, l)
        y5_s, _ = _s5_call(u_tm, 1, s5_a, s5_bmat, s5_cmat, s5_dskip, _s5_state_to_blocks(state_s5[:, l]), l)
        last = l == DEPTH - 1
        res = _out_call(xs, ohg_c.reshape(S5_NSEQ, S5_SEQ, HG_W), ohg_s.reshape(S5_NSEQ, S5_SEQ, HG_W),
                        y5_c, y5_s, mod4, norm_ffn[l].reshape(1, D_MODEL), nfin,
                        s5_w_glu, w_out, w_gate_b, w_up_b, w_down_b, l, last)
        xs = res if last else (res,)
        s5_finals.append(_s5_blocks_to_state(s5_fin))
    y_prompt, y_sample = xs
    return (y_prompt.reshape(x_prompt.shape), y_sample.reshape(x_sample.shape),
            ctx_fin, jnp.stack(s5_finals, axis=1))
```

```python
import functools
import math

import jax
import jax.numpy as jnp
import numpy as np
from jax import lax
from jax.experimental import pallas as pl
from jax.experimental.pallas import tpu as pltpu

F32 = jnp.float32
BF16 = jnp.bfloat16

LANES = 128
SUBLANES = 8

D_MODEL = 1024
DEPTH = 2
GRID_W = 64
HG_W = 512
HG_HEADS = 4
HG_D = HG_W // HG_HEADS
S5_W = 512
S5_CH = 16
S5_GROUPS = S5_W // S5_CH
S5_P = 64
S5_GB = LANES // S5_CH
S5_NGB = S5_GROUPS // S5_GB
S5_SW = S5_GB * S5_P
HG_IN_W = 5 * HG_W
IN_W = HG_IN_W + S5_W
D_FF = 2816
EPS = 1e-6

HG_CHUNK = 128
HG_LEVELS = (64, 32, 16, 8, 4, 2, 1)
S5_SEQ = 256
S5_NSEQ = 16
S5_TC = 16

TILE_S = SUBLANES
TILE_T = 64
TILE_ROWS = TILE_S * TILE_T
TIME_TILES = S5_SEQ // TILE_T
ALL_SEQ = 2 * S5_NSEQ
CTX_TILES = S5_NSEQ // TILE_S
X_RING = 3
CAST_ROWS = 128
MOD_TILE_N = 1536
VMEM_LIMIT = 56 * 1024 * 1024


def _sigmoid(x):
    return 1.0 / (1.0 + jnp.exp(-x))


def _silu(x):
    return x * _sigmoid(x)


def _gelu_tanh(x):
    return 0.5 * x * (1.0 + jnp.tanh(math.sqrt(2.0 / math.pi) * (x + 0.044715 * (x * x * x))))


def _rms(x):
    return x * lax.rsqrt(jnp.mean(x * x, axis=-1, keepdims=True) + EPS)


def _dot(a, b):
    return jnp.dot(a, b, preferred_element_type=F32)


def _dot_nt(a, b):
    return lax.dot_general(a, b, (((1,), (1,)), ((), ())), preferred_element_type=F32)


def _dot_tn(a, b):
    return lax.dot_general(a, b, (((0,), (0,)), ((), ())), preferred_element_type=F32)


def _whole_spec(shape):
    return pl.BlockSpec(tuple(shape), lambda *_: (0,) * len(shape), pipeline_mode=pl.Buffered(1))


def _layer_spec(shape, layer):
    nd = len(shape)
    return pl.BlockSpec((None,) + tuple(shape), lambda *_: (layer,) + (0,) * nd, pipeline_mode=pl.Buffered(1))


def _mod_kernel(cond_ref, w_ref, b_ref, o_ref):
    a = _silu(cond_ref[...]).astype(BF16)
    o_ref[0] = _dot(a, w_ref[0].astype(BF16)) + b_ref[0]


def _mod_call(cond, w_mod, b_mod):
    n_cond = cond.shape[0]
    n_out = w_mod.shape[-1]
    return pl.pallas_call(
        _mod_kernel,
        grid=(DEPTH, n_out // MOD_TILE_N),
        in_specs=[
            pl.BlockSpec((n_cond, D_MODEL), lambda l, j: (0, 0)),
            pl.BlockSpec((1, D_MODEL, MOD_TILE_N), lambda l, j: (l, 0, j)),
            pl.BlockSpec((1, 1, MOD_TILE_N), lambda l, j: (l, 0, j)),
        ],
        out_specs=pl.BlockSpec((1, n_cond, MOD_TILE_N), lambda l, j: (l, 0, j)),
        out_shape=jax.ShapeDtypeStruct((DEPTH, n_cond, n_out), F32),
        compiler_params=pltpu.CompilerParams(dimension_semantics=("parallel", "parallel"),
                                             vmem_limit_bytes=VMEM_LIMIT),
        name="adaln_mod",
    )(cond, w_mod, b_mod.reshape(DEPTH, 1, n_out))


def _first_step():
    return jnp.logical_and(pl.program_id(0) == 0, pl.program_id(1) == 0)


def _cast_rows(src_ref, dst_ref):
    for r in range(0, src_ref.shape[0], CAST_ROWS):
        dst_ref[r:r + CAST_ROWS, :] = src_ref[r:r + CAST_ROWS, :].astype(BF16)


def _grid_pos_tile(omega, tb):
    nf = omega.shape[-1]
    s_idx = lax.broadcasted_iota(jnp.int32, (TILE_S, nf), 0)
    j_idx = lax.broadcasted_iota(jnp.int32, (TILE_T, nf), 0)
    t0 = tb * TILE_T
    row = (s_idx * (S5_SEQ // GRID_W) + t0 // GRID_W).astype(F32) * omega
    col = (j_idx + t0 % GRID_W).astype(F32) * omega
    enc_r = jnp.concatenate([jnp.sin(row), jnp.cos(row)], axis=-1)
    enc_c = jnp.concatenate([jnp.sin(col), jnp.cos(col)], axis=-1)
    shape = (TILE_S, TILE_T, 2 * nf)
    return jnp.concatenate([jnp.broadcast_to(enc_r[:, None, :], shape),
                            jnp.broadcast_to(enc_c[None, :, :], shape)], axis=-1)


def _is_sample_tile():
    return pl.program_id(0) >= CTX_TILES


def _x_tile_copy(srcs, buf_ref, sem_ref, t, start):
    slot = t % X_RING
    sb, tb = t // TIME_TILES, t % TIME_TILES

    def run(src, sb_local):
        cp = pltpu.make_async_copy(
            src.at[pl.ds(sb_local * TILE_S, TILE_S), pl.ds(tb * TILE_T, TILE_T), :],
            buf_ref.at[slot], sem_ref.at[slot])
        if start:
            cp.start()
        else:
            cp.wait()

    if len(srcs) == 1:
        run(srcs[0], sb)
    else:
        pl.when(sb < CTX_TILES)(lambda: run(srcs[0], sb))
        pl.when(sb >= CTX_TILES)(lambda: run(srcs[1], sb - CTX_TILES))


def _in_kernel(*refs, first_layer):
    n_in = 10 if first_layer else 8
    for src, dst in zip(refs[n_in - 3:n_in], refs[-6:-3]):
        dst[...] = src[...].astype(BF16)
    refs = refs[:n_in - 3] + refs[n_in:-6] + refs[-3:]

    n_steps = pl.num_programs(0) * TIME_TILES
    step = pl.program_id(0) * TIME_TILES + pl.program_id(1)
    srcs = refs[:2] if first_layer else refs[:1]
    xbuf_ref, xsem_ref = refs[-2:]

    @pl.when(step == 0)
    def _():
        for t in range(X_RING - 1):
            _x_tile_copy(srcs, xbuf_ref, xsem_ref, jnp.int32(t), True)

    @pl.when(step + (X_RING - 1) < n_steps)
    def _():
        _x_tile_copy(srcs, xbuf_ref, xsem_ref, step + (X_RING - 1), True)

    _x_tile_copy(srcs, xbuf_ref, xsem_ref, step, False)
    x = xbuf_ref[step % X_RING]

    if first_layer:
        _, _, om_ref, gain_ref, sh_ref, sc_ref, w_ref, proj_ref, u_ref, xo_ref, wb_ref = refs[:-2]
        x = jnp.where(_is_sample_tile(), x + _grid_pos_tile(om_ref[...], pl.program_id(1)), x)
        xo_ref[...] = x
    else:
        _, gain_ref, sh_ref, sc_ref, w_ref, proj_ref, u_ref, wb_ref = refs[:-2]

    @pl.when(_first_step())
    def _():
        _cast_rows(w_ref, wb_ref)

    x = x.reshape(TILE_ROWS, D_MODEL)
    h = _rms(x) * gain_ref[...]
    h = (h * (1.0 + sc_ref[...]) + sh_ref[...]).astype(BF16)
    u = _dot(h, wb_ref[:, HG_IN_W:])
    for s in range(TILE_S):
        u_ref[:, s, :] = u[s * TILE_T:(s + 1) * TILE_T, :]
    proj_ref[...] = _dot(h, wb_ref[:, :HG_IN_W]).reshape(proj_ref.shape)


def _tile_spec(width):
    return pl.BlockSpec((TILE_S, TILE_T, width), lambda sb, tb: (sb, tb, 0))


def _tm_tile_spec(width):
    return pl.BlockSpec((TILE_T, TILE_S, width), lambda sb, tb: (tb, sb, 0))


def _ctx_index(sb, tb):
    on = sb < CTX_TILES
    return jnp.where(on, sb, CTX_TILES - 1), jnp.where(on, tb, TIME_TILES - 1)


def _smp_index(sb, tb):
    on = sb >= CTX_TILES
    return jnp.where(on, sb - CTX_TILES, 0), jnp.where(on, tb, 0)


def _part_tile_spec(width, index):
    return pl.BlockSpec((TILE_S, TILE_T, width), lambda sb, tb: index(sb, tb) + (0,))


def _part_tm_tile_spec(width, index):
    return pl.BlockSpec((TILE_T, TILE_S, width), lambda sb, tb: index(sb, tb)[::-1] + (0,))


def _mod_spec(layer, col):
    return pl.BlockSpec((None, None, 1, D_MODEL),
                        lambda sb, tb: (layer, jnp.maximum(sb - (CTX_TILES - 1), 0), 0, col))


def _in_call(xs, gain, mod4, w_in, ffn_w, layer):
    first_layer = len(xs) == 2
    n_steps = (ALL_SEQ // TILE_S) * TIME_TILES
    if first_layer:
        assert GRID_W % TILE_T == 0 and S5_SEQ % GRID_W == 0
        nf = D_MODEL // 4
        omega = 1.0 / (np.float32(10000.0) ** (np.arange(nf, dtype=np.float32) / np.float32(nf)))
        in_specs = [pl.BlockSpec(memory_space=pl.ANY), pl.BlockSpec(memory_space=pl.ANY),
                    pl.BlockSpec((1, nf), lambda sb, tb: (0, 0))]
        args = list(xs) + [jnp.asarray(omega.reshape(1, nf), F32)]
    else:
        in_specs = [pl.BlockSpec(memory_space=pl.ANY)]
        args = list(xs)
    in_specs += [
        pl.BlockSpec((1, D_MODEL), lambda sb, tb: (0, 0)),
        _mod_spec(layer, 0),
        _mod_spec(layer, 1),
        _layer_spec((D_MODEL, IN_W), layer),
    ]
    args += [gain, mod4, mod4, w_in]
    out_specs = [_tile_spec(HG_IN_W), _tm_tile_spec(S5_W)]
    out_shape = [jax.ShapeDtypeStruct((ALL_SEQ, S5_SEQ, HG_IN_W), F32),
                 jax.ShapeDtypeStruct((S5_SEQ, ALL_SEQ, S5_W), F32)]
    if first_layer:
        out_specs.append(_tile_spec(D_MODEL))
        out_shape.append(jax.ShapeDtypeStruct((ALL_SEQ, S5_SEQ, D_MODEL), F32))
    for w in ffn_w:
        _, n_rows, n_cols = w.shape
        slab = n_rows // n_steps
        assert slab * n_steps == n_rows and slab % (2 * SUBLANES) == 0
        in_specs.append(pl.BlockSpec((None, slab, n_cols), lambda sb, tb: (layer, sb * TIME_TILES + tb, 0)))
        args.append(w)
        out_specs.append(pl.BlockSpec((slab, n_cols), lambda sb, tb: (sb * TIME_TILES + tb, 0)))
        out_shape.append(jax.ShapeDtypeStruct((n_rows, n_cols), BF16))
    res = pl.pallas_call(
        functools.partial(_in_kernel, first_layer=first_layer),
        grid=(ALL_SEQ // TILE_S, TIME_TILES),
        in_specs=in_specs,
        out_specs=out_specs,
        out_shape=out_shape,
        scratch_shapes=[pltpu.VMEM((D_MODEL, IN_W), BF16),
                        pltpu.VMEM((X_RING, TILE_S, TILE_T, D_MODEL), F32),
                        pltpu.SemaphoreType.DMA((X_RING,))],
        compiler_params=pltpu.CompilerParams(dimension_semantics=("arbitrary", "arbitrary"),
                                             vmem_limit_bytes=VMEM_LIMIT),
        name="in_proj",
    )(*args)
    n_w = len(ffn_w)
    return res[0], res[1], (res[2] if first_layer else xs[0]), res[len(res) - n_w:]


def _pair_boundary(b, m, rev):
    c = b.shape[0]
    span = 2 * m
    at = m if rev else m - 1
    if span >= SUBLANES:
        b3 = b.reshape(c // span, span, LANES)
        return jnp.broadcast_to(b3[:, at:at + 1, :], b3.shape).reshape(c, LANES)
    b3 = b.reshape(c // SUBLANES, SUBLANES, LANES)
    sub = lax.broadcasted_iota(jnp.int32, b3.shape, 1)
    out = None
    for p in range(SUBLANES // span):
        piece = jnp.broadcast_to(b3[:, p * span + at:p * span + at + 1, :], b3.shape)
        out = piece if out is None else jnp.where(sub >= p * span, piece, out)
    return out.reshape(c, LANES)


def _neg_abs(x):
    bits = lax.bitcast_convert_type(x, jnp.uint32) | jnp.uint32(0x80000000)
    return lax.bitcast_convert_type(bits, F32)


def _hg_gates(chains, scale):
    outs = []
    for q, fl, lb, tri in chains:
        sig = _sigmoid(fl)
        forget = lb + (1.0 - lb) * sig
        logf = jnp.log2(forget)
        key = (1.0 - lb) * (1.0 - sig)
        hi = logf.astype(BF16)
        r1 = logf - hi.astype(F32)
        mid = r1.astype(BF16)
        lo = (r1 - mid.astype(F32)).astype(BF16)
        parts = _dot(tri, jnp.concatenate([hi, mid, lo], axis=1))
        b2 = parts[:, :LANES] + parts[:, LANES:2 * LANES] + parts[:, 2 * LANES:]
        outs.append((_silu(q) * scale, key, b2, forget))
    return outs


def _hg_scores(chains, code, eye, from_zero=False):
    c = chains[0][0].shape[0]
    o_inter = []
    for qh, key, b2, forget, v, st_ref, rev in chains:
        b_edge = b2[0:1, :] if rev else b2[c - 1:c, :]
        k_end = key * jnp.exp2(b_edge - b2)
        grown = _dot_tn(v.astype(BF16), k_end.astype(BF16))
        if from_zero:
            o_inter.append(None)
            st_ref[...] = grown
        else:
            st = st_ref[...]
            o_inter.append(_dot_nt((qh * jnp.exp2(b2)).astype(BF16), st.astype(BF16)))
            st_ref[...] = jnp.exp2(b_edge) * st + grown

    out = []
    for o, (qh, key, b2, forget, v, st_ref, rev) in zip(o_inter, chains):
        scores = jnp.where(eye, jnp.sum(qh * key, axis=-1, keepdims=True), 0.0)
        qb, kb = qh.astype(BF16), key.astype(BF16)
        for m in HG_LEVELS:
            k = int(math.log2(m)) + 1
            if m == 1:
                p = _dot_nt((qh * forget).astype(BF16), kb)
            elif m >= SUBLANES:
                g = c // (2 * m)
                t_half = slice(0, m) if rev else slice(m, 2 * m)
                s_half = slice(m, 2 * m) if rev else slice(0, m)
                at = m if rev else m - 1
                b3, q3, k3 = (x.reshape(g, 2 * m, LANES) for x in (b2, qh, key))
                seam = b3[:, at:at + 1, :]
                qt = (q3[:, t_half] * jnp.exp2(b3[:, t_half] - seam)).reshape(c // 2, LANES)
                ks = k3[:, s_half] * jnp.exp2(seam - b3[:, s_half])
                kk = jnp.concatenate([k3[:, t_half], ks] if rev else [ks, k3[:, t_half]], axis=1).reshape(c, LANES)
                p = _dot_nt(qt.astype(BF16), kk.astype(BF16)).reshape(g, m, c)
                s3 = scores.reshape(g, 2 * m, c)
                hit = code.reshape(g, 2 * m, c)[:, t_half] == (-k if rev else k)
                st = jnp.where(hit, p, s3[:, t_half])
                scores = jnp.concatenate([st, s3[:, s_half]] if rev else [s3[:, s_half], st], axis=1).reshape(c, c)
                continue
            else:
                e = jnp.exp2(_neg_abs(b2 - _pair_boundary(b2, m, rev))).astype(BF16)
                p = _dot_nt(qb * e, kb * e)
            scores = jnp.where(code == (-k if rev else k), p, scores)
        out.append((o, scores.astype(BF16)))
    return out


def _hgrn_kernel(*refs, layer, n_chunks, n_heads, zero_init):
    refs = list(refs)
    q_ref, ff_ref, fb_ref, v_ref, g_ref, lbl_ref, gain_ref, code_ref, tri_ref = refs[:9]
    s0_ref = None if zero_init else refs[9]
    o_ref, sfin_ref, st_ref, ob_ref, ab_ref, oi_ref, sc_ref = refs[-7:]
    c = HG_CHUNK
    code = code_ref[...]
    eye = code == 0
    chains = [(hd, d) for hd in range(n_heads) for d in (0, 1)]

    def lanes(hd):
        return slice(hd * HG_D, (hd + 1) * HG_D)

    def lower_bound(hd, d):
        lg = lbl_ref[d, :, lanes(hd)]
        ex = jnp.exp(lg - jnp.max(lg, axis=0, keepdims=True))
        soft = ex / jnp.sum(ex, axis=0, keepdims=True)
        return jnp.sum(soft[:layer + 1], axis=0, keepdims=True) - soft[0:1]

    lb = [lower_bound(hd, d) for hd, d in chains]
    scale = HG_D ** -0.5

    def rows_of(d, i):
        n = (n_chunks - 1 - i) if d else i
        return pl.ds(n * c if isinstance(n, int) else pl.multiple_of(n * c, c), c)

    for ch, (hd, d) in enumerate(chains):
        st_ref[ch] = jnp.zeros((HG_D, HG_D), F32) if zero_init else s0_ref[d, hd].T

    def gates(i):
        return _hg_gates([(q_ref[rows_of(d, i), lanes(hd)], (fb_ref if d else ff_ref)[rows_of(d, i), lanes(hd)],
                           lb[ch], tri_ref[d]) for ch, (hd, d) in enumerate(chains)], scale)

    def scores(i, ab, from_zero=False):
        return _hg_scores([ab[ch] + (v_ref[rows_of(d, i), lanes(hd)], st_ref.at[ch], bool(d))
                           for ch, (hd, d) in enumerate(chains)], code, eye, from_zero)

    def emit(i, oi_sc):
        for ch, (hd, d) in enumerate(chains):
            rows = rows_of(d, i)
            o_intra = _dot(oi_sc[ch][1], v_ref[rows, lanes(hd)].astype(BF16))
            (ob_ref if d else o_ref)[rows, lanes(hd)] = o_intra if oi_sc[ch][0] is None else oi_sc[ch][0] + o_intra

    if n_chunks <= 2:
        ab = [gates(i) for i in range(n_chunks)]
        for i in range(n_chunks):
            emit(i, scores(i, ab[i], from_zero=zero_init and i == 0))
    else:
        n_ab = ab_ref.shape[1]

        def put_ab(ab):
            for ch in range(len(chains)):
                for k in range(n_ab):
                    ab_ref[ch, k] = ab[ch][k]

        def get_oi_sc():
            return [(oi_ref[ch], sc_ref[ch]) for ch in range(len(chains))]

        put_ab(gates(0))
        oi_ref[...] = jnp.zeros(oi_ref.shape, F32)
        sc_ref[...] = jnp.zeros(sc_ref.shape, BF16)

        def body(i, carry):
            emit(jnp.maximum(i - 1, 0), get_oi_sc())
            res = scores(i, [tuple(ab_ref[ch, k] for k in range(n_ab)) for ch in range(len(chains))])
            for ch in range(len(chains)):
                oi_ref[ch] = res[ch][0]
                sc_ref[ch] = res[ch][1]
            put_ab(gates(jnp.minimum(i + 1, n_chunks - 1)))
            return carry

        lax.fori_loop(0, n_chunks, body, 0)
        emit(n_chunks - 1, get_oi_sc())
    if len(sfin_ref.shape) == 5:
        for other in range(sfin_ref.shape[0]):
            if other != layer:
                sfin_ref[other] = jnp.zeros(sfin_ref.shape[1:], F32)
        sfin_ref = sfin_ref.at[layer]
    for ch, (hd, d) in enumerate(chains):
        sfin_ref[d, hd] = st_ref[ch].T

    def finish(n, carry):
        rows = pl.ds(pl.multiple_of(n * (2 * c), 2 * c), 2 * c)
        for hd in range(n_heads):
            o_ref[rows, lanes(hd)] = (_rms(o_ref[rows, lanes(hd)] + ob_ref[rows, lanes(hd)]) * gain_ref[...]
                                      * _silu(g_ref[rows, lanes(hd)]))
        return carry

    lax.fori_loop(0, n_chunks // 2, finish, 0)


def _hgrn_call(proj, row0, lb_logits, gain, state, finals, layer, n_seq, seq_len):
    rows = n_seq * seq_len
    seq0 = row0 // seq_len
    zero_init = state is None
    n_chunks = seq_len // HG_CHUNK
    assert n_chunks % 2 == 0
    nh = HG_HEADS if n_chunks <= 2 else 2
    hw = nh * HG_D
    n_hb = HG_HEADS // nh

    def col_spec(k):
        return pl.BlockSpec((seq_len, hw), lambda b, h: (seq0 + b, k * n_hb + h))

    t, s = np.meshgrid(np.arange(HG_CHUNK), np.arange(HG_CHUNK), indexing="ij")
    lvl = np.where(t == s, 0, np.floor(np.log2(np.maximum(t ^ s, 1))).astype(np.int32) + 1)
    code = jnp.asarray(np.where(t > s, lvl, -lvl), jnp.int32)
    tri = jnp.asarray(np.stack([s <= t, s >= t]), BF16)

    in_specs = [col_spec(0), col_spec(1), col_spec(2), col_spec(3), col_spec(4),
                pl.BlockSpec((2, DEPTH, hw), lambda b, h: (0, 0, h)),
                pl.BlockSpec((1, HG_D), lambda b, h: (0, 0)),
                pl.BlockSpec((HG_CHUNK, HG_CHUNK), lambda b, h: (0, 0)),
                pl.BlockSpec((2, HG_CHUNK, HG_CHUNK), lambda b, h: (0, 0, 0))]
    args = [proj] * 5 + [lb_logits, gain, code, tri]
    if not zero_init:
        in_specs.append(pl.BlockSpec((None, None, 2, nh, HG_D, HG_D), lambda b, h: (b, layer, 0, h, 0, 0)))
        args.append(state)
    aliases = {}
    if finals is None:
        fin_spec = pl.BlockSpec((None, DEPTH, 2, nh, HG_D, HG_D), lambda b, h: (b, 0, 0, h, 0, 0))
    else:
        fin_spec = pl.BlockSpec((None, None, 2, nh, HG_D, HG_D), lambda b, h: (b, layer, 0, h, 0, 0))
        aliases[len(args)] = 1
        in_specs.append(pl.BlockSpec(memory_space=pl.ANY))
        args.append(finals)
    n_ch = 2 * nh
    return pl.pallas_call(
        functools.partial(_hgrn_kernel, layer=layer, n_chunks=n_chunks, n_heads=nh, zero_init=zero_init),
        grid=(n_seq, n_hb),
        in_specs=in_specs,
        out_specs=[pl.BlockSpec((seq_len, hw), lambda b, h: (b, h)), fin_spec],
        out_shape=[jax.ShapeDtypeStruct((rows, HG_W), F32),
                   jax.ShapeDtypeStruct((n_seq, DEPTH, 2, HG_HEADS, HG_D, HG_D), F32)],
        input_output_aliases=aliases,
        scratch_shapes=[pltpu.VMEM((n_ch, HG_D, HG_D), F32), pltpu.VMEM((seq_len, hw), F32),
                        pltpu.VMEM((n_ch, 4, HG_CHUNK, HG_D), F32), pltpu.VMEM((n_ch, HG_CHUNK, HG_D), F32),
                        pltpu.VMEM((n_ch, HG_CHUNK, HG_CHUNK), BF16)],
        compiler_params=pltpu.CompilerParams(dimension_semantics=("parallel", "parallel"),
                                             vmem_limit_bytes=VMEM_LIMIT),
        name="hgrn2_mixer",
    )(*args)


def _s5_params_kernel(lr_ref, li_ref, ldt_ref, btr_ref, bti_ref, cr_ref, ci_ref, a_ref, bm_ref, cm_ref):
    for k in range(lr_ref.shape[0]):
        _s5_params_block(*(r.at[k] for r in (lr_ref, li_ref, ldt_ref, btr_ref, bti_ref, cr_ref, ci_ref,
                                             a_ref, bm_ref, cm_ref)))


def _s5_params_block(lr_ref, li_ref, ldt_ref, btr_ref, bti_ref, cr_ref, ci_ref, a_ref, bm_ref, cm_ref):
    sw = S5_SW
    lr = jnp.minimum(lr_ref[...], -1e-4)
    li = li_ref[...]
    dt = jnp.exp(ldt_ref[...])
    mag = jnp.exp(lr * dt)
    ab_re = mag * jnp.cos(li * dt)
    ab_im = mag * jnp.sin(li * dt)
    nr = ab_re - 1.0
    den = lr * lr + li * li
    z_re = (nr * lr + ab_im * li) / den
    z_im = (ab_im * lr - nr * li) / den

    p_idx = lax.broadcasted_iota(jnp.int32, (S5_P, sw), 0)
    col = lax.broadcasted_iota(jnp.int32, (S5_P, sw), 1)
    for g in range(S5_GB):
        place = (col == p_idx + g * S5_P).astype(BF16)
        zr, zi = z_re[g:g + 1, :], z_im[g:g + 1, :]
        btr, bti = btr_ref[g], bti_ref[g]
        rows = slice(g * S5_CH, (g + 1) * S5_CH)
        bm_ref[rows, :sw] = _dot((zr * btr - zi * bti).astype(BF16), place).astype(BF16)
        bm_ref[rows, sw:] = _dot((zr * bti + zi * btr).astype(BF16), place).astype(BF16)
        cm_ref[rows, :sw] = _dot(cr_ref[g].astype(BF16), place).astype(BF16)
        cm_ref[rows, sw:] = _dot((-ci_ref[g]).astype(BF16), place).astype(BF16)
        a_ref[:, g * S5_P:(g + 1) * S5_P] = jnp.broadcast_to(ab_re[g:g + 1, :], (S5_NSEQ, S5_P))
        a_ref[:, sw + g * S5_P:sw + (g + 1) * S5_P] = jnp.broadcast_to(ab_im[g:g + 1, :], (S5_NSEQ, S5_P))


def _s5_params(lam_re, lam_im, log_dt, b_re, b_im, c_re, c_im):
    nb = DEPTH * 2 * S5_NGB
    gp = (nb, S5_GB, S5_P)
    gcp = (nb, S5_GB, S5_CH, S5_P)
    bt_re = jnp.swapaxes(b_re, -1, -2).reshape(gcp)
    bt_im = jnp.swapaxes(b_im, -1, -2).reshape(gcp)
    ldt = jnp.broadcast_to(log_dt.reshape(nb, S5_GB, 1), gp)
    per = S5_NGB
    gp_spec = pl.BlockSpec((per, S5_GB, S5_P), lambda i: (i, 0, 0))
    gcp_spec = pl.BlockSpec((per, S5_GB, S5_CH, S5_P), lambda i: (i, 0, 0, 0))
    a, bmat, cmat = pl.pallas_call(
        _s5_params_kernel,
        grid=(nb // per,),
        in_specs=[gp_spec] * 3 + [gcp_spec] * 4,
        out_specs=[pl.BlockSpec((per, S5_NSEQ, 2 * S5_SW), lambda i: (i, 0, 0)),
                   pl.BlockSpec((per, LANES, 2 * S5_SW), lambda i: (i, 0, 0)),
                   pl.BlockSpec((per, LANES, 2 * S5_SW), lambda i: (i, 0, 0))],
        out_shape=[jax.ShapeDtypeStruct((nb, S5_NSEQ, 2 * S5_SW), F32),
                   jax.ShapeDtypeStruct((nb, LANES, 2 * S5_SW), BF16),
                   jax.ShapeDtypeStruct((nb, LANES, 2 * S5_SW), BF16)],
        compiler_params=pltpu.CompilerParams(dimension_semantics=("parallel",)),
        name="s5_params",
    )(lam_re.reshape(gp), lam_im.reshape(gp), ldt, bt_re, bt_im, c_re.reshape(gcp), c_im.reshape(gcp))
    lead = (DEPTH, 2, S5_NGB)
    return (a.reshape(lead + a.shape[1:]), bmat.reshape(lead + bmat.shape[1:]), cmat.reshape(lead + cmat.shape[1:]))


def _s5_kernel(*refs, chained):
    refs = list(refs)
    u_ref, bm_ref, cm_ref, a_ref, d_ref = refs[:5]
    rest = refs[5:]
    s0_ref = rest.pop(0) if chained else None
    y_ref = rest.pop(0)
    hfin_ref = None if chained else rest.pop(0)
    hbuf_e, hbuf_o, hb0, hb1, hst = rest
    hbufs, hb16s = (hbuf_e, hbuf_o), (hb0, hb1)
    ns, sw = S5_NSEQ, S5_SW
    half = ns // 2
    n_tc = S5_SEQ // S5_TC
    blk = S5_TC * ns
    dirs = (0, 1)

    for d in dirs:
        hst[d] = jnp.zeros((ns, 2 * sw), F32)

    def steps_of(d, i):
        return pl.ds(((n_tc - 1 - i) if d else i) * S5_TC, S5_TC)

    def kept(i):
        return i // 2 if chained else 0

    def project(i, par):
        for d in dirs:
            u = u_ref[steps_of(d, i)].reshape(blk, LANES)
            hbufs[par][d, kept(i)] = _dot(u.astype(BF16), bm_ref[d])

    def scan(i, par, emit):
        a = [(a_ref[d, :half, :sw], a_ref[d, :half, sw:]) for d in dirs]
        h = [[(hst[d, k * half:(k + 1) * half, :sw], hst[d, k * half:(k + 1) * half, sw:]) for k in range(2)]
             for d in dirs]
        for jj in range(S5_TC):
            for d in dirs:
                j = S5_TC - 1 - jj if d else jj
                ar, ai = a[d]
                bu = hbufs[par].at[d, kept(i)]
                for k in range(2):
                    r = slice(j * ns + k * half, j * ns + (k + 1) * half)
                    hr, hi = h[d][k]
                    h[d][k] = (ar * hr - ai * hi + bu[r, :sw], ar * hi + ai * hr + bu[r, sw:])
                if emit:
                    r = slice(j * ns, (j + 1) * ns)
                    hb16s[par][d, r, :sw] = jnp.concatenate([h[d][0][0], h[d][1][0]], axis=0).astype(BF16)
                    hb16s[par][d, r, sw:] = jnp.concatenate([h[d][0][1], h[d][1][1]], axis=0).astype(BF16)
        for d in dirs:
            for k in range(2):
                hst[d, k * half:(k + 1) * half, :sw] = h[d][k][0]
                hst[d, k * half:(k + 1) * half, sw:] = h[d][k][1]

    def readout(i, par):
        for d in dirs:
            steps = steps_of(d, i)
            y = _dot_nt(hb16s[par][d], cm_ref[d])
            y_ref[steps] = y_ref[steps] + y.reshape(S5_TC, ns, LANES)

    def skip(n, carry):
        steps = pl.ds(n * S5_TC, S5_TC)
        y_ref[steps] = d_ref[...] * u_ref[steps]
        return carry

    project(0, 0)
    if chained:
        def sweep(k, carry):
            i = 2 * k
            scan(i, 0, False)
            project(i + 1, 1)
            scan(i + 1, 1, False)
            project(i + 2, 0)
            return carry

        lax.fori_loop(0, n_tc // 2 - 1, sweep, 0)
        scan(n_tc - 2, 0, False)
        project(n_tc - 1, 1)
        scan(n_tc - 1, 1, False)

        n_long = s0_ref.shape[1]
        pieces = ns // n_long
        for d in dirs:
            pr, pi = a_ref[d, 0:1, :sw], a_ref[d, 0:1, sw:]
            for _ in range(int(math.log2(S5_SEQ))):
                pr, pi = pr * pr - pi * pi, 2.0 * (pr * pi)
            for b in range(n_long):
                hr, hi = s0_ref[d, b:b + 1, :sw], s0_ref[d, b:b + 1, sw:]
                for k in (range(pieces - 1, -1, -1) if d else range(pieces)):
                    r = b * pieces + k
                    zr, zi = hst[d, r:r + 1, :sw], hst[d, r:r + 1, sw:]
                    hst[d, r:r + 1, :sw] = hr
                    hst[d, r:r + 1, sw:] = hi
                    hr, hi = pr * hr - pi * hi + zr, pr * hi + pi * hr + zi

    hb1[...] = jnp.zeros(hb1.shape, BF16)
    lax.fori_loop(0, n_tc, skip, 0)

    def body(k, carry):
        i = 2 * k
        scan(i, 0, True)
        readout(jnp.maximum(i - 1, 0), 1)
        if not chained:
            project(i + 1, 1)
        scan(i + 1, 1, True)
        readout(i, 0)
        if not chained:
            project(jnp.minimum(i + 2, n_tc - 1), 0)
        return carry

    lax.fori_loop(0, n_tc // 2, body, 0)
    if not chained:
        for d in dirs:
            hfin_ref[d] = hst[d]
    readout(n_tc - 1, 1)


def _s5_call(u_tm, part, a, bmat, cmat, dskip, s0, layer):
    chained = s0 is not None
    tm_spec = pl.BlockSpec((S5_SEQ, S5_NSEQ, LANES), lambda g: (0, 0, g))

    def mat_spec(rows):
        return pl.BlockSpec((None, 2, None, rows, 2 * S5_SW), lambda g: (layer, 0, g, 0, 0))

    in_specs = [pl.BlockSpec((S5_SEQ, S5_NSEQ, LANES), lambda g: (0, part, g)),
                mat_spec(LANES), mat_spec(LANES), mat_spec(S5_NSEQ),
                pl.BlockSpec((None, None, 1, LANES), lambda g: (layer, g, 0, 0))]
    args = [u_tm, bmat, cmat, a, dskip]
    out_specs = [tm_spec]
    out_shape = [jax.ShapeDtypeStruct((S5_SEQ, S5_NSEQ, S5_W), F32)]
    if chained:
        in_specs.append(pl.BlockSpec((2, None, s0.shape[2], 2 * S5_SW), lambda g: (0, g, 0, 0)))
        args.append(s0)
    else:
        out_specs.append(pl.BlockSpec((2, None, S5_NSEQ, 2 * S5_SW), lambda g: (0, g, 0, 0)))
        out_shape.append(jax.ShapeDtypeStruct((2, S5_NGB, S5_NSEQ, 2 * S5_SW), F32))
    blk = S5_TC * S5_NSEQ
    kept = S5_SEQ // S5_TC // 2 if chained else 1
    res = pl.pallas_call(
        functools.partial(_s5_kernel, chained=chained),
        grid=(S5_NGB,),
        in_specs=in_specs,
        out_specs=out_specs,
        out_shape=out_shape,
        scratch_shapes=[pltpu.VMEM((2, kept, blk, 2 * S5_SW), F32),
                        pltpu.VMEM((2, kept, blk, 2 * S5_SW), F32),
                        pltpu.VMEM((2, blk, 2 * S5_SW), BF16),
                        pltpu.VMEM((2, blk, 2 * S5_SW), BF16),
                        pltpu.VMEM((2, S5_NSEQ, 2 * S5_SW), F32)],
        compiler_params=pltpu.CompilerParams(dimension_semantics=("parallel",),
                                             vmem_limit_bytes=VMEM_LIMIT),
        name="s5_scan",
    )(*args)
    return (res[0], None) if chained else (res[0], res[1])


def _out_kernel(x_ref, ohc_ref, ohs_ref, y5c_ref, y5s_ref, g1_ref, sh2_ref, sc2_ref, g2_ref, nffn_ref, nfin_ref,
                wglu_ref, wout_ref, wg_ref, wu_ref, wd_ref, *rest, last_layer):
    if last_layer:
        oc_ref, os_ref, wglu_b, wout_b = rest
    else:
        o_ref, wglu_b, wout_b = rest

    @pl.when(_first_step())
    def _():
        _cast_rows(wglu_ref, wglu_b)
        _cast_rows(wout_ref, wout_b)

    smp = _is_sample_tile()
    y = jnp.concatenate([jnp.where(smp, y5s_ref[:, s, :], y5c_ref[:, s, :]) for s in range(TILE_S)],
                        axis=0)
    y = _gelu_tanh(y)
    y = y * _sigmoid(_dot(y.astype(BF16), wglu_b[...]))
    ohg = jnp.where(smp, ohs_ref[...], ohc_ref[...]).reshape(TILE_ROWS, HG_W)
    mix = _dot(ohg.astype(BF16), wout_b[:HG_W, :]) + _dot(y.astype(BF16), wout_b[HG_W:, :])
    x = x_ref[...].reshape(TILE_ROWS, D_MODEL) + g1_ref[...] * mix
    h = _rms(x) * nffn_ref[...]
    h = (h * (1.0 + sc2_ref[...]) + sh2_ref[...]).astype(BF16)
    act = (_silu(_dot(h, wg_ref[...])) * _dot(h, wu_ref[...])).astype(BF16)
    x = x + g2_ref[...] * _dot(act, wd_ref[...])
    if not last_layer:
        o_ref[...] = x.reshape(o_ref.shape)
    else:
        x = (_rms(x) * nfin_ref[...]).reshape(oc_ref.shape)

        @pl.when(smp)
        def _():
            os_ref[...] = x

        @pl.when(jnp.logical_not(smp))
        def _():
            oc_ref[...] = x


def _out_call(x3, ohg_c, ohg_s, y5_c, y5_s, mod4, nffn, nfin, wglu, wout, wg, wu, wd, layer, last_layer):
    vec = pl.BlockSpec((1, D_MODEL), lambda sb, tb: (0, 0))
    part_shape = jax.ShapeDtypeStruct((S5_NSEQ, S5_SEQ, D_MODEL), F32)
    if last_layer:
        out_specs = [_part_tile_spec(D_MODEL, _ctx_index), _part_tile_spec(D_MODEL, _smp_index)]
        out_shape = [part_shape, part_shape]
    else:
        out_specs = _tile_spec(D_MODEL)
        out_shape = jax.ShapeDtypeStruct(x3.shape, F32)
    return pl.pallas_call(
        functools.partial(_out_kernel, last_layer=last_layer),
        grid=(ALL_SEQ // TILE_S, TIME_TILES),
        in_specs=[_tile_spec(D_MODEL),
                  _part_tile_spec(HG_W, _ctx_index), _part_tile_spec(HG_W, _smp_index),
                  _part_tm_tile_spec(S5_W, _ctx_index), _part_tm_tile_spec(S5_W, _smp_index),
                  _mod_spec(layer, 2), _mod_spec(layer, 3), _mod_spec(layer, 4), _mod_spec(layer, 5),
                  vec, vec,
                  _layer_spec((S5_W, S5_W), layer), _layer_spec((D_MODEL, D_MODEL), layer),
                  _whole_spec((D_MODEL, D_FF)), _whole_spec((D_MODEL, D_FF)), _whole_spec((D_FF, D_MODEL))],
        out_specs=out_specs,
        out_shape=out_shape,
        scratch_shapes=[pltpu.VMEM((S5_W, S5_W), BF16), pltpu.VMEM((D_MODEL, D_MODEL), BF16)],
        compiler_params=pltpu.CompilerParams(dimension_semantics=("arbitrary", "arbitrary"),
                                             vmem_limit_bytes=VMEM_LIMIT),
        name="out_ffn",
    )(x3, ohg_c, ohg_s, y5_c, y5_s, mod4, mod4, mod4, mod4, nffn, nfin, wglu, wout, wg, wu, wd)


def _s5_state_to_blocks(s):
    n = s.shape[0]
    s = s.reshape(n, 2, S5_NGB, S5_GB, S5_P, 2)
    return jnp.transpose(s, (1, 2, 0, 5, 3, 4)).reshape(2, S5_NGB, n, 2 * S5_SW)


def _s5_blocks_to_state(h):
    n = h.shape[2]
    h = h.reshape(2, S5_NGB, n, 2, S5_GB, S5_P)
    return jnp.transpose(h, (2, 0, 1, 4, 5, 3)).reshape(n, 2, S5_GROUPS, S5_P, 2)


def kernel(x_prompt, x_sample, state_hgrn, state_s5, c, c_ctx, w_mod, b_mod, norm_mix, norm_ffn, norm_final, w_in, w_out, hg_lb_logits, hg_norm, s5_lam_re, s5_lam_im, s5_log_dt, s5_b_re, s5_b_im, s5_c_re, s5_c_im, s5_d, s5_w_glu, w_gate, w_up, w_down):
    n_ctx, ctx_len, _ = x_prompt.shape
    n_dec, dec_len, _ = x_sample.shape
    assert ctx_len == S5_SEQ and n_ctx == S5_NSEQ and n_dec * dec_len == S5_NSEQ * S5_SEQ

    cond = jnp.concatenate([c_ctx[None, :], c, jnp.zeros((SUBLANES - 1 - n_dec, D_MODEL), F32)], axis=0)
    mod4 = _mod_call(cond, w_mod, b_mod).reshape(DEPTH, SUBLANES, 1, 6 * D_MODEL)

    s5_a, s5_bmat, s5_cmat = _s5_params(s5_lam_re, s5_lam_im, s5_log_dt, s5_b_re, s5_b_im, s5_c_re, s5_c_im)
    s5_dskip = s5_d.reshape(DEPTH, S5_NGB, 1, LANES)
    nfin = norm_final.reshape(1, D_MODEL)

    assert dec_len // S5_SEQ == TILE_S and n_dec + 1 <= SUBLANES
    tok = (S5_NSEQ, S5_SEQ, D_MODEL)
    ctx_rows = S5_NSEQ * S5_SEQ
    xs = (x_prompt.reshape(tok), x_sample.reshape(tok))
    ctx_fin, smp_fin, s5_finals = None, None, []
    for l in range(DEPTH):
        proj3, u_tm, x_all, (w_gate_b, w_up_b, w_down_b) = _in_call(
            xs, norm_mix[l].reshape(1, D_MODEL), mod4, w_in, (w_gate, w_up, w_down), l)
        proj = proj3.reshape(ALL_SEQ * S5_SEQ, HG_IN_W)
        gain = hg_norm[l].reshape(1, HG_D)
        ohg_c, ctx_fin = _hgrn_call(proj, 0, hg_lb_logits, gain, None, ctx_fin, l, n_ctx, ctx_len)
        ohg_s, smp_fin = _hgrn_call(proj, ctx_rows, hg_lb_logits, gain, state_hgrn, smp_fin, l, n_dec, dec_len)
        y5_c, s5_fin = _s5_call(u_tm, 0, s5_a, s5_bmat, s5_cmat, s5_dskip, None, l)
        y5_s, _ = _s5_call(u_tm, 1, s5_a, s5_bmat, s5_cmat, s5_dskip, _s5_state_to_blocks(state_s5[:, l]), l)
        last = l == DEPTH - 1
        res = _out_call(x_all, ohg_c.reshape(S5_NSEQ, S5_SEQ, HG_W), ohg_s.reshape(S5_NSEQ, S5_SEQ, HG_W),
                        y5_c, y5_s, mod4, norm_ffn[l].reshape(1, D_MODEL), nfin,
                        s5_w_glu, w_out, w_gate_b, w_up_b, w_down_b, l, last)
        xs = res if last else (res,)
        s5_finals.append(_s5_blocks_to_state(s5_fin))
    y_prompt, y_sample = xs
    return (y_prompt.reshape(x_prompt.shape), y_sample.reshape(x_sample.shape),
            ctx_fin, jnp.stack(s5_finals, axis=1))
```

```python
import functools
import math

import jax
import jax.numpy as jnp
import numpy as np
from jax import lax
from jax.experimental import pallas as pl
from jax.experimental.pallas import tpu as pltpu

F32 = jnp.float32
BF16 = jnp.bfloat16

LANES = 128
SUBLANES = 8

D_MODEL = 1024
DEPTH = 2
GRID_W = 64
HG_W = 512
HG_HEADS = 4
HG_D = HG_W // HG_HEADS
S5_W = 512
S5_CH = 16
S5_GROUPS = S5_W // S5_CH
S5_P = 64
S5_GB = LANES // S5_CH
S5_NGB = S5_GROUPS // S5_GB
S5_SW = S5_GB * S5_P
HG_IN_W = 5 * HG_W
IN_W = HG_IN_W + S5_W
HG_PROJ_W = 4 * HG_W
D_FF = 2816
EPS = 1e-6

HG_CHUNK = 128
HG_LEVELS = (64, 32, 16, 8, 4, 2, 1)
S5_SEQ = 256
S5_NSEQ = 16
S5_TC = 16

TILE_S = SUBLANES
TILE_T = 64
TILE_ROWS = TILE_S * TILE_T
TIME_TILES = S5_SEQ // TILE_T
ALL_SEQ = 2 * S5_NSEQ
CTX_TILES = S5_NSEQ // TILE_S
X_RING = 3
CAST_ROWS = 128
MOD_TILE_N = 1536
VMEM_LIMIT = 56 * 1024 * 1024


def _sigmoid(x):
    return 1.0 / (1.0 + jnp.exp(-x))


def _silu(x):
    return x * _sigmoid(x)


def _gelu_tanh(x):
    return 0.5 * x * (1.0 + jnp.tanh(math.sqrt(2.0 / math.pi) * (x + 0.044715 * (x * x * x))))


def _rms(x):
    return x * lax.rsqrt(jnp.mean(x * x, axis=-1, keepdims=True) + EPS)


def _dot(a, b):
    return jnp.dot(a, b, preferred_element_type=F32)


def _dot_nt(a, b):
    return lax.dot_general(a, b, (((1,), (1,)), ((), ())), preferred_element_type=F32)


def _dot_tn(a, b):
    return lax.dot_general(a, b, (((0,), (0,)), ((), ())), preferred_element_type=F32)


def _whole_spec(shape):
    return pl.BlockSpec(tuple(shape), lambda *_: (0,) * len(shape), pipeline_mode=pl.Buffered(1))


def _layer_spec(shape, layer):
    nd = len(shape)
    return pl.BlockSpec((None,) + tuple(shape), lambda *_: (layer,) + (0,) * nd, pipeline_mode=pl.Buffered(1))


def _mod_kernel(cond_ref, w_ref, b_ref, o_ref):
    a = _silu(cond_ref[...]).astype(BF16)
    o_ref[0] = _dot(a, w_ref[0].astype(BF16)) + b_ref[0]


def _mod_call(cond, w_mod, b_mod):
    n_cond = cond.shape[0]
    n_out = w_mod.shape[-1]
    return pl.pallas_call(
        _mod_kernel,
        grid=(DEPTH, n_out // MOD_TILE_N),
        in_specs=[
            pl.BlockSpec((n_cond, D_MODEL), lambda l, j: (0, 0)),
            pl.BlockSpec((1, D_MODEL, MOD_TILE_N), lambda l, j: (l, 0, j)),
            pl.BlockSpec((1, 1, MOD_TILE_N), lambda l, j: (l, 0, j)),
        ],
        out_specs=pl.BlockSpec((1, n_cond, MOD_TILE_N), lambda l, j: (l, 0, j)),
        out_shape=jax.ShapeDtypeStruct((DEPTH, n_cond, n_out), F32),
        compiler_params=pltpu.CompilerParams(dimension_semantics=("parallel", "parallel"),
                                             vmem_limit_bytes=VMEM_LIMIT),
        name="adaln_mod",
    )(cond, w_mod, b_mod.reshape(DEPTH, 1, n_out))


def _first_step():
    return jnp.logical_and(pl.program_id(0) == 0, pl.program_id(1) == 0)


def _cast_rows(src_ref, dst_ref):
    for r in range(0, src_ref.shape[0], CAST_ROWS):
        dst_ref[r:r + CAST_ROWS, :] = src_ref[r:r + CAST_ROWS, :].astype(BF16)


def _grid_pos_tile(omega, tb):
    nf = omega.shape[-1]
    s_idx = lax.broadcasted_iota(jnp.int32, (TILE_S, nf), 0)
    j_idx = lax.broadcasted_iota(jnp.int32, (TILE_T, nf), 0)
    t0 = tb * TILE_T
    row = (s_idx * (S5_SEQ // GRID_W) + t0 // GRID_W).astype(F32) * omega
    col = (j_idx + t0 % GRID_W).astype(F32) * omega
    enc_r = jnp.concatenate([jnp.sin(row), jnp.cos(row)], axis=-1)
    enc_c = jnp.concatenate([jnp.sin(col), jnp.cos(col)], axis=-1)
    shape = (TILE_S, TILE_T, 2 * nf)
    return jnp.concatenate([jnp.broadcast_to(enc_r[:, None, :], shape),
                            jnp.broadcast_to(enc_c[None, :, :], shape)], axis=-1)


def _is_sample_tile():
    return pl.program_id(0) >= CTX_TILES


def _x_tile_copy(srcs, buf_ref, sem_ref, t, start):
    slot = t % X_RING
    sb, tb = t // TIME_TILES, t % TIME_TILES

    def run(src, sb_local):
        cp = pltpu.make_async_copy(
            src.at[pl.ds(sb_local * TILE_S, TILE_S), pl.ds(tb * TILE_T, TILE_T), :],
            buf_ref.at[slot], sem_ref.at[slot])
        if start:
            cp.start()
        else:
            cp.wait()

    if len(srcs) == 1:
        run(srcs[0], sb)
    else:
        pl.when(sb < CTX_TILES)(lambda: run(srcs[0], sb))
        pl.when(sb >= CTX_TILES)(lambda: run(srcs[1], sb - CTX_TILES))


def _in_kernel(*refs, first_layer):
    n_in = 10 if first_layer else 8
    for src, dst in zip(refs[n_in - 3:n_in], refs[-6:-3]):
        dst[...] = src[...].astype(BF16)
    refs = refs[:n_in - 3] + refs[n_in:-6] + refs[-3:]

    n_steps = pl.num_programs(0) * TIME_TILES
    step = pl.program_id(0) * TIME_TILES + pl.program_id(1)
    srcs = refs[:2] if first_layer else refs[:1]
    xbuf_ref, xsem_ref = refs[-2:]

    @pl.when(step == 0)
    def _():
        for t in range(X_RING - 1):
            _x_tile_copy(srcs, xbuf_ref, xsem_ref, jnp.int32(t), True)

    @pl.when(step + (X_RING - 1) < n_steps)
    def _():
        _x_tile_copy(srcs, xbuf_ref, xsem_ref, step + (X_RING - 1), True)

    _x_tile_copy(srcs, xbuf_ref, xsem_ref, step, False)
    x = xbuf_ref[step % X_RING]

    if first_layer:
        _, _, om_ref, gain_ref, sh_ref, sc_ref, w_ref, proj_ref, u_ref, v_ref, xo_ref, wb_ref = refs[:-2]
        x = jnp.where(_is_sample_tile(), x + _grid_pos_tile(om_ref[...], pl.program_id(1)), x)
        xo_ref[...] = x
    else:
        _, gain_ref, sh_ref, sc_ref, w_ref, proj_ref, u_ref, v_ref, wb_ref = refs[:-2]

    @pl.when(_first_step())
    def _():
        _cast_rows(w_ref, wb_ref)

    x = x.reshape(TILE_ROWS, D_MODEL)
    h = _rms(x) * gain_ref[...]
    h = (h * (1.0 + sc_ref[...]) + sh_ref[...]).astype(BF16)
    u = _dot(h, wb_ref[:, HG_IN_W:])
    for s in range(TILE_S):
        u_ref[:, s, :] = u[s * TILE_T:(s + 1) * TILE_T, :]
    p = _dot(h, wb_ref[:, :HG_IN_W])
    tile = (TILE_S, TILE_T)
    proj_ref[:, :, :3 * HG_W] = p[:, :3 * HG_W].reshape(tile + (3 * HG_W,))
    proj_ref[:, :, 3 * HG_W:] = p[:, 4 * HG_W:].reshape(tile + (HG_W,))
    v_ref[...] = p[:, 3 * HG_W:4 * HG_W].astype(BF16).reshape(v_ref.shape)


def _tile_spec(width):
    return pl.BlockSpec((TILE_S, TILE_T, width), lambda sb, tb: (sb, tb, 0))


def _tm_tile_spec(width):
    return pl.BlockSpec((TILE_T, TILE_S, width), lambda sb, tb: (tb, sb, 0))


def _ctx_index(sb, tb):
    on = sb < CTX_TILES
    return jnp.where(on, sb, CTX_TILES - 1), jnp.where(on, tb, TIME_TILES - 1)


def _smp_index(sb, tb):
    on = sb >= CTX_TILES
    return jnp.where(on, sb - CTX_TILES, 0), jnp.where(on, tb, 0)


def _part_tile_spec(width, index):
    return pl.BlockSpec((TILE_S, TILE_T, width), lambda sb, tb: index(sb, tb) + (0,))


def _part_tm_tile_spec(width, index):
    return pl.BlockSpec((TILE_T, TILE_S, width), lambda sb, tb: index(sb, tb)[::-1] + (0,))


def _mod_spec(layer, col):
    return pl.BlockSpec((None, None, 1, D_MODEL),
                        lambda sb, tb: (layer, jnp.maximum(sb - (CTX_TILES - 1), 0), 0, col))


def _in_call(xs, gain, mod4, w_in, ffn_w, layer):
    first_layer = len(xs) == 2
    n_steps = (ALL_SEQ // TILE_S) * TIME_TILES
    if first_layer:
        assert GRID_W % TILE_T == 0 and S5_SEQ % GRID_W == 0
        nf = D_MODEL // 4
        omega = 1.0 / (np.float32(10000.0) ** (np.arange(nf, dtype=np.float32) / np.float32(nf)))
        in_specs = [pl.BlockSpec(memory_space=pl.ANY), pl.BlockSpec(memory_space=pl.ANY),
                    pl.BlockSpec((1, nf), lambda sb, tb: (0, 0))]
        args = list(xs) + [jnp.asarray(omega.reshape(1, nf), F32)]
    else:
        in_specs = [pl.BlockSpec(memory_space=pl.ANY)]
        args = list(xs)
    in_specs += [
        pl.BlockSpec((1, D_MODEL), lambda sb, tb: (0, 0)),
        _mod_spec(layer, 0),
        _mod_spec(layer, 1),
        _layer_spec((D_MODEL, IN_W), layer),
    ]
    args += [gain, mod4, mod4, w_in]
    out_specs = [_tile_spec(HG_PROJ_W), _tm_tile_spec(S5_W), _tile_spec(HG_W)]
    out_shape = [jax.ShapeDtypeStruct((ALL_SEQ, S5_SEQ, HG_PROJ_W), F32),
                 jax.ShapeDtypeStruct((S5_SEQ, ALL_SEQ, S5_W), F32),
                 jax.ShapeDtypeStruct((ALL_SEQ, S5_SEQ, HG_W), BF16)]
    if first_layer:
        out_specs.append(_tile_spec(D_MODEL))
        out_shape.append(jax.ShapeDtypeStruct((ALL_SEQ, S5_SEQ, D_MODEL), F32))
    for w in ffn_w:
        _, n_rows, n_cols = w.shape
        slab = n_rows // n_steps
        assert slab * n_steps == n_rows and slab % (2 * SUBLANES) == 0
        in_specs.append(pl.BlockSpec((None, slab, n_cols), lambda sb, tb: (layer, sb * TIME_TILES + tb, 0)))
        args.append(w)
        out_specs.append(pl.BlockSpec((slab, n_cols), lambda sb, tb: (sb * TIME_TILES + tb, 0)))
        out_shape.append(jax.ShapeDtypeStruct((n_rows, n_cols), BF16))
    res = pl.pallas_call(
        functools.partial(_in_kernel, first_layer=first_layer),
        grid=(ALL_SEQ // TILE_S, TIME_TILES),
        in_specs=in_specs,
        out_specs=out_specs,
        out_shape=out_shape,
        scratch_shapes=[pltpu.VMEM((D_MODEL, IN_W), BF16),
                        pltpu.VMEM((X_RING, TILE_S, TILE_T, D_MODEL), F32),
                        pltpu.SemaphoreType.DMA((X_RING,))],
        compiler_params=pltpu.CompilerParams(dimension_semantics=("arbitrary", "arbitrary"),
                                             vmem_limit_bytes=VMEM_LIMIT),
        name="in_proj",
    )(*args)
    n_w = len(ffn_w)
    return res[0], res[1], res[2], (res[3] if first_layer else xs[0]), res[len(res) - n_w:]


def _pair_boundary(b, m, rev):
    c = b.shape[0]
    span = 2 * m
    at = m if rev else m - 1
    if span >= SUBLANES:
        b3 = b.reshape(c // span, span, LANES)
        return jnp.broadcast_to(b3[:, at:at + 1, :], b3.shape).reshape(c, LANES)
    b3 = b.reshape(c // SUBLANES, SUBLANES, LANES)
    sub = lax.broadcasted_iota(jnp.int32, b3.shape, 1)
    out = None
    for p in range(SUBLANES // span):
        piece = jnp.broadcast_to(b3[:, p * span + at:p * span + at + 1, :], b3.shape)
        out = piece if out is None else jnp.where(sub >= p * span, piece, out)
    return out.reshape(c, LANES)


def _neg_abs(x):
    bits = lax.bitcast_convert_type(x, jnp.uint32) | jnp.uint32(0x80000000)
    return lax.bitcast_convert_type(bits, F32)


def _hg_gates(chains, scale):
    outs = []
    for q, fl, lb, tri in chains:
        sig = _sigmoid(fl)
        forget = lb + (1.0 - lb) * sig
        logf = jnp.log2(forget)
        key = (1.0 - lb) * (1.0 - sig)
        hi = logf.astype(BF16)
        r1 = logf - hi.astype(F32)
        mid = r1.astype(BF16)
        lo = (r1 - mid.astype(F32)).astype(BF16)
        parts = _dot(tri, jnp.concatenate([hi, mid, lo], axis=1))
        b2 = parts[:, :LANES] + parts[:, LANES:2 * LANES] + parts[:, 2 * LANES:]
        outs.append((_silu(q) * scale, key, b2, forget))
    return outs


def _hg_scores(chains, code, eye, from_zero=False):
    c = chains[0][0].shape[0]
    o_inter = []
    for qh, key, b2, forget, v, st_ref, rev in chains:
        b_edge = b2[0:1, :] if rev else b2[c - 1:c, :]
        k_end = key * jnp.exp2(b_edge - b2)
        grown = _dot_tn(v.astype(BF16), k_end.astype(BF16))
        if from_zero:
            o_inter.append(None)
            st_ref[...] = grown
        else:
            st = st_ref[...]
            o_inter.append(_dot_nt((qh * jnp.exp2(b2)).astype(BF16), st.astype(BF16)))
            st_ref[...] = jnp.exp2(b_edge) * st + grown

    out = []
    for o, (qh, key, b2, forget, v, st_ref, rev) in zip(o_inter, chains):
        scores = jnp.where(eye, jnp.sum(qh * key, axis=-1, keepdims=True), 0.0)
        qb, kb = qh.astype(BF16), key.astype(BF16)
        for m in HG_LEVELS:
            k = int(math.log2(m)) + 1
            if m == 1:
                p = _dot_nt((qh * forget).astype(BF16), kb)
            elif m >= SUBLANES:
                g = c // (2 * m)
                t_half = slice(0, m) if rev else slice(m, 2 * m)
                s_half = slice(m, 2 * m) if rev else slice(0, m)
                at = m if rev else m - 1
                b3, q3, k3 = (x.reshape(g, 2 * m, LANES) for x in (b2, qh, key))
                seam = b3[:, at:at + 1, :]
                qt = (q3[:, t_half] * jnp.exp2(b3[:, t_half] - seam)).reshape(c // 2, LANES)
                ks = k3[:, s_half] * jnp.exp2(seam - b3[:, s_half])
                kk = jnp.concatenate([k3[:, t_half], ks] if rev else [ks, k3[:, t_half]], axis=1).reshape(c, LANES)
                p = _dot_nt(qt.astype(BF16), kk.astype(BF16)).reshape(g, m, c)
                s3 = scores.reshape(g, 2 * m, c)
                hit = code.reshape(g, 2 * m, c)[:, t_half] == (-k if rev else k)
                st = jnp.where(hit, p, s3[:, t_half])
                scores = jnp.concatenate([st, s3[:, s_half]] if rev else [s3[:, s_half], st], axis=1).reshape(c, c)
                continue
            else:
                e = jnp.exp2(_neg_abs(b2 - _pair_boundary(b2, m, rev))).astype(BF16)
                p = _dot_nt(qb * e, kb * e)
            scores = jnp.where(code == (-k if rev else k), p, scores)
        out.append((o, scores.astype(BF16)))
    return out


def _hgrn_kernel(*refs, layer, n_chunks, n_heads, zero_init):
    refs = list(refs)
    q_ref, ff_ref, fb_ref, v_ref, g_ref, lbl_ref, gain_ref, code_ref, tri_ref = refs[:9]
    s0_ref = None if zero_init else refs[9]
    o_ref, sfin_ref, st_ref, ob_ref, ab_ref, oi_ref, sc_ref = refs[-7:]
    c = HG_CHUNK
    code = code_ref[...]
    eye = code == 0
    chains = [(hd, d) for hd in range(n_heads) for d in (0, 1)]

    def lanes(hd):
        return slice(hd * HG_D, (hd + 1) * HG_D)

    def lower_bound(hd, d):
        lg = lbl_ref[d, :, lanes(hd)]
        ex = jnp.exp(lg - jnp.max(lg, axis=0, keepdims=True))
        soft = ex / jnp.sum(ex, axis=0, keepdims=True)
        return jnp.sum(soft[:layer + 1], axis=0, keepdims=True) - soft[0:1]

    lb = [lower_bound(hd, d) for hd, d in chains]
    scale = HG_D ** -0.5

    def rows_of(d, i):
        n = (n_chunks - 1 - i) if d else i
        return pl.ds(n * c if isinstance(n, int) else pl.multiple_of(n * c, c), c)

    for ch, (hd, d) in enumerate(chains):
        st_ref[ch] = jnp.zeros((HG_D, HG_D), F32) if zero_init else s0_ref[d, hd].T

    def gates(i):
        return _hg_gates([(q_ref[rows_of(d, i), lanes(hd)], (fb_ref if d else ff_ref)[rows_of(d, i), lanes(hd)],
                           lb[ch], tri_ref[d]) for ch, (hd, d) in enumerate(chains)], scale)

    def scores(i, ab, from_zero=False):
        return _hg_scores([ab[ch] + (v_ref[rows_of(d, i), lanes(hd)], st_ref.at[ch], bool(d))
                           for ch, (hd, d) in enumerate(chains)], code, eye, from_zero)

    def emit(i, oi_sc):
        for ch, (hd, d) in enumerate(chains):
            rows = rows_of(d, i)
            o_intra = _dot(oi_sc[ch][1], v_ref[rows, lanes(hd)].astype(BF16))
            (ob_ref if d else o_ref)[rows, lanes(hd)] = o_intra if oi_sc[ch][0] is None else oi_sc[ch][0] + o_intra

    if n_chunks <= 2:
        ab = [gates(i) for i in range(n_chunks)]
        for i in range(n_chunks):
            emit(i, scores(i, ab[i], from_zero=zero_init and i == 0))
    else:
        n_ab = ab_ref.shape[1]

        def put_ab(ab):
            for ch in range(len(chains)):
                for k in range(n_ab):
                    ab_ref[ch, k] = ab[ch][k]

        def get_oi_sc():
            return [(oi_ref[ch], sc_ref[ch]) for ch in range(len(chains))]

        put_ab(gates(0))
        oi_ref[...] = jnp.zeros(oi_ref.shape, F32)
        sc_ref[...] = jnp.zeros(sc_ref.shape, BF16)

        def body(i, carry):
            emit(jnp.maximum(i - 1, 0), get_oi_sc())
            res = scores(i, [tuple(ab_ref[ch, k] for k in range(n_ab)) for ch in range(len(chains))])
            for ch in range(len(chains)):
                oi_ref[ch] = res[ch][0]
                sc_ref[ch] = res[ch][1]
            put_ab(gates(jnp.minimum(i + 1, n_chunks - 1)))
            return carry

        lax.fori_loop(0, n_chunks, body, 0)
        emit(n_chunks - 1, get_oi_sc())
    if len(sfin_ref.shape) == 5:
        for other in range(sfin_ref.shape[0]):
            if other != layer:
                sfin_ref[other] = jnp.zeros(sfin_ref.shape[1:], F32)
        sfin_ref = sfin_ref.at[layer]
    for ch, (hd, d) in enumerate(chains):
        sfin_ref[d, hd] = st_ref[ch].T

    def finish(n, carry):
        rows = pl.ds(pl.multiple_of(n * (2 * c), 2 * c), 2 * c)
        for hd in range(n_heads):
            o_ref[rows, lanes(hd)] = (_rms(o_ref[rows, lanes(hd)] + ob_ref[rows, lanes(hd)]) * gain_ref[...]
                                      * _silu(g_ref[rows, lanes(hd)]))
        return carry

    lax.fori_loop(0, n_chunks // 2, finish, 0)


def _hgrn_call(proj, vals, row0, lb_logits, gain, state, finals, layer, n_seq, seq_len):
    rows = n_seq * seq_len
    seq0 = row0 // seq_len
    zero_init = state is None
    n_chunks = seq_len // HG_CHUNK
    assert n_chunks % 2 == 0
    nh = HG_HEADS if n_chunks <= 2 else 2
    hw = nh * HG_D
    n_hb = HG_HEADS // nh

    def col_spec(k):
        return pl.BlockSpec((seq_len, hw), lambda b, h: (seq0 + b, k * n_hb + h))

    t, s = np.meshgrid(np.arange(HG_CHUNK), np.arange(HG_CHUNK), indexing="ij")
    lvl = np.where(t == s, 0, np.floor(np.log2(np.maximum(t ^ s, 1))).astype(np.int32) + 1)
    code = jnp.asarray(np.where(t > s, lvl, -lvl), jnp.int32)
    tri = jnp.asarray(np.stack([s <= t, s >= t]), BF16)

    in_specs = [col_spec(0), col_spec(1), col_spec(2), col_spec(0), col_spec(3),
                pl.BlockSpec((2, DEPTH, hw), lambda b, h: (0, 0, h)),
                pl.BlockSpec((1, HG_D), lambda b, h: (0, 0)),
                pl.BlockSpec((HG_CHUNK, HG_CHUNK), lambda b, h: (0, 0)),
                pl.BlockSpec((2, HG_CHUNK, HG_CHUNK), lambda b, h: (0, 0, 0))]
    args = [proj] * 3 + [vals, proj, lb_logits, gain, code, tri]
    if not zero_init:
        in_specs.append(pl.BlockSpec((None, None, 2, nh, HG_D, HG_D), lambda b, h: (b, layer, 0, h, 0, 0)))
        args.append(state)
    aliases = {}
    if finals is None:
        fin_spec = pl.BlockSpec((None, DEPTH, 2, nh, HG_D, HG_D), lambda b, h: (b, 0, 0, h, 0, 0))
    else:
        fin_spec = pl.BlockSpec((None, None, 2, nh, HG_D, HG_D), lambda b, h: (b, layer, 0, h, 0, 0))
        aliases[len(args)] = 1
        in_specs.append(pl.BlockSpec(memory_space=pl.ANY))
        args.append(finals)
    n_ch = 2 * nh
    return pl.pallas_call(
        functools.partial(_hgrn_kernel, layer=layer, n_chunks=n_chunks, n_heads=nh, zero_init=zero_init),
        grid=(n_seq, n_hb),
        in_specs=in_specs,
        out_specs=[pl.BlockSpec((seq_len, hw), lambda b, h: (b, h)), fin_spec],
        out_shape=[jax.ShapeDtypeStruct((rows, HG_W), F32),
                   jax.ShapeDtypeStruct((n_seq, DEPTH, 2, HG_HEADS, HG_D, HG_D), F32)],
        input_output_aliases=aliases,
        scratch_shapes=[pltpu.VMEM((n_ch, HG_D, HG_D), F32), pltpu.VMEM((seq_len, hw), F32),
                        pltpu.VMEM((n_ch, 4, HG_CHUNK, HG_D), F32), pltpu.VMEM((n_ch, HG_CHUNK, HG_D), F32),
                        pltpu.VMEM((n_ch, HG_CHUNK, HG_CHUNK), BF16)],
        compiler_params=pltpu.CompilerParams(dimension_semantics=("parallel", "parallel"),
                                             vmem_limit_bytes=VMEM_LIMIT),
        name="hgrn2_mixer",
    )(*args)


def _s5_params_kernel(lr_ref, li_ref, ldt_ref, btr_ref, bti_ref, cr_ref, ci_ref, a_ref, bm_ref, cm_ref):
    for k in range(lr_ref.shape[0]):
        _s5_params_block(*(r.at[k] for r in (lr_ref, li_ref, ldt_ref, btr_ref, bti_ref, cr_ref, ci_ref,
                                             a_ref, bm_ref, cm_ref)))


def _s5_params_block(lr_ref, li_ref, ldt_ref, btr_ref, bti_ref, cr_ref, ci_ref, a_ref, bm_ref, cm_ref):
    sw = S5_SW
    lr = jnp.minimum(lr_ref[...], -1e-4)
    li = li_ref[...]
    dt = jnp.exp(ldt_ref[...])
    mag = jnp.exp(lr * dt)
    ab_re = mag * jnp.cos(li * dt)
    ab_im = mag * jnp.sin(li * dt)
    nr = ab_re - 1.0
    den = lr * lr + li * li
    z_re = (nr * lr + ab_im * li) / den
    z_im = (ab_im * lr - nr * li) / den

    p_idx = lax.broadcasted_iota(jnp.int32, (S5_P, sw), 0)
    col = lax.broadcasted_iota(jnp.int32, (S5_P, sw), 1)
    for g in range(S5_GB):
        place = (col == p_idx + g * S5_P).astype(BF16)
        zr, zi = z_re[g:g + 1, :], z_im[g:g + 1, :]
        btr, bti = btr_ref[g], bti_ref[g]
        rows = slice(g * S5_CH, (g + 1) * S5_CH)
        bm_ref[rows, :sw] = _dot((zr * btr - zi * bti).astype(BF16), place).astype(BF16)
        bm_ref[rows, sw:] = _dot((zr * bti + zi * btr).astype(BF16), place).astype(BF16)
        cm_ref[rows, :sw] = _dot(cr_ref[g].astype(BF16), place).astype(BF16)
        cm_ref[rows, sw:] = _dot((-ci_ref[g]).astype(BF16), place).astype(BF16)
        a_ref[:, g * S5_P:(g + 1) * S5_P] = jnp.broadcast_to(ab_re[g:g + 1, :], (S5_NSEQ, S5_P))
        a_ref[:, sw + g * S5_P:sw + (g + 1) * S5_P] = jnp.broadcast_to(ab_im[g:g + 1, :], (S5_NSEQ, S5_P))


def _s5_params(lam_re, lam_im, log_dt, b_re, b_im, c_re, c_im):
    nb = DEPTH * 2 * S5_NGB
    gp = (nb, S5_GB, S5_P)
    gcp = (nb, S5_GB, S5_CH, S5_P)
    bt_re = jnp.swapaxes(b_re, -1, -2).reshape(gcp)
    bt_im = jnp.swapaxes(b_im, -1, -2).reshape(gcp)
    ldt = jnp.broadcast_to(log_dt.reshape(nb, S5_GB, 1), gp)
    per = S5_NGB
    gp_spec = pl.BlockSpec((per, S5_GB, S5_P), lambda i: (i, 0, 0))
    gcp_spec = pl.BlockSpec((per, S5_GB, S5_CH, S5_P), lambda i: (i, 0, 0, 0))
    a, bmat, cmat = pl.pallas_call(
        _s5_params_kernel,
        grid=(nb // per,),
        in_specs=[gp_spec] * 3 + [gcp_spec] * 4,
        out_specs=[pl.BlockSpec((per, S5_NSEQ, 2 * S5_SW), lambda i: (i, 0, 0)),
                   pl.BlockSpec((per, LANES, 2 * S5_SW), lambda i: (i, 0, 0)),
                   pl.BlockSpec((per, LANES, 2 * S5_SW), lambda i: (i, 0, 0))],
        out_shape=[jax.ShapeDtypeStruct((nb, S5_NSEQ, 2 * S5_SW), F32),
                   jax.ShapeDtypeStruct((nb, LANES, 2 * S5_SW), BF16),
                   jax.ShapeDtypeStruct((nb, LANES, 2 * S5_SW), BF16)],
        compiler_params=pltpu.CompilerParams(dimension_semantics=("parallel",)),
        name="s5_params",
    )(lam_re.reshape(gp), lam_im.reshape(gp), ldt, bt_re, bt_im, c_re.reshape(gcp), c_im.reshape(gcp))
    lead = (DEPTH, 2, S5_NGB)
    return (a.reshape(lead + a.shape[1:]), bmat.reshape(lead + bmat.shape[1:]), cmat.reshape(lead + cmat.shape[1:]))


def _s5_kernel(*refs, chained):
    refs = list(refs)
    u_ref, bm_ref, cm_ref, a_ref, d_ref = refs[:5]
    rest = refs[5:]
    s0_ref = rest.pop(0) if chained else None
    y_ref = rest.pop(0)
    hfin_ref = None if chained else rest.pop(0)
    hbuf_e, hbuf_o, hb0, hb1, hst = rest
    hbufs, hb16s = (hbuf_e, hbuf_o), (hb0, hb1)
    ns, sw = S5_NSEQ, S5_SW
    half = ns // 2
    n_tc = S5_SEQ // S5_TC
    blk = S5_TC * ns
    dirs = (0, 1)

    for d in dirs:
        hst[d] = jnp.zeros((ns, 2 * sw), F32)

    def steps_of(d, i):
        return pl.ds(((n_tc - 1 - i) if d else i) * S5_TC, S5_TC)

    def kept(i):
        return i // 2 if chained else 0

    def project(i, par):
        for d in dirs:
            u = u_ref[steps_of(d, i)].reshape(blk, LANES)
            hbufs[par][d, kept(i)] = _dot(u.astype(BF16), bm_ref[d])

    def scan(i, par, emit):
        a = [(a_ref[d, :half, :sw], a_ref[d, :half, sw:]) for d in dirs]
        h = [[(hst[d, k * half:(k + 1) * half, :sw], hst[d, k * half:(k + 1) * half, sw:]) for k in range(2)]
             for d in dirs]
        for jj in range(S5_TC):
            for d in dirs:
                j = S5_TC - 1 - jj if d else jj
                ar, ai = a[d]
                bu = hbufs[par].at[d, kept(i)]
                for k in range(2):
                    r = slice(j * ns + k * half, j * ns + (k + 1) * half)
                    hr, hi = h[d][k]
                    h[d][k] = (ar * hr - ai * hi + bu[r, :sw], ar * hi + ai * hr + bu[r, sw:])
                if emit:
                    r = slice(j * ns, (j + 1) * ns)
                    hb16s[par][d, r, :sw] = jnp.concatenate([h[d][0][0], h[d][1][0]], axis=0).astype(BF16)
                    hb16s[par][d, r, sw:] = jnp.concatenate([h[d][0][1], h[d][1][1]], axis=0).astype(BF16)
        for d in dirs:
            for k in range(2):
                hst[d, k * half:(k + 1) * half, :sw] = h[d][k][0]
                hst[d, k * half:(k + 1) * half, sw:] = h[d][k][1]

    def readout(i, par):
        for d in dirs:
            steps = steps_of(d, i)
            y = _dot_nt(hb16s[par][d], cm_ref[d])
            y_ref[steps] = y_ref[steps] + y.reshape(S5_TC, ns, LANES)

    def skip(n, carry):
        steps = pl.ds(n * S5_TC, S5_TC)
        y_ref[steps] = d_ref[...] * u_ref[steps]
        return carry

    project(0, 0)
    if chained:
        def sweep(k, carry):
            i = 2 * k
            scan(i, 0, False)
            project(i + 1, 1)
            scan(i + 1, 1, False)
            project(i + 2, 0)
            return carry

        lax.fori_loop(0, n_tc // 2 - 1, sweep, 0)
        scan(n_tc - 2, 0, False)
        project(n_tc - 1, 1)
        scan(n_tc - 1, 1, False)

        n_long = s0_ref.shape[1]
        pieces = ns // n_long
        for d in dirs:
            pr, pi = a_ref[d, 0:1, :sw], a_ref[d, 0:1, sw:]
            for _ in range(int(math.log2(S5_SEQ))):
                pr, pi = pr * pr - pi * pi, 2.0 * (pr * pi)
            for b in range(n_long):
                hr, hi = s0_ref[d, b:b + 1, :sw], s0_ref[d, b:b + 1, sw:]
                for k in (range(pieces - 1, -1, -1) if d else range(pieces)):
                    r = b * pieces + k
                    zr, zi = hst[d, r:r + 1, :sw], hst[d, r:r + 1, sw:]
                    hst[d, r:r + 1, :sw] = hr
                    hst[d, r:r + 1, sw:] = hi
                    hr, hi = pr * hr - pi * hi + zr, pr * hi + pi * hr + zi

    hb1[...] = jnp.zeros(hb1.shape, BF16)
    lax.fori_loop(0, n_tc, skip, 0)

    def body(k, carry):
        i = 2 * k
        scan(i, 0, True)
        readout(jnp.maximum(i - 1, 0), 1)
        if not chained:
            project(i + 1, 1)
        scan(i + 1, 1, True)
        readout(i, 0)
        if not chained:
            project(jnp.minimum(i + 2, n_tc - 1), 0)
        return carry

    lax.fori_loop(0, n_tc // 2, body, 0)
    if not chained:
        for d in dirs:
            hfin_ref[d] = hst[d]
    readout(n_tc - 1, 1)


def _s5_call(u_tm, part, a, bmat, cmat, dskip, s0, layer):
    chained = s0 is not None
    tm_spec = pl.BlockSpec((S5_SEQ, S5_NSEQ, LANES), lambda g: (0, 0, g))

    def mat_spec(rows):
        return pl.BlockSpec((None, 2, None, rows, 2 * S5_SW), lambda g: (layer, 0, g, 0, 0))

    in_specs = [pl.BlockSpec((S5_SEQ, S5_NSEQ, LANES), lambda g: (0, part, g)),
                mat_spec(LANES), mat_spec(LANES), mat_spec(S5_NSEQ),
                pl.BlockSpec((None, None, 1, LANES), lambda g: (layer, g, 0, 0))]
    args = [u_tm, bmat, cmat, a, dskip]
    out_specs = [tm_spec]
    out_shape = [jax.ShapeDtypeStruct((S5_SEQ, S5_NSEQ, S5_W), F32)]
    if chained:
        in_specs.append(pl.BlockSpec((2, None, s0.shape[2], 2 * S5_SW), lambda g: (0, g, 0, 0)))
        args.append(s0)
    else:
        out_specs.append(pl.BlockSpec((2, None, S5_NSEQ, 2 * S5_SW), lambda g: (0, g, 0, 0)))
        out_shape.append(jax.ShapeDtypeStruct((2, S5_NGB, S5_NSEQ, 2 * S5_SW), F32))
    blk = S5_TC * S5_NSEQ
    kept = S5_SEQ // S5_TC // 2 if chained else 1
    res = pl.pallas_call(
        functools.partial(_s5_kernel, chained=chained),
        grid=(S5_NGB,),
        in_specs=in_specs,
        out_specs=out_specs,
        out_shape=out_shape,
        scratch_shapes=[pltpu.VMEM((2, kept, blk, 2 * S5_SW), F32),
                        pltpu.VMEM((2, kept, blk, 2 * S5_SW), F32),
                        pltpu.VMEM((2, blk, 2 * S5_SW), BF16),
                        pltpu.VMEM((2, blk, 2 * S5_SW), BF16),
                        pltpu.VMEM((2, S5_NSEQ, 2 * S5_SW), F32)],
        compiler_params=pltpu.CompilerParams(dimension_semantics=("parallel",),
                                             vmem_limit_bytes=VMEM_LIMIT),
        name="s5_scan",
    )(*args)
    return (res[0], None) if chained else (res[0], res[1])


def _out_kernel(x_ref, ohc_ref, ohs_ref, y5c_ref, y5s_ref, g1_ref, sh2_ref, sc2_ref, g2_ref, nffn_ref, nfin_ref,
                wglu_ref, wout_ref, wg_ref, wu_ref, wd_ref, *rest, last_layer):
    if last_layer:
        oc_ref, os_ref, wglu_b, wout_b = rest
    else:
        o_ref, wglu_b, wout_b = rest

    @pl.when(_first_step())
    def _():
        _cast_rows(wglu_ref, wglu_b)
        _cast_rows(wout_ref, wout_b)

    smp = _is_sample_tile()
    y = jnp.concatenate([jnp.where(smp, y5s_ref[:, s, :], y5c_ref[:, s, :]) for s in range(TILE_S)],
                        axis=0)
    y = _gelu_tanh(y)
    y = y * _sigmoid(_dot(y.astype(BF16), wglu_b[...]))
    ohg = jnp.where(smp, ohs_ref[...], ohc_ref[...]).reshape(TILE_ROWS, HG_W)
    mix = _dot(ohg.astype(BF16), wout_b[:HG_W, :]) + _dot(y.astype(BF16), wout_b[HG_W:, :])
    x = x_ref[...].reshape(TILE_ROWS, D_MODEL) + g1_ref[...] * mix
    h = _rms(x) * nffn_ref[...]
    h = (h * (1.0 + sc2_ref[...]) + sh2_ref[...]).astype(BF16)
    act = (_silu(_dot(h, wg_ref[...])) * _dot(h, wu_ref[...])).astype(BF16)
    x = x + g2_ref[...] * _dot(act, wd_ref[...])
    if not last_layer:
        o_ref[...] = x.reshape(o_ref.shape)
    else:
        x = (_rms(x) * nfin_ref[...]).reshape(oc_ref.shape)

        @pl.when(smp)
        def _():
            os_ref[...] = x

        @pl.when(jnp.logical_not(smp))
        def _():
            oc_ref[...] = x


def _out_call(x3, ohg_c, ohg_s, y5_c, y5_s, mod4, nffn, nfin, wglu, wout, wg, wu, wd, layer, last_layer):
    vec = pl.BlockSpec((1, D_MODEL), lambda sb, tb: (0, 0))
    part_shape = jax.ShapeDtypeStruct((S5_NSEQ, S5_SEQ, D_MODEL), F32)
    if last_layer:
        out_specs = [_part_tile_spec(D_MODEL, _ctx_index), _part_tile_spec(D_MODEL, _smp_index)]
        out_shape = [part_shape, part_shape]
    else:
        out_specs = _tile_spec(D_MODEL)
        out_shape = jax.ShapeDtypeStruct(x3.shape, F32)
    return pl.pallas_call(
        functools.partial(_out_kernel, last_layer=last_layer),
        grid=(ALL_SEQ // TILE_S, TIME_TILES),
        in_specs=[_tile_spec(D_MODEL),
                  _part_tile_spec(HG_W, _ctx_index), _part_tile_spec(HG_W, _smp_index),
                  _part_tm_tile_spec(S5_W, _ctx_index), _part_tm_tile_spec(S5_W, _smp_index),
                  _mod_spec(layer, 2), _mod_spec(layer, 3), _mod_spec(layer, 4), _mod_spec(layer, 5),
                  vec, vec,
                  _layer_spec((S5_W, S5_W), layer), _layer_spec((D_MODEL, D_MODEL), layer),
                  _whole_spec((D_MODEL, D_FF)), _whole_spec((D_MODEL, D_FF)), _whole_spec((D_FF, D_MODEL))],
        out_specs=out_specs,
        out_shape=out_shape,
        scratch_shapes=[pltpu.VMEM((S5_W, S5_W), BF16), pltpu.VMEM((D_MODEL, D_MODEL), BF16)],
        compiler_params=pltpu.CompilerParams(dimension_semantics=("arbitrary", "arbitrary"),
                                             vmem_limit_bytes=VMEM_LIMIT),
        name="out_ffn",
    )(x3, ohg_c, ohg_s, y5_c, y5_s, mod4, mod4, mod4, mod4, nffn, nfin, wglu, wout, wg, wu, wd)


def _s5_state_to_blocks(s):
    n = s.shape[0]
    s = s.reshape(n, 2, S5_NGB, S5_GB, S5_P, 2)
    return jnp.transpose(s, (1, 2, 0, 5, 3, 4)).reshape(2, S5_NGB, n, 2 * S5_SW)


def _s5_blocks_to_state(h):
    n = h.shape[2]
    h = h.reshape(2, S5_NGB, n, 2, S5_GB, S5_P)
    return jnp.transpose(h, (2, 0, 1, 4, 5, 3)).reshape(n, 2, S5_GROUPS, S5_P, 2)


def kernel(x_prompt, x_sample, state_hgrn, state_s5, c, c_ctx, w_mod, b_mod, norm_mix, norm_ffn, norm_final, w_in, w_out, hg_lb_logits, hg_norm, s5_lam_re, s5_lam_im, s5_log_dt, s5_b_re, s5_b_im, s5_c_re, s5_c_im, s5_d, s5_w_glu, w_gate, w_up, w_down):
    n_ctx, ctx_len, _ = x_prompt.shape
    n_dec, dec_len, _ = x_sample.shape
    assert ctx_len == S5_SEQ and n_ctx == S5_NSEQ and n_dec * dec_len == S5_NSEQ * S5_SEQ

    cond = jnp.concatenate([c_ctx[None, :], c, jnp.zeros((SUBLANES - 1 - n_dec, D_MODEL), F32)], axis=0)
    mod4 = _mod_call(cond, w_mod, b_mod).reshape(DEPTH, SUBLANES, 1, 6 * D_MODEL)

    s5_a, s5_bmat, s5_cmat = _s5_params(s5_lam_re, s5_lam_im, s5_log_dt, s5_b_re, s5_b_im, s5_c_re, s5_c_im)
    s5_dskip = s5_d.reshape(DEPTH, S5_NGB, 1, LANES)
    nfin = norm_final.reshape(1, D_MODEL)

    assert dec_len // S5_SEQ == TILE_S and n_dec + 1 <= SUBLANES
    tok = (S5_NSEQ, S5_SEQ, D_MODEL)
    ctx_rows = S5_NSEQ * S5_SEQ
    xs = (x_prompt.reshape(tok), x_sample.reshape(tok))
    ctx_fin, smp_fin, s5_finals = None, None, []
    for l in range(DEPTH):
        proj3, u_tm, vals3, x_all, (w_gate_b, w_up_b, w_down_b) = _in_call(
            xs, norm_mix[l].reshape(1, D_MODEL), mod4, w_in, (w_gate, w_up, w_down), l)
        proj = proj3.reshape(ALL_SEQ * S5_SEQ, HG_PROJ_W)
        vals = vals3.reshape(ALL_SEQ * S5_SEQ, HG_W)
        gain = hg_norm[l].reshape(1, HG_D)
        ohg_c, ctx_fin = _hgrn_call(proj, vals, 0, hg_lb_logits, gain, None, ctx_fin, l, n_ctx, ctx_len)
        ohg_s, smp_fin = _hgrn_call(proj, vals, ctx_rows, hg_lb_logits, gain, state_hgrn, smp_fin, l, n_dec,
                                    dec_len)
        y5_c, s5_fin = _s5_call(u_tm, 0, s5_a, s5_bmat, s5_cmat, s5_dskip, None, l)
        y5_s, _ = _s5_call(u_tm, 1, s5_a, s5_bmat, s5_cmat, s5_dskip, _s5_state_to_blocks(state_s5[:, l]), l)
        last = l == DEPTH - 1
        res = _out_call(x_all, ohg_c.reshape(S5_NSEQ, S5_SEQ, HG_W), ohg_s.reshape(S5_NSEQ, S5_SEQ, HG_W),
                        y5_c, y5_s, mod4, norm_ffn[l].reshape(1, D_MODEL), nfin,
                        s5_w_glu, w_out, w_gate_b, w_up_b, w_down_b, l, last)
        xs = res if last else (res,)
        s5_finals.append(_s5_blocks_to_state(s5_fin))
    y_prompt, y_sample = xs
    return (y_prompt.reshape(x_prompt.shape), y_sample.reshape(x_sample.shape),
            ctx_fin, jnp.stack(s5_finals, axis=1))
```

```python
import functools
import math

import jax
import jax.numpy as jnp
import numpy as np
from jax import lax
from jax.experimental import pallas as pl
from jax.experimental.pallas import tpu as pltpu

F32 = jnp.float32
BF16 = jnp.bfloat16

LANES = 128
SUBLANES = 8

D_MODEL = 1024
DEPTH = 2
GRID_W = 64
HG_W = 512
HG_HEADS = 4
HG_D = HG_W // HG_HEADS
S5_W = 512
S5_CH = 16
S5_GROUPS = S5_W // S5_CH
S5_P = 64
S5_GB = LANES // S5_CH
S5_NGB = S5_GROUPS // S5_GB
S5_SW = S5_GB * S5_P
HG_IN_W = 5 * HG_W
IN_W = HG_IN_W + S5_W
HG_PROJ_W = 4 * HG_W
D_FF = 2816
EPS = 1e-6

HG_CHUNK = 128
HG_LEVELS = (64, 32, 16, 8, 4, 2, 1)
S5_SEQ = 256
S5_NSEQ = 16
S5_TC = 16

TILE_S = SUBLANES
TILE_T = 64
TILE_ROWS = TILE_S * TILE_T
TIME_TILES = S5_SEQ // TILE_T
ALL_SEQ = 2 * S5_NSEQ
CTX_TILES = S5_NSEQ // TILE_S
X_RING = 3
CAST_ROWS = 128
MOD_TILE_N = 1536
VMEM_LIMIT = 56 * 1024 * 1024


def _sigmoid(x):
    return 1.0 / (1.0 + jnp.exp(-x))


def _silu(x):
    return x * _sigmoid(x)


def _gelu_tanh(x):
    return 0.5 * x * (1.0 + jnp.tanh(math.sqrt(2.0 / math.pi) * (x + 0.044715 * (x * x * x))))


def _rms(x):
    return x * lax.rsqrt(jnp.mean(x * x, axis=-1, keepdims=True) + EPS)


def _dot(a, b):
    return jnp.dot(a, b, preferred_element_type=F32)


def _dot_nt(a, b):
    return lax.dot_general(a, b, (((1,), (1,)), ((), ())), preferred_element_type=F32)


def _dot_tn(a, b):
    return lax.dot_general(a, b, (((0,), (0,)), ((), ())), preferred_element_type=F32)


def _whole_spec(shape):
    return pl.BlockSpec(tuple(shape), lambda *_: (0,) * len(shape), pipeline_mode=pl.Buffered(1))


def _layer_spec(shape, layer):
    nd = len(shape)
    return pl.BlockSpec((None,) + tuple(shape), lambda *_: (layer,) + (0,) * nd, pipeline_mode=pl.Buffered(1))


def _mod_kernel(cond_ref, w_ref, b_ref, o_ref):
    a = _silu(cond_ref[...]).astype(BF16)
    o_ref[0] = _dot(a, w_ref[0].astype(BF16)) + b_ref[0]


def _mod_call(cond, w_mod, b_mod):
    n_cond = cond.shape[0]
    n_out = w_mod.shape[-1]
    return pl.pallas_call(
        _mod_kernel,
        grid=(DEPTH, n_out // MOD_TILE_N),
        in_specs=[
            pl.BlockSpec((n_cond, D_MODEL), lambda l, j: (0, 0)),
            pl.BlockSpec((1, D_MODEL, MOD_TILE_N), lambda l, j: (l, 0, j)),
            pl.BlockSpec((1, 1, MOD_TILE_N), lambda l, j: (l, 0, j)),
        ],
        out_specs=pl.BlockSpec((1, n_cond, MOD_TILE_N), lambda l, j: (l, 0, j)),
        out_shape=jax.ShapeDtypeStruct((DEPTH, n_cond, n_out), F32),
        compiler_params=pltpu.CompilerParams(dimension_semantics=("parallel", "parallel"),
                                             vmem_limit_bytes=VMEM_LIMIT),
        name="adaln_mod",
    )(cond, w_mod, b_mod.reshape(DEPTH, 1, n_out))


def _first_step():
    return jnp.logical_and(pl.program_id(0) == 0, pl.program_id(1) == 0)


def _cast_rows(src_ref, dst_ref):
    for r in range(0, src_ref.shape[0], CAST_ROWS):
        dst_ref[r:r + CAST_ROWS, :] = src_ref[r:r + CAST_ROWS, :].astype(BF16)


def _grid_pos_tile(omega, tb):
    nf = omega.shape[-1]
    s_idx = lax.broadcasted_iota(jnp.int32, (TILE_S, nf), 0)
    j_idx = lax.broadcasted_iota(jnp.int32, (TILE_T, nf), 0)
    t0 = tb * TILE_T
    row = (s_idx * (S5_SEQ // GRID_W) + t0 // GRID_W).astype(F32) * omega
    col = (j_idx + t0 % GRID_W).astype(F32) * omega
    enc_r = jnp.concatenate([jnp.sin(row), jnp.cos(row)], axis=-1)
    enc_c = jnp.concatenate([jnp.sin(col), jnp.cos(col)], axis=-1)
    shape = (TILE_S, TILE_T, 2 * nf)
    return jnp.concatenate([jnp.broadcast_to(enc_r[:, None, :], shape),
                            jnp.broadcast_to(enc_c[None, :, :], shape)], axis=-1)


def _is_sample_tile():
    return pl.program_id(0) >= CTX_TILES


def _x_tile_copy(srcs, buf_ref, sem_ref, t, start):
    slot = t % X_RING
    sb, tb = t // TIME_TILES, t % TIME_TILES

    def run(src, sb_local):
        cp = pltpu.make_async_copy(
            src.at[pl.ds(sb_local * TILE_S, TILE_S), pl.ds(tb * TILE_T, TILE_T), :],
            buf_ref.at[slot], sem_ref.at[slot])
        if start:
            cp.start()
        else:
            cp.wait()

    if len(srcs) == 1:
        run(srcs[0], sb)
    else:
        pl.when(sb < CTX_TILES)(lambda: run(srcs[0], sb))
        pl.when(sb >= CTX_TILES)(lambda: run(srcs[1], sb - CTX_TILES))


def _in_kernel(*refs, first_layer):
    n_in = 10 if first_layer else 8
    for src, dst in zip(refs[n_in - 3:n_in], refs[-6:-3]):
        dst[...] = src[...].astype(BF16)
    refs = refs[:n_in - 3] + refs[n_in:-6] + refs[-3:]

    n_steps = pl.num_programs(0) * TIME_TILES
    step = pl.program_id(0) * TIME_TILES + pl.program_id(1)
    srcs = refs[:2] if first_layer else refs[:1]
    xbuf_ref, xsem_ref = refs[-2:]

    @pl.when(step == 0)
    def _():
        for t in range(X_RING - 1):
            _x_tile_copy(srcs, xbuf_ref, xsem_ref, jnp.int32(t), True)

    @pl.when(step + (X_RING - 1) < n_steps)
    def _():
        _x_tile_copy(srcs, xbuf_ref, xsem_ref, step + (X_RING - 1), True)

    _x_tile_copy(srcs, xbuf_ref, xsem_ref, step, False)
    x = xbuf_ref[step % X_RING]

    if first_layer:
        _, _, om_ref, gain_ref, sh_ref, sc_ref, w_ref, proj_ref, u_ref, v_ref, xo_ref, wb_ref = refs[:-2]
        x = jnp.where(_is_sample_tile(), x + _grid_pos_tile(om_ref[...], pl.program_id(1)), x)
        xo_ref[...] = x
    else:
        _, gain_ref, sh_ref, sc_ref, w_ref, proj_ref, u_ref, v_ref, wb_ref = refs[:-2]

    @pl.when(_first_step())
    def _():
        _cast_rows(w_ref, wb_ref)

    x = x.reshape(TILE_ROWS, D_MODEL)
    h = _rms(x) * gain_ref[...]
    h = (h * (1.0 + sc_ref[...]) + sh_ref[...]).astype(BF16)
    u = _dot(h, wb_ref[:, HG_IN_W:])
    for s in range(TILE_S):
        u_ref[:, s, :] = u[s * TILE_T:(s + 1) * TILE_T, :]
    p = _dot(h, wb_ref[:, :HG_IN_W])
    tile = (TILE_S, TILE_T)
    proj_ref[:, :, :3 * HG_W] = p[:, :3 * HG_W].reshape(tile + (3 * HG_W,))
    proj_ref[:, :, 3 * HG_W:] = p[:, 4 * HG_W:].reshape(tile + (HG_W,))
    v_ref[...] = p[:, 3 * HG_W:4 * HG_W].astype(BF16).reshape(v_ref.shape)


def _tile_spec(width):
    return pl.BlockSpec((TILE_S, TILE_T, width), lambda sb, tb: (sb, tb, 0))


def _tm_tile_spec(width):
    return pl.BlockSpec((TILE_T, TILE_S, width), lambda sb, tb: (tb, sb, 0))


def _ctx_index(sb, tb):
    on = sb < CTX_TILES
    return jnp.where(on, sb, CTX_TILES - 1), jnp.where(on, tb, TIME_TILES - 1)


def _smp_index(sb, tb):
    on = sb >= CTX_TILES
    return jnp.where(on, sb - CTX_TILES, 0), jnp.where(on, tb, 0)


def _part_tile_spec(width, index):
    return pl.BlockSpec((TILE_S, TILE_T, width), lambda sb, tb: index(sb, tb) + (0,))


def _part_tm_tile_spec(width, index):
    return pl.BlockSpec((TILE_T, TILE_S, width), lambda sb, tb: index(sb, tb)[::-1] + (0,))


def _mod_spec(layer, col):
    return pl.BlockSpec((None, None, 1, D_MODEL),
                        lambda sb, tb: (layer, jnp.maximum(sb - (CTX_TILES - 1), 0), 0, col))


def _in_call(xs, gain, mod4, w_in, ffn_w, layer):
    first_layer = len(xs) == 2
    n_steps = (ALL_SEQ // TILE_S) * TIME_TILES
    if first_layer:
        assert GRID_W % TILE_T == 0 and S5_SEQ % GRID_W == 0
        nf = D_MODEL // 4
        omega = 1.0 / (np.float32(10000.0) ** (np.arange(nf, dtype=np.float32) / np.float32(nf)))
        in_specs = [pl.BlockSpec(memory_space=pl.ANY), pl.BlockSpec(memory_space=pl.ANY),
                    pl.BlockSpec((1, nf), lambda sb, tb: (0, 0))]
        args = list(xs) + [jnp.asarray(omega.reshape(1, nf), F32)]
    else:
        in_specs = [pl.BlockSpec(memory_space=pl.ANY)]
        args = list(xs)
    in_specs += [
        pl.BlockSpec((1, D_MODEL), lambda sb, tb: (0, 0)),
        _mod_spec(layer, 0),
        _mod_spec(layer, 1),
        _layer_spec((D_MODEL, IN_W), layer),
    ]
    args += [gain, mod4, mod4, w_in]
    out_specs = [_tile_spec(HG_PROJ_W), _tm_tile_spec(S5_W), _tile_spec(HG_W)]
    out_shape = [jax.ShapeDtypeStruct((ALL_SEQ, S5_SEQ, HG_PROJ_W), F32),
                 jax.ShapeDtypeStruct((S5_SEQ, ALL_SEQ, S5_W), F32),
                 jax.ShapeDtypeStruct((ALL_SEQ, S5_SEQ, HG_W), BF16)]
    if first_layer:
        out_specs.append(_tile_spec(D_MODEL))
        out_shape.append(jax.ShapeDtypeStruct((ALL_SEQ, S5_SEQ, D_MODEL), F32))
    for w in ffn_w:
        _, n_rows, n_cols = w.shape
        slab = n_rows // n_steps
        assert slab * n_steps == n_rows and slab % (2 * SUBLANES) == 0
        in_specs.append(pl.BlockSpec((None, slab, n_cols), lambda sb, tb: (layer, sb * TIME_TILES + tb, 0)))
        args.append(w)
        out_specs.append(pl.BlockSpec((slab, n_cols), lambda sb, tb: (sb * TIME_TILES + tb, 0)))
        out_shape.append(jax.ShapeDtypeStruct((n_rows, n_cols), BF16))
    res = pl.pallas_call(
        functools.partial(_in_kernel, first_layer=first_layer),
        grid=(ALL_SEQ // TILE_S, TIME_TILES),
        in_specs=in_specs,
        out_specs=out_specs,
        out_shape=out_shape,
        scratch_shapes=[pltpu.VMEM((D_MODEL, IN_W), BF16),
                        pltpu.VMEM((X_RING, TILE_S, TILE_T, D_MODEL), F32),
                        pltpu.SemaphoreType.DMA((X_RING,))],
        compiler_params=pltpu.CompilerParams(dimension_semantics=("arbitrary", "arbitrary"),
                                             vmem_limit_bytes=VMEM_LIMIT),
        name="in_proj",
    )(*args)
    n_w = len(ffn_w)
    return res[0], res[1], res[2], (res[3] if first_layer else xs[0]), res[len(res) - n_w:]


def _pair_boundary(b, m, rev):
    c = b.shape[0]
    span = 2 * m
    at = m if rev else m - 1
    if span >= SUBLANES:
        b3 = b.reshape(c // span, span, LANES)
        return jnp.broadcast_to(b3[:, at:at + 1, :], b3.shape).reshape(c, LANES)
    b3 = b.reshape(c // SUBLANES, SUBLANES, LANES)
    sub = lax.broadcasted_iota(jnp.int32, b3.shape, 1)
    out = None
    for p in range(SUBLANES // span):
        piece = jnp.broadcast_to(b3[:, p * span + at:p * span + at + 1, :], b3.shape)
        out = piece if out is None else jnp.where(sub >= p * span, piece, out)
    return out.reshape(c, LANES)


def _neg_abs(x):
    bits = lax.bitcast_convert_type(x, jnp.uint32) | jnp.uint32(0x80000000)
    return lax.bitcast_convert_type(bits, F32)


def _hg_gates(chains, scale):
    outs = []
    for q, fl, lb, tri in chains:
        sig = _sigmoid(fl)
        forget = lb + (1.0 - lb) * sig
        logf = jnp.log2(forget)
        key = (1.0 - lb) * (1.0 - sig)
        hi = logf.astype(BF16)
        r1 = logf - hi.astype(F32)
        mid = r1.astype(BF16)
        lo = (r1 - mid.astype(F32)).astype(BF16)
        parts = _dot(tri, jnp.concatenate([hi, mid, lo], axis=1))
        b2 = parts[:, :LANES] + parts[:, LANES:2 * LANES] + parts[:, 2 * LANES:]
        outs.append((_silu(q) * scale, key, b2, forget))
    return outs


def _hg_scores(chains, code, eye, from_zero=False):
    c = chains[0][0].shape[0]
    o_inter = []
    for qh, key, b2, forget, v, st_ref, rev in chains:
        b_edge = b2[0:1, :] if rev else b2[c - 1:c, :]
        k_end = key * jnp.exp2(b_edge - b2)
        grown = _dot_tn(v.astype(BF16), k_end.astype(BF16))
        if from_zero:
            o_inter.append(None)
            st_ref[...] = grown
        else:
            st = st_ref[...]
            o_inter.append(_dot_nt((qh * jnp.exp2(b2)).astype(BF16), st.astype(BF16)))
            st_ref[...] = jnp.exp2(b_edge) * st + grown

    out = []
    for o, (qh, key, b2, forget, v, st_ref, rev) in zip(o_inter, chains):
        scores = jnp.where(eye, jnp.sum(qh * key, axis=-1, keepdims=True), 0.0)
        qb, kb = qh.astype(BF16), key.astype(BF16)
        for m in HG_LEVELS:
            k = int(math.log2(m)) + 1
            if m == 1:
                p = _dot_nt((qh * forget).astype(BF16), kb)
            elif m >= SUBLANES:
                g = c // (2 * m)
                t_half = slice(0, m) if rev else slice(m, 2 * m)
                s_half = slice(m, 2 * m) if rev else slice(0, m)
                at = m if rev else m - 1
                b3, q3, k3 = (x.reshape(g, 2 * m, LANES) for x in (b2, qh, key))
                seam = b3[:, at:at + 1, :]
                qt = (q3[:, t_half] * jnp.exp2(b3[:, t_half] - seam)).reshape(c // 2, LANES)
                ks = k3[:, s_half] * jnp.exp2(seam - b3[:, s_half])
                kk = jnp.concatenate([k3[:, t_half], ks] if rev else [ks, k3[:, t_half]], axis=1).reshape(c, LANES)
                p = _dot_nt(qt.astype(BF16), kk.astype(BF16)).reshape(g, m, c)
                s3 = scores.reshape(g, 2 * m, c)
                hit = code.reshape(g, 2 * m, c)[:, t_half] == (-k if rev else k)
                st = jnp.where(hit, p, s3[:, t_half])
                scores = jnp.concatenate([st, s3[:, s_half]] if rev else [s3[:, s_half], st], axis=1).reshape(c, c)
                continue
            else:
                e = jnp.exp2(_neg_abs(b2 - _pair_boundary(b2, m, rev))).astype(BF16)
                p = _dot_nt(qb * e, kb * e)
            scores = jnp.where(code == (-k if rev else k), p, scores)
        out.append((o, scores.astype(BF16)))
    return out


def _hgrn_kernel(*refs, layer, n_chunks, n_heads, zero_init):
    refs = list(refs)
    q_ref, ff_ref, fb_ref, v_ref, g_ref, lbl_ref, gain_ref, code_ref, tri_ref = refs[:9]
    s0_ref = None if zero_init else refs[9]
    o_ref, sfin_ref, st_ref, of_ref, ob_ref, ab_ref, oi_ref, sc_ref = refs[-8:]
    c = HG_CHUNK
    code = code_ref[...]
    eye = code == 0
    chains = [(hd, d) for hd in range(n_heads) for d in (0, 1)]

    def lanes(hd):
        return slice(hd * HG_D, (hd + 1) * HG_D)

    def lower_bound(hd, d):
        lg = lbl_ref[d, :, lanes(hd)]
        ex = jnp.exp(lg - jnp.max(lg, axis=0, keepdims=True))
        soft = ex / jnp.sum(ex, axis=0, keepdims=True)
        return jnp.sum(soft[:layer + 1], axis=0, keepdims=True) - soft[0:1]

    lb = [lower_bound(hd, d) for hd, d in chains]
    scale = HG_D ** -0.5

    def rows_of(d, i):
        n = (n_chunks - 1 - i) if d else i
        return pl.ds(n * c if isinstance(n, int) else pl.multiple_of(n * c, c), c)

    for ch, (hd, d) in enumerate(chains):
        st_ref[ch] = jnp.zeros((HG_D, HG_D), F32) if zero_init else s0_ref[d, hd].T

    def gates(i):
        return _hg_gates([(q_ref[rows_of(d, i), lanes(hd)], (fb_ref if d else ff_ref)[rows_of(d, i), lanes(hd)],
                           lb[ch], tri_ref[d]) for ch, (hd, d) in enumerate(chains)], scale)

    def scores(i, ab, from_zero=False):
        return _hg_scores([ab[ch] + (v_ref[rows_of(d, i), lanes(hd)], st_ref.at[ch], bool(d))
                           for ch, (hd, d) in enumerate(chains)], code, eye, from_zero)

    def emit(i, oi_sc):
        for ch, (hd, d) in enumerate(chains):
            rows = rows_of(d, i)
            o_intra = _dot(oi_sc[ch][1], v_ref[rows, lanes(hd)].astype(BF16))
            (ob_ref if d else of_ref)[rows, lanes(hd)] = o_intra if oi_sc[ch][0] is None else oi_sc[ch][0] + o_intra

    if n_chunks <= 2:
        ab = [gates(i) for i in range(n_chunks)]
        for i in range(n_chunks):
            emit(i, scores(i, ab[i], from_zero=zero_init and i == 0))
    else:
        n_ab = ab_ref.shape[1]

        def put_ab(ab):
            for ch in range(len(chains)):
                for k in range(n_ab):
                    ab_ref[ch, k] = ab[ch][k]

        def get_oi_sc():
            return [(oi_ref[ch], sc_ref[ch]) for ch in range(len(chains))]

        put_ab(gates(0))
        oi_ref[...] = jnp.zeros(oi_ref.shape, F32)
        sc_ref[...] = jnp.zeros(sc_ref.shape, BF16)

        def body(i, carry):
            emit(jnp.maximum(i - 1, 0), get_oi_sc())
            res = scores(i, [tuple(ab_ref[ch, k] for k in range(n_ab)) for ch in range(len(chains))])
            for ch in range(len(chains)):
                oi_ref[ch] = res[ch][0]
                sc_ref[ch] = res[ch][1]
            put_ab(gates(jnp.minimum(i + 1, n_chunks - 1)))
            return carry

        lax.fori_loop(0, n_chunks, body, 0)
        emit(n_chunks - 1, get_oi_sc())
    if len(sfin_ref.shape) == 5:
        for other in range(sfin_ref.shape[0]):
            if other != layer:
                sfin_ref[other] = jnp.zeros(sfin_ref.shape[1:], F32)
        sfin_ref = sfin_ref.at[layer]
    for ch, (hd, d) in enumerate(chains):
        sfin_ref[d, hd] = st_ref[ch].T

    def finish(n, carry):
        rows = pl.ds(pl.multiple_of(n * (2 * c), 2 * c), 2 * c)
        for hd in range(n_heads):
            o_ref[rows, lanes(hd)] = (_rms(of_ref[rows, lanes(hd)] + ob_ref[rows, lanes(hd)]) * gain_ref[...]
                                      * _silu(g_ref[rows, lanes(hd)])).astype(BF16)
        return carry

    lax.fori_loop(0, n_chunks // 2, finish, 0)


def _hgrn_call(proj, vals, row0, lb_logits, gain, state, finals, layer, n_seq, seq_len):
    rows = n_seq * seq_len
    seq0 = row0 // seq_len
    zero_init = state is None
    n_chunks = seq_len // HG_CHUNK
    assert n_chunks % 2 == 0
    nh = HG_HEADS if n_chunks <= 2 else 2
    hw = nh * HG_D
    n_hb = HG_HEADS // nh

    def col_spec(k):
        return pl.BlockSpec((seq_len, hw), lambda b, h: (seq0 + b, k * n_hb + h))

    t, s = np.meshgrid(np.arange(HG_CHUNK), np.arange(HG_CHUNK), indexing="ij")
    lvl = np.where(t == s, 0, np.floor(np.log2(np.maximum(t ^ s, 1))).astype(np.int32) + 1)
    code = jnp.asarray(np.where(t > s, lvl, -lvl), jnp.int32)
    tri = jnp.asarray(np.stack([s <= t, s >= t]), BF16)

    in_specs = [col_spec(0), col_spec(1), col_spec(2), col_spec(0), col_spec(3),
                pl.BlockSpec((2, DEPTH, hw), lambda b, h: (0, 0, h)),
                pl.BlockSpec((1, HG_D), lambda b, h: (0, 0)),
                pl.BlockSpec((HG_CHUNK, HG_CHUNK), lambda b, h: (0, 0)),
                pl.BlockSpec((2, HG_CHUNK, HG_CHUNK), lambda b, h: (0, 0, 0))]
    args = [proj] * 3 + [vals, proj, lb_logits, gain, code, tri]
    if not zero_init:
        in_specs.append(pl.BlockSpec((None, None, 2, nh, HG_D, HG_D), lambda b, h: (b, layer, 0, h, 0, 0)))
        args.append(state)
    aliases = {}
    if finals is None:
        fin_spec = pl.BlockSpec((None, DEPTH, 2, nh, HG_D, HG_D), lambda b, h: (b, 0, 0, h, 0, 0))
    else:
        fin_spec = pl.BlockSpec((None, None, 2, nh, HG_D, HG_D), lambda b, h: (b, layer, 0, h, 0, 0))
        aliases[len(args)] = 1
        in_specs.append(pl.BlockSpec(memory_space=pl.ANY))
        args.append(finals)
    n_ch = 2 * nh
    return pl.pallas_call(
        functools.partial(_hgrn_kernel, layer=layer, n_chunks=n_chunks, n_heads=nh, zero_init=zero_init),
        grid=(n_seq, n_hb),
        in_specs=in_specs,
        out_specs=[pl.BlockSpec((seq_len, hw), lambda b, h: (b, h)), fin_spec],
        out_shape=[jax.ShapeDtypeStruct((rows, HG_W), BF16),
                   jax.ShapeDtypeStruct((n_seq, DEPTH, 2, HG_HEADS, HG_D, HG_D), F32)],
        input_output_aliases=aliases,
        scratch_shapes=[pltpu.VMEM((n_ch, HG_D, HG_D), F32), pltpu.VMEM((seq_len, hw), F32),
                        pltpu.VMEM((seq_len, hw), F32),
                        pltpu.VMEM((n_ch, 4, HG_CHUNK, HG_D), F32), pltpu.VMEM((n_ch, HG_CHUNK, HG_D), F32),
                        pltpu.VMEM((n_ch, HG_CHUNK, HG_CHUNK), BF16)],
        compiler_params=pltpu.CompilerParams(dimension_semantics=("parallel", "parallel"),
                                             vmem_limit_bytes=VMEM_LIMIT),
        name="hgrn2_mixer",
    )(*args)


def _s5_params_kernel(lr_ref, li_ref, ldt_ref, btr_ref, bti_ref, cr_ref, ci_ref, a_ref, bm_ref, cm_ref):
    for k in range(lr_ref.shape[0]):
        _s5_params_block(*(r.at[k] for r in (lr_ref, li_ref, ldt_ref, btr_ref, bti_ref, cr_ref, ci_ref,
                                             a_ref, bm_ref, cm_ref)))


def _s5_params_block(lr_ref, li_ref, ldt_ref, btr_ref, bti_ref, cr_ref, ci_ref, a_ref, bm_ref, cm_ref):
    sw = S5_SW
    lr = jnp.minimum(lr_ref[...], -1e-4)
    li = li_ref[...]
    dt = jnp.exp(ldt_ref[...])
    mag = jnp.exp(lr * dt)
    ab_re = mag * jnp.cos(li * dt)
    ab_im = mag * jnp.sin(li * dt)
    nr = ab_re - 1.0
    den = lr * lr + li * li
    z_re = (nr * lr + ab_im * li) / den
    z_im = (ab_im * lr - nr * li) / den

    p_idx = lax.broadcasted_iota(jnp.int32, (S5_P, sw), 0)
    col = lax.broadcasted_iota(jnp.int32, (S5_P, sw), 1)
    for g in range(S5_GB):
        place = (col == p_idx + g * S5_P).astype(BF16)
        zr, zi = z_re[g:g + 1, :], z_im[g:g + 1, :]
        btr, bti = btr_ref[g], bti_ref[g]
        rows = slice(g * S5_CH, (g + 1) * S5_CH)
        bm_ref[rows, :sw] = _dot((zr * btr - zi * bti).astype(BF16), place).astype(BF16)
        bm_ref[rows, sw:] = _dot((zr * bti + zi * btr).astype(BF16), place).astype(BF16)
        cm_ref[rows, :sw] = _dot(cr_ref[g].astype(BF16), place).astype(BF16)
        cm_ref[rows, sw:] = _dot((-ci_ref[g]).astype(BF16), place).astype(BF16)
        a_ref[:, g * S5_P:(g + 1) * S5_P] = jnp.broadcast_to(ab_re[g:g + 1, :], (S5_NSEQ, S5_P))
        a_ref[:, sw + g * S5_P:sw + (g + 1) * S5_P] = jnp.broadcast_to(ab_im[g:g + 1, :], (S5_NSEQ, S5_P))


def _s5_params(lam_re, lam_im, log_dt, b_re, b_im, c_re, c_im):
    nb = DEPTH * 2 * S5_NGB
    gp = (nb, S5_GB, S5_P)
    gcp = (nb, S5_GB, S5_CH, S5_P)
    bt_re = jnp.swapaxes(b_re, -1, -2).reshape(gcp)
    bt_im = jnp.swapaxes(b_im, -1, -2).reshape(gcp)
    ldt = jnp.broadcast_to(log_dt.reshape(nb, S5_GB, 1), gp)
    per = S5_NGB
    gp_spec = pl.BlockSpec((per, S5_GB, S5_P), lambda i: (i, 0, 0))
    gcp_spec = pl.BlockSpec((per, S5_GB, S5_CH, S5_P), lambda i: (i, 0, 0, 0))
    a, bmat, cmat = pl.pallas_call(
        _s5_params_kernel,
        grid=(nb // per,),
        in_specs=[gp_spec] * 3 + [gcp_spec] * 4,
        out_specs=[pl.BlockSpec((per, S5_NSEQ, 2 * S5_SW), lambda i: (i, 0, 0)),
                   pl.BlockSpec((per, LANES, 2 * S5_SW), lambda i: (i, 0, 0)),
                   pl.BlockSpec((per, LANES, 2 * S5_SW), lambda i: (i, 0, 0))],
        out_shape=[jax.ShapeDtypeStruct((nb, S5_NSEQ, 2 * S5_SW), F32),
                   jax.ShapeDtypeStruct((nb, LANES, 2 * S5_SW), BF16),
                   jax.ShapeDtypeStruct((nb, LANES, 2 * S5_SW), BF16)],
        compiler_params=pltpu.CompilerParams(dimension_semantics=("parallel",)),
        name="s5_params",
    )(lam_re.reshape(gp), lam_im.reshape(gp), ldt, bt_re, bt_im, c_re.reshape(gcp), c_im.reshape(gcp))
    lead = (DEPTH, 2, S5_NGB)
    return (a.reshape(lead + a.shape[1:]), bmat.reshape(lead + bmat.shape[1:]), cmat.reshape(lead + cmat.shape[1:]))


def _s5_kernel(*refs, chained):
    refs = list(refs)
    u_ref, bm_ref, cm_ref, a_ref, d_ref = refs[:5]
    rest = refs[5:]
    s0_ref = rest.pop(0) if chained else None
    y_ref = rest.pop(0)
    hfin_ref = None if chained else rest.pop(0)
    hbuf_e, hbuf_o, hb0, hb1, hst = rest
    hbufs, hb16s = (hbuf_e, hbuf_o), (hb0, hb1)
    ns, sw = S5_NSEQ, S5_SW
    half = ns // 2
    n_tc = S5_SEQ // S5_TC
    blk = S5_TC * ns
    dirs = (0, 1)

    for d in dirs:
        hst[d] = jnp.zeros((ns, 2 * sw), F32)

    def steps_of(d, i):
        return pl.ds(((n_tc - 1 - i) if d else i) * S5_TC, S5_TC)

    def kept(i):
        return i // 2 if chained else 0

    def project(i, par):
        for d in dirs:
            u = u_ref[steps_of(d, i)].reshape(blk, LANES)
            hbufs[par][d, kept(i)] = _dot(u.astype(BF16), bm_ref[d])

    def scan(i, par, emit):
        a = [(a_ref[d, :half, :sw], a_ref[d, :half, sw:]) for d in dirs]
        h = [[(hst[d, k * half:(k + 1) * half, :sw], hst[d, k * half:(k + 1) * half, sw:]) for k in range(2)]
             for d in dirs]
        for jj in range(S5_TC):
            for d in dirs:
                j = S5_TC - 1 - jj if d else jj
                ar, ai = a[d]
                bu = hbufs[par].at[d, kept(i)]
                for k in range(2):
                    r = slice(j * ns + k * half, j * ns + (k + 1) * half)
                    hr, hi = h[d][k]
                    h[d][k] = (ar * hr - ai * hi + bu[r, :sw], ar * hi + ai * hr + bu[r, sw:])
                if emit:
                    r = slice(j * ns, (j + 1) * ns)
                    hb16s[par][d, r, :sw] = jnp.concatenate([h[d][0][0], h[d][1][0]], axis=0).astype(BF16)
                    hb16s[par][d, r, sw:] = jnp.concatenate([h[d][0][1], h[d][1][1]], axis=0).astype(BF16)
        for d in dirs:
            for k in range(2):
                hst[d, k * half:(k + 1) * half, :sw] = h[d][k][0]
                hst[d, k * half:(k + 1) * half, sw:] = h[d][k][1]

    def readout(i, par):
        for d in dirs:
            steps = steps_of(d, i)
            y = _dot_nt(hb16s[par][d], cm_ref[d])
            y_ref[steps] = y_ref[steps] + y.reshape(S5_TC, ns, LANES)

    def skip(n, carry):
        steps = pl.ds(n * S5_TC, S5_TC)
        y_ref[steps] = d_ref[...] * u_ref[steps]
        return carry

    project(0, 0)
    if chained:
        def sweep(k, carry):
            i = 2 * k
            scan(i, 0, False)
            project(i + 1, 1)
            scan(i + 1, 1, False)
            project(i + 2, 0)
            return carry

        lax.fori_loop(0, n_tc // 2 - 1, sweep, 0)
        scan(n_tc - 2, 0, False)
        project(n_tc - 1, 1)
        scan(n_tc - 1, 1, False)

        n_long = s0_ref.shape[1]
        pieces = ns // n_long
        for d in dirs:
            pr, pi = a_ref[d, 0:1, :sw], a_ref[d, 0:1, sw:]
            for _ in range(int(math.log2(S5_SEQ))):
                pr, pi = pr * pr - pi * pi, 2.0 * (pr * pi)
            for b in range(n_long):
                hr, hi = s0_ref[d, b:b + 1, :sw], s0_ref[d, b:b + 1, sw:]
                for k in (range(pieces - 1, -1, -1) if d else range(pieces)):
                    r = b * pieces + k
                    zr, zi = hst[d, r:r + 1, :sw], hst[d, r:r + 1, sw:]
                    hst[d, r:r + 1, :sw] = hr
                    hst[d, r:r + 1, sw:] = hi
                    hr, hi = pr * hr - pi * hi + zr, pr * hi + pi * hr + zi

    hb1[...] = jnp.zeros(hb1.shape, BF16)
    lax.fori_loop(0, n_tc, skip, 0)

    def body(k, carry):
        i = 2 * k
        scan(i, 0, True)
        readout(jnp.maximum(i - 1, 0), 1)
        if not chained:
            project(i + 1, 1)
        scan(i + 1, 1, True)
        readout(i, 0)
        if not chained:
            project(jnp.minimum(i + 2, n_tc - 1), 0)
        return carry

    lax.fori_loop(0, n_tc // 2, body, 0)
    if not chained:
        for d in dirs:
            hfin_ref[d] = hst[d]
    readout(n_tc - 1, 1)


def _s5_call(u_tm, part, a, bmat, cmat, dskip, s0, layer):
    chained = s0 is not None
    tm_spec = pl.BlockSpec((S5_SEQ, S5_NSEQ, LANES), lambda g: (0, 0, g))

    def mat_spec(rows):
        return pl.BlockSpec((None, 2, None, rows, 2 * S5_SW), lambda g: (layer, 0, g, 0, 0))

    in_specs = [pl.BlockSpec((S5_SEQ, S5_NSEQ, LANES), lambda g: (0, part, g)),
                mat_spec(LANES), mat_spec(LANES), mat_spec(S5_NSEQ),
                pl.BlockSpec((None, None, 1, LANES), lambda g: (layer, g, 0, 0))]
    args = [u_tm, bmat, cmat, a, dskip]
    out_specs = [tm_spec]
    out_shape = [jax.ShapeDtypeStruct((S5_SEQ, S5_NSEQ, S5_W), F32)]
    if chained:
        in_specs.append(pl.BlockSpec((2, None, s0.shape[2], 2 * S5_SW), lambda g: (0, g, 0, 0)))
        args.append(s0)
    else:
        out_specs.append(pl.BlockSpec((2, None, S5_NSEQ, 2 * S5_SW), lambda g: (0, g, 0, 0)))
        out_shape.append(jax.ShapeDtypeStruct((2, S5_NGB, S5_NSEQ, 2 * S5_SW), F32))
    blk = S5_TC * S5_NSEQ
    kept = S5_SEQ // S5_TC // 2 if chained else 1
    res = pl.pallas_call(
        functools.partial(_s5_kernel, chained=chained),
        grid=(S5_NGB,),
        in_specs=in_specs,
        out_specs=out_specs,
        out_shape=out_shape,
        scratch_shapes=[pltpu.VMEM((2, kept, blk, 2 * S5_SW), F32),
                        pltpu.VMEM((2, kept, blk, 2 * S5_SW), F32),
                        pltpu.VMEM((2, blk, 2 * S5_SW), BF16),
                        pltpu.VMEM((2, blk, 2 * S5_SW), BF16),
                        pltpu.VMEM((2, S5_NSEQ, 2 * S5_SW), F32)],
        compiler_params=pltpu.CompilerParams(dimension_semantics=("parallel",),
                                             vmem_limit_bytes=VMEM_LIMIT),
        name="s5_scan",
    )(*args)
    return (res[0], None) if chained else (res[0], res[1])


def _out_kernel(x_ref, ohc_ref, ohs_ref, y5c_ref, y5s_ref, g1_ref, sh2_ref, sc2_ref, g2_ref, nffn_ref, nfin_ref,
                wglu_ref, wout_ref, wg_ref, wu_ref, wd_ref, *rest, last_layer):
    if last_layer:
        oc_ref, os_ref, wglu_b, wout_b = rest
    else:
        o_ref, wglu_b, wout_b = rest

    @pl.when(_first_step())
    def _():
        _cast_rows(wglu_ref, wglu_b)
        _cast_rows(wout_ref, wout_b)

    smp = _is_sample_tile()
    y = jnp.concatenate([jnp.where(smp, y5s_ref[:, s, :], y5c_ref[:, s, :]) for s in range(TILE_S)],
                        axis=0)
    y = _gelu_tanh(y)
    y = y * _sigmoid(_dot(y.astype(BF16), wglu_b[...]))
    ohg = jnp.where(smp, ohs_ref[...], ohc_ref[...]).reshape(TILE_ROWS, HG_W)
    mix = _dot(ohg.astype(BF16), wout_b[:HG_W, :]) + _dot(y.astype(BF16), wout_b[HG_W:, :])
    x = x_ref[...].reshape(TILE_ROWS, D_MODEL) + g1_ref[...] * mix
    h = _rms(x) * nffn_ref[...]
    h = (h * (1.0 + sc2_ref[...]) + sh2_ref[...]).astype(BF16)
    act = (_silu(_dot(h, wg_ref[...])) * _dot(h, wu_ref[...])).astype(BF16)
    x = x + g2_ref[...] * _dot(act, wd_ref[...])
    if not last_layer:
        o_ref[...] = x.reshape(o_ref.shape)
    else:
        x = (_rms(x) * nfin_ref[...]).reshape(oc_ref.shape)

        @pl.when(smp)
        def _():
            os_ref[...] = x

        @pl.when(jnp.logical_not(smp))
        def _():
            oc_ref[...] = x


def _out_call(x3, ohg_c, ohg_s, y5_c, y5_s, mod4, nffn, nfin, wglu, wout, wg, wu, wd, layer, last_layer):
    vec = pl.BlockSpec((1, D_MODEL), lambda sb, tb: (0, 0))
    part_shape = jax.ShapeDtypeStruct((S5_NSEQ, S5_SEQ, D_MODEL), F32)
    if last_layer:
        out_specs = [_part_tile_spec(D_MODEL, _ctx_index), _part_tile_spec(D_MODEL, _smp_index)]
        out_shape = [part_shape, part_shape]
    else:
        out_specs = _tile_spec(D_MODEL)
        out_shape = jax.ShapeDtypeStruct(x3.shape, F32)
    return pl.pallas_call(
        functools.partial(_out_kernel, last_layer=last_layer),
        grid=(ALL_SEQ // TILE_S, TIME_TILES),
        in_specs=[_tile_spec(D_MODEL),
                  _part_tile_spec(HG_W, _ctx_index), _part_tile_spec(HG_W, _smp_index),
                  _part_tm_tile_spec(S5_W, _ctx_index), _part_tm_tile_spec(S5_W, _smp_index),
                  _mod_spec(layer, 2), _mod_spec(layer, 3), _mod_spec(layer, 4), _mod_spec(layer, 5),
                  vec, vec,
                  _layer_spec((S5_W, S5_W), layer), _layer_spec((D_MODEL, D_MODEL), layer),
                  _whole_spec((D_MODEL, D_FF)), _whole_spec((D_MODEL, D_FF)), _whole_spec((D_FF, D_MODEL))],
        out_specs=out_specs,
        out_shape=out_shape,
        scratch_shapes=[pltpu.VMEM((S5_W, S5_W), BF16), pltpu.VMEM((D_MODEL, D_MODEL), BF16)],
        compiler_params=pltpu.CompilerParams(dimension_semantics=("arbitrary", "arbitrary"),
                                             vmem_limit_bytes=VMEM_LIMIT),
        name="out_ffn",
    )(x3, ohg_c, ohg_s, y5_c, y5_s, mod4, mod4, mod4, mod4, nffn, nfin, wglu, wout, wg, wu, wd)


def _s5_state_to_blocks(s):
    n = s.shape[0]
    s = s.reshape(n, 2, S5_NGB, S5_GB, S5_P, 2)
    return jnp.transpose(s, (1, 2, 0, 5, 3, 4)).reshape(2, S5_NGB, n, 2 * S5_SW)


def _s5_blocks_to_state(h):
    n = h.shape[2]
    h = h.reshape(2, S5_NGB, n, 2, S5_GB, S5_P)
    return jnp.transpose(h, (2, 0, 1, 4, 5, 3)).reshape(n, 2, S5_GROUPS, S5_P, 2)


def kernel(x_prompt, x_sample, state_hgrn, state_s5, c, c_ctx, w_mod, b_mod, norm_mix, norm_ffn, norm_final, w_in, w_out, hg_lb_logits, hg_norm, s5_lam_re, s5_lam_im, s5_log_dt, s5_b_re, s5_b_im, s5_c_re, s5_c_im, s5_d, s5_w_glu, w_gate, w_up, w_down):
    n_ctx, ctx_len, _ = x_prompt.shape
    n_dec, dec_len, _ = x_sample.shape
    assert ctx_len == S5_SEQ and n_ctx == S5_NSEQ and n_dec * dec_len == S5_NSEQ * S5_SEQ

    cond = jnp.concatenate([c_ctx[None, :], c, jnp.zeros((SUBLANES - 1 - n_dec, D_MODEL), F32)], axis=0)
    mod4 = _mod_call(cond, w_mod, b_mod).reshape(DEPTH, SUBLANES, 1, 6 * D_MODEL)

    s5_a, s5_bmat, s5_cmat = _s5_params(s5_lam_re, s5_lam_im, s5_log_dt, s5_b_re, s5_b_im, s5_c_re, s5_c_im)
    s5_dskip = s5_d.reshape(DEPTH, S5_NGB, 1, LANES)
    nfin = norm_final.reshape(1, D_MODEL)

    assert dec_len // S5_SEQ == TILE_S and n_dec + 1 <= SUBLANES
    tok = (S5_NSEQ, S5_SEQ, D_MODEL)
    ctx_rows = S5_NSEQ * S5_SEQ
    xs = (x_prompt.reshape(tok), x_sample.reshape(tok))
    ctx_fin, smp_fin, s5_finals = None, None, []
    for l in range(DEPTH):
        proj3, u_tm, vals3, x_all, (w_gate_b, w_up_b, w_down_b) = _in_call(
            xs, norm_mix[l].reshape(1, D_MODEL), mod4, w_in, (w_gate, w_up, w_down), l)
        proj = proj3.reshape(ALL_SEQ * S5_SEQ, HG_PROJ_W)
        vals = vals3.reshape(ALL_SEQ * S5_SEQ, HG_W)
        gain = hg_norm[l].reshape(1, HG_D)
        ohg_c, ctx_fin = _hgrn_call(proj, vals, 0, hg_lb_logits, gain, None, ctx_fin, l, n_ctx, ctx_len)
        ohg_s, smp_fin = _hgrn_call(proj, vals, ctx_rows, hg_lb_logits, gain, state_hgrn, smp_fin, l, n_dec,
                                    dec_len)
        y5_c, s5_fin = _s5_call(u_tm, 0, s5_a, s5_bmat, s5_cmat, s5_dskip, None, l)
        y5_s, _ = _s5_call(u_tm, 1, s5_a, s5_bmat, s5_cmat, s5_dskip, _s5_state_to_blocks(state_s5[:, l]), l)
        last = l == DEPTH - 1
        res = _out_call(x_all, ohg_c.reshape(S5_NSEQ, S5_SEQ, HG_W), ohg_s.reshape(S5_NSEQ, S5_SEQ, HG_W),
                        y5_c, y5_s, mod4, norm_ffn[l].reshape(1, D_MODEL), nfin,
                        s5_w_glu, w_out, w_gate_b, w_up_b, w_down_b, l, last)
        xs = res if last else (res,)
        s5_finals.append(_s5_blocks_to_state(s5_fin))
    y_prompt, y_sample = xs
    return (y_prompt.reshape(x_prompt.shape), y_sample.reshape(x_sample.shape),
            ctx_fin, jnp.stack(s5_finals, axis=1))
```

```python
import functools
import math

import jax
import jax.numpy as jnp
import numpy as np
from jax import lax
from jax.experimental import pallas as pl
from jax.experimental.pallas import tpu as pltpu

F32 = jnp.float32
BF16 = jnp.bfloat16

LANES = 128
SUBLANES = 8

D_MODEL = 1024
DEPTH = 2
GRID_W = 64
HG_W = 512
HG_HEADS = 4
HG_D = HG_W // HG_HEADS
S5_W = 512
S5_CH = 16
S5_GROUPS = S5_W // S5_CH
S5_P = 64
S5_GB = LANES // S5_CH
S5_NGB = S5_GROUPS // S5_GB
S5_SW = S5_GB * S5_P
HG_IN_W = 5 * HG_W
IN_W = HG_IN_W + S5_W
HG_PROJ_W = 4 * HG_W
D_FF = 2816
EPS = 1e-6

HG_CHUNK = 128
HG_LEVELS = (64, 32, 16, 8, 4, 2, 1)
S5_SEQ = 256
S5_NSEQ = 16
S5_TC = 16

TILE_S = SUBLANES
TILE_T = 64
TILE_ROWS = TILE_S * TILE_T
TIME_TILES = S5_SEQ // TILE_T
ALL_SEQ = 2 * S5_NSEQ
CTX_TILES = S5_NSEQ // TILE_S
X_RING = 3
CAST_ROWS = 128
MOD_TILE_N = 1536
VMEM_LIMIT = 56 * 1024 * 1024


def _sigmoid(x):
    return 1.0 / (1.0 + jnp.exp(-x))


def _silu(x):
    return x * _sigmoid(x)


def _gelu_tanh(x):
    return 0.5 * x * (1.0 + jnp.tanh(math.sqrt(2.0 / math.pi) * (x + 0.044715 * (x * x * x))))


def _rms(x):
    return x * lax.rsqrt(jnp.mean(x * x, axis=-1, keepdims=True) + EPS)


def _dot(a, b):
    return jnp.dot(a, b, preferred_element_type=F32)


def _dot_nt(a, b):
    return lax.dot_general(a, b, (((1,), (1,)), ((), ())), preferred_element_type=F32)


def _dot_tn(a, b):
    return lax.dot_general(a, b, (((0,), (0,)), ((), ())), preferred_element_type=F32)


def _whole_spec(shape):
    return pl.BlockSpec(tuple(shape), lambda *_: (0,) * len(shape), pipeline_mode=pl.Buffered(1))


def _layer_spec(shape, layer):
    nd = len(shape)
    return pl.BlockSpec((None,) + tuple(shape), lambda *_: (layer,) + (0,) * nd, pipeline_mode=pl.Buffered(1))


def _mod_kernel(cond_ref, w_ref, b_ref, o_ref):
    a = _silu(cond_ref[...]).astype(BF16)
    o_ref[0] = _dot(a, w_ref[0].astype(BF16)) + b_ref[0]


def _mod_call(cond, w_mod, b_mod):
    n_cond = cond.shape[0]
    n_out = w_mod.shape[-1]
    return pl.pallas_call(
        _mod_kernel,
        grid=(DEPTH, n_out // MOD_TILE_N),
        in_specs=[
            pl.BlockSpec((n_cond, D_MODEL), lambda l, j: (0, 0)),
            pl.BlockSpec((1, D_MODEL, MOD_TILE_N), lambda l, j: (l, 0, j)),
            pl.BlockSpec((1, 1, MOD_TILE_N), lambda l, j: (l, 0, j)),
        ],
        out_specs=pl.BlockSpec((1, n_cond, MOD_TILE_N), lambda l, j: (l, 0, j)),
        out_shape=jax.ShapeDtypeStruct((DEPTH, n_cond, n_out), F32),
        compiler_params=pltpu.CompilerParams(dimension_semantics=("parallel", "parallel"),
                                             vmem_limit_bytes=VMEM_LIMIT),
        name="adaln_mod",
    )(cond, w_mod, b_mod.reshape(DEPTH, 1, n_out))


def _first_step():
    return jnp.logical_and(pl.program_id(0) == 0, pl.program_id(1) == 0)


def _cast_rows(src_ref, dst_ref):
    for r in range(0, src_ref.shape[0], CAST_ROWS):
        dst_ref[r:r + CAST_ROWS, :] = src_ref[r:r + CAST_ROWS, :].astype(BF16)


def _grid_pos_tile(omega, tb):
    nf = omega.shape[-1]
    s_idx = lax.broadcasted_iota(jnp.int32, (TILE_S, nf), 0)
    j_idx = lax.broadcasted_iota(jnp.int32, (TILE_T, nf), 0)
    t0 = tb * TILE_T
    row = (s_idx * (S5_SEQ // GRID_W) + t0 // GRID_W).astype(F32) * omega
    col = (j_idx + t0 % GRID_W).astype(F32) * omega
    enc_r = jnp.concatenate([jnp.sin(row), jnp.cos(row)], axis=-1)
    enc_c = jnp.concatenate([jnp.sin(col), jnp.cos(col)], axis=-1)
    shape = (TILE_S, TILE_T, 2 * nf)
    return jnp.concatenate([jnp.broadcast_to(enc_r[:, None, :], shape),
                            jnp.broadcast_to(enc_c[None, :, :], shape)], axis=-1)


def _is_sample_tile():
    return pl.program_id(0) >= CTX_TILES


def _x_tile_copy(srcs, buf_ref, sem_ref, t, start):
    slot = t % X_RING
    sb, tb = t // TIME_TILES, t % TIME_TILES

    def run(src, sb_local):
        cp = pltpu.make_async_copy(
            src.at[pl.ds(sb_local * TILE_S, TILE_S), pl.ds(tb * TILE_T, TILE_T), :],
            buf_ref.at[slot], sem_ref.at[slot])
        if start:
            cp.start(priority=1)
        else:
            cp.wait()

    if len(srcs) == 1:
        run(srcs[0], sb)
    else:
        pl.when(sb < CTX_TILES)(lambda: run(srcs[0], sb))
        pl.when(sb >= CTX_TILES)(lambda: run(srcs[1], sb - CTX_TILES))


def _in_kernel(*refs, first_layer):
    n_in = 10 if first_layer else 8
    for src, dst in zip(refs[n_in - 3:n_in], refs[-6:-3]):
        dst[...] = src[...].astype(BF16)
    refs = refs[:n_in - 3] + refs[n_in:-6] + refs[-3:]

    n_steps = pl.num_programs(0) * TIME_TILES
    step = pl.program_id(0) * TIME_TILES + pl.program_id(1)
    srcs = refs[:2] if first_layer else refs[:1]
    xbuf_ref, xsem_ref = refs[-2:]

    @pl.when(step == 0)
    def _():
        for t in range(X_RING - 1):
            _x_tile_copy(srcs, xbuf_ref, xsem_ref, jnp.int32(t), True)

    @pl.when(step + (X_RING - 1) < n_steps)
    def _():
        _x_tile_copy(srcs, xbuf_ref, xsem_ref, step + (X_RING - 1), True)

    _x_tile_copy(srcs, xbuf_ref, xsem_ref, step, False)
    x = xbuf_ref[step % X_RING]

    if first_layer:
        _, _, om_ref, gain_ref, sh_ref, sc_ref, w_ref, proj_ref, u_ref, v_ref, xo_ref, wb_ref = refs[:-2]
        x = jnp.where(_is_sample_tile(), x + _grid_pos_tile(om_ref[...], pl.program_id(1)), x)
        xo_ref[...] = x
    else:
        _, gain_ref, sh_ref, sc_ref, w_ref, proj_ref, u_ref, v_ref, wb_ref = refs[:-2]

    @pl.when(_first_step())
    def _():
        _cast_rows(w_ref, wb_ref)

    x = x.reshape(TILE_ROWS, D_MODEL)
    h = _rms(x) * gain_ref[...]
    h = (h * (1.0 + sc_ref[...]) + sh_ref[...]).astype(BF16)
    u = _dot(h, wb_ref[:, HG_IN_W:])
    for s in range(TILE_S):
        u_ref[:, s, :] = u[s * TILE_T:(s + 1) * TILE_T, :]
    p = _dot(h, wb_ref[:, :HG_IN_W])
    tile = (TILE_S, TILE_T)
    proj_ref[:, :, :3 * HG_W] = p[:, :3 * HG_W].reshape(tile + (3 * HG_W,))
    proj_ref[:, :, 3 * HG_W:] = p[:, 4 * HG_W:].reshape(tile + (HG_W,))
    v_ref[...] = p[:, 3 * HG_W:4 * HG_W].astype(BF16).reshape(v_ref.shape)


def _tile_spec(width):
    return pl.BlockSpec((TILE_S, TILE_T, width), lambda sb, tb: (sb, tb, 0))


def _tm_tile_spec(width):
    return pl.BlockSpec((TILE_T, TILE_S, width), lambda sb, tb: (tb, sb, 0))


def _ctx_index(sb, tb):
    on = sb < CTX_TILES
    return jnp.where(on, sb, CTX_TILES - 1), jnp.where(on, tb, TIME_TILES - 1)


def _smp_index(sb, tb):
    on = sb >= CTX_TILES
    return jnp.where(on, sb - CTX_TILES, 0), jnp.where(on, tb, 0)


def _part_tile_spec(width, index):
    return pl.BlockSpec((TILE_S, TILE_T, width), lambda sb, tb: index(sb, tb) + (0,))


def _part_tm_tile_spec(width, index):
    return pl.BlockSpec((TILE_T, TILE_S, width), lambda sb, tb: index(sb, tb)[::-1] + (0,))


def _mod_spec(layer, col):
    return pl.BlockSpec((None, None, 1, D_MODEL),
                        lambda sb, tb: (layer, jnp.maximum(sb - (CTX_TILES - 1), 0), 0, col))


def _in_call(xs, gain, mod4, w_in, ffn_w, layer):
    first_layer = len(xs) == 2
    n_steps = (ALL_SEQ // TILE_S) * TIME_TILES
    if first_layer:
        assert GRID_W % TILE_T == 0 and S5_SEQ % GRID_W == 0
        nf = D_MODEL // 4
        omega = 1.0 / (np.float32(10000.0) ** (np.arange(nf, dtype=np.float32) / np.float32(nf)))
        in_specs = [pl.BlockSpec(memory_space=pl.ANY), pl.BlockSpec(memory_space=pl.ANY),
                    pl.BlockSpec((1, nf), lambda sb, tb: (0, 0))]
        args = list(xs) + [jnp.asarray(omega.reshape(1, nf), F32)]
    else:
        in_specs = [pl.BlockSpec(memory_space=pl.ANY)]
        args = list(xs)
    in_specs += [
        pl.BlockSpec((1, D_MODEL), lambda sb, tb: (0, 0)),
        _mod_spec(layer, 0),
        _mod_spec(layer, 1),
        _layer_spec((D_MODEL, IN_W), layer),
    ]
    args += [gain, mod4, mod4, w_in]
    out_specs = [_tile_spec(HG_PROJ_W), _tm_tile_spec(S5_W), _tile_spec(HG_W)]
    out_shape = [jax.ShapeDtypeStruct((ALL_SEQ, S5_SEQ, HG_PROJ_W), F32),
                 jax.ShapeDtypeStruct((S5_SEQ, ALL_SEQ, S5_W), F32),
                 jax.ShapeDtypeStruct((ALL_SEQ, S5_SEQ, HG_W), BF16)]
    if first_layer:
        out_specs.append(_tile_spec(D_MODEL))
        out_shape.append(jax.ShapeDtypeStruct((ALL_SEQ, S5_SEQ, D_MODEL), F32))
    for w in ffn_w:
        _, n_rows, n_cols = w.shape
        slab = n_rows // n_steps
        assert slab * n_steps == n_rows and slab % (2 * SUBLANES) == 0
        in_specs.append(pl.BlockSpec((None, slab, n_cols), lambda sb, tb: (layer, sb * TIME_TILES + tb, 0)))
        args.append(w)
        out_specs.append(pl.BlockSpec((slab, n_cols), lambda sb, tb: (sb * TIME_TILES + tb, 0)))
        out_shape.append(jax.ShapeDtypeStruct((n_rows, n_cols), BF16))
    res = pl.pallas_call(
        functools.partial(_in_kernel, first_layer=first_layer),
        grid=(ALL_SEQ // TILE_S, TIME_TILES),
        in_specs=in_specs,
        out_specs=out_specs,
        out_shape=out_shape,
        scratch_shapes=[pltpu.VMEM((D_MODEL, IN_W), BF16),
                        pltpu.VMEM((X_RING, TILE_S, TILE_T, D_MODEL), F32),
                        pltpu.SemaphoreType.DMA((X_RING,))],
        compiler_params=pltpu.CompilerParams(dimension_semantics=("arbitrary", "arbitrary"),
                                             vmem_limit_bytes=VMEM_LIMIT),
        name="in_proj",
    )(*args)
    n_w = len(ffn_w)
    return res[0], res[1], res[2], (res[3] if first_layer else xs[0]), res[len(res) - n_w:]


def _pair_boundary(b, m, rev):
    c = b.shape[0]
    span = 2 * m
    at = m if rev else m - 1
    if span >= SUBLANES:
        b3 = b.reshape(c // span, span, LANES)
        return jnp.broadcast_to(b3[:, at:at + 1, :], b3.shape).reshape(c, LANES)
    b3 = b.reshape(c // SUBLANES, SUBLANES, LANES)
    sub = lax.broadcasted_iota(jnp.int32, b3.shape, 1)
    out = None
    for p in range(SUBLANES // span):
        piece = jnp.broadcast_to(b3[:, p * span + at:p * span + at + 1, :], b3.shape)
        out = piece if out is None else jnp.where(sub >= p * span, piece, out)
    return out.reshape(c, LANES)


def _neg_abs(x):
    bits = lax.bitcast_convert_type(x, jnp.uint32) | jnp.uint32(0x80000000)
    return lax.bitcast_convert_type(bits, F32)


def _hg_gates(chains, scale):
    outs = []
    for q, fl, lb, tri in chains:
        sig = _sigmoid(fl)
        forget = lb + (1.0 - lb) * sig
        logf = jnp.log2(forget)
        key = (1.0 - lb) * (1.0 - sig)
        hi = logf.astype(BF16)
        r1 = logf - hi.astype(F32)
        mid = r1.astype(BF16)
        lo = (r1 - mid.astype(F32)).astype(BF16)
        parts = _dot(tri, jnp.concatenate([hi, mid, lo], axis=1))
        b2 = parts[:, :LANES] + parts[:, LANES:2 * LANES] + parts[:, 2 * LANES:]
        outs.append((_silu(q) * scale, key, b2, forget))
    return outs


def _hg_scores(chains, code, eye, from_zero=False):
    c = chains[0][0].shape[0]
    o_inter = []
    for qh, key, b2, forget, v, st_ref, rev in chains:
        b_edge = b2[0:1, :] if rev else b2[c - 1:c, :]
        k_end = key * jnp.exp2(b_edge - b2)
        grown = _dot_tn(v.astype(BF16), k_end.astype(BF16))
        if from_zero:
            o_inter.append(None)
            st_ref[...] = grown
        else:
            st = st_ref[...]
            o_inter.append(_dot_nt((qh * jnp.exp2(b2)).astype(BF16), st.astype(BF16)))
            st_ref[...] = jnp.exp2(b_edge) * st + grown

    out = []
    for o, (qh, key, b2, forget, v, st_ref, rev) in zip(o_inter, chains):
        scores = jnp.where(eye, jnp.sum(qh * key, axis=-1, keepdims=True), 0.0)
        qb, kb = qh.astype(BF16), key.astype(BF16)
        for m in HG_LEVELS:
            k = int(math.log2(m)) + 1
            if m == 1:
                p = _dot_nt((qh * forget).astype(BF16), kb)
            elif m >= SUBLANES:
                g = c // (2 * m)
                t_half = slice(0, m) if rev else slice(m, 2 * m)
                s_half = slice(m, 2 * m) if rev else slice(0, m)
                at = m if rev else m - 1
                b3, q3, k3 = (x.reshape(g, 2 * m, LANES) for x in (b2, qh, key))
                seam = b3[:, at:at + 1, :]
                qt = (q3[:, t_half] * jnp.exp2(b3[:, t_half] - seam)).reshape(c // 2, LANES)
                ks = k3[:, s_half] * jnp.exp2(seam - b3[:, s_half])
                kk = jnp.concatenate([k3[:, t_half], ks] if rev else [ks, k3[:, t_half]], axis=1).reshape(c, LANES)
                p = _dot_nt(qt.astype(BF16), kk.astype(BF16)).reshape(g, m, c)
                s3 = scores.reshape(g, 2 * m, c)
                hit = code.reshape(g, 2 * m, c)[:, t_half] == (-k if rev else k)
                st = jnp.where(hit, p, s3[:, t_half])
                scores = jnp.concatenate([st, s3[:, s_half]] if rev else [s3[:, s_half], st], axis=1).reshape(c, c)
                continue
            else:
                e = jnp.exp2(_neg_abs(b2 - _pair_boundary(b2, m, rev))).astype(BF16)
                p = _dot_nt(qb * e, kb * e)
            scores = jnp.where(code == (-k if rev else k), p, scores)
        out.append((o, scores.astype(BF16)))
    return out


def _hgrn_kernel(*refs, layer, n_chunks, n_heads, zero_init):
    refs = list(refs)
    q_ref, ff_ref, fb_ref, v_ref, g_ref, lbl_ref, gain_ref, code_ref, tri_ref = refs[:9]
    s0_ref = None if zero_init else refs[9]
    o_ref, sfin_ref, st_ref, of_ref, ob_ref, ab_ref, oi_ref, sc_ref = refs[-8:]
    c = HG_CHUNK
    code = code_ref[...]
    eye = code == 0
    chains = [(hd, d) for hd in range(n_heads) for d in (0, 1)]

    def lanes(hd):
        return slice(hd * HG_D, (hd + 1) * HG_D)

    def lower_bound(hd, d):
        lg = lbl_ref[d, :, lanes(hd)]
        ex = jnp.exp(lg - jnp.max(lg, axis=0, keepdims=True))
        soft = ex / jnp.sum(ex, axis=0, keepdims=True)
        return jnp.sum(soft[:layer + 1], axis=0, keepdims=True) - soft[0:1]

    lb = [lower_bound(hd, d) for hd, d in chains]
    scale = HG_D ** -0.5

    def rows_of(d, i):
        n = (n_chunks - 1 - i) if d else i
        return pl.ds(n * c if isinstance(n, int) else pl.multiple_of(n * c, c), c)

    for ch, (hd, d) in enumerate(chains):
        st_ref[ch] = jnp.zeros((HG_D, HG_D), F32) if zero_init else s0_ref[d, hd].T

    def gates(i):
        return _hg_gates([(q_ref[rows_of(d, i), lanes(hd)], (fb_ref if d else ff_ref)[rows_of(d, i), lanes(hd)],
                           lb[ch], tri_ref[d]) for ch, (hd, d) in enumerate(chains)], scale)

    def scores(i, ab, from_zero=False):
        return _hg_scores([ab[ch] + (v_ref[rows_of(d, i), lanes(hd)], st_ref.at[ch], bool(d))
                           for ch, (hd, d) in enumerate(chains)], code, eye, from_zero)

    def emit(i, oi_sc):
        for ch, (hd, d) in enumerate(chains):
            rows = rows_of(d, i)
            o_intra = _dot(oi_sc[ch][1], v_ref[rows, lanes(hd)].astype(BF16))
            (ob_ref if d else of_ref)[rows, lanes(hd)] = o_intra if oi_sc[ch][0] is None else oi_sc[ch][0] + o_intra

    if n_chunks <= 2:
        ab = [gates(i) for i in range(n_chunks)]
        for i in range(n_chunks):
            emit(i, scores(i, ab[i], from_zero=zero_init and i == 0))
    else:
        n_ab = ab_ref.shape[1]

        def put_ab(ab):
            for ch in range(len(chains)):
                for k in range(n_ab):
                    ab_ref[ch, k] = ab[ch][k]

        def get_oi_sc():
            return [(oi_ref[ch], sc_ref[ch]) for ch in range(len(chains))]

        put_ab(gates(0))
        oi_ref[...] = jnp.zeros(oi_ref.shape, F32)
        sc_ref[...] = jnp.zeros(sc_ref.shape, BF16)

        def body(i, carry):
            emit(jnp.maximum(i - 1, 0), get_oi_sc())
            res = scores(i, [tuple(ab_ref[ch, k] for k in range(n_ab)) for ch in range(len(chains))])
            for ch in range(len(chains)):
                oi_ref[ch] = res[ch][0]
                sc_ref[ch] = res[ch][1]
            put_ab(gates(jnp.minimum(i + 1, n_chunks - 1)))
            return carry

        lax.fori_loop(0, n_chunks, body, 0)
        emit(n_chunks - 1, get_oi_sc())
    if len(sfin_ref.shape) == 5:
        for other in range(sfin_ref.shape[0]):
            if other != layer:
                sfin_ref[other] = jnp.zeros(sfin_ref.shape[1:], F32)
        sfin_ref = sfin_ref.at[layer]
    for ch, (hd, d) in enumerate(chains):
        sfin_ref[d, hd] = st_ref[ch].T

    def finish(n, carry):
        rows = pl.ds(pl.multiple_of(n * (2 * c), 2 * c), 2 * c)
        for hd in range(n_heads):
            o_ref[rows, lanes(hd)] = (_rms(of_ref[rows, lanes(hd)] + ob_ref[rows, lanes(hd)]) * gain_ref[...]
                                      * _silu(g_ref[rows, lanes(hd)])).astype(BF16)
        return carry

    lax.fori_loop(0, n_chunks // 2, finish, 0)


def _hgrn_call(proj, vals, row0, lb_logits, gain, state, finals, layer, n_seq, seq_len):
    rows = n_seq * seq_len
    seq0 = row0 // seq_len
    zero_init = state is None
    n_chunks = seq_len // HG_CHUNK
    assert n_chunks % 2 == 0
    nh = HG_HEADS if n_chunks <= 2 else 2
    hw = nh * HG_D
    n_hb = HG_HEADS // nh

    def col_spec(k):
        return pl.BlockSpec((seq_len, hw), lambda b, h: (seq0 + b, k * n_hb + h))

    t, s = np.meshgrid(np.arange(HG_CHUNK), np.arange(HG_CHUNK), indexing="ij")
    lvl = np.where(t == s, 0, np.floor(np.log2(np.maximum(t ^ s, 1))).astype(np.int32) + 1)
    code = jnp.asarray(np.where(t > s, lvl, -lvl), jnp.int32)
    tri = jnp.asarray(np.stack([s <= t, s >= t]), BF16)

    in_specs = [col_spec(0), col_spec(1), col_spec(2), col_spec(0), col_spec(3),
                pl.BlockSpec((2, DEPTH, hw), lambda b, h: (0, 0, h)),
                pl.BlockSpec((1, HG_D), lambda b, h: (0, 0)),
                pl.BlockSpec((HG_CHUNK, HG_CHUNK), lambda b, h: (0, 0)),
                pl.BlockSpec((2, HG_CHUNK, HG_CHUNK), lambda b, h: (0, 0, 0))]
    args = [proj] * 3 + [vals, proj, lb_logits, gain, code, tri]
    if not zero_init:
        in_specs.append(pl.BlockSpec((None, None, 2, nh, HG_D, HG_D), lambda b, h: (b, layer, 0, h, 0, 0)))
        args.append(state)
    aliases = {}
    if finals is None:
        fin_spec = pl.BlockSpec((None, DEPTH, 2, nh, HG_D, HG_D), lambda b, h: (b, 0, 0, h, 0, 0))
    else:
        fin_spec = pl.BlockSpec((None, None, 2, nh, HG_D, HG_D), lambda b, h: (b, layer, 0, h, 0, 0))
        aliases[len(args)] = 1
        in_specs.append(pl.BlockSpec(memory_space=pl.ANY))
        args.append(finals)
    n_ch = 2 * nh
    return pl.pallas_call(
        functools.partial(_hgrn_kernel, layer=layer, n_chunks=n_chunks, n_heads=nh, zero_init=zero_init),
        grid=(n_seq, n_hb),
        in_specs=in_specs,
        out_specs=[pl.BlockSpec((seq_len, hw), lambda b, h: (b, h)), fin_spec],
        out_shape=[jax.ShapeDtypeStruct((rows, HG_W), BF16),
                   jax.ShapeDtypeStruct((n_seq, DEPTH, 2, HG_HEADS, HG_D, HG_D), F32)],
        input_output_aliases=aliases,
        scratch_shapes=[pltpu.VMEM((n_ch, HG_D, HG_D), F32), pltpu.VMEM((seq_len, hw), F32),
                        pltpu.VMEM((seq_len, hw), F32),
                        pltpu.VMEM((n_ch, 4, HG_CHUNK, HG_D), F32), pltpu.VMEM((n_ch, HG_CHUNK, HG_D), F32),
                        pltpu.VMEM((n_ch, HG_CHUNK, HG_CHUNK), BF16)],
        compiler_params=pltpu.CompilerParams(dimension_semantics=("parallel", "parallel"),
                                             vmem_limit_bytes=VMEM_LIMIT),
        name="hgrn2_mixer",
    )(*args)


def _s5_params_kernel(lr_ref, li_ref, ldt_ref, btr_ref, bti_ref, cr_ref, ci_ref, a_ref, bm_ref, cm_ref):
    for k in range(lr_ref.shape[0]):
        _s5_params_block(*(r.at[k] for r in (lr_ref, li_ref, ldt_ref, btr_ref, bti_ref, cr_ref, ci_ref,
                                             a_ref, bm_ref, cm_ref)))


def _s5_params_block(lr_ref, li_ref, ldt_ref, btr_ref, bti_ref, cr_ref, ci_ref, a_ref, bm_ref, cm_ref):
    sw = S5_SW
    lr = jnp.minimum(lr_ref[...], -1e-4)
    li = li_ref[...]
    dt = jnp.exp(ldt_ref[...])
    mag = jnp.exp(lr * dt)
    ab_re = mag * jnp.cos(li * dt)
    ab_im = mag * jnp.sin(li * dt)
    nr = ab_re - 1.0
    den = lr * lr + li * li
    z_re = (nr * lr + ab_im * li) / den
    z_im = (ab_im * lr - nr * li) / den

    p_idx = lax.broadcasted_iota(jnp.int32, (S5_P, sw), 0)
    col = lax.broadcasted_iota(jnp.int32, (S5_P, sw), 1)
    for g in range(S5_GB):
        place = (col == p_idx + g * S5_P).astype(BF16)
        zr, zi = z_re[g:g + 1, :], z_im[g:g + 1, :]
        btr, bti = btr_ref[g], bti_ref[g]
        rows = slice(g * S5_CH, (g + 1) * S5_CH)
        bm_ref[rows, :sw] = _dot((zr * btr - zi * bti).astype(BF16), place).astype(BF16)
        bm_ref[rows, sw:] = _dot((zr * bti + zi * btr).astype(BF16), place).astype(BF16)
        cm_ref[rows, :sw] = _dot(cr_ref[g].astype(BF16), place).astype(BF16)
        cm_ref[rows, sw:] = _dot((-ci_ref[g]).astype(BF16), place).astype(BF16)
        a_ref[:, g * S5_P:(g + 1) * S5_P] = jnp.broadcast_to(ab_re[g:g + 1, :], (S5_NSEQ, S5_P))
        a_ref[:, sw + g * S5_P:sw + (g + 1) * S5_P] = jnp.broadcast_to(ab_im[g:g + 1, :], (S5_NSEQ, S5_P))


def _s5_params(lam_re, lam_im, log_dt, b_re, b_im, c_re, c_im):
    nb = DEPTH * 2 * S5_NGB
    gp = (nb, S5_GB, S5_P)
    gcp = (nb, S5_GB, S5_CH, S5_P)
    bt_re = jnp.swapaxes(b_re, -1, -2).reshape(gcp)
    bt_im = jnp.swapaxes(b_im, -1, -2).reshape(gcp)
    ldt = jnp.broadcast_to(log_dt.reshape(nb, S5_GB, 1), gp)
    per = S5_NGB
    gp_spec = pl.BlockSpec((per, S5_GB, S5_P), lambda i: (i, 0, 0))
    gcp_spec = pl.BlockSpec((per, S5_GB, S5_CH, S5_P), lambda i: (i, 0, 0, 0))
    a, bmat, cmat = pl.pallas_call(
        _s5_params_kernel,
        grid=(nb // per,),
        in_specs=[gp_spec] * 3 + [gcp_spec] * 4,
        out_specs=[pl.BlockSpec((per, S5_NSEQ, 2 * S5_SW), lambda i: (i, 0, 0)),
                   pl.BlockSpec((per, LANES, 2 * S5_SW), lambda i: (i, 0, 0)),
                   pl.BlockSpec((per, LANES, 2 * S5_SW), lambda i: (i, 0, 0))],
        out_shape=[jax.ShapeDtypeStruct((nb, S5_NSEQ, 2 * S5_SW), F32),
                   jax.ShapeDtypeStruct((nb, LANES, 2 * S5_SW), BF16),
                   jax.ShapeDtypeStruct((nb, LANES, 2 * S5_SW), BF16)],
        compiler_params=pltpu.CompilerParams(dimension_semantics=("parallel",)),
        name="s5_params",
    )(lam_re.reshape(gp), lam_im.reshape(gp), ldt, bt_re, bt_im, c_re.reshape(gcp), c_im.reshape(gcp))
    lead = (DEPTH, 2, S5_NGB)
    return (a.reshape(lead + a.shape[1:]), bmat.reshape(lead + bmat.shape[1:]), cmat.reshape(lead + cmat.shape[1:]))


def _s5_kernel(*refs, chained):
    refs = list(refs)
    u_ref, bm_ref, cm_ref, a_ref, d_ref = refs[:5]
    rest = refs[5:]
    s0_ref = rest.pop(0) if chained else None
    y_ref = rest.pop(0)
    hfin_ref = None if chained else rest.pop(0)
    hbuf_e, hbuf_o, hb0, hb1, hst = rest
    hbufs, hb16s = (hbuf_e, hbuf_o), (hb0, hb1)
    ns, sw = S5_NSEQ, S5_SW
    half = ns // 2
    n_tc = S5_SEQ // S5_TC
    blk = S5_TC * ns
    dirs = (0, 1)

    for d in dirs:
        hst[d] = jnp.zeros((ns, 2 * sw), F32)

    def steps_of(d, i):
        return pl.ds(((n_tc - 1 - i) if d else i) * S5_TC, S5_TC)

    def kept(i):
        return i // 2 if chained else 0

    def project(i, par):
        for d in dirs:
            u = u_ref[steps_of(d, i)].reshape(blk, LANES)
            hbufs[par][d, kept(i)] = _dot(u.astype(BF16), bm_ref[d])

    def scan(i, par, emit):
        a = [(a_ref[d, :half, :sw], a_ref[d, :half, sw:]) for d in dirs]
        h = [[(hst[d, k * half:(k + 1) * half, :sw], hst[d, k * half:(k + 1) * half, sw:]) for k in range(2)]
             for d in dirs]
        for jj in range(S5_TC):
            for d in dirs:
                j = S5_TC - 1 - jj if d else jj
                ar, ai = a[d]
                bu = hbufs[par].at[d, kept(i)]
                for k in range(2):
                    r = slice(j * ns + k * half, j * ns + (k + 1) * half)
                    hr, hi = h[d][k]
                    h[d][k] = (ar * hr - ai * hi + bu[r, :sw], ar * hi + ai * hr + bu[r, sw:])
                if emit:
                    r = slice(j * ns, (j + 1) * ns)
                    hb16s[par][d, r, :sw] = jnp.concatenate([h[d][0][0], h[d][1][0]], axis=0).astype(BF16)
                    hb16s[par][d, r, sw:] = jnp.concatenate([h[d][0][1], h[d][1][1]], axis=0).astype(BF16)
        for d in dirs:
            for k in range(2):
                hst[d, k * half:(k + 1) * half, :sw] = h[d][k][0]
                hst[d, k * half:(k + 1) * half, sw:] = h[d][k][1]

    def readout(i, par):
        for d in dirs:
            steps = steps_of(d, i)
            y = _dot_nt(hb16s[par][d], cm_ref[d])
            y_ref[steps] = y_ref[steps] + y.reshape(S5_TC, ns, LANES)

    def skip(n, carry):
        steps = pl.ds(n * S5_TC, S5_TC)
        y_ref[steps] = d_ref[...] * u_ref[steps]
        return carry

    project(0, 0)
    if chained:
        def sweep(k, carry):
            i = 2 * k
            scan(i, 0, False)
            project(i + 1, 1)
            scan(i + 1, 1, False)
            project(i + 2, 0)
            return carry

        lax.fori_loop(0, n_tc // 2 - 1, sweep, 0)
        scan(n_tc - 2, 0, False)
        project(n_tc - 1, 1)
        scan(n_tc - 1, 1, False)

        n_long = s0_ref.shape[1]
        pieces = ns // n_long
        for d in dirs:
            pr, pi = a_ref[d, 0:1, :sw], a_ref[d, 0:1, sw:]
            for _ in range(int(math.log2(S5_SEQ))):
                pr, pi = pr * pr - pi * pi, 2.0 * (pr * pi)
            for b in range(n_long):
                hr, hi = s0_ref[d, b:b + 1, :sw], s0_ref[d, b:b + 1, sw:]
                for k in (range(pieces - 1, -1, -1) if d else range(pieces)):
                    r = b * pieces + k
                    zr, zi = hst[d, r:r + 1, :sw], hst[d, r:r + 1, sw:]
                    hst[d, r:r + 1, :sw] = hr
                    hst[d, r:r + 1, sw:] = hi
                    hr, hi = pr * hr - pi * hi + zr, pr * hi + pi * hr + zi

    hb1[...] = jnp.zeros(hb1.shape, BF16)
    lax.fori_loop(0, n_tc, skip, 0)

    def body(k, carry):
        i = 2 * k
        scan(i, 0, True)
        readout(jnp.maximum(i - 1, 0), 1)
        if not chained:
            project(i + 1, 1)
        scan(i + 1, 1, True)
        readout(i, 0)
        if not chained:
            project(jnp.minimum(i + 2, n_tc - 1), 0)
        return carry

    lax.fori_loop(0, n_tc // 2, body, 0)
    if not chained:
        for d in dirs:
            hfin_ref[d] = hst[d]
    readout(n_tc - 1, 1)


def _s5_call(u_tm, part, a, bmat, cmat, dskip, s0, layer):
    chained = s0 is not None
    tm_spec = pl.BlockSpec((S5_SEQ, S5_NSEQ, LANES), lambda g: (0, 0, g))

    def mat_spec(rows):
        return pl.BlockSpec((None, 2, None, rows, 2 * S5_SW), lambda g: (layer, 0, g, 0, 0))

    in_specs = [pl.BlockSpec((S5_SEQ, S5_NSEQ, LANES), lambda g: (0, part, g)),
                mat_spec(LANES), mat_spec(LANES), mat_spec(S5_NSEQ),
                pl.BlockSpec((None, None, 1, LANES), lambda g: (layer, g, 0, 0))]
    args = [u_tm, bmat, cmat, a, dskip]
    out_specs = [tm_spec]
    out_shape = [jax.ShapeDtypeStruct((S5_SEQ, S5_NSEQ, S5_W), F32)]
    if chained:
        in_specs.append(pl.BlockSpec((2, None, s0.shape[2], 2 * S5_SW), lambda g: (0, g, 0, 0)))
        args.append(s0)
    else:
        out_specs.append(pl.BlockSpec((2, None, S5_NSEQ, 2 * S5_SW), lambda g: (0, g, 0, 0)))
        out_shape.append(jax.ShapeDtypeStruct((2, S5_NGB, S5_NSEQ, 2 * S5_SW), F32))
    blk = S5_TC * S5_NSEQ
    kept = S5_SEQ // S5_TC // 2 if chained else 1
    res = pl.pallas_call(
        functools.partial(_s5_kernel, chained=chained),
        grid=(S5_NGB,),
        in_specs=in_specs,
        out_specs=out_specs,
        out_shape=out_shape,
        scratch_shapes=[pltpu.VMEM((2, kept, blk, 2 * S5_SW), F32),
                        pltpu.VMEM((2, kept, blk, 2 * S5_SW), F32),
                        pltpu.VMEM((2, blk, 2 * S5_SW), BF16),
                        pltpu.VMEM((2, blk, 2 * S5_SW), BF16),
                        pltpu.VMEM((2, S5_NSEQ, 2 * S5_SW), F32)],
        compiler_params=pltpu.CompilerParams(dimension_semantics=("parallel",),
                                             vmem_limit_bytes=VMEM_LIMIT),
        name="s5_scan",
    )(*args)
    return (res[0], None) if chained else (res[0], res[1])


def _out_kernel(x_ref, ohc_ref, ohs_ref, y5c_ref, y5s_ref, g1_ref, sh2_ref, sc2_ref, g2_ref, nffn_ref, nfin_ref,
                wglu_ref, wout_ref, wg_ref, wu_ref, wd_ref, *rest, last_layer):
    if last_layer:
        oc_ref, os_ref, wglu_b, wout_b = rest
    else:
        o_ref, wglu_b, wout_b = rest

    @pl.when(_first_step())
    def _():
        _cast_rows(wglu_ref, wglu_b)
        _cast_rows(wout_ref, wout_b)

    smp = _is_sample_tile()
    y = jnp.concatenate([jnp.where(smp, y5s_ref[:, s, :], y5c_ref[:, s, :]) for s in range(TILE_S)],
                        axis=0)
    y = _gelu_tanh(y)
    y = y * _sigmoid(_dot(y.astype(BF16), wglu_b[...]))
    ohg = jnp.where(smp, ohs_ref[...], ohc_ref[...]).reshape(TILE_ROWS, HG_W)
    mix = _dot(ohg.astype(BF16), wout_b[:HG_W, :]) + _dot(y.astype(BF16), wout_b[HG_W:, :])
    x = x_ref[...].reshape(TILE_ROWS, D_MODEL) + g1_ref[...] * mix
    h = _rms(x) * nffn_ref[...]
    h = (h * (1.0 + sc2_ref[...]) + sh2_ref[...]).astype(BF16)
    act = (_silu(_dot(h, wg_ref[...])) * _dot(h, wu_ref[...])).astype(BF16)
    x = x + g2_ref[...] * _dot(act, wd_ref[...])
    if not last_layer:
        o_ref[...] = x.reshape(o_ref.shape)
    else:
        x = (_rms(x) * nfin_ref[...]).reshape(oc_ref.shape)

        @pl.when(smp)
        def _():
            os_ref[...] = x

        @pl.when(jnp.logical_not(smp))
        def _():
            oc_ref[...] = x


def _out_call(x3, ohg_c, ohg_s, y5_c, y5_s, mod4, nffn, nfin, wglu, wout, wg, wu, wd, layer, last_layer):
    vec = pl.BlockSpec((1, D_MODEL), lambda sb, tb: (0, 0))
    part_shape = jax.ShapeDtypeStruct((S5_NSEQ, S5_SEQ, D_MODEL), F32)
    if last_layer:
        out_specs = [_part_tile_spec(D_MODEL, _ctx_index), _part_tile_spec(D_MODEL, _smp_index)]
        out_shape = [part_shape, part_shape]
    else:
        out_specs = _tile_spec(D_MODEL)
        out_shape = jax.ShapeDtypeStruct(x3.shape, F32)
    return pl.pallas_call(
        functools.partial(_out_kernel, last_layer=last_layer),
        grid=(ALL_SEQ // TILE_S, TIME_TILES),
        in_specs=[_tile_spec(D_MODEL),
                  _part_tile_spec(HG_W, _ctx_index), _part_tile_spec(HG_W, _smp_index),
                  _part_tm_tile_spec(S5_W, _ctx_index), _part_tm_tile_spec(S5_W, _smp_index),
                  _mod_spec(layer, 2), _mod_spec(layer, 3), _mod_spec(layer, 4), _mod_spec(layer, 5),
                  vec, vec,
                  _layer_spec((S5_W, S5_W), layer), _layer_spec((D_MODEL, D_MODEL), layer),
                  _whole_spec((D_MODEL, D_FF)), _whole_spec((D_MODEL, D_FF)), _whole_spec((D_FF, D_MODEL))],
        out_specs=out_specs,
        out_shape=out_shape,
        scratch_shapes=[pltpu.VMEM((S5_W, S5_W), BF16), pltpu.VMEM((D_MODEL, D_MODEL), BF16)],
        compiler_params=pltpu.CompilerParams(dimension_semantics=("arbitrary", "arbitrary"),
                                             vmem_limit_bytes=VMEM_LIMIT),
        name="out_ffn",
    )(x3, ohg_c, ohg_s, y5_c, y5_s, mod4, mod4, mod4, mod4, nffn, nfin, wglu, wout, wg, wu, wd)


def _s5_state_to_blocks(s):
    n = s.shape[0]
    s = s.reshape(n, 2, S5_NGB, S5_GB, S5_P, 2)
    return jnp.transpose(s, (1, 2, 0, 5, 3, 4)).reshape(2, S5_NGB, n, 2 * S5_SW)


def _s5_blocks_to_state(h):
    n = h.shape[2]
    h = h.reshape(2, S5_NGB, n, 2, S5_GB, S5_P)
    return jnp.transpose(h, (2, 0, 1, 4, 5, 3)).reshape(n, 2, S5_GROUPS, S5_P, 2)


def kernel(x_prompt, x_sample, state_hgrn, state_s5, c, c_ctx, w_mod, b_mod, norm_mix, norm_ffn, norm_final, w_in, w_out, hg_lb_logits, hg_norm, s5_lam_re, s5_lam_im, s5_log_dt, s5_b_re, s5_b_im, s5_c_re, s5_c_im, s5_d, s5_w_glu, w_gate, w_up, w_down):
    n_ctx, ctx_len, _ = x_prompt.shape
    n_dec, dec_len, _ = x_sample.shape
    assert ctx_len == S5_SEQ and n_ctx == S5_NSEQ and n_dec * dec_len == S5_NSEQ * S5_SEQ

    cond = jnp.concatenate([c_ctx[None, :], c, jnp.zeros((SUBLANES - 1 - n_dec, D_MODEL), F32)], axis=0)
    mod4 = _mod_call(cond, w_mod, b_mod).reshape(DEPTH, SUBLANES, 1, 6 * D_MODEL)

    s5_a, s5_bmat, s5_cmat = _s5_params(s5_lam_re, s5_lam_im, s5_log_dt, s5_b_re, s5_b_im, s5_c_re, s5_c_im)
    s5_dskip = s5_d.reshape(DEPTH, S5_NGB, 1, LANES)
    nfin = norm_final.reshape(1, D_MODEL)

    assert dec_len // S5_SEQ == TILE_S and n_dec + 1 <= SUBLANES
    tok = (S5_NSEQ, S5_SEQ, D_MODEL)
    ctx_rows = S5_NSEQ * S5_SEQ
    xs = (x_prompt.reshape(tok), x_sample.reshape(tok))
    ctx_fin, smp_fin, s5_finals = None, None, []
    for l in range(DEPTH):
        proj3, u_tm, vals3, x_all, (w_gate_b, w_up_b, w_down_b) = _in_call(
            xs, norm_mix[l].reshape(1, D_MODEL), mod4, w_in, (w_gate, w_up, w_down), l)
        proj = proj3.reshape(ALL_SEQ * S5_SEQ, HG_PROJ_W)
        vals = vals3.reshape(ALL_SEQ * S5_SEQ, HG_W)
        gain = hg_norm[l].reshape(1, HG_D)
        ohg_c, ctx_fin = _hgrn_call(proj, vals, 0, hg_lb_logits, gain, None, ctx_fin, l, n_ctx, ctx_len)
        ohg_s, smp_fin = _hgrn_call(proj, vals, ctx_rows, hg_lb_logits, gain, state_hgrn, smp_fin, l, n_dec,
                                    dec_len)
        y5_c, s5_fin = _s5_call(u_tm, 0, s5_a, s5_bmat, s5_cmat, s5_dskip, None, l)
        y5_s, _ = _s5_call(u_tm, 1, s5_a, s5_bmat, s5_cmat, s5_dskip, _s5_state_to_blocks(state_s5[:, l]), l)
        last = l == DEPTH - 1
        res = _out_call(x_all, ohg_c.reshape(S5_NSEQ, S5_SEQ, HG_W), ohg_s.reshape(S5_NSEQ, S5_SEQ, HG_W),
                        y5_c, y5_s, mod4, norm_ffn[l].reshape(1, D_MODEL), nfin,
                        s5_w_glu, w_out, w_gate_b, w_up_b, w_down_b, l, last)
        xs = res if last else (res,)
        s5_finals.append(_s5_blocks_to_state(s5_fin))
    y_prompt, y_sample = xs
    return (y_prompt.reshape(x_prompt.shape), y_sample.reshape(x_sample.shape),
            ctx_fin, jnp.stack(s5_finals, axis=1))
```
